```python
import math
import jax, jax.numpy as jnp
from jax import lax
import numpy as np


D_MODEL = 1024
BATCH = 8
SEQ = 2048
DEPTH = 1

PLE_DIM = 256
SSD_WIDTH = D_MODEL // 2
ATT_WIDTH = D_MODEL - SSD_WIDTH
MIX_WIDTH = SSD_WIDTH + ATT_WIDTH
SSD_HEAD_DIM = 64
SSD_HEADS = SSD_WIDTH // SSD_HEAD_DIM
SSD_GROUPS = 2
SSD_STATE = 128
SSD_CONV = 4
SSD_CHUNK = 128
SSD_CONV_CH = SSD_WIDTH + 2 * SSD_GROUPS * SSD_STATE
SSD_NORM_EPS = 1e-5
ATT_HEAD_DIM = 64
ATT_HEADS = ATT_WIDTH // (2 * ATT_HEAD_DIM)
Q_BLOCK = 128
SUBLN_EPS = 1e-5
OFF_Z = 0
OFF_XBC = OFF_Z + SSD_WIDTH
OFF_DT = OFF_XBC + SSD_CONV_CH
OFF_Q = OFF_DT + SSD_HEADS
OFF_K = OFF_Q + ATT_WIDTH
OFF_V = OFF_K + ATT_WIDTH
IN_PROJ = OFF_V + ATT_WIDTH
N_EXPERTS = 32
TOP_K = 4
D_EXPERT = D_MODEL
SWIGLU_LIMIT = 7.0
SWIGLU_ALPHA = 1.702
EXPERT_BLOCK = 128
NORM_EPS = 1e-6

kernel_name = 'hybrid_ssd_diffattn_moe_ple'


def rms_norm(x, w, eps):
    xf = x.astype(jnp.float32)
    y = xf * lax.rsqrt(jnp.mean(xf * xf, axis=-1, keepdims=True) + eps)
    return (y * w.astype(jnp.float32)).astype(x.dtype)


def causal_depthwise_conv(u, w, b):
    k = w.shape[0]
    c = u.shape[-1]
    out = lax.conv_general_dilated(u, w[:, None, :].astype(u.dtype), window_strides=(1,),
                                   padding=((k - 1, 0),), dimension_numbers=('NWC', 'WIO', 'NWC'),
                                   feature_group_count=c)
    return out + b


def ssd_chunked(xdt, a_dt, b_mat, c_mat):
    bsz, seq, nh, hp = xdt.shape
    ng, ns = b_mat.shape[2], b_mat.shape[3]
    r = nh // ng
    nc = seq // SSD_CHUNK
    X = xdt.reshape(bsz, nc, SSD_CHUNK, ng, r, hp)
    A = a_dt.reshape(bsz, nc, SSD_CHUNK, ng, r).transpose(0, 3, 4, 1, 2)
    Bm = b_mat.reshape(bsz, nc, SSD_CHUNK, ng, ns)
    Cm = c_mat.reshape(bsz, nc, SSD_CHUNK, ng, ns)
    A_cum = jnp.cumsum(A, axis=-1)
    causal = jnp.tril(jnp.ones((SSD_CHUNK, SSD_CHUNK), dtype=bool))
    seg = A_cum[..., :, None] - A_cum[..., None, :]
    Lmat = jnp.where(causal, jnp.exp(jnp.where(causal, seg, 0.0)), 0.0)
    CB = jnp.einsum('bclgn,bcsgn->bgcls', Cm, Bm)
    y_diag = jnp.einsum('bgcls,bgrcls,bcsgrp->bclgrp', CB, Lmat, X)
    decay_states = jnp.exp(A_cum[..., -1:] - A_cum)
    states = jnp.einsum('bclgn,bgrcl,bclgrp->bcgrpn', Bm, decay_states, X)
    chunk_decay = jnp.exp(A_cum[..., -1])

    def step(h, inp):
        st, dec = inp
        return h * dec[..., None, None] + st, h

    init = jnp.zeros((bsz, ng, r, hp, ns), xdt.dtype)
    _, prev = lax.scan(step, init, (states.transpose(1, 0, 2, 3, 4, 5), chunk_decay.transpose(3, 0, 1, 2)))
    prev = prev.transpose(1, 0, 2, 3, 4, 5)
    y_off = jnp.einsum('bclgn,bcgrpn,bgrcl->bclgrp', Cm, prev, jnp.exp(A_cum))
    return (y_diag + y_off).reshape(bsz, seq, nh, hp)


def ssd_mixer(z, xbc, dt_raw, conv_w, conv_b, dt_bias, a_log, d_skip, norm_w):
    bsz, seq = z.shape[:2]
    xbc = jax.nn.silu(causal_depthwise_conv(xbc, conv_w, conv_b))
    xs = xbc[..., :SSD_WIDTH].reshape(bsz, seq, SSD_HEADS, SSD_HEAD_DIM).astype(jnp.float32)
    bm = xbc[..., SSD_WIDTH:SSD_WIDTH + SSD_GROUPS * SSD_STATE].reshape(bsz, seq, SSD_GROUPS, SSD_STATE).astype(jnp.float32)
    cm = xbc[..., SSD_WIDTH + SSD_GROUPS * SSD_STATE:].reshape(bsz, seq, SSD_GROUPS, SSD_STATE).astype(jnp.float32)
    dt = jax.nn.softplus(dt_raw.astype(jnp.float32) + dt_bias.astype(jnp.float32))
    a = -jnp.exp(a_log.astype(jnp.float32))
    y = ssd_chunked(xs * dt[..., None], a * dt, bm, cm)
    y = y + d_skip.astype(jnp.float32)[:, None] * xs
    y = y.reshape(bsz, seq, SSD_WIDTH) * jax.nn.silu(z.astype(jnp.float32))
    y = y.reshape(bsz, seq, SSD_GROUPS, SSD_WIDTH // SSD_GROUPS)
    y = y * lax.rsqrt(jnp.mean(y * y, axis=-1, keepdims=True) + SSD_NORM_EPS)
    y = y.reshape(bsz, seq, SSD_WIDTH) * norm_w.astype(jnp.float32)
    return y.astype(z.dtype)


def diff_attention(q, k, v, lam):
    seq = q.shape[1]
    scale = ATT_HEAD_DIM ** -0.5
    outs = []
    for blk in range(seq // Q_BLOCK):
        q0 = blk * Q_BLOCK
        q1 = q0 + Q_BLOCK
        s = jnp.einsum('bqhjd,bkhjd->bhjqk', q[:, q0:q1], k[:, :q1]).astype(jnp.float32) * scale
        qpos = q0 + jnp.arange(Q_BLOCK)[:, None]
        kpos = jnp.arange(q1)[None, :]
        s = jnp.where(kpos <= qpos, s, -jnp.inf)
        a = jax.nn.softmax(s, axis=-1)
        w = a[:, :, 0] - lam * a[:, :, 1]
        outs.append(jnp.einsum('bhqk,bkhe->bqhe', w.astype(v.dtype), v[:, :q1]))
    return jnp.concatenate(outs, axis=1)


def diff_attn_mixer(q, k, v, lam_q1, lam_k1, lam_q2, lam_k2, subln_w, lam_init):
    bsz, seq = q.shape[:2]
    q = q.reshape(bsz, seq, ATT_HEADS, 2, ATT_HEAD_DIM)
    k = k.reshape(bsz, seq, ATT_HEADS, 2, ATT_HEAD_DIM)
    v = v.reshape(bsz, seq, ATT_HEADS, 2 * ATT_HEAD_DIM)
    f32 = jnp.float32
    lam = (jnp.exp(jnp.sum(lam_q1.astype(f32) * lam_k1.astype(f32)))
           - jnp.exp(jnp.sum(lam_q2.astype(f32) * lam_k2.astype(f32))) + lam_init)
    o = diff_attention(q, k, v, lam)
    o = rms_norm(o, subln_w, SUBLN_EPS) * (1.0 - lam_init)
    return o.reshape(bsz, seq, ATT_WIDTH)


def moe_ffn(h, w_router, b_router, w_up, b_up, w_down, b_down):
    bsz, seq, d = h.shape
    t = h.reshape(-1, d)
    n_tok = t.shape[0]
    logits = (t @ w_router + b_router).astype(jnp.float32)
    top_val, top_idx = lax.top_k(logits, TOP_K)
    gates = jax.nn.softmax(top_val, axis=-1)
    n_assign = n_tok * TOP_K
    flat_e = top_idx.reshape(-1).astype(jnp.int32)
    flat_tok = jnp.repeat(jnp.arange(n_tok, dtype=jnp.int32), TOP_K)
    flat_g = gates.reshape(-1)
    order = jnp.argsort(flat_e, stable=True)
    se = flat_e[order]
    counts = jnp.bincount(flat_e, length=N_EXPERTS).astype(jnp.int32)
    padded = (counts + EXPERT_BLOCK - 1) // EXPERT_BLOCK * EXPERT_BLOCK
    start = jnp.cumsum(counts) - counts
    pend = jnp.cumsum(padded)
    pstart = pend - padded
    dest = pstart[se] + jnp.arange(n_assign, dtype=jnp.int32) - start[se]
    n_blocks = -(-n_assign // EXPERT_BLOCK) + N_EXPERTS
    n_rows = n_blocks * EXPERT_BLOCK
    row_tok = jnp.zeros((n_rows,), jnp.int32).at[dest].set(flat_tok[order])
    row_gate = jnp.zeros((n_rows,), jnp.float32).at[dest].set(flat_g[order])
    block_e = jnp.minimum(jnp.searchsorted(pend, jnp.arange(n_blocks, dtype=jnp.int32) * EXPERT_BLOCK, side='right'),
                          N_EXPERTS - 1).astype(jnp.int32)
    xs = t[row_tok].reshape(n_blocks, EXPERT_BLOCK, d)

    def expert_block(args):
        xb, e = args
        hu = xb @ w_up[e] + b_up[e]
        gate = jnp.minimum(hu[..., ::2], SWIGLU_LIMIT)
        up = jnp.clip(hu[..., 1::2], -SWIGLU_LIMIT, SWIGLU_LIMIT)
        act = (up + 1.0) * gate * jax.nn.sigmoid(SWIGLU_ALPHA * gate)
        return act @ w_down[e] + b_down[e]

    ys = lax.map(expert_block, (xs, block_e)).reshape(n_rows, d)
    out = jnp.zeros((n_tok, d), jnp.float32).at[row_tok].add(ys.astype(jnp.float32) * row_gate[:, None])
    return out.astype(h.dtype).reshape(bsz, seq, d)


def setup_inputs(seed: int = 0) -> dict:
    key = jax.random.key(seed)
    ks = jax.random.split(key, 32)
    f32 = jnp.float32
    nrm = lambda k, shape, s: jax.random.normal(k, shape, f32) * s
    gain = lambda k, shape: 1.0 + 0.02 * jax.random.normal(k, shape, f32)
    dt = jnp.exp(jax.random.uniform(ks[5], (DEPTH, SSD_HEADS), f32, math.log(1e-3), math.log(1e-1)))
    return {
        'x': nrm(ks[0], (BATCH, SEQ, D_MODEL), 1.0),
        'p': nrm(ks[1], (DEPTH, BATCH, SEQ, PLE_DIM), 1.0),
        'g_mix': gain(ks[2], (DEPTH, D_MODEL)),
        'w_in': nrm(ks[3], (DEPTH, D_MODEL, IN_PROJ), D_MODEL ** -0.5),
        'conv_w': nrm(ks[4], (DEPTH, SSD_CONV, SSD_CONV_CH), SSD_CONV ** -0.5),
        'conv_b': nrm(ks[6], (DEPTH, SSD_CONV_CH), 0.02),
        'dt_bias': dt + jnp.log(-jnp.expm1(-dt)),
        'a_log': jnp.log(jax.random.uniform(ks[7], (DEPTH, SSD_HEADS), f32, 1.0, 16.0)),
        'd_skip': gain(ks[8], (DEPTH, SSD_HEADS)),
        'ssd_norm_w': gain(ks[9], (DEPTH, SSD_WIDTH)),
        'lam_q1': nrm(ks[10], (DEPTH, ATT_HEAD_DIM), 0.1),
        'lam_k1': nrm(ks[11], (DEPTH, ATT_HEAD_DIM), 0.1),
        'lam_q2': nrm(ks[12], (DEPTH, ATT_HEAD_DIM), 0.1),
        'lam_k2': nrm(ks[13], (DEPTH, ATT_HEAD_DIM), 0.1),
        'subln_w': gain(ks[14], (DEPTH, 2 * ATT_HEAD_DIM)),
        'w_out': nrm(ks[15], (DEPTH, MIX_WIDTH, D_MODEL), MIX_WIDTH ** -0.5),
        'g_ffn': gain(ks[16], (DEPTH, D_MODEL)),
        'w_router': nrm(ks[17], (DEPTH, D_MODEL, N_EXPERTS), D_MODEL ** -0.5),
        'b_router': nrm(ks[18], (DEPTH, N_EXPERTS), 0.01),
        'w_up': nrm(ks[19], (DEPTH, N_EXPERTS, D_MODEL, 2 * D_EXPERT), D_MODEL ** -0.5),
        'b_up': nrm(ks[20], (DEPTH, N_EXPERTS, 2 * D_EXPERT), 0.02),
        'w_down': nrm(ks[21], (DEPTH, N_EXPERTS, D_EXPERT, D_MODEL), D_EXPERT ** -0.5),
        'b_down': nrm(ks[22], (DEPTH, N_EXPERTS, D_MODEL), 0.02),
        'g_ple': gain(ks[23], (DEPTH, D_MODEL)),
        'w_ple_gate': nrm(ks[24], (DEPTH, D_MODEL, D_MODEL), D_MODEL ** -0.5),
        'w_ple_proj': nrm(ks[25], (DEPTH, PLE_DIM, D_MODEL), PLE_DIM ** -0.5),
        'g_final': gain(ks[26], (D_MODEL,)),
    }


def reference(x, p, g_mix, w_in, conv_w, conv_b, dt_bias, a_log, d_skip, ssd_norm_w,
              lam_q1, lam_k1, lam_q2, lam_k2, subln_w, w_out, g_ffn, w_router, b_router,
              w_up, b_up, w_down, b_down, g_ple, w_ple_gate, w_ple_proj, g_final):
    for i in range(DEPTH):
        lam_init = 0.8 - 0.6 * math.exp(-0.3 * i)
        h = rms_norm(x, g_mix[i], NORM_EPS)
        proj = h @ w_in[i]
        y_ssd = ssd_mixer(proj[..., OFF_Z:OFF_XBC], proj[..., OFF_XBC:OFF_DT], proj[..., OFF_DT:OFF_Q],
                          conv_w[i], conv_b[i], dt_bias[i], a_log[i], d_skip[i], ssd_norm_w[i])
        y_att = diff_attn_mixer(proj[..., OFF_Q:OFF_K], proj[..., OFF_K:OFF_V], proj[..., OFF_V:IN_PROJ],
                                lam_q1[i], lam_k1[i], lam_q2[i], lam_k2[i], subln_w[i], lam_init)
        x = x + jnp.concatenate([y_ssd, y_att], axis=-1) @ w_out[i]
        x = x + moe_ffn(rms_norm(x, g_ffn[i], NORM_EPS), w_router[i], b_router[i],
                        w_up[i], b_up[i], w_down[i], b_down[i])
        gate = jax.nn.sigmoid(rms_norm(x, g_ple[i], NORM_EPS) @ w_ple_gate[i])
        x = x + (p[i] @ w_ple_proj[i]) * gate
    return rms_norm(x, g_final, NORM_EPS)
```

```python
import math

import jax
import jax.numpy as jnp
from jax import lax
from jax.experimental import pallas as pl
from jax.experimental.pallas import tpu as pltpu

F32 = jnp.float32
BF16 = jnp.bfloat16

D_MODEL = 1024
PLE_DIM = 256
SSD_WIDTH = 512
ATT_WIDTH = 512
SSD_HEAD_DIM = 64
SSD_HEADS = 8
SSD_GROUPS = 2
SSD_STATE = 128
SSD_CONV = 4
SSD_CHUNK = 128
SSD_CONV_CH = SSD_WIDTH + 2 * SSD_GROUPS * SSD_STATE
SSD_NORM_EPS = 1e-5
ATT_HEAD_DIM = 64
ATT_HEADS = 4
SUBLN_EPS = 1e-5
OFF_Z = 0
OFF_XBC = OFF_Z + SSD_WIDTH
OFF_DT = OFF_XBC + SSD_CONV_CH
OFF_Q = OFF_DT + SSD_HEADS
OFF_K = OFF_Q + ATT_WIDTH
OFF_V = OFF_K + ATT_WIDTH
IN_PROJ = OFF_V + ATT_WIDTH
N_EXPERTS = 32
TOP_K = 4
D_EXPERT = 1024
SWIGLU_LIMIT = 7.0
SWIGLU_ALPHA = 1.702
NORM_EPS = 1e-6
LAM_INIT = 0.8 - 0.6 * math.exp(-0.3 * 0)

LANES = 128
SUBLANES = 8
VMEM_LIMIT_BYTES = 56 * 1024 * 1024
ROW_TILE = (D_MODEL // LANES, LANES)

TM_PROJ = 512
ATT_BQ = 256
ATT_BK = 256
ROW_BLK = 256
TM_DISPATCH = TM_PROJ
TM_COMBINE = 256


def _cparams(sem):
    return pltpu.CompilerParams(dimension_semantics=sem, vmem_limit_bytes=VMEM_LIMIT_BYTES)


def _rms(x, w, eps):
    return x * lax.rsqrt(jnp.mean(x * x, axis=-1, keepdims=True) + eps) * w


def _dot(a, b):
    return jnp.dot(a, b, preferred_element_type=F32)


def _dot_nt(a, b):
    return lax.dot_general(a, b, (((1,), (1,)), ((), ())), preferred_element_type=F32)


def _dot_tn(a, b):
    return lax.dot_general(a, b, (((0,), (0,)), ((), ())), preferred_element_type=F32)


def _store_rows(ref, val):
    for s in range(ROW_TILE[0]):
        ref[:, s, :] = val[:, s * LANES:(s + 1) * LANES]


def _load_rows(ref):
    return jnp.concatenate([ref[:, s, :] for s in range(ROW_TILE[0])], axis=-1)


def _rows_spec(n, index_map):
    return pl.BlockSpec((n,) + ROW_TILE, index_map)


def _split3(x):
    hi = x.astype(BF16)
    r1 = x - hi.astype(F32)
    mid = r1.astype(BF16)
    lo = (r1 - mid.astype(F32)).astype(BF16)
    return hi, mid, lo


def _in_proj_kernel(x_ref, g_ref, wz_ref, wxbc_ref, wdt_ref, wq_ref, wk_ref, wv_ref,
                    z_ref, xbc_ref, dt_ref, q_ref, k_ref, v_ref):
    h = _rms(x_ref[...], g_ref[...], NORM_EPS).astype(BF16)
    z_ref[...] = _dot(h, wz_ref[...])
    xbc_ref[...] = _dot(h, wxbc_ref[...])
    dt_ref[...] = _dot(h, wdt_ref[...])
    q_ref[...] = (_dot(h, wq_ref[...]) * (ATT_HEAD_DIM ** -0.5)).astype(BF16)
    k_ref[...] = _dot(h, wk_ref[...]).astype(BF16)
    v_ref[...] = _dot(h, wv_ref[...]).astype(BF16)


def _in_proj(x2, g_mix, w_in):
    t = x2.shape[0]
    wb = w_in.astype(BF16)
    wz = wb[:, OFF_Z:OFF_XBC]
    wxbc = wb[:, OFF_XBC:OFF_DT]
    wdt = jnp.pad(wb[:, OFF_DT:OFF_Q], ((0, 0), (0, LANES - SSD_HEADS)))
    wq = wb[:, OFF_Q:OFF_K]
    wk = wb[:, OFF_K:OFF_V]
    wv = wb[:, OFF_V:IN_PROJ]
    tm = TM_PROJ
    row = lambda n: pl.BlockSpec((tm, n), lambda i: (i, 0))
    full = lambda a: pl.BlockSpec(a.shape, lambda i: (0, 0))
    return pl.pallas_call(
        _in_proj_kernel,
        grid=(t // tm,),
        in_specs=[row(D_MODEL), full(g_mix), full(wz), full(wxbc), full(wdt), full(wq), full(wk), full(wv)],
        out_specs=[row(SSD_WIDTH), row(SSD_CONV_CH), row(LANES), row(ATT_WIDTH), row(ATT_WIDTH), row(ATT_WIDTH)],
        out_shape=[
            jax.ShapeDtypeStruct((t, SSD_WIDTH), F32),
            jax.ShapeDtypeStruct((t, SSD_CONV_CH), F32),
            jax.ShapeDtypeStruct((t, LANES), F32),
            jax.ShapeDtypeStruct((t, ATT_WIDTH), BF16),
            jax.ShapeDtypeStruct((t, ATT_WIDTH), BF16),
            jax.ShapeDtypeStruct((t, ATT_WIDTH), BF16),
        ],
        compiler_params=_cparams(("arbitrary",)),
        name="in_proj",
    )(x2, g_mix, wz, wxbc, wdt, wq, wk, wv)


def _ssd_kernel(xbc_ref, dtr_ref, z_ref, cw_ref, cb_ref, dtb_ref, alog_ref, dskip_ref, nw_ref,
                y_ref, xpad_ref, state_ref):
    L = SSD_CHUNK
    c = pl.program_id(1)

    @pl.when(c == 0)
    def _():
        xpad_ref[0:SUBLANES, :] = jnp.zeros((SUBLANES, SSD_CONV_CH), F32)
        state_ref[...] = jnp.zeros_like(state_ref)

    @pl.when(c != 0)
    def _():
        xpad_ref[0:SUBLANES, :] = xpad_ref[L:L + SUBLANES, :]

    xpad_ref[SUBLANES:SUBLANES + L, :] = xbc_ref[...]

    conv = cb_ref[...]
    for j in range(SSD_CONV):
        off = SUBLANES - (SSD_CONV - 1) + j
        conv = conv + cw_ref[j:j + 1, :] * xpad_ref[off:off + L, :]
    act = conv * jax.nn.sigmoid(conv)
    xs = act[:, :SSD_WIDTH]
    bm = act[:, SSD_WIDTH:SSD_WIDTH + SSD_GROUPS * SSD_STATE].astype(BF16)
    cm = act[:, SSD_WIDTH + SSD_GROUPS * SSD_STATE:].astype(BF16)

    dt_in = dtr_ref[...] + dtb_ref[...]
    dt = jnp.maximum(dt_in, 0.0) + jnp.log1p(jnp.exp(-jnp.abs(dt_in)))
    adt = dt * (-jnp.exp(alog_ref[...]))

    ri = lax.broadcasted_iota(jnp.int32, (L, L), 0)
    ci = lax.broadcasted_iota(jnp.int32, (L, L), 1)
    causal = ci <= ri
    tril = jnp.where(causal, 1.0, 0.0).astype(BF16)
    hi, mid, lo = _split3(adt)
    acum = _dot(tril, hi) + _dot(tril, mid) + _dot(tril, lo)
    acum_t = acum.T
    a_last = acum[L - 1:L, :]
    decay_in = jnp.exp(a_last - acum)
    decay_out = jnp.exp(acum)
    chunk_decay = jnp.exp(a_last)

    lane = lax.broadcasted_iota(jnp.int32, (L, LANES), 1)
    lo_half = lane < SSD_HEAD_DIM

    def per_pair(col_a, col_b):
        return jnp.where(lo_half, col_a, col_b)

    ys = []
    for pair in range(SSD_HEADS // 2):
        g = pair // 2
        h0, h1 = 2 * pair, 2 * pair + 1
        cg = cm[:, g * SSD_STATE:(g + 1) * SSD_STATE]
        bg = bm[:, g * SSD_STATE:(g + 1) * SSD_STATE]
        cb = _dot_nt(cg, bg)
        x_pair = xs[:, pair * LANES:(pair + 1) * LANES]
        xdt = x_pair * per_pair(dt[:, h0:h0 + 1], dt[:, h1:h1 + 1])
        y_pair = jnp.zeros((L, LANES), F32)
        for hh, keep in ((h0, lo_half), (h1, jnp.logical_not(lo_half))):
            seg = acum[:, hh:hh + 1] - acum_t[hh:hh + 1, :]
            lmat = jnp.where(causal, jnp.exp(jnp.where(causal, seg, 0.0)), 0.0)
            m = (cb * lmat).astype(BF16)
            y_pair = y_pair + _dot(m, jnp.where(keep, xdt, 0.0).astype(BF16))
        s_prev = state_ref[pair]
        y_off = _dot(cg, s_prev.astype(BF16)) * per_pair(decay_out[:, h0:h0 + 1], decay_out[:, h1:h1 + 1])
        w_in = (xdt * per_pair(decay_in[:, h0:h0 + 1], decay_in[:, h1:h1 + 1])).astype(BF16)
        cd = jnp.where(lane[0:1, :] < SSD_HEAD_DIM, chunk_decay[:, h0:h0 + 1], chunk_decay[:, h1:h1 + 1])
        state_ref[pair] = s_prev * cd + _dot_tn(bg, w_in)
        ys.append(y_pair + y_off + dskip_ref[:, pair * LANES:(pair + 1) * LANES] * x_pair)

    y = jnp.concatenate(ys, axis=-1)
    zz = z_ref[...]
    y = y * (zz * jax.nn.sigmoid(zz))
    gw = SSD_WIDTH // SSD_GROUPS
    outs = []
    for g in range(SSD_GROUPS):
        yg = y[:, g * gw:(g + 1) * gw]
        outs.append(yg * lax.rsqrt(jnp.mean(yg * yg, axis=-1, keepdims=True) + SSD_NORM_EPS))
    y_ref[...] = (jnp.concatenate(outs, axis=-1) * nw_ref[...]).astype(BF16)


def _ssd(xbc, dtr, z, conv_w, conv_b, dt_bias, a_log, d_skip, norm_w, bsz, seq):
    nc = seq // SSD_CHUNK
    pad_h = lambda v: jnp.pad(v.reshape(1, SSD_HEADS), ((0, 0), (0, LANES - SSD_HEADS)))
    dskip_lanes = jnp.repeat(d_skip, SSD_HEAD_DIM).reshape(1, SSD_WIDTH)
    row = lambda n: pl.BlockSpec((SSD_CHUNK, n), lambda b, c: (b * nc + c, 0))
    full = lambda a: pl.BlockSpec(a.shape, lambda b, c: (0, 0))
    args = (xbc, dtr, z, conv_w, conv_b.reshape(1, -1), pad_h(dt_bias), pad_h(a_log), dskip_lanes,
            norm_w.reshape(1, -1))
    return pl.pallas_call(
        _ssd_kernel,
        grid=(bsz, nc),
        in_specs=[row(SSD_CONV_CH), row(LANES), row(SSD_WIDTH)] + [full(a) for a in args[3:]],
        out_specs=row(SSD_WIDTH),
        out_shape=jax.ShapeDtypeStruct((bsz * seq, SSD_WIDTH), BF16),
        scratch_shapes=[
            pltpu.VMEM((SSD_CHUNK + 2 * SUBLANES, SSD_CONV_CH), F32),
            pltpu.VMEM((SSD_HEADS // 2, SSD_STATE, LANES), F32),
        ],
        compiler_params=_cparams(("arbitrary", "arbitrary")),
        name="ssd",
    )(*args)


def _attn_kernel(q_ref, k_ref, v_ref, lq1_ref, lk1_ref, lq2_ref, lk2_ref, sw_ref, o_ref):
    bq, bk = ATT_BQ, ATT_BK
    qi = pl.program_id(2)
    q = q_ref[...]
    lane = lax.broadcasted_iota(jnp.int32, (bq, LANES), 1)
    zero = jnp.zeros_like(q)
    q1 = jnp.where(lane < ATT_HEAD_DIM, q, zero)
    q2 = jnp.where(lane >= ATT_HEAD_DIM, q, zero)

    def step(kb, vb, carry, mask):
        out = []
        for qm, (m_i, l_i, acc) in zip((q1, q2), carry):
            s = _dot_nt(qm, kb)
            if mask is not None:
                s = jnp.where(mask, s, -jnp.inf)
            m_new = jnp.maximum(m_i, jnp.max(s, axis=-1, keepdims=True))
            alpha = jnp.exp(m_i - m_new)
            p = jnp.exp(s - m_new)
            l_new = alpha * l_i + jnp.sum(p, axis=-1, keepdims=True)
            acc_new = alpha * acc + _dot(p.astype(BF16), vb)
            out.append((m_new, l_new, acc_new))
        return tuple(out)

    def body(j, carry):
        start = pl.multiple_of(j * bk, bk)
        return step(k_ref[pl.ds(start, bk), :], v_ref[pl.ds(start, bk), :], carry, None)

    init_one = (jnp.full((bq, 1), -jnp.inf, F32), jnp.zeros((bq, 1), F32), jnp.zeros((bq, LANES), F32))
    carry = lax.fori_loop(0, qi, body, (init_one, init_one))
    start = pl.multiple_of(qi * bk, bk)
    r = lax.broadcasted_iota(jnp.int32, (bq, bk), 0)
    cidx = lax.broadcasted_iota(jnp.int32, (bq, bk), 1)
    (_, l1, acc1), (_, l2, acc2) = step(k_ref[pl.ds(start, bk), :], v_ref[pl.ds(start, bk), :], carry, cidx <= r)

    lam = (jnp.exp(jnp.sum(lq1_ref[...] * lk1_ref[...], axis=-1, keepdims=True))
           - jnp.exp(jnp.sum(lq2_ref[...] * lk2_ref[...], axis=-1, keepdims=True)) + LAM_INIT)
    o = acc1 / l1 - lam * (acc2 / l2)
    o_ref[...] = (_rms(o, sw_ref[...], SUBLN_EPS) * (1.0 - LAM_INIT)).astype(BF16)


def _attn(q, k, v, lam_q1, lam_k1, lam_q2, lam_k2, subln_w, bsz, seq):
    nq = seq // ATT_BQ
    qspec = pl.BlockSpec((ATT_BQ, LANES), lambda b, h, i: (b * nq + i, h))
    kvspec = pl.BlockSpec((seq, LANES), lambda b, h, i: (b, h))
    full = lambda a: pl.BlockSpec(a.shape, lambda b, h, i: (0, 0))
    lams = [a.reshape(1, -1) for a in (lam_q1, lam_k1, lam_q2, lam_k2)]
    sw = subln_w.reshape(1, -1)
    return pl.pallas_call(
        _attn_kernel,
        grid=(bsz, ATT_HEADS, nq),
        in_specs=[qspec, kvspec, kvspec] + [full(a) for a in lams] + [full(sw)],
        out_specs=qspec,
        out_shape=jax.ShapeDtypeStruct((bsz * seq, ATT_WIDTH), BF16),
        compiler_params=_cparams(("arbitrary", "arbitrary", "arbitrary")),
        name="attn",
    )(q, k, v, *lams, sw)


def _mixer(x2, g_mix, w_in, conv_w, conv_b, dt_bias, a_log, d_skip, ssd_norm_w,
           lam_q1, lam_k1, lam_q2, lam_k2, subln_w, bsz, seq):
    z, xbc, dtr, q, k, v = _in_proj(x2, g_mix.reshape(1, -1), w_in)
    y_ssd = _ssd(xbc, dtr, z, conv_w, conv_b, dt_bias, a_log, d_skip, ssd_norm_w, bsz, seq)
    y_att = _attn(q, k, v, lam_q1, lam_k1, lam_q2, lam_k2, subln_w, bsz, seq)
    return y_ssd, y_att


def _out_proj_kernel(x_ref, ys_ref, ya_ref, wos_ref, woa_ref, g_ref, wrh_ref, wrm_ref, wrl_ref, br_ref,
                     x1_ref, xn_ref, route_ref, gate_ref, cnt_ref):
    tm = TM_PROJ
    x1 = x_ref[...] + _dot(ys_ref[...], wos_ref[...]) + _dot(ya_ref[...], woa_ref[...])
    x1_ref[...] = x1
    xn = _rms(x1, g_ref[...], NORM_EPS)
    _store_rows(xn_ref, xn)

    xh, xm, xl = _split3(xn)
    wh, wm, wl = wrh_ref[...], wrm_ref[...], wrl_ref[...]
    logits = (_dot_nt(wh, xh) + _dot_nt(wm, xh) + _dot_nt(wl, xh)
              + _dot_nt(wh, xm) + _dot_nt(wm, xm) + _dot_nt(wh, xl)) + br_ref[...]

    eidx = lax.broadcasted_iota(jnp.int32, (N_EXPERTS, tm), 0).astype(F32)
    work = logits
    vals, idxs, hots = [], [], []
    for _ in range(TOP_K):
        m = jnp.max(work, axis=0, keepdims=True)
        idx = jnp.min(jnp.where(work == m, eidx, float(N_EXPERTS)), axis=0, keepdims=True)
        hot = eidx == idx
        vals.append(m)
        idxs.append(idx.astype(jnp.int32))
        hots.append(hot)
        work = jnp.where(hot, -jnp.inf, work)
    exps = [jnp.exp(v - vals[0]) for v in vals]
    denom = exps[0] + exps[1] + exps[2] + exps[3]
    gates = [e / denom for e in exps]

    cnt = jnp.zeros((N_EXPERTS, tm), F32)
    for hot in hots:
        cnt = cnt + jnp.where(hot, 1.0, 0.0)
    cnt_b = cnt.astype(BF16)
    r = lax.broadcasted_iota(jnp.int32, (tm, tm), 0)
    c = lax.broadcasted_iota(jnp.int32, (tm, tm), 1)
    before = jnp.where(r < c, 1.0, 0.0).astype(BF16)
    prefix = _dot(cnt_b, before)
    ranks = [jnp.sum(jnp.where(hot, prefix, 0.0), axis=0, keepdims=True).astype(jnp.int32) for hot in hots]

    route_ref[...] = jnp.concatenate(idxs + ranks, axis=0)
    gate_ref[...] = jnp.concatenate(gates + [jnp.zeros((SUBLANES - TOP_K, tm), F32)], axis=0)
    cnt_pad = jnp.concatenate([cnt_b, jnp.zeros((LANES - N_EXPERTS, tm), BF16)], axis=0)
    cnt_ref[0] = _dot_nt(jnp.ones((SUBLANES, tm), BF16), cnt_pad)


def _out_proj(x2, y_ssd, y_att, w_out, g_ffn, w_router, b_router):
    t = x2.shape[0]
    tm = TM_PROJ
    nt = t // tm
    wo = w_out.astype(BF16)
    wos, woa = wo[:SSD_WIDTH], wo[SSD_WIDTH:]
    wrh, wrm, wrl = _split3(w_router.T)
    br = b_router.reshape(N_EXPERTS, 1)
    row = lambda n: pl.BlockSpec((tm, n), lambda i: (i, 0))
    col = pl.BlockSpec((SUBLANES, tm), lambda i: (0, i))
    full = lambda a: pl.BlockSpec(a.shape, lambda i: (0, 0))
    args = (x2, y_ssd, y_att, wos, woa, g_ffn.reshape(1, -1), wrh, wrm, wrl, br)
    return pl.pallas_call(
        _out_proj_kernel,
        grid=(nt,),
        in_specs=[row(D_MODEL), row(SSD_WIDTH), row(ATT_WIDTH)] + [full(a) for a in args[3:]],
        out_specs=[row(D_MODEL), _rows_spec(tm, lambda i: (i, 0, 0)), col, col,
                   pl.BlockSpec((1, SUBLANES, LANES), lambda i: (i, 0, 0))],
        out_shape=[
            jax.ShapeDtypeStruct((t, D_MODEL), F32),
            jax.ShapeDtypeStruct((t,) + ROW_TILE, F32),
            jax.ShapeDtypeStruct((SUBLANES, t), jnp.int32),
            jax.ShapeDtypeStruct((SUBLANES, t), F32),
            jax.ShapeDtypeStruct((nt, SUBLANES, LANES), F32),
        ],
        compiler_params=_cparams(("arbitrary",)),
        name="out_proj",
    )(*args)


def _dispatch_kernel(base_ref, zoff_ref, route_hbm, xn_hbm, xs_hbm, dest_hbm,
                     route_smem, dest_smem, zeros_vmem, sem_idx, sem_rows):
    tm = TM_DISPATCH
    i = pl.program_id(0)
    col0 = pl.multiple_of(i * tm, tm)

    @pl.when(i == 0)
    def _():
        zeros_vmem[...] = jnp.zeros_like(zeros_vmem)

        def zfill(e, _):
            pltpu.make_async_copy(zeros_vmem, xs_hbm.at[pl.ds(zoff_ref[e], ROW_BLK)], sem_rows).start()
            return 0

        lax.fori_loop(0, N_EXPERTS, zfill, 0)

        def zwait(e, _):
            pltpu.make_async_copy(zeros_vmem, xs_hbm.at[pl.ds(0, ROW_BLK)], sem_rows).wait()
            return 0

        lax.fori_loop(0, N_EXPERTS, zwait, 0)

        def tfill(b, _):
            pltpu.make_async_copy(zeros_vmem, xs_hbm.at[pl.ds(b * ROW_BLK, ROW_BLK)], sem_rows).start()
            return 0

        n_blk_total = xs_hbm.shape[0] // ROW_BLK
        lax.fori_loop(zoff_ref[N_EXPERTS], n_blk_total, tfill, 0)
        lax.fori_loop(zoff_ref[N_EXPERTS], n_blk_total, zwait, 0)

    cp = pltpu.make_async_copy(route_hbm.at[:, pl.ds(col0, tm)], route_smem, sem_idx)
    cp.start()
    cp.wait()

    def issue(t, _):
        for k in range(TOP_K):
            d = base_ref[i * N_EXPERTS + route_smem[k, t]] + route_smem[TOP_K + k, t]
            dest_smem[k, t] = d
            pltpu.make_async_copy(xn_hbm.at[col0 + t], xs_hbm.at[d], sem_rows).start()
        for k in range(TOP_K, SUBLANES):
            dest_smem[k, t] = 0
        return 0

    lax.fori_loop(0, tm, issue, 0)
    out = pltpu.make_async_copy(dest_smem, dest_hbm.at[:, pl.ds(col0, tm)], sem_idx)
    out.start()
    pltpu.make_async_copy(xs_hbm.at[pl.ds(0, TOP_K * tm)], xs_hbm.at[pl.ds(0, TOP_K * tm)], sem_rows).wait()
    out.wait()


def _dispatch(route, xn, base, zoff, n_rows):
    t = xn.shape[0]
    tm = TM_DISPATCH
    anyspec = pl.BlockSpec(memory_space=pl.ANY)
    return pl.pallas_call(
        _dispatch_kernel,
        grid_spec=pltpu.PrefetchScalarGridSpec(
            num_scalar_prefetch=2,
            grid=(t // tm,),
            in_specs=[anyspec, anyspec],
            out_specs=[anyspec, anyspec],
            scratch_shapes=[
                pltpu.SMEM((SUBLANES, tm), jnp.int32),
                pltpu.SMEM((SUBLANES, tm), jnp.int32),
                pltpu.VMEM((ROW_BLK,) + ROW_TILE, F32),
                pltpu.SemaphoreType.DMA,
                pltpu.SemaphoreType.DMA,
            ],
        ),
        out_shape=[
            jax.ShapeDtypeStruct((n_rows + ROW_BLK,) + ROW_TILE, F32),
            jax.ShapeDtypeStruct((SUBLANES, t), jnp.int32),
        ],
        compiler_params=_cparams(("arbitrary",)),
        name="dispatch",
    )(base, zoff, route, xn)


def _experts_kernel(blk_e_ref, nvalid_ref, xs_ref, wg_ref, wu_ref, wd_ref, bg_ref, bu_ref, bd_ref, ys_ref):
    i = pl.program_id(0)

    @pl.when(i < nvalid_ref[0])
    def _():
        xb = _load_rows(xs_ref).astype(BF16)
        gate = jnp.minimum(_dot(xb, wg_ref[0]) + bg_ref[0], SWIGLU_LIMIT)
        up = jnp.clip(_dot(xb, wu_ref[0]) + bu_ref[0], -SWIGLU_LIMIT, SWIGLU_LIMIT)
        act = (up + 1.0) * gate * jax.nn.sigmoid(SWIGLU_ALPHA * gate)
        _store_rows(ys_ref, _dot(act.astype(BF16), wd_ref[0]) + bd_ref[0])

    @pl.when(i >= nvalid_ref[0])
    def _():
        ys_ref[...] = jnp.zeros_like(ys_ref)


def _experts(xs, blk_e, nvalid, w_up, b_up, w_down, b_down, n_rows):
    nb = n_rows // ROW_BLK
    wg = w_up[:, :, 0::2].astype(BF16)
    wu = w_up[:, :, 1::2].astype(BF16)
    wd = w_down.astype(BF16)
    bg = b_up[:, 0::2].reshape(N_EXPERTS, 1, D_EXPERT)
    bu = b_up[:, 1::2].reshape(N_EXPERTS, 1, D_EXPERT)
    bd = b_down.reshape(N_EXPERTS, 1, D_MODEL)
    src = lambda i, be, nv: (jnp.minimum(i, nv[0] - 1), 0, 0)
    wspec = lambda n, m: pl.BlockSpec((1, n, m), lambda i, be, nv: (be[i], 0, 0))
    return pl.pallas_call(
        _experts_kernel,
        grid_spec=pltpu.PrefetchScalarGridSpec(
            num_scalar_prefetch=2,
            grid=(nb,),
            in_specs=[_rows_spec(ROW_BLK, src),
                      wspec(D_MODEL, D_EXPERT), wspec(D_MODEL, D_EXPERT), wspec(D_EXPERT, D_MODEL),
                      wspec(1, D_EXPERT), wspec(1, D_EXPERT), wspec(1, D_MODEL)],
            out_specs=_rows_spec(ROW_BLK, lambda i, be, nv: (i, 0, 0)),
        ),
        out_shape=jax.ShapeDtypeStruct((n_rows,) + ROW_TILE, F32),
        compiler_params=_cparams(("arbitrary",)),
        name="experts",
    )(blk_e, nvalid, xs, wg, wu, wd, bg, bu, bd)


def _combine_kernel(dest_hbm, ys_hbm, gate_ref, x1_ref, p_ref, gp_ref, wpg_ref, wpp_ref, gf_ref, o_ref,
                    dest_smem, ybuf, sem_idx, sem_rows):
    tm = TM_COMBINE
    i = pl.program_id(0)
    col0 = pl.multiple_of(i * tm, tm)
    cp = pltpu.make_async_copy(dest_hbm.at[:, pl.ds(col0, tm)], dest_smem, sem_idx)
    cp.start()
    cp.wait()

    def issue(t, _):
        for k in range(TOP_K):
            pltpu.make_async_copy(ys_hbm.at[dest_smem[k, t]], ybuf.at[k, t], sem_rows).start()
        return 0

    lax.fori_loop(0, tm, issue, 0)

    g_cols = jnp.concatenate([gate_ref[...], jnp.zeros((LANES - SUBLANES, tm), F32)], axis=0).T
    pp = _dot(p_ref[...].astype(BF16), wpp_ref[...])

    pltpu.make_async_copy(ybuf, ybuf, sem_rows).wait()
    x2 = x1_ref[...]
    for k in range(TOP_K):
        x2 = x2 + _load_rows(ybuf.at[k]) * g_cols[:, k:k + 1]
    xn = _rms(x2, gp_ref[...], NORM_EPS).astype(BF16)
    x3 = x2 + pp * jax.nn.sigmoid(_dot(xn, wpg_ref[...]))
    o_ref[...] = _rms(x3, gf_ref[...], NORM_EPS)


def _combine(dest, ys, gate_t, x1, p2, g_ple, w_ple_gate, w_ple_proj, g_final):
    t = x1.shape[0]
    tm = TM_COMBINE
    anyspec = pl.BlockSpec(memory_space=pl.ANY)
    row = lambda n: pl.BlockSpec((tm, n), lambda i: (i, 0))
    full = lambda a: pl.BlockSpec(a.shape, lambda i: (0, 0))
    args = (dest, ys, gate_t, x1, p2, g_ple.reshape(1, -1), w_ple_gate.astype(BF16), w_ple_proj.astype(BF16),
            g_final.reshape(1, -1))
    return pl.pallas_call(
        _combine_kernel,
        grid=(t // tm,),
        in_specs=[anyspec, anyspec, pl.BlockSpec((SUBLANES, tm), lambda i: (0, i)), row(D_MODEL), row(PLE_DIM)]
                 + [full(a) for a in args[5:]],
        out_specs=row(D_MODEL),
        out_shape=jax.ShapeDtypeStruct((t, D_MODEL), F32),
        scratch_shapes=[
            pltpu.SMEM((SUBLANES, tm), jnp.int32),
            pltpu.VMEM((TOP_K, tm) + ROW_TILE, F32),
            pltpu.SemaphoreType.DMA,
            pltpu.SemaphoreType.DMA,
        ],
        compiler_params=_cparams(("arbitrary",)),
        name="combine",
    )(*args)


def _routing_tables(cnt):
    tile_cnt = cnt[:, 0, :N_EXPERTS].astype(jnp.int32)
    counts = jnp.sum(tile_cnt, axis=0)
    padded = (counts + ROW_BLK - 1) // ROW_BLK * ROW_BLK
    pend = jnp.cumsum(padded)
    pstart = pend - padded
    tile_off = jnp.cumsum(tile_cnt, axis=0) - tile_cnt
    base = (pstart[None, :] + tile_off).reshape(-1)
    zoff = jnp.concatenate([pstart + counts, pend[-1:] // ROW_BLK])
    return base.astype(jnp.int32), zoff.astype(jnp.int32), pend.astype(jnp.int32)


def kernel(x, p, g_mix, w_in, conv_w, conv_b, dt_bias, a_log, d_skip, ssd_norm_w, lam_q1, lam_k1, lam_q2, lam_k2, subln_w, w_out, g_ffn, w_router, b_router, w_up, b_up, w_down, b_down, g_ple, w_ple_gate, w_ple_proj, g_final):
    bsz, seq, d = x.shape
    t = bsz * seq
    x2 = x.reshape(t, d)
    y_ssd, y_att = _mixer(x2, g_mix[0], w_in[0], conv_w[0], conv_b[0], dt_bias[0], a_log[0], d_skip[0], ssd_norm_w[0],
                          lam_q1[0], lam_k1[0], lam_q2[0], lam_k2[0], subln_w[0], bsz, seq)
    x1, xn, route, gate_t, cnt = _out_proj(x2, y_ssd, y_att, w_out[0], g_ffn[0], w_router[0], b_router[0])

    n_rows = t * TOP_K + N_EXPERTS * ROW_BLK
    nb = n_rows // ROW_BLK
    base, zoff, pend = _routing_tables(cnt)
    blk_start = jnp.arange(nb, dtype=jnp.int32) * ROW_BLK
    nvalid = (pend[-1] // ROW_BLK).reshape(1)
    blk_e = jnp.minimum(jnp.searchsorted(pend, jnp.minimum(blk_start, pend[-1] - 1), side="right"),
                        N_EXPERTS - 1).astype(jnp.int32)

    xs, dest = _dispatch(route, xn, base, zoff, n_rows)
    ys = _experts(xs, blk_e, nvalid, w_up[0], b_up[0], w_down[0], b_down[0], n_rows)
    out = _combine(dest, ys, gate_t, x1, p[0].reshape(t, PLE_DIM), g_ple[0], w_ple_gate[0], w_ple_proj[0], g_final)
    return out.reshape(bsz, seq, d)
```

```python
import math

import jax
import jax.numpy as jnp
from jax import lax
from jax.experimental import pallas as pl
from jax.experimental.pallas import tpu as pltpu

F32 = jnp.float32
BF16 = jnp.bfloat16

D_MODEL = 1024
PLE_DIM = 256
SSD_WIDTH = 512
ATT_WIDTH = 512
SSD_HEAD_DIM = 64
SSD_HEADS = 8
SSD_GROUPS = 2
SSD_STATE = 128
SSD_CONV = 4
SSD_CHUNK = 128
SSD_CONV_CH = SSD_WIDTH + 2 * SSD_GROUPS * SSD_STATE
SSD_NORM_EPS = 1e-5
ATT_HEAD_DIM = 64
ATT_HEADS = 4
SUBLN_EPS = 1e-5
OFF_Z = 0
OFF_XBC = OFF_Z + SSD_WIDTH
OFF_DT = OFF_XBC + SSD_CONV_CH
OFF_Q = OFF_DT + SSD_HEADS
OFF_K = OFF_Q + ATT_WIDTH
OFF_V = OFF_K + ATT_WIDTH
IN_PROJ = OFF_V + ATT_WIDTH
N_EXPERTS = 32
TOP_K = 4
D_EXPERT = 1024
SWIGLU_LIMIT = 7.0
SWIGLU_ALPHA = 1.702
NORM_EPS = 1e-6
LAM_INIT = 0.8 - 0.6 * math.exp(-0.3 * 0)

LANES = 128
SUBLANES = 8
VMEM_LIMIT_BYTES = 56 * 1024 * 1024
ROW_TILE = (D_MODEL // LANES, LANES)

TM_PROJ = 512
ATT_BQ = 256
ATT_BK = 256
ROW_BLK = 256
TM_ROUTE = 256
RUN_CHUNK = 32
DEINT = 2 * LANES


def _cparams(sem):
    return pltpu.CompilerParams(dimension_semantics=sem, vmem_limit_bytes=VMEM_LIMIT_BYTES)


def _rms(x, w, eps):
    return x * lax.rsqrt(jnp.mean(x * x, axis=-1, keepdims=True) + eps) * w


def _dot(a, b):
    return jnp.dot(a, b, preferred_element_type=F32)


def _dot_nt(a, b):
    return lax.dot_general(a, b, (((1,), (1,)), ((), ())), preferred_element_type=F32)


def _dot_tn(a, b):
    return lax.dot_general(a, b, (((0,), (0,)), ((), ())), preferred_element_type=F32)


def _store_rows(ref, val):
    for s in range(ROW_TILE[0]):
        ref[:, s, :] = val[:, s * LANES:(s + 1) * LANES]


def _load_rows(ref):
    return jnp.concatenate([ref[:, s, :] for s in range(ROW_TILE[0])], axis=-1)


def _rows_spec(n, index_map):
    return pl.BlockSpec((n,) + ROW_TILE, index_map)


def _split3(x):
    hi = x.astype(BF16)
    r1 = x - hi.astype(F32)
    mid = r1.astype(BF16)
    lo = (r1 - mid.astype(F32)).astype(BF16)
    return hi, mid, lo


def _in_proj_kernel(x_ref, g_ref, wz_ref, wxbc_ref, wdt_ref, wq_ref, wk_ref, wv_ref,
                    z_ref, xbc_ref, dt_ref, q_ref, k_ref, v_ref):
    h = _rms(x_ref[...], g_ref[...], NORM_EPS).astype(BF16)
    z_ref[...] = _dot(h, wz_ref[...])
    xbc_ref[...] = _dot(h, wxbc_ref[...])
    dt_ref[...] = _dot(h, wdt_ref[...])
    q_ref[...] = (_dot(h, wq_ref[...]) * (ATT_HEAD_DIM ** -0.5)).astype(BF16)
    k_ref[...] = _dot(h, wk_ref[...]).astype(BF16)
    v_ref[...] = _dot(h, wv_ref[...]).astype(BF16)


def _in_proj(x2, g_mix, w_in):
    t = x2.shape[0]
    wb = w_in.astype(BF16)
    wz = wb[:, OFF_Z:OFF_XBC]
    wxbc = wb[:, OFF_XBC:OFF_DT]
    wdt = jnp.pad(wb[:, OFF_DT:OFF_Q], ((0, 0), (0, LANES - SSD_HEADS)))
    wq = wb[:, OFF_Q:OFF_K]
    wk = wb[:, OFF_K:OFF_V]
    wv = wb[:, OFF_V:IN_PROJ]
    tm = TM_PROJ
    row = lambda n: pl.BlockSpec((tm, n), lambda i: (i, 0))
    full = lambda a: pl.BlockSpec(a.shape, lambda i: (0, 0))
    return pl.pallas_call(
        _in_proj_kernel,
        grid=(t // tm,),
        in_specs=[row(D_MODEL), full(g_mix), full(wz), full(wxbc), full(wdt), full(wq), full(wk), full(wv)],
        out_specs=[row(SSD_WIDTH), row(SSD_CONV_CH), row(LANES), row(ATT_WIDTH), row(ATT_WIDTH), row(ATT_WIDTH)],
        out_shape=[
            jax.ShapeDtypeStruct((t, SSD_WIDTH), F32),
            jax.ShapeDtypeStruct((t, SSD_CONV_CH), F32),
            jax.ShapeDtypeStruct((t, LANES), F32),
            jax.ShapeDtypeStruct((t, ATT_WIDTH), BF16),
            jax.ShapeDtypeStruct((t, ATT_WIDTH), BF16),
            jax.ShapeDtypeStruct((t, ATT_WIDTH), BF16),
        ],
        compiler_params=_cparams(("arbitrary",)),
        name="in_proj",
    )(x2, g_mix, wz, wxbc, wdt, wq, wk, wv)


def _ssd_kernel(xbc_ref, dtr_ref, z_ref, cw_ref, cb_ref, dtb_ref, alog_ref, dskip_ref, nw_ref,
                y_ref, xpad_ref, state_ref):
    L = SSD_CHUNK
    c = pl.program_id(1)

    @pl.when(c == 0)
    def _():
        xpad_ref[0:SUBLANES, :] = jnp.zeros((SUBLANES, SSD_CONV_CH), F32)
        state_ref[...] = jnp.zeros_like(state_ref)

    @pl.when(c != 0)
    def _():
        xpad_ref[0:SUBLANES, :] = xpad_ref[L:L + SUBLANES, :]

    xpad_ref[SUBLANES:SUBLANES + L, :] = xbc_ref[...]

    conv = cb_ref[...]
    for j in range(SSD_CONV):
        off = SUBLANES - (SSD_CONV - 1) + j
        conv = conv + cw_ref[j:j + 1, :] * xpad_ref[off:off + L, :]
    act = conv * jax.nn.sigmoid(conv)
    xs = act[:, :SSD_WIDTH]
    bm = act[:, SSD_WIDTH:SSD_WIDTH + SSD_GROUPS * SSD_STATE].astype(BF16)
    cm = act[:, SSD_WIDTH + SSD_GROUPS * SSD_STATE:].astype(BF16)

    dt_in = dtr_ref[...] + dtb_ref[...]
    dt = jnp.maximum(dt_in, 0.0) + jnp.log1p(jnp.exp(-jnp.abs(dt_in)))
    adt = dt * (-jnp.exp(alog_ref[...]))

    ri = lax.broadcasted_iota(jnp.int32, (L, L), 0)
    ci = lax.broadcasted_iota(jnp.int32, (L, L), 1)
    causal = ci <= ri
    tril = jnp.where(causal, 1.0, 0.0).astype(BF16)
    hi, mid, lo = _split3(adt)
    acum = _dot(tril, hi) + _dot(tril, mid) + _dot(tril, lo)
    acum_t = acum.T
    a_last = acum[L - 1:L, :]
    decay_in = jnp.exp(a_last - acum)
    decay_out = jnp.exp(acum)
    chunk_decay = jnp.exp(a_last)

    lane = lax.broadcasted_iota(jnp.int32, (L, LANES), 1)
    lo_half = lane < SSD_HEAD_DIM

    def per_pair(col_a, col_b):
        return jnp.where(lo_half, col_a, col_b)

    ys = []
    for pair in range(SSD_HEADS // 2):
        g = pair // 2
        h0, h1 = 2 * pair, 2 * pair + 1
        cg = cm[:, g * SSD_STATE:(g + 1) * SSD_STATE]
        bg = bm[:, g * SSD_STATE:(g + 1) * SSD_STATE]
        cb = _dot_nt(cg, bg)
        x_pair = xs[:, pair * LANES:(pair + 1) * LANES]
        xdt = x_pair * per_pair(dt[:, h0:h0 + 1], dt[:, h1:h1 + 1])
        y_pair = jnp.zeros((L, LANES), F32)
        for hh, keep in ((h0, lo_half), (h1, jnp.logical_not(lo_half))):
            seg = acum[:, hh:hh + 1] - acum_t[hh:hh + 1, :]
            lmat = jnp.where(causal, jnp.exp(jnp.where(causal, seg, 0.0)), 0.0)
            m = (cb * lmat).astype(BF16)
            y_pair = y_pair + _dot(m, jnp.where(keep, xdt, 0.0).astype(BF16))
        s_prev = state_ref[pair]
        y_off = _dot(cg, s_prev.astype(BF16)) * per_pair(decay_out[:, h0:h0 + 1], decay_out[:, h1:h1 + 1])
        w_in = (xdt * per_pair(decay_in[:, h0:h0 + 1], decay_in[:, h1:h1 + 1])).astype(BF16)
        cd = jnp.where(lane[0:1, :] < SSD_HEAD_DIM, chunk_decay[:, h0:h0 + 1], chunk_decay[:, h1:h1 + 1])
        state_ref[pair] = s_prev * cd + _dot_tn(bg, w_in)
        ys.append(y_pair + y_off + dskip_ref[:, pair * LANES:(pair + 1) * LANES] * x_pair)

    y = jnp.concatenate(ys, axis=-1)
    zz = z_ref[...]
    y = y * (zz * jax.nn.sigmoid(zz))
    gw = SSD_WIDTH // SSD_GROUPS
    outs = []
    for g in range(SSD_GROUPS):
        yg = y[:, g * gw:(g + 1) * gw]
        outs.append(yg * lax.rsqrt(jnp.mean(yg * yg, axis=-1, keepdims=True) + SSD_NORM_EPS))
    y_ref[...] = (jnp.concatenate(outs, axis=-1) * nw_ref[...]).astype(BF16)


def _ssd(xbc, dtr, z, conv_w, conv_b, dt_bias, a_log, d_skip, norm_w, bsz, seq):
    nc = seq // SSD_CHUNK
    pad_h = lambda v: jnp.pad(v.reshape(1, SSD_HEADS), ((0, 0), (0, LANES - SSD_HEADS)))
    dskip_lanes = jnp.repeat(d_skip, SSD_HEAD_DIM).reshape(1, SSD_WIDTH)
    row = lambda n: pl.BlockSpec((SSD_CHUNK, n), lambda b, c: (b * nc + c, 0))
    full = lambda a: pl.BlockSpec(a.shape, lambda b, c: (0, 0))
    args = (xbc, dtr, z, conv_w, conv_b.reshape(1, -1), pad_h(dt_bias), pad_h(a_log), dskip_lanes,
            norm_w.reshape(1, -1))
    return pl.pallas_call(
        _ssd_kernel,
        grid=(bsz, nc),
        in_specs=[row(SSD_CONV_CH), row(LANES), row(SSD_WIDTH)] + [full(a) for a in args[3:]],
        out_specs=row(SSD_WIDTH),
        out_shape=jax.ShapeDtypeStruct((bsz * seq, SSD_WIDTH), BF16),
        scratch_shapes=[
            pltpu.VMEM((SSD_CHUNK + 2 * SUBLANES, SSD_CONV_CH), F32),
            pltpu.VMEM((SSD_HEADS // 2, SSD_STATE, LANES), F32),
        ],
        compiler_params=_cparams(("arbitrary", "arbitrary")),
        name="ssd",
    )(*args)


def _attn_kernel(q_ref, k_ref, v_ref, lq1_ref, lk1_ref, lq2_ref, lk2_ref, sw_ref, o_ref):
    bq, bk = ATT_BQ, ATT_BK
    qi = pl.program_id(2)
    q = q_ref[...]
    lane = lax.broadcasted_iota(jnp.int32, (bq, LANES), 1)
    zero = jnp.zeros_like(q)
    q1 = jnp.where(lane < ATT_HEAD_DIM, q, zero)
    q2 = jnp.where(lane >= ATT_HEAD_DIM, q, zero)

    def step(kb, vb, carry, mask):
        out = []
        for qm, (m_i, l_i, acc) in zip((q1, q2), carry):
            s = _dot_nt(qm, kb)
            if mask is not None:
                s = jnp.where(mask, s, -jnp.inf)
            m_new = jnp.maximum(m_i, jnp.max(s, axis=-1, keepdims=True))
            alpha = jnp.exp(m_i - m_new)
            p = jnp.exp(s - m_new)
            l_new = alpha * l_i + jnp.sum(p, axis=-1, keepdims=True)
            acc_new = alpha * acc + _dot(p.astype(BF16), vb)
            out.append((m_new, l_new, acc_new))
        return tuple(out)

    def body(j, carry):
        start = pl.multiple_of(j * bk, bk)
        return step(k_ref[pl.ds(start, bk), :], v_ref[pl.ds(start, bk), :], carry, None)

    init_one = (jnp.full((bq, 1), -jnp.inf, F32), jnp.zeros((bq, 1), F32), jnp.zeros((bq, LANES), F32))
    carry = lax.fori_loop(0, qi, body, (init_one, init_one))
    start = pl.multiple_of(qi * bk, bk)
    r = lax.broadcasted_iota(jnp.int32, (bq, bk), 0)
    cidx = lax.broadcasted_iota(jnp.int32, (bq, bk), 1)
    (_, l1, acc1), (_, l2, acc2) = step(k_ref[pl.ds(start, bk), :], v_ref[pl.ds(start, bk), :], carry, cidx <= r)

    lam = (jnp.exp(jnp.sum(lq1_ref[...] * lk1_ref[...], axis=-1, keepdims=True))
           - jnp.exp(jnp.sum(lq2_ref[...] * lk2_ref[...], axis=-1, keepdims=True)) + LAM_INIT)
    o = acc1 / l1 - lam * (acc2 / l2)
    o_ref[...] = (_rms(o, sw_ref[...], SUBLN_EPS) * (1.0 - LAM_INIT)).astype(BF16)


def _attn(q, k, v, lam_q1, lam_k1, lam_q2, lam_k2, subln_w, bsz, seq):
    nq = seq // ATT_BQ
    qspec = pl.BlockSpec((ATT_BQ, LANES), lambda b, h, i: (b * nq + i, h))
    kvspec = pl.BlockSpec((seq, LANES), lambda b, h, i: (b, h))
    full = lambda a: pl.BlockSpec(a.shape, lambda b, h, i: (0, 0))
    lams = [a.reshape(1, -1) for a in (lam_q1, lam_k1, lam_q2, lam_k2)]
    sw = subln_w.reshape(1, -1)
    return pl.pallas_call(
        _attn_kernel,
        grid=(bsz, ATT_HEADS, nq),
        in_specs=[qspec, kvspec, kvspec] + [full(a) for a in lams] + [full(sw)],
        out_specs=qspec,
        out_shape=jax.ShapeDtypeStruct((bsz * seq, ATT_WIDTH), BF16),
        compiler_params=_cparams(("arbitrary", "arbitrary", "arbitrary")),
        name="attn",
    )(q, k, v, *lams, sw)


def _mixer(x2, g_mix, w_in, conv_w, conv_b, dt_bias, a_log, d_skip, ssd_norm_w,
           lam_q1, lam_k1, lam_q2, lam_k2, subln_w, bsz, seq):
    z, xbc, dtr, q, k, v = _in_proj(x2, g_mix.reshape(1, -1), w_in)
    y_ssd = _ssd(xbc, dtr, z, conv_w, conv_b, dt_bias, a_log, d_skip, ssd_norm_w, bsz, seq)
    y_att = _attn(q, k, v, lam_q1, lam_k1, lam_q2, lam_k2, subln_w, bsz, seq)
    return y_ssd, y_att


def _out_proj_kernel(x_ref, ys_ref, ya_ref, wos_ref, woa_ref, g_ref, wrh_ref, wrm_ref, wrl_ref, br_ref,
                     x1_ref, xn_ref, route_ref, gate_ref, cnt_ref):
    tm = TM_ROUTE
    x1 = x_ref[...] + _dot(ys_ref[...], wos_ref[...]) + _dot(ya_ref[...], woa_ref[...])
    x1_ref[...] = x1
    xn = _rms(x1, g_ref[...], NORM_EPS)
    xn_ref[...] = xn.astype(BF16)

    xh, xm, xl = _split3(xn)
    wh, wm, wl = wrh_ref[...], wrm_ref[...], wrl_ref[...]
    logits = (_dot_nt(wh, xh) + _dot_nt(wm, xh) + _dot_nt(wl, xh)
              + _dot_nt(wh, xm) + _dot_nt(wm, xm) + _dot_nt(wh, xl)) + br_ref[...]

    eidx = lax.broadcasted_iota(jnp.int32, (N_EXPERTS, tm), 0).astype(F32)
    work = logits
    vals, idxs, hots = [], [], []
    for _ in range(TOP_K):
        m = jnp.max(work, axis=0, keepdims=True)
        idx = jnp.min(jnp.where(work == m, eidx, float(N_EXPERTS)), axis=0, keepdims=True)
        hot = eidx == idx
        vals.append(m)
        idxs.append(idx.astype(jnp.int32))
        hots.append(hot)
        work = jnp.where(hot, -jnp.inf, work)
    exps = [jnp.exp(v - vals[0]) for v in vals]
    denom = exps[0] + exps[1] + exps[2] + exps[3]
    gates = [e / denom for e in exps]

    cnt = jnp.zeros((N_EXPERTS, tm), F32)
    for hot in hots:
        cnt = cnt + jnp.where(hot, 1.0, 0.0)
    cnt_b = cnt.astype(BF16)
    r = lax.broadcasted_iota(jnp.int32, (tm, tm), 0)
    c = lax.broadcasted_iota(jnp.int32, (tm, tm), 1)
    earlier_tok = jnp.where(r < c, 1.0, 0.0).astype(BF16)
    er = lax.broadcasted_iota(jnp.int32, (N_EXPERTS, N_EXPERTS), 0)
    ec = lax.broadcasted_iota(jnp.int32, (N_EXPERTS, N_EXPERTS), 1)
    lower_exp = jnp.where(ec < er, 1.0, 0.0).astype(BF16)
    prefix = _dot(cnt_b, earlier_tok)
    lstart = jnp.sum(_dot(lower_exp, cnt_b), axis=1, keepdims=True)
    pos = prefix + lstart
    lps = [jnp.sum(jnp.where(hot, pos, 0.0), axis=0, keepdims=True).astype(jnp.int32) for hot in hots]

    route_ref[...] = jnp.concatenate(idxs + lps, axis=0)
    gate_ref[...] = jnp.concatenate(gates + [jnp.zeros((SUBLANES - TOP_K, tm), F32)], axis=0)
    cnt_pad = jnp.concatenate([cnt_b, jnp.zeros((LANES - N_EXPERTS, tm), BF16)], axis=0)
    cnt_ref[0] = _dot_nt(jnp.ones((SUBLANES, tm), BF16), cnt_pad)


def _out_proj(x2, y_ssd, y_att, w_out, g_ffn, w_router, b_router):
    t = x2.shape[0]
    tm = TM_ROUTE
    nt = t // tm
    wo = w_out.astype(BF16)
    wos, woa = wo[:SSD_WIDTH], wo[SSD_WIDTH:]
    wrh, wrm, wrl = _split3(w_router.T)
    br = b_router.reshape(N_EXPERTS, 1)
    row = lambda n: pl.BlockSpec((tm, n), lambda i: (i, 0))
    col = pl.BlockSpec((SUBLANES, tm), lambda i: (0, i))
    full = lambda a: pl.BlockSpec(a.shape, lambda i: (0, 0))
    args = (x2, y_ssd, y_att, wos, woa, g_ffn.reshape(1, -1), wrh, wrm, wrl, br)
    return pl.pallas_call(
        _out_proj_kernel,
        grid=(nt,),
        in_specs=[row(D_MODEL), row(SSD_WIDTH), row(ATT_WIDTH)] + [full(a) for a in args[3:]],
        out_specs=[row(D_MODEL), row(D_MODEL), col, col, pl.BlockSpec((1, SUBLANES, LANES), lambda i: (i, 0, 0))],
        out_shape=[
            jax.ShapeDtypeStruct((t, D_MODEL), F32),
            jax.ShapeDtypeStruct((t, D_MODEL), BF16),
            jax.ShapeDtypeStruct((SUBLANES, t), jnp.int32),
            jax.ShapeDtypeStruct((SUBLANES, t), F32),
            jax.ShapeDtypeStruct((nt, SUBLANES, LANES), F32),
        ],
        compiler_params=_cparams(("arbitrary",)),
        name="out_proj",
    )(*args)


def _for_each_run_piece(n, piece):
    shift = RUN_CHUNK.bit_length() - 1
    n_full = lax.shift_right_logical(n, shift)

    def body(j, _):
        piece(j * RUN_CHUNK, RUN_CHUNK)
        return 0

    lax.fori_loop(0, n_full, body, 0)
    off = n_full * RUN_CHUNK
    b = RUN_CHUNK // 2
    while b >= 1:
        @pl.when((n & b) != 0)
        def _(off=off, b=b):
            piece(off, b)

        off = off + (n & b)
        b //= 2


def _one_hot_positions(route_ref, n_pos, tm):
    r = lax.broadcasted_iota(jnp.int32, (n_pos, tm), 0)
    return [r == route_ref[TOP_K + k:TOP_K + k + 1, :] for k in range(TOP_K)]


def _dispatch_kernel(cnt_ref, ls_ref, base_ref, zoff_ref, xn_ref, route_ref, xs_hbm, xloc, zeros_vmem, sem_z, sems):
    tm = TM_ROUTE
    n_pos = TOP_K * tm
    i = pl.program_id(0)
    n_steps = pl.num_programs(0)
    slot = lax.rem(i, 2)

    @pl.when(i == 0)
    def _():
        zeros_vmem[...] = jnp.zeros_like(zeros_vmem)

        def zfill(e, _):
            pltpu.make_async_copy(zeros_vmem, xs_hbm.at[pl.ds(zoff_ref[e], ROW_BLK)], sem_z).start()
            return 0

        def zwait(e, _):
            pltpu.make_async_copy(zeros_vmem, xs_hbm.at[pl.ds(0, ROW_BLK)], sem_z).wait()
            return 0

        def tfill(b, _):
            pltpu.make_async_copy(zeros_vmem, xs_hbm.at[pl.ds(b * ROW_BLK, ROW_BLK)], sem_z).start()
            return 0

        lax.fori_loop(0, N_EXPERTS, zfill, 0)
        lax.fori_loop(0, N_EXPERTS, zwait, 0)
        n_blk_total = xs_hbm.shape[0] // ROW_BLK
        lax.fori_loop(zoff_ref[N_EXPERTS], n_blk_total, tfill, 0)
        lax.fori_loop(zoff_ref[N_EXPERTS], n_blk_total, zwait, 0)

    masks = _one_hot_positions(route_ref, n_pos, tm)
    sel = jnp.where(masks[0] | masks[1] | masks[2] | masks[3], 1.0, 0.0).astype(BF16)
    rows = _dot(sel, xn_ref[...])

    def slot_wait(s):
        pltpu.make_async_copy(xloc.at[s], xs_hbm.at[pl.ds(0, n_pos)], sems.at[s]).wait()

    @pl.when(i >= 2)
    def _():
        slot_wait(slot)

    _store_rows(xloc.at[slot], rows)

    def per_expert(e, _):
        idx = i * N_EXPERTS + e
        src0 = ls_ref[idx]
        dst0 = base_ref[idx]

        def piece(off, size):
            pltpu.make_async_copy(xloc.at[slot, pl.ds(src0 + off, size)],
                                  xs_hbm.at[pl.ds(dst0 + off, size)], sems.at[slot]).start()

        _for_each_run_piece(cnt_ref[idx], piece)
        return 0

    lax.fori_loop(0, N_EXPERTS, per_expert, 0)

    @pl.when(i == n_steps - 1)
    def _():
        slot_wait(slot)

        @pl.when(n_steps > 1)
        def _():
            slot_wait(1 - slot)


def _dispatch(xn, route, tables, n_rows):
    t = xn.shape[0]
    tm = TM_ROUTE
    cnt_tbl, ls_tbl, base_tbl, zoff = tables
    return pl.pallas_call(
        _dispatch_kernel,
        grid_spec=pltpu.PrefetchScalarGridSpec(
            num_scalar_prefetch=4,
            grid=(t // tm,),
            in_specs=[pl.BlockSpec((tm, D_MODEL), lambda i, *_: (i, 0)),
                      pl.BlockSpec((SUBLANES, tm), lambda i, *_: (0, i))],
            out_specs=pl.BlockSpec(memory_space=pl.ANY),
            scratch_shapes=[
                pltpu.VMEM((2, TOP_K * tm) + ROW_TILE, F32),
                pltpu.VMEM((ROW_BLK,) + ROW_TILE, F32),
                pltpu.SemaphoreType.DMA,
                pltpu.SemaphoreType.DMA((2,)),
            ],
        ),
        out_shape=jax.ShapeDtypeStruct((n_rows + ROW_BLK,) + ROW_TILE, F32),
        compiler_params=_cparams(("arbitrary",)),
        name="dispatch",
    )(cnt_tbl, ls_tbl, base_tbl, zoff, xn, route)


def _experts_kernel(blk_e_ref, nvalid_ref, xs_ref, wup_ref, wdn_ref, bg_ref, bu_ref, bd_ref, ys_ref,
                    wg_s, wu_s, wd_s):
    i = pl.program_id(0)
    e = blk_e_ref[i]
    e_prev = blk_e_ref[jnp.maximum(i - 1, 0)]

    @pl.when((i == 0) | (e != e_prev))
    def _():
        src = lax.broadcasted_iota(jnp.int32, (DEINT, DEINT), 0)
        dst = lax.broadcasted_iota(jnp.int32, (DEINT, DEINT), 1)
        perm = jnp.where(src == jnp.where(dst < LANES, 2 * dst, 2 * (dst - LANES) + 1), 1.0, 0.0).astype(BF16)
        for g in range(2 * D_EXPERT // DEINT):
            sep = _dot(wup_ref[0, :, g * DEINT:(g + 1) * DEINT].astype(BF16), perm)
            wg_s[:, g * LANES:(g + 1) * LANES] = sep[:, :LANES].astype(BF16)
            wu_s[:, g * LANES:(g + 1) * LANES] = sep[:, LANES:].astype(BF16)
        wd_s[...] = wdn_ref[0].astype(BF16)

    @pl.when(i < nvalid_ref[0])
    def _():
        xb = _load_rows(xs_ref).astype(BF16)
        gate = jnp.minimum(_dot(xb, wg_s[...]) + bg_ref[0], SWIGLU_LIMIT)
        up = jnp.clip(_dot(xb, wu_s[...]) + bu_ref[0], -SWIGLU_LIMIT, SWIGLU_LIMIT)
        act = (up + 1.0) * gate * jax.nn.sigmoid(SWIGLU_ALPHA * gate)
        _store_rows(ys_ref, _dot(act.astype(BF16), wd_s[...]) + bd_ref[0])

    @pl.when(i >= nvalid_ref[0])
    def _():
        ys_ref[...] = jnp.zeros_like(ys_ref)


def _experts(xs, blk_e, nvalid, w_up, b_up, w_down, b_down, n_rows):
    nb = n_rows // ROW_BLK
    bg = b_up[:, 0::2].reshape(N_EXPERTS, 1, D_EXPERT)
    bu = b_up[:, 1::2].reshape(N_EXPERTS, 1, D_EXPERT)
    bd = b_down.reshape(N_EXPERTS, 1, D_MODEL)
    src = lambda i, be, nv: (jnp.minimum(i, nv[0] - 1), 0, 0)
    wspec = lambda n, m: pl.BlockSpec((1, n, m), lambda i, be, nv: (be[i], 0, 0))
    return pl.pallas_call(
        _experts_kernel,
        grid_spec=pltpu.PrefetchScalarGridSpec(
            num_scalar_prefetch=2,
            grid=(nb,),
            in_specs=[_rows_spec(ROW_BLK, src),
                      wspec(D_MODEL, 2 * D_EXPERT), wspec(D_EXPERT, D_MODEL),
                      wspec(1, D_EXPERT), wspec(1, D_EXPERT), wspec(1, D_MODEL)],
            out_specs=_rows_spec(ROW_BLK, lambda i, be, nv: (i, 0, 0)),
            scratch_shapes=[
                pltpu.VMEM((D_MODEL, D_EXPERT), BF16),
                pltpu.VMEM((D_MODEL, D_EXPERT), BF16),
                pltpu.VMEM((D_EXPERT, D_MODEL), BF16),
            ],
        ),
        out_shape=jax.ShapeDtypeStruct((n_rows,) + ROW_TILE, F32),
        compiler_params=_cparams(("arbitrary",)),
        name="experts",
    )(blk_e, nvalid, xs, w_up, w_down, bg, bu, bd)


def _combine_kernel(cnt_ref, ls_ref, base_ref, ys_hbm, route_ref, gate_ref, x1_ref, p_ref, gp_ref, wpg_ref,
                    wpp_ref, gf_ref, o_ref, yloc, sems):
    tm = TM_ROUTE
    n_pos = TOP_K * tm
    i = pl.program_id(0)
    n_steps = pl.num_programs(0)
    slot = lax.rem(i, 2)

    def gather_tile(tile, s):
        def per_expert(e, _):
            idx = tile * N_EXPERTS + e
            src0 = base_ref[idx]
            dst0 = ls_ref[idx]

            def piece(off, size):
                pltpu.make_async_copy(ys_hbm.at[pl.ds(src0 + off, size)],
                                      yloc.at[s, pl.ds(dst0 + off, size)], sems.at[s]).start()

            _for_each_run_piece(cnt_ref[idx], piece)
            return 0

        lax.fori_loop(0, N_EXPERTS, per_expert, 0)

    @pl.when(i == 0)
    def _():
        gather_tile(0, 0)

    @pl.when(i + 1 < n_steps)
    def _():
        gather_tile(i + 1, 1 - slot)

    masks = _one_hot_positions(route_ref, n_pos, tm)
    sel = jnp.where(masks[0] | masks[1] | masks[2] | masks[3], 1.0, 0.0).astype(BF16)
    gsel = jnp.zeros((n_pos, tm), F32)
    for k in range(TOP_K):
        gsel = gsel + jnp.where(masks[k], gate_ref[k:k + 1, :], 0.0)
    g_pos = jnp.sum(gsel, axis=1, keepdims=True)
    pp = _dot(p_ref[...].astype(BF16), wpp_ref[...])

    pltpu.make_async_copy(ys_hbm.at[pl.ds(0, n_pos)], yloc.at[slot], sems.at[slot]).wait()
    y_gated = (_load_rows(yloc.at[slot]) * g_pos).astype(BF16)
    x2 = x1_ref[...] + _dot_tn(sel, y_gated)
    xn = _rms(x2, gp_ref[...], NORM_EPS).astype(BF16)
    x3 = x2 + pp * jax.nn.sigmoid(_dot(xn, wpg_ref[...]))
    o_ref[...] = _rms(x3, gf_ref[...], NORM_EPS)


def _combine(ys, route, gate_t, x1, p2, tables, g_ple, w_ple_gate, w_ple_proj, g_final):
    t = x1.shape[0]
    tm = TM_ROUTE
    cnt_tbl, ls_tbl, base_tbl, _ = tables
    row = lambda n: pl.BlockSpec((tm, n), lambda i, *_: (i, 0))
    col = pl.BlockSpec((SUBLANES, tm), lambda i, *_: (0, i))
    full = lambda a: pl.BlockSpec(a.shape, lambda i, *_: (0, 0))
    consts = (g_ple.reshape(1, -1), w_ple_gate.astype(BF16), w_ple_proj.astype(BF16), g_final.reshape(1, -1))
    return pl.pallas_call(
        _combine_kernel,
        grid_spec=pltpu.PrefetchScalarGridSpec(
            num_scalar_prefetch=3,
            grid=(t // tm,),
            in_specs=[pl.BlockSpec(memory_space=pl.ANY), col, col, row(D_MODEL), row(PLE_DIM)]
                     + [full(a) for a in consts],
            out_specs=row(D_MODEL),
            scratch_shapes=[
                pltpu.VMEM((2, TOP_K * tm) + ROW_TILE, F32),
                pltpu.SemaphoreType.DMA((2,)),
            ],
        ),
        out_shape=jax.ShapeDtypeStruct((t, D_MODEL), F32),
        compiler_params=_cparams(("arbitrary",)),
        name="combine",
    )(cnt_tbl, ls_tbl, base_tbl, ys, route, gate_t, x1, p2, *consts)


def _routing_tables(cnt, n_blocks):
    tile_cnt = cnt[:, 0, :N_EXPERTS].astype(jnp.int32)
    counts = jnp.sum(tile_cnt, axis=0)
    padded = (counts + ROW_BLK - 1) // ROW_BLK * ROW_BLK
    pend = jnp.cumsum(padded)
    pstart = pend - padded
    base = pstart[None, :] + jnp.cumsum(tile_cnt, axis=0) - tile_cnt
    lstart = jnp.cumsum(tile_cnt, axis=1) - tile_cnt
    nvalid = pend[-1:] // ROW_BLK
    zoff = jnp.concatenate([pstart + counts, nvalid])
    blk_start = jnp.minimum(jnp.arange(n_blocks, dtype=jnp.int32) * ROW_BLK, pend[-1] - 1)
    blk_e = jnp.minimum(jnp.sum((pend[None, :] <= blk_start[:, None]).astype(jnp.int32), axis=1), N_EXPERTS - 1)
    i32 = lambda a: a.reshape(-1).astype(jnp.int32)
    return (i32(tile_cnt), i32(lstart), i32(base), i32(zoff)), i32(blk_e), i32(nvalid)


def kernel(x, p, g_mix, w_in, conv_w, conv_b, dt_bias, a_log, d_skip, ssd_norm_w, lam_q1, lam_k1, lam_q2, lam_k2, subln_w, w_out, g_ffn, w_router, b_router, w_up, b_up, w_down, b_down, g_ple, w_ple_gate, w_ple_proj, g_final):
    bsz, seq, d = x.shape
    t = bsz * seq
    x2 = x.reshape(t, d)
    y_ssd, y_att = _mixer(x2, g_mix[0], w_in[0], conv_w[0], conv_b[0], dt_bias[0], a_log[0], d_skip[0], ssd_norm_w[0],
                          lam_q1[0], lam_k1[0], lam_q2[0], lam_k2[0], subln_w[0], bsz, seq)
    x1, xn, route, gate_t, cnt = _out_proj(x2, y_ssd, y_att, w_out[0], g_ffn[0], w_router[0], b_router[0])

    n_rows = t * TOP_K + N_EXPERTS * ROW_BLK
    tables, blk_e, nvalid = _routing_tables(cnt, n_rows // ROW_BLK)
    xs = _dispatch(xn, route, tables, n_rows)
    ys = _experts(xs, blk_e, nvalid, w_up[0], b_up[0], w_down[0], b_down[0], n_rows)
    out = _combine(ys, route, gate_t, x1, p[0].reshape(t, PLE_DIM), tables, g_ple[0], w_ple_gate[0], w_ple_proj[0],
                   g_final)
    return out.reshape(bsz, seq, d)
```

```python
import math

import jax
import jax.numpy as jnp
from jax import lax
from jax.experimental import pallas as pl
from jax.experimental.pallas import tpu as pltpu

F32 = jnp.float32
BF16 = jnp.bfloat16

D_MODEL = 1024
PLE_DIM = 256
SSD_WIDTH = 512
ATT_WIDTH = 512
SSD_HEAD_DIM = 64
SSD_HEADS = 8
SSD_GROUPS = 2
SSD_STATE = 128
SSD_CONV = 4
SSD_CHUNK = 128
SSD_CONV_CH = SSD_WIDTH + 2 * SSD_GROUPS * SSD_STATE
SSD_NORM_EPS = 1e-5
ATT_HEAD_DIM = 64
ATT_HEADS = 4
SUBLN_EPS = 1e-5
OFF_Z = 0
OFF_XBC = OFF_Z + SSD_WIDTH
OFF_DT = OFF_XBC + SSD_CONV_CH
OFF_Q = OFF_DT + SSD_HEADS
OFF_K = OFF_Q + ATT_WIDTH
OFF_V = OFF_K + ATT_WIDTH
IN_PROJ = OFF_V + ATT_WIDTH
N_EXPERTS = 32
TOP_K = 4
D_EXPERT = 1024
SWIGLU_LIMIT = 7.0
SWIGLU_ALPHA = 1.702
NORM_EPS = 1e-6
LAM_INIT = 0.8 - 0.6 * math.exp(-0.3 * 0)

LANES = 128
SUBLANES = 8
VMEM_LIMIT_BYTES = 56 * 1024 * 1024
ROW_SUB = D_MODEL // LANES

TM_PROJ = 512
ATT_BQ = 512
ATT_BK = 512
ROW_BLK = 256
TM_ROUTE = 256
RUN_CHUNK = 32
DEINT = 2 * LANES


def _cparams(sem):
    return pltpu.CompilerParams(dimension_semantics=sem, vmem_limit_bytes=VMEM_LIMIT_BYTES)


def _rms(x, w, eps):
    return x * lax.rsqrt(jnp.mean(x * x, axis=-1, keepdims=True) + eps) * w


def _dot(a, b):
    return jnp.dot(a, b, preferred_element_type=F32)


def _dot_nt(a, b):
    return lax.dot_general(a, b, (((1,), (1,)), ((), ())), preferred_element_type=F32)


def _dot_tn(a, b):
    return lax.dot_general(a, b, (((0,), (0,)), ((), ())), preferred_element_type=F32)


def _store_rows(ref, val):
    n = val.shape[0]
    for s in range(ROW_SUB):
        ref[pl.ds(s, n, stride=ROW_SUB), :] = val[:, s * LANES:(s + 1) * LANES]


def _load_rows(ref):
    n = ref.shape[0] // ROW_SUB
    return jnp.concatenate([ref[pl.ds(s, n, stride=ROW_SUB), :] for s in range(ROW_SUB)], axis=-1)


def _rows_spec(n, index_map):
    return pl.BlockSpec((n * ROW_SUB, LANES), index_map)


def _row_slice(start, n):
    return pl.ds(pl.multiple_of(start * ROW_SUB, ROW_SUB), n * ROW_SUB)


def _split3(x):
    hi = x.astype(BF16)
    r1 = x - hi.astype(F32)
    mid = r1.astype(BF16)
    lo = (r1 - mid.astype(F32)).astype(BF16)
    return hi, mid, lo


def _in_proj_kernel(x_ref, g_ref, wz_ref, wxbc_ref, wdt_ref, wq_ref, wk_ref, wv_ref,
                    z_ref, xbc_ref, dt_ref, q_ref, k_ref, v_ref):
    h = _rms(x_ref[...], g_ref[...], NORM_EPS).astype(BF16)
    z_ref[...] = _dot(h, wz_ref[...])
    xbc_ref[...] = _dot(h, wxbc_ref[...])
    dt_ref[...] = _dot(h, wdt_ref[...])
    q_ref[...] = (_dot(h, wq_ref[...]) * (ATT_HEAD_DIM ** -0.5)).astype(BF16)
    k_ref[...] = _dot(h, wk_ref[...]).astype(BF16)
    v_ref[...] = _dot(h, wv_ref[...]).astype(BF16)


def _in_proj(x2, g_mix, w_in):
    t = x2.shape[0]
    wb = w_in.astype(BF16)
    wz = wb[:, OFF_Z:OFF_XBC]
    wxbc = wb[:, OFF_XBC:OFF_DT]
    wdt = jnp.pad(wb[:, OFF_DT:OFF_Q], ((0, 0), (0, LANES - SSD_HEADS)))
    wq = wb[:, OFF_Q:OFF_K]
    wk = wb[:, OFF_K:OFF_V]
    wv = wb[:, OFF_V:IN_PROJ]
    tm = TM_PROJ
    row = lambda n: pl.BlockSpec((tm, n), lambda i: (i, 0))
    full = lambda a: pl.BlockSpec(a.shape, lambda i: (0, 0))
    return pl.pallas_call(
        _in_proj_kernel,
        grid=(t // tm,),
        in_specs=[row(D_MODEL), full(g_mix), full(wz), full(wxbc), full(wdt), full(wq), full(wk), full(wv)],
        out_specs=[row(SSD_WIDTH), row(SSD_CONV_CH), row(LANES), row(ATT_WIDTH), row(ATT_WIDTH), row(ATT_WIDTH)],
        out_shape=[
            jax.ShapeDtypeStruct((t, SSD_WIDTH), F32),
            jax.ShapeDtypeStruct((t, SSD_CONV_CH), F32),
            jax.ShapeDtypeStruct((t, LANES), F32),
            jax.ShapeDtypeStruct((t, ATT_WIDTH), BF16),
            jax.ShapeDtypeStruct((t, ATT_WIDTH), BF16),
            jax.ShapeDtypeStruct((t, ATT_WIDTH), BF16),
        ],
        compiler_params=_cparams(("arbitrary",)),
        name="in_proj",
    )(x2, g_mix, wz, wxbc, wdt, wq, wk, wv)


def _ssd_kernel(xbc_ref, dtr_ref, z_ref, cw_ref, cb_ref, dtb_ref, alog_ref, dskip_ref, nw_ref,
                y_ref, xpad_ref, state_ref):
    L = SSD_CHUNK
    c = pl.program_id(1)

    @pl.when(c == 0)
    def _():
        xpad_ref[0:SUBLANES, :] = jnp.zeros((SUBLANES, SSD_CONV_CH), F32)
        state_ref[...] = jnp.zeros_like(state_ref)

    @pl.when(c != 0)
    def _():
        xpad_ref[0:SUBLANES, :] = xpad_ref[L:L + SUBLANES, :]

    xpad_ref[SUBLANES:SUBLANES + L, :] = xbc_ref[...]

    conv = cb_ref[...]
    for j in range(SSD_CONV):
        off = SUBLANES - (SSD_CONV - 1) + j
        conv = conv + cw_ref[j:j + 1, :] * xpad_ref[off:off + L, :]
    act = conv * jax.nn.sigmoid(conv)
    xs = act[:, :SSD_WIDTH]
    bm = act[:, SSD_WIDTH:SSD_WIDTH + SSD_GROUPS * SSD_STATE].astype(BF16)
    cm = act[:, SSD_WIDTH + SSD_GROUPS * SSD_STATE:].astype(BF16)

    dt_in = dtr_ref[...] + dtb_ref[...]
    dt = jnp.maximum(dt_in, 0.0) + jnp.log1p(jnp.exp(-jnp.abs(dt_in)))
    adt = dt * (-jnp.exp(alog_ref[...]))

    ri = lax.broadcasted_iota(jnp.int32, (L, L), 0)
    ci = lax.broadcasted_iota(jnp.int32, (L, L), 1)
    causal = ci <= ri
    tril = jnp.where(causal, 1.0, 0.0).astype(BF16)
    hi, mid, lo = _split3(adt)
    acum = _dot(tril, hi) + _dot(tril, mid) + _dot(tril, lo)
    acum_t = acum.T
    a_last = acum[L - 1:L, :]
    decay_in = jnp.exp(a_last - acum)
    decay_out = jnp.exp(acum)
    chunk_decay = jnp.exp(a_last)

    lane = lax.broadcasted_iota(jnp.int32, (L, LANES), 1)
    lo_half = lane < SSD_HEAD_DIM

    def per_pair(col_a, col_b):
        return jnp.where(lo_half, col_a, col_b)

    ys = []
    for pair in range(SSD_HEADS // 2):
        g = pair // 2
        h0, h1 = 2 * pair, 2 * pair + 1
        cg = cm[:, g * SSD_STATE:(g + 1) * SSD_STATE]
        bg = bm[:, g * SSD_STATE:(g + 1) * SSD_STATE]
        cb = _dot_nt(cg, bg)
        x_pair = xs[:, pair * LANES:(pair + 1) * LANES]
        xdt = x_pair * per_pair(dt[:, h0:h0 + 1], dt[:, h1:h1 + 1])
        y_pair = jnp.zeros((L, LANES), F32)
        for hh, keep in ((h0, lo_half), (h1, jnp.logical_not(lo_half))):
            seg = acum[:, hh:hh + 1] - acum_t[hh:hh + 1, :]
            lmat = jnp.where(causal, jnp.exp(jnp.where(causal, seg, 0.0)), 0.0)
            m = (cb * lmat).astype(BF16)
            y_pair = y_pair + _dot(m, jnp.where(keep, xdt, 0.0).astype(BF16))
        s_prev = state_ref[pair]
        y_off = _dot(cg, s_prev.astype(BF16)) * per_pair(decay_out[:, h0:h0 + 1], decay_out[:, h1:h1 + 1])
        w_in = (xdt * per_pair(decay_in[:, h0:h0 + 1], decay_in[:, h1:h1 + 1])).astype(BF16)
        cd = jnp.where(lane[0:1, :] < SSD_HEAD_DIM, chunk_decay[:, h0:h0 + 1], chunk_decay[:, h1:h1 + 1])
        state_ref[pair] = s_prev * cd + _dot_tn(bg, w_in)
        ys.append(y_pair + y_off + dskip_ref[:, pair * LANES:(pair + 1) * LANES] * x_pair)

    y = jnp.concatenate(ys, axis=-1)
    zz = z_ref[...]
    y = y * (zz * jax.nn.sigmoid(zz))
    gw = SSD_WIDTH // SSD_GROUPS
    outs = []
    for g in range(SSD_GROUPS):
        yg = y[:, g * gw:(g + 1) * gw]
        outs.append(yg * lax.rsqrt(jnp.mean(yg * yg, axis=-1, keepdims=True) + SSD_NORM_EPS))
    y_ref[...] = (jnp.concatenate(outs, axis=-1) * nw_ref[...]).astype(BF16)


def _ssd(xbc, dtr, z, conv_w, conv_b, dt_bias, a_log, d_skip, norm_w, bsz, seq):
    nc = seq // SSD_CHUNK
    pad_h = lambda v: jnp.pad(v.reshape(1, SSD_HEADS), ((0, 0), (0, LANES - SSD_HEADS)))
    dskip_lanes = jnp.repeat(d_skip, SSD_HEAD_DIM).reshape(1, SSD_WIDTH)
    row = lambda n: pl.BlockSpec((SSD_CHUNK, n), lambda b, c: (b * nc + c, 0))
    full = lambda a: pl.BlockSpec(a.shape, lambda b, c: (0, 0))
    args = (xbc, dtr, z, conv_w, conv_b.reshape(1, -1), pad_h(dt_bias), pad_h(a_log), dskip_lanes,
            norm_w.reshape(1, -1))
    return pl.pallas_call(
        _ssd_kernel,
        grid=(bsz, nc),
        in_specs=[row(SSD_CONV_CH), row(LANES), row(SSD_WIDTH)] + [full(a) for a in args[3:]],
        out_specs=row(SSD_WIDTH),
        out_shape=jax.ShapeDtypeStruct((bsz * seq, SSD_WIDTH), BF16),
        scratch_shapes=[
            pltpu.VMEM((SSD_CHUNK + 2 * SUBLANES, SSD_CONV_CH), F32),
            pltpu.VMEM((SSD_HEADS // 2, SSD_STATE, LANES), F32),
        ],
        compiler_params=_cparams(("arbitrary", "arbitrary")),
        name="ssd",
    )(*args)


def _attn_kernel(q_ref, k_ref, v_ref, lq1_ref, lk1_ref, lq2_ref, lk2_ref, sw_ref, o_ref,
                 s_scr, mt_scr, lt_scr, acc_scr):
    bq, bk = ATT_BQ, ATT_BK
    qi = pl.program_id(2)
    q = q_ref[...]
    lane = lax.broadcasted_iota(jnp.int32, (bq, LANES), 1)
    zero = jnp.zeros_like(q)
    q_maps = (jnp.where(lane < ATT_HEAD_DIM, q, zero), jnp.where(lane >= ATT_HEAD_DIM, q, zero))
    n_maps = len(q_maps)

    def fold(t, op):
        out = t[:, :LANES]
        for c in range(1, bk // LANES):
            out = op(out, t[:, c * LANES:(c + 1) * LANES])
        return out

    def scores(j, mask):
        kb = k_ref[pl.ds(pl.multiple_of(j * bk, bk), bk), :]
        for m in range(n_maps):
            s = _dot_nt(q_maps[m], kb)
            if mask is not None:
                s = jnp.where(mask, s, -jnp.inf)
            s_scr[m, j] = s
            mt_scr[m] = jnp.maximum(mt_scr[m], fold(s, jnp.maximum))

    def scores_body(j, _):
        scores(j, None)
        return 0

    mt_scr[...] = jnp.full(mt_scr.shape, -jnp.inf, F32)
    lax.fori_loop(0, qi, scores_body, 0)
    r = lax.broadcasted_iota(jnp.int32, (bq, bk), 0)
    cidx = lax.broadcasted_iota(jnp.int32, (bq, bk), 1)
    scores(qi, cidx <= r)
    row_max = [jnp.max(mt_scr[m], axis=-1, keepdims=True) for m in range(n_maps)]

    lt_scr[...] = jnp.zeros(lt_scr.shape, F32)
    acc_scr[...] = jnp.zeros(acc_scr.shape, F32)

    def accumulate(j, _):
        vb = v_ref[pl.ds(pl.multiple_of(j * bk, bk), bk), :]
        for m in range(n_maps):
            p = jnp.exp(s_scr[m, j] - row_max[m])
            lt_scr[m] = lt_scr[m] + fold(p, jnp.add)
            acc_scr[m] = acc_scr[m] + _dot(p.astype(BF16), vb)
        return 0

    lax.fori_loop(0, qi + 1, accumulate, 0)

    lam = (jnp.exp(jnp.sum(lq1_ref[...] * lk1_ref[...], axis=-1, keepdims=True))
           - jnp.exp(jnp.sum(lq2_ref[...] * lk2_ref[...], axis=-1, keepdims=True)) + LAM_INIT)
    l1 = jnp.sum(lt_scr[0], axis=-1, keepdims=True)
    l2 = jnp.sum(lt_scr[1], axis=-1, keepdims=True)
    o = acc_scr[0] / l1 - lam * (acc_scr[1] / l2)
    o_ref[...] = (_rms(o, sw_ref[...], SUBLN_EPS) * (1.0 - LAM_INIT)).astype(BF16)


def _attn(q, k, v, lam_q1, lam_k1, lam_q2, lam_k2, subln_w, bsz, seq):
    nq = seq // ATT_BQ
    qspec = pl.BlockSpec((ATT_BQ, LANES), lambda b, h, i: (b * nq + i, h))
    kvspec = pl.BlockSpec((seq, LANES), lambda b, h, i: (b, h))
    full = lambda a: pl.BlockSpec(a.shape, lambda b, h, i: (0, 0))
    lams = [a.reshape(1, -1) for a in (lam_q1, lam_k1, lam_q2, lam_k2)]
    sw = subln_w.reshape(1, -1)
    return pl.pallas_call(
        _attn_kernel,
        grid=(bsz, ATT_HEADS, nq),
        in_specs=[qspec, kvspec, kvspec] + [full(a) for a in lams] + [full(sw)],
        out_specs=qspec,
        out_shape=jax.ShapeDtypeStruct((bsz * seq, ATT_WIDTH), BF16),
        scratch_shapes=[
            pltpu.VMEM((2, seq // ATT_BK, ATT_BQ, ATT_BK), F32),
            pltpu.VMEM((2, ATT_BQ, LANES), F32),
            pltpu.VMEM((2, ATT_BQ, LANES), F32),
            pltpu.VMEM((2, ATT_BQ, LANES), F32),
        ],
        compiler_params=_cparams(("arbitrary", "arbitrary", "arbitrary")),
        name="attn",
    )(q, k, v, *lams, sw)


def _mixer(x2, g_mix, w_in, conv_w, conv_b, dt_bias, a_log, d_skip, ssd_norm_w,
           lam_q1, lam_k1, lam_q2, lam_k2, subln_w, bsz, seq):
    z, xbc, dtr, q, k, v = _in_proj(x2, g_mix.reshape(1, -1), w_in)
    y_ssd = _ssd(xbc, dtr, z, conv_w, conv_b, dt_bias, a_log, d_skip, ssd_norm_w, bsz, seq)
    y_att = _attn(q, k, v, lam_q1, lam_k1, lam_q2, lam_k2, subln_w, bsz, seq)
    return y_ssd, y_att


def _out_proj_kernel(x_ref, ys_ref, ya_ref, wos_ref, woa_ref, g_ref, wrh_ref, wrm_ref, wrl_ref, br_ref,
                     x1_ref, xn_ref, route_ref, gate_ref, cnt_ref):
    tm = TM_ROUTE
    x1 = x_ref[...] + _dot(ys_ref[...], wos_ref[...]) + _dot(ya_ref[...], woa_ref[...])
    x1_ref[...] = x1
    xn = _rms(x1, g_ref[...], NORM_EPS)
    xn_ref[...] = xn.astype(BF16)

    xh, xm, xl = _split3(xn)
    wh, wm, wl = wrh_ref[...], wrm_ref[...], wrl_ref[...]
    logits = (_dot_nt(wh, xh) + _dot_nt(wm, xh) + _dot_nt(wl, xh)
              + _dot_nt(wh, xm) + _dot_nt(wm, xm) + _dot_nt(wh, xl)) + br_ref[...]

    eidx = lax.broadcasted_iota(jnp.int32, (N_EXPERTS, tm), 0).astype(F32)
    work = logits
    vals, idxs, hots = [], [], []
    for _ in range(TOP_K):
        m = jnp.max(work, axis=0, keepdims=True)
        idx = jnp.min(jnp.where(work == m, eidx, float(N_EXPERTS)), axis=0, keepdims=True)
        hot = eidx == idx
        vals.append(m)
        idxs.append(idx.astype(jnp.int32))
        hots.append(hot)
        work = jnp.where(hot, -jnp.inf, work)
    exps = [jnp.exp(v - vals[0]) for v in vals]
    denom = exps[0] + exps[1] + exps[2] + exps[3]
    gates = [e / denom for e in exps]

    cnt = jnp.zeros((N_EXPERTS, tm), F32)
    for hot in hots:
        cnt = cnt + jnp.where(hot, 1.0, 0.0)
    cnt_b = cnt.astype(BF16)
    r = lax.broadcasted_iota(jnp.int32, (tm, tm), 0)
    c = lax.broadcasted_iota(jnp.int32, (tm, tm), 1)
    earlier_tok = jnp.where(r < c, 1.0, 0.0).astype(BF16)
    er = lax.broadcasted_iota(jnp.int32, (N_EXPERTS, N_EXPERTS), 0)
    ec = lax.broadcasted_iota(jnp.int32, (N_EXPERTS, N_EXPERTS), 1)
    lower_exp = jnp.where(ec < er, 1.0, 0.0).astype(BF16)
    prefix = _dot(cnt_b, earlier_tok)
    lstart = jnp.sum(_dot(lower_exp, cnt_b), axis=1, keepdims=True)
    pos = prefix + lstart
    lps = [jnp.sum(jnp.where(hot, pos, 0.0), axis=0, keepdims=True).astype(jnp.int32) for hot in hots]

    route_ref[...] = jnp.concatenate(idxs + lps, axis=0)
    gate_ref[...] = jnp.concatenate(gates + [jnp.zeros((SUBLANES - TOP_K, tm), F32)], axis=0)
    cnt_pad = jnp.concatenate([cnt_b, jnp.zeros((LANES - N_EXPERTS, tm), BF16)], axis=0)
    cnt_ref[0] = _dot_nt(jnp.ones((SUBLANES, tm), BF16), cnt_pad)


def _out_proj(x2, y_ssd, y_att, w_out, g_ffn, w_router, b_router):
    t = x2.shape[0]
    tm = TM_ROUTE
    nt = t // tm
    wo = w_out.astype(BF16)
    wos, woa = wo[:SSD_WIDTH], wo[SSD_WIDTH:]
    wrh, wrm, wrl = _split3(w_router.T)
    br = b_router.reshape(N_EXPERTS, 1)
    row = lambda n: pl.BlockSpec((tm, n), lambda i: (i, 0))
    col = pl.BlockSpec((SUBLANES, tm), lambda i: (0, i))
    full = lambda a: pl.BlockSpec(a.shape, lambda i: (0, 0))
    args = (x2, y_ssd, y_att, wos, woa, g_ffn.reshape(1, -1), wrh, wrm, wrl, br)
    return pl.pallas_call(
        _out_proj_kernel,
        grid=(nt,),
        in_specs=[row(D_MODEL), row(SSD_WIDTH), row(ATT_WIDTH)] + [full(a) for a in args[3:]],
        out_specs=[row(D_MODEL), row(D_MODEL), col, col, pl.BlockSpec((1, SUBLANES, LANES), lambda i: (i, 0, 0))],
        out_shape=[
            jax.ShapeDtypeStruct((t, D_MODEL), F32),
            jax.ShapeDtypeStruct((t, D_MODEL), BF16),
            jax.ShapeDtypeStruct((SUBLANES, t), jnp.int32),
            jax.ShapeDtypeStruct((SUBLANES, t), F32),
            jax.ShapeDtypeStruct((nt, SUBLANES, LANES), F32),
        ],
        compiler_params=_cparams(("arbitrary",)),
        name="out_proj",
    )(*args)


def _for_each_run_piece(n, piece):
    shift = RUN_CHUNK.bit_length() - 1
    n_full = lax.shift_right_logical(n, shift)

    def body(j, _):
        piece(j * RUN_CHUNK, RUN_CHUNK)
        return 0

    lax.fori_loop(0, n_full, body, 0)
    off = n_full * RUN_CHUNK
    b = RUN_CHUNK // 2
    while b >= 1:
        @pl.when((n & b) != 0)
        def _(off=off, b=b):
            piece(off, b)

        off = off + (n & b)
        b //= 2


def _one_hot_positions(route_ref, n_pos, tm):
    r = lax.broadcasted_iota(jnp.int32, (n_pos, tm), 0)
    return [r == route_ref[TOP_K + k:TOP_K + k + 1, :] for k in range(TOP_K)]


def _dispatch_kernel(cnt_ref, ls_ref, base_ref, zoff_ref, xn_ref, route_ref, xs_hbm, xloc, zeros_vmem, sem_z, sems):
    tm = TM_ROUTE
    n_pos = TOP_K * tm
    i = pl.program_id(0)
    n_steps = pl.num_programs(0)
    slot = lax.rem(i, 2)

    @pl.when(i == 0)
    def _():
        zeros_vmem[...] = jnp.zeros_like(zeros_vmem)

        def zfill(e, _):
            pltpu.make_async_copy(zeros_vmem, xs_hbm.at[_row_slice(zoff_ref[e], ROW_BLK)], sem_z).start()
            return 0

        def zwait(e, _):
            pltpu.make_async_copy(zeros_vmem, xs_hbm.at[_row_slice(0, ROW_BLK)], sem_z).wait()
            return 0

        def tfill(b, _):
            pltpu.make_async_copy(zeros_vmem, xs_hbm.at[_row_slice(b * ROW_BLK, ROW_BLK)], sem_z).start()
            return 0

        lax.fori_loop(0, N_EXPERTS, zfill, 0)
        lax.fori_loop(0, N_EXPERTS, zwait, 0)
        n_blk_total = xs_hbm.shape[0] // (ROW_BLK * ROW_SUB)
        lax.fori_loop(zoff_ref[N_EXPERTS], n_blk_total, tfill, 0)
        lax.fori_loop(zoff_ref[N_EXPERTS], n_blk_total, zwait, 0)

    masks = _one_hot_positions(route_ref, n_pos, tm)
    sel = jnp.where(masks[0] | masks[1] | masks[2] | masks[3], 1.0, 0.0).astype(BF16)
    rows = _dot(sel, xn_ref[...])

    def slot_wait(s):
        pltpu.make_async_copy(xloc.at[s], xs_hbm.at[_row_slice(0, n_pos)], sems.at[s]).wait()

    @pl.when(i >= 2)
    def _():
        slot_wait(slot)

    _store_rows(xloc.at[slot], rows)

    def per_expert(e, _):
        idx = i * N_EXPERTS + e
        src0 = ls_ref[idx]
        dst0 = base_ref[idx]

        def piece(off, size):
            pltpu.make_async_copy(xloc.at[slot, _row_slice(src0 + off, size)],
                                  xs_hbm.at[_row_slice(dst0 + off, size)], sems.at[slot]).start()

        _for_each_run_piece(cnt_ref[idx], piece)
        return 0

    lax.fori_loop(0, N_EXPERTS, per_expert, 0)

    @pl.when(i == n_steps - 1)
    def _():
        slot_wait(slot)

        @pl.when(n_steps > 1)
        def _():
            slot_wait(1 - slot)


def _dispatch(xn, route, tables, n_rows):
    t = xn.shape[0]
    tm = TM_ROUTE
    cnt_tbl, ls_tbl, base_tbl, zoff = tables
    return pl.pallas_call(
        _dispatch_kernel,
        grid_spec=pltpu.PrefetchScalarGridSpec(
            num_scalar_prefetch=4,
            grid=(t // tm,),
            in_specs=[pl.BlockSpec((tm, D_MODEL), lambda i, *_: (i, 0)),
                      pl.BlockSpec((SUBLANES, tm), lambda i, *_: (0, i))],
            out_specs=pl.BlockSpec(memory_space=pl.ANY),
            scratch_shapes=[
                pltpu.VMEM((2, TOP_K * tm * ROW_SUB, LANES), F32),
                pltpu.VMEM((ROW_BLK * ROW_SUB, LANES), F32),
                pltpu.SemaphoreType.DMA,
                pltpu.SemaphoreType.DMA((2,)),
            ],
        ),
        out_shape=jax.ShapeDtypeStruct(((n_rows + ROW_BLK) * ROW_SUB, LANES), F32),
        compiler_params=_cparams(("arbitrary",)),
        name="dispatch",
    )(cnt_tbl, ls_tbl, base_tbl, zoff, xn, route)


def _experts_kernel(blk_e_ref, nvalid_ref, xs_ref, wup_ref, wdn_ref, bg_ref, bu_ref, bd_ref, ys_ref,
                    wg_s, wu_s, wd_s):
    i = pl.program_id(0)
    e = blk_e_ref[i]
    e_prev = blk_e_ref[jnp.maximum(i - 1, 0)]

    @pl.when((i == 0) | (e != e_prev))
    def _():
        src = lax.broadcasted_iota(jnp.int32, (DEINT, DEINT), 0)
        dst = lax.broadcasted_iota(jnp.int32, (DEINT, DEINT), 1)
        perm = jnp.where(src == jnp.where(dst < LANES, 2 * dst, 2 * (dst - LANES) + 1), 1.0, 0.0).astype(BF16)
        for g in range(2 * D_EXPERT // DEINT):
            sep = _dot(wup_ref[0, :, g * DEINT:(g + 1) * DEINT].astype(BF16), perm)
            wg_s[:, g * LANES:(g + 1) * LANES] = sep[:, :LANES].astype(BF16)
            wu_s[:, g * LANES:(g + 1) * LANES] = sep[:, LANES:].astype(BF16)
        wd_s[...] = wdn_ref[0].astype(BF16)

    @pl.when(i < nvalid_ref[0])
    def _():
        xb = _load_rows(xs_ref).astype(BF16)
        gate = jnp.minimum(_dot(xb, wg_s[...]) + bg_ref[0], SWIGLU_LIMIT)
        up = jnp.clip(_dot(xb, wu_s[...]) + bu_ref[0], -SWIGLU_LIMIT, SWIGLU_LIMIT)
        act = (up + 1.0) * gate * jax.nn.sigmoid(SWIGLU_ALPHA * gate)
        _store_rows(ys_ref, _dot(act.astype(BF16), wd_s[...]) + bd_ref[0])

    @pl.when(i >= nvalid_ref[0])
    def _():
        ys_ref[...] = jnp.zeros_like(ys_ref)


def _experts(xs, blk_e, nvalid, w_up, b_up, w_down, b_down, n_rows):
    nb = n_rows // ROW_BLK
    bg = b_up[:, 0::2].reshape(N_EXPERTS, 1, D_EXPERT)
    bu = b_up[:, 1::2].reshape(N_EXPERTS, 1, D_EXPERT)
    bd = b_down.reshape(N_EXPERTS, 1, D_MODEL)
    src = lambda i, be, nv: (jnp.minimum(i, nv[0] - 1), 0)
    wspec = lambda n, m: pl.BlockSpec((1, n, m), lambda i, be, nv: (be[i], 0, 0))
    return pl.pallas_call(
        _experts_kernel,
        grid_spec=pltpu.PrefetchScalarGridSpec(
            num_scalar_prefetch=2,
            grid=(nb,),
            in_specs=[_rows_spec(ROW_BLK, src),
                      wspec(D_MODEL, 2 * D_EXPERT), wspec(D_EXPERT, D_MODEL),
                      wspec(1, D_EXPERT), wspec(1, D_EXPERT), wspec(1, D_MODEL)],
            out_specs=_rows_spec(ROW_BLK, lambda i, be, nv: (i, 0)),
            scratch_shapes=[
                pltpu.VMEM((D_MODEL, D_EXPERT), BF16),
                pltpu.VMEM((D_MODEL, D_EXPERT), BF16),
                pltpu.VMEM((D_EXPERT, D_MODEL), BF16),
            ],
        ),
        out_shape=jax.ShapeDtypeStruct((n_rows * ROW_SUB, LANES), F32),
        compiler_params=_cparams(("arbitrary",)),
        name="experts",
    )(blk_e, nvalid, xs, w_up, w_down, bg, bu, bd)


def _combine_kernel(cnt_ref, ls_ref, base_ref, ys_hbm, route_ref, gate_ref, x1_ref, p_ref, gp_ref, wpg_ref,
                    wpp_ref, gf_ref, o_ref, yloc, sems):
    tm = TM_ROUTE
    n_pos = TOP_K * tm
    i = pl.program_id(0)
    n_steps = pl.num_programs(0)
    slot = lax.rem(i, 2)

    def gather_tile(tile, s):
        def per_expert(e, _):
            idx = tile * N_EXPERTS + e
            src0 = base_ref[idx]
            dst0 = ls_ref[idx]

            def piece(off, size):
                pltpu.make_async_copy(ys_hbm.at[_row_slice(src0 + off, size)],
                                      yloc.at[s, _row_slice(dst0 + off, size)], sems.at[s]).start()

            _for_each_run_piece(cnt_ref[idx], piece)
            return 0

        lax.fori_loop(0, N_EXPERTS, per_expert, 0)

    @pl.when(i == 0)
    def _():
        gather_tile(0, 0)

    @pl.when(i + 1 < n_steps)
    def _():
        gather_tile(i + 1, 1 - slot)

    masks = _one_hot_positions(route_ref, n_pos, tm)
    sel = jnp.where(masks[0] | masks[1] | masks[2] | masks[3], 1.0, 0.0).astype(BF16)
    gsel = jnp.zeros((n_pos, tm), F32)
    for k in range(TOP_K):
        gsel = gsel + jnp.where(masks[k], gate_ref[k:k + 1, :], 0.0)
    g_pos = jnp.sum(gsel, axis=1, keepdims=True)
    pp = _dot(p_ref[...].astype(BF16), wpp_ref[...])

    pltpu.make_async_copy(ys_hbm.at[_row_slice(0, n_pos)], yloc.at[slot], sems.at[slot]).wait()
    y_gated = (_load_rows(yloc.at[slot]) * g_pos).astype(BF16)
    x2 = x1_ref[...] + _dot_tn(sel, y_gated)
    xn = _rms(x2, gp_ref[...], NORM_EPS).astype(BF16)
    x3 = x2 + pp * jax.nn.sigmoid(_dot(xn, wpg_ref[...]))
    o_ref[...] = _rms(x3, gf_ref[...], NORM_EPS)


def _combine(ys, route, gate_t, x1, p2, tables, g_ple, w_ple_gate, w_ple_proj, g_final):
    t = x1.shape[0]
    tm = TM_ROUTE
    cnt_tbl, ls_tbl, base_tbl, _ = tables
    row = lambda n: pl.BlockSpec((tm, n), lambda i, *_: (i, 0))
    col = pl.BlockSpec((SUBLANES, tm), lambda i, *_: (0, i))
    full = lambda a: pl.BlockSpec(a.shape, lambda i, *_: (0, 0))
    consts = (g_ple.reshape(1, -1), w_ple_gate.astype(BF16), w_ple_proj.astype(BF16), g_final.reshape(1, -1))
    return pl.pallas_call(
        _combine_kernel,
        grid_spec=pltpu.PrefetchScalarGridSpec(
            num_scalar_prefetch=3,
            grid=(t // tm,),
            in_specs=[pl.BlockSpec(memory_space=pl.ANY), col, col, row(D_MODEL), row(PLE_DIM)]
                     + [full(a) for a in consts],
            out_specs=row(D_MODEL),
            scratch_shapes=[
                pltpu.VMEM((2, TOP_K * tm * ROW_SUB, LANES), F32),
                pltpu.SemaphoreType.DMA((2,)),
            ],
        ),
        out_shape=jax.ShapeDtypeStruct((t, D_MODEL), F32),
        compiler_params=_cparams(("arbitrary",)),
        name="combine",
    )(cnt_tbl, ls_tbl, base_tbl, ys, route, gate_t, x1, p2, *consts)


def _routing_tables(cnt, n_blocks):
    tile_cnt = cnt[:, 0, :N_EXPERTS].astype(jnp.int32)
    counts = jnp.sum(tile_cnt, axis=0)
    padded = (counts + ROW_BLK - 1) // ROW_BLK * ROW_BLK
    pend = jnp.cumsum(padded)
    pstart = pend - padded
    base = pstart[None, :] + jnp.cumsum(tile_cnt, axis=0) - tile_cnt
    lstart = jnp.cumsum(tile_cnt, axis=1) - tile_cnt
    nvalid = pend[-1:] // ROW_BLK
    zoff = jnp.concatenate([pstart + counts, nvalid])
    blk_start = jnp.minimum(jnp.arange(n_blocks, dtype=jnp.int32) * ROW_BLK, pend[-1] - 1)
    blk_e = jnp.minimum(jnp.sum((pend[None, :] <= blk_start[:, None]).astype(jnp.int32), axis=1), N_EXPERTS - 1)
    i32 = lambda a: a.reshape(-1).astype(jnp.int32)
    return (i32(tile_cnt), i32(lstart), i32(base), i32(zoff)), i32(blk_e), i32(nvalid)


def kernel(x, p, g_mix, w_in, conv_w, conv_b, dt_bias, a_log, d_skip, ssd_norm_w, lam_q1, lam_k1, lam_q2, lam_k2, subln_w, w_out, g_ffn, w_router, b_router, w_up, b_up, w_down, b_down, g_ple, w_ple_gate, w_ple_proj, g_final):
    bsz, seq, d = x.shape
    t = bsz * seq
    x2 = x.reshape(t, d)
    y_ssd, y_att = _mixer(x2, g_mix[0], w_in[0], conv_w[0], conv_b[0], dt_bias[0], a_log[0], d_skip[0], ssd_norm_w[0],
                          lam_q1[0], lam_k1[0], lam_q2[0], lam_k2[0], subln_w[0], bsz, seq)
    x1, xn, route, gate_t, cnt = _out_proj(x2, y_ssd, y_att, w_out[0], g_ffn[0], w_router[0], b_router[0])

    n_rows = t * TOP_K + N_EXPERTS * ROW_BLK
    tables, blk_e, nvalid = _routing_tables(cnt, n_rows // ROW_BLK)
    xs = _dispatch(xn, route, tables, n_rows)
    ys = _experts(xs, blk_e, nvalid, w_up[0], b_up[0], w_down[0], b_down[0], n_rows)
    out = _combine(ys, route, gate_t, x1, p[0].reshape(t, PLE_DIM), tables, g_ple[0], w_ple_gate[0], w_ple_proj[0],
                   g_final)
    return out.reshape(bsz, seq, d)
```

```python
import math

import jax
import jax.numpy as jnp
from jax import lax
from jax.experimental import pallas as pl
from jax.experimental.pallas import tpu as pltpu

F32 = jnp.float32
BF16 = jnp.bfloat16

D_MODEL = 1024
PLE_DIM = 256
SSD_WIDTH = 512
ATT_WIDTH = 512
SSD_HEAD_DIM = 64
SSD_HEADS = 8
SSD_GROUPS = 2
SSD_STATE = 128
SSD_CONV = 4
SSD_CHUNK = 128
SSD_CONV_CH = SSD_WIDTH + 2 * SSD_GROUPS * SSD_STATE
SSD_NORM_EPS = 1e-5
ATT_HEAD_DIM = 64
ATT_HEADS = 4
SUBLN_EPS = 1e-5
OFF_Z = 0
OFF_XBC = OFF_Z + SSD_WIDTH
OFF_DT = OFF_XBC + SSD_CONV_CH
OFF_Q = OFF_DT + SSD_HEADS
OFF_K = OFF_Q + ATT_WIDTH
OFF_V = OFF_K + ATT_WIDTH
IN_PROJ = OFF_V + ATT_WIDTH
N_EXPERTS = 32
TOP_K = 4
D_EXPERT = 1024
SWIGLU_LIMIT = 7.0
SWIGLU_ALPHA = 1.702
NORM_EPS = 1e-6
LAM_INIT = 0.8 - 0.6 * math.exp(-0.3 * 0)

LANES = 128
SUBLANES = 8
VMEM_LIMIT_BYTES = 56 * 1024 * 1024
ROW_SUB = D_MODEL // LANES

TM_PROJ = 512
ATT_BQ = 512
ATT_BK = 512
ROW_BLK = 256
TM_ROUTE = 256
OUT_SUB = 2
RUN_CHUNK = 32
DEINT = 2 * LANES


def _cparams(sem):
    return pltpu.CompilerParams(dimension_semantics=sem, vmem_limit_bytes=VMEM_LIMIT_BYTES)


def _rms(x, w, eps):
    return x * lax.rsqrt(jnp.mean(x * x, axis=-1, keepdims=True) + eps) * w


def _dot(a, b):
    return jnp.dot(a, b, preferred_element_type=F32)


def _dot_nt(a, b):
    return lax.dot_general(a, b, (((1,), (1,)), ((), ())), preferred_element_type=F32)


def _dot_tn(a, b):
    return lax.dot_general(a, b, (((0,), (0,)), ((), ())), preferred_element_type=F32)


def _store_rows(ref, val):
    n = val.shape[0]
    for s in range(ROW_SUB):
        ref[pl.ds(s, n, stride=ROW_SUB), :] = val[:, s * LANES:(s + 1) * LANES]


def _load_rows(ref):
    n = ref.shape[0] // ROW_SUB
    return jnp.concatenate([ref[pl.ds(s, n, stride=ROW_SUB), :] for s in range(ROW_SUB)], axis=-1)


def _rows_spec(n, index_map):
    return pl.BlockSpec((n * ROW_SUB, LANES), index_map)


def _row_slice(start, n):
    return pl.ds(pl.multiple_of(start * ROW_SUB, ROW_SUB), n * ROW_SUB)


def _split3(x):
    hi = x.astype(BF16)
    r1 = x - hi.astype(F32)
    mid = r1.astype(BF16)
    lo = (r1 - mid.astype(F32)).astype(BF16)
    return hi, mid, lo


def _in_proj_kernel(x_ref, g_ref, wz_ref, wxbc_ref, wdt_ref, wq_ref, wk_ref, wv_ref,
                    z_ref, xbc_ref, dt_ref, q_ref, k_ref, v_ref):
    h = _rms(x_ref[...], g_ref[...], NORM_EPS).astype(BF16)
    z_ref[...] = _dot(h, wz_ref[...])
    xbc_ref[...] = _dot(h, wxbc_ref[...])
    dt_ref[...] = _dot(h, wdt_ref[...])
    q_ref[...] = (_dot(h, wq_ref[...]) * (ATT_HEAD_DIM ** -0.5)).astype(BF16)
    k_ref[...] = _dot(h, wk_ref[...]).astype(BF16)
    v_ref[...] = _dot(h, wv_ref[...]).astype(BF16)


def _in_proj(x2, g_mix, w_in):
    t = x2.shape[0]
    wb = w_in.astype(BF16)
    wz = wb[:, OFF_Z:OFF_XBC]
    wxbc = wb[:, OFF_XBC:OFF_DT]
    wdt = jnp.pad(wb[:, OFF_DT:OFF_Q], ((0, 0), (0, LANES - SSD_HEADS)))
    wq = wb[:, OFF_Q:OFF_K]
    wk = wb[:, OFF_K:OFF_V]
    wv = wb[:, OFF_V:IN_PROJ]
    tm = TM_PROJ
    row = lambda n: pl.BlockSpec((tm, n), lambda i: (i, 0))
    full = lambda a: pl.BlockSpec(a.shape, lambda i: (0, 0))
    return pl.pallas_call(
        _in_proj_kernel,
        grid=(t // tm,),
        in_specs=[row(D_MODEL), full(g_mix), full(wz), full(wxbc), full(wdt), full(wq), full(wk), full(wv)],
        out_specs=[row(SSD_WIDTH), row(SSD_CONV_CH), row(LANES), row(ATT_WIDTH), row(ATT_WIDTH), row(ATT_WIDTH)],
        out_shape=[
            jax.ShapeDtypeStruct((t, SSD_WIDTH), F32),
            jax.ShapeDtypeStruct((t, SSD_CONV_CH), F32),
            jax.ShapeDtypeStruct((t, LANES), F32),
            jax.ShapeDtypeStruct((t, ATT_WIDTH), BF16),
            jax.ShapeDtypeStruct((t, ATT_WIDTH), BF16),
            jax.ShapeDtypeStruct((t, ATT_WIDTH), BF16),
        ],
        compiler_params=_cparams(("arbitrary",)),
        name="in_proj",
    )(x2, g_mix, wz, wxbc, wdt, wq, wk, wv)


def _ssd_kernel(xbc_ref, dtr_ref, z_ref, cw_ref, cb_ref, dtb_ref, alog_ref, dskip_ref, nw_ref,
                y_ref, xpad_ref, state_ref):
    L = SSD_CHUNK
    c = pl.program_id(1)

    @pl.when(c == 0)
    def _():
        xpad_ref[0:SUBLANES, :] = jnp.zeros((SUBLANES, SSD_CONV_CH), F32)
        state_ref[...] = jnp.zeros_like(state_ref)

    @pl.when(c != 0)
    def _():
        xpad_ref[0:SUBLANES, :] = xpad_ref[L:L + SUBLANES, :]

    xpad_ref[SUBLANES:SUBLANES + L, :] = xbc_ref[...]

    conv = cb_ref[...]
    for j in range(SSD_CONV):
        off = SUBLANES - (SSD_CONV - 1) + j
        conv = conv + cw_ref[j:j + 1, :] * xpad_ref[off:off + L, :]
    act = conv * jax.nn.sigmoid(conv)
    xs = act[:, :SSD_WIDTH]
    bm = act[:, SSD_WIDTH:SSD_WIDTH + SSD_GROUPS * SSD_STATE].astype(BF16)
    cm = act[:, SSD_WIDTH + SSD_GROUPS * SSD_STATE:].astype(BF16)

    dt_in = dtr_ref[...] + dtb_ref[...]
    dt = jnp.maximum(dt_in, 0.0) + jnp.log1p(jnp.exp(-jnp.abs(dt_in)))
    adt = dt * (-jnp.exp(alog_ref[...]))

    ri = lax.broadcasted_iota(jnp.int32, (L, L), 0)
    ci = lax.broadcasted_iota(jnp.int32, (L, L), 1)
    causal = ci <= ri
    tril = jnp.where(causal, 1.0, 0.0).astype(BF16)
    hi, mid, lo = _split3(adt)
    acum = _dot(tril, hi) + _dot(tril, mid) + _dot(tril, lo)
    acum_t = acum.T
    a_last = acum[L - 1:L, :]
    decay_in = jnp.exp(a_last - acum)
    decay_out = jnp.exp(acum)
    chunk_decay = jnp.exp(a_last)

    lane = lax.broadcasted_iota(jnp.int32, (L, LANES), 1)
    lo_half = lane < SSD_HEAD_DIM

    def per_pair(col_a, col_b):
        return jnp.where(lo_half, col_a, col_b)

    ys = []
    for pair in range(SSD_HEADS // 2):
        g = pair // 2
        h0, h1 = 2 * pair, 2 * pair + 1
        cg = cm[:, g * SSD_STATE:(g + 1) * SSD_STATE]
        bg = bm[:, g * SSD_STATE:(g + 1) * SSD_STATE]
        cb = _dot_nt(cg, bg)
        x_pair = xs[:, pair * LANES:(pair + 1) * LANES]
        xdt = x_pair * per_pair(dt[:, h0:h0 + 1], dt[:, h1:h1 + 1])
        y_pair = jnp.zeros((L, LANES), F32)
        for hh, keep in ((h0, lo_half), (h1, jnp.logical_not(lo_half))):
            seg = acum[:, hh:hh + 1] - acum_t[hh:hh + 1, :]
            lmat = jnp.where(causal, jnp.exp(jnp.where(causal, seg, 0.0)), 0.0)
            m = (cb * lmat).astype(BF16)
            y_pair = y_pair + _dot(m, jnp.where(keep, xdt, 0.0).astype(BF16))
        s_prev = state_ref[pair]
        y_off = _dot(cg, s_prev.astype(BF16)) * per_pair(decay_out[:, h0:h0 + 1], decay_out[:, h1:h1 + 1])
        w_in = (xdt * per_pair(decay_in[:, h0:h0 + 1], decay_in[:, h1:h1 + 1])).astype(BF16)
        cd = jnp.where(lane[0:1, :] < SSD_HEAD_DIM, chunk_decay[:, h0:h0 + 1], chunk_decay[:, h1:h1 + 1])
        state_ref[pair] = s_prev * cd + _dot_tn(bg, w_in)
        ys.append(y_pair + y_off + dskip_ref[:, pair * LANES:(pair + 1) * LANES] * x_pair)

    y = jnp.concatenate(ys, axis=-1)
    zz = z_ref[...]
    y = y * (zz * jax.nn.sigmoid(zz))
    gw = SSD_WIDTH // SSD_GROUPS
    outs = []
    for g in range(SSD_GROUPS):
        yg = y[:, g * gw:(g + 1) * gw]
        outs.append(yg * lax.rsqrt(jnp.mean(yg * yg, axis=-1, keepdims=True) + SSD_NORM_EPS))
    y_ref[...] = (jnp.concatenate(outs, axis=-1) * nw_ref[...]).astype(BF16)


def _ssd(xbc, dtr, z, conv_w, conv_b, dt_bias, a_log, d_skip, norm_w, bsz, seq):
    nc = seq // SSD_CHUNK
    pad_h = lambda v: jnp.pad(v.reshape(1, SSD_HEADS), ((0, 0), (0, LANES - SSD_HEADS)))
    dskip_lanes = jnp.repeat(d_skip, SSD_HEAD_DIM).reshape(1, SSD_WIDTH)
    row = lambda n: pl.BlockSpec((SSD_CHUNK, n), lambda b, c: (b * nc + c, 0))
    full = lambda a: pl.BlockSpec(a.shape, lambda b, c: (0, 0))
    args = (xbc, dtr, z, conv_w, conv_b.reshape(1, -1), pad_h(dt_bias), pad_h(a_log), dskip_lanes,
            norm_w.reshape(1, -1))
    return pl.pallas_call(
        _ssd_kernel,
        grid=(bsz, nc),
        in_specs=[row(SSD_CONV_CH), row(LANES), row(SSD_WIDTH)] + [full(a) for a in args[3:]],
        out_specs=row(SSD_WIDTH),
        out_shape=jax.ShapeDtypeStruct((bsz * seq, SSD_WIDTH), BF16),
        scratch_shapes=[
            pltpu.VMEM((SSD_CHUNK + 2 * SUBLANES, SSD_CONV_CH), F32),
            pltpu.VMEM((SSD_HEADS // 2, SSD_STATE, LANES), F32),
        ],
        compiler_params=_cparams(("arbitrary", "arbitrary")),
        name="ssd",
    )(*args)


def _attn_kernel(q_ref, k_ref, v_ref, lq1_ref, lk1_ref, lq2_ref, lk2_ref, sw_ref, o_ref,
                 s_scr, mt_scr, lt_scr, acc_scr):
    bq, bk = ATT_BQ, ATT_BK
    qi = pl.program_id(2)
    q = q_ref[...]
    lane = lax.broadcasted_iota(jnp.int32, (bq, LANES), 1)
    zero = jnp.zeros_like(q)
    q_maps = (jnp.where(lane < ATT_HEAD_DIM, q, zero), jnp.where(lane >= ATT_HEAD_DIM, q, zero))
    n_maps = len(q_maps)

    def fold(t, op):
        out = t[:, :LANES]
        for c in range(1, bk // LANES):
            out = op(out, t[:, c * LANES:(c + 1) * LANES])
        return out

    def scores(j, mask):
        kb = k_ref[pl.ds(pl.multiple_of(j * bk, bk), bk), :]
        for m in range(n_maps):
            s = _dot_nt(q_maps[m], kb)
            if mask is not None:
                s = jnp.where(mask, s, -jnp.inf)
            s_scr[m, j] = s
            mt_scr[m] = jnp.maximum(mt_scr[m], fold(s, jnp.maximum))

    def scores_body(j, _):
        scores(j, None)
        return 0

    mt_scr[...] = jnp.full(mt_scr.shape, -jnp.inf, F32)
    lax.fori_loop(0, qi, scores_body, 0)
    r = lax.broadcasted_iota(jnp.int32, (bq, bk), 0)
    cidx = lax.broadcasted_iota(jnp.int32, (bq, bk), 1)
    scores(qi, cidx <= r)
    row_max = [jnp.max(mt_scr[m], axis=-1, keepdims=True) for m in range(n_maps)]

    lt_scr[...] = jnp.zeros(lt_scr.shape, F32)
    acc_scr[...] = jnp.zeros(acc_scr.shape, F32)

    def accumulate(j, _):
        vb = v_ref[pl.ds(pl.multiple_of(j * bk, bk), bk), :]
        for m in range(n_maps):
            p = jnp.exp(s_scr[m, j] - row_max[m])
            lt_scr[m] = lt_scr[m] + fold(p, jnp.add)
            acc_scr[m] = acc_scr[m] + _dot(p.astype(BF16), vb)
        return 0

    lax.fori_loop(0, qi + 1, accumulate, 0)

    lam = (jnp.exp(jnp.sum(lq1_ref[...] * lk1_ref[...], axis=-1, keepdims=True))
           - jnp.exp(jnp.sum(lq2_ref[...] * lk2_ref[...], axis=-1, keepdims=True)) + LAM_INIT)
    l1 = jnp.sum(lt_scr[0], axis=-1, keepdims=True)
    l2 = jnp.sum(lt_scr[1], axis=-1, keepdims=True)
    o = acc_scr[0] / l1 - lam * (acc_scr[1] / l2)
    o_ref[...] = (_rms(o, sw_ref[...], SUBLN_EPS) * (1.0 - LAM_INIT)).astype(BF16)


def _attn(q, k, v, lam_q1, lam_k1, lam_q2, lam_k2, subln_w, bsz, seq):
    nq = seq // ATT_BQ
    qspec = pl.BlockSpec((ATT_BQ, LANES), lambda b, h, i: (b * nq + i, h))
    kvspec = pl.BlockSpec((seq, LANES), lambda b, h, i: (b, h))
    full = lambda a: pl.BlockSpec(a.shape, lambda b, h, i: (0, 0))
    lams = [a.reshape(1, -1) for a in (lam_q1, lam_k1, lam_q2, lam_k2)]
    sw = subln_w.reshape(1, -1)
    return pl.pallas_call(
        _attn_kernel,
        grid=(bsz, ATT_HEADS, nq),
        in_specs=[qspec, kvspec, kvspec] + [full(a) for a in lams] + [full(sw)],
        out_specs=qspec,
        out_shape=jax.ShapeDtypeStruct((bsz * seq, ATT_WIDTH), BF16),
        scratch_shapes=[
            pltpu.VMEM((2, seq // ATT_BK, ATT_BQ, ATT_BK), F32),
            pltpu.VMEM((2, ATT_BQ, LANES), F32),
            pltpu.VMEM((2, ATT_BQ, LANES), F32),
            pltpu.VMEM((2, ATT_BQ, LANES), F32),
        ],
        compiler_params=_cparams(("arbitrary", "arbitrary", "arbitrary")),
        name="attn",
    )(q, k, v, *lams, sw)


def _mixer(x2, g_mix, w_in, conv_w, conv_b, dt_bias, a_log, d_skip, ssd_norm_w,
           lam_q1, lam_k1, lam_q2, lam_k2, subln_w, bsz, seq):
    z, xbc, dtr, q, k, v = _in_proj(x2, g_mix.reshape(1, -1), w_in)
    y_ssd = _ssd(xbc, dtr, z, conv_w, conv_b, dt_bias, a_log, d_skip, ssd_norm_w, bsz, seq)
    y_att = _attn(q, k, v, lam_q1, lam_k1, lam_q2, lam_k2, subln_w, bsz, seq)
    return y_ssd, y_att


def _out_proj_kernel(x_ref, ys_ref, ya_ref, wo_ref, g_ref, wr2_ref, br_ref,
                     x1_ref, xn_ref, route_ref, gate_ref, cnt_ref):
    tm = TM_ROUTE
    x1 = x_ref[...] + _dot(jnp.concatenate([ys_ref[...], ya_ref[...]], axis=-1), wo_ref[...])
    x1_ref[...] = x1
    xn_all = _rms(x1, g_ref[...], NORM_EPS)
    xn_ref[...] = xn_all.astype(BF16)

    n_tok = OUT_SUB * tm
    xh, xm, _ = _split3(xn_all)
    lg2 = _dot(xh, wr2_ref[...]) + _dot(xm, wr2_ref[...])
    logits = (lg2[:, :LANES] + lg2[:, LANES:]).T[:N_EXPERTS, :] + br_ref[...]

    eidx = lax.broadcasted_iota(jnp.int32, (N_EXPERTS, n_tok), 0).astype(F32)
    work = logits
    vals, idxs, hots = [], [], []
    for _ in range(TOP_K):
        m = jnp.max(work, axis=0, keepdims=True)
        idx = jnp.min(jnp.where(work == m, eidx, float(N_EXPERTS)), axis=0, keepdims=True)
        hot = eidx == idx
        vals.append(m)
        idxs.append(idx.astype(jnp.int32))
        hots.append(hot)
        work = jnp.where(hot, -jnp.inf, work)
    exps = [jnp.exp(v - vals[0]) for v in vals]
    denom = exps[0] + exps[1] + exps[2] + exps[3]
    gates = [e / denom for e in exps]

    cnt = jnp.zeros((N_EXPERTS, n_tok), F32)
    for hot in hots:
        cnt = cnt + jnp.where(hot, 1.0, 0.0)
    cnt_b = cnt.astype(BF16)
    r = lax.broadcasted_iota(jnp.int32, (tm, tm), 0)
    c = lax.broadcasted_iota(jnp.int32, (tm, tm), 1)
    earlier_tok = jnp.where(r < c, 1.0, 0.0).astype(BF16)
    er = lax.broadcasted_iota(jnp.int32, (N_EXPERTS, N_EXPERTS), 0)
    ec = lax.broadcasted_iota(jnp.int32, (N_EXPERTS, N_EXPERTS), 1)
    lower_exp = jnp.where(ec < er, 1.0, 0.0).astype(BF16)
    below = _dot(lower_exp, cnt_b)
    cnt_pad = jnp.concatenate([cnt_b, jnp.zeros((LANES - N_EXPERTS, n_tok), BF16)], axis=0)
    pos = []
    for sub in range(OUT_SUB):
        cols = slice(sub * tm, (sub + 1) * tm)
        pos.append(_dot(cnt_b[:, cols], earlier_tok) + jnp.sum(below[:, cols], axis=1, keepdims=True))
        cnt_ref[sub] = _dot_nt(jnp.ones((SUBLANES, tm), BF16), cnt_pad[:, cols])
    pos = jnp.concatenate(pos, axis=1)
    lps = [jnp.sum(jnp.where(hot, pos, 0.0), axis=0, keepdims=True).astype(jnp.int32) for hot in hots]

    route_ref[...] = jnp.concatenate(idxs + lps, axis=0)
    gate_ref[...] = jnp.concatenate(gates + [jnp.zeros((SUBLANES - TOP_K, n_tok), F32)], axis=0)


def _out_proj(x2, y_ssd, y_att, w_out, g_ffn, w_router, b_router):
    t = x2.shape[0]
    tm = OUT_SUB * TM_ROUTE
    nt = t // TM_ROUTE
    wo = w_out.astype(BF16)
    wrh, wrm, _ = _split3(jnp.pad(w_router, ((0, 0), (0, LANES - N_EXPERTS))))
    wr2 = jnp.concatenate([wrh, wrm], axis=1)
    br = b_router.reshape(N_EXPERTS, 1)
    row = lambda n: pl.BlockSpec((tm, n), lambda i: (i, 0))
    col = pl.BlockSpec((SUBLANES, tm), lambda i: (0, i))
    full = lambda a: pl.BlockSpec(a.shape, lambda i: (0, 0))
    args = (x2, y_ssd, y_att, wo, g_ffn.reshape(1, -1), wr2, br)
    return pl.pallas_call(
        _out_proj_kernel,
        grid=(t // tm,),
        in_specs=[row(D_MODEL), row(SSD_WIDTH), row(ATT_WIDTH)] + [full(a) for a in args[3:]],
        out_specs=[row(D_MODEL), row(D_MODEL), col, col,
                   pl.BlockSpec((OUT_SUB, SUBLANES, LANES), lambda i: (i, 0, 0))],
        out_shape=[
            jax.ShapeDtypeStruct((t, D_MODEL), F32),
            jax.ShapeDtypeStruct((t, D_MODEL), BF16),
            jax.ShapeDtypeStruct((SUBLANES, t), jnp.int32),
            jax.ShapeDtypeStruct((SUBLANES, t), F32),
            jax.ShapeDtypeStruct((nt, SUBLANES, LANES), F32),
        ],
        compiler_params=_cparams(("arbitrary",)),
        name="out_proj",
    )(*args)


def _for_each_run_piece(n, piece):
    shift = RUN_CHUNK.bit_length() - 1
    n_full = lax.shift_right_logical(n, shift)

    def body(j, _):
        piece(j * RUN_CHUNK, RUN_CHUNK)
        return 0

    lax.fori_loop(0, n_full, body, 0)
    off = n_full * RUN_CHUNK
    b = RUN_CHUNK // 2
    while b >= 1:
        @pl.when((n & b) != 0)
        def _(off=off, b=b):
            piece(off, b)

        off = off + (n & b)
        b //= 2


def _one_hot_positions(route_ref, n_pos, tm):
    r = lax.broadcasted_iota(jnp.int32, (n_pos, tm), 0)
    return [r == route_ref[TOP_K + k:TOP_K + k + 1, :] for k in range(TOP_K)]


def _dispatch_kernel(cnt_ref, ls_ref, base_ref, zoff_ref, xn_ref, route_ref, xs_hbm, xloc, zeros_vmem, sem_z, sems):
    tm = TM_ROUTE
    n_pos = TOP_K * tm
    i = pl.program_id(0)
    n_steps = pl.num_programs(0)
    slot = lax.rem(i, 2)

    @pl.when(i == 0)
    def _():
        zeros_vmem[...] = jnp.zeros_like(zeros_vmem)

        def zfill(e, _):
            pltpu.make_async_copy(zeros_vmem, xs_hbm.at[_row_slice(zoff_ref[e], ROW_BLK)], sem_z).start()
            return 0

        def zwait(e, _):
            pltpu.make_async_copy(zeros_vmem, xs_hbm.at[_row_slice(0, ROW_BLK)], sem_z).wait()
            return 0

        def tfill(b, _):
            pltpu.make_async_copy(zeros_vmem, xs_hbm.at[_row_slice(b * ROW_BLK, ROW_BLK)], sem_z).start()
            return 0

        lax.fori_loop(0, N_EXPERTS, zfill, 0)
        lax.fori_loop(0, N_EXPERTS, zwait, 0)
        n_blk_total = xs_hbm.shape[0] // (ROW_BLK * ROW_SUB)
        lax.fori_loop(zoff_ref[N_EXPERTS], n_blk_total, tfill, 0)
        lax.fori_loop(zoff_ref[N_EXPERTS], n_blk_total, zwait, 0)

    masks = _one_hot_positions(route_ref, n_pos, tm)
    sel = jnp.where(masks[0] | masks[1] | masks[2] | masks[3], 1.0, 0.0).astype(BF16)
    rows = _dot(sel, xn_ref[...])

    def slot_wait(s):
        pltpu.make_async_copy(xloc.at[s], xs_hbm.at[_row_slice(0, n_pos)], sems.at[s]).wait()

    @pl.when(i >= 2)
    def _():
        slot_wait(slot)

    _store_rows(xloc.at[slot], rows)

    def per_expert(e, _):
        idx = i * N_EXPERTS + e
        src0 = ls_ref[idx]
        dst0 = base_ref[idx]

        def piece(off, size):
            pltpu.make_async_copy(xloc.at[slot, _row_slice(src0 + off, size)],
                                  xs_hbm.at[_row_slice(dst0 + off, size)], sems.at[slot]).start()

        _for_each_run_piece(cnt_ref[idx], piece)
        return 0

    lax.fori_loop(0, N_EXPERTS, per_expert, 0)

    @pl.when(i == n_steps - 1)
    def _():
        slot_wait(slot)

        @pl.when(n_steps > 1)
        def _():
            slot_wait(1 - slot)


def _dispatch(xn, route, tables, n_rows):
    t = xn.shape[0]
    tm = TM_ROUTE
    cnt_tbl, ls_tbl, base_tbl, zoff = tables
    return pl.pallas_call(
        _dispatch_kernel,
        grid_spec=pltpu.PrefetchScalarGridSpec(
            num_scalar_prefetch=4,
            grid=(t // tm,),
            in_specs=[pl.BlockSpec((tm, D_MODEL), lambda i, *_: (i, 0)),
                      pl.BlockSpec((SUBLANES, tm), lambda i, *_: (0, i))],
            out_specs=pl.BlockSpec(memory_space=pl.ANY),
            scratch_shapes=[
                pltpu.VMEM((2, TOP_K * tm * ROW_SUB, LANES), F32),
                pltpu.VMEM((ROW_BLK * ROW_SUB, LANES), F32),
                pltpu.SemaphoreType.DMA,
                pltpu.SemaphoreType.DMA((2,)),
            ],
        ),
        out_shape=jax.ShapeDtypeStruct(((n_rows + ROW_BLK) * ROW_SUB, LANES), F32),
        compiler_params=_cparams(("arbitrary",)),
        name="dispatch",
    )(cnt_tbl, ls_tbl, base_tbl, zoff, xn, route)


def _experts_kernel(blk_e_ref, nvalid_ref, first_ref, wslot_ref, enext_ref, xs_ref, wup_hbm, wdn_hbm,
                    bg_ref, bu_ref, bd_ref, ys_ref, wup_buf, wdn_buf, wg_s, wu_s, wd_s, sems):
    i = pl.program_id(0)
    slot = wslot_ref[i]

    def weight_copies(e, s):
        return (pltpu.make_async_copy(wup_hbm.at[e], wup_buf.at[s], sems.at[0, s]),
                pltpu.make_async_copy(wdn_hbm.at[e], wdn_buf.at[s], sems.at[1, s]))

    @pl.when(i == 0)
    def _():
        for cp in weight_copies(blk_e_ref[0], slot):
            cp.start()

    @pl.when(first_ref[i] != 0)
    def _():
        for cp in weight_copies(blk_e_ref[i], slot):
            cp.wait()

        @pl.when(enext_ref[i] >= 0)
        def _():
            for cp in weight_copies(enext_ref[i], 1 - slot):
                cp.start()

        src = lax.broadcasted_iota(jnp.int32, (DEINT, DEINT), 0)
        dst = lax.broadcasted_iota(jnp.int32, (DEINT, DEINT), 1)
        perm = jnp.where(src == jnp.where(dst < LANES, 2 * dst, 2 * (dst - LANES) + 1), 1.0, 0.0).astype(BF16)
        for g in range(2 * D_EXPERT // DEINT):
            sep = _dot(wup_buf[slot, :, g * DEINT:(g + 1) * DEINT].astype(BF16), perm)
            wg_s[:, g * LANES:(g + 1) * LANES] = sep[:, :LANES].astype(BF16)
            wu_s[:, g * LANES:(g + 1) * LANES] = sep[:, LANES:].astype(BF16)
        wd_s[...] = wdn_buf[slot].astype(BF16)

    @pl.when(i < nvalid_ref[0])
    def _():
        xb = _load_rows(xs_ref).astype(BF16)
        gate = jnp.minimum(_dot(xb, wg_s[...]) + bg_ref[0], SWIGLU_LIMIT)
        up = jnp.clip(_dot(xb, wu_s[...]) + bu_ref[0], -SWIGLU_LIMIT, SWIGLU_LIMIT)
        act = (up + 1.0) * gate * jax.nn.sigmoid(SWIGLU_ALPHA * gate)
        _store_rows(ys_ref, _dot(act.astype(BF16), wd_s[...]) + bd_ref[0])

    @pl.when(i >= nvalid_ref[0])
    def _():
        ys_ref[...] = jnp.zeros_like(ys_ref)


def _experts(xs, plan, w_up, b_up, w_down, b_down, n_rows):
    nb = n_rows // ROW_BLK
    bg = b_up[:, 0::2].reshape(N_EXPERTS, 1, D_EXPERT)
    bu = b_up[:, 1::2].reshape(N_EXPERTS, 1, D_EXPERT)
    bd = b_down.reshape(N_EXPERTS, 1, D_MODEL)
    blk_e, nvalid, first, wslot, enext = plan
    src = lambda i, be, nv, *_: (jnp.minimum(i, nv[0] - 1), 0)
    bspec = lambda m: pl.BlockSpec((1, 1, m), lambda i, be, *_: (be[i], 0, 0))
    anyspec = pl.BlockSpec(memory_space=pl.ANY)
    return pl.pallas_call(
        _experts_kernel,
        grid_spec=pltpu.PrefetchScalarGridSpec(
            num_scalar_prefetch=5,
            grid=(nb,),
            in_specs=[_rows_spec(ROW_BLK, src), anyspec, anyspec,
                      bspec(D_EXPERT), bspec(D_EXPERT), bspec(D_MODEL)],
            out_specs=_rows_spec(ROW_BLK, lambda i, *_: (i, 0)),
            scratch_shapes=[
                pltpu.VMEM((2, D_MODEL, 2 * D_EXPERT), F32),
                pltpu.VMEM((2, D_EXPERT, D_MODEL), F32),
                pltpu.VMEM((D_MODEL, D_EXPERT), BF16),
                pltpu.VMEM((D_MODEL, D_EXPERT), BF16),
                pltpu.VMEM((D_EXPERT, D_MODEL), BF16),
                pltpu.SemaphoreType.DMA((2, 2)),
            ],
        ),
        out_shape=jax.ShapeDtypeStruct((n_rows * ROW_SUB, LANES), F32),
        compiler_params=_cparams(("arbitrary",)),
        name="experts",
    )(blk_e, nvalid, first, wslot, enext, xs, w_up, w_down, bg, bu, bd)


def _combine_kernel(cnt_ref, ls_ref, base_ref, ys_hbm, route_ref, gate_ref, x1_ref, p_ref, gp_ref, wpg_ref,
                    wpp_ref, gf_ref, o_ref, yloc, sems):
    tm = TM_ROUTE
    n_pos = TOP_K * tm
    i = pl.program_id(0)
    n_steps = pl.num_programs(0)
    slot = lax.rem(i, 2)

    def gather_tile(tile, s):
        def per_expert(e, _):
            idx = tile * N_EXPERTS + e
            src0 = base_ref[idx]
            dst0 = ls_ref[idx]

            def piece(off, size):
                pltpu.make_async_copy(ys_hbm.at[_row_slice(src0 + off, size)],
                                      yloc.at[s, _row_slice(dst0 + off, size)], sems.at[s]).start()

            _for_each_run_piece(cnt_ref[idx], piece)
            return 0

        lax.fori_loop(0, N_EXPERTS, per_expert, 0)

    @pl.when(i == 0)
    def _():
        gather_tile(0, 0)

    @pl.when(i + 1 < n_steps)
    def _():
        gather_tile(i + 1, 1 - slot)

    masks = _one_hot_positions(route_ref, n_pos, tm)
    sel = jnp.where(masks[0] | masks[1] | masks[2] | masks[3], 1.0, 0.0).astype(BF16)
    gsel = jnp.zeros((n_pos, tm), F32)
    for k in range(TOP_K):
        gsel = gsel + jnp.where(masks[k], gate_ref[k:k + 1, :], 0.0)
    g_pos = jnp.sum(gsel, axis=1, keepdims=True)
    pp = _dot(p_ref[...].astype(BF16), wpp_ref[...])

    pltpu.make_async_copy(ys_hbm.at[_row_slice(0, n_pos)], yloc.at[slot], sems.at[slot]).wait()
    y_gated = (_load_rows(yloc.at[slot]) * g_pos).astype(BF16)
    x2 = x1_ref[...] + _dot_tn(sel, y_gated)
    xn = _rms(x2, gp_ref[...], NORM_EPS).astype(BF16)
    x3 = x2 + pp * jax.nn.sigmoid(_dot(xn, wpg_ref[...]))
    o_ref[...] = _rms(x3, gf_ref[...], NORM_EPS)


def _combine(ys, route, gate_t, x1, p2, tables, g_ple, w_ple_gate, w_ple_proj, g_final):
    t = x1.shape[0]
    tm = TM_ROUTE
    cnt_tbl, ls_tbl, base_tbl, _ = tables
    row = lambda n: pl.BlockSpec((tm, n), lambda i, *_: (i, 0))
    col = pl.BlockSpec((SUBLANES, tm), lambda i, *_: (0, i))
    full = lambda a: pl.BlockSpec(a.shape, lambda i, *_: (0, 0))
    consts = (g_ple.reshape(1, -1), w_ple_gate.astype(BF16), w_ple_proj.astype(BF16), g_final.reshape(1, -1))
    return pl.pallas_call(
        _combine_kernel,
        grid_spec=pltpu.PrefetchScalarGridSpec(
            num_scalar_prefetch=3,
            grid=(t // tm,),
            in_specs=[pl.BlockSpec(memory_space=pl.ANY), col, col, row(D_MODEL), row(PLE_DIM)]
                     + [full(a) for a in consts],
            out_specs=row(D_MODEL),
            scratch_shapes=[
                pltpu.VMEM((2, TOP_K * tm * ROW_SUB, LANES), F32),
                pltpu.SemaphoreType.DMA((2,)),
            ],
        ),
        out_shape=jax.ShapeDtypeStruct((t, D_MODEL), F32),
        compiler_params=_cparams(("arbitrary",)),
        name="combine",
    )(cnt_tbl, ls_tbl, base_tbl, ys, route, gate_t, x1, p2, *consts)


def _routing_tables(cnt, n_blocks):
    tile_cnt = cnt[:, 0, :N_EXPERTS].astype(jnp.int32)
    counts = jnp.sum(tile_cnt, axis=0)
    padded = (counts + ROW_BLK - 1) // ROW_BLK * ROW_BLK
    pend = jnp.cumsum(padded)
    pstart = pend - padded
    base = pstart[None, :] + jnp.cumsum(tile_cnt, axis=0) - tile_cnt
    lstart = jnp.cumsum(tile_cnt, axis=1) - tile_cnt
    nvalid = pend[-1:] // ROW_BLK
    zoff = jnp.concatenate([pstart + counts, nvalid])
    blk_start = jnp.minimum(jnp.arange(n_blocks, dtype=jnp.int32) * ROW_BLK, pend[-1] - 1)
    blk_e = jnp.minimum(jnp.sum((pend[None, :] <= blk_start[:, None]).astype(jnp.int32), axis=1), N_EXPERTS - 1)
    first = jnp.concatenate([jnp.ones((1,), bool), blk_e[1:] != blk_e[:-1]])
    wslot = (jnp.cumsum(first.astype(jnp.int32)) - 1) % 2
    eids = jnp.arange(N_EXPERTS, dtype=jnp.int32)
    later_nonempty = (eids[None, :] > eids[:, None]) & (padded[None, :] > 0)
    next_e = jnp.min(jnp.where(later_nonempty, eids[None, :], N_EXPERTS), axis=1)
    next_e = jnp.where(next_e == N_EXPERTS, -1, next_e)
    i32 = lambda a: a.reshape(-1).astype(jnp.int32)
    plan = (i32(blk_e), i32(nvalid), i32(first), i32(wslot), i32(next_e[blk_e]))
    return (i32(tile_cnt), i32(lstart), i32(base), i32(zoff)), plan


def kernel(x, p, g_mix, w_in, conv_w, conv_b, dt_bias, a_log, d_skip, ssd_norm_w, lam_q1, lam_k1, lam_q2, lam_k2, subln_w, w_out, g_ffn, w_router, b_router, w_up, b_up, w_down, b_down, g_ple, w_ple_gate, w_ple_proj, g_final):
    bsz, seq, d = x.shape
    t = bsz * seq
    x2 = x.reshape(t, d)
    y_ssd, y_att = _mixer(x2, g_mix[0], w_in[0], conv_w[0], conv_b[0], dt_bias[0], a_log[0], d_skip[0], ssd_norm_w[0],
                          lam_q1[0], lam_k1[0], lam_q2[0], lam_k2[0], subln_w[0], bsz, seq)
    x1, xn, route, gate_t, cnt = _out_proj(x2, y_ssd, y_att, w_out[0], g_ffn[0], w_router[0], b_router[0])

    n_rows = t * TOP_K + N_EXPERTS * ROW_BLK
    tables, plan = _routing_tables(cnt, n_rows // ROW_BLK)
    xs = _dispatch(xn, route, tables, n_rows)
    ys = _experts(xs, plan, w_up[0], b_up[0], w_down[0], b_down[0], n_rows)
    out = _combine(ys, route, gate_t, x1, p[0].reshape(t, PLE_DIM), tables, g_ple[0], w_ple_gate[0], w_ple_proj[0],
                   g_final)
    return out.reshape(bsz, seq, d)
```

```python
import math

import jax
import jax.numpy as jnp
from jax import lax
from jax.experimental import pallas as pl
from jax.experimental.pallas import tpu as pltpu

F32 = jnp.float32
BF16 = jnp.bfloat16

D_MODEL = 1024
PLE_DIM = 256
SSD_WIDTH = 512
ATT_WIDTH = 512
SSD_HEAD_DIM = 64
SSD_HEADS = 8
SSD_GROUPS = 2
SSD_STATE = 128
SSD_CONV = 4
SSD_CHUNK = 128
SSD_CONV_CH = SSD_WIDTH + 2 * SSD_GROUPS * SSD_STATE
SSD_NORM_EPS = 1e-5
ATT_HEAD_DIM = 64
ATT_HEADS = 4
SUBLN_EPS = 1e-5
OFF_Z = 0
OFF_XBC = OFF_Z + SSD_WIDTH
OFF_DT = OFF_XBC + SSD_CONV_CH
OFF_Q = OFF_DT + SSD_HEADS
OFF_K = OFF_Q + ATT_WIDTH
OFF_V = OFF_K + ATT_WIDTH
IN_PROJ = OFF_V + ATT_WIDTH
N_EXPERTS = 32
TOP_K = 4
D_EXPERT = 1024
SWIGLU_LIMIT = 7.0
SWIGLU_ALPHA = 1.702
NORM_EPS = 1e-6
LAM_INIT = 0.8 - 0.6 * math.exp(-0.3 * 0)

LANES = 128
SUBLANES = 8
VMEM_LIMIT_BYTES = 56 * 1024 * 1024
ROW_SUB = D_MODEL // LANES

TM_PROJ = 512
SSD_PAR = 2
ATT_BQ = 512
ATT_BK = 512
ROW_BLK = 256
TM_ROUTE = 256
OUT_SUB = 2
DEINT = 2 * LANES


def _cparams(sem):
    return pltpu.CompilerParams(dimension_semantics=sem, vmem_limit_bytes=VMEM_LIMIT_BYTES)


def _rms(x, w, eps):
    return x * lax.rsqrt(jnp.mean(x * x, axis=-1, keepdims=True) + eps) * w


def _dot(a, b):
    return jnp.dot(a, b, preferred_element_type=F32)


def _dot_nt(a, b):
    return lax.dot_general(a, b, (((1,), (1,)), ((), ())), preferred_element_type=F32)


def _dot_tn(a, b):
    return lax.dot_general(a, b, (((0,), (0,)), ((), ())), preferred_element_type=F32)


def _store_rows(ref, val):
    n = val.shape[0]
    for s in range(ROW_SUB):
        ref[pl.ds(s, n, stride=ROW_SUB), :] = val[:, s * LANES:(s + 1) * LANES]


def _load_rows(ref):
    n = ref.shape[0] // ROW_SUB
    return jnp.concatenate([ref[pl.ds(s, n, stride=ROW_SUB), :] for s in range(ROW_SUB)], axis=-1)


def _rows_spec(n, index_map):
    return pl.BlockSpec((n * ROW_SUB, LANES), index_map)


def _row_slice(start, n):
    return pl.ds(pl.multiple_of(start * ROW_SUB, ROW_SUB), n * ROW_SUB)


def _split3(x):
    hi = x.astype(BF16)
    r1 = x - hi.astype(F32)
    mid = r1.astype(BF16)
    lo = (r1 - mid.astype(F32)).astype(BF16)
    return hi, mid, lo


def _in_proj_kernel(x_ref, g_ref, wz_ref, wxbc_ref, wdt_ref, wq_ref, wk_ref, wv_ref,
                    z_ref, xbc_ref, dt_ref, q_ref, k_ref, v_ref):
    h = _rms(x_ref[...], g_ref[...], NORM_EPS).astype(BF16)
    z_ref[...] = _dot(h, wz_ref[...])
    xbc_ref[...] = _dot(h, wxbc_ref[...])
    dt_ref[...] = _dot(h, wdt_ref[...])
    q_ref[...] = (_dot(h, wq_ref[...]) * (ATT_HEAD_DIM ** -0.5)).astype(BF16)
    k_ref[...] = _dot(h, wk_ref[...]).astype(BF16)
    v_ref[...] = _dot(h, wv_ref[...]).astype(BF16)


def _in_proj(x2, g_mix, w_in):
    t = x2.shape[0]
    wb = w_in.astype(BF16)
    wz = wb[:, OFF_Z:OFF_XBC]
    wxbc = wb[:, OFF_XBC:OFF_DT]
    wdt = jnp.pad(wb[:, OFF_DT:OFF_Q], ((0, 0), (0, LANES - SSD_HEADS)))
    wq = wb[:, OFF_Q:OFF_K]
    wk = wb[:, OFF_K:OFF_V]
    wv = wb[:, OFF_V:IN_PROJ]
    tm = TM_PROJ
    row = lambda n: pl.BlockSpec((tm, n), lambda i: (i, 0))
    full = lambda a: pl.BlockSpec(a.shape, lambda i: (0, 0))
    return pl.pallas_call(
        _in_proj_kernel,
        grid=(t // tm,),
        in_specs=[row(D_MODEL), full(g_mix), full(wz), full(wxbc), full(wdt), full(wq), full(wk), full(wv)],
        out_specs=[row(SSD_WIDTH), row(SSD_CONV_CH), row(LANES), row(ATT_WIDTH), row(ATT_WIDTH), row(ATT_WIDTH)],
        out_shape=[
            jax.ShapeDtypeStruct((t, SSD_WIDTH), F32),
            jax.ShapeDtypeStruct((t, SSD_CONV_CH), F32),
            jax.ShapeDtypeStruct((t, LANES), F32),
            jax.ShapeDtypeStruct((t, ATT_WIDTH), BF16),
            jax.ShapeDtypeStruct((t, ATT_WIDTH), BF16),
            jax.ShapeDtypeStruct((t, ATT_WIDTH), BF16),
        ],
        compiler_params=_cparams(("arbitrary",)),
        name="in_proj",
    )(x2, g_mix, wz, wxbc, wdt, wq, wk, wv)


def _ssd_kernel(xbc_ref, dtr_ref, z_ref, cw_ref, cb_ref, dtb_ref, alog_ref, dskip_ref, nw_ref,
                y_ref, xpad_ref, state_ref):
    for bb in range(SSD_PAR):
        _ssd_chunk(xbc_ref.at[bb], dtr_ref.at[bb], z_ref.at[bb], cw_ref, cb_ref, dtb_ref, alog_ref, dskip_ref,
                   nw_ref, y_ref.at[bb], xpad_ref.at[bb], state_ref.at[bb])


def _ssd_chunk(xbc_ref, dtr_ref, z_ref, cw_ref, cb_ref, dtb_ref, alog_ref, dskip_ref, nw_ref,
               y_ref, xpad_ref, state_ref):
    L = SSD_CHUNK
    c = pl.program_id(1)

    @pl.when(c == 0)
    def _():
        xpad_ref[0:SUBLANES, :] = jnp.zeros((SUBLANES, SSD_CONV_CH), F32)
        state_ref[...] = jnp.zeros_like(state_ref)

    @pl.when(c != 0)
    def _():
        xpad_ref[0:SUBLANES, :] = xpad_ref[L:L + SUBLANES, :]

    xpad_ref[SUBLANES:SUBLANES + L, :] = xbc_ref[...]

    conv = cb_ref[...]
    for j in range(SSD_CONV):
        off = SUBLANES - (SSD_CONV - 1) + j
        conv = conv + cw_ref[j:j + 1, :] * xpad_ref[off:off + L, :]
    act = conv * jax.nn.sigmoid(conv)
    xs = act[:, :SSD_WIDTH]
    bm = act[:, SSD_WIDTH:SSD_WIDTH + SSD_GROUPS * SSD_STATE].astype(BF16)
    cm = act[:, SSD_WIDTH + SSD_GROUPS * SSD_STATE:].astype(BF16)

    dt_in = dtr_ref[...] + dtb_ref[...]
    dt = jnp.maximum(dt_in, 0.0) + jnp.log1p(jnp.exp(-jnp.abs(dt_in)))
    adt = dt * (-jnp.exp(alog_ref[...]))

    ri = lax.broadcasted_iota(jnp.int32, (L, L), 0)
    ci = lax.broadcasted_iota(jnp.int32, (L, L), 1)
    causal = ci <= ri
    tril = jnp.where(causal, 1.0, 0.0).astype(BF16)
    hi, mid, lo = _split3(adt)
    acum = _dot(tril, hi) + _dot(tril, mid) + _dot(tril, lo)
    acum_t = acum.T
    a_last = acum[L - 1:L, :]
    decay_in = jnp.exp(a_last - acum)
    decay_out = jnp.exp(acum)
    chunk_decay = jnp.exp(a_last)

    lane = lax.broadcasted_iota(jnp.int32, (L, LANES), 1)
    lo_half = lane < SSD_HEAD_DIM

    def per_pair(col_a, col_b):
        return jnp.where(lo_half, col_a, col_b)

    ys = []
    for pair in range(SSD_HEADS // 2):
        g = pair // 2
        h0, h1 = 2 * pair, 2 * pair + 1
        cg = cm[:, g * SSD_STATE:(g + 1) * SSD_STATE]
        bg = bm[:, g * SSD_STATE:(g + 1) * SSD_STATE]
        cb = _dot_nt(cg, bg)
        x_pair = xs[:, pair * LANES:(pair + 1) * LANES]
        xdt = x_pair * per_pair(dt[:, h0:h0 + 1], dt[:, h1:h1 + 1])
        y_pair = jnp.zeros((L, LANES), F32)
        for hh, keep in ((h0, lo_half), (h1, jnp.logical_not(lo_half))):
            seg = acum[:, hh:hh + 1] - acum_t[hh:hh + 1, :]
            lmat = jnp.where(causal, jnp.exp(jnp.where(causal, seg, 0.0)), 0.0)
            m = (cb * lmat).astype(BF16)
            y_pair = y_pair + _dot(m, jnp.where(keep, xdt, 0.0).astype(BF16))
        s_prev = state_ref[pair]
        y_off = _dot(cg, s_prev.astype(BF16)) * per_pair(decay_out[:, h0:h0 + 1], decay_out[:, h1:h1 + 1])
        w_in = (xdt * per_pair(decay_in[:, h0:h0 + 1], decay_in[:, h1:h1 + 1])).astype(BF16)
        cd = jnp.where(lane[0:1, :] < SSD_HEAD_DIM, chunk_decay[:, h0:h0 + 1], chunk_decay[:, h1:h1 + 1])
        state_ref[pair] = s_prev * cd + _dot_tn(bg, w_in)
        ys.append(y_pair + y_off + dskip_ref[:, pair * LANES:(pair + 1) * LANES] * x_pair)

    y = jnp.concatenate(ys, axis=-1)
    zz = z_ref[...]
    y = y * (zz * jax.nn.sigmoid(zz))
    gw = SSD_WIDTH // SSD_GROUPS
    outs = []
    for g in range(SSD_GROUPS):
        yg = y[:, g * gw:(g + 1) * gw]
        outs.append(yg * lax.rsqrt(jnp.mean(yg * yg, axis=-1, keepdims=True) + SSD_NORM_EPS))
    y_ref[...] = (jnp.concatenate(outs, axis=-1) * nw_ref[...]).astype(BF16)


def _ssd(xbc, dtr, z, conv_w, conv_b, dt_bias, a_log, d_skip, norm_w, bsz, seq):
    nc = seq // SSD_CHUNK
    pad_h = lambda v: jnp.pad(v.reshape(1, SSD_HEADS), ((0, 0), (0, LANES - SSD_HEADS)))
    dskip_lanes = jnp.repeat(d_skip, SSD_HEAD_DIM).reshape(1, SSD_WIDTH)
    seqs = lambda a: a.reshape(bsz, seq, a.shape[-1])
    row = lambda n: pl.BlockSpec((SSD_PAR, SSD_CHUNK, n), lambda b, c: (b, c, 0))
    full = lambda a: pl.BlockSpec(a.shape, lambda b, c: (0, 0))
    args = (seqs(xbc), seqs(dtr), seqs(z), conv_w, conv_b.reshape(1, -1), pad_h(dt_bias), pad_h(a_log), dskip_lanes,
            norm_w.reshape(1, -1))
    y = pl.pallas_call(
        _ssd_kernel,
        grid=(bsz // SSD_PAR, nc),
        in_specs=[row(SSD_CONV_CH), row(LANES), row(SSD_WIDTH)] + [full(a) for a in args[3:]],
        out_specs=row(SSD_WIDTH),
        out_shape=jax.ShapeDtypeStruct((bsz, seq, SSD_WIDTH), BF16),
        scratch_shapes=[
            pltpu.VMEM((SSD_PAR, SSD_CHUNK + 2 * SUBLANES, SSD_CONV_CH), F32),
            pltpu.VMEM((SSD_PAR, SSD_HEADS // 2, SSD_STATE, LANES), F32),
        ],
        compiler_params=_cparams(("arbitrary", "arbitrary")),
        name="ssd",
    )(*args)
    return y.reshape(bsz * seq, SSD_WIDTH)


def _attn_kernel(q_ref, k_ref, v_ref, lq1_ref, lk1_ref, lq2_ref, lk2_ref, sw_ref, o_ref,
                 s_scr, mt_scr, lt_scr, acc_scr):
    bq, bk = ATT_BQ, ATT_BK
    qi = pl.program_id(2)
    q = q_ref[...]
    lane = lax.broadcasted_iota(jnp.int32, (bq, LANES), 1)
    zero = jnp.zeros_like(q)
    q_maps = (jnp.where(lane < ATT_HEAD_DIM, q, zero), jnp.where(lane >= ATT_HEAD_DIM, q, zero))
    n_maps = len(q_maps)

    def fold(t, op):
        out = t[:, :LANES]
        for c in range(1, bk // LANES):
            out = op(out, t[:, c * LANES:(c + 1) * LANES])
        return out

    def scores(j, mask):
        kb = k_ref[pl.ds(pl.multiple_of(j * bk, bk), bk), :]
        for m in range(n_maps):
            s = _dot_nt(q_maps[m], kb)
            if mask is not None:
                s = jnp.where(mask, s, -jnp.inf)
            s_scr[m, j] = s
            mt_scr[m] = jnp.maximum(mt_scr[m], fold(s, jnp.maximum))

    def scores_body(j, _):
        scores(j, None)
        return 0

    mt_scr[...] = jnp.full(mt_scr.shape, -jnp.inf, F32)
    lax.fori_loop(0, qi, scores_body, 0)
    r = lax.broadcasted_iota(jnp.int32, (bq, bk), 0)
    cidx = lax.broadcasted_iota(jnp.int32, (bq, bk), 1)
    scores(qi, cidx <= r)
    row_max = [jnp.max(mt_scr[m], axis=-1, keepdims=True) for m in range(n_maps)]

    lt_scr[...] = jnp.zeros(lt_scr.shape, F32)
    acc_scr[...] = jnp.zeros(acc_scr.shape, F32)

    def accumulate(j, _):
        vb = v_ref[pl.ds(pl.multiple_of(j * bk, bk), bk), :]
        for m in range(n_maps):
            p = jnp.exp(s_scr[m, j] - row_max[m])
            lt_scr[m] = lt_scr[m] + fold(p, jnp.add)
            acc_scr[m] = acc_scr[m] + _dot(p.astype(BF16), vb)
        return 0

    lax.fori_loop(0, qi + 1, accumulate, 0)

    lam = (jnp.exp(jnp.sum(lq1_ref[...] * lk1_ref[...], axis=-1, keepdims=True))
           - jnp.exp(jnp.sum(lq2_ref[...] * lk2_ref[...], axis=-1, keepdims=True)) + LAM_INIT)
    l1 = jnp.sum(lt_scr[0], axis=-1, keepdims=True)
    l2 = jnp.sum(lt_scr[1], axis=-1, keepdims=True)
    o = acc_scr[0] / l1 - lam * (acc_scr[1] / l2)
    o_ref[...] = (_rms(o, sw_ref[...], SUBLN_EPS) * (1.0 - LAM_INIT)).astype(BF16)


def _attn(q, k, v, lam_q1, lam_k1, lam_q2, lam_k2, subln_w, bsz, seq):
    nq = seq // ATT_BQ
    qspec = pl.BlockSpec((ATT_BQ, LANES), lambda b, h, i: (b * nq + i, h))
    kvspec = pl.BlockSpec((seq, LANES), lambda b, h, i: (b, h))
    full = lambda a: pl.BlockSpec(a.shape, lambda b, h, i: (0, 0))
    lams = [a.reshape(1, -1) for a in (lam_q1, lam_k1, lam_q2, lam_k2)]
    sw = subln_w.reshape(1, -1)
    return pl.pallas_call(
        _attn_kernel,
        grid=(bsz, ATT_HEADS, nq),
        in_specs=[qspec, kvspec, kvspec] + [full(a) for a in lams] + [full(sw)],
        out_specs=qspec,
        out_shape=jax.ShapeDtypeStruct((bsz * seq, ATT_WIDTH), BF16),
        scratch_shapes=[
            pltpu.VMEM((2, seq // ATT_BK, ATT_BQ, ATT_BK), F32),
            pltpu.VMEM((2, ATT_BQ, LANES), F32),
            pltpu.VMEM((2, ATT_BQ, LANES), F32),
            pltpu.VMEM((2, ATT_BQ, LANES), F32),
        ],
        compiler_params=_cparams(("arbitrary", "arbitrary", "arbitrary")),
        name="attn",
    )(q, k, v, *lams, sw)


def _mixer(x2, g_mix, w_in, conv_w, conv_b, dt_bias, a_log, d_skip, ssd_norm_w,
           lam_q1, lam_k1, lam_q2, lam_k2, subln_w, bsz, seq):
    z, xbc, dtr, q, k, v = _in_proj(x2, g_mix.reshape(1, -1), w_in)
    y_ssd = _ssd(xbc, dtr, z, conv_w, conv_b, dt_bias, a_log, d_skip, ssd_norm_w, bsz, seq)
    y_att = _attn(q, k, v, lam_q1, lam_k1, lam_q2, lam_k2, subln_w, bsz, seq)
    return y_ssd, y_att


def _out_proj_kernel(x_ref, ys_ref, ya_ref, wo_ref, g_ref, wr2_ref, br_ref,
                     x1_ref, xn_ref, route_ref, gate_ref, cnt_ref):
    tm = TM_ROUTE
    x1 = x_ref[...] + _dot(jnp.concatenate([ys_ref[...], ya_ref[...]], axis=-1), wo_ref[...])
    x1_ref[...] = x1
    xn_all = _rms(x1, g_ref[...], NORM_EPS)
    xn_ref[...] = xn_all.astype(BF16)

    n_tok = OUT_SUB * tm
    xh, xm, _ = _split3(xn_all)
    lg2 = _dot(xh, wr2_ref[...]) + _dot(xm, wr2_ref[...])
    logits = (lg2[:, :LANES] + lg2[:, LANES:]).T[:N_EXPERTS, :] + br_ref[...]

    eidx = lax.broadcasted_iota(jnp.int32, (N_EXPERTS, n_tok), 0).astype(F32)
    work = logits
    vals, idxs, hots = [], [], []
    for _ in range(TOP_K):
        m = jnp.max(work, axis=0, keepdims=True)
        idx = jnp.min(jnp.where(work == m, eidx, float(N_EXPERTS)), axis=0, keepdims=True)
        hot = eidx == idx
        vals.append(m)
        idxs.append(idx.astype(jnp.int32))
        hots.append(hot)
        work = jnp.where(hot, -jnp.inf, work)
    exps = [jnp.exp(v - vals[0]) for v in vals]
    denom = exps[0] + exps[1] + exps[2] + exps[3]
    gates = [e / denom for e in exps]

    cnt = jnp.zeros((N_EXPERTS, n_tok), F32)
    for hot in hots:
        cnt = cnt + jnp.where(hot, 1.0, 0.0)
    cnt_b = cnt.astype(BF16)
    r = lax.broadcasted_iota(jnp.int32, (tm, tm), 0)
    c = lax.broadcasted_iota(jnp.int32, (tm, tm), 1)
    earlier_tok = jnp.where(r < c, 1.0, 0.0).astype(BF16)
    er = lax.broadcasted_iota(jnp.int32, (N_EXPERTS, N_EXPERTS), 0)
    ec = lax.broadcasted_iota(jnp.int32, (N_EXPERTS, N_EXPERTS), 1)
    lower_exp = jnp.where(ec < er, 1.0, 0.0).astype(BF16)
    below = _dot(lower_exp, cnt_b)
    cnt_pad = jnp.concatenate([cnt_b, jnp.zeros((LANES - N_EXPERTS, n_tok), BF16)], axis=0)
    pos = []
    for sub in range(OUT_SUB):
        cols = slice(sub * tm, (sub + 1) * tm)
        pos.append(_dot(cnt_b[:, cols], earlier_tok) + jnp.sum(below[:, cols], axis=1, keepdims=True))
        cnt_ref[sub] = _dot_nt(jnp.ones((SUBLANES, tm), BF16), cnt_pad[:, cols])
    pos = jnp.concatenate(pos, axis=1)
    lps = [jnp.sum(jnp.where(hot, pos, 0.0), axis=0, keepdims=True).astype(jnp.int32) for hot in hots]

    route_ref[...] = jnp.concatenate(idxs + lps, axis=0)
    gate_ref[...] = jnp.concatenate(gates + [jnp.zeros((SUBLANES - TOP_K, n_tok), F32)], axis=0)


def _out_proj(x2, y_ssd, y_att, w_out, g_ffn, w_router, b_router):
    t = x2.shape[0]
    tm = OUT_SUB * TM_ROUTE
    nt = t // TM_ROUTE
    wo = w_out.astype(BF16)
    wrh, wrm, _ = _split3(jnp.pad(w_router, ((0, 0), (0, LANES - N_EXPERTS))))
    wr2 = jnp.concatenate([wrh, wrm], axis=1)
    br = b_router.reshape(N_EXPERTS, 1)
    row = lambda n: pl.BlockSpec((tm, n), lambda i: (i, 0))
    col = pl.BlockSpec((SUBLANES, tm), lambda i: (0, i))
    full = lambda a: pl.BlockSpec(a.shape, lambda i: (0, 0))
    args = (x2, y_ssd, y_att, wo, g_ffn.reshape(1, -1), wr2, br)
    return pl.pallas_call(
        _out_proj_kernel,
        grid=(t // tm,),
        in_specs=[row(D_MODEL), row(SSD_WIDTH), row(ATT_WIDTH)] + [full(a) for a in args[3:]],
        out_specs=[row(D_MODEL), row(D_MODEL), col, col,
                   pl.BlockSpec((OUT_SUB, SUBLANES, LANES), lambda i: (i, 0, 0))],
        out_shape=[
            jax.ShapeDtypeStruct((t, D_MODEL), F32),
            jax.ShapeDtypeStruct((t, D_MODEL), BF16),
            jax.ShapeDtypeStruct((SUBLANES, t), jnp.int32),
            jax.ShapeDtypeStruct((SUBLANES, t), F32),
            jax.ShapeDtypeStruct((nt, SUBLANES, LANES), F32),
        ],
        compiler_params=_cparams(("arbitrary",)),
        name="out_proj",
    )(*args)


def _copy_run(n, start_copy):
    @pl.when(n > 0)
    def _():
        start_copy(n)


def _one_hot_positions(route_ref, n_pos, tm):
    r = lax.broadcasted_iota(jnp.int32, (n_pos, tm), 0)
    return [r == route_ref[TOP_K + k:TOP_K + k + 1, :] for k in range(TOP_K)]


def _dispatch_kernel(cnt_ref, ls_ref, base_ref, zoff_ref, xn_ref, route_ref, xs_hbm, xloc, zeros_vmem, sem_z, sems):
    tm = TM_ROUTE
    n_pos = TOP_K * tm
    i = pl.program_id(0)
    n_steps = pl.num_programs(0)
    slot = lax.rem(i, 2)

    @pl.when(i == 0)
    def _():
        zeros_vmem[...] = jnp.zeros_like(zeros_vmem)

        def zfill(e, _):
            pltpu.make_async_copy(zeros_vmem, xs_hbm.at[_row_slice(zoff_ref[e], ROW_BLK)], sem_z).start()
            return 0

        def zwait(e, _):
            pltpu.make_async_copy(zeros_vmem, xs_hbm.at[_row_slice(0, ROW_BLK)], sem_z).wait()
            return 0

        def tfill(b, _):
            pltpu.make_async_copy(zeros_vmem, xs_hbm.at[_row_slice(b * ROW_BLK, ROW_BLK)], sem_z).start()
            return 0

        lax.fori_loop(0, N_EXPERTS, zfill, 0)
        lax.fori_loop(0, N_EXPERTS, zwait, 0)
        n_blk_total = xs_hbm.shape[0] // (ROW_BLK * ROW_SUB)
        lax.fori_loop(zoff_ref[N_EXPERTS], n_blk_total, tfill, 0)
        lax.fori_loop(zoff_ref[N_EXPERTS], n_blk_total, zwait, 0)

    masks = _one_hot_positions(route_ref, n_pos, tm)
    sel = jnp.where(masks[0] | masks[1] | masks[2] | masks[3], 1.0, 0.0).astype(BF16)
    rows = _dot(sel, xn_ref[...])

    def slot_wait(s):
        pltpu.make_async_copy(xloc.at[s], xs_hbm.at[_row_slice(0, n_pos)], sems.at[s]).wait()

    @pl.when(i >= 2)
    def _():
        slot_wait(slot)

    _store_rows(xloc.at[slot], rows)

    def per_expert(e, _):
        idx = i * N_EXPERTS + e
        src0 = ls_ref[idx]
        dst0 = base_ref[idx]

        def start_copy(n):
            pltpu.make_async_copy(xloc.at[slot, _row_slice(src0, n)],
                                  xs_hbm.at[_row_slice(dst0, n)], sems.at[slot]).start()

        _copy_run(cnt_ref[idx], start_copy)
        return 0

    lax.fori_loop(0, N_EXPERTS, per_expert, 0)

    @pl.when(i == n_steps - 1)
    def _():
        slot_wait(slot)

        @pl.when(n_steps > 1)
        def _():
            slot_wait(1 - slot)


def _dispatch(xn, route, tables, n_rows):
    t = xn.shape[0]
    tm = TM_ROUTE
    cnt_tbl, ls_tbl, base_tbl, zoff = tables
    return pl.pallas_call(
        _dispatch_kernel,
        grid_spec=pltpu.PrefetchScalarGridSpec(
            num_scalar_prefetch=4,
            grid=(t // tm,),
            in_specs=[pl.BlockSpec((tm, D_MODEL), lambda i, *_: (i, 0)),
                      pl.BlockSpec((SUBLANES, tm), lambda i, *_: (0, i))],
            out_specs=pl.BlockSpec(memory_space=pl.ANY),
            scratch_shapes=[
                pltpu.VMEM((2, TOP_K * tm * ROW_SUB, LANES), F32),
                pltpu.VMEM((ROW_BLK * ROW_SUB, LANES), F32),
                pltpu.SemaphoreType.DMA,
                pltpu.SemaphoreType.DMA((2,)),
            ],
        ),
        out_shape=jax.ShapeDtypeStruct(((n_rows + ROW_BLK) * ROW_SUB, LANES), F32),
        compiler_params=_cparams(("arbitrary",)),
        name="dispatch",
    )(cnt_tbl, ls_tbl, base_tbl, zoff, xn, route)


def _experts_kernel(blk_e_ref, nvalid_ref, first_ref, wslot_ref, enext_ref, xs_ref, wup_hbm, wdn_hbm,
                    bg_ref, bu_ref, bd_ref, ys_ref, wup_buf, wdn_buf, wg_s, wu_s, wd_s, sems):
    i = pl.program_id(0)
    slot = wslot_ref[i]

    def weight_copies(e, s):
        return (pltpu.make_async_copy(wup_hbm.at[e], wup_buf.at[s], sems.at[0, s]),
                pltpu.make_async_copy(wdn_hbm.at[e], wdn_buf.at[s], sems.at[1, s]))

    @pl.when(i == 0)
    def _():
        for cp in weight_copies(blk_e_ref[0], slot):
            cp.start()

    @pl.when(first_ref[i] != 0)
    def _():
        for cp in weight_copies(blk_e_ref[i], slot):
            cp.wait()

        @pl.when(enext_ref[i] >= 0)
        def _():
            for cp in weight_copies(enext_ref[i], 1 - slot):
                cp.start()

        src = lax.broadcasted_iota(jnp.int32, (DEINT, DEINT), 0)
        dst = lax.broadcasted_iota(jnp.int32, (DEINT, DEINT), 1)
        perm = jnp.where(src == jnp.where(dst < LANES, 2 * dst, 2 * (dst - LANES) + 1), 1.0, 0.0).astype(BF16)
        for g in range(2 * D_EXPERT // DEINT):
            sep = _dot(wup_buf[slot, :, g * DEINT:(g + 1) * DEINT].astype(BF16), perm)
            wg_s[:, g * LANES:(g + 1) * LANES] = sep[:, :LANES].astype(BF16)
            wu_s[:, g * LANES:(g + 1) * LANES] = sep[:, LANES:].astype(BF16)
        wd_s[...] = wdn_buf[slot].astype(BF16)

    @pl.when(i < nvalid_ref[0])
    def _():
        xb = _load_rows(xs_ref).astype(BF16)
        gate = jnp.minimum(_dot(xb, wg_s[...]) + bg_ref[0], SWIGLU_LIMIT)
        up = jnp.clip(_dot(xb, wu_s[...]) + bu_ref[0], -SWIGLU_LIMIT, SWIGLU_LIMIT)
        act = (up + 1.0) * gate * jax.nn.sigmoid(SWIGLU_ALPHA * gate)
        _store_rows(ys_ref, _dot(act.astype(BF16), wd_s[...]) + bd_ref[0])

    @pl.when(i >= nvalid_ref[0])
    def _():
        ys_ref[...] = jnp.zeros_like(ys_ref)


def _experts(xs, plan, w_up, b_up, w_down, b_down, n_rows):
    nb = n_rows // ROW_BLK
    bg = b_up[:, 0::2].reshape(N_EXPERTS, 1, D_EXPERT)
    bu = b_up[:, 1::2].reshape(N_EXPERTS, 1, D_EXPERT)
    bd = b_down.reshape(N_EXPERTS, 1, D_MODEL)
    blk_e, nvalid, first, wslot, enext = plan
    src = lambda i, be, nv, *_: (jnp.minimum(i, nv[0] - 1), 0)
    bspec = lambda m: pl.BlockSpec((1, 1, m), lambda i, be, *_: (be[i], 0, 0))
    anyspec = pl.BlockSpec(memory_space=pl.ANY)
    return pl.pallas_call(
        _experts_kernel,
        grid_spec=pltpu.PrefetchScalarGridSpec(
            num_scalar_prefetch=5,
            grid=(nb,),
            in_specs=[_rows_spec(ROW_BLK, src), anyspec, anyspec,
                      bspec(D_EXPERT), bspec(D_EXPERT), bspec(D_MODEL)],
            out_specs=_rows_spec(ROW_BLK, lambda i, *_: (i, 0)),
            scratch_shapes=[
                pltpu.VMEM((2, D_MODEL, 2 * D_EXPERT), F32),
                pltpu.VMEM((2, D_EXPERT, D_MODEL), F32),
                pltpu.VMEM((D_MODEL, D_EXPERT), BF16),
                pltpu.VMEM((D_MODEL, D_EXPERT), BF16),
                pltpu.VMEM((D_EXPERT, D_MODEL), BF16),
                pltpu.SemaphoreType.DMA((2, 2)),
            ],
        ),
        out_shape=jax.ShapeDtypeStruct((n_rows * ROW_SUB, LANES), F32),
        compiler_params=_cparams(("arbitrary",)),
        name="experts",
    )(blk_e, nvalid, first, wslot, enext, xs, w_up, w_down, bg, bu, bd)


def _combine_kernel(cnt_ref, ls_ref, base_ref, ys_hbm, route_ref, gate_ref, x1_ref, p_ref, gp_ref, wpg_ref,
                    wpp_ref, gf_ref, o_ref, yloc, sems):
    tm = TM_ROUTE
    n_pos = TOP_K * tm
    i = pl.program_id(0)
    n_steps = pl.num_programs(0)
    slot = lax.rem(i, 2)

    def gather_tile(tile, s):
        def per_expert(e, _):
            idx = tile * N_EXPERTS + e
            src0 = base_ref[idx]
            dst0 = ls_ref[idx]

            def start_copy(n):
                pltpu.make_async_copy(ys_hbm.at[_row_slice(src0, n)],
                                      yloc.at[s, _row_slice(dst0, n)], sems.at[s]).start()

            _copy_run(cnt_ref[idx], start_copy)
            return 0

        lax.fori_loop(0, N_EXPERTS, per_expert, 0)

    @pl.when(i == 0)
    def _():
        gather_tile(0, 0)

    @pl.when(i + 1 < n_steps)
    def _():
        gather_tile(i + 1, 1 - slot)

    masks = _one_hot_positions(route_ref, n_pos, tm)
    sel = jnp.where(masks[0] | masks[1] | masks[2] | masks[3], 1.0, 0.0).astype(BF16)
    gsel = jnp.zeros((n_pos, tm), F32)
    for k in range(TOP_K):
        gsel = gsel + jnp.where(masks[k], gate_ref[k:k + 1, :], 0.0)
    g_pos = jnp.sum(gsel, axis=1, keepdims=True)
    pp = _dot(p_ref[...].astype(BF16), wpp_ref[...])

    pltpu.make_async_copy(ys_hbm.at[_row_slice(0, n_pos)], yloc.at[slot], sems.at[slot]).wait()
    y_gated = (_load_rows(yloc.at[slot]) * g_pos).astype(BF16)
    x2 = x1_ref[...] + _dot_tn(sel, y_gated)
    xn = _rms(x2, gp_ref[...], NORM_EPS).astype(BF16)
    x3 = x2 + pp * jax.nn.sigmoid(_dot(xn, wpg_ref[...]))
    o_ref[...] = _rms(x3, gf_ref[...], NORM_EPS)


def _combine(ys, route, gate_t, x1, p2, tables, g_ple, w_ple_gate, w_ple_proj, g_final):
    t = x1.shape[0]
    tm = TM_ROUTE
    cnt_tbl, ls_tbl, base_tbl, _ = tables
    row = lambda n: pl.BlockSpec((tm, n), lambda i, *_: (i, 0))
    col = pl.BlockSpec((SUBLANES, tm), lambda i, *_: (0, i))
    full = lambda a: pl.BlockSpec(a.shape, lambda i, *_: (0, 0))
    consts = (g_ple.reshape(1, -1), w_ple_gate.astype(BF16), w_ple_proj.astype(BF16), g_final.reshape(1, -1))
    return pl.pallas_call(
        _combine_kernel,
        grid_spec=pltpu.PrefetchScalarGridSpec(
            num_scalar_prefetch=3,
            grid=(t // tm,),
            in_specs=[pl.BlockSpec(memory_space=pl.ANY), col, col, row(D_MODEL), row(PLE_DIM)]
                     + [full(a) for a in consts],
            out_specs=row(D_MODEL),
            scratch_shapes=[
                pltpu.VMEM((2, TOP_K * tm * ROW_SUB, LANES), F32),
                pltpu.SemaphoreType.DMA((2,)),
            ],
        ),
        out_shape=jax.ShapeDtypeStruct((t, D_MODEL), F32),
        compiler_params=_cparams(("arbitrary",)),
        name="combine",
    )(cnt_tbl, ls_tbl, base_tbl, ys, route, gate_t, x1, p2, *consts)


def _routing_tables(cnt, n_blocks):
    tile_cnt = cnt[:, 0, :N_EXPERTS].astype(jnp.int32)
    counts = jnp.sum(tile_cnt, axis=0)
    padded = (counts + ROW_BLK - 1) // ROW_BLK * ROW_BLK
    pend = jnp.cumsum(padded)
    pstart = pend - padded
    base = pstart[None, :] + jnp.cumsum(tile_cnt, axis=0) - tile_cnt
    lstart = jnp.cumsum(tile_cnt, axis=1) - tile_cnt
    nvalid = pend[-1:] // ROW_BLK
    zoff = jnp.concatenate([pstart + counts, nvalid])
    blk_start = jnp.minimum(jnp.arange(n_blocks, dtype=jnp.int32) * ROW_BLK, pend[-1] - 1)
    blk_e = jnp.minimum(jnp.sum((pend[None, :] <= blk_start[:, None]).astype(jnp.int32), axis=1), N_EXPERTS - 1)
    first = jnp.concatenate([jnp.ones((1,), bool), blk_e[1:] != blk_e[:-1]])
    wslot = (jnp.cumsum(first.astype(jnp.int32)) - 1) % 2
    eids = jnp.arange(N_EXPERTS, dtype=jnp.int32)
    later_nonempty = (eids[None, :] > eids[:, None]) & (padded[None, :] > 0)
    next_e = jnp.min(jnp.where(later_nonempty, eids[None, :], N_EXPERTS), axis=1)
    next_e = jnp.where(next_e == N_EXPERTS, -1, next_e)
    i32 = lambda a: a.reshape(-1).astype(jnp.int32)
    plan = (i32(blk_e), i32(nvalid), i32(first), i32(wslot), i32(next_e[blk_e]))
    return (i32(tile_cnt), i32(lstart), i32(base), i32(zoff)), plan


def kernel(x, p, g_mix, w_in, conv_w, conv_b, dt_bias, a_log, d_skip, ssd_norm_w, lam_q1, lam_k1, lam_q2, lam_k2, subln_w, w_out, g_ffn, w_router, b_router, w_up, b_up, w_down, b_down, g_ple, w_ple_gate, w_ple_proj, g_final):
    bsz, seq, d = x.shape
    t = bsz * seq
    x2 = x.reshape(t, d)
    y_ssd, y_att = _mixer(x2, g_mix[0], w_in[0], conv_w[0], conv_b[0], dt_bias[0], a_log[0], d_skip[0], ssd_norm_w[0],
                          lam_q1[0], lam_k1[0], lam_q2[0], lam_k2[0], subln_w[0], bsz, seq)
    x1, xn, route, gate_t, cnt = _out_proj(x2, y_ssd, y_att, w_out[0], g_ffn[0], w_router[0], b_router[0])

    n_rows = t * TOP_K + N_EXPERTS * ROW_BLK
    tables, plan = _routing_tables(cnt, n_rows // ROW_BLK)
    xs = _dispatch(xn, route, tables, n_rows)
    ys = _experts(xs, plan, w_up[0], b_up[0], w_down[0], b_down[0], n_rows)
    out = _combine(ys, route, gate_t, x1, p[0].reshape(t, PLE_DIM), tables, g_ple[0], w_ple_gate[0], w_ple_proj[0],
                   g_final)
    return out.reshape(bsz, seq, d)
```

```python
import math

import jax
import jax.numpy as jnp
from jax import lax
from jax.experimental import pallas as pl
from jax.experimental.pallas import tpu as pltpu

F32 = jnp.float32
BF16 = jnp.bfloat16

D_MODEL = 1024
PLE_DIM = 256
SSD_WIDTH = 512
ATT_WIDTH = 512
SSD_HEAD_DIM = 64
SSD_HEADS = 8
SSD_GROUPS = 2
SSD_STATE = 128
SSD_CONV = 4
SSD_CHUNK = 128
SSD_CONV_CH = SSD_WIDTH + 2 * SSD_GROUPS * SSD_STATE
SSD_NORM_EPS = 1e-5
ATT_HEAD_DIM = 64
ATT_HEADS = 4
SUBLN_EPS = 1e-5
OFF_Z = 0
OFF_XBC = OFF_Z + SSD_WIDTH
OFF_DT = OFF_XBC + SSD_CONV_CH
OFF_Q = OFF_DT + SSD_HEADS
OFF_K = OFF_Q + ATT_WIDTH
OFF_V = OFF_K + ATT_WIDTH
IN_PROJ = OFF_V + ATT_WIDTH
N_EXPERTS = 32
TOP_K = 4
D_EXPERT = 1024
SWIGLU_LIMIT = 7.0
SWIGLU_ALPHA = 1.702
NORM_EPS = 1e-6
LAM_INIT = 0.8 - 0.6 * math.exp(-0.3 * 0)

LANES = 128
SUBLANES = 8
VMEM_LIMIT_BYTES = 56 * 1024 * 1024
ROW_SUB = D_MODEL // LANES

TM_PROJ = 512
SSD_PAR = 2
ATT_BQ = 512
ATT_BK = 512
ROW_BLK = 256
TM_ROUTE = 256
OUT_SUB = 2
DEINT = 2 * LANES


def _cparams(sem):
    return pltpu.CompilerParams(dimension_semantics=sem, vmem_limit_bytes=VMEM_LIMIT_BYTES)


def _rms(x, w, eps):
    return x * lax.rsqrt(jnp.mean(x * x, axis=-1, keepdims=True) + eps) * w


def _dot(a, b):
    return jnp.dot(a, b, preferred_element_type=F32)


def _dot_nt(a, b):
    return lax.dot_general(a, b, (((1,), (1,)), ((), ())), preferred_element_type=F32)


def _dot_tn(a, b):
    return lax.dot_general(a, b, (((0,), (0,)), ((), ())), preferred_element_type=F32)


def _store_rows(ref, val):
    n = val.shape[0]
    for s in range(ROW_SUB):
        ref[pl.ds(s, n, stride=ROW_SUB), :] = val[:, s * LANES:(s + 1) * LANES]


def _load_rows(ref):
    n = ref.shape[0] // ROW_SUB
    return jnp.concatenate([ref[pl.ds(s, n, stride=ROW_SUB), :] for s in range(ROW_SUB)], axis=-1)


def _rows_spec(n, index_map):
    return pl.BlockSpec((n * ROW_SUB, LANES), index_map)


def _row_slice(start, n):
    return pl.ds(pl.multiple_of(start * ROW_SUB, ROW_SUB), n * ROW_SUB)


def _split3(x):
    hi = x.astype(BF16)
    r1 = x - hi.astype(F32)
    mid = r1.astype(BF16)
    lo = (r1 - mid.astype(F32)).astype(BF16)
    return hi, mid, lo


def _in_proj_kernel(x_ref, g_ref, wz_ref, wxbc_ref, wdt_ref, wq_ref, wk_ref, wv_ref,
                    z_ref, xbc_ref, dt_ref, q_ref, k_ref, v_ref):
    h = _rms(x_ref[...], g_ref[...], NORM_EPS).astype(BF16)
    z_ref[...] = _dot(h, wz_ref[...])
    xbc_ref[...] = _dot(h, wxbc_ref[...])
    dt_ref[...] = _dot(h, wdt_ref[...])
    q_ref[...] = (_dot(h, wq_ref[...]) * (ATT_HEAD_DIM ** -0.5 * math.log2(math.e))).astype(BF16)
    k_ref[...] = _dot(h, wk_ref[...]).astype(BF16)
    v_ref[...] = _dot(h, wv_ref[...]).astype(BF16)


def _in_proj(x2, g_mix, w_in):
    t = x2.shape[0]
    wb = w_in.astype(BF16)
    wz = wb[:, OFF_Z:OFF_XBC]
    wxbc = wb[:, OFF_XBC:OFF_DT]
    wdt = jnp.pad(wb[:, OFF_DT:OFF_Q], ((0, 0), (0, LANES - SSD_HEADS)))
    wq = wb[:, OFF_Q:OFF_K]
    wk = wb[:, OFF_K:OFF_V]
    wv = wb[:, OFF_V:IN_PROJ]
    tm = TM_PROJ
    row = lambda n: pl.BlockSpec((tm, n), lambda i: (i, 0))
    full = lambda a: pl.BlockSpec(a.shape, lambda i: (0, 0))
    return pl.pallas_call(
        _in_proj_kernel,
        grid=(t // tm,),
        in_specs=[row(D_MODEL), full(g_mix), full(wz), full(wxbc), full(wdt), full(wq), full(wk), full(wv)],
        out_specs=[row(SSD_WIDTH), row(SSD_CONV_CH), row(LANES), row(ATT_WIDTH), row(ATT_WIDTH), row(ATT_WIDTH)],
        out_shape=[
            jax.ShapeDtypeStruct((t, SSD_WIDTH), F32),
            jax.ShapeDtypeStruct((t, SSD_CONV_CH), F32),
            jax.ShapeDtypeStruct((t, LANES), F32),
            jax.ShapeDtypeStruct((t, ATT_WIDTH), BF16),
            jax.ShapeDtypeStruct((t, ATT_WIDTH), BF16),
            jax.ShapeDtypeStruct((t, ATT_WIDTH), BF16),
        ],
        compiler_params=_cparams(("arbitrary",)),
        name="in_proj",
    )(x2, g_mix, wz, wxbc, wdt, wq, wk, wv)


def _ssd_kernel(xbc_ref, dtr_ref, z_ref, cw_ref, cb_ref, dtb_ref, alog_ref, dskip_ref, nw_ref,
                y_ref, xpad_ref, state_ref):
    for bb in range(SSD_PAR):
        _ssd_chunk(xbc_ref.at[bb], dtr_ref.at[bb], z_ref.at[bb], cw_ref, cb_ref, dtb_ref, alog_ref, dskip_ref,
                   nw_ref, y_ref.at[bb], xpad_ref.at[bb], state_ref.at[bb])


def _ssd_chunk(xbc_ref, dtr_ref, z_ref, cw_ref, cb_ref, dtb_ref, alog_ref, dskip_ref, nw_ref,
               y_ref, xpad_ref, state_ref):
    L = SSD_CHUNK
    c = pl.program_id(1)

    @pl.when(c == 0)
    def _():
        xpad_ref[0:SUBLANES, :] = jnp.zeros((SUBLANES, SSD_CONV_CH), F32)
        state_ref[...] = jnp.zeros_like(state_ref)

    @pl.when(c != 0)
    def _():
        xpad_ref[0:SUBLANES, :] = xpad_ref[L:L + SUBLANES, :]

    xpad_ref[SUBLANES:SUBLANES + L, :] = xbc_ref[...]

    conv = cb_ref[...]
    for j in range(SSD_CONV):
        off = SUBLANES - (SSD_CONV - 1) + j
        conv = conv + cw_ref[j:j + 1, :] * xpad_ref[off:off + L, :]
    act = conv * jax.nn.sigmoid(conv)
    xs = act[:, :SSD_WIDTH]
    bm = act[:, SSD_WIDTH:SSD_WIDTH + SSD_GROUPS * SSD_STATE].astype(BF16)
    cm = act[:, SSD_WIDTH + SSD_GROUPS * SSD_STATE:].astype(BF16)

    dt_in = dtr_ref[...] + dtb_ref[...]
    dt = jnp.maximum(dt_in, 0.0) + jnp.log1p(jnp.exp(-jnp.abs(dt_in)))
    adt = dt * (-jnp.exp(alog_ref[...]))

    ri = lax.broadcasted_iota(jnp.int32, (L, L), 0)
    ci = lax.broadcasted_iota(jnp.int32, (L, L), 1)
    causal = ci <= ri
    tril = jnp.where(causal, 1.0, 0.0).astype(BF16)
    hi, mid, lo = _split3(adt)
    acum = _dot(tril, hi) + _dot(tril, mid) + _dot(tril, lo)
    acum_t = acum.T
    a_last = acum[L - 1:L, :]
    decay_in = jnp.exp(a_last - acum)
    decay_out = jnp.exp(acum)
    chunk_decay = jnp.exp(a_last)

    lane = lax.broadcasted_iota(jnp.int32, (L, LANES), 1)
    lo_half = lane < SSD_HEAD_DIM

    def per_pair(col_a, col_b):
        return jnp.where(lo_half, col_a, col_b)

    ys = []
    for pair in range(SSD_HEADS // 2):
        g = pair // 2
        h0, h1 = 2 * pair, 2 * pair + 1
        cg = cm[:, g * SSD_STATE:(g + 1) * SSD_STATE]
        bg = bm[:, g * SSD_STATE:(g + 1) * SSD_STATE]
        cb = _dot_nt(cg, bg)
        x_pair = xs[:, pair * LANES:(pair + 1) * LANES]
        xdt = x_pair * per_pair(dt[:, h0:h0 + 1], dt[:, h1:h1 + 1])
        y_pair = jnp.zeros((L, LANES), F32)
        for hh, keep in ((h0, lo_half), (h1, jnp.logical_not(lo_half))):
            seg = acum[:, hh:hh + 1] - acum_t[hh:hh + 1, :]
            lmat = jnp.where(causal, jnp.exp(jnp.where(causal, seg, 0.0)), 0.0)
            m = (cb * lmat).astype(BF16)
            y_pair = y_pair + _dot(m, jnp.where(keep, xdt, 0.0).astype(BF16))
        s_prev = state_ref[pair]
        y_off = _dot(cg, s_prev.astype(BF16)) * per_pair(decay_out[:, h0:h0 + 1], decay_out[:, h1:h1 + 1])
        w_in = (xdt * per_pair(decay_in[:, h0:h0 + 1], decay_in[:, h1:h1 + 1])).astype(BF16)
        cd = jnp.where(lane[0:1, :] < SSD_HEAD_DIM, chunk_decay[:, h0:h0 + 1], chunk_decay[:, h1:h1 + 1])
        state_ref[pair] = s_prev * cd + _dot_tn(bg, w_in)
        ys.append(y_pair + y_off + dskip_ref[:, pair * LANES:(pair + 1) * LANES] * x_pair)

    y = jnp.concatenate(ys, axis=-1)
    zz = z_ref[...]
    y = y * (zz * jax.nn.sigmoid(zz))
    gw = SSD_WIDTH // SSD_GROUPS
    outs = []
    for g in range(SSD_GROUPS):
        yg = y[:, g * gw:(g + 1) * gw]
        outs.append(yg * lax.rsqrt(jnp.mean(yg * yg, axis=-1, keepdims=True) + SSD_NORM_EPS))
    y_ref[...] = (jnp.concatenate(outs, axis=-1) * nw_ref[...]).astype(BF16)


def _ssd(xbc, dtr, z, conv_w, conv_b, dt_bias, a_log, d_skip, norm_w, bsz, seq):
    nc = seq // SSD_CHUNK
    pad_h = lambda v: jnp.pad(v.reshape(1, SSD_HEADS), ((0, 0), (0, LANES - SSD_HEADS)))
    dskip_lanes = jnp.repeat(d_skip, SSD_HEAD_DIM).reshape(1, SSD_WIDTH)
    seqs = lambda a: a.reshape(bsz, seq, a.shape[-1])
    row = lambda n: pl.BlockSpec((SSD_PAR, SSD_CHUNK, n), lambda b, c: (b, c, 0))
    full = lambda a: pl.BlockSpec(a.shape, lambda b, c: (0, 0))
    args = (seqs(xbc), seqs(dtr), seqs(z), conv_w, conv_b.reshape(1, -1), pad_h(dt_bias), pad_h(a_log), dskip_lanes,
            norm_w.reshape(1, -1))
    y = pl.pallas_call(
        _ssd_kernel,
        grid=(bsz // SSD_PAR, nc),
        in_specs=[row(SSD_CONV_CH), row(LANES), row(SSD_WIDTH)] + [full(a) for a in args[3:]],
        out_specs=row(SSD_WIDTH),
        out_shape=jax.ShapeDtypeStruct((bsz, seq, SSD_WIDTH), BF16),
        scratch_shapes=[
            pltpu.VMEM((SSD_PAR, SSD_CHUNK + 2 * SUBLANES, SSD_CONV_CH), F32),
            pltpu.VMEM((SSD_PAR, SSD_HEADS // 2, SSD_STATE, LANES), F32),
        ],
        compiler_params=_cparams(("arbitrary", "arbitrary")),
        name="ssd",
    )(*args)
    return y.reshape(bsz * seq, SSD_WIDTH)


def _attn_kernel(q_ref, k_ref, v_ref, lq1_ref, lk1_ref, lq2_ref, lk2_ref, sw_ref, o_ref, s_scr):
    bq, bk = ATT_BQ, ATT_BK
    qi = pl.program_id(2)
    n_maps = 2

    def fold(t, op):
        out = t[:, :LANES]
        for c in range(1, bk // LANES):
            out = op(out, t[:, c * LANES:(c + 1) * LANES])
        return out

    def merge(old, new, op):
        return new if old is None else op(old, new)

    def attend(nk):
        q = q_ref[...]
        lane = lax.broadcasted_iota(jnp.int32, (bq, LANES), 1)
        zero = jnp.zeros_like(q)
        q_maps = (jnp.where(lane < ATT_HEAD_DIM, q, zero), jnp.where(lane >= ATT_HEAD_DIM, q, zero))
        r = lax.broadcasted_iota(jnp.int32, (bq, bk), 0)
        c = lax.broadcasted_iota(jnp.int32, (bq, bk), 1)
        causal = c <= r

        mt = [None] * n_maps
        for j in range(nk):
            kb = k_ref[j * bk:(j + 1) * bk, :]
            for m in range(n_maps):
                s = _dot_nt(q_maps[m], kb)
                if j == nk - 1:
                    s = jnp.where(causal, s, -jnp.inf)
                s_scr[m, j] = s
                mt[m] = merge(mt[m], fold(s, jnp.maximum), jnp.maximum)
        row_max = [jnp.max(t, axis=-1, keepdims=True) for t in mt]

        lt = [None] * n_maps
        acc = [None] * n_maps
        for j in range(nk):
            vb = v_ref[j * bk:(j + 1) * bk, :]
            for m in range(n_maps):
                p = jnp.exp2(s_scr[m, j] - row_max[m])
                lt[m] = merge(lt[m], fold(p, jnp.add), jnp.add)
                acc[m] = merge(acc[m], _dot(p.astype(BF16), vb), jnp.add)

        lam = (jnp.exp(jnp.sum(lq1_ref[...] * lk1_ref[...], axis=-1, keepdims=True))
               - jnp.exp(jnp.sum(lq2_ref[...] * lk2_ref[...], axis=-1, keepdims=True)) + LAM_INIT)
        l1 = jnp.sum(lt[0], axis=-1, keepdims=True)
        l2 = jnp.sum(lt[1], axis=-1, keepdims=True)
        o = acc[0] / l1 - lam * (acc[1] / l2)
        o_ref[...] = (_rms(o, sw_ref[...], SUBLN_EPS) * (1.0 - LAM_INIT)).astype(BF16)

    for nk in range(1, k_ref.shape[0] // bk + 1):
        pl.when(qi == nk - 1)(lambda nk=nk: attend(nk))


def _attn(q, k, v, lam_q1, lam_k1, lam_q2, lam_k2, subln_w, bsz, seq):
    nq = seq // ATT_BQ
    qspec = pl.BlockSpec((ATT_BQ, LANES), lambda b, h, i: (b * nq + i, h))
    kvspec = pl.BlockSpec((seq, LANES), lambda b, h, i: (b, h))
    full = lambda a: pl.BlockSpec(a.shape, lambda b, h, i: (0, 0))
    lams = [a.reshape(1, -1) for a in (lam_q1, lam_k1, lam_q2, lam_k2)]
    sw = subln_w.reshape(1, -1)
    return pl.pallas_call(
        _attn_kernel,
        grid=(bsz, ATT_HEADS, nq),
        in_specs=[qspec, kvspec, kvspec] + [full(a) for a in lams] + [full(sw)],
        out_specs=qspec,
        out_shape=jax.ShapeDtypeStruct((bsz * seq, ATT_WIDTH), BF16),
        scratch_shapes=[pltpu.VMEM((2, seq // ATT_BK, ATT_BQ, ATT_BK), F32)],
        compiler_params=_cparams(("arbitrary", "arbitrary", "arbitrary")),
        name="attn",
    )(q, k, v, *lams, sw)


def _mixer(x2, g_mix, w_in, conv_w, conv_b, dt_bias, a_log, d_skip, ssd_norm_w,
           lam_q1, lam_k1, lam_q2, lam_k2, subln_w, bsz, seq):
    z, xbc, dtr, q, k, v = _in_proj(x2, g_mix.reshape(1, -1), w_in)
    y_ssd = _ssd(xbc, dtr, z, conv_w, conv_b, dt_bias, a_log, d_skip, ssd_norm_w, bsz, seq)
    y_att = _attn(q, k, v, lam_q1, lam_k1, lam_q2, lam_k2, subln_w, bsz, seq)
    return y_ssd, y_att


def _out_proj_kernel(x_ref, ys_ref, ya_ref, wo_ref, g_ref, wr2_ref, br_ref,
                     x1_ref, xn_ref, route_ref, gate_ref, cnt_ref):
    tm = TM_ROUTE
    x1 = x_ref[...] + _dot(jnp.concatenate([ys_ref[...], ya_ref[...]], axis=-1), wo_ref[...])
    x1_ref[...] = x1
    xn_all = _rms(x1, g_ref[...], NORM_EPS)
    xn_ref[...] = xn_all.astype(BF16)

    n_tok = OUT_SUB * tm
    xh, xm, _ = _split3(xn_all)
    lg2 = _dot(xh, wr2_ref[...]) + _dot(xm, wr2_ref[...])
    logits = (lg2[:, :LANES] + lg2[:, LANES:]).T[:N_EXPERTS, :] + br_ref[...]

    eidx = lax.broadcasted_iota(jnp.int32, (N_EXPERTS, n_tok), 0).astype(F32)
    work = logits
    vals, idxs, hots = [], [], []
    for _ in range(TOP_K):
        m = jnp.max(work, axis=0, keepdims=True)
        idx = jnp.min(jnp.where(work == m, eidx, float(N_EXPERTS)), axis=0, keepdims=True)
        hot = eidx == idx
        vals.append(m)
        idxs.append(idx.astype(jnp.int32))
        hots.append(hot)
        work = jnp.where(hot, -jnp.inf, work)
    exps = [jnp.exp(v - vals[0]) for v in vals]
    denom = exps[0] + exps[1] + exps[2] + exps[3]
    gates = [e / denom for e in exps]

    cnt = jnp.zeros((N_EXPERTS, n_tok), F32)
    for hot in hots:
        cnt = cnt + jnp.where(hot, 1.0, 0.0)
    cnt_b = cnt.astype(BF16)
    r = lax.broadcasted_iota(jnp.int32, (tm, tm), 0)
    c = lax.broadcasted_iota(jnp.int32, (tm, tm), 1)
    earlier_tok = jnp.where(r < c, 1.0, 0.0).astype(BF16)
    er = lax.broadcasted_iota(jnp.int32, (N_EXPERTS, N_EXPERTS), 0)
    ec = lax.broadcasted_iota(jnp.int32, (N_EXPERTS, N_EXPERTS), 1)
    lower_exp = jnp.where(ec < er, 1.0, 0.0).astype(BF16)
    below = _dot(lower_exp, cnt_b)
    cnt_pad = jnp.concatenate([cnt_b, jnp.zeros((LANES - N_EXPERTS, n_tok), BF16)], axis=0)
    pos = []
    for sub in range(OUT_SUB):
        cols = slice(sub * tm, (sub + 1) * tm)
        pos.append(_dot(cnt_b[:, cols], earlier_tok) + jnp.sum(below[:, cols], axis=1, keepdims=True))
        cnt_ref[sub] = _dot_nt(jnp.ones((SUBLANES, tm), BF16), cnt_pad[:, cols])
    pos = jnp.concatenate(pos, axis=1)
    lps = [jnp.sum(jnp.where(hot, pos, 0.0), axis=0, keepdims=True).astype(jnp.int32) for hot in hots]

    route_ref[...] = jnp.concatenate(idxs + lps, axis=0)
    gate_ref[...] = jnp.concatenate(gates + [jnp.zeros((SUBLANES - TOP_K, n_tok), F32)], axis=0)


def _out_proj(x2, y_ssd, y_att, w_out, g_ffn, w_router, b_router):
    t = x2.shape[0]
    tm = OUT_SUB * TM_ROUTE
    nt = t // TM_ROUTE
    wo = w_out.astype(BF16)
    wrh, wrm, _ = _split3(jnp.pad(w_router, ((0, 0), (0, LANES - N_EXPERTS))))
    wr2 = jnp.concatenate([wrh, wrm], axis=1)
    br = b_router.reshape(N_EXPERTS, 1)
    row = lambda n: pl.BlockSpec((tm, n), lambda i: (i, 0))
    col = pl.BlockSpec((SUBLANES, tm), lambda i: (0, i))
    full = lambda a: pl.BlockSpec(a.shape, lambda i: (0, 0))
    args = (x2, y_ssd, y_att, wo, g_ffn.reshape(1, -1), wr2, br)
    return pl.pallas_call(
        _out_proj_kernel,
        grid=(t // tm,),
        in_specs=[row(D_MODEL), row(SSD_WIDTH), row(ATT_WIDTH)] + [full(a) for a in args[3:]],
        out_specs=[row(D_MODEL), row(D_MODEL), col, col,
                   pl.BlockSpec((OUT_SUB, SUBLANES, LANES), lambda i: (i, 0, 0))],
        out_shape=[
            jax.ShapeDtypeStruct((t, D_MODEL), F32),
            jax.ShapeDtypeStruct((t, D_MODEL), BF16),
            jax.ShapeDtypeStruct((SUBLANES, t), jnp.int32),
            jax.ShapeDtypeStruct((SUBLANES, t), F32),
            jax.ShapeDtypeStruct((nt, SUBLANES, LANES), F32),
        ],
        compiler_params=_cparams(("arbitrary",)),
        name="out_proj",
    )(*args)


def _copy_run(n, start_copy):
    @pl.when(n > 0)
    def _():
        start_copy(n)


def _one_hot_positions(route_ref, n_pos, tm):
    r = lax.broadcasted_iota(jnp.int32, (n_pos, tm), 0)
    return [r == route_ref[TOP_K + k:TOP_K + k + 1, :] for k in range(TOP_K)]


def _dispatch_kernel(cnt_ref, ls_ref, base_ref, zoff_ref, xn_ref, route_ref, xs_hbm, xloc, zeros_vmem, sem_z, sems):
    tm = TM_ROUTE
    n_pos = TOP_K * tm
    i = pl.program_id(0)
    n_steps = pl.num_programs(0)
    slot = lax.rem(i, 2)

    @pl.when(i == 0)
    def _():
        zeros_vmem[...] = jnp.zeros_like(zeros_vmem)

        def zfill(e, _):
            pltpu.make_async_copy(zeros_vmem, xs_hbm.at[_row_slice(zoff_ref[e], ROW_BLK)], sem_z).start()
            return 0

        def zwait(e, _):
            pltpu.make_async_copy(zeros_vmem, xs_hbm.at[_row_slice(0, ROW_BLK)], sem_z).wait()
            return 0

        def tfill(b, _):
            pltpu.make_async_copy(zeros_vmem, xs_hbm.at[_row_slice(b * ROW_BLK, ROW_BLK)], sem_z).start()
            return 0

        lax.fori_loop(0, N_EXPERTS, zfill, 0)
        lax.fori_loop(0, N_EXPERTS, zwait, 0)
        n_blk_total = xs_hbm.shape[0] // (ROW_BLK * ROW_SUB)
        lax.fori_loop(zoff_ref[N_EXPERTS], n_blk_total, tfill, 0)
        lax.fori_loop(zoff_ref[N_EXPERTS], n_blk_total, zwait, 0)

    masks = _one_hot_positions(route_ref, n_pos, tm)
    sel = jnp.where(masks[0] | masks[1] | masks[2] | masks[3], 1.0, 0.0).astype(BF16)
    rows = _dot(sel, xn_ref[...])

    def slot_wait(s):
        pltpu.make_async_copy(xloc.at[s], xs_hbm.at[_row_slice(0, n_pos)], sems.at[s]).wait()

    @pl.when(i >= 2)
    def _():
        slot_wait(slot)

    _store_rows(xloc.at[slot], rows)

    def per_expert(e, _):
        idx = i * N_EXPERTS + e
        src0 = ls_ref[idx]
        dst0 = base_ref[idx]

        def start_copy(n):
            pltpu.make_async_copy(xloc.at[slot, _row_slice(src0, n)],
                                  xs_hbm.at[_row_slice(dst0, n)], sems.at[slot]).start()

        _copy_run(cnt_ref[idx], start_copy)
        return 0

    lax.fori_loop(0, N_EXPERTS, per_expert, 0)

    @pl.when(i == n_steps - 1)
    def _():
        slot_wait(slot)

        @pl.when(n_steps > 1)
        def _():
            slot_wait(1 - slot)


def _dispatch(xn, route, tables, n_rows):
    t = xn.shape[0]
    tm = TM_ROUTE
    cnt_tbl, ls_tbl, base_tbl, zoff = tables
    return pl.pallas_call(
        _dispatch_kernel,
        grid_spec=pltpu.PrefetchScalarGridSpec(
            num_scalar_prefetch=4,
            grid=(t // tm,),
            in_specs=[pl.BlockSpec((tm, D_MODEL), lambda i, *_: (i, 0)),
                      pl.BlockSpec((SUBLANES, tm), lambda i, *_: (0, i))],
            out_specs=pl.BlockSpec(memory_space=pl.ANY),
            scratch_shapes=[
                pltpu.VMEM((2, TOP_K * tm * ROW_SUB, LANES), F32),
                pltpu.VMEM((ROW_BLK * ROW_SUB, LANES), F32),
                pltpu.SemaphoreType.DMA,
                pltpu.SemaphoreType.DMA((2,)),
            ],
        ),
        out_shape=jax.ShapeDtypeStruct(((n_rows + ROW_BLK) * ROW_SUB, LANES), F32),
        compiler_params=_cparams(("arbitrary",)),
        name="dispatch",
    )(cnt_tbl, ls_tbl, base_tbl, zoff, xn, route)


def _experts_kernel(blk_e_ref, nvalid_ref, first_ref, wslot_ref, enext_ref, xs_ref, wup_hbm, wdn_hbm,
                    bg_ref, bu_ref, bd_ref, ys_ref, wup_buf, wdn_buf, wg_s, wu_s, wd_s, sems):
    i = pl.program_id(0)
    slot = wslot_ref[i]

    def weight_copies(e, s):
        return (pltpu.make_async_copy(wup_hbm.at[e], wup_buf.at[s], sems.at[0, s]),
                pltpu.make_async_copy(wdn_hbm.at[e], wdn_buf.at[s], sems.at[1, s]))

    @pl.when(i == 0)
    def _():
        for cp in weight_copies(blk_e_ref[0], slot):
            cp.start()

    @pl.when(first_ref[i] != 0)
    def _():
        for cp in weight_copies(blk_e_ref[i], slot):
            cp.wait()

        @pl.when(enext_ref[i] >= 0)
        def _():
            for cp in weight_copies(enext_ref[i], 1 - slot):
                cp.start()

        src = lax.broadcasted_iota(jnp.int32, (DEINT, DEINT), 0)
        dst = lax.broadcasted_iota(jnp.int32, (DEINT, DEINT), 1)
        perm = jnp.where(src == jnp.where(dst < LANES, 2 * dst, 2 * (dst - LANES) + 1), 1.0, 0.0).astype(BF16)
        for g in range(2 * D_EXPERT // DEINT):
            sep = _dot(wup_buf[slot, :, g * DEINT:(g + 1) * DEINT].astype(BF16), perm)
            wg_s[:, g * LANES:(g + 1) * LANES] = sep[:, :LANES].astype(BF16)
            wu_s[:, g * LANES:(g + 1) * LANES] = sep[:, LANES:].astype(BF16)
        wd_s[...] = wdn_buf[slot].astype(BF16)

    @pl.when(i < nvalid_ref[0])
    def _():
        xb = _load_rows(xs_ref).astype(BF16)
        gate = jnp.minimum(_dot(xb, wg_s[...]) + bg_ref[0], SWIGLU_LIMIT)
        up = jnp.clip(_dot(xb, wu_s[...]) + bu_ref[0], -SWIGLU_LIMIT, SWIGLU_LIMIT)
        act = (up + 1.0) * gate * jax.nn.sigmoid(SWIGLU_ALPHA * gate)
        _store_rows(ys_ref, _dot(act.astype(BF16), wd_s[...]) + bd_ref[0])

    @pl.when(i >= nvalid_ref[0])
    def _():
        ys_ref[...] = jnp.zeros_like(ys_ref)


def _experts(xs, plan, w_up, b_up, w_down, b_down, n_rows):
    nb = n_rows // ROW_BLK
    bg = b_up[:, 0::2].reshape(N_EXPERTS, 1, D_EXPERT)
    bu = b_up[:, 1::2].reshape(N_EXPERTS, 1, D_EXPERT)
    bd = b_down.reshape(N_EXPERTS, 1, D_MODEL)
    blk_e, nvalid, first, wslot, enext = plan
    src = lambda i, be, nv, *_: (jnp.minimum(i, nv[0] - 1), 0)
    bspec = lambda m: pl.BlockSpec((1, 1, m), lambda i, be, *_: (be[i], 0, 0))
    anyspec = pl.BlockSpec(memory_space=pl.ANY)
    return pl.pallas_call(
        _experts_kernel,
        grid_spec=pltpu.PrefetchScalarGridSpec(
            num_scalar_prefetch=5,
            grid=(nb,),
            in_specs=[_rows_spec(ROW_BLK, src), anyspec, anyspec,
                      bspec(D_EXPERT), bspec(D_EXPERT), bspec(D_MODEL)],
            out_specs=_rows_spec(ROW_BLK, lambda i, *_: (i, 0)),
            scratch_shapes=[
                pltpu.VMEM((2, D_MODEL, 2 * D_EXPERT), F32),
                pltpu.VMEM((2, D_EXPERT, D_MODEL), F32),
                pltpu.VMEM((D_MODEL, D_EXPERT), BF16),
                pltpu.VMEM((D_MODEL, D_EXPERT), BF16),
                pltpu.VMEM((D_EXPERT, D_MODEL), BF16),
                pltpu.SemaphoreType.DMA((2, 2)),
            ],
        ),
        out_shape=jax.ShapeDtypeStruct((n_rows * ROW_SUB, LANES), F32),
        compiler_params=_cparams(("arbitrary",)),
        name="experts",
    )(blk_e, nvalid, first, wslot, enext, xs, w_up, w_down, bg, bu, bd)


def _combine_kernel(cnt_ref, ls_ref, base_ref, ys_hbm, route_ref, gate_ref, x1_ref, p_ref, gp_ref, wpg_ref,
                    wpp_ref, gf_ref, o_ref, yloc, sems):
    tm = TM_ROUTE
    n_pos = TOP_K * tm
    i = pl.program_id(0)
    n_steps = pl.num_programs(0)
    slot = lax.rem(i, 2)

    def gather_tile(tile, s):
        def per_expert(e, _):
            idx = tile * N_EXPERTS + e
            src0 = base_ref[idx]
            dst0 = ls_ref[idx]

            def start_copy(n):
                pltpu.make_async_copy(ys_hbm.at[_row_slice(src0, n)],
                                      yloc.at[s, _row_slice(dst0, n)], sems.at[s]).start()

            _copy_run(cnt_ref[idx], start_copy)
            return 0

        lax.fori_loop(0, N_EXPERTS, per_expert, 0)

    @pl.when(i == 0)
    def _():
        gather_tile(0, 0)

    @pl.when(i + 1 < n_steps)
    def _():
        gather_tile(i + 1, 1 - slot)

    masks = _one_hot_positions(route_ref, n_pos, tm)
    sel = jnp.where(masks[0] | masks[1] | masks[2] | masks[3], 1.0, 0.0).astype(BF16)
    gsel = jnp.zeros((n_pos, tm), F32)
    for k in range(TOP_K):
        gsel = gsel + jnp.where(masks[k], gate_ref[k:k + 1, :], 0.0)
    g_pos = jnp.sum(gsel, axis=1, keepdims=True)
    pp = _dot(p_ref[...].astype(BF16), wpp_ref[...])

    pltpu.make_async_copy(ys_hbm.at[_row_slice(0, n_pos)], yloc.at[slot], sems.at[slot]).wait()
    y_gated = (_load_rows(yloc.at[slot]) * g_pos).astype(BF16)
    x2 = x1_ref[...] + _dot_tn(sel, y_gated)
    xn = _rms(x2, gp_ref[...], NORM_EPS).astype(BF16)
    x3 = x2 + pp * jax.nn.sigmoid(_dot(xn, wpg_ref[...]))
    o_ref[...] = _rms(x3, gf_ref[...], NORM_EPS)


def _combine(ys, route, gate_t, x1, p2, tables, g_ple, w_ple_gate, w_ple_proj, g_final):
    t = x1.shape[0]
    tm = TM_ROUTE
    cnt_tbl, ls_tbl, base_tbl, _ = tables
    row = lambda n: pl.BlockSpec((tm, n), lambda i, *_: (i, 0))
    col = pl.BlockSpec((SUBLANES, tm), lambda i, *_: (0, i))
    full = lambda a: pl.BlockSpec(a.shape, lambda i, *_: (0, 0))
    consts = (g_ple.reshape(1, -1), w_ple_gate.astype(BF16), w_ple_proj.astype(BF16), g_final.reshape(1, -1))
    return pl.pallas_call(
        _combine_kernel,
        grid_spec=pltpu.PrefetchScalarGridSpec(
            num_scalar_prefetch=3,
            grid=(t // tm,),
            in_specs=[pl.BlockSpec(memory_space=pl.ANY), col, col, row(D_MODEL), row(PLE_DIM)]
                     + [full(a) for a in consts],
            out_specs=row(D_MODEL),
            scratch_shapes=[
                pltpu.VMEM((2, TOP_K * tm * ROW_SUB, LANES), F32),
                pltpu.SemaphoreType.DMA((2,)),
            ],
        ),
        out_shape=jax.ShapeDtypeStruct((t, D_MODEL), F32),
        compiler_params=_cparams(("arbitrary",)),
        name="combine",
    )(cnt_tbl, ls_tbl, base_tbl, ys, route, gate_t, x1, p2, *consts)


def _routing_tables(cnt, n_blocks):
    tile_cnt = cnt[:, 0, :N_EXPERTS].astype(jnp.int32)
    counts = jnp.sum(tile_cnt, axis=0)
    padded = (counts + ROW_BLK - 1) // ROW_BLK * ROW_BLK
    pend = jnp.cumsum(padded)
    pstart = pend - padded
    base = pstart[None, :] + jnp.cumsum(tile_cnt, axis=0) - tile_cnt
    lstart = jnp.cumsum(tile_cnt, axis=1) - tile_cnt
    nvalid = pend[-1:] // ROW_BLK
    zoff = jnp.concatenate([pstart + counts, nvalid])
    blk_start = jnp.minimum(jnp.arange(n_blocks, dtype=jnp.int32) * ROW_BLK, pend[-1] - 1)
    blk_e = jnp.minimum(jnp.sum((pend[None, :] <= blk_start[:, None]).astype(jnp.int32), axis=1), N_EXPERTS - 1)
    first = jnp.concatenate([jnp.ones((1,), bool), blk_e[1:] != blk_e[:-1]])
    wslot = (jnp.cumsum(first.astype(jnp.int32)) - 1) % 2
    eids = jnp.arange(N_EXPERTS, dtype=jnp.int32)
    later_nonempty = (eids[None, :] > eids[:, None]) & (padded[None, :] > 0)
    next_e = jnp.min(jnp.where(later_nonempty, eids[None, :], N_EXPERTS), axis=1)
    next_e = jnp.where(next_e == N_EXPERTS, -1, next_e)
    i32 = lambda a: a.reshape(-1).astype(jnp.int32)
    plan = (i32(blk_e), i32(nvalid), i32(first), i32(wslot), i32(next_e[blk_e]))
    return (i32(tile_cnt), i32(lstart), i32(base), i32(zoff)), plan


def kernel(x, p, g_mix, w_in, conv_w, conv_b, dt_bias, a_log, d_skip, ssd_norm_w, lam_q1, lam_k1, lam_q2, lam_k2, subln_w, w_out, g_ffn, w_router, b_router, w_up, b_up, w_down, b_down, g_ple, w_ple_gate, w_ple_proj, g_final):
    bsz, seq, d = x.shape
    t = bsz * seq
    x2 = x.reshape(t, d)
    y_ssd, y_att = _mixer(x2, g_mix[0], w_in[0], conv_w[0], conv_b[0], dt_bias[0], a_log[0], d_skip[0], ssd_norm_w[0],
                          lam_q1[0], lam_k1[0], lam_q2[0], lam_k2[0], subln_w[0], bsz, seq)
    x1, xn, route, gate_t, cnt = _out_proj(x2, y_ssd, y_att, w_out[0], g_ffn[0], w_router[0], b_router[0])

    n_rows = t * TOP_K + N_EXPERTS * ROW_BLK
    tables, plan = _routing_tables(cnt, n_rows // ROW_BLK)
    xs = _dispatch(xn, route, tables, n_rows)
    ys = _experts(xs, plan, w_up[0], b_up[0], w_down[0], b_down[0], n_rows)
    out = _combine(ys, route, gate_t, x1, p[0].reshape(t, PLE_DIM), tables, g_ple[0], w_ple_gate[0], w_ple_proj[0],
                   g_final)
    return out.reshape(bsz, seq, d)
```

```python
import math

import jax
import jax.numpy as jnp
from jax import lax
from jax.experimental import pallas as pl
from jax.experimental.pallas import tpu as pltpu

F32 = jnp.float32
BF16 = jnp.bfloat16

D_MODEL = 1024
PLE_DIM = 256
SSD_WIDTH = 512
ATT_WIDTH = 512
SSD_HEAD_DIM = 64
SSD_HEADS = 8
SSD_GROUPS = 2
SSD_STATE = 128
SSD_CONV = 4
SSD_CHUNK = 128
SSD_CONV_CH = SSD_WIDTH + 2 * SSD_GROUPS * SSD_STATE
SSD_NORM_EPS = 1e-5
ATT_HEAD_DIM = 64
ATT_HEADS = 4
SUBLN_EPS = 1e-5
OFF_Z = 0
OFF_XBC = OFF_Z + SSD_WIDTH
OFF_DT = OFF_XBC + SSD_CONV_CH
OFF_Q = OFF_DT + SSD_HEADS
OFF_K = OFF_Q + ATT_WIDTH
OFF_V = OFF_K + ATT_WIDTH
IN_PROJ = OFF_V + ATT_WIDTH
N_EXPERTS = 32
TOP_K = 4
D_EXPERT = 1024
SWIGLU_LIMIT = 7.0
SWIGLU_ALPHA = 1.702
NORM_EPS = 1e-6
LAM_INIT = 0.8 - 0.6 * math.exp(-0.3 * 0)

LANES = 128
SUBLANES = 8
VMEM_LIMIT_BYTES = 56 * 1024 * 1024
ROW_SUB = D_MODEL // LANES

TM_PROJ = 512
SSD_PAR = 2
ATT_BQ = 512
ATT_BK = 512
ROW_BLK = 256
TM_ROUTE = 256
OUT_SUB = 4
DEINT = 2 * LANES


def _cparams(sem):
    return pltpu.CompilerParams(dimension_semantics=sem, vmem_limit_bytes=VMEM_LIMIT_BYTES)


def _rms(x, w, eps):
    return x * lax.rsqrt(jnp.mean(x * x, axis=-1, keepdims=True) + eps) * w


def _dot(a, b):
    return jnp.dot(a, b, preferred_element_type=F32)


def _dot_nt(a, b):
    return lax.dot_general(a, b, (((1,), (1,)), ((), ())), preferred_element_type=F32)


def _dot_tn(a, b):
    return lax.dot_general(a, b, (((0,), (0,)), ((), ())), preferred_element_type=F32)


def _store_rows(ref, val):
    n = val.shape[0]
    for s in range(ROW_SUB):
        ref[pl.ds(s, n, stride=ROW_SUB), :] = val[:, s * LANES:(s + 1) * LANES]


def _load_rows(ref):
    n = ref.shape[0] // ROW_SUB
    return jnp.concatenate([ref[pl.ds(s, n, stride=ROW_SUB), :] for s in range(ROW_SUB)], axis=-1)


def _rows_spec(n, index_map):
    return pl.BlockSpec((n * ROW_SUB, LANES), index_map)


def _row_slice(start, n):
    return pl.ds(pl.multiple_of(start * ROW_SUB, ROW_SUB), n * ROW_SUB)


def _split3(x):
    hi = x.astype(BF16)
    r1 = x - hi.astype(F32)
    mid = r1.astype(BF16)
    lo = (r1 - mid.astype(F32)).astype(BF16)
    return hi, mid, lo


def _in_proj_kernel(x_ref, g_ref, wz_ref, wxbc_ref, wdt_ref, wq_ref, wk_ref, wv_ref,
                    z_ref, xbc_ref, dt_ref, q_ref, k_ref, v_ref):
    h = _rms(x_ref[...], g_ref[...], NORM_EPS).astype(BF16)
    z_ref[...] = _dot(h, wz_ref[...])
    xbc_ref[...] = _dot(h, wxbc_ref[...])
    dt_ref[...] = _dot(h, wdt_ref[...])
    q_ref[...] = (_dot(h, wq_ref[...]) * (ATT_HEAD_DIM ** -0.5 * math.log2(math.e))).astype(BF16)
    k_ref[...] = _dot(h, wk_ref[...]).astype(BF16)
    v_ref[...] = _dot(h, wv_ref[...]).astype(BF16)


def _in_proj(x2, g_mix, w_in):
    t = x2.shape[0]
    wb = w_in.astype(BF16)
    wz = wb[:, OFF_Z:OFF_XBC]
    wxbc = wb[:, OFF_XBC:OFF_DT]
    wdt = jnp.pad(wb[:, OFF_DT:OFF_Q], ((0, 0), (0, LANES - SSD_HEADS)))
    wq = wb[:, OFF_Q:OFF_K]
    wk = wb[:, OFF_K:OFF_V]
    wv = wb[:, OFF_V:IN_PROJ]
    tm = TM_PROJ
    row = lambda n: pl.BlockSpec((tm, n), lambda i: (i, 0))
    full = lambda a: pl.BlockSpec(a.shape, lambda i: (0, 0))
    return pl.pallas_call(
        _in_proj_kernel,
        grid=(t // tm,),
        in_specs=[row(D_MODEL), full(g_mix), full(wz), full(wxbc), full(wdt), full(wq), full(wk), full(wv)],
        out_specs=[row(SSD_WIDTH), row(SSD_CONV_CH), row(LANES), row(ATT_WIDTH), row(ATT_WIDTH), row(ATT_WIDTH)],
        out_shape=[
            jax.ShapeDtypeStruct((t, SSD_WIDTH), F32),
            jax.ShapeDtypeStruct((t, SSD_CONV_CH), F32),
            jax.ShapeDtypeStruct((t, LANES), F32),
            jax.ShapeDtypeStruct((t, ATT_WIDTH), BF16),
            jax.ShapeDtypeStruct((t, ATT_WIDTH), BF16),
            jax.ShapeDtypeStruct((t, ATT_WIDTH), BF16),
        ],
        compiler_params=_cparams(("arbitrary",)),
        name="in_proj",
    )(x2, g_mix, wz, wxbc, wdt, wq, wk, wv)


def _ssd_kernel(xbc_ref, dtr_ref, z_ref, cw_ref, cb_ref, dtb_ref, alog_ref, dskip_ref, nw_ref,
                y_ref, xpad_ref, state_ref):
    for bb in range(SSD_PAR):
        _ssd_chunk(xbc_ref.at[bb], dtr_ref.at[bb], z_ref.at[bb], cw_ref, cb_ref, dtb_ref, alog_ref, dskip_ref,
                   nw_ref, y_ref.at[bb], xpad_ref.at[bb], state_ref.at[bb])


def _ssd_chunk(xbc_ref, dtr_ref, z_ref, cw_ref, cb_ref, dtb_ref, alog_ref, dskip_ref, nw_ref,
               y_ref, xpad_ref, state_ref):
    L = SSD_CHUNK
    c = pl.program_id(1)

    @pl.when(c == 0)
    def _():
        xpad_ref[0:SUBLANES, :] = jnp.zeros((SUBLANES, SSD_CONV_CH), F32)
        state_ref[...] = jnp.zeros_like(state_ref)

    @pl.when(c != 0)
    def _():
        xpad_ref[0:SUBLANES, :] = xpad_ref[L:L + SUBLANES, :]

    xpad_ref[SUBLANES:SUBLANES + L, :] = xbc_ref[...]

    conv = cb_ref[...]
    for j in range(SSD_CONV):
        off = SUBLANES - (SSD_CONV - 1) + j
        conv = conv + cw_ref[j:j + 1, :] * xpad_ref[off:off + L, :]
    act = conv * jax.nn.sigmoid(conv)
    xs = act[:, :SSD_WIDTH]
    bm = act[:, SSD_WIDTH:SSD_WIDTH + SSD_GROUPS * SSD_STATE].astype(BF16)
    cm = act[:, SSD_WIDTH + SSD_GROUPS * SSD_STATE:].astype(BF16)

    dt_in = dtr_ref[...] + dtb_ref[...]
    dt_all = jnp.maximum(dt_in, 0.0) + jnp.log1p(jnp.exp(-jnp.abs(dt_in)))
    adt = dt_all * (-jnp.exp(alog_ref[...]))

    ri = lax.broadcasted_iota(jnp.int32, (L, L), 0)
    ci = lax.broadcasted_iota(jnp.int32, (L, L), 1)
    causal = ci <= ri
    tril = jnp.where(causal, 1.0, 0.0).astype(BF16)
    hi, mid, lo = _split3(adt)
    acum_all = _dot(tril, hi) + _dot(tril, mid) + _dot(tril, lo)
    acum_t = acum_all.T
    a_last = acum_all[L - 1:L, :]
    decay_in_all = jnp.exp(a_last - acum_all)
    decay_out_all = jnp.exp(acum_all)
    chunk_decay_all = jnp.exp(a_last)
    acum = lambda h: acum_all[:, h:h + 1]
    dt = lambda h: dt_all[:, h:h + 1]
    decay_out = lambda h: decay_out_all[:, h:h + 1]
    decay_in = lambda h: decay_in_all[:, h:h + 1]
    chunk_decay = lambda h: chunk_decay_all[:, h:h + 1]

    lane = lax.broadcasted_iota(jnp.int32, (L, LANES), 1)
    lo_half = lane < SSD_HEAD_DIM

    def per_pair(col_a, col_b):
        return jnp.where(lo_half, col_a, col_b)

    ys = []
    for pair in range(SSD_HEADS // 2):
        g = pair // 2
        h0, h1 = 2 * pair, 2 * pair + 1
        cg = cm[:, g * SSD_STATE:(g + 1) * SSD_STATE]
        bg = bm[:, g * SSD_STATE:(g + 1) * SSD_STATE]
        cb = _dot_nt(cg, bg)
        x_pair = xs[:, pair * LANES:(pair + 1) * LANES]
        xdt = x_pair * per_pair(dt(h0), dt(h1))
        y_pair = jnp.zeros((L, LANES), F32)
        for hh, keep in ((h0, lo_half), (h1, jnp.logical_not(lo_half))):
            seg = acum(hh) - acum_t[hh:hh + 1, :]
            lmat = jnp.where(causal, jnp.exp(jnp.where(causal, seg, 0.0)), 0.0)
            m = (cb * lmat).astype(BF16)
            y_pair = y_pair + _dot(m, jnp.where(keep, xdt, 0.0).astype(BF16))
        s_prev = state_ref[pair]
        y_off = _dot(cg, s_prev.astype(BF16)) * per_pair(decay_out(h0), decay_out(h1))
        w_in = (xdt * per_pair(decay_in(h0), decay_in(h1))).astype(BF16)
        cd = jnp.where(lane[0:1, :] < SSD_HEAD_DIM, chunk_decay(h0), chunk_decay(h1))
        state_ref[pair] = s_prev * cd + _dot_tn(bg, w_in)
        ys.append(y_pair + y_off + dskip_ref[:, pair * LANES:(pair + 1) * LANES] * x_pair)

    y = jnp.concatenate(ys, axis=-1)
    zz = z_ref[...]
    y = y * (zz * jax.nn.sigmoid(zz))
    gw = SSD_WIDTH // SSD_GROUPS
    outs = []
    for g in range(SSD_GROUPS):
        yg = y[:, g * gw:(g + 1) * gw]
        outs.append(yg * lax.rsqrt(jnp.mean(yg * yg, axis=-1, keepdims=True) + SSD_NORM_EPS))
    y_ref[...] = (jnp.concatenate(outs, axis=-1) * nw_ref[...]).astype(BF16)


def _ssd(xbc, dtr, z, conv_w, conv_b, dt_bias, a_log, d_skip, norm_w, bsz, seq):
    nc = seq // SSD_CHUNK
    pad_h = lambda v: jnp.pad(v.reshape(1, SSD_HEADS), ((0, 0), (0, LANES - SSD_HEADS)))
    dskip_lanes = jnp.repeat(d_skip, SSD_HEAD_DIM).reshape(1, SSD_WIDTH)
    seqs = lambda a: a.reshape(bsz, seq, a.shape[-1])
    row = lambda n: pl.BlockSpec((SSD_PAR, SSD_CHUNK, n), lambda b, c: (b, c, 0))
    full = lambda a: pl.BlockSpec(a.shape, lambda b, c: (0, 0))
    args = (seqs(xbc), seqs(dtr), seqs(z), conv_w, conv_b.reshape(1, -1), pad_h(dt_bias), pad_h(a_log), dskip_lanes,
            norm_w.reshape(1, -1))
    y = pl.pallas_call(
        _ssd_kernel,
        grid=(bsz // SSD_PAR, nc),
        in_specs=[row(SSD_CONV_CH), row(LANES), row(SSD_WIDTH)] + [full(a) for a in args[3:]],
        out_specs=row(SSD_WIDTH),
        out_shape=jax.ShapeDtypeStruct((bsz, seq, SSD_WIDTH), BF16),
        scratch_shapes=[
            pltpu.VMEM((SSD_PAR, SSD_CHUNK + 2 * SUBLANES, SSD_CONV_CH), F32),
            pltpu.VMEM((SSD_PAR, SSD_HEADS // 2, SSD_STATE, LANES), F32),
        ],
        compiler_params=_cparams(("arbitrary", "arbitrary")),
        name="ssd",
    )(*args)
    return y.reshape(bsz * seq, SSD_WIDTH)


def _attn_kernel(q_ref, k_ref, v_ref, lq1_ref, lk1_ref, lq2_ref, lk2_ref, sw_ref, o_ref, s_scr):
    bq, bk = ATT_BQ, ATT_BK
    qi = pl.program_id(2)
    n_maps = 2

    def fold(t, op):
        out = t[:, :LANES]
        for c in range(1, bk // LANES):
            out = op(out, t[:, c * LANES:(c + 1) * LANES])
        return out

    def merge(old, new, op):
        return new if old is None else op(old, new)

    def attend(nk):
        q = q_ref[...]
        lane = lax.broadcasted_iota(jnp.int32, (bq, LANES), 1)
        zero = jnp.zeros_like(q)
        q_maps = (jnp.where(lane < ATT_HEAD_DIM, q, zero), jnp.where(lane >= ATT_HEAD_DIM, q, zero))
        r = lax.broadcasted_iota(jnp.int32, (bq, bk), 0)
        c = lax.broadcasted_iota(jnp.int32, (bq, bk), 1)
        causal = c <= r

        mt = [None] * n_maps
        for j in range(nk):
            kb = k_ref[j * bk:(j + 1) * bk, :]
            for m in range(n_maps):
                s = _dot_nt(q_maps[m], kb)
                if j == nk - 1:
                    s = jnp.where(causal, s, -jnp.inf)
                s_scr[m, j] = s
                mt[m] = merge(mt[m], fold(s, jnp.maximum), jnp.maximum)
        row_max = [jnp.max(t, axis=-1, keepdims=True) for t in mt]

        lt = [None] * n_maps
        acc = [None] * n_maps
        for j in range(nk):
            vb = v_ref[j * bk:(j + 1) * bk, :]
            for m in range(n_maps):
                p = jnp.exp2(s_scr[m, j] - row_max[m])
                lt[m] = merge(lt[m], fold(p, jnp.add), jnp.add)
                acc[m] = merge(acc[m], _dot(p.astype(BF16), vb), jnp.add)

        lam = (jnp.exp(jnp.sum(lq1_ref[...] * lk1_ref[...], axis=-1, keepdims=True))
               - jnp.exp(jnp.sum(lq2_ref[...] * lk2_ref[...], axis=-1, keepdims=True)) + LAM_INIT)
        l1 = jnp.sum(lt[0], axis=-1, keepdims=True)
        l2 = jnp.sum(lt[1], axis=-1, keepdims=True)
        o = acc[0] / l1 - lam * (acc[1] / l2)
        o_ref[...] = (_rms(o, sw_ref[...], SUBLN_EPS) * (1.0 - LAM_INIT)).astype(BF16)

    for nk in range(1, k_ref.shape[0] // bk + 1):
        pl.when(qi == nk - 1)(lambda nk=nk: attend(nk))


def _attn(q, k, v, lam_q1, lam_k1, lam_q2, lam_k2, subln_w, bsz, seq):
    nq = seq // ATT_BQ
    qspec = pl.BlockSpec((ATT_BQ, LANES), lambda b, h, i: (b * nq + i, h))
    kvspec = pl.BlockSpec((seq, LANES), lambda b, h, i: (b, h))
    full = lambda a: pl.BlockSpec(a.shape, lambda b, h, i: (0, 0))
    lams = [a.reshape(1, -1) for a in (lam_q1, lam_k1, lam_q2, lam_k2)]
    sw = subln_w.reshape(1, -1)
    return pl.pallas_call(
        _attn_kernel,
        grid=(bsz, ATT_HEADS, nq),
        in_specs=[qspec, kvspec, kvspec] + [full(a) for a in lams] + [full(sw)],
        out_specs=qspec,
        out_shape=jax.ShapeDtypeStruct((bsz * seq, ATT_WIDTH), BF16),
        scratch_shapes=[pltpu.VMEM((2, seq // ATT_BK, ATT_BQ, ATT_BK), F32)],
        compiler_params=_cparams(("arbitrary", "arbitrary", "arbitrary")),
        name="attn",
    )(q, k, v, *lams, sw)


def _mixer(x2, g_mix, w_in, conv_w, conv_b, dt_bias, a_log, d_skip, ssd_norm_w,
           lam_q1, lam_k1, lam_q2, lam_k2, subln_w, bsz, seq):
    z, xbc, dtr, q, k, v = _in_proj(x2, g_mix.reshape(1, -1), w_in)
    y_ssd = _ssd(xbc, dtr, z, conv_w, conv_b, dt_bias, a_log, d_skip, ssd_norm_w, bsz, seq)
    y_att = _attn(q, k, v, lam_q1, lam_k1, lam_q2, lam_k2, subln_w, bsz, seq)
    return y_ssd, y_att


def _out_proj_kernel(x_ref, ys_ref, ya_ref, wo_ref, g_ref, wr2_ref, br_ref,
                     x1_ref, xn_ref, route_ref, gate_ref, cnt_ref):
    tm = TM_ROUTE
    x1 = x_ref[...] + _dot(jnp.concatenate([ys_ref[...], ya_ref[...]], axis=-1), wo_ref[...])
    x1_ref[...] = x1
    xn_all = _rms(x1, g_ref[...], NORM_EPS)
    xn_ref[...] = xn_all.astype(BF16)

    n_tok = OUT_SUB * tm
    xh, xm, _ = _split3(xn_all)
    lg2 = _dot(xh, wr2_ref[...]) + _dot(xm, wr2_ref[...])
    logits = (lg2[:, :LANES] + lg2[:, LANES:]).T[:N_EXPERTS, :] + br_ref[...]

    eidx = lax.broadcasted_iota(jnp.int32, (N_EXPERTS, n_tok), 0).astype(F32)
    work = logits
    vals, idxs, hots = [], [], []
    for _ in range(TOP_K):
        m = jnp.max(work, axis=0, keepdims=True)
        idx = jnp.min(jnp.where(work == m, eidx, float(N_EXPERTS)), axis=0, keepdims=True)
        hot = eidx == idx
        vals.append(m)
        idxs.append(idx.astype(jnp.int32))
        hots.append(hot)
        work = jnp.where(hot, -jnp.inf, work)
    exps = [jnp.exp(v - vals[0]) for v in vals]
    denom = exps[0] + exps[1] + exps[2] + exps[3]
    gates = [e / denom for e in exps]

    cnt = jnp.zeros((N_EXPERTS, n_tok), F32)
    for hot in hots:
        cnt = cnt + jnp.where(hot, 1.0, 0.0)
    cnt_b = cnt.astype(BF16)
    r = lax.broadcasted_iota(jnp.int32, (tm, tm), 0)
    c = lax.broadcasted_iota(jnp.int32, (tm, tm), 1)
    earlier_tok = jnp.where(r < c, 1.0, 0.0).astype(BF16)
    er = lax.broadcasted_iota(jnp.int32, (N_EXPERTS, N_EXPERTS), 0)
    ec = lax.broadcasted_iota(jnp.int32, (N_EXPERTS, N_EXPERTS), 1)
    lower_exp = jnp.where(ec < er, 1.0, 0.0).astype(BF16)
    below = _dot(lower_exp, cnt_b)
    cnt_pad = jnp.concatenate([cnt_b, jnp.zeros((LANES - N_EXPERTS, n_tok), BF16)], axis=0)
    pos = []
    for sub in range(OUT_SUB):
        cols = slice(sub * tm, (sub + 1) * tm)
        pos.append(_dot(cnt_b[:, cols], earlier_tok) + jnp.sum(below[:, cols], axis=1, keepdims=True))
        cnt_ref[sub] = _dot_nt(jnp.ones((SUBLANES, tm), BF16), cnt_pad[:, cols])
    pos = jnp.concatenate(pos, axis=1)
    lps = [jnp.sum(jnp.where(hot, pos, 0.0), axis=0, keepdims=True).astype(jnp.int32) for hot in hots]

    route_ref[...] = jnp.concatenate(idxs + lps, axis=0)
    gate_ref[...] = jnp.concatenate(gates + [jnp.zeros((SUBLANES - TOP_K, n_tok), F32)], axis=0)


def _out_proj(x2, y_ssd, y_att, w_out, g_ffn, w_router, b_router):
    t = x2.shape[0]
    tm = OUT_SUB * TM_ROUTE
    nt = t // TM_ROUTE
    wo = w_out.astype(BF16)
    wrh, wrm, _ = _split3(jnp.pad(w_router, ((0, 0), (0, LANES - N_EXPERTS))))
    wr2 = jnp.concatenate([wrh, wrm], axis=1)
    br = b_router.reshape(N_EXPERTS, 1)
    row = lambda n: pl.BlockSpec((tm, n), lambda i: (i, 0))
    col = pl.BlockSpec((SUBLANES, tm), lambda i: (0, i))
    full = lambda a: pl.BlockSpec(a.shape, lambda i: (0, 0))
    args = (x2, y_ssd, y_att, wo, g_ffn.reshape(1, -1), wr2, br)
    return pl.pallas_call(
        _out_proj_kernel,
        grid=(t // tm,),
        in_specs=[row(D_MODEL), row(SSD_WIDTH), row(ATT_WIDTH)] + [full(a) for a in args[3:]],
        out_specs=[row(D_MODEL), row(D_MODEL), col, col,
                   pl.BlockSpec((OUT_SUB, SUBLANES, LANES), lambda i: (i, 0, 0))],
        out_shape=[
            jax.ShapeDtypeStruct((t, D_MODEL), F32),
            jax.ShapeDtypeStruct((t, D_MODEL), BF16),
            jax.ShapeDtypeStruct((SUBLANES, t), jnp.int32),
            jax.ShapeDtypeStruct((SUBLANES, t), F32),
            jax.ShapeDtypeStruct((nt, SUBLANES, LANES), F32),
        ],
        compiler_params=_cparams(("arbitrary",)),
        name="out_proj",
    )(*args)


def _copy_run(n, start_copy):
    @pl.when(n > 0)
    def _():
        start_copy(n)


def _select_by_position(route_ref, values, n_pos, tm):
    r = lax.broadcasted_iota(jnp.int32, (n_pos, tm), 0)
    out = jnp.zeros((n_pos, tm), F32)
    for k in reversed(range(TOP_K)):
        out = jnp.where(r == route_ref[TOP_K + k:TOP_K + k + 1, :], values[k], out)
    return out


def _dispatch_kernel(cnt_ref, ls_ref, base_ref, zoff_ref, xn_ref, route_ref, xs_hbm, xloc, zeros_vmem, sem_z, sems):
    tm = TM_ROUTE
    n_pos = TOP_K * tm
    i = pl.program_id(0)
    n_steps = pl.num_programs(0)
    slot = lax.rem(i, 2)

    @pl.when(i == 0)
    def _():
        zeros_vmem[...] = jnp.zeros_like(zeros_vmem)

        def zfill(e, _):
            pltpu.make_async_copy(zeros_vmem, xs_hbm.at[_row_slice(zoff_ref[e], ROW_BLK)], sem_z).start()
            return 0

        def zwait(e, _):
            pltpu.make_async_copy(zeros_vmem, xs_hbm.at[_row_slice(0, ROW_BLK)], sem_z).wait()
            return 0

        def tfill(b, _):
            pltpu.make_async_copy(zeros_vmem, xs_hbm.at[_row_slice(b * ROW_BLK, ROW_BLK)], sem_z).start()
            return 0

        lax.fori_loop(0, N_EXPERTS, zfill, 0)
        lax.fori_loop(0, N_EXPERTS, zwait, 0)
        n_blk_total = xs_hbm.shape[0] // (ROW_BLK * ROW_SUB)
        lax.fori_loop(zoff_ref[N_EXPERTS], n_blk_total, tfill, 0)
        lax.fori_loop(zoff_ref[N_EXPERTS], n_blk_total, zwait, 0)

    sel = _select_by_position(route_ref, [1.0] * TOP_K, n_pos, tm).astype(BF16)
    rows = _dot(sel, xn_ref[...])

    def slot_wait(s):
        pltpu.make_async_copy(xloc.at[s], xs_hbm.at[_row_slice(0, n_pos)], sems.at[s]).wait()

    @pl.when(i >= 2)
    def _():
        slot_wait(slot)

    _store_rows(xloc.at[slot], rows)

    def per_expert(e, _):
        idx = i * N_EXPERTS + e
        src0 = ls_ref[idx]
        dst0 = base_ref[idx]

        def start_copy(n):
            pltpu.make_async_copy(xloc.at[slot, _row_slice(src0, n)],
                                  xs_hbm.at[_row_slice(dst0, n)], sems.at[slot]).start()

        _copy_run(cnt_ref[idx], start_copy)
        return 0

    lax.fori_loop(0, N_EXPERTS, per_expert, 0)

    @pl.when(i == n_steps - 1)
    def _():
        slot_wait(slot)

        @pl.when(n_steps > 1)
        def _():
            slot_wait(1 - slot)


def _dispatch(xn, route, tables, n_rows):
    t = xn.shape[0]
    tm = TM_ROUTE
    cnt_tbl, ls_tbl, base_tbl, zoff = tables
    return pl.pallas_call(
        _dispatch_kernel,
        grid_spec=pltpu.PrefetchScalarGridSpec(
            num_scalar_prefetch=4,
            grid=(t // tm,),
            in_specs=[pl.BlockSpec((tm, D_MODEL), lambda i, *_: (i, 0)),
                      pl.BlockSpec((SUBLANES, tm), lambda i, *_: (0, i))],
            out_specs=pl.BlockSpec(memory_space=pl.ANY),
            scratch_shapes=[
                pltpu.VMEM((2, TOP_K * tm * ROW_SUB, LANES), F32),
                pltpu.VMEM((ROW_BLK * ROW_SUB, LANES), F32),
                pltpu.SemaphoreType.DMA,
                pltpu.SemaphoreType.DMA((2,)),
            ],
        ),
        out_shape=jax.ShapeDtypeStruct(((n_rows + ROW_BLK) * ROW_SUB, LANES), F32),
        compiler_params=_cparams(("arbitrary",)),
        name="dispatch",
    )(cnt_tbl, ls_tbl, base_tbl, zoff, xn, route)


def _experts_kernel(blk_e_ref, nvalid_ref, first_ref, wslot_ref, enext_ref, xs_ref, wup_hbm, wdn_hbm,
                    bg_ref, bu_ref, bd_ref, ys_ref, wup_buf, wdn_buf, wg_s, wu_s, wd_s, sems):
    i = pl.program_id(0)
    slot = wslot_ref[i]

    def weight_copies(e, s):
        return (pltpu.make_async_copy(wup_hbm.at[e], wup_buf.at[s], sems.at[0, s]),
                pltpu.make_async_copy(wdn_hbm.at[e], wdn_buf.at[s], sems.at[1, s]))

    @pl.when(i == 0)
    def _():
        for cp in weight_copies(blk_e_ref[0], slot):
            cp.start()

    @pl.when(first_ref[i] != 0)
    def _():
        for cp in weight_copies(blk_e_ref[i], slot):
            cp.wait()

        @pl.when(enext_ref[i] >= 0)
        def _():
            for cp in weight_copies(enext_ref[i], 1 - slot):
                cp.start()

        src = lax.broadcasted_iota(jnp.int32, (DEINT, DEINT), 0)
        dst = lax.broadcasted_iota(jnp.int32, (DEINT, DEINT), 1)
        perm = jnp.where(src == jnp.where(dst < LANES, 2 * dst, 2 * (dst - LANES) + 1), 1.0, 0.0).astype(BF16)
        for g in range(2 * D_EXPERT // DEINT):
            sep = _dot(wup_buf[slot, :, g * DEINT:(g + 1) * DEINT].astype(BF16), perm)
            wg_s[:, g * LANES:(g + 1) * LANES] = sep[:, :LANES].astype(BF16)
            wu_s[:, g * LANES:(g + 1) * LANES] = sep[:, LANES:].astype(BF16)
        wd_s[...] = wdn_buf[slot].astype(BF16)

    @pl.when(i < nvalid_ref[0])
    def _():
        xb = _load_rows(xs_ref).astype(BF16)
        gate = jnp.minimum(_dot(xb, wg_s[...]) + bg_ref[0], SWIGLU_LIMIT)
        up = jnp.clip(_dot(xb, wu_s[...]) + bu_ref[0], -SWIGLU_LIMIT, SWIGLU_LIMIT)
        act = (up + 1.0) * gate * jax.nn.sigmoid(SWIGLU_ALPHA * gate)
        _store_rows(ys_ref, _dot(act.astype(BF16), wd_s[...]) + bd_ref[0])

    @pl.when(i >= nvalid_ref[0])
    def _():
        ys_ref[...] = jnp.zeros_like(ys_ref)


def _experts(xs, plan, w_up, b_up, w_down, b_down, n_rows):
    nb = n_rows // ROW_BLK
    bg = b_up[:, 0::2].reshape(N_EXPERTS, 1, D_EXPERT)
    bu = b_up[:, 1::2].reshape(N_EXPERTS, 1, D_EXPERT)
    bd = b_down.reshape(N_EXPERTS, 1, D_MODEL)
    blk_e, nvalid, first, wslot, enext = plan
    src = lambda i, be, nv, *_: (jnp.minimum(i, nv[0] - 1), 0)
    bspec = lambda m: pl.BlockSpec((1, 1, m), lambda i, be, *_: (be[i], 0, 0))
    anyspec = pl.BlockSpec(memory_space=pl.ANY)
    return pl.pallas_call(
        _experts_kernel,
        grid_spec=pltpu.PrefetchScalarGridSpec(
            num_scalar_prefetch=5,
            grid=(nb,),
            in_specs=[_rows_spec(ROW_BLK, src), anyspec, anyspec,
                      bspec(D_EXPERT), bspec(D_EXPERT), bspec(D_MODEL)],
            out_specs=_rows_spec(ROW_BLK, lambda i, *_: (i, 0)),
            scratch_shapes=[
                pltpu.VMEM((2, D_MODEL, 2 * D_EXPERT), F32),
                pltpu.VMEM((2, D_EXPERT, D_MODEL), F32),
                pltpu.VMEM((D_MODEL, D_EXPERT), BF16),
                pltpu.VMEM((D_MODEL, D_EXPERT), BF16),
                pltpu.VMEM((D_EXPERT, D_MODEL), BF16),
                pltpu.SemaphoreType.DMA((2, 2)),
            ],
        ),
        out_shape=jax.ShapeDtypeStruct((n_rows * ROW_SUB, LANES), F32),
        compiler_params=_cparams(("arbitrary",)),
        name="experts",
    )(blk_e, nvalid, first, wslot, enext, xs, w_up, w_down, bg, bu, bd)


def _combine_kernel(cnt_ref, ls_ref, base_ref, ys_hbm, route_ref, gate_ref, x1_ref, p_ref, gp_ref, wpg_ref,
                    wpp_ref, gf_ref, o_ref, yloc, sems):
    tm = TM_ROUTE
    n_pos = TOP_K * tm
    i = pl.program_id(0)
    n_steps = pl.num_programs(0)
    slot = lax.rem(i, 2)

    def gather_tile(tile, s):
        def per_expert(e, _):
            idx = tile * N_EXPERTS + e
            src0 = base_ref[idx]
            dst0 = ls_ref[idx]

            def start_copy(n):
                pltpu.make_async_copy(ys_hbm.at[_row_slice(src0, n)],
                                      yloc.at[s, _row_slice(dst0, n)], sems.at[s]).start()

            _copy_run(cnt_ref[idx], start_copy)
            return 0

        lax.fori_loop(0, N_EXPERTS, per_expert, 0)

    @pl.when(i == 0)
    def _():
        gather_tile(0, 0)

    @pl.when(i + 1 < n_steps)
    def _():
        gather_tile(i + 1, 1 - slot)

    sel = _select_by_position(route_ref, [1.0] * TOP_K, n_pos, tm).astype(BF16)
    gsel = _select_by_position(route_ref, [gate_ref[k:k + 1, :] for k in range(TOP_K)], n_pos, tm)
    g_pos = jnp.sum(gsel, axis=1, keepdims=True)
    pp = _dot(p_ref[...].astype(BF16), wpp_ref[...])

    pltpu.make_async_copy(ys_hbm.at[_row_slice(0, n_pos)], yloc.at[slot], sems.at[slot]).wait()
    y_gated = (_load_rows(yloc.at[slot]) * g_pos).astype(BF16)
    x2 = x1_ref[...] + _dot_tn(sel, y_gated)
    xn = _rms(x2, gp_ref[...], NORM_EPS).astype(BF16)
    x3 = x2 + pp * jax.nn.sigmoid(_dot(xn, wpg_ref[...]))
    o_ref[...] = _rms(x3, gf_ref[...], NORM_EPS)


def _combine(ys, route, gate_t, x1, p2, tables, g_ple, w_ple_gate, w_ple_proj, g_final):
    t = x1.shape[0]
    tm = TM_ROUTE
    cnt_tbl, ls_tbl, base_tbl, _ = tables
    row = lambda n: pl.BlockSpec((tm, n), lambda i, *_: (i, 0))
    col = pl.BlockSpec((SUBLANES, tm), lambda i, *_: (0, i))
    full = lambda a: pl.BlockSpec(a.shape, lambda i, *_: (0, 0))
    consts = (g_ple.reshape(1, -1), w_ple_gate.astype(BF16), w_ple_proj.astype(BF16), g_final.reshape(1, -1))
    return pl.pallas_call(
        _combine_kernel,
        grid_spec=pltpu.PrefetchScalarGridSpec(
            num_scalar_prefetch=3,
            grid=(t // tm,),
            in_specs=[pl.BlockSpec(memory_space=pl.ANY), col, col, row(D_MODEL), row(PLE_DIM)]
                     + [full(a) for a in consts],
            out_specs=row(D_MODEL),
            scratch_shapes=[
                pltpu.VMEM((2, TOP_K * tm * ROW_SUB, LANES), F32),
                pltpu.SemaphoreType.DMA((2,)),
            ],
        ),
        out_shape=jax.ShapeDtypeStruct((t, D_MODEL), F32),
        compiler_params=_cparams(("arbitrary",)),
        name="combine",
    )(cnt_tbl, ls_tbl, base_tbl, ys, route, gate_t, x1, p2, *consts)


def _routing_tables(cnt, n_blocks):
    tile_cnt = cnt[:, 0, :N_EXPERTS].astype(jnp.int32)
    counts = jnp.sum(tile_cnt, axis=0)
    padded = (counts + ROW_BLK - 1) // ROW_BLK * ROW_BLK
    pend = jnp.cumsum(padded)
    pstart = pend - padded
    base = pstart[None, :] + jnp.cumsum(tile_cnt, axis=0) - tile_cnt
    lstart = jnp.cumsum(tile_cnt, axis=1) - tile_cnt
    nvalid = pend[-1:] // ROW_BLK
    zoff = jnp.concatenate([pstart + counts, nvalid])
    blk_start = jnp.minimum(jnp.arange(n_blocks, dtype=jnp.int32) * ROW_BLK, pend[-1] - 1)
    blk_e = jnp.minimum(jnp.sum((pend[None, :] <= blk_start[:, None]).astype(jnp.int32), axis=1), N_EXPERTS - 1)
    first = jnp.concatenate([jnp.ones((1,), bool), blk_e[1:] != blk_e[:-1]])
    wslot = (jnp.cumsum(first.astype(jnp.int32)) - 1) % 2
    eids = jnp.arange(N_EXPERTS, dtype=jnp.int32)
    later_nonempty = (eids[None, :] > eids[:, None]) & (padded[None, :] > 0)
    next_e = jnp.min(jnp.where(later_nonempty, eids[None, :], N_EXPERTS), axis=1)
    next_e = jnp.where(next_e == N_EXPERTS, -1, next_e)
    i32 = lambda a: a.reshape(-1).astype(jnp.int32)
    plan = (i32(blk_e), i32(nvalid), i32(first), i32(wslot), i32(next_e[blk_e]))
    return (i32(tile_cnt), i32(lstart), i32(base), i32(zoff)), plan


def kernel(x, p, g_mix, w_in, conv_w, conv_b, dt_bias, a_log, d_skip, ssd_norm_w, lam_q1, lam_k1, lam_q2, lam_k2, subln_w, w_out, g_ffn, w_router, b_router, w_up, b_up, w_down, b_down, g_ple, w_ple_gate, w_ple_proj, g_final):
    bsz, seq, d = x.shape
    t = bsz * seq
    x2 = x.reshape(t, d)
    y_ssd, y_att = _mixer(x2, g_mix[0], w_in[0], conv_w[0], conv_b[0], dt_bias[0], a_log[0], d_skip[0], ssd_norm_w[0],
                          lam_q1[0], lam_k1[0], lam_q2[0], lam_k2[0], subln_w[0], bsz, seq)
    x1, xn, route, gate_t, cnt = _out_proj(x2, y_ssd, y_att, w_out[0], g_ffn[0], w_router[0], b_router[0])

    n_rows = t * TOP_K + N_EXPERTS * ROW_BLK
    tables, plan = _routing_tables(cnt, n_rows // ROW_BLK)
    xs = _dispatch(xn, route, tables, n_rows)
    ys = _experts(xs, plan, w_up[0], b_up[0], w_down[0], b_down[0], n_rows)
    out = _combine(ys, route, gate_t, x1, p[0].reshape(t, PLE_DIM), tables, g_ple[0], w_ple_gate[0], w_ple_proj[0],
                   g_final)
    return out.reshape(bsz, seq, d)
```

```python
import math

import jax
import jax.numpy as jnp
from jax import lax
from jax.experimental import pallas as pl
from jax.experimental.pallas import tpu as pltpu

F32 = jnp.float32
BF16 = jnp.bfloat16

D_MODEL = 1024
PLE_DIM = 256
SSD_WIDTH = 512
ATT_WIDTH = 512
SSD_HEAD_DIM = 64
SSD_HEADS = 8
SSD_GROUPS = 2
SSD_STATE = 128
SSD_CONV = 4
SSD_CHUNK = 128
SSD_CONV_CH = SSD_WIDTH + 2 * SSD_GROUPS * SSD_STATE
SSD_NORM_EPS = 1e-5
ATT_HEAD_DIM = 64
ATT_HEADS = 4
SUBLN_EPS = 1e-5
OFF_Z = 0
OFF_XBC = OFF_Z + SSD_WIDTH
OFF_DT = OFF_XBC + SSD_CONV_CH
OFF_Q = OFF_DT + SSD_HEADS
OFF_K = OFF_Q + ATT_WIDTH
OFF_V = OFF_K + ATT_WIDTH
IN_PROJ = OFF_V + ATT_WIDTH
N_EXPERTS = 32
TOP_K = 4
D_EXPERT = 1024
SWIGLU_LIMIT = 7.0
SWIGLU_ALPHA = 1.702
NORM_EPS = 1e-6
LAM_INIT = 0.8 - 0.6 * math.exp(-0.3 * 0)

LANES = 128
SUBLANES = 8
VMEM_LIMIT_BYTES = 56 * 1024 * 1024
ROW_SUB = D_MODEL // LANES

TM_PROJ = 512
SSD_PAR = 2
ATT_BQ = 512
ATT_BK = 512
ROW_BLK = 256
TM_ROUTE = 256
OUT_SUB = 4
DEINT = 2 * LANES


def _cparams(sem):
    return pltpu.CompilerParams(dimension_semantics=sem, vmem_limit_bytes=VMEM_LIMIT_BYTES)


def _rms(x, w, eps):
    return x * lax.rsqrt(jnp.mean(x * x, axis=-1, keepdims=True) + eps) * w


def _dot(a, b):
    return jnp.dot(a, b, preferred_element_type=F32)


def _dot_nt(a, b):
    return lax.dot_general(a, b, (((1,), (1,)), ((), ())), preferred_element_type=F32)


def _dot_tn(a, b):
    return lax.dot_general(a, b, (((0,), (0,)), ((), ())), preferred_element_type=F32)


def _store_rows(ref, val):
    n = val.shape[0]
    for s in range(ROW_SUB):
        ref[pl.ds(s, n, stride=ROW_SUB), :] = val[:, s * LANES:(s + 1) * LANES]


def _load_rows(ref):
    n = ref.shape[0] // ROW_SUB
    return jnp.concatenate([ref[pl.ds(s, n, stride=ROW_SUB), :] for s in range(ROW_SUB)], axis=-1)


def _rows_spec(n, index_map):
    return pl.BlockSpec((n * ROW_SUB, LANES), index_map)


def _row_slice(start, n):
    return pl.ds(pl.multiple_of(start * ROW_SUB, ROW_SUB), n * ROW_SUB)


def _split3(x):
    hi = x.astype(BF16)
    r1 = x - hi.astype(F32)
    mid = r1.astype(BF16)
    lo = (r1 - mid.astype(F32)).astype(BF16)
    return hi, mid, lo


def _in_proj_kernel(x_ref, g_ref, wz_ref, wxbc_ref, wdt_ref, wq_ref, wk_ref, wv_ref,
                    z_ref, xbc_ref, dt_ref, q_ref, k_ref, v_ref):
    h = _rms(x_ref[...], g_ref[...], NORM_EPS).astype(BF16)
    z_ref[...] = _dot(h, wz_ref[...])
    xbc_ref[...] = _dot(h, wxbc_ref[...])
    dt_ref[...] = _dot(h, wdt_ref[...])
    q_ref[...] = (_dot(h, wq_ref[...]) * (ATT_HEAD_DIM ** -0.5 * math.log2(math.e))).astype(BF16)
    k_ref[...] = _dot(h, wk_ref[...]).astype(BF16)
    v_ref[...] = _dot(h, wv_ref[...]).astype(BF16)


def _in_proj(x2, g_mix, w_in):
    t = x2.shape[0]
    wb = w_in.astype(BF16)
    wz = wb[:, OFF_Z:OFF_XBC]
    wxbc = wb[:, OFF_XBC:OFF_DT]
    wdt = jnp.pad(wb[:, OFF_DT:OFF_Q], ((0, 0), (0, LANES - SSD_HEADS)))
    wq = wb[:, OFF_Q:OFF_K]
    wk = wb[:, OFF_K:OFF_V]
    wv = wb[:, OFF_V:IN_PROJ]
    tm = TM_PROJ
    row = lambda n: pl.BlockSpec((tm, n), lambda i: (i, 0))
    full = lambda a: pl.BlockSpec(a.shape, lambda i: (0, 0))
    return pl.pallas_call(
        _in_proj_kernel,
        grid=(t // tm,),
        in_specs=[row(D_MODEL), full(g_mix), full(wz), full(wxbc), full(wdt), full(wq), full(wk), full(wv)],
        out_specs=[row(SSD_WIDTH), row(SSD_CONV_CH), row(LANES), row(ATT_WIDTH), row(ATT_WIDTH), row(ATT_WIDTH)],
        out_shape=[
            jax.ShapeDtypeStruct((t, SSD_WIDTH), F32),
            jax.ShapeDtypeStruct((t, SSD_CONV_CH), F32),
            jax.ShapeDtypeStruct((t, LANES), F32),
            jax.ShapeDtypeStruct((t, ATT_WIDTH), BF16),
            jax.ShapeDtypeStruct((t, ATT_WIDTH), BF16),
            jax.ShapeDtypeStruct((t, ATT_WIDTH), BF16),
        ],
        compiler_params=_cparams(("arbitrary",)),
        name="in_proj",
    )(x2, g_mix, wz, wxbc, wdt, wq, wk, wv)


def _ssd_kernel(xbc_ref, dtr_ref, z_ref, cw_ref, cb_ref, dtb_ref, alog_ref, dskip_ref, nw_ref,
                y_ref, xpad_ref, state_ref):
    for bb in range(SSD_PAR):
        _ssd_chunk(xbc_ref.at[bb], dtr_ref.at[bb], z_ref.at[bb], cw_ref, cb_ref, dtb_ref, alog_ref, dskip_ref,
                   nw_ref, y_ref.at[bb], xpad_ref.at[bb], state_ref.at[bb])


def _ssd_chunk(xbc_ref, dtr_ref, z_ref, cw_ref, cb_ref, dtb_ref, alog_ref, dskip_ref, nw_ref,
               y_ref, xpad_ref, state_ref):
    L = SSD_CHUNK
    c = pl.program_id(1)

    @pl.when(c == 0)
    def _():
        xpad_ref[0:SUBLANES, :] = jnp.zeros((SUBLANES, SSD_CONV_CH), F32)
        state_ref[...] = jnp.zeros_like(state_ref)

    @pl.when(c != 0)
    def _():
        xpad_ref[0:SUBLANES, :] = xpad_ref[L:L + SUBLANES, :]

    xpad_ref[SUBLANES:SUBLANES + L, :] = xbc_ref[...]

    conv = cb_ref[...]
    for j in range(SSD_CONV):
        off = SUBLANES - (SSD_CONV - 1) + j
        conv = conv + cw_ref[j:j + 1, :] * xpad_ref[off:off + L, :]
    act = conv * jax.nn.sigmoid(conv)
    xs = act[:, :SSD_WIDTH]
    bm = act[:, SSD_WIDTH:SSD_WIDTH + SSD_GROUPS * SSD_STATE].astype(BF16)
    cm = act[:, SSD_WIDTH + SSD_GROUPS * SSD_STATE:].astype(BF16)

    dt_in = dtr_ref[...] + dtb_ref[...]
    dt_all = jnp.maximum(dt_in, 0.0) + jnp.log1p(jnp.exp(-jnp.abs(dt_in)))
    adt = dt_all * (-jnp.exp(alog_ref[...]))

    ri = lax.broadcasted_iota(jnp.int32, (L, L), 0)
    ci = lax.broadcasted_iota(jnp.int32, (L, L), 1)
    causal = ci <= ri
    tril = jnp.where(causal, 1.0, 0.0).astype(BF16)
    hi, mid, lo = _split3(adt)
    acum_all = _dot(tril, hi) + _dot(tril, mid) + _dot(tril, lo)
    acum_t = acum_all.T
    a_last = acum_all[L - 1:L, :]
    decay_in_all = jnp.exp(a_last - acum_all)
    decay_out_all = jnp.exp(acum_all)
    chunk_decay_all = jnp.exp(a_last)
    acum = lambda h: acum_all[:, h:h + 1]
    dt = lambda h: dt_all[:, h:h + 1]
    decay_out = lambda h: decay_out_all[:, h:h + 1]
    decay_in = lambda h: decay_in_all[:, h:h + 1]
    chunk_decay = lambda h: chunk_decay_all[:, h:h + 1]

    lane = lax.broadcasted_iota(jnp.int32, (L, LANES), 1)
    lo_half = lane < SSD_HEAD_DIM

    def per_pair(col_a, col_b):
        return jnp.where(lo_half, col_a, col_b)

    ys = []
    for pair in range(SSD_HEADS // 2):
        g = pair // 2
        h0, h1 = 2 * pair, 2 * pair + 1
        cg = cm[:, g * SSD_STATE:(g + 1) * SSD_STATE]
        bg = bm[:, g * SSD_STATE:(g + 1) * SSD_STATE]
        cb = _dot_nt(cg, bg)
        x_pair = xs[:, pair * LANES:(pair + 1) * LANES]
        xdt = x_pair * per_pair(dt(h0), dt(h1))
        y_pair = jnp.zeros((L, LANES), F32)
        for hh, keep in ((h0, lo_half), (h1, jnp.logical_not(lo_half))):
            seg = acum(hh) - acum_t[hh:hh + 1, :]
            lmat = jnp.where(causal, jnp.exp(jnp.where(causal, seg, 0.0)), 0.0)
            m = (cb * lmat).astype(BF16)
            y_pair = y_pair + _dot(m, jnp.where(keep, xdt, 0.0).astype(BF16))
        s_prev = state_ref[pair]
        y_off = _dot(cg, s_prev.astype(BF16)) * per_pair(decay_out(h0), decay_out(h1))
        w_in = (xdt * per_pair(decay_in(h0), decay_in(h1))).astype(BF16)
        cd = jnp.where(lane[0:1, :] < SSD_HEAD_DIM, chunk_decay(h0), chunk_decay(h1))
        state_ref[pair] = s_prev * cd + _dot_tn(bg, w_in)
        ys.append(y_pair + y_off + dskip_ref[:, pair * LANES:(pair + 1) * LANES] * x_pair)

    y = jnp.concatenate(ys, axis=-1)
    zz = z_ref[...]
    y = y * (zz * jax.nn.sigmoid(zz))
    gw = SSD_WIDTH // SSD_GROUPS
    outs = []
    for g in range(SSD_GROUPS):
        yg = y[:, g * gw:(g + 1) * gw]
        outs.append(yg * lax.rsqrt(jnp.mean(yg * yg, axis=-1, keepdims=True) + SSD_NORM_EPS))
    y_ref[...] = (jnp.concatenate(outs, axis=-1) * nw_ref[...]).astype(BF16)


def _ssd(xbc, dtr, z, conv_w, conv_b, dt_bias, a_log, d_skip, norm_w, bsz, seq):
    nc = seq // SSD_CHUNK
    pad_h = lambda v: jnp.pad(v.reshape(1, SSD_HEADS), ((0, 0), (0, LANES - SSD_HEADS)))
    dskip_lanes = jnp.repeat(d_skip, SSD_HEAD_DIM).reshape(1, SSD_WIDTH)
    seqs = lambda a: a.reshape(bsz, seq, a.shape[-1])
    row = lambda n: pl.BlockSpec((SSD_PAR, SSD_CHUNK, n), lambda b, c: (b, c, 0))
    full = lambda a: pl.BlockSpec(a.shape, lambda b, c: (0, 0))
    args = (seqs(xbc), seqs(dtr), seqs(z), conv_w, conv_b.reshape(1, -1), pad_h(dt_bias), pad_h(a_log), dskip_lanes,
            norm_w.reshape(1, -1))
    y = pl.pallas_call(
        _ssd_kernel,
        grid=(bsz // SSD_PAR, nc),
        in_specs=[row(SSD_CONV_CH), row(LANES), row(SSD_WIDTH)] + [full(a) for a in args[3:]],
        out_specs=row(SSD_WIDTH),
        out_shape=jax.ShapeDtypeStruct((bsz, seq, SSD_WIDTH), BF16),
        scratch_shapes=[
            pltpu.VMEM((SSD_PAR, SSD_CHUNK + 2 * SUBLANES, SSD_CONV_CH), F32),
            pltpu.VMEM((SSD_PAR, SSD_HEADS // 2, SSD_STATE, LANES), F32),
        ],
        compiler_params=_cparams(("arbitrary", "arbitrary")),
        name="ssd",
    )(*args)
    return y.reshape(bsz * seq, SSD_WIDTH)


def _attn_kernel(q_ref, k_ref, v_ref, lq1_ref, lk1_ref, lq2_ref, lk2_ref, sw_ref, o_ref, s_scr, vt_scr):
    bq, bk = ATT_BQ, ATT_BK
    qi = pl.program_id(2)
    n_maps = 2

    @pl.when(qi == 0)
    def _():
        vt_scr[...] = v_ref[...].astype(F32).T.astype(BF16)

    def fold(t, reduce):
        return reduce(t.reshape(bk // SUBLANES, SUBLANES, bq), axis=0)

    def merge(old, new, op):
        return new if old is None else op(old, new)

    def attend(nk):
        q = q_ref[...]
        lane = lax.broadcasted_iota(jnp.int32, (bq, LANES), 1)
        zero = jnp.zeros_like(q)
        q_maps = (jnp.where(lane < ATT_HEAD_DIM, q, zero), jnp.where(lane >= ATT_HEAD_DIM, q, zero))
        key = lax.broadcasted_iota(jnp.int32, (bk, bq), 0)
        qry = lax.broadcasted_iota(jnp.int32, (bk, bq), 1)
        causal = key <= qry

        mt = [None] * n_maps
        for j in range(nk):
            kb = k_ref[j * bk:(j + 1) * bk, :]
            for m in range(n_maps):
                s = _dot_nt(kb, q_maps[m])
                if j == nk - 1:
                    s = jnp.where(causal, s, -jnp.inf)
                s_scr[m, j] = s
                mt[m] = merge(mt[m], fold(s, jnp.max), jnp.maximum)
        q_max = [jnp.max(t, axis=0, keepdims=True) for t in mt]

        lt = [None] * n_maps
        acc = [None] * n_maps
        for j in range(nk):
            vt = vt_scr[:, j * bk:(j + 1) * bk]
            for m in range(n_maps):
                p = jnp.exp2(s_scr[m, j] - q_max[m])
                lt[m] = merge(lt[m], fold(p, jnp.sum), jnp.add)
                acc[m] = merge(acc[m], _dot(vt, p.astype(BF16)), jnp.add)

        lam = (jnp.exp(jnp.sum(lq1_ref[...] * lk1_ref[...], axis=-1, keepdims=True))
               - jnp.exp(jnp.sum(lq2_ref[...] * lk2_ref[...], axis=-1, keepdims=True)) + LAM_INIT)
        l1 = jnp.sum(lt[0], axis=0, keepdims=True)
        l2 = jnp.sum(lt[1], axis=0, keepdims=True)
        o = acc[0] / l1 - lam * (acc[1] / l2)
        o = o * lax.rsqrt(jnp.mean(o * o, axis=0, keepdims=True) + SUBLN_EPS) * sw_ref[...]
        o_ref[...] = (o * (1.0 - LAM_INIT)).T.astype(BF16)

    for nk in range(1, k_ref.shape[0] // bk + 1):
        pl.when(qi == nk - 1)(lambda nk=nk: attend(nk))


def _attn(q, k, v, lam_q1, lam_k1, lam_q2, lam_k2, subln_w, bsz, seq):
    nq = seq // ATT_BQ
    qspec = pl.BlockSpec((ATT_BQ, LANES), lambda b, h, i: (b * nq + i, h))
    kvspec = pl.BlockSpec((seq, LANES), lambda b, h, i: (b, h))
    full = lambda a: pl.BlockSpec(a.shape, lambda b, h, i: (0, 0))
    lams = [a.reshape(1, -1) for a in (lam_q1, lam_k1, lam_q2, lam_k2)]
    sw = subln_w.reshape(-1, 1)
    return pl.pallas_call(
        _attn_kernel,
        grid=(bsz, ATT_HEADS, nq),
        in_specs=[qspec, kvspec, kvspec] + [full(a) for a in lams] + [full(sw)],
        out_specs=qspec,
        out_shape=jax.ShapeDtypeStruct((bsz * seq, ATT_WIDTH), BF16),
        scratch_shapes=[pltpu.VMEM((2, seq // ATT_BK, ATT_BK, ATT_BQ), F32),
                        pltpu.VMEM((LANES, seq), BF16)],
        compiler_params=_cparams(("arbitrary", "arbitrary", "arbitrary")),
        name="attn",
    )(q, k, v, *lams, sw)


def _mixer(x2, g_mix, w_in, conv_w, conv_b, dt_bias, a_log, d_skip, ssd_norm_w,
           lam_q1, lam_k1, lam_q2, lam_k2, subln_w, bsz, seq):
    z, xbc, dtr, q, k, v = _in_proj(x2, g_mix.reshape(1, -1), w_in)
    y_ssd = _ssd(xbc, dtr, z, conv_w, conv_b, dt_bias, a_log, d_skip, ssd_norm_w, bsz, seq)
    y_att = _attn(q, k, v, lam_q1, lam_k1, lam_q2, lam_k2, subln_w, bsz, seq)
    return y_ssd, y_att


def _out_proj_kernel(x_ref, ys_ref, ya_ref, wo_ref, g_ref, wr2_ref, br_ref,
                     x1_ref, xn_ref, route_ref, gate_ref, cnt_ref):
    tm = TM_ROUTE
    x1 = x_ref[...] + _dot(jnp.concatenate([ys_ref[...], ya_ref[...]], axis=-1), wo_ref[...])
    x1_ref[...] = x1
    xn_all = _rms(x1, g_ref[...], NORM_EPS)
    xn_ref[...] = xn_all.astype(BF16)

    n_tok = OUT_SUB * tm
    xh, xm, _ = _split3(xn_all)
    lg2 = _dot(xh, wr2_ref[...]) + _dot(xm, wr2_ref[...])
    logits = (lg2[:, :LANES] + lg2[:, LANES:]).T[:N_EXPERTS, :] + br_ref[...]

    eidx = lax.broadcasted_iota(jnp.int32, (N_EXPERTS, n_tok), 0).astype(F32)
    work = logits
    vals, idxs, hots = [], [], []
    for _ in range(TOP_K):
        m = jnp.max(work, axis=0, keepdims=True)
        idx = jnp.min(jnp.where(work == m, eidx, float(N_EXPERTS)), axis=0, keepdims=True)
        hot = eidx == idx
        vals.append(m)
        idxs.append(idx.astype(jnp.int32))
        hots.append(hot)
        work = jnp.where(hot, -jnp.inf, work)
    exps = [jnp.exp(v - vals[0]) for v in vals]
    denom = exps[0] + exps[1] + exps[2] + exps[3]
    gates = [e / denom for e in exps]

    cnt = jnp.zeros((N_EXPERTS, n_tok), F32)
    for hot in hots:
        cnt = cnt + jnp.where(hot, 1.0, 0.0)
    cnt_b = cnt.astype(BF16)
    r = lax.broadcasted_iota(jnp.int32, (tm, tm), 0)
    c = lax.broadcasted_iota(jnp.int32, (tm, tm), 1)
    earlier_tok = jnp.where(r < c, 1.0, 0.0).astype(BF16)
    er = lax.broadcasted_iota(jnp.int32, (N_EXPERTS, N_EXPERTS), 0)
    ec = lax.broadcasted_iota(jnp.int32, (N_EXPERTS, N_EXPERTS), 1)
    lower_exp = jnp.where(ec < er, 1.0, 0.0).astype(BF16)
    below = _dot(lower_exp, cnt_b)
    cnt_pad = jnp.concatenate([cnt_b, jnp.zeros((LANES - N_EXPERTS, n_tok), BF16)], axis=0)
    pos = []
    for sub in range(OUT_SUB):
        cols = slice(sub * tm, (sub + 1) * tm)
        pos.append(_dot(cnt_b[:, cols], earlier_tok) + jnp.sum(below[:, cols], axis=1, keepdims=True))
        cnt_ref[sub] = _dot_nt(jnp.ones((SUBLANES, tm), BF16), cnt_pad[:, cols])
    pos = jnp.concatenate(pos, axis=1)
    lps = [jnp.sum(jnp.where(hot, pos, 0.0), axis=0, keepdims=True).astype(jnp.int32) for hot in hots]

    route_ref[...] = jnp.concatenate(idxs + lps, axis=0)
    gate_ref[...] = jnp.concatenate(gates + [jnp.zeros((SUBLANES - TOP_K, n_tok), F32)], axis=0)


def _out_proj(x2, y_ssd, y_att, w_out, g_ffn, w_router, b_router):
    t = x2.shape[0]
    tm = OUT_SUB * TM_ROUTE
    nt = t // TM_ROUTE
    wo = w_out.astype(BF16)
    wrh, wrm, _ = _split3(jnp.pad(w_router, ((0, 0), (0, LANES - N_EXPERTS))))
    wr2 = jnp.concatenate([wrh, wrm], axis=1)
    br = b_router.reshape(N_EXPERTS, 1)
    row = lambda n: pl.BlockSpec((tm, n), lambda i: (i, 0))
    col = pl.BlockSpec((SUBLANES, tm), lambda i: (0, i))
    full = lambda a: pl.BlockSpec(a.shape, lambda i: (0, 0))
    args = (x2, y_ssd, y_att, wo, g_ffn.reshape(1, -1), wr2, br)
    return pl.pallas_call(
        _out_proj_kernel,
        grid=(t // tm,),
        in_specs=[row(D_MODEL), row(SSD_WIDTH), row(ATT_WIDTH)] + [full(a) for a in args[3:]],
        out_specs=[row(D_MODEL), row(D_MODEL), col, col,
                   pl.BlockSpec((OUT_SUB, SUBLANES, LANES), lambda i: (i, 0, 0))],
        out_shape=[
            jax.ShapeDtypeStruct((t, D_MODEL), F32),
            jax.ShapeDtypeStruct((t, D_MODEL), BF16),
            jax.ShapeDtypeStruct((SUBLANES, t), jnp.int32),
            jax.ShapeDtypeStruct((SUBLANES, t), F32),
            jax.ShapeDtypeStruct((nt, SUBLANES, LANES), F32),
        ],
        compiler_params=_cparams(("arbitrary",)),
        name="out_proj",
    )(*args)


def _copy_run(n, start_copy):
    @pl.when(n > 0)
    def _():
        start_copy(n)


def _select_by_position(route_ref, values, n_pos, tm):
    r = lax.broadcasted_iota(jnp.int32, (n_pos, tm), 0)
    out = jnp.zeros((n_pos, tm), F32)
    for k in reversed(range(TOP_K)):
        out = jnp.where(r == route_ref[TOP_K + k:TOP_K + k + 1, :], values[k], out)
    return out


def _dispatch_kernel(cnt_ref, ls_ref, base_ref, zoff_ref, xn_ref, route_ref, xs_hbm, xloc, zeros_vmem, sem_z, sems):
    tm = TM_ROUTE
    n_pos = TOP_K * tm
    i = pl.program_id(0)
    n_steps = pl.num_programs(0)
    slot = lax.rem(i, 2)

    @pl.when(i == 0)
    def _():
        zeros_vmem[...] = jnp.zeros_like(zeros_vmem)

        def zfill(e, _):
            pltpu.make_async_copy(zeros_vmem, xs_hbm.at[_row_slice(zoff_ref[e], ROW_BLK)], sem_z).start()
            return 0

        def zwait(e, _):
            pltpu.make_async_copy(zeros_vmem, xs_hbm.at[_row_slice(0, ROW_BLK)], sem_z).wait()
            return 0

        def tfill(b, _):
            pltpu.make_async_copy(zeros_vmem, xs_hbm.at[_row_slice(b * ROW_BLK, ROW_BLK)], sem_z).start()
            return 0

        lax.fori_loop(0, N_EXPERTS, zfill, 0)
        lax.fori_loop(0, N_EXPERTS, zwait, 0)
        n_blk_total = xs_hbm.shape[0] // (ROW_BLK * ROW_SUB)
        lax.fori_loop(zoff_ref[N_EXPERTS], n_blk_total, tfill, 0)
        lax.fori_loop(zoff_ref[N_EXPERTS], n_blk_total, zwait, 0)

    sel = _select_by_position(route_ref, [1.0] * TOP_K, n_pos, tm).astype(BF16)
    rows = _dot(sel, xn_ref[...])

    def slot_wait(s):
        pltpu.make_async_copy(xloc.at[s], xs_hbm.at[_row_slice(0, n_pos)], sems.at[s]).wait()

    @pl.when(i >= 2)
    def _():
        slot_wait(slot)

    _store_rows(xloc.at[slot], rows)

    def per_expert(e, _):
        idx = i * N_EXPERTS + e
        src0 = ls_ref[idx]
        dst0 = base_ref[idx]

        def start_copy(n):
            pltpu.make_async_copy(xloc.at[slot, _row_slice(src0, n)],
                                  xs_hbm.at[_row_slice(dst0, n)], sems.at[slot]).start()

        _copy_run(cnt_ref[idx], start_copy)
        return 0

    lax.fori_loop(0, N_EXPERTS, per_expert, 0)

    @pl.when(i == n_steps - 1)
    def _():
        slot_wait(slot)

        @pl.when(n_steps > 1)
        def _():
            slot_wait(1 - slot)


def _dispatch(xn, route, tables, n_rows):
    t = xn.shape[0]
    tm = TM_ROUTE
    cnt_tbl, ls_tbl, base_tbl, zoff = tables
    return pl.pallas_call(
        _dispatch_kernel,
        grid_spec=pltpu.PrefetchScalarGridSpec(
            num_scalar_prefetch=4,
            grid=(t // tm,),
            in_specs=[pl.BlockSpec((tm, D_MODEL), lambda i, *_: (i, 0)),
                      pl.BlockSpec((SUBLANES, tm), lambda i, *_: (0, i))],
            out_specs=pl.BlockSpec(memory_space=pl.ANY),
            scratch_shapes=[
                pltpu.VMEM((2, TOP_K * tm * ROW_SUB, LANES), F32),
                pltpu.VMEM((ROW_BLK * ROW_SUB, LANES), F32),
                pltpu.SemaphoreType.DMA,
                pltpu.SemaphoreType.DMA((2,)),
            ],
        ),
        out_shape=jax.ShapeDtypeStruct(((n_rows + ROW_BLK) * ROW_SUB, LANES), F32),
        compiler_params=_cparams(("arbitrary",)),
        name="dispatch",
    )(cnt_tbl, ls_tbl, base_tbl, zoff, xn, route)


def _experts_kernel(blk_e_ref, nvalid_ref, first_ref, wslot_ref, enext_ref, xs_ref, wup_hbm, wdn_hbm,
                    bg_ref, bu_ref, bd_ref, ys_ref, wup_buf, wdn_buf, wg_s, wu_s, wd_s, sems):
    i = pl.program_id(0)
    slot = wslot_ref[i]

    def weight_copies(e, s):
        return (pltpu.make_async_copy(wup_hbm.at[e], wup_buf.at[s], sems.at[0, s]),
                pltpu.make_async_copy(wdn_hbm.at[e], wdn_buf.at[s], sems.at[1, s]))

    @pl.when(i == 0)
    def _():
        for cp in weight_copies(blk_e_ref[0], slot):
            cp.start()

    @pl.when(first_ref[i] != 0)
    def _():
        for cp in weight_copies(blk_e_ref[i], slot):
            cp.wait()

        @pl.when(enext_ref[i] >= 0)
        def _():
            for cp in weight_copies(enext_ref[i], 1 - slot):
                cp.start()

        src = lax.broadcasted_iota(jnp.int32, (DEINT, DEINT), 0)
        dst = lax.broadcasted_iota(jnp.int32, (DEINT, DEINT), 1)
        perm = jnp.where(src == jnp.where(dst < LANES, 2 * dst, 2 * (dst - LANES) + 1), 1.0, 0.0).astype(BF16)
        for g in range(2 * D_EXPERT // DEINT):
            sep = _dot(wup_buf[slot, :, g * DEINT:(g + 1) * DEINT].astype(BF16), perm)
            wg_s[:, g * LANES:(g + 1) * LANES] = sep[:, :LANES].astype(BF16)
            wu_s[:, g * LANES:(g + 1) * LANES] = sep[:, LANES:].astype(BF16)
        wd_s[...] = wdn_buf[slot].astype(BF16)

    @pl.when(i < nvalid_ref[0])
    def _():
        xb = _load_rows(xs_ref).astype(BF16)
        gate = jnp.minimum(_dot(xb, wg_s[...]) + bg_ref[0], SWIGLU_LIMIT)
        up = jnp.clip(_dot(xb, wu_s[...]) + bu_ref[0], -SWIGLU_LIMIT, SWIGLU_LIMIT)
        act = (up + 1.0) * gate * jax.nn.sigmoid(SWIGLU_ALPHA * gate)
        _store_rows(ys_ref, _dot(act.astype(BF16), wd_s[...]) + bd_ref[0])

    @pl.when(i >= nvalid_ref[0])
    def _():
        ys_ref[...] = jnp.zeros_like(ys_ref)


def _experts(xs, plan, w_up, b_up, w_down, b_down, n_rows):
    nb = n_rows // ROW_BLK
    bg = b_up[:, 0::2].reshape(N_EXPERTS, 1, D_EXPERT)
    bu = b_up[:, 1::2].reshape(N_EXPERTS, 1, D_EXPERT)
    bd = b_down.reshape(N_EXPERTS, 1, D_MODEL)
    blk_e, nvalid, first, wslot, enext = plan
    src = lambda i, be, nv, *_: (jnp.minimum(i, nv[0] - 1), 0)
    bspec = lambda m: pl.BlockSpec((1, 1, m), lambda i, be, *_: (be[i], 0, 0))
    anyspec = pl.BlockSpec(memory_space=pl.ANY)
    return pl.pallas_call(
        _experts_kernel,
        grid_spec=pltpu.PrefetchScalarGridSpec(
            num_scalar_prefetch=5,
            grid=(nb,),
            in_specs=[_rows_spec(ROW_BLK, src), anyspec, anyspec,
                      bspec(D_EXPERT), bspec(D_EXPERT), bspec(D_MODEL)],
            out_specs=_rows_spec(ROW_BLK, lambda i, *_: (i, 0)),
            scratch_shapes=[
                pltpu.VMEM((2, D_MODEL, 2 * D_EXPERT), F32),
                pltpu.VMEM((2, D_EXPERT, D_MODEL), F32),
                pltpu.VMEM((D_MODEL, D_EXPERT), BF16),
                pltpu.VMEM((D_MODEL, D_EXPERT), BF16),
                pltpu.VMEM((D_EXPERT, D_MODEL), BF16),
                pltpu.SemaphoreType.DMA((2, 2)),
            ],
        ),
        out_shape=jax.ShapeDtypeStruct((n_rows * ROW_SUB, LANES), F32),
        compiler_params=_cparams(("arbitrary",)),
        name="experts",
    )(blk_e, nvalid, first, wslot, enext, xs, w_up, w_down, bg, bu, bd)


def _combine_kernel(cnt_ref, ls_ref, base_ref, ys_hbm, route_ref, gate_ref, x1_ref, p_ref, gp_ref, wpg_ref,
                    wpp_ref, gf_ref, o_ref, yloc, sems):
    tm = TM_ROUTE
    n_pos = TOP_K * tm
    i = pl.program_id(0)
    n_steps = pl.num_programs(0)
    slot = lax.rem(i, 2)

    def gather_tile(tile, s):
        def per_expert(e, _):
            idx = tile * N_EXPERTS + e
            src0 = base_ref[idx]
            dst0 = ls_ref[idx]

            def start_copy(n):
                pltpu.make_async_copy(ys_hbm.at[_row_slice(src0, n)],
                                      yloc.at[s, _row_slice(dst0, n)], sems.at[s]).start()

            _copy_run(cnt_ref[idx], start_copy)
            return 0

        lax.fori_loop(0, N_EXPERTS, per_expert, 0)

    @pl.when(i == 0)
    def _():
        gather_tile(0, 0)

    @pl.when(i + 1 < n_steps)
    def _():
        gather_tile(i + 1, 1 - slot)

    sel = _select_by_position(route_ref, [1.0] * TOP_K, n_pos, tm).astype(BF16)
    gsel = _select_by_position(route_ref, [gate_ref[k:k + 1, :] for k in range(TOP_K)], n_pos, tm)
    g_pos = jnp.sum(gsel, axis=1, keepdims=True)
    pp = _dot(p_ref[...].astype(BF16), wpp_ref[...])

    pltpu.make_async_copy(ys_hbm.at[_row_slice(0, n_pos)], yloc.at[slot], sems.at[slot]).wait()
    y_gated = (_load_rows(yloc.at[slot]) * g_pos).astype(BF16)
    x2 = x1_ref[...] + _dot_tn(sel, y_gated)
    xn = _rms(x2, gp_ref[...], NORM_EPS).astype(BF16)
    x3 = x2 + pp * jax.nn.sigmoid(_dot(xn, wpg_ref[...]))
    o_ref[...] = _rms(x3, gf_ref[...], NORM_EPS)


def _combine(ys, route, gate_t, x1, p2, tables, g_ple, w_ple_gate, w_ple_proj, g_final):
    t = x1.shape[0]
    tm = TM_ROUTE
    cnt_tbl, ls_tbl, base_tbl, _ = tables
    row = lambda n: pl.BlockSpec((tm, n), lambda i, *_: (i, 0))
    col = pl.BlockSpec((SUBLANES, tm), lambda i, *_: (0, i))
    full = lambda a: pl.BlockSpec(a.shape, lambda i, *_: (0, 0))
    consts = (g_ple.reshape(1, -1), w_ple_gate.astype(BF16), w_ple_proj.astype(BF16), g_final.reshape(1, -1))
    return pl.pallas_call(
        _combine_kernel,
        grid_spec=pltpu.PrefetchScalarGridSpec(
            num_scalar_prefetch=3,
            grid=(t // tm,),
            in_specs=[pl.BlockSpec(memory_space=pl.ANY), col, col, row(D_MODEL), row(PLE_DIM)]
                     + [full(a) for a in consts],
            out_specs=row(D_MODEL),
            scratch_shapes=[
                pltpu.VMEM((2, TOP_K * tm * ROW_SUB, LANES), F32),
                pltpu.SemaphoreType.DMA((2,)),
            ],
        ),
        out_shape=jax.ShapeDtypeStruct((t, D_MODEL), F32),
        compiler_params=_cparams(("arbitrary",)),
        name="combine",
    )(cnt_tbl, ls_tbl, base_tbl, ys, route, gate_t, x1, p2, *consts)


def _routing_tables(cnt, n_blocks):
    tile_cnt = cnt[:, 0, :N_EXPERTS].astype(jnp.int32)
    counts = jnp.sum(tile_cnt, axis=0)
    padded = (counts + ROW_BLK - 1) // ROW_BLK * ROW_BLK
    pend = jnp.cumsum(padded)
    pstart = pend - padded
    base = pstart[None, :] + jnp.cumsum(tile_cnt, axis=0) - tile_cnt
    lstart = jnp.cumsum(tile_cnt, axis=1) - tile_cnt
    nvalid = pend[-1:] // ROW_BLK
    zoff = jnp.concatenate([pstart + counts, nvalid])
    blk_start = jnp.minimum(jnp.arange(n_blocks, dtype=jnp.int32) * ROW_BLK, pend[-1] - 1)
    blk_e = jnp.minimum(jnp.sum((pend[None, :] <= blk_start[:, None]).astype(jnp.int32), axis=1), N_EXPERTS - 1)
    first = jnp.concatenate([jnp.ones((1,), bool), blk_e[1:] != blk_e[:-1]])
    wslot = (jnp.cumsum(first.astype(jnp.int32)) - 1) % 2
    eids = jnp.arange(N_EXPERTS, dtype=jnp.int32)
    later_nonempty = (eids[None, :] > eids[:, None]) & (padded[None, :] > 0)
    next_e = jnp.min(jnp.where(later_nonempty, eids[None, :], N_EXPERTS), axis=1)
    next_e = jnp.where(next_e == N_EXPERTS, -1, next_e)
    i32 = lambda a: a.reshape(-1).astype(jnp.int32)
    plan = (i32(blk_e), i32(nvalid), i32(first), i32(wslot), i32(next_e[blk_e]))
    return (i32(tile_cnt), i32(lstart), i32(base), i32(zoff)), plan


def kernel(x, p, g_mix, w_in, conv_w, conv_b, dt_bias, a_log, d_skip, ssd_norm_w, lam_q1, lam_k1, lam_q2, lam_k2, subln_w, w_out, g_ffn, w_router, b_router, w_up, b_up, w_down, b_down, g_ple, w_ple_gate, w_ple_proj, g_final):
    bsz, seq, d = x.shape
    t = bsz * seq
    x2 = x.reshape(t, d)
    y_ssd, y_att = _mixer(x2, g_mix[0], w_in[0], conv_w[0], conv_b[0], dt_bias[0], a_log[0], d_skip[0], ssd_norm_w[0],
                          lam_q1[0], lam_k1[0], lam_q2[0], lam_k2[0], subln_w[0], bsz, seq)
    x1, xn, route, gate_t, cnt = _out_proj(x2, y_ssd, y_att, w_out[0], g_ffn[0], w_router[0], b_router[0])

    n_rows = t * TOP_K + N_EXPERTS * ROW_BLK
    tables, plan = _routing_tables(cnt, n_rows // ROW_BLK)
    xs = _dispatch(xn, route, tables, n_rows)
    ys = _experts(xs, plan, w_up[0], b_up[0], w_down[0], b_down[0], n_rows)
    out = _combine(ys, route, gate_t, x1, p[0].reshape(t, PLE_DIM), tables, g_ple[0], w_ple_gate[0], w_ple_proj[0],
                   g_final)
    return out.reshape(bsz, seq, d)
```

```python
import math

import jax
import jax.numpy as jnp
from jax import lax
from jax.experimental import pallas as pl
from jax.experimental.pallas import tpu as pltpu

F32 = jnp.float32
BF16 = jnp.bfloat16

D_MODEL = 1024
PLE_DIM = 256
SSD_WIDTH = 512
ATT_WIDTH = 512
SSD_HEAD_DIM = 64
SSD_HEADS = 8
SSD_GROUPS = 2
SSD_STATE = 128
SSD_CONV = 4
SSD_CHUNK = 128
SSD_CONV_CH = SSD_WIDTH + 2 * SSD_GROUPS * SSD_STATE
SSD_NORM_EPS = 1e-5
ATT_HEAD_DIM = 64
ATT_HEADS = 4
SUBLN_EPS = 1e-5
OFF_Z = 0
OFF_XBC = OFF_Z + SSD_WIDTH
OFF_DT = OFF_XBC + SSD_CONV_CH
OFF_Q = OFF_DT + SSD_HEADS
OFF_K = OFF_Q + ATT_WIDTH
OFF_V = OFF_K + ATT_WIDTH
IN_PROJ = OFF_V + ATT_WIDTH
N_EXPERTS = 32
TOP_K = 4
D_EXPERT = 1024
SWIGLU_LIMIT = 7.0
SWIGLU_ALPHA = 1.702
NORM_EPS = 1e-6
LAM_INIT = 0.8 - 0.6 * math.exp(-0.3 * 0)

LANES = 128
SUBLANES = 8
VMEM_LIMIT_BYTES = 56 * 1024 * 1024
ROW_SUB = D_MODEL // LANES

TM_PROJ = 512
SSD_PAR = 2
ATT_BQ = 512
ATT_BK = 512
ROW_BLK = 512
ROW_PIECE = 256
TM_ROUTE = 256
OUT_SUB = 4
COMB_SUB = 2
DEINT = 2 * LANES


def _cparams(sem):
    return pltpu.CompilerParams(dimension_semantics=sem, vmem_limit_bytes=VMEM_LIMIT_BYTES)


def _rms(x, w, eps):
    return x * lax.rsqrt(jnp.mean(x * x, axis=-1, keepdims=True) + eps) * w


def _dot(a, b):
    return jnp.dot(a, b, preferred_element_type=F32)


def _dot_nt(a, b):
    return lax.dot_general(a, b, (((1,), (1,)), ((), ())), preferred_element_type=F32)


def _dot_tn(a, b):
    return lax.dot_general(a, b, (((0,), (0,)), ((), ())), preferred_element_type=F32)


def _store_rows(ref, val):
    n = val.shape[0]
    for s in range(ROW_SUB):
        ref[pl.ds(s, n, stride=ROW_SUB), :] = val[:, s * LANES:(s + 1) * LANES]


def _load_rows(ref):
    n = ref.shape[0] // ROW_SUB
    return jnp.concatenate([ref[pl.ds(s, n, stride=ROW_SUB), :] for s in range(ROW_SUB)], axis=-1)


def _rows_spec(n, index_map):
    return pl.BlockSpec((n * ROW_SUB, LANES), index_map)


def _row_slice(start, n):
    return pl.ds(pl.multiple_of(start * ROW_SUB, ROW_SUB), n * ROW_SUB)


def _split3(x):
    hi = x.astype(BF16)
    r1 = x - hi.astype(F32)
    mid = r1.astype(BF16)
    lo = (r1 - mid.astype(F32)).astype(BF16)
    return hi, mid, lo


def _in_proj_kernel(x_ref, g_ref, wz_ref, wxbc_ref, wdt_ref, wq_ref, wk_ref, wv_ref,
                    z_ref, xbc_ref, dt_ref, q_ref, k_ref, v_ref):
    h = _rms(x_ref[...], g_ref[...], NORM_EPS).astype(BF16)
    z_ref[...] = _dot(h, wz_ref[...])
    xbc_ref[...] = _dot(h, wxbc_ref[...])
    dt_ref[...] = _dot(h, wdt_ref[...])
    q_ref[...] = (_dot(h, wq_ref[...]) * (ATT_HEAD_DIM ** -0.5 * math.log2(math.e))).astype(BF16)
    k_ref[...] = _dot(h, wk_ref[...]).astype(BF16)
    v_ref[...] = _dot(h, wv_ref[...]).astype(BF16)


def _in_proj(x2, g_mix, w_in):
    t = x2.shape[0]
    wb = w_in.astype(BF16)
    wz = wb[:, OFF_Z:OFF_XBC]
    wxbc = wb[:, OFF_XBC:OFF_DT]
    wdt = jnp.pad(wb[:, OFF_DT:OFF_Q], ((0, 0), (0, LANES - SSD_HEADS)))
    wq = wb[:, OFF_Q:OFF_K]
    wk = wb[:, OFF_K:OFF_V]
    wv = wb[:, OFF_V:IN_PROJ]
    tm = TM_PROJ
    row = lambda n: pl.BlockSpec((tm, n), lambda i: (i, 0))
    full = lambda a: pl.BlockSpec(a.shape, lambda i: (0, 0))
    return pl.pallas_call(
        _in_proj_kernel,
        grid=(t // tm,),
        in_specs=[row(D_MODEL), full(g_mix), full(wz), full(wxbc), full(wdt), full(wq), full(wk), full(wv)],
        out_specs=[row(SSD_WIDTH), row(SSD_CONV_CH), row(LANES), row(ATT_WIDTH), row(ATT_WIDTH), row(ATT_WIDTH)],
        out_shape=[
            jax.ShapeDtypeStruct((t, SSD_WIDTH), F32),
            jax.ShapeDtypeStruct((t, SSD_CONV_CH), F32),
            jax.ShapeDtypeStruct((t, LANES), F32),
            jax.ShapeDtypeStruct((t, ATT_WIDTH), BF16),
            jax.ShapeDtypeStruct((t, ATT_WIDTH), BF16),
            jax.ShapeDtypeStruct((t, ATT_WIDTH), BF16),
        ],
        compiler_params=_cparams(("arbitrary",)),
        name="in_proj",
    )(x2, g_mix, wz, wxbc, wdt, wq, wk, wv)


def _ssd_kernel(xbc_ref, dtr_ref, z_ref, cw_ref, cb_ref, dtb_ref, alog_ref, dskip_ref, nw_ref,
                y_ref, xpad_ref, state_ref):
    for bb in range(SSD_PAR):
        _ssd_chunk(xbc_ref.at[bb], dtr_ref.at[bb], z_ref.at[bb], cw_ref, cb_ref, dtb_ref, alog_ref, dskip_ref,
                   nw_ref, y_ref.at[bb], xpad_ref.at[bb], state_ref.at[bb])


def _ssd_chunk(xbc_ref, dtr_ref, z_ref, cw_ref, cb_ref, dtb_ref, alog_ref, dskip_ref, nw_ref,
               y_ref, xpad_ref, state_ref):
    L = SSD_CHUNK
    c = pl.program_id(1)

    @pl.when(c == 0)
    def _():
        xpad_ref[0:SUBLANES, :] = jnp.zeros((SUBLANES, SSD_CONV_CH), F32)
        state_ref[...] = jnp.zeros_like(state_ref)

    @pl.when(c != 0)
    def _():
        xpad_ref[0:SUBLANES, :] = xpad_ref[L:L + SUBLANES, :]

    xpad_ref[SUBLANES:SUBLANES + L, :] = xbc_ref[...]

    conv = cb_ref[...]
    for j in range(SSD_CONV):
        off = SUBLANES - (SSD_CONV - 1) + j
        conv = conv + cw_ref[j:j + 1, :] * xpad_ref[off:off + L, :]
    act = conv * jax.nn.sigmoid(conv)
    xs = act[:, :SSD_WIDTH]
    bm = act[:, SSD_WIDTH:SSD_WIDTH + SSD_GROUPS * SSD_STATE].astype(BF16)
    cm = act[:, SSD_WIDTH + SSD_GROUPS * SSD_STATE:].astype(BF16)

    dt_in = dtr_ref[...] + dtb_ref[...]
    dt_all = jnp.maximum(dt_in, 0.0) + jnp.log1p(jnp.exp(-jnp.abs(dt_in)))
    adt = dt_all * (-jnp.exp(alog_ref[...]))

    ri = lax.broadcasted_iota(jnp.int32, (L, L), 0)
    ci = lax.broadcasted_iota(jnp.int32, (L, L), 1)
    causal = ci <= ri
    tril = jnp.where(causal, 1.0, 0.0).astype(BF16)
    hi, mid, lo = _split3(adt)
    acum_all = _dot(tril, hi) + _dot(tril, mid) + _dot(tril, lo)
    acum_t = acum_all.T
    a_last = acum_all[L - 1:L, :]
    decay_in_all = jnp.exp(a_last - acum_all)
    decay_out_all = jnp.exp(acum_all)
    chunk_decay_all = jnp.exp(a_last)
    acum = lambda h: acum_all[:, h:h + 1]
    dt = lambda h: dt_all[:, h:h + 1]
    decay_out = lambda h: decay_out_all[:, h:h + 1]
    decay_in = lambda h: decay_in_all[:, h:h + 1]
    chunk_decay = lambda h: chunk_decay_all[:, h:h + 1]

    lane = lax.broadcasted_iota(jnp.int32, (L, LANES), 1)
    lo_half = lane < SSD_HEAD_DIM

    def per_pair(col_a, col_b):
        return jnp.where(lo_half, col_a, col_b)

    ys = []
    for pair in range(SSD_HEADS // 2):
        g = pair // 2
        h0, h1 = 2 * pair, 2 * pair + 1
        cg = cm[:, g * SSD_STATE:(g + 1) * SSD_STATE]
        bg = bm[:, g * SSD_STATE:(g + 1) * SSD_STATE]
        cb = _dot_nt(cg, bg)
        x_pair = xs[:, pair * LANES:(pair + 1) * LANES]
        xdt = x_pair * per_pair(dt(h0), dt(h1))
        y_pair = jnp.zeros((L, LANES), F32)
        for hh, keep in ((h0, lo_half), (h1, jnp.logical_not(lo_half))):
            seg = acum(hh) - acum_t[hh:hh + 1, :]
            lmat = jnp.where(causal, jnp.exp(jnp.where(causal, seg, 0.0)), 0.0)
            m = (cb * lmat).astype(BF16)
            y_pair = y_pair + _dot(m, jnp.where(keep, xdt, 0.0).astype(BF16))
        s_prev = state_ref[pair]
        y_off = _dot(cg, s_prev.astype(BF16)) * per_pair(decay_out(h0), decay_out(h1))
        w_in = (xdt * per_pair(decay_in(h0), decay_in(h1))).astype(BF16)
        cd = jnp.where(lane[0:1, :] < SSD_HEAD_DIM, chunk_decay(h0), chunk_decay(h1))
        state_ref[pair] = s_prev * cd + _dot_tn(bg, w_in)
        ys.append(y_pair + y_off + dskip_ref[:, pair * LANES:(pair + 1) * LANES] * x_pair)

    y = jnp.concatenate(ys, axis=-1)
    zz = z_ref[...]
    y = y * (zz * jax.nn.sigmoid(zz))
    gw = SSD_WIDTH // SSD_GROUPS
    outs = []
    for g in range(SSD_GROUPS):
        yg = y[:, g * gw:(g + 1) * gw]
        outs.append(yg * lax.rsqrt(jnp.mean(yg * yg, axis=-1, keepdims=True) + SSD_NORM_EPS))
    y_ref[...] = (jnp.concatenate(outs, axis=-1) * nw_ref[...]).astype(BF16)


def _ssd(xbc, dtr, z, conv_w, conv_b, dt_bias, a_log, d_skip, norm_w, bsz, seq):
    nc = seq // SSD_CHUNK
    pad_h = lambda v: jnp.pad(v.reshape(1, SSD_HEADS), ((0, 0), (0, LANES - SSD_HEADS)))
    dskip_lanes = jnp.repeat(d_skip, SSD_HEAD_DIM).reshape(1, SSD_WIDTH)
    seqs = lambda a: a.reshape(bsz, seq, a.shape[-1])
    row = lambda n: pl.BlockSpec((SSD_PAR, SSD_CHUNK, n), lambda b, c: (b, c, 0))
    full = lambda a: pl.BlockSpec(a.shape, lambda b, c: (0, 0))
    args = (seqs(xbc), seqs(dtr), seqs(z), conv_w, conv_b.reshape(1, -1), pad_h(dt_bias), pad_h(a_log), dskip_lanes,
            norm_w.reshape(1, -1))
    y = pl.pallas_call(
        _ssd_kernel,
        grid=(bsz // SSD_PAR, nc),
        in_specs=[row(SSD_CONV_CH), row(LANES), row(SSD_WIDTH)] + [full(a) for a in args[3:]],
        out_specs=row(SSD_WIDTH),
        out_shape=jax.ShapeDtypeStruct((bsz, seq, SSD_WIDTH), BF16),
        scratch_shapes=[
            pltpu.VMEM((SSD_PAR, SSD_CHUNK + 2 * SUBLANES, SSD_CONV_CH), F32),
            pltpu.VMEM((SSD_PAR, SSD_HEADS // 2, SSD_STATE, LANES), F32),
        ],
        compiler_params=_cparams(("arbitrary", "arbitrary")),
        name="ssd",
    )(*args)
    return y.reshape(bsz * seq, SSD_WIDTH)


def _attn_kernel(q_ref, k_ref, v_ref, lq1_ref, lk1_ref, lq2_ref, lk2_ref, sw_ref, o_ref, s_scr, vt_scr):
    bq, bk = ATT_BQ, ATT_BK
    qi = pl.program_id(2)
    n_maps = 2

    @pl.when(qi == 0)
    def _():
        vt_scr[...] = v_ref[...].astype(F32).T.astype(BF16)

    def fold(t, reduce):
        return reduce(t.reshape(bk // SUBLANES, SUBLANES, bq), axis=0)

    def merge(old, new, op):
        return new if old is None else op(old, new)

    def attend(nk):
        q = q_ref[...]
        lane = lax.broadcasted_iota(jnp.int32, (bq, LANES), 1)
        zero = jnp.zeros_like(q)
        q_maps = (jnp.where(lane < ATT_HEAD_DIM, q, zero), jnp.where(lane >= ATT_HEAD_DIM, q, zero))
        key = lax.broadcasted_iota(jnp.int32, (bk, bq), 0)
        qry = lax.broadcasted_iota(jnp.int32, (bk, bq), 1)
        causal = key <= qry

        mt = [None] * n_maps
        for j in range(nk):
            kb = k_ref[j * bk:(j + 1) * bk, :]
            for m in range(n_maps):
                s = _dot_nt(kb, q_maps[m])
                if j == nk - 1:
                    s = jnp.where(causal, s, -jnp.inf)
                s_scr[m, j] = s
                mt[m] = merge(mt[m], fold(s, jnp.max), jnp.maximum)
        q_max = [jnp.max(t, axis=0, keepdims=True) for t in mt]

        lt = [None] * n_maps
        acc = [None] * n_maps
        for j in range(nk):
            vt = vt_scr[:, j * bk:(j + 1) * bk]
            for m in range(n_maps):
                p = jnp.exp2(s_scr[m, j] - q_max[m])
                lt[m] = merge(lt[m], fold(p, jnp.sum), jnp.add)
                acc[m] = merge(acc[m], _dot(vt, p.astype(BF16)), jnp.add)

        lam = (jnp.exp(jnp.sum(lq1_ref[...] * lk1_ref[...], axis=-1, keepdims=True))
               - jnp.exp(jnp.sum(lq2_ref[...] * lk2_ref[...], axis=-1, keepdims=True)) + LAM_INIT)
        l1 = jnp.sum(lt[0], axis=0, keepdims=True)
        l2 = jnp.sum(lt[1], axis=0, keepdims=True)
        o = acc[0] / l1 - lam * (acc[1] / l2)
        o = o * lax.rsqrt(jnp.mean(o * o, axis=0, keepdims=True) + SUBLN_EPS) * sw_ref[...]
        o_ref[...] = (o * (1.0 - LAM_INIT)).T.astype(BF16)

    for nk in range(1, k_ref.shape[0] // bk + 1):
        pl.when(qi == nk - 1)(lambda nk=nk: attend(nk))


def _attn(q, k, v, lam_q1, lam_k1, lam_q2, lam_k2, subln_w, bsz, seq):
    nq = seq // ATT_BQ
    qspec = pl.BlockSpec((ATT_BQ, LANES), lambda b, h, i: (b * nq + i, h))
    kvspec = pl.BlockSpec((seq, LANES), lambda b, h, i: (b, h))
    full = lambda a: pl.BlockSpec(a.shape, lambda b, h, i: (0, 0))
    lams = [a.reshape(1, -1) for a in (lam_q1, lam_k1, lam_q2, lam_k2)]
    sw = subln_w.reshape(-1, 1)
    return pl.pallas_call(
        _attn_kernel,
        grid=(bsz, ATT_HEADS, nq),
        in_specs=[qspec, kvspec, kvspec] + [full(a) for a in lams] + [full(sw)],
        out_specs=qspec,
        out_shape=jax.ShapeDtypeStruct((bsz * seq, ATT_WIDTH), BF16),
        scratch_shapes=[pltpu.VMEM((2, seq // ATT_BK, ATT_BK, ATT_BQ), F32),
                        pltpu.VMEM((LANES, seq), BF16)],
        compiler_params=_cparams(("arbitrary", "arbitrary", "arbitrary")),
        name="attn",
    )(q, k, v, *lams, sw)


def _mixer(x2, g_mix, w_in, conv_w, conv_b, dt_bias, a_log, d_skip, ssd_norm_w,
           lam_q1, lam_k1, lam_q2, lam_k2, subln_w, bsz, seq):
    z, xbc, dtr, q, k, v = _in_proj(x2, g_mix.reshape(1, -1), w_in)
    y_ssd = _ssd(xbc, dtr, z, conv_w, conv_b, dt_bias, a_log, d_skip, ssd_norm_w, bsz, seq)
    y_att = _attn(q, k, v, lam_q1, lam_k1, lam_q2, lam_k2, subln_w, bsz, seq)
    return y_ssd, y_att


def _out_proj_kernel(x_ref, ys_ref, ya_ref, wo_ref, g_ref, wr2_ref, br_ref,
                     x1_ref, xn_ref, route_ref, gate_ref, cnt_ref):
    tm = TM_ROUTE
    x1 = x_ref[...] + _dot(jnp.concatenate([ys_ref[...], ya_ref[...]], axis=-1), wo_ref[...])
    x1_ref[...] = x1
    xn_all = _rms(x1, g_ref[...], NORM_EPS)
    xn_ref[...] = xn_all.astype(BF16)

    n_tok = OUT_SUB * tm
    xh, xm, _ = _split3(xn_all)
    lg2 = _dot(xh, wr2_ref[...]) + _dot(xm, wr2_ref[...])
    logits = (lg2[:, :LANES] + lg2[:, LANES:]).T[:N_EXPERTS, :] + br_ref[...]

    eidx = lax.broadcasted_iota(jnp.int32, (N_EXPERTS, n_tok), 0).astype(F32)
    work = logits
    vals, idxs, hots = [], [], []
    for _ in range(TOP_K):
        m = jnp.max(work, axis=0, keepdims=True)
        idx = jnp.min(jnp.where(work == m, eidx, float(N_EXPERTS)), axis=0, keepdims=True)
        hot = eidx == idx
        vals.append(m)
        idxs.append(idx.astype(jnp.int32))
        hots.append(hot)
        work = jnp.where(hot, -jnp.inf, work)
    exps = [jnp.exp(v - vals[0]) for v in vals]
    denom = exps[0] + exps[1] + exps[2] + exps[3]
    gates = [e / denom for e in exps]

    cnt = jnp.zeros((N_EXPERTS, n_tok), F32)
    for hot in hots:
        cnt = cnt + jnp.where(hot, 1.0, 0.0)
    cnt_b = cnt.astype(BF16)
    r = lax.broadcasted_iota(jnp.int32, (tm, tm), 0)
    c = lax.broadcasted_iota(jnp.int32, (tm, tm), 1)
    earlier_tok = jnp.where(r < c, 1.0, 0.0).astype(BF16)
    er = lax.broadcasted_iota(jnp.int32, (N_EXPERTS, N_EXPERTS), 0)
    ec = lax.broadcasted_iota(jnp.int32, (N_EXPERTS, N_EXPERTS), 1)
    lower_exp = jnp.where(ec < er, 1.0, 0.0).astype(BF16)
    below = _dot(lower_exp, cnt_b)
    cnt_pad = jnp.concatenate([cnt_b, jnp.zeros((LANES - N_EXPERTS, n_tok), BF16)], axis=0)
    pos = []
    for sub in range(OUT_SUB):
        cols = slice(sub * tm, (sub + 1) * tm)
        pos.append(_dot(cnt_b[:, cols], earlier_tok) + jnp.sum(below[:, cols], axis=1, keepdims=True))
        cnt_ref[sub] = _dot_nt(jnp.ones((SUBLANES, tm), BF16), cnt_pad[:, cols])
    pos = jnp.concatenate(pos, axis=1)
    lps = [jnp.sum(jnp.where(hot, pos, 0.0), axis=0, keepdims=True).astype(jnp.int32) for hot in hots]

    route_ref[...] = jnp.concatenate(idxs + lps, axis=0)
    gate_ref[...] = jnp.concatenate(gates + [jnp.zeros((SUBLANES - TOP_K, n_tok), F32)], axis=0)


def _out_proj(x2, y_ssd, y_att, w_out, g_ffn, w_router, b_router):
    t = x2.shape[0]
    tm = OUT_SUB * TM_ROUTE
    nt = t // TM_ROUTE
    wo = w_out.astype(BF16)
    wrh, wrm, _ = _split3(jnp.pad(w_router, ((0, 0), (0, LANES - N_EXPERTS))))
    wr2 = jnp.concatenate([wrh, wrm], axis=1)
    br = b_router.reshape(N_EXPERTS, 1)
    row = lambda n: pl.BlockSpec((tm, n), lambda i: (i, 0))
    col = pl.BlockSpec((SUBLANES, tm), lambda i: (0, i))
    full = lambda a: pl.BlockSpec(a.shape, lambda i: (0, 0))
    args = (x2, y_ssd, y_att, wo, g_ffn.reshape(1, -1), wr2, br)
    return pl.pallas_call(
        _out_proj_kernel,
        grid=(t // tm,),
        in_specs=[row(D_MODEL), row(SSD_WIDTH), row(ATT_WIDTH)] + [full(a) for a in args[3:]],
        out_specs=[row(D_MODEL), row(D_MODEL), col, col,
                   pl.BlockSpec((OUT_SUB, SUBLANES, LANES), lambda i: (i, 0, 0))],
        out_shape=[
            jax.ShapeDtypeStruct((t, D_MODEL), F32),
            jax.ShapeDtypeStruct((t, D_MODEL), BF16),
            jax.ShapeDtypeStruct((SUBLANES, t), jnp.int32),
            jax.ShapeDtypeStruct((SUBLANES, t), F32),
            jax.ShapeDtypeStruct((nt, SUBLANES, LANES), F32),
        ],
        compiler_params=_cparams(("arbitrary",)),
        name="out_proj",
    )(*args)


def _copy_run(n, start_copy):
    @pl.when(n > 0)
    def _():
        start_copy(n)


def _select_by_position(positions, values, n_pos):
    tm = positions[0].shape[1]
    r = lax.broadcasted_iota(jnp.int32, (n_pos, tm), 0)
    out = jnp.zeros((n_pos, tm), F32)
    for k in reversed(range(TOP_K)):
        out = jnp.where(r == positions[k], values[k], out)
    return out


def _dispatch_kernel(cnt_ref, ls_ref, base_ref, zoff_ref, xn_ref, route_ref, xs_hbm, xloc, zeros_vmem, sem_z, sems):
    tm = TM_ROUTE
    n_pos = TOP_K * tm
    i = pl.program_id(0)
    n_steps = pl.num_programs(0)
    slot = lax.rem(i, 2)

    @pl.when(i == 0)
    def _():
        zeros_vmem[...] = jnp.zeros_like(zeros_vmem)

        def zfill(e, _):
            pltpu.make_async_copy(zeros_vmem, xs_hbm.at[_row_slice(zoff_ref[e], ROW_BLK)], sem_z).start()
            return 0

        def zwait(e, _):
            pltpu.make_async_copy(zeros_vmem, xs_hbm.at[_row_slice(0, ROW_BLK)], sem_z).wait()
            return 0

        def tfill(b, _):
            pltpu.make_async_copy(zeros_vmem, xs_hbm.at[_row_slice(b * ROW_BLK, ROW_BLK)], sem_z).start()
            return 0

        lax.fori_loop(0, N_EXPERTS, zfill, 0)
        lax.fori_loop(0, N_EXPERTS, zwait, 0)
        n_blk_total = xs_hbm.shape[0] // (ROW_BLK * ROW_SUB)
        lax.fori_loop(zoff_ref[N_EXPERTS], n_blk_total, tfill, 0)
        lax.fori_loop(zoff_ref[N_EXPERTS], n_blk_total, zwait, 0)

    positions = [route_ref[TOP_K + k:TOP_K + k + 1, :] for k in range(TOP_K)]
    sel = _select_by_position(positions, [1.0] * TOP_K, n_pos).astype(BF16)
    rows = _dot(sel, xn_ref[...])

    def slot_wait(s):
        pltpu.make_async_copy(xloc.at[s], xs_hbm.at[_row_slice(0, n_pos)], sems.at[s]).wait()

    @pl.when(i >= 2)
    def _():
        slot_wait(slot)

    _store_rows(xloc.at[slot], rows)

    def per_expert(e, _):
        idx = i * N_EXPERTS + e
        src0 = ls_ref[idx]
        dst0 = base_ref[idx]

        def start_copy(n):
            pltpu.make_async_copy(xloc.at[slot, _row_slice(src0, n)],
                                  xs_hbm.at[_row_slice(dst0, n)], sems.at[slot]).start()

        _copy_run(cnt_ref[idx], start_copy)
        return 0

    lax.fori_loop(0, N_EXPERTS, per_expert, 0)

    @pl.when(i == n_steps - 1)
    def _():
        slot_wait(slot)

        @pl.when(n_steps > 1)
        def _():
            slot_wait(1 - slot)


def _dispatch(xn, route, tables, n_rows):
    t = xn.shape[0]
    tm = TM_ROUTE
    cnt_tbl, ls_tbl, base_tbl, zoff = tables
    return pl.pallas_call(
        _dispatch_kernel,
        grid_spec=pltpu.PrefetchScalarGridSpec(
            num_scalar_prefetch=4,
            grid=(t // tm,),
            in_specs=[pl.BlockSpec((tm, D_MODEL), lambda i, *_: (i, 0)),
                      pl.BlockSpec((SUBLANES, tm), lambda i, *_: (0, i))],
            out_specs=pl.BlockSpec(memory_space=pl.ANY),
            scratch_shapes=[
                pltpu.VMEM((2, TOP_K * tm * ROW_SUB, LANES), F32),
                pltpu.VMEM((ROW_BLK * ROW_SUB, LANES), F32),
                pltpu.SemaphoreType.DMA,
                pltpu.SemaphoreType.DMA((2,)),
            ],
        ),
        out_shape=jax.ShapeDtypeStruct(((n_rows + ROW_BLK) * ROW_SUB, LANES), F32),
        compiler_params=_cparams(("arbitrary",)),
        name="dispatch",
    )(cnt_tbl, ls_tbl, base_tbl, zoff, xn, route)


def _experts_kernel(blk_e_ref, nvalid_ref, first_ref, wslot_ref, enext_ref, pieces_ref, xs_ref, wup_hbm, wdn_hbm,
                    bg_ref, bu_ref, bd_ref, ys_ref, wup_buf, wdn_buf, wg_s, wu_s, wd_s, sems):
    i = pl.program_id(0)
    slot = wslot_ref[i]

    def weight_copies(e, s):
        return (pltpu.make_async_copy(wup_hbm.at[e], wup_buf.at[s], sems.at[0, s]),
                pltpu.make_async_copy(wdn_hbm.at[e], wdn_buf.at[s], sems.at[1, s]))

    @pl.when(i == 0)
    def _():
        for cp in weight_copies(blk_e_ref[0], slot):
            cp.start()

    @pl.when(first_ref[i] != 0)
    def _():
        for cp in weight_copies(blk_e_ref[i], slot):
            cp.wait()

        @pl.when(enext_ref[i] >= 0)
        def _():
            for cp in weight_copies(enext_ref[i], 1 - slot):
                cp.start()

        src = lax.broadcasted_iota(jnp.int32, (DEINT, DEINT), 0)
        dst = lax.broadcasted_iota(jnp.int32, (DEINT, DEINT), 1)
        perm = jnp.where(src == jnp.where(dst < LANES, 2 * dst, 2 * (dst - LANES) + 1), 1.0, 0.0).astype(BF16)
        for g in range(2 * D_EXPERT // DEINT):
            sep = _dot(wup_buf[slot, :, g * DEINT:(g + 1) * DEINT].astype(BF16), perm)
            wg_s[:, g * LANES:(g + 1) * LANES] = sep[:, :LANES].astype(BF16)
            wu_s[:, g * LANES:(g + 1) * LANES] = sep[:, LANES:].astype(BF16)
        wd_s[...] = wdn_buf[slot].astype(BF16)

    def ffn(n_rows):
        used = n_rows * ROW_SUB
        if n_rows:
            xb = _load_rows(xs_ref.at[pl.ds(0, used)]).astype(BF16)
            gate = jnp.minimum(_dot(xb, wg_s[...]) + bg_ref[0], SWIGLU_LIMIT)
            up = jnp.clip(_dot(xb, wu_s[...]) + bu_ref[0], -SWIGLU_LIMIT, SWIGLU_LIMIT)
            act = (up + 1.0) * gate * jax.nn.sigmoid(SWIGLU_ALPHA * gate)
            _store_rows(ys_ref.at[pl.ds(0, used)], _dot(act.astype(BF16), wd_s[...]) + bd_ref[0])
        if n_rows < ROW_BLK:
            ys_ref[used:, :] = jnp.zeros((ROW_BLK * ROW_SUB - used, LANES), F32)

    for pieces in range(ROW_BLK // ROW_PIECE + 1):
        pl.when(pieces_ref[i] == pieces)(lambda pieces=pieces: ffn(pieces * ROW_PIECE))


def _experts(xs, plan, w_up, b_up, w_down, b_down, n_rows):
    nb = n_rows // ROW_BLK
    bg = b_up[:, 0::2].reshape(N_EXPERTS, 1, D_EXPERT)
    bu = b_up[:, 1::2].reshape(N_EXPERTS, 1, D_EXPERT)
    bd = b_down.reshape(N_EXPERTS, 1, D_MODEL)
    blk_e, nvalid, first, wslot, enext, pieces = plan
    src = lambda i, be, nv, *_: (jnp.minimum(i, nv[0] - 1), 0)
    bspec = lambda m: pl.BlockSpec((1, 1, m), lambda i, be, *_: (be[i], 0, 0))
    anyspec = pl.BlockSpec(memory_space=pl.ANY)
    return pl.pallas_call(
        _experts_kernel,
        grid_spec=pltpu.PrefetchScalarGridSpec(
            num_scalar_prefetch=6,
            grid=(nb,),
            in_specs=[_rows_spec(ROW_BLK, src), anyspec, anyspec,
                      bspec(D_EXPERT), bspec(D_EXPERT), bspec(D_MODEL)],
            out_specs=_rows_spec(ROW_BLK, lambda i, *_: (i, 0)),
            scratch_shapes=[
                pltpu.VMEM((2, D_MODEL, 2 * D_EXPERT), F32),
                pltpu.VMEM((2, D_EXPERT, D_MODEL), F32),
                pltpu.VMEM((D_MODEL, D_EXPERT), BF16),
                pltpu.VMEM((D_MODEL, D_EXPERT), BF16),
                pltpu.VMEM((D_EXPERT, D_MODEL), BF16),
                pltpu.SemaphoreType.DMA((2, 2)),
            ],
        ),
        out_shape=jax.ShapeDtypeStruct((n_rows * ROW_SUB, LANES), F32),
        compiler_params=_cparams(("arbitrary",)),
        name="experts",
    )(blk_e, nvalid, first, wslot, enext, pieces, xs, w_up, w_down, bg, bu, bd)


def _combine_kernel(cnt_ref, ls_ref, base_ref, ys_hbm, route_ref, gate_ref, x1_ref, p_ref, gp_ref, wpg_ref,
                    wpp_ref, gf_ref, o_ref, yloc, sems):
    tm = TM_ROUTE
    n_pos = TOP_K * tm
    i = pl.program_id(0)
    n_steps = pl.num_programs(0)
    slot = lax.rem(i, 2)

    def gather_step(step, s):
        for u in range(COMB_SUB):
            def per_expert(e, _, u=u):
                idx = (step * COMB_SUB + u) * N_EXPERTS + e
                src0 = base_ref[idx]
                dst0 = ls_ref[idx]

                def start_copy(n):
                    pltpu.make_async_copy(ys_hbm.at[_row_slice(src0, n)],
                                          yloc.at[s, u, _row_slice(dst0, n)], sems.at[s]).start()

                _copy_run(cnt_ref[idx], start_copy)
                return 0

            lax.fori_loop(0, N_EXPERTS, per_expert, 0)

    @pl.when(i == 0)
    def _():
        gather_step(0, 0)

    @pl.when(i + 1 < n_steps)
    def _():
        gather_step(i + 1, 1 - slot)

    sels, g_poss = [], []
    for u in range(COMB_SUB):
        cols = slice(u * tm, (u + 1) * tm)
        positions = [route_ref[TOP_K + k:TOP_K + k + 1, cols] for k in range(TOP_K)]
        sels.append(_select_by_position(positions, [1.0] * TOP_K, n_pos).astype(BF16))
        gsel = _select_by_position(positions, [gate_ref[k:k + 1, cols] for k in range(TOP_K)], n_pos)
        g_poss.append(jnp.sum(gsel, axis=1, keepdims=True))
    pp = _dot(p_ref[...].astype(BF16), wpp_ref[...])

    for u in range(COMB_SUB):
        pltpu.make_async_copy(ys_hbm.at[_row_slice(0, n_pos)], yloc.at[slot, u], sems.at[slot]).wait()
    moe = []
    for u in range(COMB_SUB):
        y_gated = (_load_rows(yloc.at[slot, u]) * g_poss[u]).astype(BF16)
        moe.append(_dot_tn(sels[u], y_gated))
    x2 = x1_ref[...] + jnp.concatenate(moe, axis=0)
    xn = _rms(x2, gp_ref[...], NORM_EPS).astype(BF16)
    x3 = x2 + pp * jax.nn.sigmoid(_dot(xn, wpg_ref[...]))
    o_ref[...] = _rms(x3, gf_ref[...], NORM_EPS)


def _combine(ys, route, gate_t, x1, p2, tables, g_ple, w_ple_gate, w_ple_proj, g_final):
    t = x1.shape[0]
    tm = COMB_SUB * TM_ROUTE
    cnt_tbl, ls_tbl, base_tbl, _ = tables
    row = lambda n: pl.BlockSpec((tm, n), lambda i, *_: (i, 0))
    col = pl.BlockSpec((SUBLANES, tm), lambda i, *_: (0, i))
    full = lambda a: pl.BlockSpec(a.shape, lambda i, *_: (0, 0))
    consts = (g_ple.reshape(1, -1), w_ple_gate.astype(BF16), w_ple_proj.astype(BF16), g_final.reshape(1, -1))
    return pl.pallas_call(
        _combine_kernel,
        grid_spec=pltpu.PrefetchScalarGridSpec(
            num_scalar_prefetch=3,
            grid=(t // tm,),
            in_specs=[pl.BlockSpec(memory_space=pl.ANY), col, col, row(D_MODEL), row(PLE_DIM)]
                     + [full(a) for a in consts],
            out_specs=row(D_MODEL),
            scratch_shapes=[
                pltpu.VMEM((2, COMB_SUB, TOP_K * TM_ROUTE * ROW_SUB, LANES), F32),
                pltpu.SemaphoreType.DMA((2,)),
            ],
        ),
        out_shape=jax.ShapeDtypeStruct((t, D_MODEL), F32),
        compiler_params=_cparams(("arbitrary",)),
        name="combine",
    )(cnt_tbl, ls_tbl, base_tbl, ys, route, gate_t, x1, p2, *consts)


def _routing_tables(cnt, n_blocks):
    tile_cnt = cnt[:, 0, :N_EXPERTS].astype(jnp.int32)
    counts = jnp.sum(tile_cnt, axis=0)
    padded = (counts + ROW_BLK - 1) // ROW_BLK * ROW_BLK
    pend = jnp.cumsum(padded)
    pstart = pend - padded
    base = pstart[None, :] + jnp.cumsum(tile_cnt, axis=0) - tile_cnt
    lstart = jnp.cumsum(tile_cnt, axis=1) - tile_cnt
    nvalid = pend[-1:] // ROW_BLK
    zoff = jnp.concatenate([pstart + counts, nvalid])
    blk_start = jnp.minimum(jnp.arange(n_blocks, dtype=jnp.int32) * ROW_BLK, pend[-1] - 1)
    blk_e = jnp.minimum(jnp.sum((pend[None, :] <= blk_start[:, None]).astype(jnp.int32), axis=1), N_EXPERTS - 1)
    first = jnp.concatenate([jnp.ones((1,), bool), blk_e[1:] != blk_e[:-1]])
    wslot = (jnp.cumsum(first.astype(jnp.int32)) - 1) % 2
    eids = jnp.arange(N_EXPERTS, dtype=jnp.int32)
    later_nonempty = (eids[None, :] > eids[:, None]) & (padded[None, :] > 0)
    next_e = jnp.min(jnp.where(later_nonempty, eids[None, :], N_EXPERTS), axis=1)
    next_e = jnp.where(next_e == N_EXPERTS, -1, next_e)
    blk_ids = jnp.arange(n_blocks, dtype=jnp.int32)
    real_rows = jnp.clip((pstart + counts)[blk_e] - blk_ids * ROW_BLK, 0, ROW_BLK) * (blk_ids < nvalid[0])
    pieces = (real_rows + ROW_PIECE - 1) // ROW_PIECE
    i32 = lambda a: a.reshape(-1).astype(jnp.int32)
    plan = (i32(blk_e), i32(nvalid), i32(first), i32(wslot), i32(next_e[blk_e]), i32(pieces))
    return (i32(tile_cnt), i32(lstart), i32(base), i32(zoff)), plan


def kernel(x, p, g_mix, w_in, conv_w, conv_b, dt_bias, a_log, d_skip, ssd_norm_w, lam_q1, lam_k1, lam_q2, lam_k2, subln_w, w_out, g_ffn, w_router, b_router, w_up, b_up, w_down, b_down, g_ple, w_ple_gate, w_ple_proj, g_final):
    bsz, seq, d = x.shape
    t = bsz * seq
    x2 = x.reshape(t, d)
    y_ssd, y_att = _mixer(x2, g_mix[0], w_in[0], conv_w[0], conv_b[0], dt_bias[0], a_log[0], d_skip[0], ssd_norm_w[0],
                          lam_q1[0], lam_k1[0], lam_q2[0], lam_k2[0], subln_w[0], bsz, seq)
    x1, xn, route, gate_t, cnt = _out_proj(x2, y_ssd, y_att, w_out[0], g_ffn[0], w_router[0], b_router[0])

    n_rows = t * TOP_K + N_EXPERTS * ROW_BLK
    tables, plan = _routing_tables(cnt, n_rows // ROW_BLK)
    xs = _dispatch(xn, route, tables, n_rows)
    ys = _experts(xs, plan, w_up[0], b_up[0], w_down[0], b_down[0], n_rows)
    out = _combine(ys, route, gate_t, x1, p[0].reshape(t, PLE_DIM), tables, g_ple[0], w_ple_gate[0], w_ple_proj[0],
                   g_final)
    return out.reshape(bsz, seq, d)
```

```python
import math

import jax
import jax.numpy as jnp
from jax import lax
from jax.experimental import pallas as pl
from jax.experimental.pallas import tpu as pltpu

F32 = jnp.float32
BF16 = jnp.bfloat16

D_MODEL = 1024
PLE_DIM = 256
SSD_WIDTH = 512
ATT_WIDTH = 512
SSD_HEAD_DIM = 64
SSD_HEADS = 8
SSD_GROUPS = 2
SSD_STATE = 128
SSD_CONV = 4
SSD_CHUNK = 128
SSD_CONV_CH = SSD_WIDTH + 2 * SSD_GROUPS * SSD_STATE
SSD_NORM_EPS = 1e-5
ATT_HEAD_DIM = 64
ATT_HEADS = 4
SUBLN_EPS = 1e-5
OFF_Z = 0
OFF_XBC = OFF_Z + SSD_WIDTH
OFF_DT = OFF_XBC + SSD_CONV_CH
OFF_Q = OFF_DT + SSD_HEADS
OFF_K = OFF_Q + ATT_WIDTH
OFF_V = OFF_K + ATT_WIDTH
IN_PROJ = OFF_V + ATT_WIDTH
N_EXPERTS = 32
TOP_K = 4
D_EXPERT = 1024
SWIGLU_LIMIT = 7.0
SWIGLU_ALPHA = 1.702
NORM_EPS = 1e-6
LAM_INIT = 0.8 - 0.6 * math.exp(-0.3 * 0)

LANES = 128
SUBLANES = 8
VMEM_LIMIT_BYTES = 56 * 1024 * 1024
ROW_SUB = D_MODEL // LANES

TM_PROJ = 512
SSD_PAR = 2
ATT_BQ = 512
ATT_BK = 512
ROW_BLK = 512
ROW_PIECE = 256
TM_ROUTE = 256
OUT_SUB = 4
COMB_SUB = 2
DEINT = 2 * LANES


def _cparams(sem):
    return pltpu.CompilerParams(dimension_semantics=sem, vmem_limit_bytes=VMEM_LIMIT_BYTES)


def _rms(x, w, eps):
    return x * lax.rsqrt(jnp.mean(x * x, axis=-1, keepdims=True) + eps) * w


def _dot(a, b):
    return jnp.dot(a, b, preferred_element_type=F32)


def _dot_nt(a, b):
    return lax.dot_general(a, b, (((1,), (1,)), ((), ())), preferred_element_type=F32)


def _dot_tn(a, b):
    return lax.dot_general(a, b, (((0,), (0,)), ((), ())), preferred_element_type=F32)


def _store_rows(ref, val):
    n = val.shape[0]
    for s in range(ROW_SUB):
        ref[pl.ds(s, n, stride=ROW_SUB), :] = val[:, s * LANES:(s + 1) * LANES]


def _load_rows(ref):
    n = ref.shape[0] // ROW_SUB
    return jnp.concatenate([ref[pl.ds(s, n, stride=ROW_SUB), :] for s in range(ROW_SUB)], axis=-1)


def _rows_spec(n, index_map):
    return pl.BlockSpec((n * ROW_SUB, LANES), index_map)


def _row_slice(start, n):
    return pl.ds(pl.multiple_of(start * ROW_SUB, ROW_SUB), n * ROW_SUB)


def _split3(x):
    hi = x.astype(BF16)
    r1 = x - hi.astype(F32)
    mid = r1.astype(BF16)
    lo = (r1 - mid.astype(F32)).astype(BF16)
    return hi, mid, lo


def _in_proj_kernel(x_ref, g_ref, wz_ref, wxbc_ref, wdt_ref, wq_ref, wk_ref, wv_ref,
                    z_ref, xbc_ref, dt_ref, q_ref, k_ref, v_ref):
    h = _rms(x_ref[...], g_ref[...], NORM_EPS).astype(BF16)
    z_ref[...] = _dot(h, wz_ref[...])
    xbc_ref[...] = _dot(h, wxbc_ref[...])
    dt_ref[...] = _dot(h, wdt_ref[...])
    q_ref[...] = (_dot(h, wq_ref[...]) * (ATT_HEAD_DIM ** -0.5 * math.log2(math.e))).astype(BF16)
    k_ref[...] = _dot(h, wk_ref[...]).astype(BF16)
    v_ref[...] = _dot(h, wv_ref[...]).astype(BF16)


def _in_proj(x2, g_mix, w_in):
    t = x2.shape[0]
    wb = w_in.astype(BF16)
    wz = wb[:, OFF_Z:OFF_XBC]
    wxbc = wb[:, OFF_XBC:OFF_DT]
    wdt = jnp.pad(wb[:, OFF_DT:OFF_Q], ((0, 0), (0, LANES - SSD_HEADS)))
    wq = wb[:, OFF_Q:OFF_K]
    wk = wb[:, OFF_K:OFF_V]
    wv = wb[:, OFF_V:IN_PROJ]
    tm = TM_PROJ
    row = lambda n: pl.BlockSpec((tm, n), lambda i: (i, 0))
    full = lambda a: pl.BlockSpec(a.shape, lambda i: (0, 0))
    return pl.pallas_call(
        _in_proj_kernel,
        grid=(t // tm,),
        in_specs=[row(D_MODEL), full(g_mix), full(wz), full(wxbc), full(wdt), full(wq), full(wk), full(wv)],
        out_specs=[row(SSD_WIDTH), row(SSD_CONV_CH), row(LANES), row(ATT_WIDTH), row(ATT_WIDTH), row(ATT_WIDTH)],
        out_shape=[
            jax.ShapeDtypeStruct((t, SSD_WIDTH), F32),
            jax.ShapeDtypeStruct((t, SSD_CONV_CH), F32),
            jax.ShapeDtypeStruct((t, LANES), F32),
            jax.ShapeDtypeStruct((t, ATT_WIDTH), BF16),
            jax.ShapeDtypeStruct((t, ATT_WIDTH), BF16),
            jax.ShapeDtypeStruct((t, ATT_WIDTH), BF16),
        ],
        compiler_params=_cparams(("arbitrary",)),
        name="in_proj",
    )(x2, g_mix, wz, wxbc, wdt, wq, wk, wv)


def _ssd_kernel(xbc_ref, dtr_ref, z_ref, cw_ref, cb_ref, dtb_ref, alog_ref, dskip_ref, nw_ref,
                y_ref, xpad_ref, state_ref):
    for bb in range(SSD_PAR):
        _ssd_chunk(xbc_ref.at[bb], dtr_ref.at[bb], z_ref.at[bb], cw_ref, cb_ref, dtb_ref, alog_ref, dskip_ref,
                   nw_ref, y_ref.at[bb], xpad_ref.at[bb], state_ref.at[bb])


def _ssd_chunk(xbc_ref, dtr_ref, z_ref, cw_ref, cb_ref, dtb_ref, alog_ref, dskip_ref, nw_ref,
               y_ref, xpad_ref, state_ref):
    L = SSD_CHUNK
    c = pl.program_id(1)

    @pl.when(c == 0)
    def _():
        xpad_ref[0:SUBLANES, :] = jnp.zeros((SUBLANES, SSD_CONV_CH), F32)
        state_ref[...] = jnp.zeros_like(state_ref)

    @pl.when(c != 0)
    def _():
        xpad_ref[0:SUBLANES, :] = xpad_ref[L:L + SUBLANES, :]

    xpad_ref[SUBLANES:SUBLANES + L, :] = xbc_ref[...]

    conv = cb_ref[...]
    for j in range(SSD_CONV):
        off = SUBLANES - (SSD_CONV - 1) + j
        conv = conv + cw_ref[j:j + 1, :] * xpad_ref[off:off + L, :]
    act = conv * jax.nn.sigmoid(conv)
    xs = act[:, :SSD_WIDTH]
    bm = act[:, SSD_WIDTH:SSD_WIDTH + SSD_GROUPS * SSD_STATE].astype(BF16)
    cm = act[:, SSD_WIDTH + SSD_GROUPS * SSD_STATE:].astype(BF16)

    dt_in = dtr_ref[...] + dtb_ref[...]
    dt_all = jnp.maximum(dt_in, 0.0) + jnp.log1p(jnp.exp(-jnp.abs(dt_in)))
    adt = dt_all * (-jnp.exp(alog_ref[...]))

    ri = lax.broadcasted_iota(jnp.int32, (L, L), 0)
    ci = lax.broadcasted_iota(jnp.int32, (L, L), 1)
    causal = ci <= ri
    tril = jnp.where(causal, 1.0, 0.0).astype(BF16)
    hi, mid, lo = _split3(adt)
    acum_all = _dot(tril, hi) + _dot(tril, mid) + _dot(tril, lo)
    acum_t = acum_all.T
    a_last = acum_all[L - 1:L, :]
    decay_in_all = jnp.exp(a_last - acum_all)
    decay_out_all = jnp.exp(acum_all)
    chunk_decay_all = jnp.exp(a_last)
    acum = lambda h: acum_all[:, h:h + 1]
    dt = lambda h: dt_all[:, h:h + 1]
    decay_out = lambda h: decay_out_all[:, h:h + 1]
    decay_in = lambda h: decay_in_all[:, h:h + 1]
    chunk_decay = lambda h: chunk_decay_all[:, h:h + 1]

    lane = lax.broadcasted_iota(jnp.int32, (L, LANES), 1)
    lo_half = lane < SSD_HEAD_DIM

    def per_pair(col_a, col_b):
        return jnp.where(lo_half, col_a, col_b)

    ys = []
    for pair in range(SSD_HEADS // 2):
        g = pair // 2
        h0, h1 = 2 * pair, 2 * pair + 1
        cg = cm[:, g * SSD_STATE:(g + 1) * SSD_STATE]
        bg = bm[:, g * SSD_STATE:(g + 1) * SSD_STATE]
        cb = _dot_nt(cg, bg)
        x_pair = xs[:, pair * LANES:(pair + 1) * LANES]
        xdt = x_pair * per_pair(dt(h0), dt(h1))
        y_pair = jnp.zeros((L, LANES), F32)
        for hh, keep in ((h0, lo_half), (h1, jnp.logical_not(lo_half))):
            seg = acum(hh) - acum_t[hh:hh + 1, :]
            lmat = jnp.where(causal, jnp.exp(jnp.where(causal, seg, 0.0)), 0.0)
            m = (cb * lmat).astype(BF16)
            y_pair = y_pair + _dot(m, jnp.where(keep, xdt, 0.0).astype(BF16))
        s_prev = state_ref[pair]
        y_off = _dot(cg, s_prev.astype(BF16)) * per_pair(decay_out(h0), decay_out(h1))
        w_in = (xdt * per_pair(decay_in(h0), decay_in(h1))).astype(BF16)
        cd = jnp.where(lane[0:1, :] < SSD_HEAD_DIM, chunk_decay(h0), chunk_decay(h1))
        state_ref[pair] = s_prev * cd + _dot_tn(bg, w_in)
        ys.append(y_pair + y_off + dskip_ref[:, pair * LANES:(pair + 1) * LANES] * x_pair)

    y = jnp.concatenate(ys, axis=-1)
    zz = z_ref[...]
    y = y * (zz * jax.nn.sigmoid(zz))
    gw = SSD_WIDTH // SSD_GROUPS
    outs = []
    for g in range(SSD_GROUPS):
        yg = y[:, g * gw:(g + 1) * gw]
        outs.append(yg * lax.rsqrt(jnp.mean(yg * yg, axis=-1, keepdims=True) + SSD_NORM_EPS))
    y_ref[...] = (jnp.concatenate(outs, axis=-1) * nw_ref[...]).astype(BF16)


def _ssd(xbc, dtr, z, conv_w, conv_b, dt_bias, a_log, d_skip, norm_w, bsz, seq):
    nc = seq // SSD_CHUNK
    pad_h = lambda v: jnp.pad(v.reshape(1, SSD_HEADS), ((0, 0), (0, LANES - SSD_HEADS)))
    dskip_lanes = jnp.repeat(d_skip, SSD_HEAD_DIM).reshape(1, SSD_WIDTH)
    seqs = lambda a: a.reshape(bsz, seq, a.shape[-1])
    row = lambda n: pl.BlockSpec((SSD_PAR, SSD_CHUNK, n), lambda b, c: (b, c, 0))
    full = lambda a: pl.BlockSpec(a.shape, lambda b, c: (0, 0))
    args = (seqs(xbc), seqs(dtr), seqs(z), conv_w, conv_b.reshape(1, -1), pad_h(dt_bias), pad_h(a_log), dskip_lanes,
            norm_w.reshape(1, -1))
    y = pl.pallas_call(
        _ssd_kernel,
        grid=(bsz // SSD_PAR, nc),
        in_specs=[row(SSD_CONV_CH), row(LANES), row(SSD_WIDTH)] + [full(a) for a in args[3:]],
        out_specs=row(SSD_WIDTH),
        out_shape=jax.ShapeDtypeStruct((bsz, seq, SSD_WIDTH), BF16),
        scratch_shapes=[
            pltpu.VMEM((SSD_PAR, SSD_CHUNK + 2 * SUBLANES, SSD_CONV_CH), F32),
            pltpu.VMEM((SSD_PAR, SSD_HEADS // 2, SSD_STATE, LANES), F32),
        ],
        compiler_params=_cparams(("arbitrary", "arbitrary")),
        name="ssd",
    )(*args)
    return y.reshape(bsz * seq, SSD_WIDTH)


def _attn_kernel(q_ref, k_ref, v_ref, lq1_ref, lk1_ref, lq2_ref, lk2_ref, sw_ref, o_ref, s_scr, vt_scr):
    bq, bk = ATT_BQ, ATT_BK
    qi = pl.program_id(2)
    n_maps = 2

    @pl.when(qi == 0)
    def _():
        vt_scr[...] = v_ref[...].astype(F32).T.astype(BF16)

    def fold(t, reduce):
        return reduce(t.reshape(bk // SUBLANES, SUBLANES, bq), axis=0)

    def merge(old, new, op):
        return new if old is None else op(old, new)

    def attend(nk):
        q = q_ref[...]
        lane = lax.broadcasted_iota(jnp.int32, (bq, LANES), 1)
        zero = jnp.zeros_like(q)
        q_maps = (jnp.where(lane < ATT_HEAD_DIM, q, zero), jnp.where(lane >= ATT_HEAD_DIM, q, zero))
        key = lax.broadcasted_iota(jnp.int32, (bk, bq), 0)
        qry = lax.broadcasted_iota(jnp.int32, (bk, bq), 1)
        causal = key <= qry

        mt = [None] * n_maps
        for j in range(nk):
            kb = k_ref[j * bk:(j + 1) * bk, :]
            for m in range(n_maps):
                s = _dot_nt(kb, q_maps[m])
                if j == nk - 1:
                    s = jnp.where(causal, s, -jnp.inf)
                s_scr[m, j] = s
                mt[m] = merge(mt[m], fold(s, jnp.max), jnp.maximum)
        q_max = [jnp.max(t, axis=0, keepdims=True) for t in mt]

        lt = [None] * n_maps
        acc = [None] * n_maps
        for j in range(nk):
            vt = vt_scr[:, j * bk:(j + 1) * bk]
            for m in range(n_maps):
                p = jnp.exp2(s_scr[m, j] - q_max[m])
                lt[m] = merge(lt[m], fold(p, jnp.sum), jnp.add)
                acc[m] = merge(acc[m], _dot(vt, p.astype(BF16)), jnp.add)

        lam = (jnp.exp(jnp.sum(lq1_ref[...] * lk1_ref[...], axis=-1, keepdims=True))
               - jnp.exp(jnp.sum(lq2_ref[...] * lk2_ref[...], axis=-1, keepdims=True)) + LAM_INIT)
        l1 = jnp.sum(lt[0], axis=0, keepdims=True)
        l2 = jnp.sum(lt[1], axis=0, keepdims=True)
        o = acc[0] / l1 - lam * (acc[1] / l2)
        o = o * lax.rsqrt(jnp.mean(o * o, axis=0, keepdims=True) + SUBLN_EPS) * sw_ref[...]
        o_ref[...] = (o * (1.0 - LAM_INIT)).T.astype(BF16)

    for nk in range(1, k_ref.shape[0] // bk + 1):
        pl.when(qi == nk - 1)(lambda nk=nk: attend(nk))


def _attn(q, k, v, lam_q1, lam_k1, lam_q2, lam_k2, subln_w, bsz, seq):
    nq = seq // ATT_BQ
    qspec = pl.BlockSpec((ATT_BQ, LANES), lambda b, h, i: (b * nq + i, h))
    kvspec = pl.BlockSpec((seq, LANES), lambda b, h, i: (b, h))
    full = lambda a: pl.BlockSpec(a.shape, lambda b, h, i: (0, 0))
    lams = [a.reshape(1, -1) for a in (lam_q1, lam_k1, lam_q2, lam_k2)]
    sw = subln_w.reshape(-1, 1)
    return pl.pallas_call(
        _attn_kernel,
        grid=(bsz, ATT_HEADS, nq),
        in_specs=[qspec, kvspec, kvspec] + [full(a) for a in lams] + [full(sw)],
        out_specs=qspec,
        out_shape=jax.ShapeDtypeStruct((bsz * seq, ATT_WIDTH), BF16),
        scratch_shapes=[pltpu.VMEM((2, seq // ATT_BK, ATT_BK, ATT_BQ), F32),
                        pltpu.VMEM((LANES, seq), BF16)],
        compiler_params=_cparams(("arbitrary", "arbitrary", "arbitrary")),
        name="attn",
    )(q, k, v, *lams, sw)


def _mixer(x2, g_mix, w_in, conv_w, conv_b, dt_bias, a_log, d_skip, ssd_norm_w,
           lam_q1, lam_k1, lam_q2, lam_k2, subln_w, bsz, seq):
    z, xbc, dtr, q, k, v = _in_proj(x2, g_mix.reshape(1, -1), w_in)
    y_ssd = _ssd(xbc, dtr, z, conv_w, conv_b, dt_bias, a_log, d_skip, ssd_norm_w, bsz, seq)
    y_att = _attn(q, k, v, lam_q1, lam_k1, lam_q2, lam_k2, subln_w, bsz, seq)
    return y_ssd, y_att


def _out_proj_kernel(x_ref, ys_ref, ya_ref, wo_ref, g_ref, wr2_ref, br_ref,
                     x1_ref, xn_ref, route_ref, gate_ref, cnt_ref):
    tm = TM_ROUTE
    x1 = x_ref[...] + _dot(jnp.concatenate([ys_ref[...], ya_ref[...]], axis=-1), wo_ref[...])
    x1_ref[...] = x1
    xn_all = _rms(x1, g_ref[...], NORM_EPS)
    xn_ref[...] = xn_all.astype(BF16)

    n_tok = OUT_SUB * tm
    xh, xm, _ = _split3(xn_all)
    lg2 = _dot(xh, wr2_ref[...]) + _dot(xm, wr2_ref[...])
    logits = (lg2[:, :LANES] + lg2[:, LANES:]).T[:N_EXPERTS, :] + br_ref[...]

    eidx = lax.broadcasted_iota(jnp.int32, (N_EXPERTS, n_tok), 0).astype(F32)
    work = logits
    vals, idxs, hots = [], [], []
    for _ in range(TOP_K):
        m = jnp.max(work, axis=0, keepdims=True)
        idx = jnp.min(jnp.where(work == m, eidx, float(N_EXPERTS)), axis=0, keepdims=True)
        hot = eidx == idx
        vals.append(m)
        idxs.append(idx.astype(jnp.int32))
        hots.append(hot)
        work = jnp.where(hot, -jnp.inf, work)
    exps = [jnp.exp(v - vals[0]) for v in vals]
    denom = exps[0] + exps[1] + exps[2] + exps[3]
    gates = [e / denom for e in exps]

    cnt = jnp.zeros((N_EXPERTS, n_tok), F32)
    for hot in hots:
        cnt = cnt + jnp.where(hot, 1.0, 0.0)
    cnt_b = cnt.astype(BF16)
    r = lax.broadcasted_iota(jnp.int32, (tm, tm), 0)
    c = lax.broadcasted_iota(jnp.int32, (tm, tm), 1)
    earlier_tok = jnp.where(r < c, 1.0, 0.0).astype(BF16)
    er = lax.broadcasted_iota(jnp.int32, (N_EXPERTS, N_EXPERTS), 0)
    ec = lax.broadcasted_iota(jnp.int32, (N_EXPERTS, N_EXPERTS), 1)
    lower_exp = jnp.where(ec < er, 1.0, 0.0).astype(BF16)
    below = _dot(lower_exp, cnt_b)
    cnt_pad = jnp.concatenate([cnt_b, jnp.zeros((LANES - N_EXPERTS, n_tok), BF16)], axis=0)
    pos = []
    for sub in range(OUT_SUB):
        cols = slice(sub * tm, (sub + 1) * tm)
        pos.append(_dot(cnt_b[:, cols], earlier_tok) + jnp.sum(below[:, cols], axis=1, keepdims=True))
        cnt_ref[sub] = _dot_nt(jnp.ones((SUBLANES, tm), BF16), cnt_pad[:, cols])
    pos = jnp.concatenate(pos, axis=1)
    lps = [jnp.sum(jnp.where(hot, pos, 0.0), axis=0, keepdims=True).astype(jnp.int32) for hot in hots]

    route_ref[...] = jnp.concatenate(idxs + lps, axis=0)
    gate_ref[...] = jnp.concatenate(gates + [jnp.zeros((SUBLANES - TOP_K, n_tok), F32)], axis=0)


def _out_proj(x2, y_ssd, y_att, w_out, g_ffn, w_router, b_router):
    t = x2.shape[0]
    tm = OUT_SUB * TM_ROUTE
    nt = t // TM_ROUTE
    wo = w_out.astype(BF16)
    wrh, wrm, _ = _split3(jnp.pad(w_router, ((0, 0), (0, LANES - N_EXPERTS))))
    wr2 = jnp.concatenate([wrh, wrm], axis=1)
    br = b_router.reshape(N_EXPERTS, 1)
    row = lambda n: pl.BlockSpec((tm, n), lambda i: (i, 0))
    col = pl.BlockSpec((SUBLANES, tm), lambda i: (0, i))
    full = lambda a: pl.BlockSpec(a.shape, lambda i: (0, 0))
    args = (x2, y_ssd, y_att, wo, g_ffn.reshape(1, -1), wr2, br)
    return pl.pallas_call(
        _out_proj_kernel,
        grid=(t // tm,),
        in_specs=[row(D_MODEL), row(SSD_WIDTH), row(ATT_WIDTH)] + [full(a) for a in args[3:]],
        out_specs=[row(D_MODEL), row(D_MODEL), col, col,
                   pl.BlockSpec((OUT_SUB, SUBLANES, LANES), lambda i: (i, 0, 0))],
        out_shape=[
            jax.ShapeDtypeStruct((t, D_MODEL), F32),
            jax.ShapeDtypeStruct((t, D_MODEL), BF16),
            jax.ShapeDtypeStruct((SUBLANES, t), jnp.int32),
            jax.ShapeDtypeStruct((SUBLANES, t), F32),
            jax.ShapeDtypeStruct((nt, SUBLANES, LANES), F32),
        ],
        compiler_params=_cparams(("arbitrary",)),
        name="out_proj",
    )(*args)


def _copy_run(n, start_copy):
    @pl.when(n > 0)
    def _():
        start_copy(n)


def _select_by_position(positions, values, n_pos):
    tm = positions[0].shape[1]
    r = lax.broadcasted_iota(jnp.int32, (n_pos, tm), 0)
    out = jnp.zeros((n_pos, tm), F32)
    for k in reversed(range(TOP_K)):
        out = jnp.where(r == positions[k], values[k], out)
    return out


def _dispatch_kernel(cnt_ref, ls_ref, base_ref, zoff_ref, xn_ref, route_ref, xs_hbm, xloc, zeros_vmem, sem_z, sems):
    tm = TM_ROUTE
    n_pos = TOP_K * tm
    i = pl.program_id(0)
    n_steps = pl.num_programs(0)
    slot = lax.rem(i, 2)

    def zero_fill(op):
        def pad_rows(e, _):
            n = zoff_ref[N_EXPERTS + 1 + e]

            @pl.when(n > 0)
            def _():
                op(pltpu.make_async_copy(zeros_vmem.at[_row_slice(0, n)],
                                         xs_hbm.at[_row_slice(zoff_ref[e], n)], sem_z))
            return 0

        def unused_block(b, _):
            op(pltpu.make_async_copy(zeros_vmem, xs_hbm.at[_row_slice(b * ROW_BLK, ROW_BLK)], sem_z))
            return 0

        lax.fori_loop(0, N_EXPERTS, pad_rows, 0)
        lax.fori_loop(zoff_ref[N_EXPERTS], xs_hbm.shape[0] // (ROW_BLK * ROW_SUB), unused_block, 0)

    @pl.when(i == 0)
    def _():
        zeros_vmem[...] = jnp.zeros_like(zeros_vmem)
        zero_fill(lambda cp: cp.start())

    positions = [route_ref[TOP_K + k:TOP_K + k + 1, :] for k in range(TOP_K)]
    sel = _select_by_position(positions, [1.0] * TOP_K, n_pos).astype(BF16)
    rows = _dot(sel, xn_ref[...])

    def slot_wait(s):
        pltpu.make_async_copy(xloc.at[s], xs_hbm.at[_row_slice(0, n_pos)], sems.at[s]).wait()

    @pl.when(i >= 2)
    def _():
        slot_wait(slot)

    _store_rows(xloc.at[slot], rows)

    def per_expert(e, _):
        idx = i * N_EXPERTS + e
        src0 = ls_ref[idx]
        dst0 = base_ref[idx]

        def start_copy(n):
            pltpu.make_async_copy(xloc.at[slot, _row_slice(src0, n)],
                                  xs_hbm.at[_row_slice(dst0, n)], sems.at[slot]).start()

        _copy_run(cnt_ref[idx], start_copy)
        return 0

    lax.fori_loop(0, N_EXPERTS, per_expert, 0)

    @pl.when(i == n_steps - 1)
    def _():
        slot_wait(slot)

        @pl.when(n_steps > 1)
        def _():
            slot_wait(1 - slot)

        zero_fill(lambda cp: cp.wait())


def _dispatch(xn, route, tables, n_rows):
    t = xn.shape[0]
    tm = TM_ROUTE
    cnt_tbl, ls_tbl, base_tbl, zoff = tables
    return pl.pallas_call(
        _dispatch_kernel,
        grid_spec=pltpu.PrefetchScalarGridSpec(
            num_scalar_prefetch=4,
            grid=(t // tm,),
            in_specs=[pl.BlockSpec((tm, D_MODEL), lambda i, *_: (i, 0)),
                      pl.BlockSpec((SUBLANES, tm), lambda i, *_: (0, i))],
            out_specs=pl.BlockSpec(memory_space=pl.ANY),
            scratch_shapes=[
                pltpu.VMEM((2, TOP_K * tm * ROW_SUB, LANES), F32),
                pltpu.VMEM((ROW_BLK * ROW_SUB, LANES), F32),
                pltpu.SemaphoreType.DMA,
                pltpu.SemaphoreType.DMA((2,)),
            ],
        ),
        out_shape=jax.ShapeDtypeStruct(((n_rows + ROW_BLK) * ROW_SUB, LANES), F32),
        compiler_params=_cparams(("arbitrary",)),
        name="dispatch",
    )(cnt_tbl, ls_tbl, base_tbl, zoff, xn, route)


def _experts_kernel(blk_e_ref, nvalid_ref, first_ref, wslot_ref, enext_ref, pieces_ref, xs_ref, wup_hbm, wdn_hbm,
                    bg_ref, bu_ref, bd_ref, ys_ref, wup_buf, wdn_buf, wg_s, wu_s, wd_s, sems):
    i = pl.program_id(0)
    slot = wslot_ref[i]

    def weight_copies(e, s):
        return (pltpu.make_async_copy(wup_hbm.at[e], wup_buf.at[s], sems.at[0, s]),
                pltpu.make_async_copy(wdn_hbm.at[e], wdn_buf.at[s], sems.at[1, s]))

    @pl.when(i == 0)
    def _():
        for cp in weight_copies(blk_e_ref[0], slot):
            cp.start()

    @pl.when(first_ref[i] != 0)
    def _():
        for cp in weight_copies(blk_e_ref[i], slot):
            cp.wait()

        @pl.when(enext_ref[i] >= 0)
        def _():
            for cp in weight_copies(enext_ref[i], 1 - slot):
                cp.start()

        src = lax.broadcasted_iota(jnp.int32, (DEINT, DEINT), 0)
        dst = lax.broadcasted_iota(jnp.int32, (DEINT, DEINT), 1)
        perm = jnp.where(src == jnp.where(dst < LANES, 2 * dst, 2 * (dst - LANES) + 1), 1.0, 0.0).astype(BF16)
        for g in range(2 * D_EXPERT // DEINT):
            sep = _dot(wup_buf[slot, :, g * DEINT:(g + 1) * DEINT].astype(BF16), perm)
            wg_s[:, g * LANES:(g + 1) * LANES] = sep[:, :LANES].astype(BF16)
            wu_s[:, g * LANES:(g + 1) * LANES] = sep[:, LANES:].astype(BF16)
        wd_s[...] = wdn_buf[slot].astype(BF16)

    def ffn(n_rows):
        used = n_rows * ROW_SUB
        if n_rows:
            xb = _load_rows(xs_ref.at[pl.ds(0, used)]).astype(BF16)
            gate = jnp.minimum(_dot(xb, wg_s[...]) + bg_ref[0], SWIGLU_LIMIT)
            up = jnp.clip(_dot(xb, wu_s[...]) + bu_ref[0], -SWIGLU_LIMIT, SWIGLU_LIMIT)
            act = (up + 1.0) * gate * jax.nn.sigmoid(SWIGLU_ALPHA * gate)
            _store_rows(ys_ref.at[pl.ds(0, used)], _dot(act.astype(BF16), wd_s[...]) + bd_ref[0])
        if n_rows < ROW_BLK:
            ys_ref[used:, :] = jnp.zeros((ROW_BLK * ROW_SUB - used, LANES), F32)

    for pieces in range(ROW_BLK // ROW_PIECE + 1):
        pl.when(pieces_ref[i] == pieces)(lambda pieces=pieces: ffn(pieces * ROW_PIECE))


def _experts(xs, plan, w_up, b_up, w_down, b_down, n_rows):
    nb = n_rows // ROW_BLK
    bg = b_up[:, 0::2].reshape(N_EXPERTS, 1, D_EXPERT)
    bu = b_up[:, 1::2].reshape(N_EXPERTS, 1, D_EXPERT)
    bd = b_down.reshape(N_EXPERTS, 1, D_MODEL)
    blk_e, nvalid, first, wslot, enext, pieces = plan
    src = lambda i, be, nv, *_: (jnp.minimum(i, nv[0] - 1), 0)
    bspec = lambda m: pl.BlockSpec((1, 1, m), lambda i, be, *_: (be[i], 0, 0))
    anyspec = pl.BlockSpec(memory_space=pl.ANY)
    return pl.pallas_call(
        _experts_kernel,
        grid_spec=pltpu.PrefetchScalarGridSpec(
            num_scalar_prefetch=6,
            grid=(nb,),
            in_specs=[_rows_spec(ROW_BLK, src), anyspec, anyspec,
                      bspec(D_EXPERT), bspec(D_EXPERT), bspec(D_MODEL)],
            out_specs=_rows_spec(ROW_BLK, lambda i, *_: (i, 0)),
            scratch_shapes=[
                pltpu.VMEM((2, D_MODEL, 2 * D_EXPERT), F32),
                pltpu.VMEM((2, D_EXPERT, D_MODEL), F32),
                pltpu.VMEM((D_MODEL, D_EXPERT), BF16),
                pltpu.VMEM((D_MODEL, D_EXPERT), BF16),
                pltpu.VMEM((D_EXPERT, D_MODEL), BF16),
                pltpu.SemaphoreType.DMA((2, 2)),
            ],
        ),
        out_shape=jax.ShapeDtypeStruct((n_rows * ROW_SUB, LANES), F32),
        compiler_params=_cparams(("arbitrary",)),
        name="experts",
    )(blk_e, nvalid, first, wslot, enext, pieces, xs, w_up, w_down, bg, bu, bd)


def _combine_kernel(cnt_ref, ls_ref, base_ref, ys_hbm, route_ref, gate_ref, x1_ref, p_ref, gp_ref, wpg_ref,
                    wpp_ref, gf_ref, o_ref, yloc, sems):
    tm = TM_ROUTE
    n_pos = TOP_K * tm
    i = pl.program_id(0)
    n_steps = pl.num_programs(0)
    slot = lax.rem(i, 2)

    def gather_step(step, s):
        for u in range(COMB_SUB):
            def per_expert(e, _, u=u):
                idx = (step * COMB_SUB + u) * N_EXPERTS + e
                src0 = base_ref[idx]
                dst0 = ls_ref[idx]

                def start_copy(n):
                    pltpu.make_async_copy(ys_hbm.at[_row_slice(src0, n)],
                                          yloc.at[s, u, _row_slice(dst0, n)], sems.at[s]).start()

                _copy_run(cnt_ref[idx], start_copy)
                return 0

            lax.fori_loop(0, N_EXPERTS, per_expert, 0)

    @pl.when(i == 0)
    def _():
        gather_step(0, 0)

    @pl.when(i + 1 < n_steps)
    def _():
        gather_step(i + 1, 1 - slot)

    sels, g_poss = [], []
    for u in range(COMB_SUB):
        cols = slice(u * tm, (u + 1) * tm)
        positions = [route_ref[TOP_K + k:TOP_K + k + 1, cols] for k in range(TOP_K)]
        sels.append(_select_by_position(positions, [1.0] * TOP_K, n_pos).astype(BF16))
        gsel = _select_by_position(positions, [gate_ref[k:k + 1, cols] for k in range(TOP_K)], n_pos)
        g_poss.append(jnp.sum(gsel, axis=1, keepdims=True))
    pp = _dot(p_ref[...].astype(BF16), wpp_ref[...])

    for u in range(COMB_SUB):
        pltpu.make_async_copy(ys_hbm.at[_row_slice(0, n_pos)], yloc.at[slot, u], sems.at[slot]).wait()
    moe = []
    for u in range(COMB_SUB):
        y_gated = (_load_rows(yloc.at[slot, u]) * g_poss[u]).astype(BF16)
        moe.append(_dot_tn(sels[u], y_gated))
    x2 = x1_ref[...] + jnp.concatenate(moe, axis=0)
    xn = _rms(x2, gp_ref[...], NORM_EPS).astype(BF16)
    x3 = x2 + pp * jax.nn.sigmoid(_dot(xn, wpg_ref[...]))
    o_ref[...] = _rms(x3, gf_ref[...], NORM_EPS)


def _combine(ys, route, gate_t, x1, p2, tables, g_ple, w_ple_gate, w_ple_proj, g_final):
    t = x1.shape[0]
    tm = COMB_SUB * TM_ROUTE
    cnt_tbl, ls_tbl, base_tbl, _ = tables
    row = lambda n: pl.BlockSpec((tm, n), lambda i, *_: (i, 0))
    col = pl.BlockSpec((SUBLANES, tm), lambda i, *_: (0, i))
    full = lambda a: pl.BlockSpec(a.shape, lambda i, *_: (0, 0))
    consts = (g_ple.reshape(1, -1), w_ple_gate.astype(BF16), w_ple_proj.astype(BF16), g_final.reshape(1, -1))
    return pl.pallas_call(
        _combine_kernel,
        grid_spec=pltpu.PrefetchScalarGridSpec(
            num_scalar_prefetch=3,
            grid=(t // tm,),
            in_specs=[pl.BlockSpec(memory_space=pl.ANY), col, col, row(D_MODEL), row(PLE_DIM)]
                     + [full(a) for a in consts],
            out_specs=row(D_MODEL),
            scratch_shapes=[
                pltpu.VMEM((2, COMB_SUB, TOP_K * TM_ROUTE * ROW_SUB, LANES), F32),
                pltpu.SemaphoreType.DMA((2,)),
            ],
        ),
        out_shape=jax.ShapeDtypeStruct((t, D_MODEL), F32),
        compiler_params=_cparams(("arbitrary",)),
        name="combine",
    )(cnt_tbl, ls_tbl, base_tbl, ys, route, gate_t, x1, p2, *consts)


def _routing_tables(cnt, n_blocks):
    tile_cnt = cnt[:, 0, :N_EXPERTS].astype(jnp.int32)
    counts = jnp.sum(tile_cnt, axis=0)
    padded = (counts + ROW_BLK - 1) // ROW_BLK * ROW_BLK
    pend = jnp.cumsum(padded)
    pstart = pend - padded
    base = pstart[None, :] + jnp.cumsum(tile_cnt, axis=0) - tile_cnt
    lstart = jnp.cumsum(tile_cnt, axis=1) - tile_cnt
    nvalid = pend[-1:] // ROW_BLK
    zoff = jnp.concatenate([pstart + counts, nvalid, padded - counts])
    blk_start = jnp.minimum(jnp.arange(n_blocks, dtype=jnp.int32) * ROW_BLK, pend[-1] - 1)
    blk_e = jnp.minimum(jnp.sum((pend[None, :] <= blk_start[:, None]).astype(jnp.int32), axis=1), N_EXPERTS - 1)
    first = jnp.concatenate([jnp.ones((1,), bool), blk_e[1:] != blk_e[:-1]])
    wslot = (jnp.cumsum(first.astype(jnp.int32)) - 1) % 2
    eids = jnp.arange(N_EXPERTS, dtype=jnp.int32)
    later_nonempty = (eids[None, :] > eids[:, None]) & (padded[None, :] > 0)
    next_e = jnp.min(jnp.where(later_nonempty, eids[None, :], N_EXPERTS), axis=1)
    next_e = jnp.where(next_e == N_EXPERTS, -1, next_e)
    blk_hot = blk_e[:, None] == eids[None, :]
    per_block = lambda v: jnp.sum(jnp.where(blk_hot, v[None, :], 0), axis=1)
    blk_ids = jnp.arange(n_blocks, dtype=jnp.int32)
    real_rows = jnp.clip(per_block(pstart + counts) - blk_ids * ROW_BLK, 0, ROW_BLK) * (blk_ids < nvalid[0])
    pieces = (real_rows + ROW_PIECE - 1) // ROW_PIECE
    i32 = lambda a: a.reshape(-1).astype(jnp.int32)
    plan = (i32(blk_e), i32(nvalid), i32(first), i32(wslot), i32(per_block(next_e)), i32(pieces))
    return (i32(tile_cnt), i32(lstart), i32(base), i32(zoff)), plan


def kernel(x, p, g_mix, w_in, conv_w, conv_b, dt_bias, a_log, d_skip, ssd_norm_w, lam_q1, lam_k1, lam_q2, lam_k2, subln_w, w_out, g_ffn, w_router, b_router, w_up, b_up, w_down, b_down, g_ple, w_ple_gate, w_ple_proj, g_final):
    bsz, seq, d = x.shape
    t = bsz * seq
    x2 = x.reshape(t, d)
    y_ssd, y_att = _mixer(x2, g_mix[0], w_in[0], conv_w[0], conv_b[0], dt_bias[0], a_log[0], d_skip[0], ssd_norm_w[0],
                          lam_q1[0], lam_k1[0], lam_q2[0], lam_k2[0], subln_w[0], bsz, seq)
    x1, xn, route, gate_t, cnt = _out_proj(x2, y_ssd, y_att, w_out[0], g_ffn[0], w_router[0], b_router[0])

    n_rows = t * TOP_K + N_EXPERTS * ROW_BLK
    tables, plan = _routing_tables(cnt, n_rows // ROW_BLK)
    xs = _dispatch(xn, route, tables, n_rows)
    ys = _experts(xs, plan, w_up[0], b_up[0], w_down[0], b_down[0], n_rows)
    out = _combine(ys, route, gate_t, x1, p[0].reshape(t, PLE_DIM), tables, g_ple[0], w_ple_gate[0], w_ple_proj[0],
                   g_final)
    return out.reshape(bsz, seq, d)
```

```python
import math

import jax
import jax.numpy as jnp
from jax import lax
from jax.experimental import pallas as pl
from jax.experimental.pallas import tpu as pltpu

F32 = jnp.float32
BF16 = jnp.bfloat16

D_MODEL = 1024
PLE_DIM = 256
SSD_WIDTH = 512
ATT_WIDTH = 512
SSD_HEAD_DIM = 64
SSD_HEADS = 8
SSD_GROUPS = 2
SSD_STATE = 128
SSD_CONV = 4
SSD_CHUNK = 128
SSD_CONV_CH = SSD_WIDTH + 2 * SSD_GROUPS * SSD_STATE
SSD_NORM_EPS = 1e-5
ATT_HEAD_DIM = 64
ATT_HEADS = 4
SUBLN_EPS = 1e-5
OFF_Z = 0
OFF_XBC = OFF_Z + SSD_WIDTH
OFF_DT = OFF_XBC + SSD_CONV_CH
OFF_Q = OFF_DT + SSD_HEADS
OFF_K = OFF_Q + ATT_WIDTH
OFF_V = OFF_K + ATT_WIDTH
IN_PROJ = OFF_V + ATT_WIDTH
N_EXPERTS = 32
TOP_K = 4
D_EXPERT = 1024
SWIGLU_LIMIT = 7.0
SWIGLU_ALPHA = 1.702
NORM_EPS = 1e-6
LAM_INIT = 0.8 - 0.6 * math.exp(-0.3 * 0)

LANES = 128
SUBLANES = 8
VMEM_LIMIT_BYTES = 56 * 1024 * 1024
ROW_SUB = D_MODEL // LANES

TM_PROJ = 512
SSD_PAR = 2
ATT_BQ = 512
ATT_BK = 512
ROW_BLK = 512
ROW_PIECE = 256
TM_ROUTE = 256
OUT_SUB = 4
COMB_SUB = 2
DEINT = 2 * LANES


def _cparams(sem):
    return pltpu.CompilerParams(dimension_semantics=sem, vmem_limit_bytes=VMEM_LIMIT_BYTES)


def _rms(x, w, eps):
    return x * lax.rsqrt(jnp.mean(x * x, axis=-1, keepdims=True) + eps) * w


def _dot(a, b):
    return jnp.dot(a, b, preferred_element_type=F32)


def _dot_nt(a, b):
    return lax.dot_general(a, b, (((1,), (1,)), ((), ())), preferred_element_type=F32)


def _dot_tn(a, b):
    return lax.dot_general(a, b, (((0,), (0,)), ((), ())), preferred_element_type=F32)


def _store_rows(ref, val):
    n = val.shape[0]
    for s in range(ROW_SUB):
        ref[pl.ds(s, n, stride=ROW_SUB), :] = val[:, s * LANES:(s + 1) * LANES]


def _load_rows(ref):
    n = ref.shape[0] // ROW_SUB
    return jnp.concatenate([ref[pl.ds(s, n, stride=ROW_SUB), :] for s in range(ROW_SUB)], axis=-1)


def _rows_spec(n, index_map):
    return pl.BlockSpec((n * ROW_SUB, LANES), index_map)


def _row_slice(start, n):
    return pl.ds(pl.multiple_of(start * ROW_SUB, ROW_SUB), n * ROW_SUB)


def _split3(x):
    hi = x.astype(BF16)
    r1 = x - hi.astype(F32)
    mid = r1.astype(BF16)
    lo = (r1 - mid.astype(F32)).astype(BF16)
    return hi, mid, lo


def _in_proj_kernel(x_ref, g_ref, wz_ref, wxbc_ref, wdt_ref, wq_ref, wk_ref, wv_ref,
                    z_ref, xbc_ref, dt_ref, q_ref, k_ref, v_ref):
    h = _rms(x_ref[...], g_ref[...], NORM_EPS).astype(BF16)
    z_ref[...] = _dot(h, wz_ref[...])
    xbc_ref[...] = _dot(h, wxbc_ref[...])
    dt_ref[...] = _dot(h, wdt_ref[...])
    q_ref[...] = (_dot(h, wq_ref[...]) * (ATT_HEAD_DIM ** -0.5 * math.log2(math.e))).astype(BF16)
    k_ref[...] = _dot(h, wk_ref[...]).astype(BF16)
    v_ref[...] = _dot(h, wv_ref[...]).astype(BF16)


def _in_proj(x2, g_mix, w_in):
    t = x2.shape[0]
    wb = w_in.astype(BF16)
    wz = wb[:, OFF_Z:OFF_XBC]
    wxbc = wb[:, OFF_XBC:OFF_DT]
    wdt = jnp.pad(wb[:, OFF_DT:OFF_Q], ((0, 0), (0, LANES - SSD_HEADS)))
    wq = wb[:, OFF_Q:OFF_K]
    wk = wb[:, OFF_K:OFF_V]
    wv = wb[:, OFF_V:IN_PROJ]
    tm = TM_PROJ
    row = lambda n: pl.BlockSpec((tm, n), lambda i: (i, 0))
    full = lambda a: pl.BlockSpec(a.shape, lambda i: (0, 0))
    return pl.pallas_call(
        _in_proj_kernel,
        grid=(t // tm,),
        in_specs=[row(D_MODEL), full(g_mix), full(wz), full(wxbc), full(wdt), full(wq), full(wk), full(wv)],
        out_specs=[row(SSD_WIDTH), row(SSD_CONV_CH), row(LANES), row(ATT_WIDTH), row(ATT_WIDTH), row(ATT_WIDTH)],
        out_shape=[
            jax.ShapeDtypeStruct((t, SSD_WIDTH), F32),
            jax.ShapeDtypeStruct((t, SSD_CONV_CH), F32),
            jax.ShapeDtypeStruct((t, LANES), F32),
            jax.ShapeDtypeStruct((t, ATT_WIDTH), BF16),
            jax.ShapeDtypeStruct((t, ATT_WIDTH), BF16),
            jax.ShapeDtypeStruct((t, ATT_WIDTH), BF16),
        ],
        compiler_params=_cparams(("arbitrary",)),
        name="in_proj",
    )(x2, g_mix, wz, wxbc, wdt, wq, wk, wv)


def _ssd_kernel(xbc_ref, dtr_ref, z_ref, cw_ref, cb_ref, dtb_ref, alog_ref, dskip_ref, nw_ref,
                y_ref, xpad_ref, state_ref):
    for bb in range(SSD_PAR):
        _ssd_chunk(xbc_ref.at[bb], dtr_ref.at[bb], z_ref.at[bb], cw_ref, cb_ref, dtb_ref, alog_ref, dskip_ref,
                   nw_ref, y_ref.at[bb], xpad_ref.at[bb], state_ref.at[bb])


def _ssd_chunk(xbc_ref, dtr_ref, z_ref, cw_ref, cb_ref, dtb_ref, alog_ref, dskip_ref, nw_ref,
               y_ref, xpad_ref, state_ref):
    L = SSD_CHUNK
    c = pl.program_id(1)

    @pl.when(c == 0)
    def _():
        xpad_ref[0:SUBLANES, :] = jnp.zeros((SUBLANES, SSD_CONV_CH), F32)
        state_ref[...] = jnp.zeros_like(state_ref)

    @pl.when(c != 0)
    def _():
        xpad_ref[0:SUBLANES, :] = xpad_ref[L:L + SUBLANES, :]

    xpad_ref[SUBLANES:SUBLANES + L, :] = xbc_ref[...]

    conv = cb_ref[...]
    for j in range(SSD_CONV):
        off = SUBLANES - (SSD_CONV - 1) + j
        conv = conv + cw_ref[j:j + 1, :] * xpad_ref[off:off + L, :]
    act = conv * jax.nn.sigmoid(conv)
    xs = act[:, :SSD_WIDTH]
    bm = act[:, SSD_WIDTH:SSD_WIDTH + SSD_GROUPS * SSD_STATE].astype(BF16)
    cm = act[:, SSD_WIDTH + SSD_GROUPS * SSD_STATE:].astype(BF16)

    dt_in = dtr_ref[...] + dtb_ref[...]
    dt_all = jnp.maximum(dt_in, 0.0) + jnp.log1p(jnp.exp(-jnp.abs(dt_in)))
    adt = dt_all * (-jnp.exp(alog_ref[...]))

    ri = lax.broadcasted_iota(jnp.int32, (L, L), 0)
    ci = lax.broadcasted_iota(jnp.int32, (L, L), 1)
    causal = ci <= ri
    tril = jnp.where(causal, 1.0, 0.0).astype(BF16)
    hi, mid, lo = _split3(adt)
    acum_all = _dot(tril, hi) + _dot(tril, mid) + _dot(tril, lo)
    acum_t = acum_all.T
    a_last = acum_all[L - 1:L, :]
    decay_in_all = jnp.exp(a_last - acum_all)
    decay_out_all = jnp.exp(acum_all)
    chunk_decay_all = jnp.exp(a_last)
    acum = lambda h: acum_all[:, h:h + 1]
    dt = lambda h: dt_all[:, h:h + 1]
    decay_out = lambda h: decay_out_all[:, h:h + 1]
    decay_in = lambda h: decay_in_all[:, h:h + 1]
    chunk_decay = lambda h: chunk_decay_all[:, h:h + 1]

    lane = lax.broadcasted_iota(jnp.int32, (L, LANES), 1)
    lo_half = lane < SSD_HEAD_DIM

    def per_pair(col_a, col_b):
        return jnp.where(lo_half, col_a, col_b)

    ys = []
    for pair in range(SSD_HEADS // 2):
        g = pair // 2
        h0, h1 = 2 * pair, 2 * pair + 1
        cg = cm[:, g * SSD_STATE:(g + 1) * SSD_STATE]
        bg = bm[:, g * SSD_STATE:(g + 1) * SSD_STATE]
        cb = _dot_nt(cg, bg)
        x_pair = xs[:, pair * LANES:(pair + 1) * LANES]
        xdt = x_pair * per_pair(dt(h0), dt(h1))
        y_pair = jnp.zeros((L, LANES), F32)
        for hh, keep in ((h0, lo_half), (h1, jnp.logical_not(lo_half))):
            seg = acum(hh) - acum_t[hh:hh + 1, :]
            lmat = jnp.where(causal, jnp.exp(jnp.where(causal, seg, 0.0)), 0.0)
            m = (cb * lmat).astype(BF16)
            y_pair = y_pair + _dot(m, jnp.where(keep, xdt, 0.0).astype(BF16))
        s_prev = state_ref[pair]
        y_off = _dot(cg, s_prev.astype(BF16)) * per_pair(decay_out(h0), decay_out(h1))
        w_in = (xdt * per_pair(decay_in(h0), decay_in(h1))).astype(BF16)
        cd = jnp.where(lane[0:1, :] < SSD_HEAD_DIM, chunk_decay(h0), chunk_decay(h1))
        state_ref[pair] = s_prev * cd + _dot_tn(bg, w_in)
        ys.append(y_pair + y_off + dskip_ref[:, pair * LANES:(pair + 1) * LANES] * x_pair)

    y = jnp.concatenate(ys, axis=-1)
    zz = z_ref[...]
    y = y * (zz * jax.nn.sigmoid(zz))
    gw = SSD_WIDTH // SSD_GROUPS
    outs = []
    for g in range(SSD_GROUPS):
        yg = y[:, g * gw:(g + 1) * gw]
        outs.append(yg * lax.rsqrt(jnp.mean(yg * yg, axis=-1, keepdims=True) + SSD_NORM_EPS))
    y_ref[...] = (jnp.concatenate(outs, axis=-1) * nw_ref[...]).astype(BF16)


def _ssd(xbc, dtr, z, conv_w, conv_b, dt_bias, a_log, d_skip, norm_w, bsz, seq):
    nc = seq // SSD_CHUNK
    pad_h = lambda v: jnp.pad(v.reshape(1, SSD_HEADS), ((0, 0), (0, LANES - SSD_HEADS)))
    dskip_lanes = jnp.repeat(d_skip, SSD_HEAD_DIM).reshape(1, SSD_WIDTH)
    seqs = lambda a: a.reshape(bsz, seq, a.shape[-1])
    row = lambda n: pl.BlockSpec((SSD_PAR, SSD_CHUNK, n), lambda b, c: (b, c, 0))
    full = lambda a: pl.BlockSpec(a.shape, lambda b, c: (0, 0))
    args = (seqs(xbc), seqs(dtr), seqs(z), conv_w, conv_b.reshape(1, -1), pad_h(dt_bias), pad_h(a_log), dskip_lanes,
            norm_w.reshape(1, -1))
    y = pl.pallas_call(
        _ssd_kernel,
        grid=(bsz // SSD_PAR, nc),
        in_specs=[row(SSD_CONV_CH), row(LANES), row(SSD_WIDTH)] + [full(a) for a in args[3:]],
        out_specs=row(SSD_WIDTH),
        out_shape=jax.ShapeDtypeStruct((bsz, seq, SSD_WIDTH), BF16),
        scratch_shapes=[
            pltpu.VMEM((SSD_PAR, SSD_CHUNK + 2 * SUBLANES, SSD_CONV_CH), F32),
            pltpu.VMEM((SSD_PAR, SSD_HEADS // 2, SSD_STATE, LANES), F32),
        ],
        compiler_params=_cparams(("arbitrary", "arbitrary")),
        name="ssd",
    )(*args)
    return y.reshape(bsz * seq, SSD_WIDTH)


def _attn_kernel(q_ref, k_ref, v_ref, lq1_ref, lk1_ref, lq2_ref, lk2_ref, sw_ref, o_ref, s_scr, vt_scr):
    bq, bk = ATT_BQ, ATT_BK
    qi = pl.program_id(2)
    n_maps = 2

    @pl.when(qi == 0)
    def _():
        vt_scr[...] = v_ref[...].astype(F32).T.astype(BF16)

    def fold(t, reduce):
        return reduce(t.reshape(bk // SUBLANES, SUBLANES, bq), axis=0)

    def merge(old, new, op):
        return new if old is None else op(old, new)

    def attend(nk):
        q = q_ref[...]
        lane = lax.broadcasted_iota(jnp.int32, (bq, LANES), 1)
        zero = jnp.zeros_like(q)
        q_maps = (jnp.where(lane < ATT_HEAD_DIM, q, zero), jnp.where(lane >= ATT_HEAD_DIM, q, zero))
        key = lax.broadcasted_iota(jnp.int32, (bk, bq), 0)
        qry = lax.broadcasted_iota(jnp.int32, (bk, bq), 1)
        causal = key <= qry

        mt = [None] * n_maps
        for j in range(nk):
            kb = k_ref[j * bk:(j + 1) * bk, :]
            for m in range(n_maps):
                s = _dot_nt(kb, q_maps[m])
                if j == nk - 1:
                    s = jnp.where(causal, s, -jnp.inf)
                s_scr[m, j] = s
                mt[m] = merge(mt[m], fold(s, jnp.max), jnp.maximum)
        q_max = [jnp.max(t, axis=0, keepdims=True) for t in mt]

        lt = [None] * n_maps
        acc = [None] * n_maps
        for j in range(nk):
            vt = vt_scr[:, j * bk:(j + 1) * bk]
            for m in range(n_maps):
                p = jnp.exp2(s_scr[m, j] - q_max[m])
                lt[m] = merge(lt[m], fold(p, jnp.sum), jnp.add)
                acc[m] = merge(acc[m], _dot(vt, p.astype(BF16)), jnp.add)

        lam = (jnp.exp(jnp.sum(lq1_ref[...] * lk1_ref[...], axis=-1, keepdims=True))
               - jnp.exp(jnp.sum(lq2_ref[...] * lk2_ref[...], axis=-1, keepdims=True)) + LAM_INIT)
        l1 = jnp.sum(lt[0], axis=0, keepdims=True)
        l2 = jnp.sum(lt[1], axis=0, keepdims=True)
        o = acc[0] / l1 - lam * (acc[1] / l2)
        o = o * lax.rsqrt(jnp.mean(o * o, axis=0, keepdims=True) + SUBLN_EPS) * sw_ref[...]
        o_ref[...] = (o * (1.0 - LAM_INIT)).T.astype(BF16)

    for nk in range(1, k_ref.shape[0] // bk + 1):
        pl.when(qi == nk - 1)(lambda nk=nk: attend(nk))


def _attn(q, k, v, lam_q1, lam_k1, lam_q2, lam_k2, subln_w, bsz, seq):
    nq = seq // ATT_BQ
    qspec = pl.BlockSpec((ATT_BQ, LANES), lambda b, h, i: (b * nq + i, h))
    kvspec = pl.BlockSpec((seq, LANES), lambda b, h, i: (b, h))
    full = lambda a: pl.BlockSpec(a.shape, lambda b, h, i: (0, 0))
    lams = [a.reshape(1, -1) for a in (lam_q1, lam_k1, lam_q2, lam_k2)]
    sw = subln_w.reshape(-1, 1)
    return pl.pallas_call(
        _attn_kernel,
        grid=(bsz, ATT_HEADS, nq),
        in_specs=[qspec, kvspec, kvspec] + [full(a) for a in lams] + [full(sw)],
        out_specs=qspec,
        out_shape=jax.ShapeDtypeStruct((bsz * seq, ATT_WIDTH), BF16),
        scratch_shapes=[pltpu.VMEM((2, seq // ATT_BK, ATT_BK, ATT_BQ), F32),
                        pltpu.VMEM((LANES, seq), BF16)],
        compiler_params=_cparams(("arbitrary", "arbitrary", "arbitrary")),
        name="attn",
    )(q, k, v, *lams, sw)


def _mixer(x2, g_mix, w_in, conv_w, conv_b, dt_bias, a_log, d_skip, ssd_norm_w,
           lam_q1, lam_k1, lam_q2, lam_k2, subln_w, bsz, seq):
    z, xbc, dtr, q, k, v = _in_proj(x2, g_mix.reshape(1, -1), w_in)
    y_ssd = _ssd(xbc, dtr, z, conv_w, conv_b, dt_bias, a_log, d_skip, ssd_norm_w, bsz, seq)
    y_att = _attn(q, k, v, lam_q1, lam_k1, lam_q2, lam_k2, subln_w, bsz, seq)
    return y_ssd, y_att


def _out_proj_kernel(x_ref, ys_ref, ya_ref, wo_ref, g_ref, wr2_ref, br_ref,
                     x1_ref, xn_ref, route_ref, gate_ref, cnt_ref):
    tm = TM_ROUTE
    x1 = x_ref[...] + _dot(jnp.concatenate([ys_ref[...], ya_ref[...]], axis=-1), wo_ref[...])
    x1_ref[...] = x1
    xn_all = _rms(x1, g_ref[...], NORM_EPS)
    xn_ref[...] = xn_all.astype(BF16)

    n_tok = OUT_SUB * tm
    xh, xm, _ = _split3(xn_all)
    lg2 = _dot(xh, wr2_ref[...]) + _dot(xm, wr2_ref[...])
    logits = (lg2[:, :LANES] + lg2[:, LANES:]).T[:N_EXPERTS, :] + br_ref[...]

    eidx = lax.broadcasted_iota(jnp.int32, (N_EXPERTS, n_tok), 0).astype(F32)
    work = logits
    vals, idxs, hots = [], [], []
    for _ in range(TOP_K):
        m = jnp.max(work, axis=0, keepdims=True)
        idx = jnp.min(jnp.where(work == m, eidx, float(N_EXPERTS)), axis=0, keepdims=True)
        hot = eidx == idx
        vals.append(m)
        idxs.append(idx.astype(jnp.int32))
        hots.append(hot)
        work = jnp.where(hot, -jnp.inf, work)
    exps = [jnp.exp(v - vals[0]) for v in vals]
    denom = exps[0] + exps[1] + exps[2] + exps[3]
    gates = [e / denom for e in exps]

    cnt = jnp.zeros((N_EXPERTS, n_tok), F32)
    for hot in hots:
        cnt = cnt + jnp.where(hot, 1.0, 0.0)
    cnt_b = cnt.astype(BF16)
    r = lax.broadcasted_iota(jnp.int32, (tm, tm), 0)
    c = lax.broadcasted_iota(jnp.int32, (tm, tm), 1)
    earlier_tok = jnp.where(r < c, 1.0, 0.0).astype(BF16)
    er = lax.broadcasted_iota(jnp.int32, (N_EXPERTS, N_EXPERTS), 0)
    ec = lax.broadcasted_iota(jnp.int32, (N_EXPERTS, N_EXPERTS), 1)
    lower_exp = jnp.where(ec < er, 1.0, 0.0).astype(BF16)
    below = _dot(lower_exp, cnt_b)
    cnt_pad = jnp.concatenate([cnt_b, jnp.zeros((LANES - N_EXPERTS, n_tok), BF16)], axis=0)
    pos = []
    for sub in range(OUT_SUB):
        cols = slice(sub * tm, (sub + 1) * tm)
        pos.append(_dot(cnt_b[:, cols], earlier_tok) + jnp.sum(below[:, cols], axis=1, keepdims=True))
        cnt_ref[sub] = _dot_nt(jnp.ones((SUBLANES, tm), BF16), cnt_pad[:, cols])
    pos = jnp.concatenate(pos, axis=1)
    lps = [jnp.sum(jnp.where(hot, pos, 0.0), axis=0, keepdims=True).astype(jnp.int32) for hot in hots]

    route_ref[...] = jnp.concatenate(idxs + lps, axis=0)
    gate_ref[...] = jnp.concatenate(gates + [jnp.zeros((SUBLANES - TOP_K, n_tok), F32)], axis=0)


def _out_proj(x2, y_ssd, y_att, w_out, g_ffn, w_router, b_router):
    t = x2.shape[0]
    tm = OUT_SUB * TM_ROUTE
    nt = t // TM_ROUTE
    wo = w_out.astype(BF16)
    wrh, wrm, _ = _split3(jnp.pad(w_router, ((0, 0), (0, LANES - N_EXPERTS))))
    wr2 = jnp.concatenate([wrh, wrm], axis=1)
    br = b_router.reshape(N_EXPERTS, 1)
    row = lambda n: pl.BlockSpec((tm, n), lambda i: (i, 0))
    col = pl.BlockSpec((SUBLANES, tm), lambda i: (0, i))
    full = lambda a: pl.BlockSpec(a.shape, lambda i: (0, 0))
    args = (x2, y_ssd, y_att, wo, g_ffn.reshape(1, -1), wr2, br)
    return pl.pallas_call(
        _out_proj_kernel,
        grid=(t // tm,),
        in_specs=[row(D_MODEL), row(SSD_WIDTH), row(ATT_WIDTH)] + [full(a) for a in args[3:]],
        out_specs=[row(D_MODEL), row(D_MODEL), col, col,
                   pl.BlockSpec((OUT_SUB, SUBLANES, LANES), lambda i: (i, 0, 0))],
        out_shape=[
            jax.ShapeDtypeStruct((t, D_MODEL), F32),
            jax.ShapeDtypeStruct((t, D_MODEL), BF16),
            jax.ShapeDtypeStruct((SUBLANES, t), jnp.int32),
            jax.ShapeDtypeStruct((SUBLANES, t), F32),
            jax.ShapeDtypeStruct((nt, SUBLANES, LANES), F32),
        ],
        compiler_params=_cparams(("arbitrary",)),
        name="out_proj",
    )(*args)


def _copy_run(n, start_copy):
    @pl.when(n > 0)
    def _():
        start_copy(n)


def _select_by_position(positions, values, n_pos):
    tm = positions[0].shape[1]
    r = lax.broadcasted_iota(jnp.int32, (n_pos, tm), 0)
    out = jnp.zeros((n_pos, tm), F32)
    for k in reversed(range(TOP_K)):
        out = jnp.where(r == positions[k], values[k], out)
    return out


def _dispatch_kernel(cnt_ref, ls_ref, base_ref, zoff_ref, xn_ref, route_ref, xs_hbm, xloc, zeros_vmem, sem_z, sems):
    tm = TM_ROUTE
    n_pos = TOP_K * tm
    i = pl.program_id(0)
    n_steps = pl.num_programs(0)
    slot = lax.rem(i, 2)

    def zero_fill(op):
        def pad_rows(e, _):
            n = zoff_ref[N_EXPERTS + 1 + e]

            @pl.when(n > 0)
            def _():
                op(pltpu.make_async_copy(zeros_vmem.at[_row_slice(0, n)],
                                         xs_hbm.at[_row_slice(zoff_ref[e], n)], sem_z))
            return 0

        def unused_block(b, _):
            op(pltpu.make_async_copy(zeros_vmem, xs_hbm.at[_row_slice(b * ROW_BLK, ROW_BLK)], sem_z))
            return 0

        lax.fori_loop(0, N_EXPERTS, pad_rows, 0)
        lax.fori_loop(zoff_ref[N_EXPERTS], xs_hbm.shape[0] // (ROW_BLK * ROW_SUB), unused_block, 0)

    @pl.when(i == 0)
    def _():
        zeros_vmem[...] = jnp.zeros_like(zeros_vmem)
        zero_fill(lambda cp: cp.start())

    positions = [route_ref[TOP_K + k:TOP_K + k + 1, :] for k in range(TOP_K)]
    sel = _select_by_position(positions, [1.0] * TOP_K, n_pos).astype(BF16)
    rows = _dot(sel, xn_ref[...])

    def slot_wait(s):
        pltpu.make_async_copy(xloc.at[s], xs_hbm.at[_row_slice(0, n_pos)], sems.at[s]).wait()

    @pl.when(i >= 2)
    def _():
        slot_wait(slot)

    _store_rows(xloc.at[slot], rows)

    def per_expert_pair(e2, _):
        for priority in range(2):
            idx = i * N_EXPERTS + 2 * e2 + priority
            src0 = ls_ref[idx]
            dst0 = base_ref[idx]

            def start_copy(n, src0=src0, dst0=dst0, priority=priority):
                pltpu.make_async_copy(xloc.at[slot, _row_slice(src0, n)],
                                      xs_hbm.at[_row_slice(dst0, n)], sems.at[slot]).start(priority=priority)

            _copy_run(cnt_ref[idx], start_copy)
        return 0

    lax.fori_loop(0, N_EXPERTS // 2, per_expert_pair, 0)

    @pl.when(i == n_steps - 1)
    def _():
        slot_wait(slot)

        @pl.when(n_steps > 1)
        def _():
            slot_wait(1 - slot)

        zero_fill(lambda cp: cp.wait())


def _dispatch(xn, route, tables, n_rows):
    t = xn.shape[0]
    tm = TM_ROUTE
    cnt_tbl, ls_tbl, base_tbl, zoff = tables
    return pl.pallas_call(
        _dispatch_kernel,
        grid_spec=pltpu.PrefetchScalarGridSpec(
            num_scalar_prefetch=4,
            grid=(t // tm,),
            in_specs=[pl.BlockSpec((tm, D_MODEL), lambda i, *_: (i, 0)),
                      pl.BlockSpec((SUBLANES, tm), lambda i, *_: (0, i))],
            out_specs=pl.BlockSpec(memory_space=pl.ANY),
            scratch_shapes=[
                pltpu.VMEM((2, TOP_K * tm * ROW_SUB, LANES), F32),
                pltpu.VMEM((ROW_BLK * ROW_SUB, LANES), F32),
                pltpu.SemaphoreType.DMA,
                pltpu.SemaphoreType.DMA((2,)),
            ],
        ),
        out_shape=jax.ShapeDtypeStruct(((n_rows + ROW_BLK) * ROW_SUB, LANES), F32),
        compiler_params=_cparams(("arbitrary",)),
        name="dispatch",
    )(cnt_tbl, ls_tbl, base_tbl, zoff, xn, route)


def _experts_kernel(blk_e_ref, nvalid_ref, first_ref, wslot_ref, enext_ref, pieces_ref, xs_ref, wup_hbm, wdn_hbm,
                    bg_ref, bu_ref, bd_ref, ys_ref, wup_buf, wdn_buf, wg_s, wu_s, wd_s, sems):
    i = pl.program_id(0)
    slot = wslot_ref[i]

    def weight_copies(e, s):
        return (pltpu.make_async_copy(wup_hbm.at[e], wup_buf.at[s], sems.at[0, s]),
                pltpu.make_async_copy(wdn_hbm.at[e], wdn_buf.at[s], sems.at[1, s]))

    @pl.when(i == 0)
    def _():
        for cp in weight_copies(blk_e_ref[0], slot):
            cp.start()

    @pl.when(first_ref[i] != 0)
    def _():
        for cp in weight_copies(blk_e_ref[i], slot):
            cp.wait()

        @pl.when(enext_ref[i] >= 0)
        def _():
            for cp in weight_copies(enext_ref[i], 1 - slot):
                cp.start()

        src = lax.broadcasted_iota(jnp.int32, (DEINT, DEINT), 0)
        dst = lax.broadcasted_iota(jnp.int32, (DEINT, DEINT), 1)
        perm = jnp.where(src == jnp.where(dst < LANES, 2 * dst, 2 * (dst - LANES) + 1), 1.0, 0.0).astype(BF16)
        for g in range(2 * D_EXPERT // DEINT):
            sep = _dot(wup_buf[slot, :, g * DEINT:(g + 1) * DEINT].astype(BF16), perm)
            wg_s[:, g * LANES:(g + 1) * LANES] = sep[:, :LANES].astype(BF16)
            wu_s[:, g * LANES:(g + 1) * LANES] = sep[:, LANES:].astype(BF16)
        wd_s[...] = wdn_buf[slot].astype(BF16)

    def ffn(n_rows):
        used = n_rows * ROW_SUB
        if n_rows:
            xb = _load_rows(xs_ref.at[pl.ds(0, used)]).astype(BF16)
            gate = jnp.minimum(_dot(xb, wg_s[...]) + bg_ref[0], SWIGLU_LIMIT)
            up = jnp.clip(_dot(xb, wu_s[...]) + bu_ref[0], -SWIGLU_LIMIT, SWIGLU_LIMIT)
            act = (up + 1.0) * gate * jax.nn.sigmoid(SWIGLU_ALPHA * gate)
            _store_rows(ys_ref.at[pl.ds(0, used)], _dot(act.astype(BF16), wd_s[...]) + bd_ref[0])
        if n_rows < ROW_BLK:
            ys_ref[used:, :] = jnp.zeros((ROW_BLK * ROW_SUB - used, LANES), F32)

    for pieces in range(ROW_BLK // ROW_PIECE + 1):
        pl.when(pieces_ref[i] == pieces)(lambda pieces=pieces: ffn(pieces * ROW_PIECE))


def _experts(xs, plan, w_up, b_up, w_down, b_down, n_rows):
    nb = n_rows // ROW_BLK
    bg = b_up[:, 0::2].reshape(N_EXPERTS, 1, D_EXPERT)
    bu = b_up[:, 1::2].reshape(N_EXPERTS, 1, D_EXPERT)
    bd = b_down.reshape(N_EXPERTS, 1, D_MODEL)
    blk_e, nvalid, first, wslot, enext, pieces = plan
    src = lambda i, be, nv, *_: (jnp.minimum(i, nv[0] - 1), 0)
    bspec = lambda m: pl.BlockSpec((1, 1, m), lambda i, be, *_: (be[i], 0, 0))
    anyspec = pl.BlockSpec(memory_space=pl.ANY)
    return pl.pallas_call(
        _experts_kernel,
        grid_spec=pltpu.PrefetchScalarGridSpec(
            num_scalar_prefetch=6,
            grid=(nb,),
            in_specs=[_rows_spec(ROW_BLK, src), anyspec, anyspec,
                      bspec(D_EXPERT), bspec(D_EXPERT), bspec(D_MODEL)],
            out_specs=_rows_spec(ROW_BLK, lambda i, *_: (i, 0)),
            scratch_shapes=[
                pltpu.VMEM((2, D_MODEL, 2 * D_EXPERT), F32),
                pltpu.VMEM((2, D_EXPERT, D_MODEL), F32),
                pltpu.VMEM((D_MODEL, D_EXPERT), BF16),
                pltpu.VMEM((D_MODEL, D_EXPERT), BF16),
                pltpu.VMEM((D_EXPERT, D_MODEL), BF16),
                pltpu.SemaphoreType.DMA((2, 2)),
            ],
        ),
        out_shape=jax.ShapeDtypeStruct((n_rows * ROW_SUB, LANES), F32),
        compiler_params=_cparams(("arbitrary",)),
        name="experts",
    )(blk_e, nvalid, first, wslot, enext, pieces, xs, w_up, w_down, bg, bu, bd)


def _combine_kernel(cnt_ref, ls_ref, base_ref, ys_hbm, route_ref, gate_ref, x1_ref, p_ref, gp_ref, wpg_ref,
                    wpp_ref, gf_ref, o_ref, yloc, sems):
    tm = TM_ROUTE
    n_pos = TOP_K * tm
    i = pl.program_id(0)
    n_steps = pl.num_programs(0)
    slot = lax.rem(i, 2)

    def gather_step(step, s):
        for u in range(COMB_SUB):
            def per_expert_pair(e2, _, u=u):
                for priority in range(2):
                    idx = (step * COMB_SUB + u) * N_EXPERTS + 2 * e2 + priority
                    src0 = base_ref[idx]
                    dst0 = ls_ref[idx]

                    def start_copy(n, src0=src0, dst0=dst0, priority=priority):
                        pltpu.make_async_copy(ys_hbm.at[_row_slice(src0, n)], yloc.at[s, u, _row_slice(dst0, n)],
                                              sems.at[s]).start(priority=priority)

                    _copy_run(cnt_ref[idx], start_copy)
                return 0

            lax.fori_loop(0, N_EXPERTS // 2, per_expert_pair, 0)

    @pl.when(i == 0)
    def _():
        gather_step(0, 0)

    @pl.when(i + 1 < n_steps)
    def _():
        gather_step(i + 1, 1 - slot)

    sels, g_poss = [], []
    for u in range(COMB_SUB):
        cols = slice(u * tm, (u + 1) * tm)
        positions = [route_ref[TOP_K + k:TOP_K + k + 1, cols] for k in range(TOP_K)]
        sels.append(_select_by_position(positions, [1.0] * TOP_K, n_pos).astype(BF16))
        gsel = _select_by_position(positions, [gate_ref[k:k + 1, cols] for k in range(TOP_K)], n_pos)
        g_poss.append(jnp.sum(gsel, axis=1, keepdims=True))
    pp = _dot(p_ref[...].astype(BF16), wpp_ref[...])

    for u in range(COMB_SUB):
        pltpu.make_async_copy(ys_hbm.at[_row_slice(0, n_pos)], yloc.at[slot, u], sems.at[slot]).wait()
    moe = []
    for u in range(COMB_SUB):
        y_gated = (_load_rows(yloc.at[slot, u]) * g_poss[u]).astype(BF16)
        moe.append(_dot_tn(sels[u], y_gated))
    x2 = x1_ref[...] + jnp.concatenate(moe, axis=0)
    xn = _rms(x2, gp_ref[...], NORM_EPS).astype(BF16)
    x3 = x2 + pp * jax.nn.sigmoid(_dot(xn, wpg_ref[...]))
    o_ref[...] = _rms(x3, gf_ref[...], NORM_EPS)


def _combine(ys, route, gate_t, x1, p2, tables, g_ple, w_ple_gate, w_ple_proj, g_final):
    t = x1.shape[0]
    tm = COMB_SUB * TM_ROUTE
    cnt_tbl, ls_tbl, base_tbl, _ = tables
    row = lambda n: pl.BlockSpec((tm, n), lambda i, *_: (i, 0))
    col = pl.BlockSpec((SUBLANES, tm), lambda i, *_: (0, i))
    full = lambda a: pl.BlockSpec(a.shape, lambda i, *_: (0, 0))
    consts = (g_ple.reshape(1, -1), w_ple_gate.astype(BF16), w_ple_proj.astype(BF16), g_final.reshape(1, -1))
    return pl.pallas_call(
        _combine_kernel,
        grid_spec=pltpu.PrefetchScalarGridSpec(
            num_scalar_prefetch=3,
            grid=(t // tm,),
            in_specs=[pl.BlockSpec(memory_space=pl.ANY), col, col, row(D_MODEL), row(PLE_DIM)]
                     + [full(a) for a in consts],
            out_specs=row(D_MODEL),
            scratch_shapes=[
                pltpu.VMEM((2, COMB_SUB, TOP_K * TM_ROUTE * ROW_SUB, LANES), F32),
                pltpu.SemaphoreType.DMA((2,)),
            ],
        ),
        out_shape=jax.ShapeDtypeStruct((t, D_MODEL), F32),
        compiler_params=_cparams(("arbitrary",)),
        name="combine",
    )(cnt_tbl, ls_tbl, base_tbl, ys, route, gate_t, x1, p2, *consts)


def _routing_tables(cnt, n_blocks):
    tile_cnt = cnt[:, 0, :N_EXPERTS].astype(jnp.int32)
    counts = jnp.sum(tile_cnt, axis=0)
    padded = (counts + ROW_BLK - 1) // ROW_BLK * ROW_BLK
    pend = jnp.cumsum(padded)
    pstart = pend - padded
    base = pstart[None, :] + jnp.cumsum(tile_cnt, axis=0) - tile_cnt
    lstart = jnp.cumsum(tile_cnt, axis=1) - tile_cnt
    nvalid = pend[-1:] // ROW_BLK
    zoff = jnp.concatenate([pstart + counts, nvalid, padded - counts])
    blk_start = jnp.minimum(jnp.arange(n_blocks, dtype=jnp.int32) * ROW_BLK, pend[-1] - 1)
    blk_e = jnp.minimum(jnp.sum((pend[None, :] <= blk_start[:, None]).astype(jnp.int32), axis=1), N_EXPERTS - 1)
    first = jnp.concatenate([jnp.ones((1,), bool), blk_e[1:] != blk_e[:-1]])
    wslot = (jnp.cumsum(first.astype(jnp.int32)) - 1) % 2
    eids = jnp.arange(N_EXPERTS, dtype=jnp.int32)
    later_nonempty = (eids[None, :] > eids[:, None]) & (padded[None, :] > 0)
    next_e = jnp.min(jnp.where(later_nonempty, eids[None, :], N_EXPERTS), axis=1)
    next_e = jnp.where(next_e == N_EXPERTS, -1, next_e)
    blk_hot = blk_e[:, None] == eids[None, :]
    per_block = lambda v: jnp.sum(jnp.where(blk_hot, v[None, :], 0), axis=1)
    blk_ids = jnp.arange(n_blocks, dtype=jnp.int32)
    real_rows = jnp.clip(per_block(pstart + counts) - blk_ids * ROW_BLK, 0, ROW_BLK) * (blk_ids < nvalid[0])
    pieces = (real_rows + ROW_PIECE - 1) // ROW_PIECE
    i32 = lambda a: a.reshape(-1).astype(jnp.int32)
    plan = (i32(blk_e), i32(nvalid), i32(first), i32(wslot), i32(per_block(next_e)), i32(pieces))
    return (i32(tile_cnt), i32(lstart), i32(base), i32(zoff)), plan


def kernel(x, p, g_mix, w_in, conv_w, conv_b, dt_bias, a_log, d_skip, ssd_norm_w, lam_q1, lam_k1, lam_q2, lam_k2, subln_w, w_out, g_ffn, w_router, b_router, w_up, b_up, w_down, b_down, g_ple, w_ple_gate, w_ple_proj, g_final):
    bsz, seq, d = x.shape
    t = bsz * seq
    x2 = x.reshape(t, d)
    y_ssd, y_att = _mixer(x2, g_mix[0], w_in[0], conv_w[0], conv_b[0], dt_bias[0], a_log[0], d_skip[0], ssd_norm_w[0],
                          lam_q1[0], lam_k1[0], lam_q2[0], lam_k2[0], subln_w[0], bsz, seq)
    x1, xn, route, gate_t, cnt = _out_proj(x2, y_ssd, y_att, w_out[0], g_ffn[0], w_router[0], b_router[0])

    n_rows = t * TOP_K + N_EXPERTS * ROW_BLK
    tables, plan = _routing_tables(cnt, n_rows // ROW_BLK)
    xs = _dispatch(xn, route, tables, n_rows)
    ys = _experts(xs, plan, w_up[0], b_up[0], w_down[0], b_down[0], n_rows)
    out = _combine(ys, route, gate_t, x1, p[0].reshape(t, PLE_DIM), tables, g_ple[0], w_ple_gate[0], w_ple_proj[0],
                   g_final)
    return out.reshape(bsz, seq, d)
```

```python
import math

import jax
import jax.numpy as jnp
from jax import lax
from jax.experimental import pallas as pl
from jax.experimental.pallas import tpu as pltpu

F32 = jnp.float32
BF16 = jnp.bfloat16

D_MODEL = 1024
PLE_DIM = 256
SSD_WIDTH = 512
ATT_WIDTH = 512
SSD_HEAD_DIM = 64
SSD_HEADS = 8
SSD_GROUPS = 2
SSD_STATE = 128
SSD_CONV = 4
SSD_CHUNK = 128
SSD_CONV_CH = SSD_WIDTH + 2 * SSD_GROUPS * SSD_STATE
SSD_NORM_EPS = 1e-5
ATT_HEAD_DIM = 64
ATT_HEADS = 4
SUBLN_EPS = 1e-5
OFF_Z = 0
OFF_XBC = OFF_Z + SSD_WIDTH
OFF_DT = OFF_XBC + SSD_CONV_CH
OFF_Q = OFF_DT + SSD_HEADS
OFF_K = OFF_Q + ATT_WIDTH
OFF_V = OFF_K + ATT_WIDTH
IN_PROJ = OFF_V + ATT_WIDTH
N_EXPERTS = 32
TOP_K = 4
D_EXPERT = 1024
SWIGLU_LIMIT = 7.0
SWIGLU_ALPHA = 1.702
NORM_EPS = 1e-6
LAM_INIT = 0.8 - 0.6 * math.exp(-0.3 * 0)

LANES = 128
SUBLANES = 8
VMEM_LIMIT_BYTES = 56 * 1024 * 1024
ROW_SUB = D_MODEL // LANES

TM_PROJ = 512
ATT_BQ = 512
ATT_BK = 512
ROW_BLK = 512
ROW_PIECE = 256
TM_ROUTE = 256
OUT_SUB = 4
COMB_SUB = 2
DEINT = 2 * LANES


def _cparams(sem):
    return pltpu.CompilerParams(dimension_semantics=sem, vmem_limit_bytes=VMEM_LIMIT_BYTES)


def _rms(x, w, eps):
    return x * lax.rsqrt(jnp.mean(x * x, axis=-1, keepdims=True) + eps) * w


def _dot(a, b):
    return jnp.dot(a, b, preferred_element_type=F32)


def _dot_nt(a, b):
    return lax.dot_general(a, b, (((1,), (1,)), ((), ())), preferred_element_type=F32)


def _dot_tn(a, b):
    return lax.dot_general(a, b, (((0,), (0,)), ((), ())), preferred_element_type=F32)


def _store_rows(ref, val):
    n = val.shape[0]
    for s in range(ROW_SUB):
        ref[pl.ds(s, n, stride=ROW_SUB), :] = val[:, s * LANES:(s + 1) * LANES]


def _load_rows(ref):
    n = ref.shape[0] // ROW_SUB
    return jnp.concatenate([ref[pl.ds(s, n, stride=ROW_SUB), :] for s in range(ROW_SUB)], axis=-1)


def _rows_spec(n, index_map):
    return pl.BlockSpec((n * ROW_SUB, LANES), index_map)


def _row_slice(start, n):
    return pl.ds(pl.multiple_of(start * ROW_SUB, ROW_SUB), n * ROW_SUB)


def _split3(x):
    hi = x.astype(BF16)
    r1 = x - hi.astype(F32)
    mid = r1.astype(BF16)
    lo = (r1 - mid.astype(F32)).astype(BF16)
    return hi, mid, lo


def _in_proj_kernel(x_ref, g_ref, wz_ref, wxbc_ref, wdt_ref, wq_ref, wk_ref, wv_ref,
                    z_ref, xbc_ref, dt_ref, q_ref, k_ref, v_ref):
    h = _rms(x_ref[...], g_ref[...], NORM_EPS).astype(BF16)
    z_ref[...] = _dot(h, wz_ref[...])
    xbc_ref[...] = _dot(h, wxbc_ref[...])
    dt_ref[...] = _dot(h, wdt_ref[...])
    q_ref[...] = (_dot(h, wq_ref[...]) * (ATT_HEAD_DIM ** -0.5 * math.log2(math.e))).astype(BF16)
    k_ref[...] = _dot(h, wk_ref[...]).astype(BF16)
    v_ref[...] = _dot(h, wv_ref[...]).astype(BF16)


def _in_proj(x2, g_mix, w_in):
    t = x2.shape[0]
    wb = w_in.astype(BF16)
    wz = wb[:, OFF_Z:OFF_XBC]
    wxbc = wb[:, OFF_XBC:OFF_DT]
    wdt = jnp.pad(wb[:, OFF_DT:OFF_Q], ((0, 0), (0, LANES - SSD_HEADS)))
    wq = wb[:, OFF_Q:OFF_K]
    wk = wb[:, OFF_K:OFF_V]
    wv = wb[:, OFF_V:IN_PROJ]
    tm = TM_PROJ
    row = lambda n: pl.BlockSpec((tm, n), lambda i: (i, 0))
    full = lambda a: pl.BlockSpec(a.shape, lambda i: (0, 0))
    return pl.pallas_call(
        _in_proj_kernel,
        grid=(t // tm,),
        in_specs=[row(D_MODEL), full(g_mix), full(wz), full(wxbc), full(wdt), full(wq), full(wk), full(wv)],
        out_specs=[row(SSD_WIDTH), row(SSD_CONV_CH), row(LANES), row(ATT_WIDTH), row(ATT_WIDTH), row(ATT_WIDTH)],
        out_shape=[
            jax.ShapeDtypeStruct((t, SSD_WIDTH), F32),
            jax.ShapeDtypeStruct((t, SSD_CONV_CH), F32),
            jax.ShapeDtypeStruct((t, LANES), F32),
            jax.ShapeDtypeStruct((t, ATT_WIDTH), BF16),
            jax.ShapeDtypeStruct((t, ATT_WIDTH), BF16),
            jax.ShapeDtypeStruct((t, ATT_WIDTH), BF16),
        ],
        compiler_params=_cparams(("arbitrary",)),
        name="in_proj",
    )(x2, g_mix, wz, wxbc, wdt, wq, wk, wv)


def _ssd_chunk_init(c, xpad_ref, state_ref):
    L = SSD_CHUNK

    @pl.when(c == 0)
    def _():
        xpad_ref[0:SUBLANES, :] = jnp.zeros((SUBLANES, SSD_CONV_CH), F32)
        state_ref[...] = jnp.zeros_like(state_ref)

    @pl.when(c != 0)
    def _():
        xpad_ref[0:SUBLANES, :] = xpad_ref[L:L + SUBLANES, :]


def _ssd_chunk(xbc_ref, dtr_ref, z_ref, cw_ref, cb_ref, dtb_ref, alog_ref, dskip_ref, nw_ref,
               y_ref, xpad_ref, state_ref):
    L = SSD_CHUNK
    xpad_ref[SUBLANES:SUBLANES + L, :] = xbc_ref[...]

    conv = cb_ref[...]
    for j in range(SSD_CONV):
        off = SUBLANES - (SSD_CONV - 1) + j
        conv = conv + cw_ref[j:j + 1, :] * xpad_ref[off:off + L, :]
    act = conv * jax.nn.sigmoid(conv)
    xs = act[:, :SSD_WIDTH]
    bm = act[:, SSD_WIDTH:SSD_WIDTH + SSD_GROUPS * SSD_STATE].astype(BF16)
    cm = act[:, SSD_WIDTH + SSD_GROUPS * SSD_STATE:].astype(BF16)

    dt_in = dtr_ref[...] + dtb_ref[...]
    dt_all = jnp.maximum(dt_in, 0.0) + jnp.log1p(jnp.exp(-jnp.abs(dt_in)))
    adt = dt_all * (-jnp.exp(alog_ref[...]))

    ri = lax.broadcasted_iota(jnp.int32, (L, L), 0)
    ci = lax.broadcasted_iota(jnp.int32, (L, L), 1)
    causal = ci <= ri
    tril = jnp.where(causal, 1.0, 0.0).astype(BF16)
    hi, mid, lo = _split3(adt)
    acum_all = _dot(tril, hi) + _dot(tril, mid) + _dot(tril, lo)
    acum_t = acum_all.T
    a_last = acum_all[L - 1:L, :]
    decay_in_all = jnp.exp(a_last - acum_all)
    decay_out_all = jnp.exp(acum_all)
    chunk_decay_all = jnp.exp(a_last)
    acum = lambda h: acum_all[:, h:h + 1]
    dt = lambda h: dt_all[:, h:h + 1]
    decay_out = lambda h: decay_out_all[:, h:h + 1]
    decay_in = lambda h: decay_in_all[:, h:h + 1]
    chunk_decay = lambda h: chunk_decay_all[:, h:h + 1]

    lane = lax.broadcasted_iota(jnp.int32, (L, LANES), 1)
    lo_half = lane < SSD_HEAD_DIM

    def per_pair(col_a, col_b):
        return jnp.where(lo_half, col_a, col_b)

    ys = []
    for pair in range(SSD_HEADS // 2):
        g = pair // 2
        h0, h1 = 2 * pair, 2 * pair + 1
        cg = cm[:, g * SSD_STATE:(g + 1) * SSD_STATE]
        bg = bm[:, g * SSD_STATE:(g + 1) * SSD_STATE]
        cb = _dot_nt(cg, bg)
        x_pair = xs[:, pair * LANES:(pair + 1) * LANES]
        xdt = x_pair * per_pair(dt(h0), dt(h1))
        y_pair = jnp.zeros((L, LANES), F32)
        for hh, keep in ((h0, lo_half), (h1, jnp.logical_not(lo_half))):
            seg = acum(hh) - acum_t[hh:hh + 1, :]
            lmat = jnp.where(causal, jnp.exp(jnp.where(causal, seg, 0.0)), 0.0)
            m = (cb * lmat).astype(BF16)
            y_pair = y_pair + _dot(m, jnp.where(keep, xdt, 0.0).astype(BF16))
        s_prev = state_ref[pair]
        y_off = _dot(cg, s_prev.astype(BF16)) * per_pair(decay_out(h0), decay_out(h1))
        w_in = (xdt * per_pair(decay_in(h0), decay_in(h1))).astype(BF16)
        cd = jnp.where(lane[0:1, :] < SSD_HEAD_DIM, chunk_decay(h0), chunk_decay(h1))
        state_ref[pair] = s_prev * cd + _dot_tn(bg, w_in)
        ys.append(y_pair + y_off + dskip_ref[:, pair * LANES:(pair + 1) * LANES] * x_pair)

    y = jnp.concatenate(ys, axis=-1)
    zz = z_ref[...]
    y = y * (zz * jax.nn.sigmoid(zz))
    gw = SSD_WIDTH // SSD_GROUPS
    outs = []
    for g in range(SSD_GROUPS):
        yg = y[:, g * gw:(g + 1) * gw]
        outs.append(yg * lax.rsqrt(jnp.mean(yg * yg, axis=-1, keepdims=True) + SSD_NORM_EPS))
    y_ref[...] = (jnp.concatenate(outs, axis=-1) * nw_ref[...]).astype(BF16)


def _attn_ssd_kernel(q_ref, k_ref, v_ref, lq1_ref, lk1_ref, lq2_ref, lk2_ref, sw_ref,
                     xbc_ref, dtr_ref, z_ref, cw_ref, cb_ref, dtb_ref, alog_ref, dskip_ref, nw_ref,
                     o_ref, y_ref, s_scr, vt_scr, xpad_ref, state_ref):
    bq, bk = ATT_BQ, ATT_BK
    qi = pl.program_id(2)
    n_maps = 2
    _ssd_chunk_init(pl.program_id(1) * pl.num_programs(2) + qi, xpad_ref, state_ref)

    @pl.when(qi == 0)
    def _():
        vt_scr[...] = v_ref[...].astype(F32).T.astype(BF16)

    def fold(t, reduce):
        return reduce(t.reshape(bk // SUBLANES, SUBLANES, bq), axis=0)

    def merge(old, new, op):
        return new if old is None else op(old, new)

    def attend(nk):
        _ssd_chunk(xbc_ref, dtr_ref, z_ref, cw_ref, cb_ref, dtb_ref, alog_ref, dskip_ref, nw_ref,
                   y_ref, xpad_ref, state_ref)
        q = q_ref[...]
        lane = lax.broadcasted_iota(jnp.int32, (bq, LANES), 1)
        zero = jnp.zeros_like(q)
        q_maps = (jnp.where(lane < ATT_HEAD_DIM, q, zero), jnp.where(lane >= ATT_HEAD_DIM, q, zero))
        key = lax.broadcasted_iota(jnp.int32, (bk, bq), 0)
        qry = lax.broadcasted_iota(jnp.int32, (bk, bq), 1)
        causal = key <= qry

        mt = [None] * n_maps
        for j in range(nk):
            kb = k_ref[j * bk:(j + 1) * bk, :]
            for m in range(n_maps):
                s = _dot_nt(kb, q_maps[m])
                if j == nk - 1:
                    s = jnp.where(causal, s, -jnp.inf)
                s_scr[m, j] = s
                mt[m] = merge(mt[m], fold(s, jnp.max), jnp.maximum)
        q_max = [jnp.max(t, axis=0, keepdims=True) for t in mt]

        lt = [None] * n_maps
        acc = [None] * n_maps
        for j in range(nk):
            vt = vt_scr[:, j * bk:(j + 1) * bk]
            for m in range(n_maps):
                p = jnp.exp2(s_scr[m, j] - q_max[m])
                lt[m] = merge(lt[m], fold(p, jnp.sum), jnp.add)
                acc[m] = merge(acc[m], _dot(vt, p.astype(BF16)), jnp.add)

        lam = (jnp.exp(jnp.sum(lq1_ref[...] * lk1_ref[...], axis=-1, keepdims=True))
               - jnp.exp(jnp.sum(lq2_ref[...] * lk2_ref[...], axis=-1, keepdims=True)) + LAM_INIT)
        l1 = jnp.sum(lt[0], axis=0, keepdims=True)
        l2 = jnp.sum(lt[1], axis=0, keepdims=True)
        o = acc[0] / l1 - lam * (acc[1] / l2)
        o = o * lax.rsqrt(jnp.mean(o * o, axis=0, keepdims=True) + SUBLN_EPS) * sw_ref[...]
        o_ref[...] = (o * (1.0 - LAM_INIT)).T.astype(BF16)

    for nk in range(1, k_ref.shape[0] // bk + 1):
        pl.when(qi == nk - 1)(lambda nk=nk: attend(nk))


def _attn_ssd(q, k, v, lam_q1, lam_k1, lam_q2, lam_k2, subln_w, xbc, dtr, z, conv_w, conv_b, dt_bias, a_log, d_skip,
              norm_w, bsz, seq):
    nq = seq // ATT_BQ
    nc = seq // SSD_CHUNK
    assert nc == ATT_HEADS * nq
    qspec = pl.BlockSpec((ATT_BQ, LANES), lambda b, h, i: (b * nq + i, h))
    kvspec = pl.BlockSpec((seq, LANES), lambda b, h, i: (b, h))
    full = lambda a: pl.BlockSpec(a.shape, lambda b, h, i: (0, 0))
    chunk = lambda n: pl.BlockSpec((SSD_CHUNK, n), lambda b, h, i: (b * nc + h * nq + i, 0))
    lams = [a.reshape(1, -1) for a in (lam_q1, lam_k1, lam_q2, lam_k2)]
    sw = subln_w.reshape(-1, 1)
    pad_h = lambda v: jnp.pad(v.reshape(1, SSD_HEADS), ((0, 0), (0, LANES - SSD_HEADS)))
    dskip_lanes = jnp.repeat(d_skip, SSD_HEAD_DIM).reshape(1, SSD_WIDTH)
    consts = (conv_w, conv_b.reshape(1, -1), pad_h(dt_bias), pad_h(a_log), dskip_lanes, norm_w.reshape(1, -1))
    return pl.pallas_call(
        _attn_ssd_kernel,
        grid=(bsz, ATT_HEADS, nq),
        in_specs=[qspec, kvspec, kvspec] + [full(a) for a in lams] + [full(sw)]
                 + [chunk(SSD_CONV_CH), chunk(LANES), chunk(SSD_WIDTH)] + [full(a) for a in consts],
        out_specs=[qspec, chunk(SSD_WIDTH)],
        out_shape=[jax.ShapeDtypeStruct((bsz * seq, ATT_WIDTH), BF16),
                   jax.ShapeDtypeStruct((bsz * seq, SSD_WIDTH), BF16)],
        scratch_shapes=[pltpu.VMEM((2, seq // ATT_BK, ATT_BK, ATT_BQ), F32),
                        pltpu.VMEM((LANES, seq), BF16),
                        pltpu.VMEM((SSD_CHUNK + 2 * SUBLANES, SSD_CONV_CH), F32),
                        pltpu.VMEM((SSD_HEADS // 2, SSD_STATE, LANES), F32)],
        compiler_params=_cparams(("arbitrary", "arbitrary", "arbitrary")),
        name="attn_ssd",
    )(q, k, v, *lams, sw, xbc, dtr, z, *consts)


def _mixer(x2, g_mix, w_in, conv_w, conv_b, dt_bias, a_log, d_skip, ssd_norm_w,
           lam_q1, lam_k1, lam_q2, lam_k2, subln_w, bsz, seq):
    z, xbc, dtr, q, k, v = _in_proj(x2, g_mix.reshape(1, -1), w_in)
    y_att, y_ssd = _attn_ssd(q, k, v, lam_q1, lam_k1, lam_q2, lam_k2, subln_w, xbc, dtr, z, conv_w, conv_b, dt_bias,
                             a_log, d_skip, ssd_norm_w, bsz, seq)
    return y_ssd, y_att


def _out_proj_kernel(x_ref, ys_ref, ya_ref, wo_ref, g_ref, wr2_ref, br_ref,
                     x1_ref, xn_ref, route_ref, gate_ref, cnt_ref):
    tm = TM_ROUTE
    x1 = x_ref[...] + _dot(jnp.concatenate([ys_ref[...], ya_ref[...]], axis=-1), wo_ref[...])
    x1_ref[...] = x1
    xn_all = _rms(x1, g_ref[...], NORM_EPS)
    xn_ref[...] = xn_all.astype(BF16)

    n_tok = OUT_SUB * tm
    xh, xm, _ = _split3(xn_all)
    lg2 = _dot(xh, wr2_ref[...]) + _dot(xm, wr2_ref[...])
    logits = (lg2[:, :LANES] + lg2[:, LANES:]).T[:N_EXPERTS, :] + br_ref[...]

    eidx = lax.broadcasted_iota(jnp.int32, (N_EXPERTS, n_tok), 0).astype(F32)
    work = logits
    vals, idxs, hots = [], [], []
    for _ in range(TOP_K):
        m = jnp.max(work, axis=0, keepdims=True)
        idx = jnp.min(jnp.where(work == m, eidx, float(N_EXPERTS)), axis=0, keepdims=True)
        hot = eidx == idx
        vals.append(m)
        idxs.append(idx.astype(jnp.int32))
        hots.append(hot)
        work = jnp.where(hot, -jnp.inf, work)
    exps = [jnp.exp(v - vals[0]) for v in vals]
    denom = exps[0] + exps[1] + exps[2] + exps[3]
    gates = [e / denom for e in exps]

    cnt = jnp.zeros((N_EXPERTS, n_tok), F32)
    for hot in hots:
        cnt = cnt + jnp.where(hot, 1.0, 0.0)
    cnt_b = cnt.astype(BF16)
    r = lax.broadcasted_iota(jnp.int32, (tm, tm), 0)
    c = lax.broadcasted_iota(jnp.int32, (tm, tm), 1)
    earlier_tok = jnp.where(r < c, 1.0, 0.0).astype(BF16)
    er = lax.broadcasted_iota(jnp.int32, (N_EXPERTS, N_EXPERTS), 0)
    ec = lax.broadcasted_iota(jnp.int32, (N_EXPERTS, N_EXPERTS), 1)
    lower_exp = jnp.where(ec < er, 1.0, 0.0).astype(BF16)
    below = _dot(lower_exp, cnt_b)
    cnt_pad = jnp.concatenate([cnt_b, jnp.zeros((LANES - N_EXPERTS, n_tok), BF16)], axis=0)
    pos = []
    for sub in range(OUT_SUB):
        cols = slice(sub * tm, (sub + 1) * tm)
        pos.append(_dot(cnt_b[:, cols], earlier_tok) + jnp.sum(below[:, cols], axis=1, keepdims=True))
        cnt_ref[sub] = _dot_nt(jnp.ones((SUBLANES, tm), BF16), cnt_pad[:, cols])
    pos = jnp.concatenate(pos, axis=1)
    lps = [jnp.sum(jnp.where(hot, pos, 0.0), axis=0, keepdims=True).astype(jnp.int32) for hot in hots]

    route_ref[...] = jnp.concatenate(idxs + lps, axis=0)
    gate_ref[...] = jnp.concatenate(gates + [jnp.zeros((SUBLANES - TOP_K, n_tok), F32)], axis=0)


def _out_proj(x2, y_ssd, y_att, w_out, g_ffn, w_router, b_router):
    t = x2.shape[0]
    tm = OUT_SUB * TM_ROUTE
    nt = t // TM_ROUTE
    wo = w_out.astype(BF16)
    wrh, wrm, _ = _split3(jnp.pad(w_router, ((0, 0), (0, LANES - N_EXPERTS))))
    wr2 = jnp.concatenate([wrh, wrm], axis=1)
    br = b_router.reshape(N_EXPERTS, 1)
    row = lambda n: pl.BlockSpec((tm, n), lambda i: (i, 0))
    col = pl.BlockSpec((SUBLANES, tm), lambda i: (0, i))
    full = lambda a: pl.BlockSpec(a.shape, lambda i: (0, 0))
    args = (x2, y_ssd, y_att, wo, g_ffn.reshape(1, -1), wr2, br)
    return pl.pallas_call(
        _out_proj_kernel,
        grid=(t // tm,),
        in_specs=[row(D_MODEL), row(SSD_WIDTH), row(ATT_WIDTH)] + [full(a) for a in args[3:]],
        out_specs=[row(D_MODEL), row(D_MODEL), col, col,
                   pl.BlockSpec((OUT_SUB, SUBLANES, LANES), lambda i: (i, 0, 0))],
        out_shape=[
            jax.ShapeDtypeStruct((t, D_MODEL), F32),
            jax.ShapeDtypeStruct((t, D_MODEL), BF16),
            jax.ShapeDtypeStruct((SUBLANES, t), jnp.int32),
            jax.ShapeDtypeStruct((SUBLANES, t), F32),
            jax.ShapeDtypeStruct((nt, SUBLANES, LANES), F32),
        ],
        compiler_params=_cparams(("arbitrary",)),
        name="out_proj",
    )(*args)


def _copy_run(n, start_copy):
    @pl.when(n > 0)
    def _():
        start_copy(n)


def _select_by_position(positions, values, n_pos):
    tm = positions[0].shape[1]
    r = lax.broadcasted_iota(jnp.int32, (n_pos, tm), 0)
    out = jnp.zeros((n_pos, tm), F32)
    for k in reversed(range(TOP_K)):
        out = jnp.where(r == positions[k], values[k], out)
    return out


def _dispatch_kernel(cnt_ref, ls_ref, base_ref, zoff_ref, xn_ref, route_ref, xs_hbm, xloc, zeros_vmem, sem_z, sems):
    tm = TM_ROUTE
    n_pos = TOP_K * tm
    i = pl.program_id(0)
    n_steps = pl.num_programs(0)
    slot = lax.rem(i, 2)

    def zero_fill(op):
        def pad_rows(e, _):
            n = zoff_ref[N_EXPERTS + 1 + e]

            @pl.when(n > 0)
            def _():
                op(pltpu.make_async_copy(zeros_vmem.at[_row_slice(0, n)],
                                         xs_hbm.at[_row_slice(zoff_ref[e], n)], sem_z))
            return 0

        def unused_block(b, _):
            op(pltpu.make_async_copy(zeros_vmem, xs_hbm.at[_row_slice(b * ROW_BLK, ROW_BLK)], sem_z))
            return 0

        lax.fori_loop(0, N_EXPERTS, pad_rows, 0)
        lax.fori_loop(zoff_ref[N_EXPERTS], xs_hbm.shape[0] // (ROW_BLK * ROW_SUB), unused_block, 0)

    @pl.when(i == 0)
    def _():
        zeros_vmem[...] = jnp.zeros_like(zeros_vmem)
        zero_fill(lambda cp: cp.start())

    positions = [route_ref[TOP_K + k:TOP_K + k + 1, :] for k in range(TOP_K)]
    sel = _select_by_position(positions, [1.0] * TOP_K, n_pos).astype(BF16)
    rows = _dot(sel, xn_ref[...])

    def slot_wait(s):
        pltpu.make_async_copy(xloc.at[s], xs_hbm.at[_row_slice(0, n_pos)], sems.at[s]).wait()

    @pl.when(i >= 2)
    def _():
        slot_wait(slot)

    _store_rows(xloc.at[slot], rows)

    def per_expert(e, _):
        idx = i * N_EXPERTS + e
        src0 = ls_ref[idx]
        dst0 = base_ref[idx]

        def start_copy(n):
            pltpu.make_async_copy(xloc.at[slot, _row_slice(src0, n)],
                                  xs_hbm.at[_row_slice(dst0, n)], sems.at[slot]).start()

        _copy_run(cnt_ref[idx], start_copy)
        return 0

    lax.fori_loop(0, N_EXPERTS, per_expert, 0)

    @pl.when(i == n_steps - 1)
    def _():
        slot_wait(slot)

        @pl.when(n_steps > 1)
        def _():
            slot_wait(1 - slot)

        zero_fill(lambda cp: cp.wait())


def _dispatch(xn, route, tables, n_rows):
    t = xn.shape[0]
    tm = TM_ROUTE
    cnt_tbl, ls_tbl, base_tbl, zoff = tables
    return pl.pallas_call(
        _dispatch_kernel,
        grid_spec=pltpu.PrefetchScalarGridSpec(
            num_scalar_prefetch=4,
            grid=(t // tm,),
            in_specs=[pl.BlockSpec((tm, D_MODEL), lambda i, *_: (i, 0)),
                      pl.BlockSpec((SUBLANES, tm), lambda i, *_: (0, i))],
            out_specs=pl.BlockSpec(memory_space=pl.ANY),
            scratch_shapes=[
                pltpu.VMEM((2, TOP_K * tm * ROW_SUB, LANES), F32),
                pltpu.VMEM((ROW_BLK * ROW_SUB, LANES), F32),
                pltpu.SemaphoreType.DMA,
                pltpu.SemaphoreType.DMA((2,)),
            ],
        ),
        out_shape=jax.ShapeDtypeStruct(((n_rows + ROW_BLK) * ROW_SUB, LANES), F32),
        compiler_params=_cparams(("arbitrary",)),
        name="dispatch",
    )(cnt_tbl, ls_tbl, base_tbl, zoff, xn, route)


def _experts_kernel(blk_e_ref, nvalid_ref, first_ref, wslot_ref, enext_ref, pieces_ref, xs_ref, wup_hbm, wdn_hbm,
                    bg_ref, bu_ref, bd_ref, ys_ref, wup_buf, wdn_buf, wg_s, wu_s, wd_s, sems):
    i = pl.program_id(0)
    slot = wslot_ref[i]

    def weight_copies(e, s):
        return (pltpu.make_async_copy(wup_hbm.at[e], wup_buf.at[s], sems.at[0, s]),
                pltpu.make_async_copy(wdn_hbm.at[e], wdn_buf.at[s], sems.at[1, s]))

    @pl.when(i == 0)
    def _():
        for cp in weight_copies(blk_e_ref[0], slot):
            cp.start()

    @pl.when(first_ref[i] != 0)
    def _():
        for cp in weight_copies(blk_e_ref[i], slot):
            cp.wait()

        @pl.when(enext_ref[i] >= 0)
        def _():
            for cp in weight_copies(enext_ref[i], 1 - slot):
                cp.start()

        src = lax.broadcasted_iota(jnp.int32, (DEINT, DEINT), 0)
        dst = lax.broadcasted_iota(jnp.int32, (DEINT, DEINT), 1)
        perm = jnp.where(src == jnp.where(dst < LANES, 2 * dst, 2 * (dst - LANES) + 1), 1.0, 0.0).astype(BF16)
        for g in range(2 * D_EXPERT // DEINT):
            sep = _dot(wup_buf[slot, :, g * DEINT:(g + 1) * DEINT].astype(BF16), perm)
            wg_s[:, g * LANES:(g + 1) * LANES] = sep[:, :LANES].astype(BF16)
            wu_s[:, g * LANES:(g + 1) * LANES] = sep[:, LANES:].astype(BF16)
        wd_s[...] = wdn_buf[slot].astype(BF16)

    def ffn(n_rows):
        used = n_rows * ROW_SUB
        if n_rows:
            xb = _load_rows(xs_ref.at[pl.ds(0, used)]).astype(BF16)
            gate = jnp.minimum(_dot(xb, wg_s[...]) + bg_ref[0], SWIGLU_LIMIT)
            up = jnp.clip(_dot(xb, wu_s[...]) + bu_ref[0], -SWIGLU_LIMIT, SWIGLU_LIMIT)
            act = (up + 1.0) * gate * jax.nn.sigmoid(SWIGLU_ALPHA * gate)
            _store_rows(ys_ref.at[pl.ds(0, used)], _dot(act.astype(BF16), wd_s[...]) + bd_ref[0])
        if n_rows < ROW_BLK:
            ys_ref[used:, :] = jnp.zeros((ROW_BLK * ROW_SUB - used, LANES), F32)

    for pieces in range(ROW_BLK // ROW_PIECE + 1):
        pl.when(pieces_ref[i] == pieces)(lambda pieces=pieces: ffn(pieces * ROW_PIECE))


def _experts(xs, plan, w_up, b_up, w_down, b_down, n_rows):
    nb = n_rows // ROW_BLK
    bg = b_up[:, 0::2].reshape(N_EXPERTS, 1, D_EXPERT)
    bu = b_up[:, 1::2].reshape(N_EXPERTS, 1, D_EXPERT)
    bd = b_down.reshape(N_EXPERTS, 1, D_MODEL)
    blk_e, nvalid, first, wslot, enext, pieces = plan
    src = lambda i, be, nv, *_: (jnp.minimum(i, nv[0] - 1), 0)
    bspec = lambda m: pl.BlockSpec((1, 1, m), lambda i, be, *_: (be[i], 0, 0))
    anyspec = pl.BlockSpec(memory_space=pl.ANY)
    return pl.pallas_call(
        _experts_kernel,
        grid_spec=pltpu.PrefetchScalarGridSpec(
            num_scalar_prefetch=6,
            grid=(nb,),
            in_specs=[_rows_spec(ROW_BLK, src), anyspec, anyspec,
                      bspec(D_EXPERT), bspec(D_EXPERT), bspec(D_MODEL)],
            out_specs=_rows_spec(ROW_BLK, lambda i, *_: (i, 0)),
            scratch_shapes=[
                pltpu.VMEM((2, D_MODEL, 2 * D_EXPERT), F32),
                pltpu.VMEM((2, D_EXPERT, D_MODEL), F32),
                pltpu.VMEM((D_MODEL, D_EXPERT), BF16),
                pltpu.VMEM((D_MODEL, D_EXPERT), BF16),
                pltpu.VMEM((D_EXPERT, D_MODEL), BF16),
                pltpu.SemaphoreType.DMA((2, 2)),
            ],
        ),
        out_shape=jax.ShapeDtypeStruct((n_rows * ROW_SUB, LANES), F32),
        compiler_params=_cparams(("arbitrary",)),
        name="experts",
    )(blk_e, nvalid, first, wslot, enext, pieces, xs, w_up, w_down, bg, bu, bd)


def _combine_kernel(cnt_ref, ls_ref, base_ref, ys_hbm, route_ref, gate_ref, x1_ref, p_ref, gp_ref, wpg_ref,
                    wpp_ref, gf_ref, o_ref, yloc, sems):
    tm = TM_ROUTE
    n_pos = TOP_K * tm
    i = pl.program_id(0)
    n_steps = pl.num_programs(0)
    slot = lax.rem(i, 2)

    def gather_step(step, s):
        for u in range(COMB_SUB):
            def per_expert(e, _, u=u):
                idx = (step * COMB_SUB + u) * N_EXPERTS + e
                src0 = base_ref[idx]
                dst0 = ls_ref[idx]

                def start_copy(n):
                    pltpu.make_async_copy(ys_hbm.at[_row_slice(src0, n)],
                                          yloc.at[s, u, _row_slice(dst0, n)], sems.at[s]).start()

                _copy_run(cnt_ref[idx], start_copy)
                return 0

            lax.fori_loop(0, N_EXPERTS, per_expert, 0)

    @pl.when(i == 0)
    def _():
        gather_step(0, 0)

    @pl.when(i + 1 < n_steps)
    def _():
        gather_step(i + 1, 1 - slot)

    sels, g_poss = [], []
    for u in range(COMB_SUB):
        cols = slice(u * tm, (u + 1) * tm)
        positions = [route_ref[TOP_K + k:TOP_K + k + 1, cols] for k in range(TOP_K)]
        sels.append(_select_by_position(positions, [1.0] * TOP_K, n_pos).astype(BF16))
        gsel = _select_by_position(positions, [gate_ref[k:k + 1, cols] for k in range(TOP_K)], n_pos)
        g_poss.append(jnp.sum(gsel, axis=1, keepdims=True))
    pp = _dot(p_ref[...].astype(BF16), wpp_ref[...])

    for u in range(COMB_SUB):
        pltpu.make_async_copy(ys_hbm.at[_row_slice(0, n_pos)], yloc.at[slot, u], sems.at[slot]).wait()
    moe = []
    for u in range(COMB_SUB):
        y_gated = (_load_rows(yloc.at[slot, u]) * g_poss[u]).astype(BF16)
        moe.append(_dot_tn(sels[u], y_gated))
    x2 = x1_ref[...] + jnp.concatenate(moe, axis=0)
    xn = _rms(x2, gp_ref[...], NORM_EPS).astype(BF16)
    x3 = x2 + pp * jax.nn.sigmoid(_dot(xn, wpg_ref[...]))
    o_ref[...] = _rms(x3, gf_ref[...], NORM_EPS)


def _combine(ys, route, gate_t, x1, p2, tables, g_ple, w_ple_gate, w_ple_proj, g_final):
    t = x1.shape[0]
    tm = COMB_SUB * TM_ROUTE
    cnt_tbl, ls_tbl, base_tbl, _ = tables
    row = lambda n: pl.BlockSpec((tm, n), lambda i, *_: (i, 0))
    col = pl.BlockSpec((SUBLANES, tm), lambda i, *_: (0, i))
    full = lambda a: pl.BlockSpec(a.shape, lambda i, *_: (0, 0))
    consts = (g_ple.reshape(1, -1), w_ple_gate.astype(BF16), w_ple_proj.astype(BF16), g_final.reshape(1, -1))
    return pl.pallas_call(
        _combine_kernel,
        grid_spec=pltpu.PrefetchScalarGridSpec(
            num_scalar_prefetch=3,
            grid=(t // tm,),
            in_specs=[pl.BlockSpec(memory_space=pl.ANY), col, col, row(D_MODEL), row(PLE_DIM)]
                     + [full(a) for a in consts],
            out_specs=row(D_MODEL),
            scratch_shapes=[
                pltpu.VMEM((2, COMB_SUB, TOP_K * TM_ROUTE * ROW_SUB, LANES), F32),
                pltpu.SemaphoreType.DMA((2,)),
            ],
        ),
        out_shape=jax.ShapeDtypeStruct((t, D_MODEL), F32),
        compiler_params=_cparams(("arbitrary",)),
        name="combine",
    )(cnt_tbl, ls_tbl, base_tbl, ys, route, gate_t, x1, p2, *consts)


def _routing_tables(cnt, n_blocks):
    tile_cnt = cnt[:, 0, :N_EXPERTS].astype(jnp.int32)
    counts = jnp.sum(tile_cnt, axis=0)
    padded = (counts + ROW_BLK - 1) // ROW_BLK * ROW_BLK
    pend = jnp.cumsum(padded)
    pstart = pend - padded
    base = pstart[None, :] + jnp.cumsum(tile_cnt, axis=0) - tile_cnt
    lstart = jnp.cumsum(tile_cnt, axis=1) - tile_cnt
    nvalid = pend[-1:] // ROW_BLK
    zoff = jnp.concatenate([pstart + counts, nvalid, padded - counts])
    blk_start = jnp.minimum(jnp.arange(n_blocks, dtype=jnp.int32) * ROW_BLK, pend[-1] - 1)
    blk_e = jnp.minimum(jnp.sum((pend[None, :] <= blk_start[:, None]).astype(jnp.int32), axis=1), N_EXPERTS - 1)
    first = jnp.concatenate([jnp.ones((1,), bool), blk_e[1:] != blk_e[:-1]])
    wslot = (jnp.cumsum(first.astype(jnp.int32)) - 1) % 2
    eids = jnp.arange(N_EXPERTS, dtype=jnp.int32)
    later_nonempty = (eids[None, :] > eids[:, None]) & (padded[None, :] > 0)
    next_e = jnp.min(jnp.where(later_nonempty, eids[None, :], N_EXPERTS), axis=1)
    next_e = jnp.where(next_e == N_EXPERTS, -1, next_e)
    blk_hot = blk_e[:, None] == eids[None, :]
    per_block = lambda v: jnp.sum(jnp.where(blk_hot, v[None, :], 0), axis=1)
    blk_ids = jnp.arange(n_blocks, dtype=jnp.int32)
    real_rows = jnp.clip(per_block(pstart + counts) - blk_ids * ROW_BLK, 0, ROW_BLK) * (blk_ids < nvalid[0])
    pieces = (real_rows + ROW_PIECE - 1) // ROW_PIECE
    i32 = lambda a: a.reshape(-1).astype(jnp.int32)
    plan = (i32(blk_e), i32(nvalid), i32(first), i32(wslot), i32(per_block(next_e)), i32(pieces))
    return (i32(tile_cnt), i32(lstart), i32(base), i32(zoff)), plan


def kernel(x, p, g_mix, w_in, conv_w, conv_b, dt_bias, a_log, d_skip, ssd_norm_w, lam_q1, lam_k1, lam_q2, lam_k2, subln_w, w_out, g_ffn, w_router, b_router, w_up, b_up, w_down, b_down, g_ple, w_ple_gate, w_ple_proj, g_final):
    bsz, seq, d = x.shape
    t = bsz * seq
    x2 = x.reshape(t, d)
    y_ssd, y_att = _mixer(x2, g_mix[0], w_in[0], conv_w[0], conv_b[0], dt_bias[0], a_log[0], d_skip[0], ssd_norm_w[0],
                          lam_q1[0], lam_k1[0], lam_q2[0], lam_k2[0], subln_w[0], bsz, seq)
    x1, xn, route, gate_t, cnt = _out_proj(x2, y_ssd, y_att, w_out[0], g_ffn[0], w_router[0], b_router[0])

    n_rows = t * TOP_K + N_EXPERTS * ROW_BLK
    tables, plan = _routing_tables(cnt, n_rows // ROW_BLK)
    xs = _dispatch(xn, route, tables, n_rows)
    ys = _experts(xs, plan, w_up[0], b_up[0], w_down[0], b_down[0], n_rows)
    out = _combine(ys, route, gate_t, x1, p[0].reshape(t, PLE_DIM), tables, g_ple[0], w_ple_gate[0], w_ple_proj[0],
                   g_final)
    return out.reshape(bsz, seq, d)
```

```python
import math

import jax
import jax.numpy as jnp
from jax import lax
from jax.experimental import pallas as pl
from jax.experimental.pallas import tpu as pltpu

F32 = jnp.float32
BF16 = jnp.bfloat16

D_MODEL = 1024
PLE_DIM = 256
SSD_WIDTH = 512
ATT_WIDTH = 512
SSD_HEAD_DIM = 64
SSD_HEADS = 8
SSD_GROUPS = 2
SSD_STATE = 128
SSD_CONV = 4
SSD_CHUNK = 128
SSD_CONV_CH = SSD_WIDTH + 2 * SSD_GROUPS * SSD_STATE
SSD_NORM_EPS = 1e-5
ATT_HEAD_DIM = 64
ATT_HEADS = 4
SUBLN_EPS = 1e-5
OFF_Z = 0
OFF_XBC = OFF_Z + SSD_WIDTH
OFF_DT = OFF_XBC + SSD_CONV_CH
OFF_Q = OFF_DT + SSD_HEADS
OFF_K = OFF_Q + ATT_WIDTH
OFF_V = OFF_K + ATT_WIDTH
IN_PROJ = OFF_V + ATT_WIDTH
N_EXPERTS = 32
TOP_K = 4
D_EXPERT = 1024
SWIGLU_LIMIT = 7.0
SWIGLU_ALPHA = 1.702
NORM_EPS = 1e-6
LAM_INIT = 0.8 - 0.6 * math.exp(-0.3 * 0)

LANES = 128
SUBLANES = 8
VMEM_LIMIT_BYTES = 56 * 1024 * 1024
ROW_SUB = D_MODEL // LANES

TM_PROJ = 512
ATT_BQ = 512
ATT_BK = 512
ROW_BLK = 512
ROW_PIECE = 256
TM_ROUTE = 256
OUT_SUB = 4
COMB_SUB = 2
DEINT = 2 * LANES


def _cparams(sem):
    return pltpu.CompilerParams(dimension_semantics=sem, vmem_limit_bytes=VMEM_LIMIT_BYTES)


def _rms(x, w, eps):
    return x * lax.rsqrt(jnp.mean(x * x, axis=-1, keepdims=True) + eps) * w


def _dot(a, b):
    return jnp.dot(a, b, preferred_element_type=F32)


def _dot_nt(a, b):
    return lax.dot_general(a, b, (((1,), (1,)), ((), ())), preferred_element_type=F32)


def _dot_tn(a, b):
    return lax.dot_general(a, b, (((0,), (0,)), ((), ())), preferred_element_type=F32)


def _store_rows(ref, val):
    n = val.shape[0]
    for s in range(ROW_SUB):
        ref[pl.ds(s, n, stride=ROW_SUB), :] = val[:, s * LANES:(s + 1) * LANES]


def _load_rows(ref):
    n = ref.shape[0] // ROW_SUB
    return jnp.concatenate([ref[pl.ds(s, n, stride=ROW_SUB), :] for s in range(ROW_SUB)], axis=-1)


def _rows_spec(n, index_map):
    return pl.BlockSpec((n * ROW_SUB, LANES), index_map)


def _row_slice(start, n):
    return pl.ds(pl.multiple_of(start * ROW_SUB, ROW_SUB), n * ROW_SUB)


def _split3(x):
    hi = x.astype(BF16)
    r1 = x - hi.astype(F32)
    mid = r1.astype(BF16)
    lo = (r1 - mid.astype(F32)).astype(BF16)
    return hi, mid, lo


def _in_proj_kernel(x_ref, g_ref, wz_ref, wxbc_ref, wdt_ref, wq_ref, wk_ref, wv_ref,
                    z_ref, xbc_ref, dt_ref, q_ref, k_ref, v_ref):
    h = _rms(x_ref[...], g_ref[...], NORM_EPS).astype(BF16)
    z_ref[...] = _dot(h, wz_ref[...])
    xbc_ref[...] = _dot(h, wxbc_ref[...])
    dt_ref[...] = _dot(h, wdt_ref[...])
    q_ref[...] = (_dot(h, wq_ref[...]) * (ATT_HEAD_DIM ** -0.5 * math.log2(math.e))).astype(BF16)
    k_ref[...] = _dot(h, wk_ref[...]).astype(BF16)
    v_ref[...] = _dot(h, wv_ref[...]).astype(BF16)


def _in_proj(x2, g_mix, w_in):
    t = x2.shape[0]
    wb = w_in.astype(BF16)
    wz = wb[:, OFF_Z:OFF_XBC]
    wxbc = wb[:, OFF_XBC:OFF_DT]
    wdt = jnp.pad(wb[:, OFF_DT:OFF_Q], ((0, 0), (0, LANES - SSD_HEADS)))
    wq = wb[:, OFF_Q:OFF_K]
    wk = wb[:, OFF_K:OFF_V]
    wv = wb[:, OFF_V:IN_PROJ]
    tm = TM_PROJ
    row = lambda n: pl.BlockSpec((tm, n), lambda i: (i, 0))
    full = lambda a: pl.BlockSpec(a.shape, lambda i: (0, 0))
    return pl.pallas_call(
        _in_proj_kernel,
        grid=(t // tm,),
        in_specs=[row(D_MODEL), full(g_mix), full(wz), full(wxbc), full(wdt), full(wq), full(wk), full(wv)],
        out_specs=[row(SSD_WIDTH), row(SSD_CONV_CH), row(LANES), row(ATT_WIDTH), row(ATT_WIDTH), row(ATT_WIDTH)],
        out_shape=[
            jax.ShapeDtypeStruct((t, SSD_WIDTH), F32),
            jax.ShapeDtypeStruct((t, SSD_CONV_CH), F32),
            jax.ShapeDtypeStruct((t, LANES), F32),
            jax.ShapeDtypeStruct((t, ATT_WIDTH), BF16),
            jax.ShapeDtypeStruct((t, ATT_WIDTH), BF16),
            jax.ShapeDtypeStruct((t, ATT_WIDTH), BF16),
        ],
        compiler_params=_cparams(("arbitrary",)),
        name="in_proj",
    )(x2, g_mix, wz, wxbc, wdt, wq, wk, wv)


def _ssd_chunk_init(c, xpad_ref, state_ref):
    L = SSD_CHUNK

    @pl.when(c == 0)
    def _():
        xpad_ref[0:SUBLANES, :] = jnp.zeros((SUBLANES, SSD_CONV_CH), F32)
        state_ref[...] = jnp.zeros_like(state_ref)

    @pl.when(c != 0)
    def _():
        xpad_ref[0:SUBLANES, :] = xpad_ref[L:L + SUBLANES, :]


def _ssd_chunk(xbc_ref, dtr_ref, z_ref, cw_ref, cb_ref, dtb_ref, alog_ref, dskip_ref, nw_ref,
               y_ref, xpad_ref, state_ref):
    L = SSD_CHUNK
    xpad_ref[SUBLANES:SUBLANES + L, :] = xbc_ref[...]

    conv = cb_ref[...]
    for j in range(SSD_CONV):
        off = SUBLANES - (SSD_CONV - 1) + j
        conv = conv + cw_ref[j:j + 1, :] * xpad_ref[off:off + L, :]
    act = conv * jax.nn.sigmoid(conv)
    xs = act[:, :SSD_WIDTH]
    bm = act[:, SSD_WIDTH:SSD_WIDTH + SSD_GROUPS * SSD_STATE].astype(BF16)
    cm = act[:, SSD_WIDTH + SSD_GROUPS * SSD_STATE:].astype(BF16)

    dt_in = dtr_ref[...] + dtb_ref[...]
    dt_all = jnp.maximum(dt_in, 0.0) + jnp.log1p(jnp.exp(-jnp.abs(dt_in)))
    adt = dt_all * (-jnp.exp(alog_ref[...]))

    ri = lax.broadcasted_iota(jnp.int32, (L, L), 0)
    ci = lax.broadcasted_iota(jnp.int32, (L, L), 1)
    causal = ci <= ri
    tril = jnp.where(causal, 1.0, 0.0).astype(BF16)
    hi, mid, lo = _split3(adt)
    acum_all = _dot(tril, hi) + _dot(tril, mid) + _dot(tril, lo)
    acum_t = acum_all.T
    a_last = acum_all[L - 1:L, :]
    decay_in_all = jnp.exp(a_last - acum_all)
    decay_out_all = jnp.exp(acum_all)
    chunk_decay_all = jnp.exp(a_last)
    acum = lambda h: acum_all[:, h:h + 1]
    dt = lambda h: dt_all[:, h:h + 1]
    decay_out = lambda h: decay_out_all[:, h:h + 1]
    decay_in = lambda h: decay_in_all[:, h:h + 1]
    chunk_decay = lambda h: chunk_decay_all[:, h:h + 1]

    lane = lax.broadcasted_iota(jnp.int32, (L, LANES), 1)
    lo_half = lane < SSD_HEAD_DIM

    def per_pair(col_a, col_b):
        return jnp.where(lo_half, col_a, col_b)

    ys = []
    for pair in range(SSD_HEADS // 2):
        g = pair // 2
        h0, h1 = 2 * pair, 2 * pair + 1
        cg = cm[:, g * SSD_STATE:(g + 1) * SSD_STATE]
        bg = bm[:, g * SSD_STATE:(g + 1) * SSD_STATE]
        cb = _dot_nt(cg, bg)
        x_pair = xs[:, pair * LANES:(pair + 1) * LANES]
        xdt = x_pair * per_pair(dt(h0), dt(h1))
        y_pair = jnp.zeros((L, LANES), F32)
        for hh, keep in ((h0, lo_half), (h1, jnp.logical_not(lo_half))):
            seg = acum(hh) - acum_t[hh:hh + 1, :]
            lmat = jnp.where(causal, jnp.exp(jnp.where(causal, seg, 0.0)), 0.0)
            m = (cb * lmat).astype(BF16)
            y_pair = y_pair + _dot(m, jnp.where(keep, xdt, 0.0).astype(BF16))
        s_prev = state_ref[pair]
        y_off = _dot(cg, s_prev.astype(BF16)) * per_pair(decay_out(h0), decay_out(h1))
        w_in = (xdt * per_pair(decay_in(h0), decay_in(h1))).astype(BF16)
        cd = jnp.where(lane[0:1, :] < SSD_HEAD_DIM, chunk_decay(h0), chunk_decay(h1))
        state_ref[pair] = s_prev * cd + _dot_tn(bg, w_in)
        ys.append(y_pair + y_off + dskip_ref[:, pair * LANES:(pair + 1) * LANES] * x_pair)

    y = jnp.concatenate(ys, axis=-1)
    zz = z_ref[...]
    y = y * (zz * jax.nn.sigmoid(zz))
    gw = SSD_WIDTH // SSD_GROUPS
    outs = []
    for g in range(SSD_GROUPS):
        yg = y[:, g * gw:(g + 1) * gw]
        outs.append(yg * lax.rsqrt(jnp.mean(yg * yg, axis=-1, keepdims=True) + SSD_NORM_EPS))
    y_ref[...] = (jnp.concatenate(outs, axis=-1) * nw_ref[...]).astype(BF16)


def _attn_ssd_kernel(q_ref, k_ref, v_ref, lq1_ref, lk1_ref, lq2_ref, lk2_ref, sw_ref,
                     xbc_ref, dtr_ref, z_ref, cw_ref, cb_ref, dtb_ref, alog_ref, dskip_ref, nw_ref,
                     o_ref, y_ref, s_scr, vt_scr, xpad_ref, state_ref):
    bq, bk = ATT_BQ, ATT_BK
    qi = pl.program_id(2)
    n_maps = 2
    _ssd_chunk_init(pl.program_id(1) * pl.num_programs(2) + qi, xpad_ref, state_ref)

    @pl.when(qi == 0)
    def _():
        vt_scr[...] = v_ref[...].astype(F32).T.astype(BF16)

    def fold(t, reduce):
        return reduce(t.reshape(bk // SUBLANES, SUBLANES, bq), axis=0)

    def merge(old, new, op):
        return new if old is None else op(old, new)

    def attend(nk):
        _ssd_chunk(xbc_ref, dtr_ref, z_ref, cw_ref, cb_ref, dtb_ref, alog_ref, dskip_ref, nw_ref,
                   y_ref, xpad_ref, state_ref)
        q = q_ref[...]
        lane = lax.broadcasted_iota(jnp.int32, (bq, LANES), 1)
        zero = jnp.zeros_like(q)
        q_maps = (jnp.where(lane < ATT_HEAD_DIM, q, zero), jnp.where(lane >= ATT_HEAD_DIM, q, zero))
        key = lax.broadcasted_iota(jnp.int32, (bk, bq), 0)
        qry = lax.broadcasted_iota(jnp.int32, (bk, bq), 1)
        causal = key <= qry

        mt = [None] * n_maps
        for j in range(nk):
            kb = k_ref[j * bk:(j + 1) * bk, :]
            for m in range(n_maps):
                s = _dot_nt(kb, q_maps[m])
                if j == nk - 1:
                    s = jnp.where(causal, s, -jnp.inf)
                s_scr[m, j] = s
                mt[m] = merge(mt[m], fold(s, jnp.max), jnp.maximum)
        q_max = [jnp.max(t, axis=0, keepdims=True) for t in mt]

        lt = [None] * n_maps
        acc = [None] * n_maps
        for j in range(nk):
            vt = vt_scr[:, j * bk:(j + 1) * bk]
            for m in range(n_maps):
                p = jnp.exp2(s_scr[m, j] - q_max[m])
                lt[m] = merge(lt[m], fold(p, jnp.sum), jnp.add)
                acc[m] = merge(acc[m], _dot(vt, p.astype(BF16)), jnp.add)

        lam = (jnp.exp(jnp.sum(lq1_ref[...] * lk1_ref[...], axis=-1, keepdims=True))
               - jnp.exp(jnp.sum(lq2_ref[...] * lk2_ref[...], axis=-1, keepdims=True)) + LAM_INIT)
        l1 = jnp.sum(lt[0], axis=0, keepdims=True)
        l2 = jnp.sum(lt[1], axis=0, keepdims=True)
        o = acc[0] / l1 - lam * (acc[1] / l2)
        o = o * lax.rsqrt(jnp.mean(o * o, axis=0, keepdims=True) + SUBLN_EPS) * sw_ref[...]
        o_ref[...] = (o * (1.0 - LAM_INIT)).T.astype(BF16)

    for nk in range(1, k_ref.shape[0] // bk + 1):
        pl.when(qi == nk - 1)(lambda nk=nk: attend(nk))


def _attn_ssd(q, k, v, lam_q1, lam_k1, lam_q2, lam_k2, subln_w, xbc, dtr, z, conv_w, conv_b, dt_bias, a_log, d_skip,
              norm_w, bsz, seq):
    nq = seq // ATT_BQ
    nc = seq // SSD_CHUNK
    assert nc == ATT_HEADS * nq
    qspec = pl.BlockSpec((ATT_BQ, LANES), lambda b, h, i: (b * nq + i, h))
    kvspec = pl.BlockSpec((seq, LANES), lambda b, h, i: (b, h))
    full = lambda a: pl.BlockSpec(a.shape, lambda b, h, i: (0, 0))
    chunk = lambda n: pl.BlockSpec((SSD_CHUNK, n), lambda b, h, i: (b * nc + h * nq + i, 0))
    lams = [a.reshape(1, -1) for a in (lam_q1, lam_k1, lam_q2, lam_k2)]
    sw = subln_w.reshape(-1, 1)
    pad_h = lambda v: jnp.pad(v.reshape(1, SSD_HEADS), ((0, 0), (0, LANES - SSD_HEADS)))
    dskip_lanes = jnp.repeat(d_skip, SSD_HEAD_DIM).reshape(1, SSD_WIDTH)
    consts = (conv_w, conv_b.reshape(1, -1), pad_h(dt_bias), pad_h(a_log), dskip_lanes, norm_w.reshape(1, -1))
    return pl.pallas_call(
        _attn_ssd_kernel,
        grid=(bsz, ATT_HEADS, nq),
        in_specs=[qspec, kvspec, kvspec] + [full(a) for a in lams] + [full(sw)]
                 + [chunk(SSD_CONV_CH), chunk(LANES), chunk(SSD_WIDTH)] + [full(a) for a in consts],
        out_specs=[qspec, chunk(SSD_WIDTH)],
        out_shape=[jax.ShapeDtypeStruct((bsz * seq, ATT_WIDTH), BF16),
                   jax.ShapeDtypeStruct((bsz * seq, SSD_WIDTH), BF16)],
        scratch_shapes=[pltpu.VMEM((2, seq // ATT_BK, ATT_BK, ATT_BQ), F32),
                        pltpu.VMEM((LANES, seq), BF16),
                        pltpu.VMEM((SSD_CHUNK + 2 * SUBLANES, SSD_CONV_CH), F32),
                        pltpu.VMEM((SSD_HEADS // 2, SSD_STATE, LANES), F32)],
        compiler_params=_cparams(("arbitrary", "arbitrary", "arbitrary")),
        name="attn_ssd",
    )(q, k, v, *lams, sw, xbc, dtr, z, *consts)


def _mixer(x2, g_mix, w_in, conv_w, conv_b, dt_bias, a_log, d_skip, ssd_norm_w,
           lam_q1, lam_k1, lam_q2, lam_k2, subln_w, bsz, seq):
    z, xbc, dtr, q, k, v = _in_proj(x2, g_mix.reshape(1, -1), w_in)
    y_att, y_ssd = _attn_ssd(q, k, v, lam_q1, lam_k1, lam_q2, lam_k2, subln_w, xbc, dtr, z, conv_w, conv_b, dt_bias,
                             a_log, d_skip, ssd_norm_w, bsz, seq)
    return y_ssd, y_att


def _out_proj_kernel(x_ref, ys_ref, ya_ref, wo_ref, g_ref, wr2_ref, br_ref,
                     x1_ref, xn_ref, route_ref, gate_ref, cnt_ref):
    tm = TM_ROUTE
    x1 = x_ref[...] + _dot(jnp.concatenate([ys_ref[...], ya_ref[...]], axis=-1), wo_ref[...])
    x1_ref[...] = x1
    xn_all = _rms(x1, g_ref[...], NORM_EPS)
    xn_ref[...] = xn_all.astype(BF16)

    n_tok = OUT_SUB * tm
    xh, xm, _ = _split3(xn_all)
    lg2 = _dot(xh, wr2_ref[...]) + _dot(xm, wr2_ref[...])
    logits = (lg2[:, :LANES] + lg2[:, LANES:]).T[:N_EXPERTS, :] + br_ref[...]

    eidx = lax.broadcasted_iota(jnp.int32, (N_EXPERTS, n_tok), 0).astype(F32)
    work = logits
    vals, idxs, hots = [], [], []
    for _ in range(TOP_K):
        m = jnp.max(work, axis=0, keepdims=True)
        idx = jnp.min(jnp.where(work == m, eidx, float(N_EXPERTS)), axis=0, keepdims=True)
        hot = eidx == idx
        vals.append(m)
        idxs.append(idx.astype(jnp.int32))
        hots.append(hot)
        work = jnp.where(hot, -jnp.inf, work)
    exps = [jnp.exp(v - vals[0]) for v in vals]
    denom = exps[0] + exps[1] + exps[2] + exps[3]
    gates = [e / denom for e in exps]

    cnt = jnp.zeros((N_EXPERTS, n_tok), F32)
    for hot in hots:
        cnt = cnt + jnp.where(hot, 1.0, 0.0)
    cnt_b = cnt.astype(BF16)
    r = lax.broadcasted_iota(jnp.int32, (tm, tm), 0)
    c = lax.broadcasted_iota(jnp.int32, (tm, tm), 1)
    earlier_tok = jnp.where(r < c, 1.0, 0.0).astype(BF16)
    er = lax.broadcasted_iota(jnp.int32, (N_EXPERTS, N_EXPERTS), 0)
    ec = lax.broadcasted_iota(jnp.int32, (N_EXPERTS, N_EXPERTS), 1)
    lower_exp = jnp.where(ec < er, 1.0, 0.0).astype(BF16)
    below = _dot(lower_exp, cnt_b)
    cnt_pad = jnp.concatenate([cnt_b, jnp.zeros((LANES - N_EXPERTS, n_tok), BF16)], axis=0)
    pos = []
    for sub in range(OUT_SUB):
        cols = slice(sub * tm, (sub + 1) * tm)
        pos.append(_dot(cnt_b[:, cols], earlier_tok) + jnp.sum(below[:, cols], axis=1, keepdims=True))
        cnt_ref[sub] = _dot_nt(jnp.ones((SUBLANES, tm), BF16), cnt_pad[:, cols])
    pos = jnp.concatenate(pos, axis=1)
    lps = [jnp.sum(jnp.where(hot, pos, 0.0), axis=0, keepdims=True).astype(jnp.int32) for hot in hots]

    route_ref[...] = jnp.concatenate(idxs + lps, axis=0)
    gate_ref[...] = jnp.concatenate(gates + [jnp.zeros((SUBLANES - TOP_K, n_tok), F32)], axis=0)


def _out_proj(x2, y_ssd, y_att, w_out, g_ffn, w_router, b_router):
    t = x2.shape[0]
    tm = OUT_SUB * TM_ROUTE
    nt = t // TM_ROUTE
    wo = w_out.astype(BF16)
    wrh, wrm, _ = _split3(jnp.pad(w_router, ((0, 0), (0, LANES - N_EXPERTS))))
    wr2 = jnp.concatenate([wrh, wrm], axis=1)
    br = b_router.reshape(N_EXPERTS, 1)
    row = lambda n: pl.BlockSpec((tm, n), lambda i: (i, 0))
    col = pl.BlockSpec((SUBLANES, tm), lambda i: (0, i))
    full = lambda a: pl.BlockSpec(a.shape, lambda i: (0, 0))
    args = (x2, y_ssd, y_att, wo, g_ffn.reshape(1, -1), wr2, br)
    return pl.pallas_call(
        _out_proj_kernel,
        grid=(t // tm,),
        in_specs=[row(D_MODEL), row(SSD_WIDTH), row(ATT_WIDTH)] + [full(a) for a in args[3:]],
        out_specs=[row(D_MODEL), row(D_MODEL), col, col,
                   pl.BlockSpec((OUT_SUB, SUBLANES, LANES), lambda i: (i, 0, 0))],
        out_shape=[
            jax.ShapeDtypeStruct((t, D_MODEL), F32),
            jax.ShapeDtypeStruct((t, D_MODEL), BF16),
            jax.ShapeDtypeStruct((SUBLANES, t), jnp.int32),
            jax.ShapeDtypeStruct((SUBLANES, t), F32),
            jax.ShapeDtypeStruct((nt, SUBLANES, LANES), F32),
        ],
        compiler_params=_cparams(("arbitrary",)),
        name="out_proj",
    )(*args)


def _copy_run(n, start_copy):
    @pl.when(n > 0)
    def _():
        start_copy(n)


def _select_by_position(positions, values, n_pos):
    tm = positions[0].shape[1]
    r = lax.broadcasted_iota(jnp.int32, (n_pos, tm), 0)
    out = jnp.zeros((n_pos, tm), F32)
    for k in reversed(range(TOP_K)):
        out = jnp.where(r == positions[k], values[k], out)
    return out


def _dispatch_kernel(cnt_ref, ls_ref, base_ref, zoff_ref, xn_ref, route_ref, xs_hbm, xloc, zeros_vmem, sem_z, sems):
    tm = TM_ROUTE
    n_pos = TOP_K * tm
    i = pl.program_id(0)
    n_steps = pl.num_programs(0)
    slot = lax.rem(i, 2)

    def zero_fill(op):
        def pad_rows(e, _):
            n = zoff_ref[N_EXPERTS + 1 + e]

            @pl.when(n > 0)
            def _():
                op(pltpu.make_async_copy(zeros_vmem.at[_row_slice(0, n)],
                                         xs_hbm.at[_row_slice(zoff_ref[e], n)], sem_z))
            return 0

        def unused_block(b, _):
            op(pltpu.make_async_copy(zeros_vmem, xs_hbm.at[_row_slice(b * ROW_BLK, ROW_BLK)], sem_z))
            return 0

        lax.fori_loop(0, N_EXPERTS, pad_rows, 0)
        lax.fori_loop(zoff_ref[N_EXPERTS], xs_hbm.shape[0] // (ROW_BLK * ROW_SUB), unused_block, 0)

    @pl.when(i == 0)
    def _():
        zeros_vmem[...] = jnp.zeros_like(zeros_vmem)
        zero_fill(lambda cp: cp.start())

    def slot_wait(s):
        pltpu.make_async_copy(xloc.at[s], xs_hbm.at[_row_slice(0, n_pos)], sems.at[s]).wait()

    @pl.when(i >= 2)
    def _():
        slot_wait(slot)

    positions = [route_ref[TOP_K + k:TOP_K + k + 1, :] for k in range(TOP_K)]
    sel = _select_by_position(positions, [1.0] * TOP_K, n_pos).astype(BF16)
    rows = _dot(sel, xn_ref[...])
    _store_rows(xloc.at[slot], rows)

    def per_expert(e, _):
        idx = i * N_EXPERTS + e
        src0 = ls_ref[idx]
        dst0 = base_ref[idx]

        def start_copy(n):
            pltpu.make_async_copy(xloc.at[slot, _row_slice(src0, n)],
                                  xs_hbm.at[_row_slice(dst0, n)], sems.at[slot]).start()

        _copy_run(cnt_ref[idx], start_copy)
        return 0

    lax.fori_loop(0, N_EXPERTS, per_expert, 0)

    @pl.when(i == n_steps - 1)
    def _():
        slot_wait(slot)

        @pl.when(n_steps > 1)
        def _():
            slot_wait(1 - slot)

        zero_fill(lambda cp: cp.wait())


def _dispatch(xn, route, tables, n_rows):
    t = xn.shape[0]
    tm = TM_ROUTE
    cnt_tbl, ls_tbl, base_tbl, zoff = tables
    return pl.pallas_call(
        _dispatch_kernel,
        grid_spec=pltpu.PrefetchScalarGridSpec(
            num_scalar_prefetch=4,
            grid=(t // tm,),
            in_specs=[pl.BlockSpec((tm, D_MODEL), lambda i, *_: (i, 0)),
                      pl.BlockSpec((SUBLANES, tm), lambda i, *_: (0, i))],
            out_specs=pl.BlockSpec(memory_space=pl.ANY),
            scratch_shapes=[
                pltpu.VMEM((2, TOP_K * tm * ROW_SUB, LANES), F32),
                pltpu.VMEM((ROW_BLK * ROW_SUB, LANES), F32),
                pltpu.SemaphoreType.DMA,
                pltpu.SemaphoreType.DMA((2,)),
            ],
        ),
        out_shape=jax.ShapeDtypeStruct(((n_rows + ROW_BLK) * ROW_SUB, LANES), F32),
        compiler_params=_cparams(("arbitrary",)),
        name="dispatch",
    )(cnt_tbl, ls_tbl, base_tbl, zoff, xn, route)


def _experts_kernel(blk_e_ref, nvalid_ref, first_ref, wslot_ref, enext_ref, pieces_ref, xs_ref, wup_hbm, wdn_hbm,
                    bg_ref, bu_ref, bd_ref, ys_ref, wup_buf, wdn_buf, wg_s, wu_s, wd_s, sems):
    i = pl.program_id(0)
    slot = wslot_ref[i]

    def weight_copies(e, s):
        return (pltpu.make_async_copy(wup_hbm.at[e], wup_buf.at[s], sems.at[0, s]),
                pltpu.make_async_copy(wdn_hbm.at[e], wdn_buf.at[s], sems.at[1, s]))

    @pl.when(i == 0)
    def _():
        for cp in weight_copies(blk_e_ref[0], slot):
            cp.start()

    @pl.when(first_ref[i] != 0)
    def _():
        for cp in weight_copies(blk_e_ref[i], slot):
            cp.wait()

        @pl.when(enext_ref[i] >= 0)
        def _():
            for cp in weight_copies(enext_ref[i], 1 - slot):
                cp.start()

        src = lax.broadcasted_iota(jnp.int32, (DEINT, DEINT), 0)
        dst = lax.broadcasted_iota(jnp.int32, (DEINT, DEINT), 1)
        perm = jnp.where(src == jnp.where(dst < LANES, 2 * dst, 2 * (dst - LANES) + 1), 1.0, 0.0).astype(BF16)
        for g in range(2 * D_EXPERT // DEINT):
            sep = _dot(wup_buf[slot, :, g * DEINT:(g + 1) * DEINT].astype(BF16), perm)
            wg_s[:, g * LANES:(g + 1) * LANES] = sep[:, :LANES].astype(BF16)
            wu_s[:, g * LANES:(g + 1) * LANES] = sep[:, LANES:].astype(BF16)
        wd_s[...] = wdn_buf[slot].astype(BF16)

    def ffn(n_rows):
        used = n_rows * ROW_SUB
        if n_rows:
            xb = _load_rows(xs_ref.at[pl.ds(0, used)]).astype(BF16)
            gate = jnp.minimum(_dot(xb, wg_s[...]) + bg_ref[0], SWIGLU_LIMIT)
            up = jnp.clip(_dot(xb, wu_s[...]) + bu_ref[0], -SWIGLU_LIMIT, SWIGLU_LIMIT)
            act = (up + 1.0) * gate * jax.nn.sigmoid(SWIGLU_ALPHA * gate)
            _store_rows(ys_ref.at[pl.ds(0, used)], _dot(act.astype(BF16), wd_s[...]) + bd_ref[0])
        if n_rows < ROW_BLK:
            ys_ref[used:, :] = jnp.zeros((ROW_BLK * ROW_SUB - used, LANES), F32)

    for pieces in range(ROW_BLK // ROW_PIECE + 1):
        pl.when(pieces_ref[i] == pieces)(lambda pieces=pieces: ffn(pieces * ROW_PIECE))


def _experts(xs, plan, w_up, b_up, w_down, b_down, n_rows):
    nb = n_rows // ROW_BLK
    bg = b_up[:, 0::2].reshape(N_EXPERTS, 1, D_EXPERT)
    bu = b_up[:, 1::2].reshape(N_EXPERTS, 1, D_EXPERT)
    bd = b_down.reshape(N_EXPERTS, 1, D_MODEL)
    blk_e, nvalid, first, wslot, enext, pieces = plan
    src = lambda i, be, nv, *_: (jnp.minimum(i, nv[0] - 1), 0)
    bspec = lambda m: pl.BlockSpec((1, 1, m), lambda i, be, *_: (be[i], 0, 0))
    anyspec = pl.BlockSpec(memory_space=pl.ANY)
    return pl.pallas_call(
        _experts_kernel,
        grid_spec=pltpu.PrefetchScalarGridSpec(
            num_scalar_prefetch=6,
            grid=(nb,),
            in_specs=[_rows_spec(ROW_BLK, src), anyspec, anyspec,
                      bspec(D_EXPERT), bspec(D_EXPERT), bspec(D_MODEL)],
            out_specs=_rows_spec(ROW_BLK, lambda i, *_: (i, 0)),
            scratch_shapes=[
                pltpu.VMEM((2, D_MODEL, 2 * D_EXPERT), F32),
                pltpu.VMEM((2, D_EXPERT, D_MODEL), F32),
                pltpu.VMEM((D_MODEL, D_EXPERT), BF16),
                pltpu.VMEM((D_MODEL, D_EXPERT), BF16),
                pltpu.VMEM((D_EXPERT, D_MODEL), BF16),
                pltpu.SemaphoreType.DMA((2, 2)),
            ],
        ),
        out_shape=jax.ShapeDtypeStruct((n_rows * ROW_SUB, LANES), F32),
        compiler_params=_cparams(("arbitrary",)),
        name="experts",
    )(blk_e, nvalid, first, wslot, enext, pieces, xs, w_up, w_down, bg, bu, bd)


def _combine_kernel(cnt_ref, ls_ref, base_ref, ys_hbm, route_ref, gate_ref, x1_ref, p_ref, gp_ref, wpg_ref,
                    wpp_ref, gf_ref, o_ref, yloc, sems):
    tm = TM_ROUTE
    n_pos = TOP_K * tm
    i = pl.program_id(0)
    n_steps = pl.num_programs(0)
    slot = lax.rem(i, 2)

    def gather_step(step, s):
        for u in range(COMB_SUB):
            def per_expert(e, _, u=u):
                idx = (step * COMB_SUB + u) * N_EXPERTS + e
                src0 = base_ref[idx]
                dst0 = ls_ref[idx]

                def start_copy(n):
                    pltpu.make_async_copy(ys_hbm.at[_row_slice(src0, n)],
                                          yloc.at[s, u, _row_slice(dst0, n)], sems.at[s]).start()

                _copy_run(cnt_ref[idx], start_copy)
                return 0

            lax.fori_loop(0, N_EXPERTS, per_expert, 0)

    @pl.when(i == 0)
    def _():
        gather_step(0, 0)

    @pl.when(i + 1 < n_steps)
    def _():
        gather_step(i + 1, 1 - slot)

    for u in range(COMB_SUB):
        pltpu.make_async_copy(ys_hbm.at[_row_slice(0, n_pos)], yloc.at[slot, u], sems.at[slot]).wait()

    moe = []
    for u in range(COMB_SUB):
        cols = slice(u * tm, (u + 1) * tm)
        positions = [route_ref[TOP_K + k:TOP_K + k + 1, cols] for k in range(TOP_K)]
        sel = _select_by_position(positions, [1.0] * TOP_K, n_pos).astype(BF16)
        gsel = _select_by_position(positions, [gate_ref[k:k + 1, cols] for k in range(TOP_K)], n_pos)
        g_pos = jnp.sum(gsel, axis=1, keepdims=True)
        y_gated = (_load_rows(yloc.at[slot, u]) * g_pos).astype(BF16)
        moe.append(_dot_tn(sel, y_gated))
    x2 = x1_ref[...] + jnp.concatenate(moe, axis=0)
    xn = _rms(x2, gp_ref[...], NORM_EPS).astype(BF16)
    pp = _dot(p_ref[...].astype(BF16), wpp_ref[...])
    x3 = x2 + pp * jax.nn.sigmoid(_dot(xn, wpg_ref[...]))
    o_ref[...] = _rms(x3, gf_ref[...], NORM_EPS)


def _combine(ys, route, gate_t, x1, p2, tables, g_ple, w_ple_gate, w_ple_proj, g_final):
    t = x1.shape[0]
    tm = COMB_SUB * TM_ROUTE
    cnt_tbl, ls_tbl, base_tbl, _ = tables
    row = lambda n: pl.BlockSpec((tm, n), lambda i, *_: (i, 0))
    col = pl.BlockSpec((SUBLANES, tm), lambda i, *_: (0, i))
    full = lambda a: pl.BlockSpec(a.shape, lambda i, *_: (0, 0))
    consts = (g_ple.reshape(1, -1), w_ple_gate.astype(BF16), w_ple_proj.astype(BF16), g_final.reshape(1, -1))
    return pl.pallas_call(
        _combine_kernel,
        grid_spec=pltpu.PrefetchScalarGridSpec(
            num_scalar_prefetch=3,
            grid=(t // tm,),
            in_specs=[pl.BlockSpec(memory_space=pl.ANY), col, col, row(D_MODEL), row(PLE_DIM)]
                     + [full(a) for a in consts],
            out_specs=row(D_MODEL),
            scratch_shapes=[
                pltpu.VMEM((2, COMB_SUB, TOP_K * TM_ROUTE * ROW_SUB, LANES), F32),
                pltpu.SemaphoreType.DMA((2,)),
            ],
        ),
        out_shape=jax.ShapeDtypeStruct((t, D_MODEL), F32),
        compiler_params=_cparams(("arbitrary",)),
        name="combine",
    )(cnt_tbl, ls_tbl, base_tbl, ys, route, gate_t, x1, p2, *consts)


def _routing_tables(cnt, n_blocks):
    tile_cnt = cnt[:, 0, :N_EXPERTS].astype(jnp.int32)
    counts = jnp.sum(tile_cnt, axis=0)
    padded = (counts + ROW_BLK - 1) // ROW_BLK * ROW_BLK
    pend = jnp.cumsum(padded)
    pstart = pend - padded
    base = pstart[None, :] + jnp.cumsum(tile_cnt, axis=0) - tile_cnt
    lstart = jnp.cumsum(tile_cnt, axis=1) - tile_cnt
    nvalid = pend[-1:] // ROW_BLK
    zoff = jnp.concatenate([pstart + counts, nvalid, padded - counts])
    blk_start = jnp.minimum(jnp.arange(n_blocks, dtype=jnp.int32) * ROW_BLK, pend[-1] - 1)
    blk_e = jnp.minimum(jnp.sum((pend[None, :] <= blk_start[:, None]).astype(jnp.int32), axis=1), N_EXPERTS - 1)
    first = jnp.concatenate([jnp.ones((1,), bool), blk_e[1:] != blk_e[:-1]])
    wslot = (jnp.cumsum(first.astype(jnp.int32)) - 1) % 2
    eids = jnp.arange(N_EXPERTS, dtype=jnp.int32)
    later_nonempty = (eids[None, :] > eids[:, None]) & (padded[None, :] > 0)
    next_e = jnp.min(jnp.where(later_nonempty, eids[None, :], N_EXPERTS), axis=1)
    next_e = jnp.where(next_e == N_EXPERTS, -1, next_e)
    blk_hot = blk_e[:, None] == eids[None, :]
    per_block = lambda v: jnp.sum(jnp.where(blk_hot, v[None, :], 0), axis=1)
    blk_ids = jnp.arange(n_blocks, dtype=jnp.int32)
    real_rows = jnp.clip(per_block(pstart + counts) - blk_ids * ROW_BLK, 0, ROW_BLK) * (blk_ids < nvalid[0])
    pieces = (real_rows + ROW_PIECE - 1) // ROW_PIECE
    i32 = lambda a: a.reshape(-1).astype(jnp.int32)
    plan = (i32(blk_e), i32(nvalid), i32(first), i32(wslot), i32(per_block(next_e)), i32(pieces))
    return (i32(tile_cnt), i32(lstart), i32(base), i32(zoff)), plan


def kernel(x, p, g_mix, w_in, conv_w, conv_b, dt_bias, a_log, d_skip, ssd_norm_w, lam_q1, lam_k1, lam_q2, lam_k2, subln_w, w_out, g_ffn, w_router, b_router, w_up, b_up, w_down, b_down, g_ple, w_ple_gate, w_ple_proj, g_final):
    bsz, seq, d = x.shape
    t = bsz * seq
    x2 = x.reshape(t, d)
    y_ssd, y_att = _mixer(x2, g_mix[0], w_in[0], conv_w[0], conv_b[0], dt_bias[0], a_log[0], d_skip[0], ssd_norm_w[0],
                          lam_q1[0], lam_k1[0], lam_q2[0], lam_k2[0], subln_w[0], bsz, seq)
    x1, xn, route, gate_t, cnt = _out_proj(x2, y_ssd, y_att, w_out[0], g_ffn[0], w_router[0], b_router[0])

    n_rows = t * TOP_K + N_EXPERTS * ROW_BLK
    tables, plan = _routing_tables(cnt, n_rows // ROW_BLK)
    xs = _dispatch(xn, route, tables, n_rows)
    ys = _experts(xs, plan, w_up[0], b_up[0], w_down[0], b_down[0], n_rows)
    out = _combine(ys, route, gate_t, x1, p[0].reshape(t, PLE_DIM), tables, g_ple[0], w_ple_gate[0], w_ple_proj[0],
                   g_final)
    return out.reshape(bsz, seq, d)
```

```python
import math

import jax
import jax.numpy as jnp
from jax import lax
from jax.experimental import pallas as pl
from jax.experimental.pallas import tpu as pltpu

F32 = jnp.float32
BF16 = jnp.bfloat16

D_MODEL = 1024
PLE_DIM = 256
SSD_WIDTH = 512
ATT_WIDTH = 512
SSD_HEAD_DIM = 64
SSD_HEADS = 8
SSD_GROUPS = 2
SSD_STATE = 128
SSD_CONV = 4
SSD_CHUNK = 128
SSD_CONV_CH = SSD_WIDTH + 2 * SSD_GROUPS * SSD_STATE
SSD_NORM_EPS = 1e-5
ATT_HEAD_DIM = 64
ATT_HEADS = 4
SUBLN_EPS = 1e-5
OFF_Z = 0
OFF_XBC = OFF_Z + SSD_WIDTH
OFF_DT = OFF_XBC + SSD_CONV_CH
OFF_Q = OFF_DT + SSD_HEADS
OFF_K = OFF_Q + ATT_WIDTH
OFF_V = OFF_K + ATT_WIDTH
IN_PROJ = OFF_V + ATT_WIDTH
N_EXPERTS = 32
TOP_K = 4
D_EXPERT = 1024
SWIGLU_LIMIT = 7.0
SWIGLU_ALPHA = 1.702
NORM_EPS = 1e-6
LAM_INIT = 0.8 - 0.6 * math.exp(-0.3 * 0)

LANES = 128
SUBLANES = 8
VMEM_LIMIT_BYTES = 56 * 1024 * 1024
ROW_SUB = D_MODEL // LANES
PACK_SUB = ROW_SUB // 2

TM_PROJ = 512
ATT_BQ = 512
ATT_BK = 512
ROW_BLK = 512
ROW_PIECE = 256
TM_ROUTE = 256
N_POS = TOP_K * TM_ROUTE + N_EXPERTS
OUT_SUB = 4
COMB_SUB = 2
DEINT = 2 * LANES


def _cparams(sem):
    return pltpu.CompilerParams(dimension_semantics=sem, vmem_limit_bytes=VMEM_LIMIT_BYTES)


def _rms(x, w, eps):
    return x * lax.rsqrt(jnp.mean(x * x, axis=-1, keepdims=True) + eps) * w


def _dot(a, b):
    return jnp.dot(a, b, preferred_element_type=F32)


def _dot_nt(a, b):
    return lax.dot_general(a, b, (((1,), (1,)), ((), ())), preferred_element_type=F32)


def _dot_tn(a, b):
    return lax.dot_general(a, b, (((0,), (0,)), ((), ())), preferred_element_type=F32)


def _store_rows(ref, val):
    n = val.shape[0]
    for s in range(ROW_SUB):
        ref[pl.ds(s, n, stride=ROW_SUB), :] = val[:, s * LANES:(s + 1) * LANES]


def _load_rows(ref):
    n = ref.shape[0] // ROW_SUB
    return jnp.concatenate([ref[pl.ds(s, n, stride=ROW_SUB), :] for s in range(ROW_SUB)], axis=-1)


def _rows_spec(n, index_map):
    return pl.BlockSpec((n * ROW_SUB, LANES), index_map)


def _row_slice(start, n, sub=ROW_SUB):
    return pl.ds(pl.multiple_of(start * sub, SUBLANES), n * sub)


def _pack_rows(ref, val):
    n = val.shape[0]
    bits = lambda v: lax.bitcast_convert_type(v, jnp.uint32)
    words = (bits(val[:, :D_MODEL // 2]) & jnp.uint32(0xFFFF0000)) | (bits(val[:, D_MODEL // 2:]) >> 16)
    for s in range(PACK_SUB):
        ref[pl.ds(s, n, stride=PACK_SUB), :] = words[:, s * LANES:(s + 1) * LANES]


def _unpack_rows(ref):
    n = ref.shape[0] // PACK_SUB
    words = [ref[pl.ds(s, n, stride=PACK_SUB), :] for s in range(PACK_SUB)]
    as_f32 = lambda w: lax.bitcast_convert_type(w, F32)
    hi = [as_f32(w & jnp.uint32(0xFFFF0000)) for w in words]
    lo = [as_f32(w << 16) for w in words]
    return jnp.concatenate(hi + lo, axis=-1).astype(BF16)


def _split3(x):
    hi = x.astype(BF16)
    r1 = x - hi.astype(F32)
    mid = r1.astype(BF16)
    lo = (r1 - mid.astype(F32)).astype(BF16)
    return hi, mid, lo


def _in_proj_kernel(x_ref, g_ref, wz_ref, wxbc_ref, wdt_ref, wq_ref, wk_ref, wv_ref,
                    z_ref, xbc_ref, dt_ref, q_ref, k_ref, v_ref):
    h = _rms(x_ref[...], g_ref[...], NORM_EPS).astype(BF16)
    z_ref[...] = _dot(h, wz_ref[...])
    xbc_ref[...] = _dot(h, wxbc_ref[...])
    dt_ref[...] = _dot(h, wdt_ref[...])
    q_ref[...] = (_dot(h, wq_ref[...]) * (ATT_HEAD_DIM ** -0.5 * math.log2(math.e))).astype(BF16)
    k_ref[...] = _dot(h, wk_ref[...]).astype(BF16)
    v_ref[...] = _dot(h, wv_ref[...]).astype(BF16)


def _in_proj(x2, g_mix, w_in):
    t = x2.shape[0]
    wb = w_in.astype(BF16)
    wz = wb[:, OFF_Z:OFF_XBC]
    wxbc = wb[:, OFF_XBC:OFF_DT]
    wdt = jnp.pad(wb[:, OFF_DT:OFF_Q], ((0, 0), (0, LANES - SSD_HEADS)))
    wq = wb[:, OFF_Q:OFF_K]
    wk = wb[:, OFF_K:OFF_V]
    wv = wb[:, OFF_V:IN_PROJ]
    tm = TM_PROJ
    row = lambda n: pl.BlockSpec((tm, n), lambda i: (i, 0))
    full = lambda a: pl.BlockSpec(a.shape, lambda i: (0, 0))
    return pl.pallas_call(
        _in_proj_kernel,
        grid=(t // tm,),
        in_specs=[row(D_MODEL), full(g_mix), full(wz), full(wxbc), full(wdt), full(wq), full(wk), full(wv)],
        out_specs=[row(SSD_WIDTH), row(SSD_CONV_CH), row(LANES), row(ATT_WIDTH), row(ATT_WIDTH), row(ATT_WIDTH)],
        out_shape=[
            jax.ShapeDtypeStruct((t, SSD_WIDTH), F32),
            jax.ShapeDtypeStruct((t, SSD_CONV_CH), F32),
            jax.ShapeDtypeStruct((t, LANES), F32),
            jax.ShapeDtypeStruct((t, ATT_WIDTH), BF16),
            jax.ShapeDtypeStruct((t, ATT_WIDTH), BF16),
            jax.ShapeDtypeStruct((t, ATT_WIDTH), BF16),
        ],
        compiler_params=_cparams(("arbitrary",)),
        name="in_proj",
    )(x2, g_mix, wz, wxbc, wdt, wq, wk, wv)


def _ssd_chunk_init(c, xpad_ref, state_ref):
    L = SSD_CHUNK

    @pl.when(c == 0)
    def _():
        xpad_ref[0:SUBLANES, :] = jnp.zeros((SUBLANES, SSD_CONV_CH), F32)
        state_ref[...] = jnp.zeros_like(state_ref)

    @pl.when(c != 0)
    def _():
        xpad_ref[0:SUBLANES, :] = xpad_ref[L:L + SUBLANES, :]


def _ssd_chunk(xbc_ref, dtr_ref, z_ref, cw_ref, cb_ref, dtb_ref, alog_ref, dskip_ref, nw_ref,
               y_ref, xpad_ref, state_ref):
    L = SSD_CHUNK
    xpad_ref[SUBLANES:SUBLANES + L, :] = xbc_ref[...]

    conv = cb_ref[...]
    for j in range(SSD_CONV):
        off = SUBLANES - (SSD_CONV - 1) + j
        conv = conv + cw_ref[j:j + 1, :] * xpad_ref[off:off + L, :]
    act = conv * jax.nn.sigmoid(conv)
    xs = act[:, :SSD_WIDTH]
    bm = act[:, SSD_WIDTH:SSD_WIDTH + SSD_GROUPS * SSD_STATE].astype(BF16)
    cm = act[:, SSD_WIDTH + SSD_GROUPS * SSD_STATE:].astype(BF16)

    dt_in = dtr_ref[...] + dtb_ref[...]
    dt_all = jnp.maximum(dt_in, 0.0) + jnp.log1p(jnp.exp(-jnp.abs(dt_in)))
    adt = dt_all * (-jnp.exp(alog_ref[...]))

    ri = lax.broadcasted_iota(jnp.int32, (L, L), 0)
    ci = lax.broadcasted_iota(jnp.int32, (L, L), 1)
    causal = ci <= ri
    tril = jnp.where(causal, 1.0, 0.0).astype(BF16)
    hi, mid, lo = _split3(adt)
    acum_all = _dot(tril, hi) + _dot(tril, mid) + _dot(tril, lo)
    acum_t = acum_all.T
    a_last = acum_all[L - 1:L, :]
    decay_in_all = jnp.exp(a_last - acum_all)
    decay_out_all = jnp.exp(acum_all)
    chunk_decay_all = jnp.exp(a_last)
    acum = lambda h: acum_all[:, h:h + 1]
    dt = lambda h: dt_all[:, h:h + 1]
    decay_out = lambda h: decay_out_all[:, h:h + 1]
    decay_in = lambda h: decay_in_all[:, h:h + 1]
    chunk_decay = lambda h: chunk_decay_all[:, h:h + 1]

    lane = lax.broadcasted_iota(jnp.int32, (L, LANES), 1)
    lo_half = lane < SSD_HEAD_DIM

    def per_pair(col_a, col_b):
        return jnp.where(lo_half, col_a, col_b)

    ys = []
    for pair in range(SSD_HEADS // 2):
        g = pair // 2
        h0, h1 = 2 * pair, 2 * pair + 1
        cg = cm[:, g * SSD_STATE:(g + 1) * SSD_STATE]
        bg = bm[:, g * SSD_STATE:(g + 1) * SSD_STATE]
        cb = _dot_nt(cg, bg)
        x_pair = xs[:, pair * LANES:(pair + 1) * LANES]
        xdt = x_pair * per_pair(dt(h0), dt(h1))
        y_pair = jnp.zeros((L, LANES), F32)
        for hh, keep in ((h0, lo_half), (h1, jnp.logical_not(lo_half))):
            seg = acum(hh) - acum_t[hh:hh + 1, :]
            lmat = jnp.where(causal, jnp.exp(jnp.where(causal, seg, 0.0)), 0.0)
            m = (cb * lmat).astype(BF16)
            y_pair = y_pair + _dot(m, jnp.where(keep, xdt, 0.0).astype(BF16))
        s_prev = state_ref[pair]
        y_off = _dot(cg, s_prev.astype(BF16)) * per_pair(decay_out(h0), decay_out(h1))
        w_in = (xdt * per_pair(decay_in(h0), decay_in(h1))).astype(BF16)
        cd = jnp.where(lane[0:1, :] < SSD_HEAD_DIM, chunk_decay(h0), chunk_decay(h1))
        state_ref[pair] = s_prev * cd + _dot_tn(bg, w_in)
        ys.append(y_pair + y_off + dskip_ref[:, pair * LANES:(pair + 1) * LANES] * x_pair)

    y = jnp.concatenate(ys, axis=-1)
    zz = z_ref[...]
    y = y * (zz * jax.nn.sigmoid(zz))
    gw = SSD_WIDTH // SSD_GROUPS
    outs = []
    for g in range(SSD_GROUPS):
        yg = y[:, g * gw:(g + 1) * gw]
        outs.append(yg * lax.rsqrt(jnp.mean(yg * yg, axis=-1, keepdims=True) + SSD_NORM_EPS))
    y_ref[...] = (jnp.concatenate(outs, axis=-1) * nw_ref[...]).astype(BF16)


def _attn_ssd_kernel(q_ref, k_ref, v_ref, lq1_ref, lk1_ref, lq2_ref, lk2_ref, sw_ref,
                     xbc_ref, dtr_ref, z_ref, cw_ref, cb_ref, dtb_ref, alog_ref, dskip_ref, nw_ref,
                     o_ref, y_ref, s_scr, vt_scr, xpad_ref, state_ref):
    bq, bk = ATT_BQ, ATT_BK
    qi = pl.program_id(2)
    n_maps = 2
    _ssd_chunk_init(pl.program_id(1) * pl.num_programs(2) + qi, xpad_ref, state_ref)

    @pl.when(qi == 0)
    def _():
        vt_scr[...] = v_ref[...].astype(F32).T.astype(BF16)

    def fold(t, reduce):
        return reduce(t.reshape(bk // SUBLANES, SUBLANES, bq), axis=0)

    def merge(old, new, op):
        return new if old is None else op(old, new)

    def attend(nk):
        _ssd_chunk(xbc_ref, dtr_ref, z_ref, cw_ref, cb_ref, dtb_ref, alog_ref, dskip_ref, nw_ref,
                   y_ref, xpad_ref, state_ref)
        q = q_ref[...]
        lane = lax.broadcasted_iota(jnp.int32, (bq, LANES), 1)
        zero = jnp.zeros_like(q)
        q_maps = (jnp.where(lane < ATT_HEAD_DIM, q, zero), jnp.where(lane >= ATT_HEAD_DIM, q, zero))
        key = lax.broadcasted_iota(jnp.int32, (bk, bq), 0)
        qry = lax.broadcasted_iota(jnp.int32, (bk, bq), 1)
        causal = key <= qry

        mt = [None] * n_maps
        for j in range(nk):
            kb = k_ref[j * bk:(j + 1) * bk, :]
            for m in range(n_maps):
                s = _dot_nt(kb, q_maps[m])
                if j == nk - 1:
                    s = jnp.where(causal, s, -jnp.inf)
                s_scr[m, j] = s
                mt[m] = merge(mt[m], fold(s, jnp.max), jnp.maximum)
        q_max = [jnp.max(t, axis=0, keepdims=True) for t in mt]

        lt = [None] * n_maps
        acc = [None] * n_maps
        for j in range(nk):
            vt = vt_scr[:, j * bk:(j + 1) * bk]
            for m in range(n_maps):
                p = jnp.exp2(s_scr[m, j] - q_max[m])
                lt[m] = merge(lt[m], fold(p, jnp.sum), jnp.add)
                acc[m] = merge(acc[m], _dot(vt, p.astype(BF16)), jnp.add)

        lam = (jnp.exp(jnp.sum(lq1_ref[...] * lk1_ref[...], axis=-1, keepdims=True))
               - jnp.exp(jnp.sum(lq2_ref[...] * lk2_ref[...], axis=-1, keepdims=True)) + LAM_INIT)
        l1 = jnp.sum(lt[0], axis=0, keepdims=True)
        l2 = jnp.sum(lt[1], axis=0, keepdims=True)
        o = acc[0] / l1 - lam * (acc[1] / l2)
        o = o * lax.rsqrt(jnp.mean(o * o, axis=0, keepdims=True) + SUBLN_EPS) * sw_ref[...]
        o_ref[...] = (o * (1.0 - LAM_INIT)).T.astype(BF16)

    for nk in range(1, k_ref.shape[0] // bk + 1):
        pl.when(qi == nk - 1)(lambda nk=nk: attend(nk))


def _attn_ssd(q, k, v, lam_q1, lam_k1, lam_q2, lam_k2, subln_w, xbc, dtr, z, conv_w, conv_b, dt_bias, a_log, d_skip,
              norm_w, bsz, seq):
    nq = seq // ATT_BQ
    nc = seq // SSD_CHUNK
    assert nc == ATT_HEADS * nq
    qspec = pl.BlockSpec((ATT_BQ, LANES), lambda b, h, i: (b * nq + i, h))
    kvspec = pl.BlockSpec((seq, LANES), lambda b, h, i: (b, h))
    full = lambda a: pl.BlockSpec(a.shape, lambda b, h, i: (0, 0))
    chunk = lambda n: pl.BlockSpec((SSD_CHUNK, n), lambda b, h, i: (b * nc + h * nq + i, 0))
    lams = [a.reshape(1, -1) for a in (lam_q1, lam_k1, lam_q2, lam_k2)]
    sw = subln_w.reshape(-1, 1)
    pad_h = lambda v: jnp.pad(v.reshape(1, SSD_HEADS), ((0, 0), (0, LANES - SSD_HEADS)))
    dskip_lanes = jnp.repeat(d_skip, SSD_HEAD_DIM).reshape(1, SSD_WIDTH)
    consts = (conv_w, conv_b.reshape(1, -1), pad_h(dt_bias), pad_h(a_log), dskip_lanes, norm_w.reshape(1, -1))
    return pl.pallas_call(
        _attn_ssd_kernel,
        grid=(bsz, ATT_HEADS, nq),
        in_specs=[qspec, kvspec, kvspec] + [full(a) for a in lams] + [full(sw)]
                 + [chunk(SSD_CONV_CH), chunk(LANES), chunk(SSD_WIDTH)] + [full(a) for a in consts],
        out_specs=[qspec, chunk(SSD_WIDTH)],
        out_shape=[jax.ShapeDtypeStruct((bsz * seq, ATT_WIDTH), BF16),
                   jax.ShapeDtypeStruct((bsz * seq, SSD_WIDTH), BF16)],
        scratch_shapes=[pltpu.VMEM((2, seq // ATT_BK, ATT_BK, ATT_BQ), F32),
                        pltpu.VMEM((LANES, seq), BF16),
                        pltpu.VMEM((SSD_CHUNK + 2 * SUBLANES, SSD_CONV_CH), F32),
                        pltpu.VMEM((SSD_HEADS // 2, SSD_STATE, LANES), F32)],
        compiler_params=_cparams(("arbitrary", "arbitrary", "arbitrary")),
        name="attn_ssd",
    )(q, k, v, *lams, sw, xbc, dtr, z, *consts)


def _mixer(x2, g_mix, w_in, conv_w, conv_b, dt_bias, a_log, d_skip, ssd_norm_w,
           lam_q1, lam_k1, lam_q2, lam_k2, subln_w, bsz, seq):
    z, xbc, dtr, q, k, v = _in_proj(x2, g_mix.reshape(1, -1), w_in)
    y_att, y_ssd = _attn_ssd(q, k, v, lam_q1, lam_k1, lam_q2, lam_k2, subln_w, xbc, dtr, z, conv_w, conv_b, dt_bias,
                             a_log, d_skip, ssd_norm_w, bsz, seq)
    return y_ssd, y_att


def _out_proj_kernel(x_ref, ys_ref, ya_ref, wo_ref, g_ref, wr2_ref, br_ref,
                     x1_ref, xn_ref, route_ref, gate_ref, cnt_ref):
    tm = TM_ROUTE
    x1 = x_ref[...] + _dot(jnp.concatenate([ys_ref[...], ya_ref[...]], axis=-1), wo_ref[...])
    x1_ref[...] = x1
    xn_all = _rms(x1, g_ref[...], NORM_EPS)
    xn_ref[...] = xn_all.astype(BF16)

    n_tok = OUT_SUB * tm
    xh, xm, _ = _split3(xn_all)
    lg2 = _dot(xh, wr2_ref[...]) + _dot(xm, wr2_ref[...])
    logits = (lg2[:, :LANES] + lg2[:, LANES:]).T[:N_EXPERTS, :] + br_ref[...]

    eidx = lax.broadcasted_iota(jnp.int32, (N_EXPERTS, n_tok), 0).astype(F32)
    work = logits
    vals, idxs, hots = [], [], []
    for _ in range(TOP_K):
        m = jnp.max(work, axis=0, keepdims=True)
        idx = jnp.min(jnp.where(work == m, eidx, float(N_EXPERTS)), axis=0, keepdims=True)
        hot = eidx == idx
        vals.append(m)
        idxs.append(idx.astype(jnp.int32))
        hots.append(hot)
        work = jnp.where(hot, -jnp.inf, work)
    exps = [jnp.exp(v - vals[0]) for v in vals]
    denom = exps[0] + exps[1] + exps[2] + exps[3]
    gates = [e / denom for e in exps]

    cnt = jnp.zeros((N_EXPERTS, n_tok), F32)
    for hot in hots:
        cnt = cnt + jnp.where(hot, 1.0, 0.0)
    cnt_b = cnt.astype(BF16)
    r = lax.broadcasted_iota(jnp.int32, (tm, tm), 0)
    c = lax.broadcasted_iota(jnp.int32, (tm, tm), 1)
    earlier_tok = jnp.where(r < c, 1.0, 0.0).astype(BF16)
    er = lax.broadcasted_iota(jnp.int32, (N_EXPERTS, N_EXPERTS), 0)
    ec = lax.broadcasted_iota(jnp.int32, (N_EXPERTS, N_EXPERTS), 1)
    lower_exp = jnp.where(ec < er, 1.0, 0.0).astype(BF16)
    cnt_pad = jnp.concatenate([cnt_b, jnp.zeros((LANES - N_EXPERTS, n_tok), BF16)], axis=0)
    pos = []
    for sub in range(OUT_SUB):
        cols = slice(sub * tm, (sub + 1) * tm)
        run = jnp.sum(cnt[:, cols], axis=1, keepdims=True)
        run_even = 2.0 * jnp.floor(0.5 * run + 0.5)
        run_start = _dot(lower_exp, jnp.broadcast_to(run_even, (N_EXPERTS, LANES)).astype(BF16))[:, 0:1]
        pos.append(_dot(cnt_b[:, cols], earlier_tok) + run_start)
        cnt_ref[sub] = _dot_nt(jnp.ones((SUBLANES, tm), BF16), cnt_pad[:, cols])
    pos = jnp.concatenate(pos, axis=1)
    lps = [jnp.sum(jnp.where(hot, pos, 0.0), axis=0, keepdims=True).astype(jnp.int32) for hot in hots]

    route_ref[...] = jnp.concatenate(idxs + lps, axis=0)
    gate_ref[...] = jnp.concatenate(gates + [jnp.zeros((SUBLANES - TOP_K, n_tok), F32)], axis=0)


def _out_proj(x2, y_ssd, y_att, w_out, g_ffn, w_router, b_router):
    t = x2.shape[0]
    tm = OUT_SUB * TM_ROUTE
    nt = t // TM_ROUTE
    wo = w_out.astype(BF16)
    wrh, wrm, _ = _split3(jnp.pad(w_router, ((0, 0), (0, LANES - N_EXPERTS))))
    wr2 = jnp.concatenate([wrh, wrm], axis=1)
    br = b_router.reshape(N_EXPERTS, 1)
    row = lambda n: pl.BlockSpec((tm, n), lambda i: (i, 0))
    col = pl.BlockSpec((SUBLANES, tm), lambda i: (0, i))
    full = lambda a: pl.BlockSpec(a.shape, lambda i: (0, 0))
    args = (x2, y_ssd, y_att, wo, g_ffn.reshape(1, -1), wr2, br)
    return pl.pallas_call(
        _out_proj_kernel,
        grid=(t // tm,),
        in_specs=[row(D_MODEL), row(SSD_WIDTH), row(ATT_WIDTH)] + [full(a) for a in args[3:]],
        out_specs=[row(D_MODEL), row(D_MODEL), col, col,
                   pl.BlockSpec((OUT_SUB, SUBLANES, LANES), lambda i: (i, 0, 0))],
        out_shape=[
            jax.ShapeDtypeStruct((t, D_MODEL), F32),
            jax.ShapeDtypeStruct((t, D_MODEL), BF16),
            jax.ShapeDtypeStruct((SUBLANES, t), jnp.int32),
            jax.ShapeDtypeStruct((SUBLANES, t), F32),
            jax.ShapeDtypeStruct((nt, SUBLANES, LANES), F32),
        ],
        compiler_params=_cparams(("arbitrary",)),
        name="out_proj",
    )(*args)


def _copy_run(n, start_copy):
    @pl.when(n > 0)
    def _():
        start_copy(n)


def _select_by_position(positions, values, n_pos):
    tm = positions[0].shape[1]
    r = lax.broadcasted_iota(jnp.int32, (n_pos, tm), 0)
    out = jnp.zeros((n_pos, tm), F32)
    for k in reversed(range(TOP_K)):
        out = jnp.where(r == positions[k], values[k], out)
    return out


def _dispatch_kernel(cnt_ref, ls_ref, base_ref, zoff_ref, rows_ref, xn_ref, route_ref, xs_hbm, xloc, zeros_vmem,
                     sem_z, sems):
    n_pos = N_POS
    i = pl.program_id(0)
    n_steps = pl.num_programs(0)
    slot = lax.rem(i, 2)

    def zero_fill(op):
        def pad_rows(e, _):
            n = zoff_ref[N_EXPERTS + 1 + e]

            @pl.when(n > 0)
            def _():
                op(pltpu.make_async_copy(zeros_vmem.at[_row_slice(0, n, PACK_SUB)],
                                         xs_hbm.at[_row_slice(zoff_ref[e], n, PACK_SUB)], sem_z))
            return 0

        def unused_block(b, _):
            op(pltpu.make_async_copy(zeros_vmem, xs_hbm.at[_row_slice(b * ROW_BLK, ROW_BLK, PACK_SUB)], sem_z))
            return 0

        lax.fori_loop(0, N_EXPERTS, pad_rows, 0)
        lax.fori_loop(zoff_ref[N_EXPERTS], xs_hbm.shape[0] // (ROW_BLK * PACK_SUB), unused_block, 0)

    @pl.when(i == 0)
    def _():
        zeros_vmem[...] = jnp.zeros_like(zeros_vmem)
        zero_fill(lambda cp: cp.start())

    def slot_wait(s, step):
        n = rows_ref[step]
        pltpu.make_async_copy(xloc.at[s, _row_slice(0, n, PACK_SUB)], xs_hbm.at[_row_slice(0, n, PACK_SUB)],
                              sems.at[s]).wait()

    @pl.when(i >= 2)
    def _():
        slot_wait(slot, i - 2)

    positions = [route_ref[TOP_K + k:TOP_K + k + 1, :] for k in range(TOP_K)]
    sel = _select_by_position(positions, [1.0] * TOP_K, n_pos).astype(BF16)
    rows = _dot(sel, xn_ref[...])
    _pack_rows(xloc.at[slot], rows)

    def per_expert(e, _):
        idx = i * N_EXPERTS + e
        src0 = ls_ref[idx]
        dst0 = base_ref[idx]

        def start_copy(n):
            pltpu.make_async_copy(xloc.at[slot, _row_slice(src0, n, PACK_SUB)],
                                  xs_hbm.at[_row_slice(dst0, n, PACK_SUB)], sems.at[slot]).start()

        _copy_run(cnt_ref[idx], start_copy)
        return 0

    lax.fori_loop(0, N_EXPERTS, per_expert, 0)

    @pl.when(i == n_steps - 1)
    def _():
        slot_wait(slot, i)

        @pl.when(n_steps > 1)
        def _():
            slot_wait(1 - slot, i - 1)

        zero_fill(lambda cp: cp.wait())


def _dispatch(xn, route, tables, n_rows):
    t = xn.shape[0]
    tm = TM_ROUTE
    cnt_tbl, ls_tbl, base_tbl, zoff, tile_rows = tables
    return pl.pallas_call(
        _dispatch_kernel,
        grid_spec=pltpu.PrefetchScalarGridSpec(
            num_scalar_prefetch=5,
            grid=(t // tm,),
            in_specs=[pl.BlockSpec((tm, D_MODEL), lambda i, *_: (i, 0)),
                      pl.BlockSpec((SUBLANES, tm), lambda i, *_: (0, i))],
            out_specs=pl.BlockSpec(memory_space=pl.ANY),
            scratch_shapes=[
                pltpu.VMEM((2, N_POS * PACK_SUB, LANES), jnp.uint32),
                pltpu.VMEM((ROW_BLK * PACK_SUB, LANES), jnp.uint32),
                pltpu.SemaphoreType.DMA,
                pltpu.SemaphoreType.DMA((2,)),
            ],
        ),
        out_shape=jax.ShapeDtypeStruct(((n_rows + ROW_BLK) * PACK_SUB, LANES), jnp.uint32),
        compiler_params=_cparams(("arbitrary",)),
        name="dispatch",
    )(cnt_tbl, ls_tbl, base_tbl, zoff, tile_rows, xn, route)


def _experts_kernel(blk_e_ref, nvalid_ref, first_ref, wslot_ref, enext_ref, pieces_ref, xs_ref, wup_hbm, wdn_hbm,
                    bg_ref, bu_ref, bd_ref, ys_ref, wup_buf, wdn_buf, wg_s, wu_s, wd_s, sems):
    i = pl.program_id(0)
    slot = wslot_ref[i]

    def weight_copies(e, s):
        return (pltpu.make_async_copy(wup_hbm.at[e], wup_buf.at[s], sems.at[0, s]),
                pltpu.make_async_copy(wdn_hbm.at[e], wdn_buf.at[s], sems.at[1, s]))

    @pl.when(i == 0)
    def _():
        for cp in weight_copies(blk_e_ref[0], slot):
            cp.start()

    @pl.when(first_ref[i] != 0)
    def _():
        for cp in weight_copies(blk_e_ref[i], slot):
            cp.wait()

        @pl.when(enext_ref[i] >= 0)
        def _():
            for cp in weight_copies(enext_ref[i], 1 - slot):
                cp.start()

        src = lax.broadcasted_iota(jnp.int32, (DEINT, DEINT), 0)
        dst = lax.broadcasted_iota(jnp.int32, (DEINT, DEINT), 1)
        perm = jnp.where(src == jnp.where(dst < LANES, 2 * dst, 2 * (dst - LANES) + 1), 1.0, 0.0).astype(BF16)
        for g in range(2 * D_EXPERT // DEINT):
            sep = _dot(wup_buf[slot, :, g * DEINT:(g + 1) * DEINT].astype(BF16), perm)
            wg_s[:, g * LANES:(g + 1) * LANES] = sep[:, :LANES].astype(BF16)
            wu_s[:, g * LANES:(g + 1) * LANES] = sep[:, LANES:].astype(BF16)
        wd_s[...] = wdn_buf[slot].astype(BF16)

    def ffn(n_rows):
        used = n_rows * ROW_SUB
        if n_rows:
            xb = _unpack_rows(xs_ref.at[pl.ds(0, n_rows * PACK_SUB)])
            gate = jnp.minimum(_dot(xb, wg_s[...]) + bg_ref[0], SWIGLU_LIMIT)
            up = jnp.clip(_dot(xb, wu_s[...]) + bu_ref[0], -SWIGLU_LIMIT, SWIGLU_LIMIT)
            act = (up + 1.0) * gate * jax.nn.sigmoid(SWIGLU_ALPHA * gate)
            _store_rows(ys_ref.at[pl.ds(0, used)], _dot(act.astype(BF16), wd_s[...]) + bd_ref[0])
        if n_rows < ROW_BLK:
            ys_ref[used:, :] = jnp.zeros((ROW_BLK * ROW_SUB - used, LANES), F32)

    for pieces in range(ROW_BLK // ROW_PIECE + 1):
        pl.when(pieces_ref[i] == pieces)(lambda pieces=pieces: ffn(pieces * ROW_PIECE))


def _experts(xs, plan, w_up, b_up, w_down, b_down, n_rows):
    nb = n_rows // ROW_BLK
    bg = b_up[:, 0::2].reshape(N_EXPERTS, 1, D_EXPERT)
    bu = b_up[:, 1::2].reshape(N_EXPERTS, 1, D_EXPERT)
    bd = b_down.reshape(N_EXPERTS, 1, D_MODEL)
    blk_e, nvalid, first, wslot, enext, pieces = plan
    src = lambda i, be, nv, *_: (jnp.minimum(i, nv[0] - 1), 0)
    bspec = lambda m: pl.BlockSpec((1, 1, m), lambda i, be, *_: (be[i], 0, 0))
    anyspec = pl.BlockSpec(memory_space=pl.ANY)
    return pl.pallas_call(
        _experts_kernel,
        grid_spec=pltpu.PrefetchScalarGridSpec(
            num_scalar_prefetch=6,
            grid=(nb,),
            in_specs=[pl.BlockSpec((ROW_BLK * PACK_SUB, LANES), src), anyspec, anyspec,
                      bspec(D_EXPERT), bspec(D_EXPERT), bspec(D_MODEL)],
            out_specs=_rows_spec(ROW_BLK, lambda i, *_: (i, 0)),
            scratch_shapes=[
                pltpu.VMEM((2, D_MODEL, 2 * D_EXPERT), F32),
                pltpu.VMEM((2, D_EXPERT, D_MODEL), F32),
                pltpu.VMEM((D_MODEL, D_EXPERT), BF16),
                pltpu.VMEM((D_MODEL, D_EXPERT), BF16),
                pltpu.VMEM((D_EXPERT, D_MODEL), BF16),
                pltpu.SemaphoreType.DMA((2, 2)),
            ],
        ),
        out_shape=jax.ShapeDtypeStruct((n_rows * ROW_SUB, LANES), F32),
        compiler_params=_cparams(("arbitrary",)),
        name="experts",
    )(blk_e, nvalid, first, wslot, enext, pieces, xs, w_up, w_down, bg, bu, bd)


def _combine_kernel(cnt_ref, ls_ref, base_ref, rows_ref, ys_hbm, route_ref, gate_ref, x1_ref, p_ref, gp_ref, wpg_ref,
                    wpp_ref, gf_ref, o_ref, yloc, sems):
    tm = TM_ROUTE
    n_pos = N_POS
    i = pl.program_id(0)
    n_steps = pl.num_programs(0)
    slot = lax.rem(i, 2)

    def gather_step(step, s):
        for u in range(COMB_SUB):
            def per_expert(e, _, u=u):
                idx = (step * COMB_SUB + u) * N_EXPERTS + e
                src0 = base_ref[idx]
                dst0 = ls_ref[idx]

                def start_copy(n):
                    pltpu.make_async_copy(ys_hbm.at[_row_slice(src0, n)],
                                          yloc.at[s, u, _row_slice(dst0, n)], sems.at[s]).start()

                _copy_run(cnt_ref[idx], start_copy)
                return 0

            lax.fori_loop(0, N_EXPERTS, per_expert, 0)

    @pl.when(i == 0)
    def _():
        yloc[...] = jnp.zeros_like(yloc)
        gather_step(0, 0)

    @pl.when(i + 1 < n_steps)
    def _():
        gather_step(i + 1, 1 - slot)

    for u in range(COMB_SUB):
        n = rows_ref[i * COMB_SUB + u]
        pltpu.make_async_copy(ys_hbm.at[_row_slice(0, n)], yloc.at[slot, u, _row_slice(0, n)], sems.at[slot]).wait()

    moe = []
    for u in range(COMB_SUB):
        cols = slice(u * tm, (u + 1) * tm)
        positions = [route_ref[TOP_K + k:TOP_K + k + 1, cols] for k in range(TOP_K)]
        sel = _select_by_position(positions, [1.0] * TOP_K, n_pos).astype(BF16)
        gsel = _select_by_position(positions, [gate_ref[k:k + 1, cols] for k in range(TOP_K)], n_pos)
        g_pos = jnp.sum(gsel, axis=1, keepdims=True)
        y_gated = (_load_rows(yloc.at[slot, u]) * g_pos).astype(BF16)
        moe.append(_dot_tn(sel, y_gated))
    x2 = x1_ref[...] + jnp.concatenate(moe, axis=0)
    xn = _rms(x2, gp_ref[...], NORM_EPS).astype(BF16)
    pp = _dot(p_ref[...].astype(BF16), wpp_ref[...])
    x3 = x2 + pp * jax.nn.sigmoid(_dot(xn, wpg_ref[...]))
    o_ref[...] = _rms(x3, gf_ref[...], NORM_EPS)


def _combine(ys, route, gate_t, x1, p2, tables, g_ple, w_ple_gate, w_ple_proj, g_final):
    t = x1.shape[0]
    tm = COMB_SUB * TM_ROUTE
    cnt_tbl, ls_tbl, base_tbl, _, tile_rows = tables
    row = lambda n: pl.BlockSpec((tm, n), lambda i, *_: (i, 0))
    col = pl.BlockSpec((SUBLANES, tm), lambda i, *_: (0, i))
    full = lambda a: pl.BlockSpec(a.shape, lambda i, *_: (0, 0))
    consts = (g_ple.reshape(1, -1), w_ple_gate.astype(BF16), w_ple_proj.astype(BF16), g_final.reshape(1, -1))
    return pl.pallas_call(
        _combine_kernel,
        grid_spec=pltpu.PrefetchScalarGridSpec(
            num_scalar_prefetch=4,
            grid=(t // tm,),
            in_specs=[pl.BlockSpec(memory_space=pl.ANY), col, col, row(D_MODEL), row(PLE_DIM)]
                     + [full(a) for a in consts],
            out_specs=row(D_MODEL),
            scratch_shapes=[
                pltpu.VMEM((2, COMB_SUB, N_POS * ROW_SUB, LANES), F32),
                pltpu.SemaphoreType.DMA((2,)),
            ],
        ),
        out_shape=jax.ShapeDtypeStruct((t, D_MODEL), F32),
        compiler_params=_cparams(("arbitrary",)),
        name="combine",
    )(cnt_tbl, ls_tbl, base_tbl, tile_rows, ys, route, gate_t, x1, p2, *consts)


def _routing_tables(cnt, n_blocks):
    tile_cnt = cnt[:, 0, :N_EXPERTS].astype(jnp.int32)
    tile_cnt = tile_cnt + tile_cnt % 2
    counts = jnp.sum(tile_cnt, axis=0)
    padded = (counts + ROW_BLK - 1) // ROW_BLK * ROW_BLK
    pend = jnp.cumsum(padded)
    pstart = pend - padded
    base = pstart[None, :] + jnp.cumsum(tile_cnt, axis=0) - tile_cnt
    lstart = jnp.cumsum(tile_cnt, axis=1) - tile_cnt
    nvalid = pend[-1:] // ROW_BLK
    zoff = jnp.concatenate([pstart + counts, nvalid, padded - counts])
    blk_start = jnp.minimum(jnp.arange(n_blocks, dtype=jnp.int32) * ROW_BLK, pend[-1] - 1)
    blk_e = jnp.minimum(jnp.sum((pend[None, :] <= blk_start[:, None]).astype(jnp.int32), axis=1), N_EXPERTS - 1)
    first = jnp.concatenate([jnp.ones((1,), bool), blk_e[1:] != blk_e[:-1]])
    wslot = (jnp.cumsum(first.astype(jnp.int32)) - 1) % 2
    eids = jnp.arange(N_EXPERTS, dtype=jnp.int32)
    later_nonempty = (eids[None, :] > eids[:, None]) & (padded[None, :] > 0)
    next_e = jnp.min(jnp.where(later_nonempty, eids[None, :], N_EXPERTS), axis=1)
    next_e = jnp.where(next_e == N_EXPERTS, -1, next_e)
    blk_hot = blk_e[:, None] == eids[None, :]
    per_block = lambda v: jnp.sum(jnp.where(blk_hot, v[None, :], 0), axis=1)
    blk_ids = jnp.arange(n_blocks, dtype=jnp.int32)
    real_rows = jnp.clip(per_block(pstart + counts) - blk_ids * ROW_BLK, 0, ROW_BLK) * (blk_ids < nvalid[0])
    pieces = (real_rows + ROW_PIECE - 1) // ROW_PIECE
    tile_rows = jnp.sum(tile_cnt, axis=1)
    i32 = lambda a: a.reshape(-1).astype(jnp.int32)
    plan = (i32(blk_e), i32(nvalid), i32(first), i32(wslot), i32(per_block(next_e)), i32(pieces))
    return (i32(tile_cnt), i32(lstart), i32(base), i32(zoff), i32(tile_rows)), plan


def kernel(x, p, g_mix, w_in, conv_w, conv_b, dt_bias, a_log, d_skip, ssd_norm_w, lam_q1, lam_k1, lam_q2, lam_k2, subln_w, w_out, g_ffn, w_router, b_router, w_up, b_up, w_down, b_down, g_ple, w_ple_gate, w_ple_proj, g_final):
    bsz, seq, d = x.shape
    t = bsz * seq
    x2 = x.reshape(t, d)
    y_ssd, y_att = _mixer(x2, g_mix[0], w_in[0], conv_w[0], conv_b[0], dt_bias[0], a_log[0], d_skip[0], ssd_norm_w[0],
                          lam_q1[0], lam_k1[0], lam_q2[0], lam_k2[0], subln_w[0], bsz, seq)
    x1, xn, route, gate_t, cnt = _out_proj(x2, y_ssd, y_att, w_out[0], g_ffn[0], w_router[0], b_router[0])

    n_rows = t * TOP_K + (t // TM_ROUTE) * N_EXPERTS + N_EXPERTS * ROW_BLK
    n_rows = -(-n_rows // ROW_BLK) * ROW_BLK
    tables, plan = _routing_tables(cnt, n_rows // ROW_BLK)
    xs = _dispatch(xn, route, tables, n_rows)
    ys = _experts(xs, plan, w_up[0], b_up[0], w_down[0], b_down[0], n_rows)
    out = _combine(ys, route, gate_t, x1, p[0].reshape(t, PLE_DIM), tables, g_ple[0], w_ple_gate[0], w_ple_proj[0],
                   g_final)
    return out.reshape(bsz, seq, d)
```

```python
import math

import jax
import jax.numpy as jnp
from jax import lax
from jax.experimental import pallas as pl
from jax.experimental.pallas import tpu as pltpu

F32 = jnp.float32
BF16 = jnp.bfloat16

D_MODEL = 1024
PLE_DIM = 256
SSD_WIDTH = 512
ATT_WIDTH = 512
SSD_HEAD_DIM = 64
SSD_HEADS = 8
SSD_GROUPS = 2
SSD_STATE = 128
SSD_CONV = 4
SSD_CHUNK = 128
SSD_CONV_CH = SSD_WIDTH + 2 * SSD_GROUPS * SSD_STATE
SSD_NORM_EPS = 1e-5
ATT_HEAD_DIM = 64
ATT_HEADS = 4
SUBLN_EPS = 1e-5
OFF_Z = 0
OFF_XBC = OFF_Z + SSD_WIDTH
OFF_DT = OFF_XBC + SSD_CONV_CH
OFF_Q = OFF_DT + SSD_HEADS
OFF_K = OFF_Q + ATT_WIDTH
OFF_V = OFF_K + ATT_WIDTH
IN_PROJ = OFF_V + ATT_WIDTH
N_EXPERTS = 32
TOP_K = 4
D_EXPERT = 1024
SWIGLU_LIMIT = 7.0
SWIGLU_ALPHA = 1.702
NORM_EPS = 1e-6
LAM_INIT = 0.8 - 0.6 * math.exp(-0.3 * 0)

LANES = 128
SUBLANES = 8
VMEM_LIMIT_BYTES = 56 * 1024 * 1024
ROW_SUB = D_MODEL // LANES
PACK_SUB = ROW_SUB // 2

TM_PROJ = 512
ATT_BQ = 512
ATT_BK = 512
ROW_BLK = 512
ROW_PIECE = 128
TM_ROUTE = 256
N_POS = TOP_K * TM_ROUTE + N_EXPERTS
OUT_SUB = 4
COMB_SUB = 2
DEINT = 2 * LANES


def _cparams(sem):
    return pltpu.CompilerParams(dimension_semantics=sem, vmem_limit_bytes=VMEM_LIMIT_BYTES)


def _rms(x, w, eps):
    return x * lax.rsqrt(jnp.mean(x * x, axis=-1, keepdims=True) + eps) * w


def _dot(a, b):
    return jnp.dot(a, b, preferred_element_type=F32)


def _dot_nt(a, b):
    return lax.dot_general(a, b, (((1,), (1,)), ((), ())), preferred_element_type=F32)


def _dot_tn(a, b):
    return lax.dot_general(a, b, (((0,), (0,)), ((), ())), preferred_element_type=F32)


def _store_rows(ref, val):
    n = val.shape[0]
    for s in range(ROW_SUB):
        ref[pl.ds(s, n, stride=ROW_SUB), :] = val[:, s * LANES:(s + 1) * LANES]


def _load_rows(ref):
    n = ref.shape[0] // ROW_SUB
    return jnp.concatenate([ref[pl.ds(s, n, stride=ROW_SUB), :] for s in range(ROW_SUB)], axis=-1)


def _rows_spec(n, index_map):
    return pl.BlockSpec((n * ROW_SUB, LANES), index_map)


def _row_slice(start, n, sub=ROW_SUB):
    return pl.ds(pl.multiple_of(start * sub, SUBLANES), n * sub)


def _pack_rows(ref, val):
    n = val.shape[0]
    bits = lambda v: lax.bitcast_convert_type(v, jnp.uint32)
    words = (bits(val[:, :D_MODEL // 2]) & jnp.uint32(0xFFFF0000)) | (bits(val[:, D_MODEL // 2:]) >> 16)
    for s in range(PACK_SUB):
        ref[pl.ds(s, n, stride=PACK_SUB), :] = words[:, s * LANES:(s + 1) * LANES]


def _unpack_rows(ref):
    n = ref.shape[0] // PACK_SUB
    words = [ref[pl.ds(s, n, stride=PACK_SUB), :] for s in range(PACK_SUB)]
    as_f32 = lambda w: lax.bitcast_convert_type(w, F32)
    hi = [as_f32(w & jnp.uint32(0xFFFF0000)) for w in words]
    lo = [as_f32(w << 16) for w in words]
    return jnp.concatenate(hi + lo, axis=-1).astype(BF16)


def _split3(x):
    hi = x.astype(BF16)
    r1 = x - hi.astype(F32)
    mid = r1.astype(BF16)
    lo = (r1 - mid.astype(F32)).astype(BF16)
    return hi, mid, lo


def _in_proj_kernel(x_ref, g_ref, wz_ref, wxbc_ref, wdt_ref, wq_ref, wk_ref, wv_ref,
                    z_ref, xbc_ref, dt_ref, q_ref, k_ref, v_ref):
    h = _rms(x_ref[...], g_ref[...], NORM_EPS).astype(BF16)
    z_ref[...] = _dot(h, wz_ref[...])
    xbc_ref[...] = _dot(h, wxbc_ref[...])
    dt_ref[...] = _dot(h, wdt_ref[...])
    q_ref[...] = (_dot(h, wq_ref[...]) * (ATT_HEAD_DIM ** -0.5 * math.log2(math.e))).astype(BF16)
    k_ref[...] = _dot(h, wk_ref[...]).astype(BF16)
    v_ref[...] = _dot(h, wv_ref[...]).astype(BF16)


def _in_proj(x2, g_mix, w_in):
    t = x2.shape[0]
    wb = w_in.astype(BF16)
    wz = wb[:, OFF_Z:OFF_XBC]
    wxbc = wb[:, OFF_XBC:OFF_DT]
    wdt = jnp.pad(wb[:, OFF_DT:OFF_Q], ((0, 0), (0, LANES - SSD_HEADS)))
    wq = wb[:, OFF_Q:OFF_K]
    wk = wb[:, OFF_K:OFF_V]
    wv = wb[:, OFF_V:IN_PROJ]
    tm = TM_PROJ
    row = lambda n: pl.BlockSpec((tm, n), lambda i: (i, 0))
    full = lambda a: pl.BlockSpec(a.shape, lambda i: (0, 0))
    return pl.pallas_call(
        _in_proj_kernel,
        grid=(t // tm,),
        in_specs=[row(D_MODEL), full(g_mix), full(wz), full(wxbc), full(wdt), full(wq), full(wk), full(wv)],
        out_specs=[row(SSD_WIDTH), row(SSD_CONV_CH), row(LANES), row(ATT_WIDTH), row(ATT_WIDTH), row(ATT_WIDTH)],
        out_shape=[
            jax.ShapeDtypeStruct((t, SSD_WIDTH), F32),
            jax.ShapeDtypeStruct((t, SSD_CONV_CH), F32),
            jax.ShapeDtypeStruct((t, LANES), F32),
            jax.ShapeDtypeStruct((t, ATT_WIDTH), BF16),
            jax.ShapeDtypeStruct((t, ATT_WIDTH), BF16),
            jax.ShapeDtypeStruct((t, ATT_WIDTH), BF16),
        ],
        compiler_params=_cparams(("arbitrary",)),
        name="in_proj",
    )(x2, g_mix, wz, wxbc, wdt, wq, wk, wv)


def _ssd_chunk_init(c, xpad_ref, state_ref):
    L = SSD_CHUNK

    @pl.when(c == 0)
    def _():
        xpad_ref[0:SUBLANES, :] = jnp.zeros((SUBLANES, SSD_CONV_CH), F32)
        state_ref[...] = jnp.zeros_like(state_ref)

    @pl.when(c != 0)
    def _():
        xpad_ref[0:SUBLANES, :] = xpad_ref[L:L + SUBLANES, :]


def _ssd_chunk(xbc_ref, dtr_ref, z_ref, cw_ref, cb_ref, dtb_ref, alog_ref, dskip_ref, nw_ref,
               y_ref, xpad_ref, state_ref):
    L = SSD_CHUNK
    xpad_ref[SUBLANES:SUBLANES + L, :] = xbc_ref[...]

    conv = cb_ref[...]
    for j in range(SSD_CONV):
        off = SUBLANES - (SSD_CONV - 1) + j
        conv = conv + cw_ref[j:j + 1, :] * xpad_ref[off:off + L, :]
    act = conv * jax.nn.sigmoid(conv)
    xs = act[:, :SSD_WIDTH]
    bm = act[:, SSD_WIDTH:SSD_WIDTH + SSD_GROUPS * SSD_STATE].astype(BF16)
    cm = act[:, SSD_WIDTH + SSD_GROUPS * SSD_STATE:].astype(BF16)

    dt_in = dtr_ref[...] + dtb_ref[...]
    dt_all = jnp.maximum(dt_in, 0.0) + jnp.log1p(jnp.exp(-jnp.abs(dt_in)))
    adt = dt_all * (-jnp.exp(alog_ref[...]))

    ri = lax.broadcasted_iota(jnp.int32, (L, L), 0)
    ci = lax.broadcasted_iota(jnp.int32, (L, L), 1)
    causal = ci <= ri
    tril = jnp.where(causal, 1.0, 0.0).astype(BF16)
    hi, mid, lo = _split3(adt)
    acum_all = _dot(tril, hi) + _dot(tril, mid) + _dot(tril, lo)
    acum_t = acum_all.T
    a_last = acum_all[L - 1:L, :]
    decay_in_all = jnp.exp(a_last - acum_all)
    decay_out_all = jnp.exp(acum_all)
    chunk_decay_all = jnp.exp(a_last)
    acum = lambda h: acum_all[:, h:h + 1]
    dt = lambda h: dt_all[:, h:h + 1]
    decay_out = lambda h: decay_out_all[:, h:h + 1]
    decay_in = lambda h: decay_in_all[:, h:h + 1]
    chunk_decay = lambda h: chunk_decay_all[:, h:h + 1]

    lane = lax.broadcasted_iota(jnp.int32, (L, LANES), 1)
    lo_half = lane < SSD_HEAD_DIM

    def per_pair(col_a, col_b):
        return jnp.where(lo_half, col_a, col_b)

    ys = []
    for pair in range(SSD_HEADS // 2):
        g = pair // 2
        h0, h1 = 2 * pair, 2 * pair + 1
        cg = cm[:, g * SSD_STATE:(g + 1) * SSD_STATE]
        bg = bm[:, g * SSD_STATE:(g + 1) * SSD_STATE]
        cb = _dot_nt(cg, bg)
        x_pair = xs[:, pair * LANES:(pair + 1) * LANES]
        xdt = x_pair * per_pair(dt(h0), dt(h1))
        y_pair = jnp.zeros((L, LANES), F32)
        for hh, keep in ((h0, lo_half), (h1, jnp.logical_not(lo_half))):
            seg = acum(hh) - acum_t[hh:hh + 1, :]
            lmat = jnp.where(causal, jnp.exp(jnp.where(causal, seg, 0.0)), 0.0)
            m = (cb * lmat).astype(BF16)
            y_pair = y_pair + _dot(m, jnp.where(keep, xdt, 0.0).astype(BF16))
        s_prev = state_ref[pair]
        y_off = _dot(cg, s_prev.astype(BF16)) * per_pair(decay_out(h0), decay_out(h1))
        w_in = (xdt * per_pair(decay_in(h0), decay_in(h1))).astype(BF16)
        cd = jnp.where(lane[0:1, :] < SSD_HEAD_DIM, chunk_decay(h0), chunk_decay(h1))
        state_ref[pair] = s_prev * cd + _dot_tn(bg, w_in)
        ys.append(y_pair + y_off + dskip_ref[:, pair * LANES:(pair + 1) * LANES] * x_pair)

    y = jnp.concatenate(ys, axis=-1)
    zz = z_ref[...]
    y = y * (zz * jax.nn.sigmoid(zz))
    gw = SSD_WIDTH // SSD_GROUPS
    outs = []
    for g in range(SSD_GROUPS):
        yg = y[:, g * gw:(g + 1) * gw]
        outs.append(yg * lax.rsqrt(jnp.mean(yg * yg, axis=-1, keepdims=True) + SSD_NORM_EPS))
    y_ref[...] = (jnp.concatenate(outs, axis=-1) * nw_ref[...]).astype(BF16)


def _attn_ssd_kernel(q_ref, k_ref, v_ref, lq1_ref, lk1_ref, lq2_ref, lk2_ref, sw_ref,
                     xbc_ref, dtr_ref, z_ref, cw_ref, cb_ref, dtb_ref, alog_ref, dskip_ref, nw_ref,
                     o_ref, y_ref, s_scr, vt_scr, xpad_ref, state_ref):
    bq, bk = ATT_BQ, ATT_BK
    qi = pl.program_id(2)
    n_maps = 2
    _ssd_chunk_init(pl.program_id(1) * pl.num_programs(2) + qi, xpad_ref, state_ref)

    @pl.when(qi == 0)
    def _():
        vt_scr[...] = v_ref[...].astype(F32).T.astype(BF16)

    def fold(t, reduce):
        return reduce(t.reshape(bk // SUBLANES, SUBLANES, bq), axis=0)

    def merge(old, new, op):
        return new if old is None else op(old, new)

    def attend(nk):
        _ssd_chunk(xbc_ref, dtr_ref, z_ref, cw_ref, cb_ref, dtb_ref, alog_ref, dskip_ref, nw_ref,
                   y_ref, xpad_ref, state_ref)
        q = q_ref[...]
        lane = lax.broadcasted_iota(jnp.int32, (bq, LANES), 1)
        zero = jnp.zeros_like(q)
        q_maps = (jnp.where(lane < ATT_HEAD_DIM, q, zero), jnp.where(lane >= ATT_HEAD_DIM, q, zero))
        key = lax.broadcasted_iota(jnp.int32, (bk, bq), 0)
        qry = lax.broadcasted_iota(jnp.int32, (bk, bq), 1)
        causal = key <= qry

        mt = [None] * n_maps
        for j in range(nk):
            kb = k_ref[j * bk:(j + 1) * bk, :]
            for m in range(n_maps):
                s = _dot_nt(kb, q_maps[m])
                if j == nk - 1:
                    s = jnp.where(causal, s, -jnp.inf)
                s_scr[m, j] = s
                mt[m] = merge(mt[m], fold(s, jnp.max), jnp.maximum)
        q_max = [jnp.max(t, axis=0, keepdims=True) for t in mt]

        lt = [None] * n_maps
        acc = [None] * n_maps
        for j in range(nk):
            vt = vt_scr[:, j * bk:(j + 1) * bk]
            for m in range(n_maps):
                p = jnp.exp2(s_scr[m, j] - q_max[m])
                lt[m] = merge(lt[m], fold(p, jnp.sum), jnp.add)
                acc[m] = merge(acc[m], _dot(vt, p.astype(BF16)), jnp.add)

        lam = (jnp.exp(jnp.sum(lq1_ref[...] * lk1_ref[...], axis=-1, keepdims=True))
               - jnp.exp(jnp.sum(lq2_ref[...] * lk2_ref[...], axis=-1, keepdims=True)) + LAM_INIT)
        l1 = jnp.sum(lt[0], axis=0, keepdims=True)
        l2 = jnp.sum(lt[1], axis=0, keepdims=True)
        o = acc[0] / l1 - lam * (acc[1] / l2)
        o = o * lax.rsqrt(jnp.mean(o * o, axis=0, keepdims=True) + SUBLN_EPS) * sw_ref[...]
        o_ref[...] = (o * (1.0 - LAM_INIT)).T.astype(BF16)

    for nk in range(1, k_ref.shape[0] // bk + 1):
        pl.when(qi == nk - 1)(lambda nk=nk: attend(nk))


def _attn_ssd(q, k, v, lam_q1, lam_k1, lam_q2, lam_k2, subln_w, xbc, dtr, z, conv_w, conv_b, dt_bias, a_log, d_skip,
              norm_w, bsz, seq):
    nq = seq // ATT_BQ
    nc = seq // SSD_CHUNK
    assert nc == ATT_HEADS * nq
    qspec = pl.BlockSpec((ATT_BQ, LANES), lambda b, h, i: (b * nq + i, h))
    kvspec = pl.BlockSpec((seq, LANES), lambda b, h, i: (b, h))
    full = lambda a: pl.BlockSpec(a.shape, lambda b, h, i: (0, 0))
    chunk = lambda n: pl.BlockSpec((SSD_CHUNK, n), lambda b, h, i: (b * nc + h * nq + i, 0))
    lams = [a.reshape(1, -1) for a in (lam_q1, lam_k1, lam_q2, lam_k2)]
    sw = subln_w.reshape(-1, 1)
    pad_h = lambda v: jnp.pad(v.reshape(1, SSD_HEADS), ((0, 0), (0, LANES - SSD_HEADS)))
    dskip_lanes = jnp.repeat(d_skip, SSD_HEAD_DIM).reshape(1, SSD_WIDTH)
    consts = (conv_w, conv_b.reshape(1, -1), pad_h(dt_bias), pad_h(a_log), dskip_lanes, norm_w.reshape(1, -1))
    return pl.pallas_call(
        _attn_ssd_kernel,
        grid=(bsz, ATT_HEADS, nq),
        in_specs=[qspec, kvspec, kvspec] + [full(a) for a in lams] + [full(sw)]
                 + [chunk(SSD_CONV_CH), chunk(LANES), chunk(SSD_WIDTH)] + [full(a) for a in consts],
        out_specs=[qspec, chunk(SSD_WIDTH)],
        out_shape=[jax.ShapeDtypeStruct((bsz * seq, ATT_WIDTH), BF16),
                   jax.ShapeDtypeStruct((bsz * seq, SSD_WIDTH), BF16)],
        scratch_shapes=[pltpu.VMEM((2, seq // ATT_BK, ATT_BK, ATT_BQ), F32),
                        pltpu.VMEM((LANES, seq), BF16),
                        pltpu.VMEM((SSD_CHUNK + 2 * SUBLANES, SSD_CONV_CH), F32),
                        pltpu.VMEM((SSD_HEADS // 2, SSD_STATE, LANES), F32)],
        compiler_params=_cparams(("arbitrary", "arbitrary", "arbitrary")),
        name="attn_ssd",
    )(q, k, v, *lams, sw, xbc, dtr, z, *consts)


def _mixer(x2, g_mix, w_in, conv_w, conv_b, dt_bias, a_log, d_skip, ssd_norm_w,
           lam_q1, lam_k1, lam_q2, lam_k2, subln_w, bsz, seq):
    z, xbc, dtr, q, k, v = _in_proj(x2, g_mix.reshape(1, -1), w_in)
    y_att, y_ssd = _attn_ssd(q, k, v, lam_q1, lam_k1, lam_q2, lam_k2, subln_w, xbc, dtr, z, conv_w, conv_b, dt_bias,
                             a_log, d_skip, ssd_norm_w, bsz, seq)
    return y_ssd, y_att


def _out_proj_kernel(x_ref, ys_ref, ya_ref, wo_ref, g_ref, wr2_ref, br_ref,
                     x1_ref, xn_ref, route_ref, gate_ref, cnt_ref):
    tm = TM_ROUTE
    x1 = x_ref[...] + _dot(jnp.concatenate([ys_ref[...], ya_ref[...]], axis=-1), wo_ref[...])
    x1_ref[...] = x1
    xn_all = _rms(x1, g_ref[...], NORM_EPS)
    xn_ref[...] = xn_all.astype(BF16)

    n_tok = OUT_SUB * tm
    xh, xm, _ = _split3(xn_all)
    lg2 = _dot(xh, wr2_ref[...]) + _dot(xm, wr2_ref[...])
    logits = (lg2[:, :LANES] + lg2[:, LANES:]).T[:N_EXPERTS, :] + br_ref[...]

    eidx = lax.broadcasted_iota(jnp.int32, (N_EXPERTS, n_tok), 0).astype(F32)
    work = logits
    vals, idxs, hots = [], [], []
    for _ in range(TOP_K):
        m = jnp.max(work, axis=0, keepdims=True)
        idx = jnp.min(jnp.where(work == m, eidx, float(N_EXPERTS)), axis=0, keepdims=True)
        hot = eidx == idx
        vals.append(m)
        idxs.append(idx.astype(jnp.int32))
        hots.append(hot)
        work = jnp.where(hot, -jnp.inf, work)
    exps = [jnp.exp(v - vals[0]) for v in vals]
    denom = exps[0] + exps[1] + exps[2] + exps[3]
    gates = [e / denom for e in exps]

    cnt = jnp.zeros((N_EXPERTS, n_tok), F32)
    for hot in hots:
        cnt = cnt + jnp.where(hot, 1.0, 0.0)
    cnt_b = cnt.astype(BF16)
    r = lax.broadcasted_iota(jnp.int32, (tm, tm), 0)
    c = lax.broadcasted_iota(jnp.int32, (tm, tm), 1)
    earlier_tok = jnp.where(r < c, 1.0, 0.0).astype(BF16)
    er = lax.broadcasted_iota(jnp.int32, (N_EXPERTS, N_EXPERTS), 0)
    ec = lax.broadcasted_iota(jnp.int32, (N_EXPERTS, N_EXPERTS), 1)
    lower_exp = jnp.where(ec < er, 1.0, 0.0).astype(BF16)
    cnt_pad = jnp.concatenate([cnt_b, jnp.zeros((LANES - N_EXPERTS, n_tok), BF16)], axis=0)
    pos = []
    for sub in range(OUT_SUB):
        cols = slice(sub * tm, (sub + 1) * tm)
        run = jnp.sum(cnt[:, cols], axis=1, keepdims=True)
        run_even = 2.0 * jnp.floor(0.5 * run + 0.5)
        run_start = _dot(lower_exp, jnp.broadcast_to(run_even, (N_EXPERTS, LANES)).astype(BF16))[:, 0:1]
        pos.append(_dot(cnt_b[:, cols], earlier_tok) + run_start)
        cnt_ref[sub] = _dot_nt(jnp.ones((SUBLANES, tm), BF16), cnt_pad[:, cols])
    pos = jnp.concatenate(pos, axis=1)
    lps = [jnp.sum(jnp.where(hot, pos, 0.0), axis=0, keepdims=True).astype(jnp.int32) for hot in hots]

    route_ref[...] = jnp.concatenate(idxs + lps, axis=0)
    gate_ref[...] = jnp.concatenate(gates + [jnp.zeros((SUBLANES - TOP_K, n_tok), F32)], axis=0)


def _out_proj(x2, y_ssd, y_att, w_out, g_ffn, w_router, b_router):
    t = x2.shape[0]
    tm = OUT_SUB * TM_ROUTE
    nt = t // TM_ROUTE
    wo = w_out.astype(BF16)
    wrh, wrm, _ = _split3(jnp.pad(w_router, ((0, 0), (0, LANES - N_EXPERTS))))
    wr2 = jnp.concatenate([wrh, wrm], axis=1)
    br = b_router.reshape(N_EXPERTS, 1)
    row = lambda n: pl.BlockSpec((tm, n), lambda i: (i, 0))
    col = pl.BlockSpec((SUBLANES, tm), lambda i: (0, i))
    full = lambda a: pl.BlockSpec(a.shape, lambda i: (0, 0))
    args = (x2, y_ssd, y_att, wo, g_ffn.reshape(1, -1), wr2, br)
    return pl.pallas_call(
        _out_proj_kernel,
        grid=(t // tm,),
        in_specs=[row(D_MODEL), row(SSD_WIDTH), row(ATT_WIDTH)] + [full(a) for a in args[3:]],
        out_specs=[row(D_MODEL), row(D_MODEL), col, col,
                   pl.BlockSpec((OUT_SUB, SUBLANES, LANES), lambda i: (i, 0, 0))],
        out_shape=[
            jax.ShapeDtypeStruct((t, D_MODEL), F32),
            jax.ShapeDtypeStruct((t, D_MODEL), BF16),
            jax.ShapeDtypeStruct((SUBLANES, t), jnp.int32),
            jax.ShapeDtypeStruct((SUBLANES, t), F32),
            jax.ShapeDtypeStruct((nt, SUBLANES, LANES), F32),
        ],
        compiler_params=_cparams(("arbitrary",)),
        name="out_proj",
    )(*args)


def _copy_run(n, start_copy):
    @pl.when(n > 0)
    def _():
        start_copy(n)


def _select_by_position(positions, values, n_pos):
    tm = positions[0].shape[1]
    r = lax.broadcasted_iota(jnp.int32, (n_pos, tm), 0)
    out = jnp.zeros((n_pos, tm), F32)
    for k in reversed(range(TOP_K)):
        out = jnp.where(r == positions[k], values[k], out)
    return out


def _dispatch_kernel(cnt_ref, ls_ref, base_ref, zoff_ref, rows_ref, xn_ref, route_ref, xs_hbm, xloc, zeros_vmem,
                     sem_z, sems):
    n_pos = N_POS
    i = pl.program_id(0)
    n_steps = pl.num_programs(0)
    slot = lax.rem(i, 2)

    def zero_fill(op):
        def pad_rows(e, _):
            n = zoff_ref[N_EXPERTS + 1 + e]

            @pl.when(n > 0)
            def _():
                op(pltpu.make_async_copy(zeros_vmem.at[_row_slice(0, n, PACK_SUB)],
                                         xs_hbm.at[_row_slice(zoff_ref[e], n, PACK_SUB)], sem_z))
            return 0

        def unused_block(b, _):
            op(pltpu.make_async_copy(zeros_vmem, xs_hbm.at[_row_slice(b * ROW_BLK, ROW_BLK, PACK_SUB)], sem_z))
            return 0

        lax.fori_loop(0, N_EXPERTS, pad_rows, 0)
        lax.fori_loop(zoff_ref[N_EXPERTS], xs_hbm.shape[0] // (ROW_BLK * PACK_SUB), unused_block, 0)

    @pl.when(i == 0)
    def _():
        zeros_vmem[...] = jnp.zeros_like(zeros_vmem)
        zero_fill(lambda cp: cp.start())

    def slot_wait(s, step):
        n = rows_ref[step]
        pltpu.make_async_copy(xloc.at[s, _row_slice(0, n, PACK_SUB)], xs_hbm.at[_row_slice(0, n, PACK_SUB)],
                              sems.at[s]).wait()

    @pl.when(i >= 2)
    def _():
        slot_wait(slot, i - 2)

    positions = [route_ref[TOP_K + k:TOP_K + k + 1, :] for k in range(TOP_K)]
    sel = _select_by_position(positions, [1.0] * TOP_K, n_pos).astype(BF16)
    rows = _dot(sel, xn_ref[...])
    _pack_rows(xloc.at[slot], rows)

    def per_expert(e, _):
        idx = i * N_EXPERTS + e
        src0 = ls_ref[idx]
        dst0 = base_ref[idx]

        def start_copy(n):
            pltpu.make_async_copy(xloc.at[slot, _row_slice(src0, n, PACK_SUB)],
                                  xs_hbm.at[_row_slice(dst0, n, PACK_SUB)], sems.at[slot]).start()

        _copy_run(cnt_ref[idx], start_copy)
        return 0

    lax.fori_loop(0, N_EXPERTS, per_expert, 0)

    @pl.when(i == n_steps - 1)
    def _():
        slot_wait(slot, i)

        @pl.when(n_steps > 1)
        def _():
            slot_wait(1 - slot, i - 1)

        zero_fill(lambda cp: cp.wait())


def _dispatch(xn, route, tables, n_rows):
    t = xn.shape[0]
    tm = TM_ROUTE
    cnt_tbl, ls_tbl, base_tbl, zoff, tile_rows = tables
    return pl.pallas_call(
        _dispatch_kernel,
        grid_spec=pltpu.PrefetchScalarGridSpec(
            num_scalar_prefetch=5,
            grid=(t // tm,),
            in_specs=[pl.BlockSpec((tm, D_MODEL), lambda i, *_: (i, 0)),
                      pl.BlockSpec((SUBLANES, tm), lambda i, *_: (0, i))],
            out_specs=pl.BlockSpec(memory_space=pl.ANY),
            scratch_shapes=[
                pltpu.VMEM((2, N_POS * PACK_SUB, LANES), jnp.uint32),
                pltpu.VMEM((ROW_BLK * PACK_SUB, LANES), jnp.uint32),
                pltpu.SemaphoreType.DMA,
                pltpu.SemaphoreType.DMA((2,)),
            ],
        ),
        out_shape=jax.ShapeDtypeStruct(((n_rows + ROW_BLK) * PACK_SUB, LANES), jnp.uint32),
        compiler_params=_cparams(("arbitrary",)),
        name="dispatch",
    )(cnt_tbl, ls_tbl, base_tbl, zoff, tile_rows, xn, route)


def _experts_kernel(blk_e_ref, nvalid_ref, first_ref, wslot_ref, enext_ref, pieces_ref, xs_ref, wup_hbm, wdn_hbm,
                    bg_ref, bu_ref, bd_ref, ys_ref, wup_buf, wdn_buf, wg_s, wu_s, wd_s, sems):
    i = pl.program_id(0)
    slot = wslot_ref[i]

    def weight_copies(e, s):
        return (pltpu.make_async_copy(wup_hbm.at[e], wup_buf.at[s], sems.at[0, s]),
                pltpu.make_async_copy(wdn_hbm.at[e], wdn_buf.at[s], sems.at[1, s]))

    @pl.when(i == 0)
    def _():
        for cp in weight_copies(blk_e_ref[0], slot):
            cp.start()

    @pl.when(first_ref[i] != 0)
    def _():
        for cp in weight_copies(blk_e_ref[i], slot):
            cp.wait()

        @pl.when(enext_ref[i] >= 0)
        def _():
            for cp in weight_copies(enext_ref[i], 1 - slot):
                cp.start()

        src = lax.broadcasted_iota(jnp.int32, (DEINT, DEINT), 0)
        dst = lax.broadcasted_iota(jnp.int32, (DEINT, DEINT), 1)
        perm = jnp.where(src == jnp.where(dst < LANES, 2 * dst, 2 * (dst - LANES) + 1), 1.0, 0.0).astype(BF16)
        for g in range(2 * D_EXPERT // DEINT):
            sep = _dot(wup_buf[slot, :, g * DEINT:(g + 1) * DEINT].astype(BF16), perm)
            wg_s[:, g * LANES:(g + 1) * LANES] = sep[:, :LANES].astype(BF16)
            wu_s[:, g * LANES:(g + 1) * LANES] = sep[:, LANES:].astype(BF16)
        wd_s[...] = wdn_buf[slot].astype(BF16)

    def ffn(n_rows):
        used = n_rows * ROW_SUB
        if n_rows:
            xb = _unpack_rows(xs_ref.at[pl.ds(0, n_rows * PACK_SUB)])
            gate = jnp.minimum(_dot(xb, wg_s[...]) + bg_ref[0], SWIGLU_LIMIT)
            up = jnp.clip(_dot(xb, wu_s[...]) + bu_ref[0], -SWIGLU_LIMIT, SWIGLU_LIMIT)
            act = (up + 1.0) * gate * jax.nn.sigmoid(SWIGLU_ALPHA * gate)
            _store_rows(ys_ref.at[pl.ds(0, used)], _dot(act.astype(BF16), wd_s[...]) + bd_ref[0])
        if n_rows < ROW_BLK:
            ys_ref[used:, :] = jnp.zeros((ROW_BLK * ROW_SUB - used, LANES), F32)

    for pieces in range(ROW_BLK // ROW_PIECE + 1):
        pl.when(pieces_ref[i] == pieces)(lambda pieces=pieces: ffn(pieces * ROW_PIECE))


def _experts(xs, plan, w_up, b_up, w_down, b_down, n_rows):
    nb = n_rows // ROW_BLK
    bg = b_up[:, 0::2].reshape(N_EXPERTS, 1, D_EXPERT)
    bu = b_up[:, 1::2].reshape(N_EXPERTS, 1, D_EXPERT)
    bd = b_down.reshape(N_EXPERTS, 1, D_MODEL)
    blk_e, nvalid, first, wslot, enext, pieces = plan
    src = lambda i, be, nv, *_: (jnp.minimum(i, nv[0] - 1), 0)
    bspec = lambda m: pl.BlockSpec((1, 1, m), lambda i, be, *_: (be[i], 0, 0))
    anyspec = pl.BlockSpec(memory_space=pl.ANY)
    return pl.pallas_call(
        _experts_kernel,
        grid_spec=pltpu.PrefetchScalarGridSpec(
            num_scalar_prefetch=6,
            grid=(nb,),
            in_specs=[pl.BlockSpec((ROW_BLK * PACK_SUB, LANES), src), anyspec, anyspec,
                      bspec(D_EXPERT), bspec(D_EXPERT), bspec(D_MODEL)],
            out_specs=_rows_spec(ROW_BLK, lambda i, *_: (i, 0)),
            scratch_shapes=[
                pltpu.VMEM((2, D_MODEL, 2 * D_EXPERT), F32),
                pltpu.VMEM((2, D_EXPERT, D_MODEL), F32),
                pltpu.VMEM((D_MODEL, D_EXPERT), BF16),
                pltpu.VMEM((D_MODEL, D_EXPERT), BF16),
                pltpu.VMEM((D_EXPERT, D_MODEL), BF16),
                pltpu.SemaphoreType.DMA((2, 2)),
            ],
        ),
        out_shape=jax.ShapeDtypeStruct((n_rows * ROW_SUB, LANES), F32),
        compiler_params=_cparams(("arbitrary",)),
        name="experts",
    )(blk_e, nvalid, first, wslot, enext, pieces, xs, w_up, w_down, bg, bu, bd)


def _combine_kernel(cnt_ref, ls_ref, base_ref, rows_ref, ys_hbm, route_ref, gate_ref, x1_ref, p_ref, gp_ref, wpg_ref,
                    wpp_ref, gf_ref, o_ref, yloc, sems):
    tm = TM_ROUTE
    n_pos = N_POS
    i = pl.program_id(0)
    n_steps = pl.num_programs(0)
    slot = lax.rem(i, 2)

    def gather_step(step, s):
        for u in range(COMB_SUB):
            def per_expert(e, _, u=u):
                idx = (step * COMB_SUB + u) * N_EXPERTS + e
                src0 = base_ref[idx]
                dst0 = ls_ref[idx]

                def start_copy(n):
                    pltpu.make_async_copy(ys_hbm.at[_row_slice(src0, n)],
                                          yloc.at[s, u, _row_slice(dst0, n)], sems.at[s]).start()

                _copy_run(cnt_ref[idx], start_copy)
                return 0

            lax.fori_loop(0, N_EXPERTS, per_expert, 0)

    @pl.when(i == 0)
    def _():
        yloc[...] = jnp.zeros_like(yloc)
        gather_step(0, 0)

    @pl.when(i + 1 < n_steps)
    def _():
        gather_step(i + 1, 1 - slot)

    for u in range(COMB_SUB):
        n = rows_ref[i * COMB_SUB + u]
        pltpu.make_async_copy(ys_hbm.at[_row_slice(0, n)], yloc.at[slot, u, _row_slice(0, n)], sems.at[slot]).wait()

    moe = []
    for u in range(COMB_SUB):
        cols = slice(u * tm, (u + 1) * tm)
        positions = [route_ref[TOP_K + k:TOP_K + k + 1, cols] for k in range(TOP_K)]
        sel = _select_by_position(positions, [1.0] * TOP_K, n_pos).astype(BF16)
        gsel = _select_by_position(positions, [gate_ref[k:k + 1, cols] for k in range(TOP_K)], n_pos)
        g_pos = jnp.sum(gsel, axis=1, keepdims=True)
        y_gated = (_load_rows(yloc.at[slot, u]) * g_pos).astype(BF16)
        moe.append(_dot_tn(sel, y_gated))
    x2 = x1_ref[...] + jnp.concatenate(moe, axis=0)
    xn = _rms(x2, gp_ref[...], NORM_EPS).astype(BF16)
    pp = _dot(p_ref[...].astype(BF16), wpp_ref[...])
    x3 = x2 + pp * jax.nn.sigmoid(_dot(xn, wpg_ref[...]))
    o_ref[...] = _rms(x3, gf_ref[...], NORM_EPS)


def _combine(ys, route, gate_t, x1, p2, tables, g_ple, w_ple_gate, w_ple_proj, g_final):
    t = x1.shape[0]
    tm = COMB_SUB * TM_ROUTE
    cnt_tbl, ls_tbl, base_tbl, _, tile_rows = tables
    row = lambda n: pl.BlockSpec((tm, n), lambda i, *_: (i, 0))
    col = pl.BlockSpec((SUBLANES, tm), lambda i, *_: (0, i))
    full = lambda a: pl.BlockSpec(a.shape, lambda i, *_: (0, 0))
    consts = (g_ple.reshape(1, -1), w_ple_gate.astype(BF16), w_ple_proj.astype(BF16), g_final.reshape(1, -1))
    return pl.pallas_call(
        _combine_kernel,
        grid_spec=pltpu.PrefetchScalarGridSpec(
            num_scalar_prefetch=4,
            grid=(t // tm,),
            in_specs=[pl.BlockSpec(memory_space=pl.ANY), col, col, row(D_MODEL), row(PLE_DIM)]
                     + [full(a) for a in consts],
            out_specs=row(D_MODEL),
            scratch_shapes=[
                pltpu.VMEM((2, COMB_SUB, N_POS * ROW_SUB, LANES), F32),
                pltpu.SemaphoreType.DMA((2,)),
            ],
        ),
        out_shape=jax.ShapeDtypeStruct((t, D_MODEL), F32),
        compiler_params=_cparams(("arbitrary",)),
        name="combine",
    )(cnt_tbl, ls_tbl, base_tbl, tile_rows, ys, route, gate_t, x1, p2, *consts)


def _routing_tables(cnt, n_blocks):
    tile_cnt = cnt[:, 0, :N_EXPERTS].astype(jnp.int32)
    tile_cnt = tile_cnt + tile_cnt % 2
    counts = jnp.sum(tile_cnt, axis=0)
    padded = (counts + ROW_BLK - 1) // ROW_BLK * ROW_BLK
    pend = jnp.cumsum(padded)
    pstart = pend - padded
    base = pstart[None, :] + jnp.cumsum(tile_cnt, axis=0) - tile_cnt
    lstart = jnp.cumsum(tile_cnt, axis=1) - tile_cnt
    nvalid = pend[-1:] // ROW_BLK
    zoff = jnp.concatenate([pstart + counts, nvalid, padded - counts])
    blk_start = jnp.minimum(jnp.arange(n_blocks, dtype=jnp.int32) * ROW_BLK, pend[-1] - 1)
    blk_e = jnp.minimum(jnp.sum((pend[None, :] <= blk_start[:, None]).astype(jnp.int32), axis=1), N_EXPERTS - 1)
    first = jnp.concatenate([jnp.ones((1,), bool), blk_e[1:] != blk_e[:-1]])
    wslot = (jnp.cumsum(first.astype(jnp.int32)) - 1) % 2
    eids = jnp.arange(N_EXPERTS, dtype=jnp.int32)
    later_nonempty = (eids[None, :] > eids[:, None]) & (padded[None, :] > 0)
    next_e = jnp.min(jnp.where(later_nonempty, eids[None, :], N_EXPERTS), axis=1)
    next_e = jnp.where(next_e == N_EXPERTS, -1, next_e)
    blk_hot = blk_e[:, None] == eids[None, :]
    per_block = lambda v: jnp.sum(jnp.where(blk_hot, v[None, :], 0), axis=1)
    blk_ids = jnp.arange(n_blocks, dtype=jnp.int32)
    real_rows = jnp.clip(per_block(pstart + counts) - blk_ids * ROW_BLK, 0, ROW_BLK) * (blk_ids < nvalid[0])
    pieces = (real_rows + ROW_PIECE - 1) // ROW_PIECE
    tile_rows = jnp.sum(tile_cnt, axis=1)
    i32 = lambda a: a.reshape(-1).astype(jnp.int32)
    plan = (i32(blk_e), i32(nvalid), i32(first), i32(wslot), i32(per_block(next_e)), i32(pieces))
    return (i32(tile_cnt), i32(lstart), i32(base), i32(zoff), i32(tile_rows)), plan


def kernel(x, p, g_mix, w_in, conv_w, conv_b, dt_bias, a_log, d_skip, ssd_norm_w, lam_q1, lam_k1, lam_q2, lam_k2, subln_w, w_out, g_ffn, w_router, b_router, w_up, b_up, w_down, b_down, g_ple, w_ple_gate, w_ple_proj, g_final):
    bsz, seq, d = x.shape
    t = bsz * seq
    x2 = x.reshape(t, d)
    y_ssd, y_att = _mixer(x2, g_mix[0], w_in[0], conv_w[0], conv_b[0], dt_bias[0], a_log[0], d_skip[0], ssd_norm_w[0],
                          lam_q1[0], lam_k1[0], lam_q2[0], lam_k2[0], subln_w[0], bsz, seq)
    x1, xn, route, gate_t, cnt = _out_proj(x2, y_ssd, y_att, w_out[0], g_ffn[0], w_router[0], b_router[0])

    n_rows = t * TOP_K + (t // TM_ROUTE) * N_EXPERTS + N_EXPERTS * ROW_BLK
    n_rows = -(-n_rows // ROW_BLK) * ROW_BLK
    tables, plan = _routing_tables(cnt, n_rows // ROW_BLK)
    xs = _dispatch(xn, route, tables, n_rows)
    ys = _experts(xs, plan, w_up[0], b_up[0], w_down[0], b_down[0], n_rows)
    out = _combine(ys, route, gate_t, x1, p[0].reshape(t, PLE_DIM), tables, g_ple[0], w_ple_gate[0], w_ple_proj[0],
                   g_final)
    return out.reshape(bsz, seq, d)
```

```python
import math

import jax
import jax.numpy as jnp
from jax import lax
from jax.experimental import pallas as pl
from jax.experimental.pallas import tpu as pltpu

F32 = jnp.float32
BF16 = jnp.bfloat16

D_MODEL = 1024
PLE_DIM = 256
SSD_WIDTH = 512
ATT_WIDTH = 512
SSD_HEAD_DIM = 64
SSD_HEADS = 8
SSD_GROUPS = 2
SSD_STATE = 128
SSD_CONV = 4
SSD_CHUNK = 128
SSD_CONV_CH = SSD_WIDTH + 2 * SSD_GROUPS * SSD_STATE
SSD_NORM_EPS = 1e-5
ATT_HEAD_DIM = 64
ATT_HEADS = 4
SUBLN_EPS = 1e-5
OFF_Z = 0
OFF_XBC = OFF_Z + SSD_WIDTH
OFF_DT = OFF_XBC + SSD_CONV_CH
OFF_Q = OFF_DT + SSD_HEADS
OFF_K = OFF_Q + ATT_WIDTH
OFF_V = OFF_K + ATT_WIDTH
IN_PROJ = OFF_V + ATT_WIDTH
N_EXPERTS = 32
TOP_K = 4
D_EXPERT = 1024
SWIGLU_LIMIT = 7.0
SWIGLU_ALPHA = 1.702
NORM_EPS = 1e-6
LAM_INIT = 0.8 - 0.6 * math.exp(-0.3 * 0)

LANES = 128
SUBLANES = 8
VMEM_LIMIT_BYTES = 56 * 1024 * 1024
ROW_SUB = D_MODEL // LANES
PACK_SUB = ROW_SUB // 2

TM_PROJ = 512
ATT_BQ = 512
ATT_BK = 512
ROW_BLK = 512
ROW_PIECE = 128
TM_ROUTE = 256
N_POS = TOP_K * TM_ROUTE + N_EXPERTS
OUT_SUB = 4
COMB_SUB = 2
DEINT = 2 * LANES


def _cparams(sem):
    return pltpu.CompilerParams(dimension_semantics=sem, vmem_limit_bytes=VMEM_LIMIT_BYTES)


def _rms(x, w, eps):
    return x * lax.rsqrt(jnp.mean(x * x, axis=-1, keepdims=True) + eps) * w


def _dot(a, b):
    return jnp.dot(a, b, preferred_element_type=F32)


def _dot_nt(a, b):
    return lax.dot_general(a, b, (((1,), (1,)), ((), ())), preferred_element_type=F32)


def _dot_tn(a, b):
    return lax.dot_general(a, b, (((0,), (0,)), ((), ())), preferred_element_type=F32)


def _store_rows(ref, val):
    n = val.shape[0]
    for s in range(ROW_SUB):
        ref[pl.ds(s, n, stride=ROW_SUB), :] = val[:, s * LANES:(s + 1) * LANES]


def _load_rows(ref):
    n = ref.shape[0] // ROW_SUB
    return jnp.concatenate([ref[pl.ds(s, n, stride=ROW_SUB), :] for s in range(ROW_SUB)], axis=-1)


def _rows_spec(n, index_map):
    return pl.BlockSpec((n * ROW_SUB, LANES), index_map)


def _row_slice(start, n, sub=ROW_SUB):
    return pl.ds(pl.multiple_of(start * sub, SUBLANES), n * sub)


def _pack_rows(ref, val):
    n = val.shape[0]
    bits = lambda v: lax.bitcast_convert_type(v, jnp.uint32)
    words = (bits(val[:, :D_MODEL // 2]) & jnp.uint32(0xFFFF0000)) | (bits(val[:, D_MODEL // 2:]) >> 16)
    for s in range(PACK_SUB):
        ref[pl.ds(s, n, stride=PACK_SUB), :] = words[:, s * LANES:(s + 1) * LANES]


def _unpack_rows(ref):
    n = ref.shape[0] // PACK_SUB
    words = [ref[pl.ds(s, n, stride=PACK_SUB), :] for s in range(PACK_SUB)]
    as_f32 = lambda w: lax.bitcast_convert_type(w, F32)
    hi = [as_f32(w & jnp.uint32(0xFFFF0000)) for w in words]
    lo = [as_f32(w << 16) for w in words]
    return jnp.concatenate(hi + lo, axis=-1).astype(BF16)


def _split3(x):
    hi = x.astype(BF16)
    r1 = x - hi.astype(F32)
    mid = r1.astype(BF16)
    lo = (r1 - mid.astype(F32)).astype(BF16)
    return hi, mid, lo


def _in_proj_kernel(x_ref, g_ref, w_ref, z_ref, xbc_ref, dt_ref, q_ref, k_ref, v_ref):
    h = _rms(x_ref[...], g_ref[...], NORM_EPS).astype(BF16)
    dt0 = OFF_DT
    q0 = OFF_DT + LANES
    k0, v0 = q0 + ATT_WIDTH, q0 + 2 * ATT_WIDTH
    z_ref[...] = _dot(h, w_ref[:, OFF_Z:OFF_XBC])
    xbc_ref[...] = _dot(h, w_ref[:, OFF_XBC:OFF_DT])
    dt_ref[...] = _dot(h, w_ref[:, dt0:q0])
    q_ref[...] = (_dot(h, w_ref[:, q0:k0]) * (ATT_HEAD_DIM ** -0.5 * math.log2(math.e))).astype(BF16)
    k_ref[...] = _dot(h, w_ref[:, k0:v0]).astype(BF16)
    v_ref[...] = _dot(h, w_ref[:, v0:v0 + ATT_WIDTH]).astype(BF16)


def _in_proj(x2, g_mix, w_in):
    t = x2.shape[0]
    w_all = jnp.concatenate([w_in[:, :OFF_Q], jnp.zeros((D_MODEL, LANES - SSD_HEADS), w_in.dtype), w_in[:, OFF_Q:]],
                            axis=1).astype(BF16)
    tm = TM_PROJ
    row = lambda n: pl.BlockSpec((tm, n), lambda i: (i, 0))
    full = lambda a: pl.BlockSpec(a.shape, lambda i: (0, 0))
    return pl.pallas_call(
        _in_proj_kernel,
        grid=(t // tm,),
        in_specs=[row(D_MODEL), full(g_mix), full(w_all)],
        out_specs=[row(SSD_WIDTH), row(SSD_CONV_CH), row(LANES), row(ATT_WIDTH), row(ATT_WIDTH), row(ATT_WIDTH)],
        out_shape=[
            jax.ShapeDtypeStruct((t, SSD_WIDTH), F32),
            jax.ShapeDtypeStruct((t, SSD_CONV_CH), F32),
            jax.ShapeDtypeStruct((t, LANES), F32),
            jax.ShapeDtypeStruct((t, ATT_WIDTH), BF16),
            jax.ShapeDtypeStruct((t, ATT_WIDTH), BF16),
            jax.ShapeDtypeStruct((t, ATT_WIDTH), BF16),
        ],
        compiler_params=_cparams(("arbitrary",)),
        name="in_proj",
    )(x2, g_mix, w_all)


def _ssd_chunk_init(c, xpad_ref, state_ref):
    L = SSD_CHUNK

    @pl.when(c == 0)
    def _():
        xpad_ref[0:SUBLANES, :] = jnp.zeros((SUBLANES, SSD_CONV_CH), F32)
        state_ref[...] = jnp.zeros_like(state_ref)

    @pl.when(c != 0)
    def _():
        xpad_ref[0:SUBLANES, :] = xpad_ref[L:L + SUBLANES, :]


def _ssd_chunk(xbc_ref, dtr_ref, z_ref, cw_ref, cb_ref, dtb_ref, alog_ref, dskip_ref, nw_ref,
               y_ref, xpad_ref, state_ref):
    L = SSD_CHUNK
    xpad_ref[SUBLANES:SUBLANES + L, :] = xbc_ref[...]

    conv = cb_ref[...]
    for j in range(SSD_CONV):
        off = SUBLANES - (SSD_CONV - 1) + j
        conv = conv + cw_ref[j:j + 1, :] * xpad_ref[off:off + L, :]
    act = conv * jax.nn.sigmoid(conv)
    xs = act[:, :SSD_WIDTH]
    bm = act[:, SSD_WIDTH:SSD_WIDTH + SSD_GROUPS * SSD_STATE].astype(BF16)
    cm = act[:, SSD_WIDTH + SSD_GROUPS * SSD_STATE:].astype(BF16)

    dt_in = dtr_ref[...] + dtb_ref[...]
    dt_all = jnp.maximum(dt_in, 0.0) + jnp.log1p(jnp.exp(-jnp.abs(dt_in)))
    adt = dt_all * (-jnp.exp(alog_ref[...]))

    ri = lax.broadcasted_iota(jnp.int32, (L, L), 0)
    ci = lax.broadcasted_iota(jnp.int32, (L, L), 1)
    causal = ci <= ri
    tril = jnp.where(causal, 1.0, 0.0).astype(BF16)
    hi, mid, lo = _split3(adt)
    acum_all = _dot(tril, hi) + _dot(tril, mid) + _dot(tril, lo)
    acum_t = acum_all.T
    a_last = acum_all[L - 1:L, :]
    decay_in_all = jnp.exp(a_last - acum_all)
    decay_out_all = jnp.exp(acum_all)
    chunk_decay_all = jnp.exp(a_last)
    acum = lambda h: acum_all[:, h:h + 1]
    dt = lambda h: dt_all[:, h:h + 1]
    decay_out = lambda h: decay_out_all[:, h:h + 1]
    decay_in = lambda h: decay_in_all[:, h:h + 1]
    chunk_decay = lambda h: chunk_decay_all[:, h:h + 1]

    lane = lax.broadcasted_iota(jnp.int32, (L, LANES), 1)
    lo_half = lane < SSD_HEAD_DIM

    def per_pair(col_a, col_b):
        return jnp.where(lo_half, col_a, col_b)

    ys = []
    for pair in range(SSD_HEADS // 2):
        g = pair // 2
        h0, h1 = 2 * pair, 2 * pair + 1
        cg = cm[:, g * SSD_STATE:(g + 1) * SSD_STATE]
        bg = bm[:, g * SSD_STATE:(g + 1) * SSD_STATE]
        cb = _dot_nt(cg, bg)
        x_pair = xs[:, pair * LANES:(pair + 1) * LANES]
        xdt = x_pair * per_pair(dt(h0), dt(h1))
        y_pair = jnp.zeros((L, LANES), F32)
        for hh, keep in ((h0, lo_half), (h1, jnp.logical_not(lo_half))):
            seg = acum(hh) - acum_t[hh:hh + 1, :]
            lmat = jnp.where(causal, jnp.exp(jnp.where(causal, seg, 0.0)), 0.0)
            m = (cb * lmat).astype(BF16)
            y_pair = y_pair + _dot(m, jnp.where(keep, xdt, 0.0).astype(BF16))
        s_prev = state_ref[pair]
        y_off = _dot(cg, s_prev.astype(BF16)) * per_pair(decay_out(h0), decay_out(h1))
        w_in = (xdt * per_pair(decay_in(h0), decay_in(h1))).astype(BF16)
        cd = jnp.where(lane[0:1, :] < SSD_HEAD_DIM, chunk_decay(h0), chunk_decay(h1))
        state_ref[pair] = s_prev * cd + _dot_tn(bg, w_in)
        ys.append(y_pair + y_off + dskip_ref[:, pair * LANES:(pair + 1) * LANES] * x_pair)

    y = jnp.concatenate(ys, axis=-1)
    zz = z_ref[...]
    y = y * (zz * jax.nn.sigmoid(zz))
    gw = SSD_WIDTH // SSD_GROUPS
    outs = []
    for g in range(SSD_GROUPS):
        yg = y[:, g * gw:(g + 1) * gw]
        outs.append(yg * lax.rsqrt(jnp.mean(yg * yg, axis=-1, keepdims=True) + SSD_NORM_EPS))
    y_ref[...] = (jnp.concatenate(outs, axis=-1) * nw_ref[...]).astype(BF16)


def _attn_ssd_kernel(q_ref, k_ref, v_ref, lq1_ref, lk1_ref, lq2_ref, lk2_ref, sw_ref,
                     xbc_ref, dtr_ref, z_ref, cw_ref, cb_ref, dtb_ref, alog_ref, dskip_ref, nw_ref,
                     o_ref, y_ref, s_scr, vt_scr, xpad_ref, state_ref):
    bq, bk = ATT_BQ, ATT_BK
    qi = pl.program_id(2)
    n_maps = 2
    _ssd_chunk_init(pl.program_id(1) * pl.num_programs(2) + qi, xpad_ref, state_ref)

    @pl.when(qi == 0)
    def _():
        vt_scr[...] = v_ref[...].astype(F32).T.astype(BF16)

    def fold(t, reduce):
        return reduce(t.reshape(bk // SUBLANES, SUBLANES, bq), axis=0)

    def merge(old, new, op):
        return new if old is None else op(old, new)

    def attend(nk):
        _ssd_chunk(xbc_ref, dtr_ref, z_ref, cw_ref, cb_ref, dtb_ref, alog_ref, dskip_ref, nw_ref,
                   y_ref, xpad_ref, state_ref)
        q = q_ref[...]
        lane = lax.broadcasted_iota(jnp.int32, (bq, LANES), 1)
        zero = jnp.zeros_like(q)
        q_maps = (jnp.where(lane < ATT_HEAD_DIM, q, zero), jnp.where(lane >= ATT_HEAD_DIM, q, zero))
        key = lax.broadcasted_iota(jnp.int32, (bk, bq), 0)
        qry = lax.broadcasted_iota(jnp.int32, (bk, bq), 1)
        causal = key <= qry

        mt = [None] * n_maps
        for j in range(nk):
            kb = k_ref[j * bk:(j + 1) * bk, :]
            for m in range(n_maps):
                s = _dot_nt(kb, q_maps[m])
                if j == nk - 1:
                    s = jnp.where(causal, s, -jnp.inf)
                s_scr[m, j] = s
                mt[m] = merge(mt[m], fold(s, jnp.max), jnp.maximum)
        q_max = [jnp.max(t, axis=0, keepdims=True) for t in mt]

        lt = [None] * n_maps
        acc = [None] * n_maps
        for j in range(nk):
            vt = vt_scr[:, j * bk:(j + 1) * bk]
            for m in range(n_maps):
                p = jnp.exp2(s_scr[m, j] - q_max[m])
                lt[m] = merge(lt[m], fold(p, jnp.sum), jnp.add)
                acc[m] = merge(acc[m], _dot(vt, p.astype(BF16)), jnp.add)

        lam = (jnp.exp(jnp.sum(lq1_ref[...] * lk1_ref[...], axis=-1, keepdims=True))
               - jnp.exp(jnp.sum(lq2_ref[...] * lk2_ref[...], axis=-1, keepdims=True)) + LAM_INIT)
        l1 = jnp.sum(lt[0], axis=0, keepdims=True)
        l2 = jnp.sum(lt[1], axis=0, keepdims=True)
        o = acc[0] / l1 - lam * (acc[1] / l2)
        o = o * lax.rsqrt(jnp.mean(o * o, axis=0, keepdims=True) + SUBLN_EPS) * sw_ref[...]
        o_ref[...] = (o * (1.0 - LAM_INIT)).T.astype(BF16)

    for nk in range(1, k_ref.shape[0] // bk + 1):
        pl.when(qi == nk - 1)(lambda nk=nk: attend(nk))


def _attn_ssd(q, k, v, lam_q1, lam_k1, lam_q2, lam_k2, subln_w, xbc, dtr, z, conv_w, conv_b, dt_bias, a_log, d_skip,
              norm_w, bsz, seq):
    nq = seq // ATT_BQ
    nc = seq // SSD_CHUNK
    assert nc == ATT_HEADS * nq
    qspec = pl.BlockSpec((ATT_BQ, LANES), lambda b, h, i: (b * nq + i, h))
    kvspec = pl.BlockSpec((seq, LANES), lambda b, h, i: (b, h))
    full = lambda a: pl.BlockSpec(a.shape, lambda b, h, i: (0, 0))
    chunk = lambda n: pl.BlockSpec((SSD_CHUNK, n), lambda b, h, i: (b * nc + h * nq + i, 0))
    lams = [a.reshape(1, -1) for a in (lam_q1, lam_k1, lam_q2, lam_k2)]
    sw = subln_w.reshape(-1, 1)
    pad_h = lambda v: jnp.pad(v.reshape(1, SSD_HEADS), ((0, 0), (0, LANES - SSD_HEADS)))
    dskip_lanes = jnp.repeat(d_skip, SSD_HEAD_DIM).reshape(1, SSD_WIDTH)
    consts = (conv_w, conv_b.reshape(1, -1), pad_h(dt_bias), pad_h(a_log), dskip_lanes, norm_w.reshape(1, -1))
    return pl.pallas_call(
        _attn_ssd_kernel,
        grid=(bsz, ATT_HEADS, nq),
        in_specs=[qspec, kvspec, kvspec] + [full(a) for a in lams] + [full(sw)]
                 + [chunk(SSD_CONV_CH), chunk(LANES), chunk(SSD_WIDTH)] + [full(a) for a in consts],
        out_specs=[qspec, chunk(SSD_WIDTH)],
        out_shape=[jax.ShapeDtypeStruct((bsz * seq, ATT_WIDTH), BF16),
                   jax.ShapeDtypeStruct((bsz * seq, SSD_WIDTH), BF16)],
        scratch_shapes=[pltpu.VMEM((2, seq // ATT_BK, ATT_BK, ATT_BQ), F32),
                        pltpu.VMEM((LANES, seq), BF16),
                        pltpu.VMEM((SSD_CHUNK + 2 * SUBLANES, SSD_CONV_CH), F32),
                        pltpu.VMEM((SSD_HEADS // 2, SSD_STATE, LANES), F32)],
        compiler_params=_cparams(("arbitrary", "arbitrary", "arbitrary")),
        name="attn_ssd",
    )(q, k, v, *lams, sw, xbc, dtr, z, *consts)


def _mixer(x2, g_mix, w_in, conv_w, conv_b, dt_bias, a_log, d_skip, ssd_norm_w,
           lam_q1, lam_k1, lam_q2, lam_k2, subln_w, bsz, seq):
    z, xbc, dtr, q, k, v = _in_proj(x2, g_mix.reshape(1, -1), w_in)
    y_att, y_ssd = _attn_ssd(q, k, v, lam_q1, lam_k1, lam_q2, lam_k2, subln_w, xbc, dtr, z, conv_w, conv_b, dt_bias,
                             a_log, d_skip, ssd_norm_w, bsz, seq)
    return y_ssd, y_att


def _out_proj_kernel(x_ref, ys_ref, ya_ref, wo_ref, g_ref, wr2_ref, br_ref,
                     x1_ref, xn_ref, route_ref, gate_ref, cnt_ref):
    tm = TM_ROUTE
    x1 = x_ref[...] + _dot(jnp.concatenate([ys_ref[...], ya_ref[...]], axis=-1), wo_ref[...])
    x1_ref[...] = x1
    xn_all = _rms(x1, g_ref[...], NORM_EPS)
    xn_ref[...] = xn_all.astype(BF16)

    n_tok = OUT_SUB * tm
    xh, xm, _ = _split3(xn_all)
    lg2 = _dot(xh, wr2_ref[...]) + _dot(xm, wr2_ref[...])
    logits = (lg2[:, :LANES] + lg2[:, LANES:]).T[:N_EXPERTS, :] + br_ref[...]

    eidx = lax.broadcasted_iota(jnp.int32, (N_EXPERTS, n_tok), 0).astype(F32)
    work = logits
    vals, idxs, hots = [], [], []
    for _ in range(TOP_K):
        m = jnp.max(work, axis=0, keepdims=True)
        idx = jnp.min(jnp.where(work == m, eidx, float(N_EXPERTS)), axis=0, keepdims=True)
        hot = eidx == idx
        vals.append(m)
        idxs.append(idx.astype(jnp.int32))
        hots.append(hot)
        work = jnp.where(hot, -jnp.inf, work)
    exps = [jnp.exp(v - vals[0]) for v in vals]
    denom = exps[0] + exps[1] + exps[2] + exps[3]
    gates = [e / denom for e in exps]

    cnt = jnp.zeros((N_EXPERTS, n_tok), F32)
    for hot in hots:
        cnt = cnt + jnp.where(hot, 1.0, 0.0)
    cnt_b = cnt.astype(BF16)
    r = lax.broadcasted_iota(jnp.int32, (tm, tm), 0)
    c = lax.broadcasted_iota(jnp.int32, (tm, tm), 1)
    earlier_tok = jnp.where(r < c, 1.0, 0.0).astype(BF16)
    er = lax.broadcasted_iota(jnp.int32, (N_EXPERTS, N_EXPERTS), 0)
    ec = lax.broadcasted_iota(jnp.int32, (N_EXPERTS, N_EXPERTS), 1)
    lower_exp = jnp.where(ec < er, 1.0, 0.0).astype(BF16)
    cnt_pad = jnp.concatenate([cnt_b, jnp.zeros((LANES - N_EXPERTS, n_tok), BF16)], axis=0)
    pos = []
    for sub in range(OUT_SUB):
        cols = slice(sub * tm, (sub + 1) * tm)
        run = jnp.sum(cnt[:, cols], axis=1, keepdims=True)
        run_even = 2.0 * jnp.floor(0.5 * run + 0.5)
        run_start = _dot(lower_exp, jnp.broadcast_to(run_even, (N_EXPERTS, LANES)).astype(BF16))[:, 0:1]
        pos.append(_dot(cnt_b[:, cols], earlier_tok) + run_start)
        cnt_ref[sub] = _dot_nt(jnp.ones((SUBLANES, tm), BF16), cnt_pad[:, cols])
    pos = jnp.concatenate(pos, axis=1)
    lps = [jnp.sum(jnp.where(hot, pos, 0.0), axis=0, keepdims=True).astype(jnp.int32) for hot in hots]

    route_ref[...] = jnp.concatenate(idxs + lps, axis=0)
    gate_ref[...] = jnp.concatenate(gates + [jnp.zeros((SUBLANES - TOP_K, n_tok), F32)], axis=0)


def _out_proj(x2, y_ssd, y_att, w_out, g_ffn, w_router, b_router):
    t = x2.shape[0]
    tm = OUT_SUB * TM_ROUTE
    nt = t // TM_ROUTE
    wo = w_out.astype(BF16)
    wrh, wrm, _ = _split3(jnp.pad(w_router, ((0, 0), (0, LANES - N_EXPERTS))))
    wr2 = jnp.concatenate([wrh, wrm], axis=1)
    br = b_router.reshape(N_EXPERTS, 1)
    row = lambda n: pl.BlockSpec((tm, n), lambda i: (i, 0))
    col = pl.BlockSpec((SUBLANES, tm), lambda i: (0, i))
    full = lambda a: pl.BlockSpec(a.shape, lambda i: (0, 0))
    args = (x2, y_ssd, y_att, wo, g_ffn.reshape(1, -1), wr2, br)
    return pl.pallas_call(
        _out_proj_kernel,
        grid=(t // tm,),
        in_specs=[row(D_MODEL), row(SSD_WIDTH), row(ATT_WIDTH)] + [full(a) for a in args[3:]],
        out_specs=[row(D_MODEL), row(D_MODEL), col, col,
                   pl.BlockSpec((OUT_SUB, SUBLANES, LANES), lambda i: (i, 0, 0))],
        out_shape=[
            jax.ShapeDtypeStruct((t, D_MODEL), F32),
            jax.ShapeDtypeStruct((t, D_MODEL), BF16),
            jax.ShapeDtypeStruct((SUBLANES, t), jnp.int32),
            jax.ShapeDtypeStruct((SUBLANES, t), F32),
            jax.ShapeDtypeStruct((nt, SUBLANES, LANES), F32),
        ],
        compiler_params=_cparams(("arbitrary",)),
        name="out_proj",
    )(*args)


def _copy_run(n, start_copy):
    @pl.when(n > 0)
    def _():
        start_copy(n)


def _select_by_position(positions, values, n_pos):
    tm = positions[0].shape[1]
    r = lax.broadcasted_iota(jnp.int32, (n_pos, tm), 0)
    out = jnp.zeros((n_pos, tm), F32)
    for k in reversed(range(TOP_K)):
        out = jnp.where(r == positions[k], values[k], out)
    return out


def _dispatch_kernel(cnt_ref, ls_ref, base_ref, zoff_ref, rows_ref, xn_ref, route_ref, xs_hbm, xloc, zeros_vmem,
                     sem_z, sems):
    n_pos = N_POS
    i = pl.program_id(0)
    n_steps = pl.num_programs(0)
    slot = lax.rem(i, 2)

    def zero_fill(op):
        def pad_rows(e, _):
            n = zoff_ref[N_EXPERTS + 1 + e]

            @pl.when(n > 0)
            def _():
                op(pltpu.make_async_copy(zeros_vmem.at[_row_slice(0, n, PACK_SUB)],
                                         xs_hbm.at[_row_slice(zoff_ref[e], n, PACK_SUB)], sem_z))
            return 0

        def unused_block(b, _):
            op(pltpu.make_async_copy(zeros_vmem, xs_hbm.at[_row_slice(b * ROW_BLK, ROW_BLK, PACK_SUB)], sem_z))
            return 0

        lax.fori_loop(0, N_EXPERTS, pad_rows, 0)
        lax.fori_loop(zoff_ref[N_EXPERTS], xs_hbm.shape[0] // (ROW_BLK * PACK_SUB), unused_block, 0)

    @pl.when(i == 0)
    def _():
        zeros_vmem[...] = jnp.zeros_like(zeros_vmem)
        zero_fill(lambda cp: cp.start())

    def slot_wait(s, step):
        n = rows_ref[step]
        pltpu.make_async_copy(xloc.at[s, _row_slice(0, n, PACK_SUB)], xs_hbm.at[_row_slice(0, n, PACK_SUB)],
                              sems.at[s]).wait()

    @pl.when(i >= 2)
    def _():
        slot_wait(slot, i - 2)

    positions = [route_ref[TOP_K + k:TOP_K + k + 1, :] for k in range(TOP_K)]
    sel = _select_by_position(positions, [1.0] * TOP_K, n_pos).astype(BF16)
    rows = _dot(sel, xn_ref[...])
    _pack_rows(xloc.at[slot], rows)

    def per_expert(e, _):
        idx = i * N_EXPERTS + e
        src0 = ls_ref[idx]
        dst0 = base_ref[idx]

        def start_copy(n):
            pltpu.make_async_copy(xloc.at[slot, _row_slice(src0, n, PACK_SUB)],
                                  xs_hbm.at[_row_slice(dst0, n, PACK_SUB)], sems.at[slot]).start()

        _copy_run(cnt_ref[idx], start_copy)
        return 0

    lax.fori_loop(0, N_EXPERTS, per_expert, 0)

    @pl.when(i == n_steps - 1)
    def _():
        slot_wait(slot, i)

        @pl.when(n_steps > 1)
        def _():
            slot_wait(1 - slot, i - 1)

        zero_fill(lambda cp: cp.wait())


def _dispatch(xn, route, tables, n_rows):
    t = xn.shape[0]
    tm = TM_ROUTE
    cnt_tbl, ls_tbl, base_tbl, zoff, tile_rows = tables
    return pl.pallas_call(
        _dispatch_kernel,
        grid_spec=pltpu.PrefetchScalarGridSpec(
            num_scalar_prefetch=5,
            grid=(t // tm,),
            in_specs=[pl.BlockSpec((tm, D_MODEL), lambda i, *_: (i, 0)),
                      pl.BlockSpec((SUBLANES, tm), lambda i, *_: (0, i))],
            out_specs=pl.BlockSpec(memory_space=pl.ANY),
            scratch_shapes=[
                pltpu.VMEM((2, N_POS * PACK_SUB, LANES), jnp.uint32),
                pltpu.VMEM((ROW_BLK * PACK_SUB, LANES), jnp.uint32),
                pltpu.SemaphoreType.DMA,
                pltpu.SemaphoreType.DMA((2,)),
            ],
        ),
        out_shape=jax.ShapeDtypeStruct(((n_rows + ROW_BLK) * PACK_SUB, LANES), jnp.uint32),
        compiler_params=_cparams(("arbitrary",)),
        name="dispatch",
    )(cnt_tbl, ls_tbl, base_tbl, zoff, tile_rows, xn, route)


def _experts_kernel(blk_e_ref, nvalid_ref, first_ref, wslot_ref, enext_ref, pieces_ref, xs_ref, wup_hbm, wdn_hbm,
                    bgu_ref, bd_ref, ys_ref, wup_buf, wdn_buf, wg_s, wu_s, wd_s, sems):
    i = pl.program_id(0)
    slot = wslot_ref[i]

    def weight_copies(e, s):
        return (pltpu.make_async_copy(wup_hbm.at[e], wup_buf.at[s], sems.at[0, s]),
                pltpu.make_async_copy(wdn_hbm.at[e], wdn_buf.at[s], sems.at[1, s]))

    @pl.when(i == 0)
    def _():
        for cp in weight_copies(blk_e_ref[0], slot):
            cp.start()

    @pl.when(first_ref[i] != 0)
    def _():
        for cp in weight_copies(blk_e_ref[i], slot):
            cp.wait()

        @pl.when(enext_ref[i] >= 0)
        def _():
            for cp in weight_copies(enext_ref[i], 1 - slot):
                cp.start()

        src = lax.broadcasted_iota(jnp.int32, (DEINT, DEINT), 0)
        dst = lax.broadcasted_iota(jnp.int32, (DEINT, DEINT), 1)
        perm = jnp.where(src == jnp.where(dst < LANES, 2 * dst, 2 * (dst - LANES) + 1), 1.0, 0.0).astype(BF16)
        for g in range(2 * D_EXPERT // DEINT):
            sep = _dot(wup_buf[slot, :, g * DEINT:(g + 1) * DEINT].astype(BF16), perm)
            wg_s[:, g * LANES:(g + 1) * LANES] = sep[:, :LANES].astype(BF16)
            wu_s[:, g * LANES:(g + 1) * LANES] = sep[:, LANES:].astype(BF16)
        wd_s[...] = wdn_buf[slot].astype(BF16)

    def ffn(n_rows):
        used = n_rows * ROW_SUB
        if n_rows:
            xb = _unpack_rows(xs_ref.at[pl.ds(0, n_rows * PACK_SUB)])
            gate = jnp.minimum(_dot(xb, wg_s[...]) + bgu_ref[0, 0:1, :], SWIGLU_LIMIT)
            up = jnp.clip(_dot(xb, wu_s[...]) + bgu_ref[0, 1:2, :], -SWIGLU_LIMIT, SWIGLU_LIMIT)
            act = (up + 1.0) * gate * jax.nn.sigmoid(SWIGLU_ALPHA * gate)
            _store_rows(ys_ref.at[pl.ds(0, used)], _dot(act.astype(BF16), wd_s[...]) + bd_ref[0])
        if n_rows < ROW_BLK:
            ys_ref[used:, :] = jnp.zeros((ROW_BLK * ROW_SUB - used, LANES), F32)

    for pieces in range(ROW_BLK // ROW_PIECE + 1):
        pl.when(pieces_ref[i] == pieces)(lambda pieces=pieces: ffn(pieces * ROW_PIECE))


def _experts(xs, plan, w_up, b_up, w_down, b_down, n_rows):
    nb = n_rows // ROW_BLK
    b_gu = b_up.reshape(N_EXPERTS, D_EXPERT, 2).transpose(0, 2, 1)
    bd = b_down.reshape(N_EXPERTS, 1, D_MODEL)
    blk_e, nvalid, first, wslot, enext, pieces = plan
    src = lambda i, be, nv, *_: (jnp.minimum(i, nv[0] - 1), 0)
    bspec = lambda r, m: pl.BlockSpec((1, r, m), lambda i, be, *_: (be[i], 0, 0))
    anyspec = pl.BlockSpec(memory_space=pl.ANY)
    return pl.pallas_call(
        _experts_kernel,
        grid_spec=pltpu.PrefetchScalarGridSpec(
            num_scalar_prefetch=6,
            grid=(nb,),
            in_specs=[pl.BlockSpec((ROW_BLK * PACK_SUB, LANES), src), anyspec, anyspec,
                      bspec(2, D_EXPERT), bspec(1, D_MODEL)],
            out_specs=_rows_spec(ROW_BLK, lambda i, *_: (i, 0)),
            scratch_shapes=[
                pltpu.VMEM((2, D_MODEL, 2 * D_EXPERT), F32),
                pltpu.VMEM((2, D_EXPERT, D_MODEL), F32),
                pltpu.VMEM((D_MODEL, D_EXPERT), BF16),
                pltpu.VMEM((D_MODEL, D_EXPERT), BF16),
                pltpu.VMEM((D_EXPERT, D_MODEL), BF16),
                pltpu.SemaphoreType.DMA((2, 2)),
            ],
        ),
        out_shape=jax.ShapeDtypeStruct((n_rows * ROW_SUB, LANES), F32),
        compiler_params=_cparams(("arbitrary",)),
        name="experts",
    )(blk_e, nvalid, first, wslot, enext, pieces, xs, w_up, w_down, b_gu, bd)


def _combine_kernel(cnt_ref, ls_ref, base_ref, rows_ref, ys_hbm, route_ref, gate_ref, x1_ref, p_ref, gp_ref, wpg_ref,
                    wpp_ref, gf_ref, o_ref, yloc, sems):
    tm = TM_ROUTE
    n_pos = N_POS
    i = pl.program_id(0)
    n_steps = pl.num_programs(0)
    slot = lax.rem(i, 2)

    def gather_step(step, s):
        for u in range(COMB_SUB):
            def per_expert(e, _, u=u):
                idx = (step * COMB_SUB + u) * N_EXPERTS + e
                src0 = base_ref[idx]
                dst0 = ls_ref[idx]

                def start_copy(n):
                    pltpu.make_async_copy(ys_hbm.at[_row_slice(src0, n)],
                                          yloc.at[s, u, _row_slice(dst0, n)], sems.at[s]).start()

                _copy_run(cnt_ref[idx], start_copy)
                return 0

            lax.fori_loop(0, N_EXPERTS, per_expert, 0)

    @pl.when(i == 0)
    def _():
        yloc[...] = jnp.zeros_like(yloc)
        gather_step(0, 0)

    @pl.when(i + 1 < n_steps)
    def _():
        gather_step(i + 1, 1 - slot)

    for u in range(COMB_SUB):
        n = rows_ref[i * COMB_SUB + u]
        pltpu.make_async_copy(ys_hbm.at[_row_slice(0, n)], yloc.at[slot, u, _row_slice(0, n)], sems.at[slot]).wait()

    moe = []
    for u in range(COMB_SUB):
        cols = slice(u * tm, (u + 1) * tm)
        positions = [route_ref[TOP_K + k:TOP_K + k + 1, cols] for k in range(TOP_K)]
        gsel = _select_by_position(positions, [gate_ref[k:k + 1, cols] for k in range(TOP_K)], n_pos)
        g_pos = jnp.sum(gsel, axis=1, keepdims=True)
        sel = jnp.where(gsel != 0.0, 1.0, 0.0).astype(BF16)
        y_gated = (_load_rows(yloc.at[slot, u]) * g_pos).astype(BF16)
        moe.append(_dot_tn(sel, y_gated))
    x2 = x1_ref[...] + jnp.concatenate(moe, axis=0)
    xn = _rms(x2, gp_ref[...], NORM_EPS).astype(BF16)
    pp = _dot(p_ref[...].astype(BF16), wpp_ref[...])
    x3 = x2 + pp * jax.nn.sigmoid(_dot(xn, wpg_ref[...]))
    o_ref[...] = _rms(x3, gf_ref[...], NORM_EPS)


def _combine(ys, route, gate_t, x1, p2, tables, g_ple, w_ple_gate, w_ple_proj, g_final):
    t = x1.shape[0]
    tm = COMB_SUB * TM_ROUTE
    cnt_tbl, ls_tbl, base_tbl, _, tile_rows = tables
    row = lambda n: pl.BlockSpec((tm, n), lambda i, *_: (i, 0))
    col = pl.BlockSpec((SUBLANES, tm), lambda i, *_: (0, i))
    full = lambda a: pl.BlockSpec(a.shape, lambda i, *_: (0, 0))
    consts = (g_ple.reshape(1, -1), w_ple_gate.astype(BF16), w_ple_proj.astype(BF16), g_final.reshape(1, -1))
    return pl.pallas_call(
        _combine_kernel,
        grid_spec=pltpu.PrefetchScalarGridSpec(
            num_scalar_prefetch=4,
            grid=(t // tm,),
            in_specs=[pl.BlockSpec(memory_space=pl.ANY), col, col, row(D_MODEL), row(PLE_DIM)]
                     + [full(a) for a in consts],
            out_specs=row(D_MODEL),
            scratch_shapes=[
                pltpu.VMEM((2, COMB_SUB, N_POS * ROW_SUB, LANES), F32),
                pltpu.SemaphoreType.DMA((2,)),
            ],
        ),
        out_shape=jax.ShapeDtypeStruct((t, D_MODEL), F32),
        compiler_params=_cparams(("arbitrary",)),
        name="combine",
    )(cnt_tbl, ls_tbl, base_tbl, tile_rows, ys, route, gate_t, x1, p2, *consts)


def _routing_tables(cnt, n_blocks):
    tile_cnt = cnt[:, 0, :N_EXPERTS].astype(jnp.int32)
    tile_cnt = tile_cnt + tile_cnt % 2
    counts = jnp.sum(tile_cnt, axis=0)
    padded = (counts + ROW_BLK - 1) // ROW_BLK * ROW_BLK
    pend = jnp.cumsum(padded)
    pstart = pend - padded
    base = pstart[None, :] + jnp.cumsum(tile_cnt, axis=0) - tile_cnt
    lstart = jnp.cumsum(tile_cnt, axis=1) - tile_cnt
    nvalid = pend[-1:] // ROW_BLK
    zoff = jnp.concatenate([pstart + counts, nvalid, padded - counts])
    blk_start = jnp.minimum(jnp.arange(n_blocks, dtype=jnp.int32) * ROW_BLK, pend[-1] - 1)
    blk_e = jnp.minimum(jnp.sum((pend[None, :] <= blk_start[:, None]).astype(jnp.int32), axis=1), N_EXPERTS - 1)
    first = jnp.concatenate([jnp.ones((1,), bool), blk_e[1:] != blk_e[:-1]])
    wslot = (jnp.cumsum(first.astype(jnp.int32)) - 1) % 2
    eids = jnp.arange(N_EXPERTS, dtype=jnp.int32)
    later_nonempty = (eids[None, :] > eids[:, None]) & (padded[None, :] > 0)
    next_e = jnp.min(jnp.where(later_nonempty, eids[None, :], N_EXPERTS), axis=1)
    next_e = jnp.where(next_e == N_EXPERTS, -1, next_e)
    blk_hot = blk_e[:, None] == eids[None, :]
    per_block = lambda v: jnp.sum(jnp.where(blk_hot, v[None, :], 0), axis=1)
    blk_ids = jnp.arange(n_blocks, dtype=jnp.int32)
    real_rows = jnp.clip(per_block(pstart + counts) - blk_ids * ROW_BLK, 0, ROW_BLK) * (blk_ids < nvalid[0])
    pieces = (real_rows + ROW_PIECE - 1) // ROW_PIECE
    tile_rows = jnp.sum(tile_cnt, axis=1)
    i32 = lambda a: a.reshape(-1).astype(jnp.int32)
    plan = (i32(blk_e), i32(nvalid), i32(first), i32(wslot), i32(per_block(next_e)), i32(pieces))
    return (i32(tile_cnt), i32(lstart), i32(base), i32(zoff), i32(tile_rows)), plan


def kernel(x, p, g_mix, w_in, conv_w, conv_b, dt_bias, a_log, d_skip, ssd_norm_w, lam_q1, lam_k1, lam_q2, lam_k2, subln_w, w_out, g_ffn, w_router, b_router, w_up, b_up, w_down, b_down, g_ple, w_ple_gate, w_ple_proj, g_final):
    bsz, seq, d = x.shape
    t = bsz * seq
    x2 = x.reshape(t, d)
    y_ssd, y_att = _mixer(x2, g_mix[0], w_in[0], conv_w[0], conv_b[0], dt_bias[0], a_log[0], d_skip[0], ssd_norm_w[0],
                          lam_q1[0], lam_k1[0], lam_q2[0], lam_k2[0], subln_w[0], bsz, seq)
    x1, xn, route, gate_t, cnt = _out_proj(x2, y_ssd, y_att, w_out[0], g_ffn[0], w_router[0], b_router[0])

    n_rows = t * TOP_K + (t // TM_ROUTE) * N_EXPERTS + N_EXPERTS * ROW_BLK
    n_rows = -(-n_rows // ROW_BLK) * ROW_BLK
    tables, plan = _routing_tables(cnt, n_rows // ROW_BLK)
    xs = _dispatch(xn, route, tables, n_rows)
    ys = _experts(xs, plan, w_up[0], b_up[0], w_down[0], b_down[0], n_rows)
    out = _combine(ys, route, gate_t, x1, p[0].reshape(t, PLE_DIM), tables, g_ple[0], w_ple_gate[0], w_ple_proj[0],
                   g_final)
    return out.reshape(bsz, seq, d)
```

```python
import math

import jax
import jax.numpy as jnp
from jax import lax
from jax.experimental import pallas as pl
from jax.experimental.pallas import tpu as pltpu

F32 = jnp.float32
BF16 = jnp.bfloat16

D_MODEL = 1024
PLE_DIM = 256
SSD_WIDTH = 512
ATT_WIDTH = 512
SSD_HEAD_DIM = 64
SSD_HEADS = 8
SSD_GROUPS = 2
SSD_STATE = 128
SSD_CONV = 4
SSD_CHUNK = 128
SSD_CONV_CH = SSD_WIDTH + 2 * SSD_GROUPS * SSD_STATE
SSD_NORM_EPS = 1e-5
ATT_HEAD_DIM = 64
ATT_HEADS = 4
SUBLN_EPS = 1e-5
OFF_Z = 0
OFF_XBC = OFF_Z + SSD_WIDTH
OFF_DT = OFF_XBC + SSD_CONV_CH
OFF_Q = OFF_DT + SSD_HEADS
OFF_K = OFF_Q + ATT_WIDTH
OFF_V = OFF_K + ATT_WIDTH
IN_PROJ = OFF_V + ATT_WIDTH
N_EXPERTS = 32
TOP_K = 4
D_EXPERT = 1024
SWIGLU_LIMIT = 7.0
SWIGLU_ALPHA = 1.702
NORM_EPS = 1e-6
LAM_INIT = 0.8 - 0.6 * math.exp(-0.3 * 0)

LANES = 128
SUBLANES = 8
VMEM_LIMIT_BYTES = 56 * 1024 * 1024
ROW_SUB = D_MODEL // LANES
PACK_SUB = ROW_SUB // 2

TM_PROJ = 512
ATT_BQ = 512
ATT_BK = 512
ROW_BLK = 512
ROW_PIECE = 128
TM_ROUTE = 256
N_POS = TOP_K * TM_ROUTE + N_EXPERTS
OUT_SUB = 4
COMB_SUB = 2
DEINT = 2 * LANES


def _cparams(sem):
    return pltpu.CompilerParams(dimension_semantics=sem, vmem_limit_bytes=VMEM_LIMIT_BYTES)


def _rms(x, w, eps):
    return x * lax.rsqrt(jnp.mean(x * x, axis=-1, keepdims=True) + eps) * w


def _dot(a, b):
    return jnp.dot(a, b, preferred_element_type=F32)


def _dot_nt(a, b):
    return lax.dot_general(a, b, (((1,), (1,)), ((), ())), preferred_element_type=F32)


def _dot_tn(a, b):
    return lax.dot_general(a, b, (((0,), (0,)), ((), ())), preferred_element_type=F32)


def _store_rows(ref, val):
    n = val.shape[0]
    for s in range(ROW_SUB):
        ref[pl.ds(s, n, stride=ROW_SUB), :] = val[:, s * LANES:(s + 1) * LANES]


def _load_rows(ref):
    n = ref.shape[0] // ROW_SUB
    return jnp.concatenate([ref[pl.ds(s, n, stride=ROW_SUB), :] for s in range(ROW_SUB)], axis=-1)


def _rows_spec(n, index_map):
    return pl.BlockSpec((n * ROW_SUB, LANES), index_map)


def _row_slice(start, n, sub=ROW_SUB):
    return pl.ds(pl.multiple_of(start * sub, SUBLANES), n * sub)


def _pack_rows(ref, val):
    n = val.shape[0]
    bits = lambda v: lax.bitcast_convert_type(v, jnp.uint32)
    words = (bits(val[:, :D_MODEL // 2]) & jnp.uint32(0xFFFF0000)) | (bits(val[:, D_MODEL // 2:]) >> 16)
    for s in range(PACK_SUB):
        ref[pl.ds(s, n, stride=PACK_SUB), :] = words[:, s * LANES:(s + 1) * LANES]


def _unpack_rows(ref):
    n = ref.shape[0] // PACK_SUB
    words = [ref[pl.ds(s, n, stride=PACK_SUB), :] for s in range(PACK_SUB)]
    as_f32 = lambda w: lax.bitcast_convert_type(w, F32)
    hi = [as_f32(w & jnp.uint32(0xFFFF0000)) for w in words]
    lo = [as_f32(w << 16) for w in words]
    return jnp.concatenate(hi + lo, axis=-1).astype(BF16)


def _split3(x):
    hi = x.astype(BF16)
    r1 = x - hi.astype(F32)
    mid = r1.astype(BF16)
    lo = (r1 - mid.astype(F32)).astype(BF16)
    return hi, mid, lo


def _in_proj_kernel(x_ref, g_ref, win_ref, z_ref, xbc_ref, dt_ref, q_ref, k_ref, v_ref, w_ref):
    @pl.when(pl.program_id(0) == 0)
    def _():
        w_ref[:, :OFF_Q] = win_ref[:, :OFF_Q].astype(BF16)
        w_ref[:, OFF_Q:OFF_DT + LANES] = jnp.zeros((D_MODEL, LANES - SSD_HEADS), BF16)
        w_ref[:, OFF_DT + LANES:] = win_ref[:, OFF_Q:].astype(BF16)

    h = _rms(x_ref[...], g_ref[...], NORM_EPS).astype(BF16)
    dt0 = OFF_DT
    q0 = OFF_DT + LANES
    k0, v0 = q0 + ATT_WIDTH, q0 + 2 * ATT_WIDTH
    z_ref[...] = _dot(h, w_ref[:, OFF_Z:OFF_XBC])
    xbc_ref[...] = _dot(h, w_ref[:, OFF_XBC:OFF_DT])
    dt_ref[...] = _dot(h, w_ref[:, dt0:q0])
    q_ref[...] = (_dot(h, w_ref[:, q0:k0]) * (ATT_HEAD_DIM ** -0.5 * math.log2(math.e))).astype(BF16)
    k_ref[...] = _dot(h, w_ref[:, k0:v0]).astype(BF16)
    v_ref[...] = _dot(h, w_ref[:, v0:v0 + ATT_WIDTH]).astype(BF16)


def _in_proj(x2, g_mix, w_in):
    t = x2.shape[0]
    tm = TM_PROJ
    row = lambda n: pl.BlockSpec((tm, n), lambda i: (i, 0))
    full = lambda a: pl.BlockSpec(a.shape, lambda i: (0, 0))
    return pl.pallas_call(
        _in_proj_kernel,
        grid=(t // tm,),
        in_specs=[row(D_MODEL), full(g_mix), full(w_in)],
        out_specs=[row(SSD_WIDTH), row(SSD_CONV_CH), row(LANES), row(ATT_WIDTH), row(ATT_WIDTH), row(ATT_WIDTH)],
        out_shape=[
            jax.ShapeDtypeStruct((t, SSD_WIDTH), F32),
            jax.ShapeDtypeStruct((t, SSD_CONV_CH), F32),
            jax.ShapeDtypeStruct((t, LANES), F32),
            jax.ShapeDtypeStruct((t, ATT_WIDTH), BF16),
            jax.ShapeDtypeStruct((t, ATT_WIDTH), BF16),
            jax.ShapeDtypeStruct((t, ATT_WIDTH), BF16),
        ],
        scratch_shapes=[pltpu.VMEM((D_MODEL, IN_PROJ + LANES - SSD_HEADS), BF16)],
        compiler_params=_cparams(("arbitrary",)),
        name="in_proj",
    )(x2, g_mix, w_in)


def _ssd_chunk_init(c, xpad_ref, state_ref):
    L = SSD_CHUNK

    @pl.when(c == 0)
    def _():
        xpad_ref[0:SUBLANES, :] = jnp.zeros((SUBLANES, SSD_CONV_CH), F32)
        state_ref[...] = jnp.zeros_like(state_ref)

    @pl.when(c != 0)
    def _():
        xpad_ref[0:SUBLANES, :] = xpad_ref[L:L + SUBLANES, :]


def _ssd_chunk(xbc_ref, dtr_ref, z_ref, cw_ref, cb_ref, dtb_ref, alog_ref, dskip_ref, nw_ref,
               y_ref, xpad_ref, state_ref):
    L = SSD_CHUNK
    xpad_ref[SUBLANES:SUBLANES + L, :] = xbc_ref[...]

    conv = cb_ref[...]
    for j in range(SSD_CONV):
        off = SUBLANES - (SSD_CONV - 1) + j
        conv = conv + cw_ref[j:j + 1, :] * xpad_ref[off:off + L, :]
    act = conv * jax.nn.sigmoid(conv)
    xs = act[:, :SSD_WIDTH]
    bm = act[:, SSD_WIDTH:SSD_WIDTH + SSD_GROUPS * SSD_STATE].astype(BF16)
    cm = act[:, SSD_WIDTH + SSD_GROUPS * SSD_STATE:].astype(BF16)

    dt_in = dtr_ref[...] + dtb_ref[...]
    dt_all = jnp.maximum(dt_in, 0.0) + jnp.log1p(jnp.exp(-jnp.abs(dt_in)))
    adt = dt_all * (-jnp.exp(alog_ref[...]))

    ri = lax.broadcasted_iota(jnp.int32, (L, L), 0)
    ci = lax.broadcasted_iota(jnp.int32, (L, L), 1)
    causal = ci <= ri
    tril = jnp.where(causal, 1.0, 0.0).astype(BF16)
    hi, mid, lo = _split3(adt)
    acum_all = _dot(tril, hi) + _dot(tril, mid) + _dot(tril, lo)
    acum_t = acum_all.T
    a_last = acum_all[L - 1:L, :]
    decay_in_all = jnp.exp(a_last - acum_all)
    decay_out_all = jnp.exp(acum_all)
    chunk_decay_all = jnp.exp(a_last)
    acum = lambda h: acum_all[:, h:h + 1]
    dt = lambda h: dt_all[:, h:h + 1]
    decay_out = lambda h: decay_out_all[:, h:h + 1]
    decay_in = lambda h: decay_in_all[:, h:h + 1]
    chunk_decay = lambda h: chunk_decay_all[:, h:h + 1]

    lane = lax.broadcasted_iota(jnp.int32, (L, LANES), 1)
    lo_half = lane < SSD_HEAD_DIM

    def per_pair(col_a, col_b):
        return jnp.where(lo_half, col_a, col_b)

    ys = []
    for pair in range(SSD_HEADS // 2):
        g = pair // 2
        h0, h1 = 2 * pair, 2 * pair + 1
        cg = cm[:, g * SSD_STATE:(g + 1) * SSD_STATE]
        bg = bm[:, g * SSD_STATE:(g + 1) * SSD_STATE]
        cb = _dot_nt(cg, bg)
        x_pair = xs[:, pair * LANES:(pair + 1) * LANES]
        xdt = x_pair * per_pair(dt(h0), dt(h1))
        y_pair = jnp.zeros((L, LANES), F32)
        for hh, keep in ((h0, lo_half), (h1, jnp.logical_not(lo_half))):
            seg = acum(hh) - acum_t[hh:hh + 1, :]
            lmat = jnp.where(causal, jnp.exp(jnp.where(causal, seg, 0.0)), 0.0)
            m = (cb * lmat).astype(BF16)
            y_pair = y_pair + _dot(m, jnp.where(keep, xdt, 0.0).astype(BF16))
        s_prev = state_ref[pair]
        y_off = _dot(cg, s_prev.astype(BF16)) * per_pair(decay_out(h0), decay_out(h1))
        w_in = (xdt * per_pair(decay_in(h0), decay_in(h1))).astype(BF16)
        cd = jnp.where(lane[0:1, :] < SSD_HEAD_DIM, chunk_decay(h0), chunk_decay(h1))
        state_ref[pair] = s_prev * cd + _dot_tn(bg, w_in)
        ys.append(y_pair + y_off + dskip_ref[:, pair * LANES:(pair + 1) * LANES] * x_pair)

    y = jnp.concatenate(ys, axis=-1)
    zz = z_ref[...]
    y = y * (zz * jax.nn.sigmoid(zz))
    gw = SSD_WIDTH // SSD_GROUPS
    outs = []
    for g in range(SSD_GROUPS):
        yg = y[:, g * gw:(g + 1) * gw]
        outs.append(yg * lax.rsqrt(jnp.mean(yg * yg, axis=-1, keepdims=True) + SSD_NORM_EPS))
    y_ref[...] = (jnp.concatenate(outs, axis=-1) * nw_ref[...]).astype(BF16)


def _attn_ssd_kernel(q_ref, k_ref, v_ref, lq1_ref, lk1_ref, lq2_ref, lk2_ref, sw_ref,
                     xbc_ref, dtr_ref, z_ref, cw_ref, cb_ref, dtb_ref, alog_ref, dskip_ref, nw_ref,
                     o_ref, y_ref, s_scr, vt_scr, xpad_ref, state_ref):
    bq, bk = ATT_BQ, ATT_BK
    qi = pl.program_id(2)
    n_maps = 2
    _ssd_chunk_init(pl.program_id(1) * pl.num_programs(2) + qi, xpad_ref, state_ref)

    @pl.when(qi == 0)
    def _():
        vt_scr[...] = v_ref[...].astype(F32).T.astype(BF16)

    def fold(t, reduce):
        return reduce(t.reshape(bk // SUBLANES, SUBLANES, bq), axis=0)

    def merge(old, new, op):
        return new if old is None else op(old, new)

    def attend(nk):
        _ssd_chunk(xbc_ref, dtr_ref, z_ref, cw_ref, cb_ref, dtb_ref, alog_ref, dskip_ref, nw_ref,
                   y_ref, xpad_ref, state_ref)
        q = q_ref[...]
        lane = lax.broadcasted_iota(jnp.int32, (bq, LANES), 1)
        zero = jnp.zeros_like(q)
        q_maps = (jnp.where(lane < ATT_HEAD_DIM, q, zero), jnp.where(lane >= ATT_HEAD_DIM, q, zero))
        key = lax.broadcasted_iota(jnp.int32, (bk, bq), 0)
        qry = lax.broadcasted_iota(jnp.int32, (bk, bq), 1)
        causal = key <= qry

        mt = [None] * n_maps
        for j in range(nk):
            kb = k_ref[j * bk:(j + 1) * bk, :]
            for m in range(n_maps):
                s = _dot_nt(kb, q_maps[m])
                if j == nk - 1:
                    s = jnp.where(causal, s, -jnp.inf)
                s_scr[m, j] = s
                mt[m] = merge(mt[m], fold(s, jnp.max), jnp.maximum)
        q_max = [jnp.max(t, axis=0, keepdims=True) for t in mt]

        lt = [None] * n_maps
        acc = [None] * n_maps
        for j in range(nk):
            vt = vt_scr[:, j * bk:(j + 1) * bk]
            for m in range(n_maps):
                p = jnp.exp2(s_scr[m, j] - q_max[m])
                lt[m] = merge(lt[m], fold(p, jnp.sum), jnp.add)
                acc[m] = merge(acc[m], _dot(vt, p.astype(BF16)), jnp.add)

        lam = (jnp.exp(jnp.sum(lq1_ref[...] * lk1_ref[...], axis=-1, keepdims=True))
               - jnp.exp(jnp.sum(lq2_ref[...] * lk2_ref[...], axis=-1, keepdims=True)) + LAM_INIT)
        l1 = jnp.sum(lt[0], axis=0, keepdims=True)
        l2 = jnp.sum(lt[1], axis=0, keepdims=True)
        o = acc[0] / l1 - lam * (acc[1] / l2)
        o = o * lax.rsqrt(jnp.mean(o * o, axis=0, keepdims=True) + SUBLN_EPS) * sw_ref[...]
        o_ref[...] = (o * (1.0 - LAM_INIT)).T.astype(BF16)

    for nk in range(1, k_ref.shape[0] // bk + 1):
        pl.when(qi == nk - 1)(lambda nk=nk: attend(nk))


def _attn_ssd(q, k, v, lam_q1, lam_k1, lam_q2, lam_k2, subln_w, xbc, dtr, z, conv_w, conv_b, dt_bias, a_log, d_skip,
              norm_w, bsz, seq):
    nq = seq // ATT_BQ
    nc = seq // SSD_CHUNK
    assert nc == ATT_HEADS * nq
    qspec = pl.BlockSpec((ATT_BQ, LANES), lambda b, h, i: (b * nq + i, h))
    kvspec = pl.BlockSpec((seq, LANES), lambda b, h, i: (b, h))
    full = lambda a: pl.BlockSpec(a.shape, lambda b, h, i: (0, 0))
    chunk = lambda n: pl.BlockSpec((SSD_CHUNK, n), lambda b, h, i: (b * nc + h * nq + i, 0))
    lams = [a.reshape(1, -1) for a in (lam_q1, lam_k1, lam_q2, lam_k2)]
    sw = subln_w.reshape(-1, 1)
    pad_h = lambda v: jnp.pad(v.reshape(1, SSD_HEADS), ((0, 0), (0, LANES - SSD_HEADS)))
    dskip_lanes = jnp.repeat(d_skip, SSD_HEAD_DIM).reshape(1, SSD_WIDTH)
    consts = (conv_w, conv_b.reshape(1, -1), pad_h(dt_bias), pad_h(a_log), dskip_lanes, norm_w.reshape(1, -1))
    return pl.pallas_call(
        _attn_ssd_kernel,
        grid=(bsz, ATT_HEADS, nq),
        in_specs=[qspec, kvspec, kvspec] + [full(a) for a in lams] + [full(sw)]
                 + [chunk(SSD_CONV_CH), chunk(LANES), chunk(SSD_WIDTH)] + [full(a) for a in consts],
        out_specs=[qspec, chunk(SSD_WIDTH)],
        out_shape=[jax.ShapeDtypeStruct((bsz * seq, ATT_WIDTH), BF16),
                   jax.ShapeDtypeStruct((bsz * seq, SSD_WIDTH), BF16)],
        scratch_shapes=[pltpu.VMEM((2, seq // ATT_BK, ATT_BK, ATT_BQ), F32),
                        pltpu.VMEM((LANES, seq), BF16),
                        pltpu.VMEM((SSD_CHUNK + 2 * SUBLANES, SSD_CONV_CH), F32),
                        pltpu.VMEM((SSD_HEADS // 2, SSD_STATE, LANES), F32)],
        compiler_params=_cparams(("arbitrary", "arbitrary", "arbitrary")),
        name="attn_ssd",
    )(q, k, v, *lams, sw, xbc, dtr, z, *consts)


def _mixer(x2, g_mix, w_in, conv_w, conv_b, dt_bias, a_log, d_skip, ssd_norm_w,
           lam_q1, lam_k1, lam_q2, lam_k2, subln_w, bsz, seq):
    z, xbc, dtr, q, k, v = _in_proj(x2, g_mix.reshape(1, -1), w_in)
    y_att, y_ssd = _attn_ssd(q, k, v, lam_q1, lam_k1, lam_q2, lam_k2, subln_w, xbc, dtr, z, conv_w, conv_b, dt_bias,
                             a_log, d_skip, ssd_norm_w, bsz, seq)
    return y_ssd, y_att


def _out_proj_kernel(x_ref, ys_ref, ya_ref, wo_ref, g_ref, wr2_ref, br_ref,
                     x1_ref, xn_ref, route_ref, gate_ref, cnt_ref):
    tm = TM_ROUTE
    x1 = x_ref[...] + _dot(jnp.concatenate([ys_ref[...], ya_ref[...]], axis=-1), wo_ref[...])
    x1_ref[...] = x1
    xn_all = _rms(x1, g_ref[...], NORM_EPS)
    xn_ref[...] = xn_all.astype(BF16)

    n_tok = OUT_SUB * tm
    xh, xm, _ = _split3(xn_all)
    lg2 = _dot(xh, wr2_ref[...]) + _dot(xm, wr2_ref[...])
    logits = (lg2[:, :LANES] + lg2[:, LANES:]).T[:N_EXPERTS, :] + br_ref[...]

    eidx = lax.broadcasted_iota(jnp.int32, (N_EXPERTS, n_tok), 0).astype(F32)
    work = logits
    vals, idxs, hots = [], [], []
    for _ in range(TOP_K):
        m = jnp.max(work, axis=0, keepdims=True)
        idx = jnp.min(jnp.where(work == m, eidx, float(N_EXPERTS)), axis=0, keepdims=True)
        hot = eidx == idx
        vals.append(m)
        idxs.append(idx.astype(jnp.int32))
        hots.append(hot)
        work = jnp.where(hot, -jnp.inf, work)
    exps = [jnp.exp(v - vals[0]) for v in vals]
    denom = exps[0] + exps[1] + exps[2] + exps[3]
    gates = [e / denom for e in exps]

    cnt = jnp.zeros((N_EXPERTS, n_tok), F32)
    for hot in hots:
        cnt = cnt + jnp.where(hot, 1.0, 0.0)
    cnt_b = cnt.astype(BF16)
    r = lax.broadcasted_iota(jnp.int32, (tm, tm), 0)
    c = lax.broadcasted_iota(jnp.int32, (tm, tm), 1)
    earlier_tok = jnp.where(r < c, 1.0, 0.0).astype(BF16)
    er = lax.broadcasted_iota(jnp.int32, (N_EXPERTS, N_EXPERTS), 0)
    ec = lax.broadcasted_iota(jnp.int32, (N_EXPERTS, N_EXPERTS), 1)
    lower_exp = jnp.where(ec < er, 1.0, 0.0).astype(BF16)
    cnt_pad = jnp.concatenate([cnt_b, jnp.zeros((LANES - N_EXPERTS, n_tok), BF16)], axis=0)
    pos = []
    for sub in range(OUT_SUB):
        cols = slice(sub * tm, (sub + 1) * tm)
        run = jnp.sum(cnt[:, cols], axis=1, keepdims=True)
        run_even = 2.0 * jnp.floor(0.5 * run + 0.5)
        run_start = _dot(lower_exp, jnp.broadcast_to(run_even, (N_EXPERTS, LANES)).astype(BF16))[:, 0:1]
        pos.append(_dot(cnt_b[:, cols], earlier_tok) + run_start)
        cnt_ref[sub] = _dot_nt(jnp.ones((SUBLANES, tm), BF16), cnt_pad[:, cols])
    pos = jnp.concatenate(pos, axis=1)
    lps = [jnp.sum(jnp.where(hot, pos, 0.0), axis=0, keepdims=True).astype(jnp.int32) for hot in hots]

    route_ref[...] = jnp.concatenate(idxs + lps, axis=0)
    gate_ref[...] = jnp.concatenate(gates + [jnp.zeros((SUBLANES - TOP_K, n_tok), F32)], axis=0)


def _out_proj(x2, y_ssd, y_att, w_out, g_ffn, w_router, b_router):
    t = x2.shape[0]
    tm = OUT_SUB * TM_ROUTE
    nt = t // TM_ROUTE
    wo = w_out.astype(BF16)
    wrh, wrm, _ = _split3(jnp.pad(w_router, ((0, 0), (0, LANES - N_EXPERTS))))
    wr2 = jnp.concatenate([wrh, wrm], axis=1)
    br = b_router.reshape(N_EXPERTS, 1)
    row = lambda n: pl.BlockSpec((tm, n), lambda i: (i, 0))
    col = pl.BlockSpec((SUBLANES, tm), lambda i: (0, i))
    full = lambda a: pl.BlockSpec(a.shape, lambda i: (0, 0))
    args = (x2, y_ssd, y_att, wo, g_ffn.reshape(1, -1), wr2, br)
    return pl.pallas_call(
        _out_proj_kernel,
        grid=(t // tm,),
        in_specs=[row(D_MODEL), row(SSD_WIDTH), row(ATT_WIDTH)] + [full(a) for a in args[3:]],
        out_specs=[row(D_MODEL), row(D_MODEL), col, col,
                   pl.BlockSpec((OUT_SUB, SUBLANES, LANES), lambda i: (i, 0, 0))],
        out_shape=[
            jax.ShapeDtypeStruct((t, D_MODEL), F32),
            jax.ShapeDtypeStruct((t, D_MODEL), BF16),
            jax.ShapeDtypeStruct((SUBLANES, t), jnp.int32),
            jax.ShapeDtypeStruct((SUBLANES, t), F32),
            jax.ShapeDtypeStruct((nt, SUBLANES, LANES), F32),
        ],
        compiler_params=_cparams(("arbitrary",)),
        name="out_proj",
    )(*args)


def _copy_run(n, start_copy):
    @pl.when(n > 0)
    def _():
        start_copy(n)


def _select_by_position(positions, values, n_pos):
    tm = positions[0].shape[1]
    r = lax.broadcasted_iota(jnp.int32, (n_pos, tm), 0)
    out = jnp.zeros((n_pos, tm), F32)
    for k in reversed(range(TOP_K)):
        out = jnp.where(r == positions[k], values[k], out)
    return out


def _dispatch_kernel(cnt_ref, ls_ref, base_ref, zoff_ref, rows_ref, xn_ref, route_ref, xs_hbm, xloc, zeros_vmem,
                     sem_z, sems):
    n_pos = N_POS
    i = pl.program_id(0)
    n_steps = pl.num_programs(0)
    slot = lax.rem(i, 2)

    def zero_fill(op):
        def pad_rows(e, _):
            n = zoff_ref[N_EXPERTS + 1 + e]

            @pl.when(n > 0)
            def _():
                op(pltpu.make_async_copy(zeros_vmem.at[_row_slice(0, n, PACK_SUB)],
                                         xs_hbm.at[_row_slice(zoff_ref[e], n, PACK_SUB)], sem_z))
            return 0

        def unused_block(b, _):
            op(pltpu.make_async_copy(zeros_vmem, xs_hbm.at[_row_slice(b * ROW_BLK, ROW_BLK, PACK_SUB)], sem_z))
            return 0

        lax.fori_loop(0, N_EXPERTS, pad_rows, 0)
        lax.fori_loop(zoff_ref[N_EXPERTS], xs_hbm.shape[0] // (ROW_BLK * PACK_SUB), unused_block, 0)

    @pl.when(i == 0)
    def _():
        zeros_vmem[...] = jnp.zeros_like(zeros_vmem)
        zero_fill(lambda cp: cp.start())

    def slot_wait(s, step):
        n = rows_ref[step]
        pltpu.make_async_copy(xloc.at[s, _row_slice(0, n, PACK_SUB)], xs_hbm.at[_row_slice(0, n, PACK_SUB)],
                              sems.at[s]).wait()

    @pl.when(i >= 2)
    def _():
        slot_wait(slot, i - 2)

    positions = [route_ref[TOP_K + k:TOP_K + k + 1, :] for k in range(TOP_K)]
    sel = _select_by_position(positions, [1.0] * TOP_K, n_pos).astype(BF16)
    rows = _dot(sel, xn_ref[...])
    _pack_rows(xloc.at[slot], rows)

    def per_expert(e, _):
        idx = i * N_EXPERTS + e
        src0 = ls_ref[idx]
        dst0 = base_ref[idx]

        def start_copy(n):
            pltpu.make_async_copy(xloc.at[slot, _row_slice(src0, n, PACK_SUB)],
                                  xs_hbm.at[_row_slice(dst0, n, PACK_SUB)], sems.at[slot]).start()

        _copy_run(cnt_ref[idx], start_copy)
        return 0

    lax.fori_loop(0, N_EXPERTS, per_expert, 0)

    @pl.when(i == n_steps - 1)
    def _():
        slot_wait(slot, i)

        @pl.when(n_steps > 1)
        def _():
            slot_wait(1 - slot, i - 1)

        zero_fill(lambda cp: cp.wait())


def _dispatch(xn, route, tables, n_rows):
    t = xn.shape[0]
    tm = TM_ROUTE
    cnt_tbl, ls_tbl, base_tbl, zoff, tile_rows = tables
    return pl.pallas_call(
        _dispatch_kernel,
        grid_spec=pltpu.PrefetchScalarGridSpec(
            num_scalar_prefetch=5,
            grid=(t // tm,),
            in_specs=[pl.BlockSpec((tm, D_MODEL), lambda i, *_: (i, 0)),
                      pl.BlockSpec((SUBLANES, tm), lambda i, *_: (0, i))],
            out_specs=pl.BlockSpec(memory_space=pl.ANY),
            scratch_shapes=[
                pltpu.VMEM((2, N_POS * PACK_SUB, LANES), jnp.uint32),
                pltpu.VMEM((ROW_BLK * PACK_SUB, LANES), jnp.uint32),
                pltpu.SemaphoreType.DMA,
                pltpu.SemaphoreType.DMA((2,)),
            ],
        ),
        out_shape=jax.ShapeDtypeStruct(((n_rows + ROW_BLK) * PACK_SUB, LANES), jnp.uint32),
        compiler_params=_cparams(("arbitrary",)),
        name="dispatch",
    )(cnt_tbl, ls_tbl, base_tbl, zoff, tile_rows, xn, route)


def _experts_kernel(blk_e_ref, nvalid_ref, first_ref, wslot_ref, enext_ref, pieces_ref, xs_ref, wup_hbm, wdn_hbm,
                    bgu_ref, bd_ref, ys_ref, wup_buf, wdn_buf, wg_s, wu_s, wd_s, sems):
    i = pl.program_id(0)
    slot = wslot_ref[i]

    def weight_copies(e, s):
        return (pltpu.make_async_copy(wup_hbm.at[e], wup_buf.at[s], sems.at[0, s]),
                pltpu.make_async_copy(wdn_hbm.at[e], wdn_buf.at[s], sems.at[1, s]))

    @pl.when(i == 0)
    def _():
        for cp in weight_copies(blk_e_ref[0], slot):
            cp.start()

    @pl.when(first_ref[i] != 0)
    def _():
        for cp in weight_copies(blk_e_ref[i], slot):
            cp.wait()

        @pl.when(enext_ref[i] >= 0)
        def _():
            for cp in weight_copies(enext_ref[i], 1 - slot):
                cp.start()

        src = lax.broadcasted_iota(jnp.int32, (DEINT, DEINT), 0)
        dst = lax.broadcasted_iota(jnp.int32, (DEINT, DEINT), 1)
        perm = jnp.where(src == jnp.where(dst < LANES, 2 * dst, 2 * (dst - LANES) + 1), 1.0, 0.0).astype(BF16)
        for g in range(2 * D_EXPERT // DEINT):
            sep = _dot(wup_buf[slot, :, g * DEINT:(g + 1) * DEINT].astype(BF16), perm)
            wg_s[:, g * LANES:(g + 1) * LANES] = sep[:, :LANES].astype(BF16)
            wu_s[:, g * LANES:(g + 1) * LANES] = sep[:, LANES:].astype(BF16)
        wd_s[...] = wdn_buf[slot].astype(BF16)

    def ffn(n_rows):
        used = n_rows * ROW_SUB
        if n_rows:
            xb = _unpack_rows(xs_ref.at[pl.ds(0, n_rows * PACK_SUB)])
            gate = jnp.minimum(_dot(xb, wg_s[...]) + bgu_ref[0, 0:1, :], SWIGLU_LIMIT)
            up = jnp.clip(_dot(xb, wu_s[...]) + bgu_ref[0, 1:2, :], -SWIGLU_LIMIT, SWIGLU_LIMIT)
            act = (up + 1.0) * gate * jax.nn.sigmoid(SWIGLU_ALPHA * gate)
            _store_rows(ys_ref.at[pl.ds(0, used)], _dot(act.astype(BF16), wd_s[...]) + bd_ref[0])
        if n_rows < ROW_BLK:
            ys_ref[used:, :] = jnp.zeros((ROW_BLK * ROW_SUB - used, LANES), F32)

    for pieces in range(ROW_BLK // ROW_PIECE + 1):
        pl.when(pieces_ref[i] == pieces)(lambda pieces=pieces: ffn(pieces * ROW_PIECE))


def _experts(xs, plan, w_up, b_up, w_down, b_down, n_rows):
    nb = n_rows // ROW_BLK
    b_gu = b_up.reshape(N_EXPERTS, D_EXPERT, 2).transpose(0, 2, 1)
    bd = b_down.reshape(N_EXPERTS, 1, D_MODEL)
    blk_e, nvalid, first, wslot, enext, pieces = plan
    src = lambda i, be, nv, *_: (jnp.minimum(i, nv[0] - 1), 0)
    bspec = lambda r, m: pl.BlockSpec((1, r, m), lambda i, be, *_: (be[i], 0, 0))
    anyspec = pl.BlockSpec(memory_space=pl.ANY)
    return pl.pallas_call(
        _experts_kernel,
        grid_spec=pltpu.PrefetchScalarGridSpec(
            num_scalar_prefetch=6,
            grid=(nb,),
            in_specs=[pl.BlockSpec((ROW_BLK * PACK_SUB, LANES), src), anyspec, anyspec,
                      bspec(2, D_EXPERT), bspec(1, D_MODEL)],
            out_specs=_rows_spec(ROW_BLK, lambda i, *_: (i, 0)),
            scratch_shapes=[
                pltpu.VMEM((2, D_MODEL, 2 * D_EXPERT), F32),
                pltpu.VMEM((2, D_EXPERT, D_MODEL), F32),
                pltpu.VMEM((D_MODEL, D_EXPERT), BF16),
                pltpu.VMEM((D_MODEL, D_EXPERT), BF16),
                pltpu.VMEM((D_EXPERT, D_MODEL), BF16),
                pltpu.SemaphoreType.DMA((2, 2)),
            ],
        ),
        out_shape=jax.ShapeDtypeStruct((n_rows * ROW_SUB, LANES), F32),
        compiler_params=_cparams(("arbitrary",)),
        name="experts",
    )(blk_e, nvalid, first, wslot, enext, pieces, xs, w_up, w_down, b_gu, bd)


def _combine_kernel(cnt_ref, ls_ref, base_ref, rows_ref, ys_hbm, route_ref, gate_ref, x1_ref, p_ref, gp_ref, wpg_ref,
                    wpp_ref, gf_ref, o_ref, yloc, sems):
    tm = TM_ROUTE
    n_pos = N_POS
    i = pl.program_id(0)
    n_steps = pl.num_programs(0)
    slot = lax.rem(i, 2)

    def gather_step(step, s):
        for u in range(COMB_SUB):
            def per_expert(e, _, u=u):
                idx = (step * COMB_SUB + u) * N_EXPERTS + e
                src0 = base_ref[idx]
                dst0 = ls_ref[idx]

                def start_copy(n):
                    pltpu.make_async_copy(ys_hbm.at[_row_slice(src0, n)],
                                          yloc.at[s, u, _row_slice(dst0, n)], sems.at[s]).start()

                _copy_run(cnt_ref[idx], start_copy)
                return 0

            lax.fori_loop(0, N_EXPERTS, per_expert, 0)

    @pl.when(i == 0)
    def _():
        yloc[...] = jnp.zeros_like(yloc)
        gather_step(0, 0)

    @pl.when(i + 1 < n_steps)
    def _():
        gather_step(i + 1, 1 - slot)

    for u in range(COMB_SUB):
        n = rows_ref[i * COMB_SUB + u]
        pltpu.make_async_copy(ys_hbm.at[_row_slice(0, n)], yloc.at[slot, u, _row_slice(0, n)], sems.at[slot]).wait()

    moe = []
    for u in range(COMB_SUB):
        cols = slice(u * tm, (u + 1) * tm)
        positions = [route_ref[TOP_K + k:TOP_K + k + 1, cols] for k in range(TOP_K)]
        gsel = _select_by_position(positions, [gate_ref[k:k + 1, cols] for k in range(TOP_K)], n_pos)
        g_pos = jnp.sum(gsel, axis=1, keepdims=True)
        sel = jnp.where(gsel != 0.0, 1.0, 0.0).astype(BF16)
        y_gated = (_load_rows(yloc.at[slot, u]) * g_pos).astype(BF16)
        moe.append(_dot_tn(sel, y_gated))
    x2 = x1_ref[...] + jnp.concatenate(moe, axis=0)
    xn = _rms(x2, gp_ref[...], NORM_EPS).astype(BF16)
    pp = _dot(p_ref[...].astype(BF16), wpp_ref[...])
    x3 = x2 + pp * jax.nn.sigmoid(_dot(xn, wpg_ref[...]))
    o_ref[...] = _rms(x3, gf_ref[...], NORM_EPS)


def _combine(ys, route, gate_t, x1, p2, tables, g_ple, w_ple_gate, w_ple_proj, g_final):
    t = x1.shape[0]
    tm = COMB_SUB * TM_ROUTE
    cnt_tbl, ls_tbl, base_tbl, _, tile_rows = tables
    row = lambda n: pl.BlockSpec((tm, n), lambda i, *_: (i, 0))
    col = pl.BlockSpec((SUBLANES, tm), lambda i, *_: (0, i))
    full = lambda a: pl.BlockSpec(a.shape, lambda i, *_: (0, 0))
    consts = (g_ple.reshape(1, -1), w_ple_gate.astype(BF16), w_ple_proj.astype(BF16), g_final.reshape(1, -1))
    return pl.pallas_call(
        _combine_kernel,
        grid_spec=pltpu.PrefetchScalarGridSpec(
            num_scalar_prefetch=4,
            grid=(t // tm,),
            in_specs=[pl.BlockSpec(memory_space=pl.ANY), col, col, row(D_MODEL), row(PLE_DIM)]
                     + [full(a) for a in consts],
            out_specs=row(D_MODEL),
            scratch_shapes=[
                pltpu.VMEM((2, COMB_SUB, N_POS * ROW_SUB, LANES), F32),
                pltpu.SemaphoreType.DMA((2,)),
            ],
        ),
        out_shape=jax.ShapeDtypeStruct((t, D_MODEL), F32),
        compiler_params=_cparams(("arbitrary",)),
        name="combine",
    )(cnt_tbl, ls_tbl, base_tbl, tile_rows, ys, route, gate_t, x1, p2, *consts)


def _routing_tables(cnt, n_blocks):
    tile_cnt = cnt[:, 0, :N_EXPERTS].astype(jnp.int32)
    tile_cnt = tile_cnt + tile_cnt % 2
    counts = jnp.sum(tile_cnt, axis=0)
    padded = (counts + ROW_BLK - 1) // ROW_BLK * ROW_BLK
    pend = jnp.cumsum(padded)
    pstart = pend - padded
    base = pstart[None, :] + jnp.cumsum(tile_cnt, axis=0) - tile_cnt
    lstart = jnp.cumsum(tile_cnt, axis=1) - tile_cnt
    nvalid = pend[-1:] // ROW_BLK
    zoff = jnp.concatenate([pstart + counts, nvalid, padded - counts])
    blk_start = jnp.minimum(jnp.arange(n_blocks, dtype=jnp.int32) * ROW_BLK, pend[-1] - 1)
    blk_e = jnp.minimum(jnp.sum((pend[None, :] <= blk_start[:, None]).astype(jnp.int32), axis=1), N_EXPERTS - 1)
    first = jnp.concatenate([jnp.ones((1,), bool), blk_e[1:] != blk_e[:-1]])
    wslot = (jnp.cumsum(first.astype(jnp.int32)) - 1) % 2
    eids = jnp.arange(N_EXPERTS, dtype=jnp.int32)
    later_nonempty = (eids[None, :] > eids[:, None]) & (padded[None, :] > 0)
    next_e = jnp.min(jnp.where(later_nonempty, eids[None, :], N_EXPERTS), axis=1)
    next_e = jnp.where(next_e == N_EXPERTS, -1, next_e)
    blk_hot = blk_e[:, None] == eids[None, :]
    per_block = lambda v: jnp.sum(jnp.where(blk_hot, v[None, :], 0), axis=1)
    blk_ids = jnp.arange(n_blocks, dtype=jnp.int32)
    real_rows = jnp.clip(per_block(pstart + counts) - blk_ids * ROW_BLK, 0, ROW_BLK) * (blk_ids < nvalid[0])
    pieces = (real_rows + ROW_PIECE - 1) // ROW_PIECE
    tile_rows = jnp.sum(tile_cnt, axis=1)
    i32 = lambda a: a.reshape(-1).astype(jnp.int32)
    plan = (i32(blk_e), i32(nvalid), i32(first), i32(wslot), i32(per_block(next_e)), i32(pieces))
    return (i32(tile_cnt), i32(lstart), i32(base), i32(zoff), i32(tile_rows)), plan


def kernel(x, p, g_mix, w_in, conv_w, conv_b, dt_bias, a_log, d_skip, ssd_norm_w, lam_q1, lam_k1, lam_q2, lam_k2, subln_w, w_out, g_ffn, w_router, b_router, w_up, b_up, w_down, b_down, g_ple, w_ple_gate, w_ple_proj, g_final):
    bsz, seq, d = x.shape
    t = bsz * seq
    x2 = x.reshape(t, d)
    y_ssd, y_att = _mixer(x2, g_mix[0], w_in[0], conv_w[0], conv_b[0], dt_bias[0], a_log[0], d_skip[0], ssd_norm_w[0],
                          lam_q1[0], lam_k1[0], lam_q2[0], lam_k2[0], subln_w[0], bsz, seq)
    x1, xn, route, gate_t, cnt = _out_proj(x2, y_ssd, y_att, w_out[0], g_ffn[0], w_router[0], b_router[0])

    n_rows = t * TOP_K + (t // TM_ROUTE) * N_EXPERTS + N_EXPERTS * ROW_BLK
    n_rows = -(-n_rows // ROW_BLK) * ROW_BLK
    tables, plan = _routing_tables(cnt, n_rows // ROW_BLK)
    xs = _dispatch(xn, route, tables, n_rows)
    ys = _experts(xs, plan, w_up[0], b_up[0], w_down[0], b_down[0], n_rows)
    out = _combine(ys, route, gate_t, x1, p[0].reshape(t, PLE_DIM), tables, g_ple[0], w_ple_gate[0], w_ple_proj[0],
                   g_final)
    return out.reshape(bsz, seq, d)
```

```python
import math

import jax
import jax.numpy as jnp
from jax import lax
from jax.experimental import pallas as pl
from jax.experimental.pallas import tpu as pltpu

F32 = jnp.float32
BF16 = jnp.bfloat16

D_MODEL = 1024
PLE_DIM = 256
SSD_WIDTH = 512
ATT_WIDTH = 512
SSD_HEAD_DIM = 64
SSD_HEADS = 8
SSD_GROUPS = 2
SSD_STATE = 128
SSD_CONV = 4
SSD_CHUNK = 128
SSD_CONV_CH = SSD_WIDTH + 2 * SSD_GROUPS * SSD_STATE
SSD_NORM_EPS = 1e-5
ATT_HEAD_DIM = 64
ATT_HEADS = 4
SUBLN_EPS = 1e-5
OFF_Z = 0
OFF_XBC = OFF_Z + SSD_WIDTH
OFF_DT = OFF_XBC + SSD_CONV_CH
OFF_Q = OFF_DT + SSD_HEADS
OFF_K = OFF_Q + ATT_WIDTH
OFF_V = OFF_K + ATT_WIDTH
IN_PROJ = OFF_V + ATT_WIDTH
N_EXPERTS = 32
TOP_K = 4
D_EXPERT = 1024
SWIGLU_LIMIT = 7.0
SWIGLU_ALPHA = 1.702
NORM_EPS = 1e-6
LAM_INIT = 0.8 - 0.6 * math.exp(-0.3 * 0)

LANES = 128
SUBLANES = 8
VMEM_LIMIT_BYTES = 56 * 1024 * 1024
ROW_SUB = D_MODEL // LANES
PACK_SUB = ROW_SUB // 2

TM_PROJ = 512
ATT_BQ = 512
ATT_BK = 512
ROW_BLK = 512
ROW_PIECE = 128
TM_ROUTE = 256
N_POS = TOP_K * TM_ROUTE + N_EXPERTS
OUT_SUB = 4
COMB_SUB = 2
DEINT = 2 * LANES


def _cparams(sem):
    return pltpu.CompilerParams(dimension_semantics=sem, vmem_limit_bytes=VMEM_LIMIT_BYTES)


def _rms(x, w, eps):
    return x * lax.rsqrt(jnp.mean(x * x, axis=-1, keepdims=True) + eps) * w


def _dot(a, b):
    return jnp.dot(a, b, preferred_element_type=F32)


def _dot_nt(a, b):
    return lax.dot_general(a, b, (((1,), (1,)), ((), ())), preferred_element_type=F32)


def _dot_tn(a, b):
    return lax.dot_general(a, b, (((0,), (0,)), ((), ())), preferred_element_type=F32)


def _store_rows(ref, val):
    n = val.shape[0]
    for s in range(ROW_SUB):
        ref[pl.ds(s, n, stride=ROW_SUB), :] = val[:, s * LANES:(s + 1) * LANES]


def _load_rows(ref):
    n = ref.shape[0] // ROW_SUB
    return jnp.concatenate([ref[pl.ds(s, n, stride=ROW_SUB), :] for s in range(ROW_SUB)], axis=-1)


def _rows_spec(n, index_map):
    return pl.BlockSpec((n * ROW_SUB, LANES), index_map)


def _row_slice(start, n, sub=ROW_SUB):
    return pl.ds(pl.multiple_of(start * sub, SUBLANES), n * sub)


def _pack_rows(ref, val):
    n = val.shape[0]
    bits = lambda v: lax.bitcast_convert_type(v, jnp.uint32)
    words = (bits(val[:, :D_MODEL // 2]) & jnp.uint32(0xFFFF0000)) | (bits(val[:, D_MODEL // 2:]) >> 16)
    for s in range(PACK_SUB):
        ref[pl.ds(s, n, stride=PACK_SUB), :] = words[:, s * LANES:(s + 1) * LANES]


def _unpack_rows(ref):
    n = ref.shape[0] // PACK_SUB
    words = [ref[pl.ds(s, n, stride=PACK_SUB), :] for s in range(PACK_SUB)]
    as_f32 = lambda w: lax.bitcast_convert_type(w, F32)
    hi = [as_f32(w & jnp.uint32(0xFFFF0000)) for w in words]
    lo = [as_f32(w << 16) for w in words]
    return jnp.concatenate(hi + lo, axis=-1).astype(BF16)


def _split3(x):
    hi = x.astype(BF16)
    r1 = x - hi.astype(F32)
    mid = r1.astype(BF16)
    lo = (r1 - mid.astype(F32)).astype(BF16)
    return hi, mid, lo


def _in_proj_kernel(x_ref, g_ref, win_ref, z_ref, xbc_ref, dt_ref, q_ref, k_ref, v_ref, w_ref):
    @pl.when(pl.program_id(0) == 0)
    def _():
        w_ref[:, :OFF_Q] = win_ref[:, :OFF_Q].astype(BF16)
        w_ref[:, OFF_Q:OFF_DT + LANES] = jnp.zeros((D_MODEL, LANES - SSD_HEADS), BF16)
        w_ref[:, OFF_DT + LANES:] = win_ref[:, OFF_Q:].astype(BF16)

    h = _rms(x_ref[...], g_ref[...], NORM_EPS).astype(BF16)
    dt0 = OFF_DT
    q0 = OFF_DT + LANES
    k0, v0 = q0 + ATT_WIDTH, q0 + 2 * ATT_WIDTH
    z_ref[...] = _dot(h, w_ref[:, OFF_Z:OFF_XBC])
    xbc_ref[...] = _dot(h, w_ref[:, OFF_XBC:OFF_DT])
    dt_ref[...] = _dot(h, w_ref[:, dt0:q0])
    q_ref[...] = (_dot(h, w_ref[:, q0:k0]) * (ATT_HEAD_DIM ** -0.5 * math.log2(math.e))).astype(BF16)
    k_ref[...] = _dot(h, w_ref[:, k0:v0]).astype(BF16)
    v_ref[...] = _dot(h, w_ref[:, v0:v0 + ATT_WIDTH]).astype(BF16)


def _in_proj(x2, g_mix, w_in):
    t = x2.shape[0]
    tm = TM_PROJ
    row = lambda n: pl.BlockSpec((tm, n), lambda i: (i, 0))
    full = lambda a: pl.BlockSpec(a.shape, lambda i: (0, 0))
    return pl.pallas_call(
        _in_proj_kernel,
        grid=(t // tm,),
        in_specs=[row(D_MODEL), full(g_mix), full(w_in)],
        out_specs=[row(SSD_WIDTH), row(SSD_CONV_CH), row(LANES), row(ATT_WIDTH), row(ATT_WIDTH), row(ATT_WIDTH)],
        out_shape=[
            jax.ShapeDtypeStruct((t, SSD_WIDTH), F32),
            jax.ShapeDtypeStruct((t, SSD_CONV_CH), F32),
            jax.ShapeDtypeStruct((t, LANES), F32),
            jax.ShapeDtypeStruct((t, ATT_WIDTH), BF16),
            jax.ShapeDtypeStruct((t, ATT_WIDTH), BF16),
            jax.ShapeDtypeStruct((t, ATT_WIDTH), BF16),
        ],
        scratch_shapes=[pltpu.VMEM((D_MODEL, IN_PROJ + LANES - SSD_HEADS), BF16)],
        compiler_params=_cparams(("arbitrary",)),
        name="in_proj",
    )(x2, g_mix, w_in)


def _ssd_chunk_init(c, xpad_ref, state_ref):
    L = SSD_CHUNK

    @pl.when(c == 0)
    def _():
        xpad_ref[0:SUBLANES, :] = jnp.zeros((SUBLANES, SSD_CONV_CH), F32)
        state_ref[...] = jnp.zeros_like(state_ref)

    @pl.when(c != 0)
    def _():
        xpad_ref[0:SUBLANES, :] = xpad_ref[L:L + SUBLANES, :]


def _ssd_chunk(xbc_ref, dtr_ref, z_ref, cw_ref, cb_ref, dtb_ref, alog_ref, dskip_ref, nw_ref,
               y_ref, xpad_ref, state_ref):
    L = SSD_CHUNK
    xpad_ref[SUBLANES:SUBLANES + L, :] = xbc_ref[...]

    conv = cb_ref[...]
    for j in range(SSD_CONV):
        off = SUBLANES - (SSD_CONV - 1) + j
        conv = conv + cw_ref[j:j + 1, :] * xpad_ref[off:off + L, :]
    act = conv * jax.nn.sigmoid(conv)
    xs = act[:, :SSD_WIDTH]
    bm = act[:, SSD_WIDTH:SSD_WIDTH + SSD_GROUPS * SSD_STATE].astype(BF16)
    cm = act[:, SSD_WIDTH + SSD_GROUPS * SSD_STATE:].astype(BF16)

    dt_in = dtr_ref[...] + dtb_ref[...]
    dt_all = jnp.maximum(dt_in, 0.0) + jnp.log1p(jnp.exp(-jnp.abs(dt_in)))
    adt = dt_all * (-jnp.exp(alog_ref[...]))

    ri = lax.broadcasted_iota(jnp.int32, (L, L), 0)
    ci = lax.broadcasted_iota(jnp.int32, (L, L), 1)
    causal = ci <= ri
    tril = jnp.where(causal, 1.0, 0.0).astype(BF16)
    hi, mid, lo = _split3(adt)
    acum_all = _dot(tril, hi) + _dot(tril, mid) + _dot(tril, lo)
    acum_t = acum_all.T
    a_last = acum_all[L - 1:L, :]
    decay_in_all = jnp.exp(a_last - acum_all)
    decay_out_all = jnp.exp(acum_all)
    chunk_decay_all = jnp.exp(a_last)
    acum = lambda h: acum_all[:, h:h + 1]
    dt = lambda h: dt_all[:, h:h + 1]
    decay_out = lambda h: decay_out_all[:, h:h + 1]
    decay_in = lambda h: decay_in_all[:, h:h + 1]
    chunk_decay = lambda h: chunk_decay_all[:, h:h + 1]

    lane = lax.broadcasted_iota(jnp.int32, (L, LANES), 1)
    lo_half = lane < SSD_HEAD_DIM

    def per_pair(col_a, col_b):
        return jnp.where(lo_half, col_a, col_b)

    ys = []
    for pair in range(SSD_HEADS // 2):
        g = pair // 2
        h0, h1 = 2 * pair, 2 * pair + 1
        cg = cm[:, g * SSD_STATE:(g + 1) * SSD_STATE]
        bg = bm[:, g * SSD_STATE:(g + 1) * SSD_STATE]
        cb = _dot_nt(cg, bg)
        x_pair = xs[:, pair * LANES:(pair + 1) * LANES]
        xdt = x_pair * per_pair(dt(h0), dt(h1))
        y_pair = jnp.zeros((L, LANES), F32)
        for hh, keep in ((h0, lo_half), (h1, jnp.logical_not(lo_half))):
            seg = acum(hh) - acum_t[hh:hh + 1, :]
            lmat = jnp.where(causal, jnp.exp(jnp.where(causal, seg, 0.0)), 0.0)
            m = (cb * lmat).astype(BF16)
            y_pair = y_pair + _dot(m, jnp.where(keep, xdt, 0.0).astype(BF16))
        s_prev = state_ref[pair]
        y_off = _dot(cg, s_prev.astype(BF16)) * per_pair(decay_out(h0), decay_out(h1))
        w_in = (xdt * per_pair(decay_in(h0), decay_in(h1))).astype(BF16)
        cd = jnp.where(lane[0:1, :] < SSD_HEAD_DIM, chunk_decay(h0), chunk_decay(h1))
        state_ref[pair] = s_prev * cd + _dot_tn(bg, w_in)
        ys.append(y_pair + y_off + dskip_ref[:, pair * LANES:(pair + 1) * LANES] * x_pair)

    y = jnp.concatenate(ys, axis=-1)
    zz = z_ref[...]
    y = y * (zz * jax.nn.sigmoid(zz))
    gw = SSD_WIDTH // SSD_GROUPS
    outs = []
    for g in range(SSD_GROUPS):
        yg = y[:, g * gw:(g + 1) * gw]
        outs.append(yg * lax.rsqrt(jnp.mean(yg * yg, axis=-1, keepdims=True) + SSD_NORM_EPS))
    y_ref[...] = (jnp.concatenate(outs, axis=-1) * nw_ref[...]).astype(BF16)


def _attn_ssd_kernel(q_ref, k_ref, v_ref, lq1_ref, lk1_ref, lq2_ref, lk2_ref, sw_ref,
                     xbc_ref, dtr_ref, z_ref, cw_ref, cb_ref, dtb_ref, alog_ref, dskip_ref, nw_ref,
                     o_ref, y_ref, s_scr, vt_scr, xpad_ref, state_ref):
    bq, bk = ATT_BQ, ATT_BK
    qi = pl.program_id(2)
    n_maps = 2
    _ssd_chunk_init(pl.program_id(1) * pl.num_programs(2) + qi, xpad_ref, state_ref)

    @pl.when(qi == 0)
    def _():
        vt_scr[...] = v_ref[...].astype(F32).T.astype(BF16)

    def fold(t, reduce):
        return reduce(t.reshape(bk // SUBLANES, SUBLANES, bq), axis=0)

    def merge(old, new, op):
        return new if old is None else op(old, new)

    def attend(nk):
        _ssd_chunk(xbc_ref, dtr_ref, z_ref, cw_ref, cb_ref, dtb_ref, alog_ref, dskip_ref, nw_ref,
                   y_ref, xpad_ref, state_ref)
        q = q_ref[...]
        lane = lax.broadcasted_iota(jnp.int32, (bq, LANES), 1)
        zero = jnp.zeros_like(q)
        q_maps = (jnp.where(lane < ATT_HEAD_DIM, q, zero), jnp.where(lane >= ATT_HEAD_DIM, q, zero))
        key = lax.broadcasted_iota(jnp.int32, (bk, bq), 0)
        qry = lax.broadcasted_iota(jnp.int32, (bk, bq), 1)
        causal = key <= qry

        mt = [None] * n_maps
        for j in range(nk):
            kb = k_ref[j * bk:(j + 1) * bk, :]
            for m in range(n_maps):
                s = _dot_nt(kb, q_maps[m])
                if j == nk - 1:
                    s = jnp.where(causal, s, -jnp.inf)
                s_scr[m, j] = s
                mt[m] = merge(mt[m], fold(s, jnp.max), jnp.maximum)
        q_max = [jnp.max(t, axis=0, keepdims=True) for t in mt]

        lt = [None] * n_maps
        acc = [None] * n_maps
        for j in range(nk):
            vt = vt_scr[:, j * bk:(j + 1) * bk]
            for m in range(n_maps):
                p = jnp.exp2(s_scr[m, j] - q_max[m])
                lt[m] = merge(lt[m], fold(p, jnp.sum), jnp.add)
                acc[m] = merge(acc[m], _dot(vt, p.astype(BF16)), jnp.add)

        lam = (jnp.exp(jnp.sum(lq1_ref[...] * lk1_ref[...], axis=-1, keepdims=True))
               - jnp.exp(jnp.sum(lq2_ref[...] * lk2_ref[...], axis=-1, keepdims=True)) + LAM_INIT)
        l1 = jnp.sum(lt[0], axis=0, keepdims=True)
        l2 = jnp.sum(lt[1], axis=0, keepdims=True)
        o = acc[0] / l1 - lam * (acc[1] / l2)
        o = o * lax.rsqrt(jnp.mean(o * o, axis=0, keepdims=True) + SUBLN_EPS) * sw_ref[...]
        o_ref[...] = (o * (1.0 - LAM_INIT)).T.astype(BF16)

    for nk in range(1, k_ref.shape[0] // bk + 1):
        pl.when(qi == nk - 1)(lambda nk=nk: attend(nk))


def _attn_ssd(q, k, v, lam_q1, lam_k1, lam_q2, lam_k2, subln_w, xbc, dtr, z, conv_w, conv_b, dt_bias, a_log, d_skip,
              norm_w, bsz, seq):
    nq = seq // ATT_BQ
    nc = seq // SSD_CHUNK
    assert nc == ATT_HEADS * nq
    qspec = pl.BlockSpec((ATT_BQ, LANES), lambda b, h, i: (b * nq + i, h))
    kvspec = pl.BlockSpec((seq, LANES), lambda b, h, i: (b, h))
    full = lambda a: pl.BlockSpec(a.shape, lambda b, h, i: (0, 0))
    chunk = lambda n: pl.BlockSpec((SSD_CHUNK, n), lambda b, h, i: (b * nc + h * nq + i, 0))
    lams = [a.reshape(1, -1) for a in (lam_q1, lam_k1, lam_q2, lam_k2)]
    sw = subln_w.reshape(-1, 1)
    pad_h = lambda v: jnp.pad(v.reshape(1, SSD_HEADS), ((0, 0), (0, LANES - SSD_HEADS)))
    dskip_lanes = jnp.repeat(d_skip, SSD_HEAD_DIM).reshape(1, SSD_WIDTH)
    consts = (conv_w, conv_b.reshape(1, -1), pad_h(dt_bias), pad_h(a_log), dskip_lanes, norm_w.reshape(1, -1))
    return pl.pallas_call(
        _attn_ssd_kernel,
        grid=(bsz, ATT_HEADS, nq),
        in_specs=[qspec, kvspec, kvspec] + [full(a) for a in lams] + [full(sw)]
                 + [chunk(SSD_CONV_CH), chunk(LANES), chunk(SSD_WIDTH)] + [full(a) for a in consts],
        out_specs=[qspec, chunk(SSD_WIDTH)],
        out_shape=[jax.ShapeDtypeStruct((bsz * seq, ATT_WIDTH), BF16),
                   jax.ShapeDtypeStruct((bsz * seq, SSD_WIDTH), BF16)],
        scratch_shapes=[pltpu.VMEM((2, seq // ATT_BK, ATT_BK, ATT_BQ), F32),
                        pltpu.VMEM((LANES, seq), BF16),
                        pltpu.VMEM((SSD_CHUNK + 2 * SUBLANES, SSD_CONV_CH), F32),
                        pltpu.VMEM((SSD_HEADS // 2, SSD_STATE, LANES), F32)],
        compiler_params=_cparams(("arbitrary", "arbitrary", "arbitrary")),
        name="attn_ssd",
    )(q, k, v, *lams, sw, xbc, dtr, z, *consts)


def _mixer(x2, g_mix, w_in, conv_w, conv_b, dt_bias, a_log, d_skip, ssd_norm_w,
           lam_q1, lam_k1, lam_q2, lam_k2, subln_w, bsz, seq):
    z, xbc, dtr, q, k, v = _in_proj(x2, g_mix.reshape(1, -1), w_in)
    y_att, y_ssd = _attn_ssd(q, k, v, lam_q1, lam_k1, lam_q2, lam_k2, subln_w, xbc, dtr, z, conv_w, conv_b, dt_bias,
                             a_log, d_skip, ssd_norm_w, bsz, seq)
    return y_ssd, y_att


def _out_proj_kernel(x_ref, ys_ref, ya_ref, wo_ref, g_ref, wr2_ref, br_ref,
                     x1_ref, xn_ref, route_ref, gate_ref, cnt_ref):
    tm = TM_ROUTE
    x1 = x_ref[...] + _dot(jnp.concatenate([ys_ref[...], ya_ref[...]], axis=-1), wo_ref[...])
    x1_ref[...] = x1
    xn_all = _rms(x1, g_ref[...], NORM_EPS)
    xn_b = xn_all.astype(BF16)
    xn_ref[...] = xn_b

    n_tok = OUT_SUB * tm
    lg2 = _dot(xn_b, wr2_ref[...])
    logits = (lg2[:, :LANES] + lg2[:, LANES:]).T[:N_EXPERTS, :] + br_ref[...]

    eidx = lax.broadcasted_iota(jnp.int32, (N_EXPERTS, n_tok), 0).astype(F32)
    work = logits
    vals, idxs, hots = [], [], []
    for _ in range(TOP_K):
        m = jnp.max(work, axis=0, keepdims=True)
        idx = jnp.min(jnp.where(work == m, eidx, float(N_EXPERTS)), axis=0, keepdims=True)
        hot = eidx == idx
        vals.append(m)
        idxs.append(idx.astype(jnp.int32))
        hots.append(hot)
        work = jnp.where(hot, -jnp.inf, work)
    exps = [jnp.exp(v - vals[0]) for v in vals]
    denom = exps[0] + exps[1] + exps[2] + exps[3]
    gates = [e / denom for e in exps]

    cnt = jnp.zeros((N_EXPERTS, n_tok), F32)
    for hot in hots:
        cnt = cnt + jnp.where(hot, 1.0, 0.0)
    cnt_b = cnt.astype(BF16)
    r = lax.broadcasted_iota(jnp.int32, (tm, tm), 0)
    c = lax.broadcasted_iota(jnp.int32, (tm, tm), 1)
    earlier_tok = jnp.where(r < c, 1.0, 0.0).astype(BF16)
    er = lax.broadcasted_iota(jnp.int32, (N_EXPERTS, N_EXPERTS), 0)
    ec = lax.broadcasted_iota(jnp.int32, (N_EXPERTS, N_EXPERTS), 1)
    lower_exp = jnp.where(ec < er, 1.0, 0.0).astype(BF16)
    cnt_pad = jnp.concatenate([cnt_b, jnp.zeros((LANES - N_EXPERTS, n_tok), BF16)], axis=0)
    pos = []
    for sub in range(OUT_SUB):
        cols = slice(sub * tm, (sub + 1) * tm)
        run = jnp.sum(cnt[:, cols], axis=1, keepdims=True)
        run_even = 2.0 * jnp.floor(0.5 * run + 0.5)
        run_start = _dot(lower_exp, jnp.broadcast_to(run_even, (N_EXPERTS, LANES)).astype(BF16))[:, 0:1]
        pos.append(_dot(cnt_b[:, cols], earlier_tok) + run_start)
        cnt_ref[sub] = _dot_nt(jnp.ones((SUBLANES, tm), BF16), cnt_pad[:, cols])
    pos = jnp.concatenate(pos, axis=1)
    lps = [jnp.sum(jnp.where(hot, pos, 0.0), axis=0, keepdims=True).astype(jnp.int32) for hot in hots]

    route_ref[...] = jnp.concatenate(idxs + lps, axis=0)
    gate_ref[...] = jnp.concatenate(gates + [jnp.zeros((SUBLANES - TOP_K, n_tok), F32)], axis=0)


def _out_proj(x2, y_ssd, y_att, w_out, g_ffn, w_router, b_router):
    t = x2.shape[0]
    tm = OUT_SUB * TM_ROUTE
    nt = t // TM_ROUTE
    wo = w_out.astype(BF16)
    wrh, wrm, _ = _split3(jnp.pad(w_router, ((0, 0), (0, LANES - N_EXPERTS))))
    wr2 = jnp.concatenate([wrh, wrm], axis=1)
    br = b_router.reshape(N_EXPERTS, 1)
    row = lambda n: pl.BlockSpec((tm, n), lambda i: (i, 0))
    col = pl.BlockSpec((SUBLANES, tm), lambda i: (0, i))
    full = lambda a: pl.BlockSpec(a.shape, lambda i: (0, 0))
    args = (x2, y_ssd, y_att, wo, g_ffn.reshape(1, -1), wr2, br)
    return pl.pallas_call(
        _out_proj_kernel,
        grid=(t // tm,),
        in_specs=[row(D_MODEL), row(SSD_WIDTH), row(ATT_WIDTH)] + [full(a) for a in args[3:]],
        out_specs=[row(D_MODEL), row(D_MODEL), col, col,
                   pl.BlockSpec((OUT_SUB, SUBLANES, LANES), lambda i: (i, 0, 0))],
        out_shape=[
            jax.ShapeDtypeStruct((t, D_MODEL), F32),
            jax.ShapeDtypeStruct((t, D_MODEL), BF16),
            jax.ShapeDtypeStruct((SUBLANES, t), jnp.int32),
            jax.ShapeDtypeStruct((SUBLANES, t), F32),
            jax.ShapeDtypeStruct((nt, SUBLANES, LANES), F32),
        ],
        compiler_params=_cparams(("arbitrary",)),
        name="out_proj",
    )(*args)


def _copy_run(n, start_copy):
    @pl.when(n > 0)
    def _():
        start_copy(n)


def _select_by_position(positions, values, n_pos):
    tm = positions[0].shape[1]
    r = lax.broadcasted_iota(jnp.int32, (n_pos, tm), 0)
    out = jnp.zeros((n_pos, tm), F32)
    for k in reversed(range(TOP_K)):
        out = jnp.where(r == positions[k], values[k], out)
    return out


def _dispatch_kernel(cnt_ref, ls_ref, base_ref, zoff_ref, rows_ref, xn_ref, route_ref, xs_hbm, xloc, zeros_vmem,
                     sem_z, sems):
    n_pos = N_POS
    i = pl.program_id(0)
    n_steps = pl.num_programs(0)
    slot = lax.rem(i, 2)

    def zero_fill(op):
        def pad_rows(e, _):
            n = zoff_ref[N_EXPERTS + 1 + e]

            @pl.when(n > 0)
            def _():
                op(pltpu.make_async_copy(zeros_vmem.at[_row_slice(0, n, PACK_SUB)],
                                         xs_hbm.at[_row_slice(zoff_ref[e], n, PACK_SUB)], sem_z))
            return 0

        def unused_block(b, _):
            op(pltpu.make_async_copy(zeros_vmem, xs_hbm.at[_row_slice(b * ROW_BLK, ROW_BLK, PACK_SUB)], sem_z))
            return 0

        lax.fori_loop(0, N_EXPERTS, pad_rows, 0)
        lax.fori_loop(zoff_ref[N_EXPERTS], xs_hbm.shape[0] // (ROW_BLK * PACK_SUB), unused_block, 0)

    @pl.when(i == 0)
    def _():
        zeros_vmem[...] = jnp.zeros_like(zeros_vmem)
        zero_fill(lambda cp: cp.start())

    def slot_wait(s, step):
        n = rows_ref[step]
        pltpu.make_async_copy(xloc.at[s, _row_slice(0, n, PACK_SUB)], xs_hbm.at[_row_slice(0, n, PACK_SUB)],
                              sems.at[s]).wait()

    @pl.when(i >= 2)
    def _():
        slot_wait(slot, i - 2)

    positions = [route_ref[TOP_K + k:TOP_K + k + 1, :] for k in range(TOP_K)]
    sel = _select_by_position(positions, [1.0] * TOP_K, n_pos).astype(BF16)
    rows = _dot(sel, xn_ref[...])
    _pack_rows(xloc.at[slot], rows)

    def per_expert(e, _):
        idx = i * N_EXPERTS + e
        src0 = ls_ref[idx]
        dst0 = base_ref[idx]

        def start_copy(n):
            pltpu.make_async_copy(xloc.at[slot, _row_slice(src0, n, PACK_SUB)],
                                  xs_hbm.at[_row_slice(dst0, n, PACK_SUB)], sems.at[slot]).start()

        _copy_run(cnt_ref[idx], start_copy)
        return 0

    lax.fori_loop(0, N_EXPERTS, per_expert, 0)

    @pl.when(i == n_steps - 1)
    def _():
        slot_wait(slot, i)

        @pl.when(n_steps > 1)
        def _():
            slot_wait(1 - slot, i - 1)

        zero_fill(lambda cp: cp.wait())


def _dispatch(xn, route, tables, n_rows):
    t = xn.shape[0]
    tm = TM_ROUTE
    cnt_tbl, ls_tbl, base_tbl, zoff, tile_rows = tables
    return pl.pallas_call(
        _dispatch_kernel,
        grid_spec=pltpu.PrefetchScalarGridSpec(
            num_scalar_prefetch=5,
            grid=(t // tm,),
            in_specs=[pl.BlockSpec((tm, D_MODEL), lambda i, *_: (i, 0)),
                      pl.BlockSpec((SUBLANES, tm), lambda i, *_: (0, i))],
            out_specs=pl.BlockSpec(memory_space=pl.ANY),
            scratch_shapes=[
                pltpu.VMEM((2, N_POS * PACK_SUB, LANES), jnp.uint32),
                pltpu.VMEM((ROW_BLK * PACK_SUB, LANES), jnp.uint32),
                pltpu.SemaphoreType.DMA,
                pltpu.SemaphoreType.DMA((2,)),
            ],
        ),
        out_shape=jax.ShapeDtypeStruct(((n_rows + ROW_BLK) * PACK_SUB, LANES), jnp.uint32),
        compiler_params=_cparams(("arbitrary",)),
        name="dispatch",
    )(cnt_tbl, ls_tbl, base_tbl, zoff, tile_rows, xn, route)


def _experts_kernel(blk_e_ref, nvalid_ref, first_ref, wslot_ref, enext_ref, pieces_ref, xs_ref, wup_hbm, wdn_hbm,
                    bgu_ref, bd_ref, ys_ref, wup_buf, wdn_buf, wg_s, wu_s, wd_s, sems):
    i = pl.program_id(0)
    slot = wslot_ref[i]

    def weight_copies(e, s):
        return (pltpu.make_async_copy(wup_hbm.at[e], wup_buf.at[s], sems.at[0, s]),
                pltpu.make_async_copy(wdn_hbm.at[e], wdn_buf.at[s], sems.at[1, s]))

    @pl.when(i == 0)
    def _():
        for cp in weight_copies(blk_e_ref[0], slot):
            cp.start()

    @pl.when(first_ref[i] != 0)
    def _():
        for cp in weight_copies(blk_e_ref[i], slot):
            cp.wait()

        @pl.when(enext_ref[i] >= 0)
        def _():
            for cp in weight_copies(enext_ref[i], 1 - slot):
                cp.start()

        src = lax.broadcasted_iota(jnp.int32, (DEINT, DEINT), 0)
        dst = lax.broadcasted_iota(jnp.int32, (DEINT, DEINT), 1)
        perm = jnp.where(src == jnp.where(dst < LANES, 2 * dst, 2 * (dst - LANES) + 1), 1.0, 0.0).astype(BF16)
        for g in range(2 * D_EXPERT // DEINT):
            sep = _dot(wup_buf[slot, :, g * DEINT:(g + 1) * DEINT].astype(BF16), perm)
            wg_s[:, g * LANES:(g + 1) * LANES] = sep[:, :LANES].astype(BF16)
            wu_s[:, g * LANES:(g + 1) * LANES] = sep[:, LANES:].astype(BF16)
        wd_s[...] = wdn_buf[slot].astype(BF16)

    def ffn(n_rows):
        used = n_rows * ROW_SUB
        if n_rows:
            xb = _unpack_rows(xs_ref.at[pl.ds(0, n_rows * PACK_SUB)])
            gate = jnp.minimum(_dot(xb, wg_s[...]) + bgu_ref[0, 0:1, :], SWIGLU_LIMIT)
            up = jnp.clip(_dot(xb, wu_s[...]) + bgu_ref[0, 1:2, :], -SWIGLU_LIMIT, SWIGLU_LIMIT)
            act = (up + 1.0) * gate * jax.nn.sigmoid(SWIGLU_ALPHA * gate)
            _store_rows(ys_ref.at[pl.ds(0, used)], _dot(act.astype(BF16), wd_s[...]) + bd_ref[0])
        if n_rows < ROW_BLK:
            ys_ref[used:, :] = jnp.zeros((ROW_BLK * ROW_SUB - used, LANES), F32)

    for pieces in range(ROW_BLK // ROW_PIECE + 1):
        pl.when(pieces_ref[i] == pieces)(lambda pieces=pieces: ffn(pieces * ROW_PIECE))


def _experts(xs, plan, w_up, b_up, w_down, b_down, n_rows):
    nb = n_rows // ROW_BLK
    b_gu = b_up.reshape(N_EXPERTS, D_EXPERT, 2).transpose(0, 2, 1)
    bd = b_down.reshape(N_EXPERTS, 1, D_MODEL)
    blk_e, nvalid, first, wslot, enext, pieces = plan
    src = lambda i, be, nv, *_: (jnp.minimum(i, nv[0] - 1), 0)
    bspec = lambda r, m: pl.BlockSpec((1, r, m), lambda i, be, *_: (be[i], 0, 0))
    anyspec = pl.BlockSpec(memory_space=pl.ANY)
    return pl.pallas_call(
        _experts_kernel,
        grid_spec=pltpu.PrefetchScalarGridSpec(
            num_scalar_prefetch=6,
            grid=(nb,),
            in_specs=[pl.BlockSpec((ROW_BLK * PACK_SUB, LANES), src), anyspec, anyspec,
                      bspec(2, D_EXPERT), bspec(1, D_MODEL)],
            out_specs=_rows_spec(ROW_BLK, lambda i, *_: (i, 0)),
            scratch_shapes=[
                pltpu.VMEM((2, D_MODEL, 2 * D_EXPERT), F32),
                pltpu.VMEM((2, D_EXPERT, D_MODEL), F32),
                pltpu.VMEM((D_MODEL, D_EXPERT), BF16),
                pltpu.VMEM((D_MODEL, D_EXPERT), BF16),
                pltpu.VMEM((D_EXPERT, D_MODEL), BF16),
                pltpu.SemaphoreType.DMA((2, 2)),
            ],
        ),
        out_shape=jax.ShapeDtypeStruct((n_rows * ROW_SUB, LANES), F32),
        compiler_params=_cparams(("arbitrary",)),
        name="experts",
    )(blk_e, nvalid, first, wslot, enext, pieces, xs, w_up, w_down, b_gu, bd)


def _combine_kernel(cnt_ref, ls_ref, base_ref, rows_ref, ys_hbm, route_ref, gate_ref, x1_ref, p_ref, gp_ref, wpg_ref,
                    wpp_ref, gf_ref, o_ref, yloc, sems):
    tm = TM_ROUTE
    n_pos = N_POS
    i = pl.program_id(0)
    n_steps = pl.num_programs(0)
    slot = lax.rem(i, 2)

    def gather_step(step, s):
        for u in range(COMB_SUB):
            def per_expert(e, _, u=u):
                idx = (step * COMB_SUB + u) * N_EXPERTS + e
                src0 = base_ref[idx]
                dst0 = ls_ref[idx]

                def start_copy(n):
                    pltpu.make_async_copy(ys_hbm.at[_row_slice(src0, n)],
                                          yloc.at[s, u, _row_slice(dst0, n)], sems.at[s]).start()

                _copy_run(cnt_ref[idx], start_copy)
                return 0

            lax.fori_loop(0, N_EXPERTS, per_expert, 0)

    @pl.when(i == 0)
    def _():
        yloc[...] = jnp.zeros_like(yloc)
        gather_step(0, 0)

    @pl.when(i + 1 < n_steps)
    def _():
        gather_step(i + 1, 1 - slot)

    for u in range(COMB_SUB):
        n = rows_ref[i * COMB_SUB + u]
        pltpu.make_async_copy(ys_hbm.at[_row_slice(0, n)], yloc.at[slot, u, _row_slice(0, n)], sems.at[slot]).wait()

    moe = []
    for u in range(COMB_SUB):
        cols = slice(u * tm, (u + 1) * tm)
        positions = [route_ref[TOP_K + k:TOP_K + k + 1, cols] for k in range(TOP_K)]
        gsel = _select_by_position(positions, [gate_ref[k:k + 1, cols] for k in range(TOP_K)], n_pos)
        g_pos = jnp.sum(gsel, axis=1, keepdims=True)
        sel = jnp.where(gsel != 0.0, 1.0, 0.0).astype(BF16)
        y_gated = (_load_rows(yloc.at[slot, u]) * g_pos).astype(BF16)
        moe.append(_dot_tn(sel, y_gated))
    x2 = x1_ref[...] + jnp.concatenate(moe, axis=0)
    xn = _rms(x2, gp_ref[...], NORM_EPS).astype(BF16)
    pp = _dot(p_ref[...].astype(BF16), wpp_ref[...])
    x3 = x2 + pp * jax.nn.sigmoid(_dot(xn, wpg_ref[...]))
    o_ref[...] = _rms(x3, gf_ref[...], NORM_EPS)


def _combine(ys, route, gate_t, x1, p2, tables, g_ple, w_ple_gate, w_ple_proj, g_final):
    t = x1.shape[0]
    tm = COMB_SUB * TM_ROUTE
    cnt_tbl, ls_tbl, base_tbl, _, tile_rows = tables
    row = lambda n: pl.BlockSpec((tm, n), lambda i, *_: (i, 0))
    col = pl.BlockSpec((SUBLANES, tm), lambda i, *_: (0, i))
    full = lambda a: pl.BlockSpec(a.shape, lambda i, *_: (0, 0))
    consts = (g_ple.reshape(1, -1), w_ple_gate.astype(BF16), w_ple_proj.astype(BF16), g_final.reshape(1, -1))
    return pl.pallas_call(
        _combine_kernel,
        grid_spec=pltpu.PrefetchScalarGridSpec(
            num_scalar_prefetch=4,
            grid=(t // tm,),
            in_specs=[pl.BlockSpec(memory_space=pl.ANY), col, col, row(D_MODEL), row(PLE_DIM)]
                     + [full(a) for a in consts],
            out_specs=row(D_MODEL),
            scratch_shapes=[
                pltpu.VMEM((2, COMB_SUB, N_POS * ROW_SUB, LANES), F32),
                pltpu.SemaphoreType.DMA((2,)),
            ],
        ),
        out_shape=jax.ShapeDtypeStruct((t, D_MODEL), F32),
        compiler_params=_cparams(("arbitrary",)),
        name="combine",
    )(cnt_tbl, ls_tbl, base_tbl, tile_rows, ys, route, gate_t, x1, p2, *consts)


def _routing_tables(cnt, n_blocks):
    tile_cnt = cnt[:, 0, :N_EXPERTS].astype(jnp.int32)
    tile_cnt = tile_cnt + tile_cnt % 2
    counts = jnp.sum(tile_cnt, axis=0)
    padded = (counts + ROW_BLK - 1) // ROW_BLK * ROW_BLK
    pend = jnp.cumsum(padded)
    pstart = pend - padded
    base = pstart[None, :] + jnp.cumsum(tile_cnt, axis=0) - tile_cnt
    lstart = jnp.cumsum(tile_cnt, axis=1) - tile_cnt
    nvalid = pend[-1:] // ROW_BLK
    zoff = jnp.concatenate([pstart + counts, nvalid, padded - counts])
    blk_start = jnp.minimum(jnp.arange(n_blocks, dtype=jnp.int32) * ROW_BLK, pend[-1] - 1)
    blk_e = jnp.minimum(jnp.sum((pend[None, :] <= blk_start[:, None]).astype(jnp.int32), axis=1), N_EXPERTS - 1)
    first = jnp.concatenate([jnp.ones((1,), bool), blk_e[1:] != blk_e[:-1]])
    wslot = (jnp.cumsum(first.astype(jnp.int32)) - 1) % 2
    eids = jnp.arange(N_EXPERTS, dtype=jnp.int32)
    later_nonempty = (eids[None, :] > eids[:, None]) & (padded[None, :] > 0)
    next_e = jnp.min(jnp.where(later_nonempty, eids[None, :], N_EXPERTS), axis=1)
    next_e = jnp.where(next_e == N_EXPERTS, -1, next_e)
    blk_hot = blk_e[:, None] == eids[None, :]
    per_block = lambda v: jnp.sum(jnp.where(blk_hot, v[None, :], 0), axis=1)
    blk_ids = jnp.arange(n_blocks, dtype=jnp.int32)
    real_rows = jnp.clip(per_block(pstart + counts) - blk_ids * ROW_BLK, 0, ROW_BLK) * (blk_ids < nvalid[0])
    pieces = (real_rows + ROW_PIECE - 1) // ROW_PIECE
    tile_rows = jnp.sum(tile_cnt, axis=1)
    i32 = lambda a: a.reshape(-1).astype(jnp.int32)
    plan = (i32(blk_e), i32(nvalid), i32(first), i32(wslot), i32(per_block(next_e)), i32(pieces))
    return (i32(tile_cnt), i32(lstart), i32(base), i32(zoff), i32(tile_rows)), plan


def kernel(x, p, g_mix, w_in, conv_w, conv_b, dt_bias, a_log, d_skip, ssd_norm_w, lam_q1, lam_k1, lam_q2, lam_k2, subln_w, w_out, g_ffn, w_router, b_router, w_up, b_up, w_down, b_down, g_ple, w_ple_gate, w_ple_proj, g_final):
    bsz, seq, d = x.shape
    t = bsz * seq
    x2 = x.reshape(t, d)
    y_ssd, y_att = _mixer(x2, g_mix[0], w_in[0], conv_w[0], conv_b[0], dt_bias[0], a_log[0], d_skip[0], ssd_norm_w[0],
                          lam_q1[0], lam_k1[0], lam_q2[0], lam_k2[0], subln_w[0], bsz, seq)
    x1, xn, route, gate_t, cnt = _out_proj(x2, y_ssd, y_att, w_out[0], g_ffn[0], w_router[0], b_router[0])

    n_rows = t * TOP_K + (t // TM_ROUTE) * N_EXPERTS + N_EXPERTS * ROW_BLK
    n_rows = -(-n_rows // ROW_BLK) * ROW_BLK
    tables, plan = _routing_tables(cnt, n_rows // ROW_BLK)
    xs = _dispatch(xn, route, tables, n_rows)
    ys = _experts(xs, plan, w_up[0], b_up[0], w_down[0], b_down[0], n_rows)
    out = _combine(ys, route, gate_t, x1, p[0].reshape(t, PLE_DIM), tables, g_ple[0], w_ple_gate[0], w_ple_proj[0],
                   g_final)
    return out.reshape(bsz, seq, d)
```

```python
import math

import jax
import jax.numpy as jnp
from jax import lax
from jax.experimental import pallas as pl
from jax.experimental.pallas import tpu as pltpu

F32 = jnp.float32
BF16 = jnp.bfloat16

D_MODEL = 1024
PLE_DIM = 256
SSD_WIDTH = 512
ATT_WIDTH = 512
SSD_HEAD_DIM = 64
SSD_HEADS = 8
SSD_GROUPS = 2
SSD_STATE = 128
SSD_CONV = 4
SSD_CHUNK = 128
SSD_CONV_CH = SSD_WIDTH + 2 * SSD_GROUPS * SSD_STATE
SSD_NORM_EPS = 1e-5
ATT_HEAD_DIM = 64
ATT_HEADS = 4
SUBLN_EPS = 1e-5
OFF_Z = 0
OFF_XBC = OFF_Z + SSD_WIDTH
OFF_DT = OFF_XBC + SSD_CONV_CH
OFF_Q = OFF_DT + SSD_HEADS
OFF_K = OFF_Q + ATT_WIDTH
OFF_V = OFF_K + ATT_WIDTH
IN_PROJ = OFF_V + ATT_WIDTH
N_EXPERTS = 32
TOP_K = 4
D_EXPERT = 1024
SWIGLU_LIMIT = 7.0
SWIGLU_ALPHA = 1.702
NORM_EPS = 1e-6
LAM_INIT = 0.8 - 0.6 * math.exp(-0.3 * 0)

LANES = 128
SUBLANES = 8
VMEM_LIMIT_BYTES = 56 * 1024 * 1024
ROW_SUB = D_MODEL // LANES
PACK_SUB = ROW_SUB // 2

TM_PROJ = 512
ATT_BQ = 512
ATT_BK = 512
ROW_BLK = 512
ROW_PIECE = 128
TM_ROUTE = 256
N_POS = TOP_K * TM_ROUTE + N_EXPERTS
OUT_SUB = 4
COMB_SUB = 2
DEINT = 2 * LANES


def _cparams(sem):
    return pltpu.CompilerParams(dimension_semantics=sem, vmem_limit_bytes=VMEM_LIMIT_BYTES)


def _rms(x, w, eps):
    return x * lax.rsqrt(jnp.mean(x * x, axis=-1, keepdims=True) + eps) * w


def _dot(a, b):
    return jnp.dot(a, b, preferred_element_type=F32)


def _dot_nt(a, b):
    return lax.dot_general(a, b, (((1,), (1,)), ((), ())), preferred_element_type=F32)


def _dot_tn(a, b):
    return lax.dot_general(a, b, (((0,), (0,)), ((), ())), preferred_element_type=F32)


def _store_rows(ref, val):
    n = val.shape[0]
    for s in range(ROW_SUB):
        ref[pl.ds(s, n, stride=ROW_SUB), :] = val[:, s * LANES:(s + 1) * LANES]


def _load_rows(ref):
    n = ref.shape[0] // ROW_SUB
    return jnp.concatenate([ref[pl.ds(s, n, stride=ROW_SUB), :] for s in range(ROW_SUB)], axis=-1)


def _rows_spec(n, index_map):
    return pl.BlockSpec((n * ROW_SUB, LANES), index_map)


def _row_slice(start, n, sub=ROW_SUB):
    return pl.ds(pl.multiple_of(start * sub, SUBLANES), n * sub)


def _pack_rows(ref, val):
    n = val.shape[0]
    bits = lambda v: lax.bitcast_convert_type(v, jnp.uint32)
    words = (bits(val[:, :D_MODEL // 2]) & jnp.uint32(0xFFFF0000)) | (bits(val[:, D_MODEL // 2:]) >> 16)
    for s in range(PACK_SUB):
        ref[pl.ds(s, n, stride=PACK_SUB), :] = words[:, s * LANES:(s + 1) * LANES]


def _unpack_rows(ref):
    n = ref.shape[0] // PACK_SUB
    words = [ref[pl.ds(s, n, stride=PACK_SUB), :] for s in range(PACK_SUB)]
    as_f32 = lambda w: lax.bitcast_convert_type(w, F32)
    hi = [as_f32(w & jnp.uint32(0xFFFF0000)) for w in words]
    lo = [as_f32(w << 16) for w in words]
    return jnp.concatenate(hi + lo, axis=-1).astype(BF16)


def _split3(x):
    hi = x.astype(BF16)
    r1 = x - hi.astype(F32)
    mid = r1.astype(BF16)
    lo = (r1 - mid.astype(F32)).astype(BF16)
    return hi, mid, lo


def _in_proj_kernel(x_ref, g_ref, win_ref, z_ref, xbc_ref, dt_ref, q_ref, k_ref, v_ref, w_ref):
    @pl.when(pl.program_id(0) == 0)
    def _():
        w_ref[:OFF_DT, :] = win_ref[:OFF_DT, :].astype(BF16)
        dt_rows = jnp.concatenate([win_ref[OFF_DT:OFF_Q, :], jnp.zeros((LANES - SSD_HEADS, D_MODEL), F32)], axis=0)
        w_ref[OFF_DT:OFF_DT + LANES, :] = dt_rows.astype(BF16)
        w_ref[OFF_DT + LANES:, :] = win_ref[OFF_Q:, :].astype(BF16)

    h = _rms(x_ref[...], g_ref[...], NORM_EPS).astype(BF16)
    dt0 = OFF_DT
    q0 = OFF_DT + LANES
    k0, v0 = q0 + ATT_WIDTH, q0 + 2 * ATT_WIDTH
    z_ref[...] = _dot_nt(h, w_ref[OFF_Z:OFF_XBC, :])
    xbc_ref[...] = _dot_nt(h, w_ref[OFF_XBC:OFF_DT, :])
    dt_ref[...] = _dot_nt(h, w_ref[dt0:q0, :])
    q_ref[...] = (_dot_nt(h, w_ref[q0:k0, :]) * (ATT_HEAD_DIM ** -0.5 * math.log2(math.e))).astype(BF16)
    k_ref[...] = _dot_nt(h, w_ref[k0:v0, :]).astype(BF16)
    v_ref[...] = _dot_nt(h, w_ref[v0:v0 + ATT_WIDTH, :]).astype(BF16)


def _in_proj(x2, g_mix, w_in_t):
    t = x2.shape[0]
    tm = TM_PROJ
    row = lambda n: pl.BlockSpec((tm, n), lambda i: (i, 0))
    full = lambda a: pl.BlockSpec(a.shape, lambda i: (0, 0))
    return pl.pallas_call(
        _in_proj_kernel,
        grid=(t // tm,),
        in_specs=[row(D_MODEL), full(g_mix), full(w_in_t)],
        out_specs=[row(SSD_WIDTH), row(SSD_CONV_CH), row(LANES), row(ATT_WIDTH), row(ATT_WIDTH), row(ATT_WIDTH)],
        out_shape=[
            jax.ShapeDtypeStruct((t, SSD_WIDTH), F32),
            jax.ShapeDtypeStruct((t, SSD_CONV_CH), F32),
            jax.ShapeDtypeStruct((t, LANES), F32),
            jax.ShapeDtypeStruct((t, ATT_WIDTH), BF16),
            jax.ShapeDtypeStruct((t, ATT_WIDTH), BF16),
            jax.ShapeDtypeStruct((t, ATT_WIDTH), BF16),
        ],
        scratch_shapes=[pltpu.VMEM((IN_PROJ + LANES - SSD_HEADS, D_MODEL), BF16)],
        compiler_params=_cparams(("arbitrary",)),
        name="in_proj",
    )(x2, g_mix, w_in_t)


def _ssd_chunk_init(c, xpad_ref, state_ref):
    L = SSD_CHUNK

    @pl.when(c == 0)
    def _():
        xpad_ref[0:SUBLANES, :] = jnp.zeros((SUBLANES, SSD_CONV_CH), F32)
        state_ref[...] = jnp.zeros_like(state_ref)

    @pl.when(c != 0)
    def _():
        xpad_ref[0:SUBLANES, :] = xpad_ref[L:L + SUBLANES, :]


def _ssd_chunk(xbc_ref, dtr_ref, z_ref, cw_ref, cb_ref, dtb_ref, alog_ref, dskip_ref, nw_ref,
               y_ref, xpad_ref, state_ref):
    L = SSD_CHUNK
    xpad_ref[SUBLANES:SUBLANES + L, :] = xbc_ref[...]

    conv = cb_ref[...]
    for j in range(SSD_CONV):
        off = SUBLANES - (SSD_CONV - 1) + j
        conv = conv + cw_ref[j:j + 1, :] * xpad_ref[off:off + L, :]
    act = conv * jax.nn.sigmoid(conv)
    xs = act[:, :SSD_WIDTH]
    bm = act[:, SSD_WIDTH:SSD_WIDTH + SSD_GROUPS * SSD_STATE].astype(BF16)
    cm = act[:, SSD_WIDTH + SSD_GROUPS * SSD_STATE:].astype(BF16)

    dt_in = dtr_ref[...] + dtb_ref[...]
    dt_all = jnp.maximum(dt_in, 0.0) + jnp.log1p(jnp.exp(-jnp.abs(dt_in)))
    adt = dt_all * (-jnp.exp(alog_ref[...]))

    ri = lax.broadcasted_iota(jnp.int32, (L, L), 0)
    ci = lax.broadcasted_iota(jnp.int32, (L, L), 1)
    causal = ci <= ri
    tril = jnp.where(causal, 1.0, 0.0).astype(BF16)
    hi, mid, lo = _split3(adt)
    acum_all = _dot(tril, hi) + _dot(tril, mid) + _dot(tril, lo)
    acum_t = acum_all.T
    a_last = acum_all[L - 1:L, :]
    decay_in_all = jnp.exp(a_last - acum_all)
    decay_out_all = jnp.exp(acum_all)
    chunk_decay_all = jnp.exp(a_last)
    acum = lambda h: acum_all[:, h:h + 1]
    dt = lambda h: dt_all[:, h:h + 1]
    decay_out = lambda h: decay_out_all[:, h:h + 1]
    decay_in = lambda h: decay_in_all[:, h:h + 1]
    chunk_decay = lambda h: chunk_decay_all[:, h:h + 1]

    lane = lax.broadcasted_iota(jnp.int32, (L, LANES), 1)
    lo_half = lane < SSD_HEAD_DIM

    def per_pair(col_a, col_b):
        return jnp.where(lo_half, col_a, col_b)

    ys = []
    for pair in range(SSD_HEADS // 2):
        g = pair // 2
        h0, h1 = 2 * pair, 2 * pair + 1
        cg = cm[:, g * SSD_STATE:(g + 1) * SSD_STATE]
        bg = bm[:, g * SSD_STATE:(g + 1) * SSD_STATE]
        cb = _dot_nt(cg, bg)
        x_pair = xs[:, pair * LANES:(pair + 1) * LANES]
        xdt = x_pair * per_pair(dt(h0), dt(h1))
        y_pair = jnp.zeros((L, LANES), F32)
        for hh, keep in ((h0, lo_half), (h1, jnp.logical_not(lo_half))):
            seg = acum(hh) - acum_t[hh:hh + 1, :]
            lmat = jnp.where(causal, jnp.exp(jnp.where(causal, seg, 0.0)), 0.0)
            m = (cb * lmat).astype(BF16)
            y_pair = y_pair + _dot(m, jnp.where(keep, xdt, 0.0).astype(BF16))
        s_prev = state_ref[pair]
        y_off = _dot(cg, s_prev.astype(BF16)) * per_pair(decay_out(h0), decay_out(h1))
        w_in = (xdt * per_pair(decay_in(h0), decay_in(h1))).astype(BF16)
        cd = jnp.where(lane[0:1, :] < SSD_HEAD_DIM, chunk_decay(h0), chunk_decay(h1))
        state_ref[pair] = s_prev * cd + _dot_tn(bg, w_in)
        ys.append(y_pair + y_off + dskip_ref[:, pair * LANES:(pair + 1) * LANES] * x_pair)

    y = jnp.concatenate(ys, axis=-1)
    zz = z_ref[...]
    y = y * (zz * jax.nn.sigmoid(zz))
    gw = SSD_WIDTH // SSD_GROUPS
    outs = []
    for g in range(SSD_GROUPS):
        yg = y[:, g * gw:(g + 1) * gw]
        outs.append(yg * lax.rsqrt(jnp.mean(yg * yg, axis=-1, keepdims=True) + SSD_NORM_EPS))
    y_ref[...] = (jnp.concatenate(outs, axis=-1) * nw_ref[...]).astype(BF16)


def _attn_ssd_kernel(q_ref, k_ref, v_ref, lq1_ref, lk1_ref, lq2_ref, lk2_ref, sw_ref,
                     xbc_ref, dtr_ref, z_ref, cw_ref, cb_ref, dtb_ref, alog_ref, dskip_ref, nw_ref,
                     o_ref, y_ref, s_scr, vt_scr, xpad_ref, state_ref):
    bq, bk = ATT_BQ, ATT_BK
    qi = pl.program_id(2)
    n_maps = 2
    _ssd_chunk_init(pl.program_id(1) * pl.num_programs(2) + qi, xpad_ref, state_ref)

    @pl.when(qi == 0)
    def _():
        vt_scr[...] = v_ref[...].astype(F32).T.astype(BF16)

    def fold(t, reduce):
        return reduce(t.reshape(bk // SUBLANES, SUBLANES, bq), axis=0)

    def merge(old, new, op):
        return new if old is None else op(old, new)

    def attend(nk):
        _ssd_chunk(xbc_ref, dtr_ref, z_ref, cw_ref, cb_ref, dtb_ref, alog_ref, dskip_ref, nw_ref,
                   y_ref, xpad_ref, state_ref)
        q = q_ref[...]
        lane = lax.broadcasted_iota(jnp.int32, (bq, LANES), 1)
        zero = jnp.zeros_like(q)
        q_maps = (jnp.where(lane < ATT_HEAD_DIM, q, zero), jnp.where(lane >= ATT_HEAD_DIM, q, zero))
        key = lax.broadcasted_iota(jnp.int32, (bk, bq), 0)
        qry = lax.broadcasted_iota(jnp.int32, (bk, bq), 1)
        causal = key <= qry

        mt = [None] * n_maps
        for j in range(nk):
            kb = k_ref[j * bk:(j + 1) * bk, :]
            for m in range(n_maps):
                s = _dot_nt(kb, q_maps[m])
                if j == nk - 1:
                    s = jnp.where(causal, s, -jnp.inf)
                s_scr[m, j] = s
                mt[m] = merge(mt[m], fold(s, jnp.max), jnp.maximum)
        q_max = [jnp.max(t, axis=0, keepdims=True) for t in mt]

        lt = [None] * n_maps
        acc = [None] * n_maps
        for j in range(nk):
            vt = vt_scr[:, j * bk:(j + 1) * bk]
            for m in range(n_maps):
                p = jnp.exp2(s_scr[m, j] - q_max[m])
                lt[m] = merge(lt[m], fold(p, jnp.sum), jnp.add)
                acc[m] = merge(acc[m], _dot(vt, p.astype(BF16)), jnp.add)

        lam = (jnp.exp(jnp.sum(lq1_ref[...] * lk1_ref[...], axis=-1, keepdims=True))
               - jnp.exp(jnp.sum(lq2_ref[...] * lk2_ref[...], axis=-1, keepdims=True)) + LAM_INIT)
        l1 = jnp.sum(lt[0], axis=0, keepdims=True)
        l2 = jnp.sum(lt[1], axis=0, keepdims=True)
        o = acc[0] / l1 - lam * (acc[1] / l2)
        o = o * lax.rsqrt(jnp.mean(o * o, axis=0, keepdims=True) + SUBLN_EPS) * sw_ref[...]
        o_ref[...] = (o * (1.0 - LAM_INIT)).T.astype(BF16)

    for nk in range(1, k_ref.shape[0] // bk + 1):
        pl.when(qi == nk - 1)(lambda nk=nk: attend(nk))


def _attn_ssd(q, k, v, lam_q1, lam_k1, lam_q2, lam_k2, subln_w, xbc, dtr, z, conv_w, conv_b, dt_bias, a_log, d_skip,
              norm_w, bsz, seq):
    nq = seq // ATT_BQ
    nc = seq // SSD_CHUNK
    assert nc == ATT_HEADS * nq
    qspec = pl.BlockSpec((ATT_BQ, LANES), lambda b, h, i: (b * nq + i, h))
    kvspec = pl.BlockSpec((seq, LANES), lambda b, h, i: (b, h))
    full = lambda a: pl.BlockSpec(a.shape, lambda b, h, i: (0, 0))
    chunk = lambda n: pl.BlockSpec((SSD_CHUNK, n), lambda b, h, i: (b * nc + h * nq + i, 0))
    lams = [a.reshape(1, -1) for a in (lam_q1, lam_k1, lam_q2, lam_k2)]
    sw = subln_w.reshape(-1, 1)
    pad_h = lambda v: jnp.pad(v.reshape(1, SSD_HEADS), ((0, 0), (0, LANES - SSD_HEADS)))
    dskip_lanes = jnp.repeat(d_skip, SSD_HEAD_DIM).reshape(1, SSD_WIDTH)
    consts = (conv_w, conv_b.reshape(1, -1), pad_h(dt_bias), pad_h(a_log), dskip_lanes, norm_w.reshape(1, -1))
    return pl.pallas_call(
        _attn_ssd_kernel,
        grid=(bsz, ATT_HEADS, nq),
        in_specs=[qspec, kvspec, kvspec] + [full(a) for a in lams] + [full(sw)]
                 + [chunk(SSD_CONV_CH), chunk(LANES), chunk(SSD_WIDTH)] + [full(a) for a in consts],
        out_specs=[qspec, chunk(SSD_WIDTH)],
        out_shape=[jax.ShapeDtypeStruct((bsz * seq, ATT_WIDTH), BF16),
                   jax.ShapeDtypeStruct((bsz * seq, SSD_WIDTH), BF16)],
        scratch_shapes=[pltpu.VMEM((2, seq // ATT_BK, ATT_BK, ATT_BQ), F32),
                        pltpu.VMEM((LANES, seq), BF16),
                        pltpu.VMEM((SSD_CHUNK + 2 * SUBLANES, SSD_CONV_CH), F32),
                        pltpu.VMEM((SSD_HEADS // 2, SSD_STATE, LANES), F32)],
        compiler_params=_cparams(("arbitrary", "arbitrary", "arbitrary")),
        name="attn_ssd",
    )(q, k, v, *lams, sw, xbc, dtr, z, *consts)


def _mixer(x2, g_mix, w_in, conv_w, conv_b, dt_bias, a_log, d_skip, ssd_norm_w,
           lam_q1, lam_k1, lam_q2, lam_k2, subln_w, bsz, seq):
    z, xbc, dtr, q, k, v = _in_proj(x2, g_mix.reshape(1, -1), w_in.T)
    y_att, y_ssd = _attn_ssd(q, k, v, lam_q1, lam_k1, lam_q2, lam_k2, subln_w, xbc, dtr, z, conv_w, conv_b, dt_bias,
                             a_log, d_skip, ssd_norm_w, bsz, seq)
    return y_ssd, y_att


def _out_proj_kernel(x_ref, ys_ref, ya_ref, wo_ref, g_ref, wr2_ref, br_ref,
                     x1_ref, xn_ref, route_ref, gate_ref, cnt_ref):
    tm = TM_ROUTE
    x1 = x_ref[...] + _dot(jnp.concatenate([ys_ref[...], ya_ref[...]], axis=-1), wo_ref[...])
    x1_ref[...] = x1
    xn_all = _rms(x1, g_ref[...], NORM_EPS)
    xn_b = xn_all.astype(BF16)
    xn_ref[...] = xn_b

    n_tok = OUT_SUB * tm
    lg2 = _dot(xn_b, wr2_ref[...])
    logits = (lg2[:, :LANES] + lg2[:, LANES:]).T[:N_EXPERTS, :] + br_ref[...]

    eidx = lax.broadcasted_iota(jnp.int32, (N_EXPERTS, n_tok), 0).astype(F32)
    work = logits
    vals, idxs, hots = [], [], []
    for _ in range(TOP_K):
        m = jnp.max(work, axis=0, keepdims=True)
        idx = jnp.min(jnp.where(work == m, eidx, float(N_EXPERTS)), axis=0, keepdims=True)
        hot = eidx == idx
        vals.append(m)
        idxs.append(idx.astype(jnp.int32))
        hots.append(hot)
        work = jnp.where(hot, -jnp.inf, work)
    exps = [jnp.exp(v - vals[0]) for v in vals]
    denom = exps[0] + exps[1] + exps[2] + exps[3]
    gates = [e / denom for e in exps]

    cnt = jnp.zeros((N_EXPERTS, n_tok), F32)
    for hot in hots:
        cnt = cnt + jnp.where(hot, 1.0, 0.0)
    cnt_b = cnt.astype(BF16)
    r = lax.broadcasted_iota(jnp.int32, (tm, tm), 0)
    c = lax.broadcasted_iota(jnp.int32, (tm, tm), 1)
    earlier_tok = jnp.where(r < c, 1.0, 0.0).astype(BF16)
    er = lax.broadcasted_iota(jnp.int32, (N_EXPERTS, N_EXPERTS), 0)
    ec = lax.broadcasted_iota(jnp.int32, (N_EXPERTS, N_EXPERTS), 1)
    lower_exp = jnp.where(ec < er, 1.0, 0.0).astype(BF16)
    cnt_pad = jnp.concatenate([cnt_b, jnp.zeros((LANES - N_EXPERTS, n_tok), BF16)], axis=0)
    pos = []
    for sub in range(OUT_SUB):
        cols = slice(sub * tm, (sub + 1) * tm)
        run = jnp.sum(cnt[:, cols], axis=1, keepdims=True)
        run_even = 2.0 * jnp.floor(0.5 * run + 0.5)
        run_start = _dot(lower_exp, jnp.broadcast_to(run_even, (N_EXPERTS, LANES)).astype(BF16))[:, 0:1]
        pos.append(_dot(cnt_b[:, cols], earlier_tok) + run_start)
        cnt_ref[sub] = _dot_nt(jnp.ones((SUBLANES, tm), BF16), cnt_pad[:, cols])
    pos = jnp.concatenate(pos, axis=1)
    lps = [jnp.sum(jnp.where(hot, pos, 0.0), axis=0, keepdims=True).astype(jnp.int32) for hot in hots]

    route_ref[...] = jnp.concatenate(idxs + lps, axis=0)
    gate_ref[...] = jnp.concatenate(gates + [jnp.zeros((SUBLANES - TOP_K, n_tok), F32)], axis=0)


def _out_proj(x2, y_ssd, y_att, w_out, g_ffn, w_router, b_router):
    t = x2.shape[0]
    tm = OUT_SUB * TM_ROUTE
    nt = t // TM_ROUTE
    wo = w_out.astype(BF16)
    wrh, wrm, _ = _split3(jnp.pad(w_router, ((0, 0), (0, LANES - N_EXPERTS))))
    wr2 = jnp.concatenate([wrh, wrm], axis=1)
    br = b_router.reshape(N_EXPERTS, 1)
    row = lambda n: pl.BlockSpec((tm, n), lambda i: (i, 0))
    col = pl.BlockSpec((SUBLANES, tm), lambda i: (0, i))
    full = lambda a: pl.BlockSpec(a.shape, lambda i: (0, 0))
    args = (x2, y_ssd, y_att, wo, g_ffn.reshape(1, -1), wr2, br)
    return pl.pallas_call(
        _out_proj_kernel,
        grid=(t // tm,),
        in_specs=[row(D_MODEL), row(SSD_WIDTH), row(ATT_WIDTH)] + [full(a) for a in args[3:]],
        out_specs=[row(D_MODEL), row(D_MODEL), col, col,
                   pl.BlockSpec((OUT_SUB, SUBLANES, LANES), lambda i: (i, 0, 0))],
        out_shape=[
            jax.ShapeDtypeStruct((t, D_MODEL), F32),
            jax.ShapeDtypeStruct((t, D_MODEL), BF16),
            jax.ShapeDtypeStruct((SUBLANES, t), jnp.int32),
            jax.ShapeDtypeStruct((SUBLANES, t), F32),
            jax.ShapeDtypeStruct((nt, SUBLANES, LANES), F32),
        ],
        compiler_params=_cparams(("arbitrary",)),
        name="out_proj",
    )(*args)


def _copy_run(n, start_copy):
    @pl.when(n > 0)
    def _():
        start_copy(n)


def _select_by_position(positions, values, n_pos):
    tm = positions[0].shape[1]
    r = lax.broadcasted_iota(jnp.int32, (n_pos, tm), 0)
    out = jnp.zeros((n_pos, tm), F32)
    for k in reversed(range(TOP_K)):
        out = jnp.where(r == positions[k], values[k], out)
    return out


def _dispatch_kernel(cnt_ref, ls_ref, base_ref, zoff_ref, rows_ref, xn_ref, route_ref, xs_hbm, xloc, zeros_vmem,
                     sem_z, sems):
    n_pos = N_POS
    i = pl.program_id(0)
    n_steps = pl.num_programs(0)
    slot = lax.rem(i, 2)

    def zero_fill(op):
        def pad_rows(e, _):
            n = zoff_ref[N_EXPERTS + 1 + e]

            @pl.when(n > 0)
            def _():
                op(pltpu.make_async_copy(zeros_vmem.at[_row_slice(0, n, PACK_SUB)],
                                         xs_hbm.at[_row_slice(zoff_ref[e], n, PACK_SUB)], sem_z))
            return 0

        def unused_block(b, _):
            op(pltpu.make_async_copy(zeros_vmem, xs_hbm.at[_row_slice(b * ROW_BLK, ROW_BLK, PACK_SUB)], sem_z))
            return 0

        lax.fori_loop(0, N_EXPERTS, pad_rows, 0)
        lax.fori_loop(zoff_ref[N_EXPERTS], xs_hbm.shape[0] // (ROW_BLK * PACK_SUB), unused_block, 0)

    @pl.when(i == 0)
    def _():
        zeros_vmem[...] = jnp.zeros_like(zeros_vmem)
        zero_fill(lambda cp: cp.start())

    def slot_wait(s, step):
        n = rows_ref[step]
        pltpu.make_async_copy(xloc.at[s, _row_slice(0, n, PACK_SUB)], xs_hbm.at[_row_slice(0, n, PACK_SUB)],
                              sems.at[s]).wait()

    @pl.when(i >= 2)
    def _():
        slot_wait(slot, i - 2)

    positions = [route_ref[TOP_K + k:TOP_K + k + 1, :] for k in range(TOP_K)]
    sel = _select_by_position(positions, [1.0] * TOP_K, n_pos).astype(BF16)
    rows = _dot(sel, xn_ref[...])
    _pack_rows(xloc.at[slot], rows)

    def per_expert(e, _):
        idx = i * N_EXPERTS + e
        src0 = ls_ref[idx]
        dst0 = base_ref[idx]

        def start_copy(n):
            pltpu.make_async_copy(xloc.at[slot, _row_slice(src0, n, PACK_SUB)],
                                  xs_hbm.at[_row_slice(dst0, n, PACK_SUB)], sems.at[slot]).start()

        _copy_run(cnt_ref[idx], start_copy)
        return 0

    lax.fori_loop(0, N_EXPERTS, per_expert, 0)

    @pl.when(i == n_steps - 1)
    def _():
        slot_wait(slot, i)

        @pl.when(n_steps > 1)
        def _():
            slot_wait(1 - slot, i - 1)

        zero_fill(lambda cp: cp.wait())


def _dispatch(xn, route, tables, n_rows):
    t = xn.shape[0]
    tm = TM_ROUTE
    cnt_tbl, ls_tbl, base_tbl, zoff, tile_rows = tables
    return pl.pallas_call(
        _dispatch_kernel,
        grid_spec=pltpu.PrefetchScalarGridSpec(
            num_scalar_prefetch=5,
            grid=(t // tm,),
            in_specs=[pl.BlockSpec((tm, D_MODEL), lambda i, *_: (i, 0)),
                      pl.BlockSpec((SUBLANES, tm), lambda i, *_: (0, i))],
            out_specs=pl.BlockSpec(memory_space=pl.ANY),
            scratch_shapes=[
                pltpu.VMEM((2, N_POS * PACK_SUB, LANES), jnp.uint32),
                pltpu.VMEM((ROW_BLK * PACK_SUB, LANES), jnp.uint32),
                pltpu.SemaphoreType.DMA,
                pltpu.SemaphoreType.DMA((2,)),
            ],
        ),
        out_shape=jax.ShapeDtypeStruct(((n_rows + ROW_BLK) * PACK_SUB, LANES), jnp.uint32),
        compiler_params=_cparams(("arbitrary",)),
        name="dispatch",
    )(cnt_tbl, ls_tbl, base_tbl, zoff, tile_rows, xn, route)


def _experts_kernel(blk_e_ref, nvalid_ref, first_ref, wslot_ref, enext_ref, pieces_ref, xs_ref, wup_hbm, wdn_hbm,
                    bgu_ref, bd_ref, ys_ref, wup_buf, wdn_buf, wg_s, wu_s, wd_s, sems):
    i = pl.program_id(0)
    slot = wslot_ref[i]

    def weight_copies(e, s):
        return (pltpu.make_async_copy(wup_hbm.at[e], wup_buf.at[s], sems.at[0, s]),
                pltpu.make_async_copy(wdn_hbm.at[e], wdn_buf.at[s], sems.at[1, s]))

    @pl.when(i == 0)
    def _():
        for cp in weight_copies(blk_e_ref[0], slot):
            cp.start()

    @pl.when(first_ref[i] != 0)
    def _():
        for cp in weight_copies(blk_e_ref[i], slot):
            cp.wait()

        @pl.when(enext_ref[i] >= 0)
        def _():
            for cp in weight_copies(enext_ref[i], 1 - slot):
                cp.start()

        src = lax.broadcasted_iota(jnp.int32, (DEINT, DEINT), 0)
        dst = lax.broadcasted_iota(jnp.int32, (DEINT, DEINT), 1)
        perm = jnp.where(src == jnp.where(dst < LANES, 2 * dst, 2 * (dst - LANES) + 1), 1.0, 0.0).astype(BF16)
        for g in range(2 * D_EXPERT // DEINT):
            sep = _dot(wup_buf[slot, :, g * DEINT:(g + 1) * DEINT].astype(BF16), perm)
            wg_s[:, g * LANES:(g + 1) * LANES] = sep[:, :LANES].astype(BF16)
            wu_s[:, g * LANES:(g + 1) * LANES] = sep[:, LANES:].astype(BF16)
        wd_s[...] = wdn_buf[slot].astype(BF16)

    def ffn(n_rows):
        used = n_rows * ROW_SUB
        if n_rows:
            xb = _unpack_rows(xs_ref.at[pl.ds(0, n_rows * PACK_SUB)])
            gate = jnp.minimum(_dot(xb, wg_s[...]) + bgu_ref[0, 0:1, :], SWIGLU_LIMIT)
            up = jnp.clip(_dot(xb, wu_s[...]) + bgu_ref[0, 1:2, :], -SWIGLU_LIMIT, SWIGLU_LIMIT)
            act = (up + 1.0) * gate * jax.nn.sigmoid(SWIGLU_ALPHA * gate)
            _store_rows(ys_ref.at[pl.ds(0, used)], _dot(act.astype(BF16), wd_s[...]) + bd_ref[0])
        if n_rows < ROW_BLK:
            ys_ref[used:, :] = jnp.zeros((ROW_BLK * ROW_SUB - used, LANES), F32)

    for pieces in range(ROW_BLK // ROW_PIECE + 1):
        pl.when(pieces_ref[i] == pieces)(lambda pieces=pieces: ffn(pieces * ROW_PIECE))


def _experts(xs, plan, w_up, b_up, w_down, b_down, n_rows):
    nb = n_rows // ROW_BLK
    b_gu = b_up.reshape(N_EXPERTS, D_EXPERT, 2).transpose(0, 2, 1)
    bd = b_down.reshape(N_EXPERTS, 1, D_MODEL)
    blk_e, nvalid, first, wslot, enext, pieces = plan
    src = lambda i, be, nv, *_: (jnp.minimum(i, nv[0] - 1), 0)
    bspec = lambda r, m: pl.BlockSpec((1, r, m), lambda i, be, *_: (be[i], 0, 0))
    anyspec = pl.BlockSpec(memory_space=pl.ANY)
    return pl.pallas_call(
        _experts_kernel,
        grid_spec=pltpu.PrefetchScalarGridSpec(
            num_scalar_prefetch=6,
            grid=(nb,),
            in_specs=[pl.BlockSpec((ROW_BLK * PACK_SUB, LANES), src), anyspec, anyspec,
                      bspec(2, D_EXPERT), bspec(1, D_MODEL)],
            out_specs=_rows_spec(ROW_BLK, lambda i, *_: (i, 0)),
            scratch_shapes=[
                pltpu.VMEM((2, D_MODEL, 2 * D_EXPERT), F32),
                pltpu.VMEM((2, D_EXPERT, D_MODEL), F32),
                pltpu.VMEM((D_MODEL, D_EXPERT), BF16),
                pltpu.VMEM((D_MODEL, D_EXPERT), BF16),
                pltpu.VMEM((D_EXPERT, D_MODEL), BF16),
                pltpu.SemaphoreType.DMA((2, 2)),
            ],
        ),
        out_shape=jax.ShapeDtypeStruct((n_rows * ROW_SUB, LANES), F32),
        compiler_params=_cparams(("arbitrary",)),
        name="experts",
    )(blk_e, nvalid, first, wslot, enext, pieces, xs, w_up, w_down, b_gu, bd)


def _combine_kernel(cnt_ref, ls_ref, base_ref, rows_ref, ys_hbm, route_ref, gate_ref, x1_ref, p_ref, gp_ref, wpg_ref,
                    wpp_ref, gf_ref, o_ref, yloc, sems):
    tm = TM_ROUTE
    n_pos = N_POS
    i = pl.program_id(0)
    n_steps = pl.num_programs(0)
    slot = lax.rem(i, 2)

    def gather_step(step, s):
        for u in range(COMB_SUB):
            def per_expert(e, _, u=u):
                idx = (step * COMB_SUB + u) * N_EXPERTS + e
                src0 = base_ref[idx]
                dst0 = ls_ref[idx]

                def start_copy(n):
                    pltpu.make_async_copy(ys_hbm.at[_row_slice(src0, n)],
                                          yloc.at[s, u, _row_slice(dst0, n)], sems.at[s]).start()

                _copy_run(cnt_ref[idx], start_copy)
                return 0

            lax.fori_loop(0, N_EXPERTS, per_expert, 0)

    @pl.when(i == 0)
    def _():
        yloc[...] = jnp.zeros_like(yloc)
        gather_step(0, 0)

    @pl.when(i + 1 < n_steps)
    def _():
        gather_step(i + 1, 1 - slot)

    for u in range(COMB_SUB):
        n = rows_ref[i * COMB_SUB + u]
        pltpu.make_async_copy(ys_hbm.at[_row_slice(0, n)], yloc.at[slot, u, _row_slice(0, n)], sems.at[slot]).wait()

    moe = []
    for u in range(COMB_SUB):
        cols = slice(u * tm, (u + 1) * tm)
        positions = [route_ref[TOP_K + k:TOP_K + k + 1, cols] for k in range(TOP_K)]
        gsel = _select_by_position(positions, [gate_ref[k:k + 1, cols] for k in range(TOP_K)], n_pos)
        g_pos = jnp.sum(gsel, axis=1, keepdims=True)
        sel = jnp.where(gsel != 0.0, 1.0, 0.0).astype(BF16)
        y_gated = (_load_rows(yloc.at[slot, u]) * g_pos).astype(BF16)
        moe.append(_dot_tn(sel, y_gated))
    x2 = x1_ref[...] + jnp.concatenate(moe, axis=0)
    xn = _rms(x2, gp_ref[...], NORM_EPS).astype(BF16)
    pp = _dot(p_ref[...].astype(BF16), wpp_ref[...])
    x3 = x2 + pp * jax.nn.sigmoid(_dot(xn, wpg_ref[...]))
    o_ref[...] = _rms(x3, gf_ref[...], NORM_EPS)


def _combine(ys, route, gate_t, x1, p2, tables, g_ple, w_ple_gate, w_ple_proj, g_final):
    t = x1.shape[0]
    tm = COMB_SUB * TM_ROUTE
    cnt_tbl, ls_tbl, base_tbl, _, tile_rows = tables
    row = lambda n: pl.BlockSpec((tm, n), lambda i, *_: (i, 0))
    col = pl.BlockSpec((SUBLANES, tm), lambda i, *_: (0, i))
    full = lambda a: pl.BlockSpec(a.shape, lambda i, *_: (0, 0))
    consts = (g_ple.reshape(1, -1), w_ple_gate.astype(BF16), w_ple_proj.astype(BF16), g_final.reshape(1, -1))
    return pl.pallas_call(
        _combine_kernel,
        grid_spec=pltpu.PrefetchScalarGridSpec(
            num_scalar_prefetch=4,
            grid=(t // tm,),
            in_specs=[pl.BlockSpec(memory_space=pl.ANY), col, col, row(D_MODEL), row(PLE_DIM)]
                     + [full(a) for a in consts],
            out_specs=row(D_MODEL),
            scratch_shapes=[
                pltpu.VMEM((2, COMB_SUB, N_POS * ROW_SUB, LANES), F32),
                pltpu.SemaphoreType.DMA((2,)),
            ],
        ),
        out_shape=jax.ShapeDtypeStruct((t, D_MODEL), F32),
        compiler_params=_cparams(("arbitrary",)),
        name="combine",
    )(cnt_tbl, ls_tbl, base_tbl, tile_rows, ys, route, gate_t, x1, p2, *consts)


def _routing_tables(cnt, n_blocks):
    tile_cnt = cnt[:, 0, :N_EXPERTS].astype(jnp.int32)
    tile_cnt = tile_cnt + tile_cnt % 2
    counts = jnp.sum(tile_cnt, axis=0)
    padded = (counts + ROW_BLK - 1) // ROW_BLK * ROW_BLK
    pend = jnp.cumsum(padded)
    pstart = pend - padded
    base = pstart[None, :] + jnp.cumsum(tile_cnt, axis=0) - tile_cnt
    lstart = jnp.cumsum(tile_cnt, axis=1) - tile_cnt
    nvalid = pend[-1:] // ROW_BLK
    zoff = jnp.concatenate([pstart + counts, nvalid, padded - counts])
    blk_start = jnp.minimum(jnp.arange(n_blocks, dtype=jnp.int32) * ROW_BLK, pend[-1] - 1)
    blk_e = jnp.minimum(jnp.sum((pend[None, :] <= blk_start[:, None]).astype(jnp.int32), axis=1), N_EXPERTS - 1)
    first = jnp.concatenate([jnp.ones((1,), bool), blk_e[1:] != blk_e[:-1]])
    wslot = (jnp.cumsum(first.astype(jnp.int32)) - 1) % 2
    eids = jnp.arange(N_EXPERTS, dtype=jnp.int32)
    later_nonempty = (eids[None, :] > eids[:, None]) & (padded[None, :] > 0)
    next_e = jnp.min(jnp.where(later_nonempty, eids[None, :], N_EXPERTS), axis=1)
    next_e = jnp.where(next_e == N_EXPERTS, -1, next_e)
    blk_hot = blk_e[:, None] == eids[None, :]
    per_block = lambda v: jnp.sum(jnp.where(blk_hot, v[None, :], 0), axis=1)
    blk_ids = jnp.arange(n_blocks, dtype=jnp.int32)
    real_rows = jnp.clip(per_block(pstart + counts) - blk_ids * ROW_BLK, 0, ROW_BLK) * (blk_ids < nvalid[0])
    pieces = (real_rows + ROW_PIECE - 1) // ROW_PIECE
    tile_rows = jnp.sum(tile_cnt, axis=1)
    i32 = lambda a: a.reshape(-1).astype(jnp.int32)
    plan = (i32(blk_e), i32(nvalid), i32(first), i32(wslot), i32(per_block(next_e)), i32(pieces))
    return (i32(tile_cnt), i32(lstart), i32(base), i32(zoff), i32(tile_rows)), plan


def kernel(x, p, g_mix, w_in, conv_w, conv_b, dt_bias, a_log, d_skip, ssd_norm_w, lam_q1, lam_k1, lam_q2, lam_k2, subln_w, w_out, g_ffn, w_router, b_router, w_up, b_up, w_down, b_down, g_ple, w_ple_gate, w_ple_proj, g_final):
    bsz, seq, d = x.shape
    t = bsz * seq
    x2 = x.reshape(t, d)
    y_ssd, y_att = _mixer(x2, g_mix[0], w_in[0], conv_w[0], conv_b[0], dt_bias[0], a_log[0], d_skip[0], ssd_norm_w[0],
                          lam_q1[0], lam_k1[0], lam_q2[0], lam_k2[0], subln_w[0], bsz, seq)
    x1, xn, route, gate_t, cnt = _out_proj(x2, y_ssd, y_att, w_out[0], g_ffn[0], w_router[0], b_router[0])

    n_rows = t * TOP_K + (t // TM_ROUTE) * N_EXPERTS + N_EXPERTS * ROW_BLK
    n_rows = -(-n_rows // ROW_BLK) * ROW_BLK
    tables, plan = _routing_tables(cnt, n_rows // ROW_BLK)
    xs = _dispatch(xn, route, tables, n_rows)
    ys = _experts(xs, plan, w_up[0], b_up[0], w_down[0], b_down[0], n_rows)
    out = _combine(ys, route, gate_t, x1, p[0].reshape(t, PLE_DIM), tables, g_ple[0], w_ple_gate[0], w_ple_proj[0],
                   g_final)
    return out.reshape(bsz, seq, d)
```

```python
import math

import jax
import jax.numpy as jnp
from jax import lax
from jax.experimental import pallas as pl
from jax.experimental.pallas import tpu as pltpu

F32 = jnp.float32
BF16 = jnp.bfloat16

D_MODEL = 1024
PLE_DIM = 256
SSD_WIDTH = 512
ATT_WIDTH = 512
SSD_HEAD_DIM = 64
SSD_HEADS = 8
SSD_GROUPS = 2
SSD_STATE = 128
SSD_CONV = 4
SSD_CHUNK = 128
SSD_CONV_CH = SSD_WIDTH + 2 * SSD_GROUPS * SSD_STATE
SSD_NORM_EPS = 1e-5
ATT_HEAD_DIM = 64
ATT_HEADS = 4
SUBLN_EPS = 1e-5
OFF_Z = 0
OFF_XBC = OFF_Z + SSD_WIDTH
OFF_DT = OFF_XBC + SSD_CONV_CH
OFF_Q = OFF_DT + SSD_HEADS
OFF_K = OFF_Q + ATT_WIDTH
OFF_V = OFF_K + ATT_WIDTH
IN_PROJ = OFF_V + ATT_WIDTH
N_EXPERTS = 32
TOP_K = 4
D_EXPERT = 1024
SWIGLU_LIMIT = 7.0
SWIGLU_ALPHA = 1.702
NORM_EPS = 1e-6
LAM_INIT = 0.8 - 0.6 * math.exp(-0.3 * 0)

LANES = 128
SUBLANES = 8
VMEM_LIMIT_BYTES = 56 * 1024 * 1024
ROW_SUB = D_MODEL // LANES
PACK_SUB = ROW_SUB // 2

TM_PROJ = 512
ATT_BQ = 512
ATT_BK = 512
ROW_BLK = 512
ROW_PIECE = 128
TM_ROUTE = 256
N_POS = TOP_K * TM_ROUTE + N_EXPERTS
OUT_SUB = 4
COMB_SUB = 2
DEINT = 2 * LANES
ZXD_WIDTH = OFF_DT + LANES
CP_CONV_B = SSD_CONV
CP_WIDE = SSD_CONV + 1
CP_SMALL = SSD_CONV + 2
CP_DT_BIAS = 4 * LANES
CP_A_LOG = 5 * LANES


def _cparams(sem):
    return pltpu.CompilerParams(dimension_semantics=sem, vmem_limit_bytes=VMEM_LIMIT_BYTES)


def _rms(x, w, eps):
    return x * lax.rsqrt(jnp.mean(x * x, axis=-1, keepdims=True) + eps) * w


def _dot(a, b):
    return jnp.dot(a, b, preferred_element_type=F32)


def _dot_nt(a, b):
    return lax.dot_general(a, b, (((1,), (1,)), ((), ())), preferred_element_type=F32)


def _dot_tn(a, b):
    return lax.dot_general(a, b, (((0,), (0,)), ((), ())), preferred_element_type=F32)


def _store_rows(ref, val):
    n = val.shape[0]
    for s in range(ROW_SUB):
        ref[pl.ds(s, n, stride=ROW_SUB), :] = val[:, s * LANES:(s + 1) * LANES]


def _load_rows(ref):
    n = ref.shape[0] // ROW_SUB
    return jnp.concatenate([ref[pl.ds(s, n, stride=ROW_SUB), :] for s in range(ROW_SUB)], axis=-1)


def _rows_spec(n, index_map):
    return pl.BlockSpec((n * ROW_SUB, LANES), index_map)


def _row_slice(start, n, sub=ROW_SUB):
    return pl.ds(pl.multiple_of(start * sub, SUBLANES), n * sub)


def _pack_rows(ref, val):
    n = val.shape[0]
    bits = lambda v: lax.bitcast_convert_type(v, jnp.uint32)
    words = (bits(val[:, :D_MODEL // 2]) & jnp.uint32(0xFFFF0000)) | (bits(val[:, D_MODEL // 2:]) >> 16)
    for s in range(PACK_SUB):
        ref[pl.ds(s, n, stride=PACK_SUB), :] = words[:, s * LANES:(s + 1) * LANES]


def _unpack_rows(ref):
    n = ref.shape[0] // PACK_SUB
    words = [ref[pl.ds(s, n, stride=PACK_SUB), :] for s in range(PACK_SUB)]
    as_f32 = lambda w: lax.bitcast_convert_type(w, F32)
    hi = [as_f32(w & jnp.uint32(0xFFFF0000)) for w in words]
    lo = [as_f32(w << 16) for w in words]
    return jnp.concatenate(hi + lo, axis=-1).astype(BF16)


def _split3(x):
    hi = x.astype(BF16)
    r1 = x - hi.astype(F32)
    mid = r1.astype(BF16)
    lo = (r1 - mid.astype(F32)).astype(BF16)
    return hi, mid, lo


def _in_proj_kernel(x_ref, g_ref, win_ref, zxd_ref, q_ref, k_ref, v_ref, w_ref):
    @pl.when(pl.program_id(0) == 0)
    def _():
        w_ref[:OFF_DT, :] = win_ref[:OFF_DT, :].astype(BF16)
        dt_rows = jnp.concatenate([win_ref[OFF_DT:OFF_Q, :], jnp.zeros((LANES - SSD_HEADS, D_MODEL), F32)], axis=0)
        w_ref[OFF_DT:OFF_DT + LANES, :] = dt_rows.astype(BF16)
        w_ref[OFF_DT + LANES:, :] = win_ref[OFF_Q:, :].astype(BF16)

    h = _rms(x_ref[...], g_ref[...], NORM_EPS).astype(BF16)
    q0 = ZXD_WIDTH
    k0, v0 = q0 + ATT_WIDTH, q0 + 2 * ATT_WIDTH
    zxd_ref[...] = _dot_nt(h, w_ref[:q0, :])
    q_ref[...] = (_dot_nt(h, w_ref[q0:k0, :]) * (ATT_HEAD_DIM ** -0.5 * math.log2(math.e))).astype(BF16)
    k_ref[...] = _dot_nt(h, w_ref[k0:v0, :]).astype(BF16)
    v_ref[...] = _dot_nt(h, w_ref[v0:v0 + ATT_WIDTH, :]).astype(BF16)


def _in_proj(x2, g_mix, w_in_t):
    t = x2.shape[0]
    tm = TM_PROJ
    row = lambda n: pl.BlockSpec((tm, n), lambda i: (i, 0))
    full = lambda a: pl.BlockSpec(a.shape, lambda i: (0, 0))
    return pl.pallas_call(
        _in_proj_kernel,
        grid=(t // tm,),
        in_specs=[row(D_MODEL), full(g_mix), full(w_in_t)],
        out_specs=[row(ZXD_WIDTH), row(ATT_WIDTH), row(ATT_WIDTH), row(ATT_WIDTH)],
        out_shape=[
            jax.ShapeDtypeStruct((t, ZXD_WIDTH), F32),
            jax.ShapeDtypeStruct((t, ATT_WIDTH), BF16),
            jax.ShapeDtypeStruct((t, ATT_WIDTH), BF16),
            jax.ShapeDtypeStruct((t, ATT_WIDTH), BF16),
        ],
        scratch_shapes=[pltpu.VMEM((IN_PROJ + LANES - SSD_HEADS, D_MODEL), BF16)],
        compiler_params=_cparams(("arbitrary",)),
        name="in_proj",
    )(x2, g_mix, w_in_t)


def _ssd_chunk_init(c, xpad_ref, state_ref):
    L = SSD_CHUNK

    @pl.when(c == 0)
    def _():
        xpad_ref[0:SUBLANES, :] = jnp.zeros((SUBLANES, SSD_CONV_CH), F32)
        state_ref[...] = jnp.zeros_like(state_ref)

    @pl.when(c != 0)
    def _():
        xpad_ref[0:SUBLANES, :] = xpad_ref[L:L + SUBLANES, :]


def _ssd_chunk(zxd_ref, cp_ref, y_ref, xpad_ref, state_ref):
    L = SSD_CHUNK
    xpad_ref[SUBLANES:SUBLANES + L, :] = zxd_ref[:, OFF_XBC:OFF_DT]

    conv = cp_ref[CP_CONV_B:CP_CONV_B + 1, :]
    for j in range(SSD_CONV):
        off = SUBLANES - (SSD_CONV - 1) + j
        conv = conv + cp_ref[j:j + 1, :] * xpad_ref[off:off + L, :]
    act = conv * jax.nn.sigmoid(conv)
    xs = act[:, :SSD_WIDTH]
    bm = act[:, SSD_WIDTH:SSD_WIDTH + SSD_GROUPS * SSD_STATE].astype(BF16)
    cm = act[:, SSD_WIDTH + SSD_GROUPS * SSD_STATE:].astype(BF16)

    dt_in = zxd_ref[:, OFF_DT:] + cp_ref[CP_SMALL:CP_SMALL + 1, CP_DT_BIAS:CP_DT_BIAS + LANES]
    dt_all = jnp.maximum(dt_in, 0.0) + jnp.log1p(jnp.exp(-jnp.abs(dt_in)))
    adt = dt_all * (-jnp.exp(cp_ref[CP_SMALL:CP_SMALL + 1, CP_A_LOG:CP_A_LOG + LANES]))

    ri = lax.broadcasted_iota(jnp.int32, (L, L), 0)
    ci = lax.broadcasted_iota(jnp.int32, (L, L), 1)
    causal = ci <= ri
    tril = jnp.where(causal, 1.0, 0.0).astype(BF16)
    hi, mid, lo = _split3(adt)
    acum_all = _dot(tril, hi) + _dot(tril, mid) + _dot(tril, lo)
    acum_t = acum_all.T
    a_last = acum_all[L - 1:L, :]
    decay_in_all = jnp.exp(a_last - acum_all)
    decay_out_all = jnp.exp(acum_all)
    chunk_decay_all = jnp.exp(a_last)
    acum = lambda h: acum_all[:, h:h + 1]
    dt = lambda h: dt_all[:, h:h + 1]
    decay_out = lambda h: decay_out_all[:, h:h + 1]
    decay_in = lambda h: decay_in_all[:, h:h + 1]
    chunk_decay = lambda h: chunk_decay_all[:, h:h + 1]

    lane = lax.broadcasted_iota(jnp.int32, (L, LANES), 1)
    lo_half = lane < SSD_HEAD_DIM

    def per_pair(col_a, col_b):
        return jnp.where(lo_half, col_a, col_b)

    ys = []
    for pair in range(SSD_HEADS // 2):
        g = pair // 2
        h0, h1 = 2 * pair, 2 * pair + 1
        cg = cm[:, g * SSD_STATE:(g + 1) * SSD_STATE]
        bg = bm[:, g * SSD_STATE:(g + 1) * SSD_STATE]
        cb = _dot_nt(cg, bg)
        x_pair = xs[:, pair * LANES:(pair + 1) * LANES]
        xdt = x_pair * per_pair(dt(h0), dt(h1))
        y_pair = jnp.zeros((L, LANES), F32)
        for hh, keep in ((h0, lo_half), (h1, jnp.logical_not(lo_half))):
            seg = acum(hh) - acum_t[hh:hh + 1, :]
            lmat = jnp.where(causal, jnp.exp(jnp.where(causal, seg, 0.0)), 0.0)
            m = (cb * lmat).astype(BF16)
            y_pair = y_pair + _dot(m, jnp.where(keep, xdt, 0.0).astype(BF16))
        s_prev = state_ref[pair]
        y_off = _dot(cg, s_prev.astype(BF16)) * per_pair(decay_out(h0), decay_out(h1))
        w_in = (xdt * per_pair(decay_in(h0), decay_in(h1))).astype(BF16)
        cd = jnp.where(lane[0:1, :] < SSD_HEAD_DIM, chunk_decay(h0), chunk_decay(h1))
        state_ref[pair] = s_prev * cd + _dot_tn(bg, w_in)
        ys.append(y_pair + y_off + cp_ref[CP_WIDE:CP_WIDE + 1, pair * LANES:(pair + 1) * LANES] * x_pair)

    y = jnp.concatenate(ys, axis=-1)
    zz = zxd_ref[:, OFF_Z:OFF_XBC]
    y = y * (zz * jax.nn.sigmoid(zz))
    gw = SSD_WIDTH // SSD_GROUPS
    outs = []
    for g in range(SSD_GROUPS):
        yg = y[:, g * gw:(g + 1) * gw]
        outs.append(yg * lax.rsqrt(jnp.mean(yg * yg, axis=-1, keepdims=True) + SSD_NORM_EPS))
    y_ref[...] = (jnp.concatenate(outs, axis=-1) * cp_ref[CP_WIDE:CP_WIDE + 1, SSD_WIDTH:]).astype(BF16)


def _attn_ssd_kernel(q_ref, k_ref, v_ref, sw_ref, zxd_ref, cp_ref, o_ref, y_ref, s_scr, vt_scr, xpad_ref, state_ref):
    bq, bk = ATT_BQ, ATT_BK
    qi = pl.program_id(2)
    n_maps = 2
    _ssd_chunk_init(pl.program_id(1) * pl.num_programs(2) + qi, xpad_ref, state_ref)

    @pl.when(qi == 0)
    def _():
        vt_scr[...] = v_ref[...].astype(F32).T.astype(BF16)

    def fold(t, reduce):
        return reduce(t.reshape(bk // SUBLANES, SUBLANES, bq), axis=0)

    def merge(old, new, op):
        return new if old is None else op(old, new)

    def attend(nk):
        _ssd_chunk(zxd_ref, cp_ref, y_ref, xpad_ref, state_ref)
        q = q_ref[...]
        lane = lax.broadcasted_iota(jnp.int32, (bq, LANES), 1)
        zero = jnp.zeros_like(q)
        q_maps = (jnp.where(lane < ATT_HEAD_DIM, q, zero), jnp.where(lane >= ATT_HEAD_DIM, q, zero))
        key = lax.broadcasted_iota(jnp.int32, (bk, bq), 0)
        qry = lax.broadcasted_iota(jnp.int32, (bk, bq), 1)
        causal = key <= qry

        mt = [None] * n_maps
        for j in range(nk):
            kb = k_ref[j * bk:(j + 1) * bk, :]
            for m in range(n_maps):
                s = _dot_nt(kb, q_maps[m])
                if j == nk - 1:
                    s = jnp.where(causal, s, -jnp.inf)
                s_scr[m, j] = s
                mt[m] = merge(mt[m], fold(s, jnp.max), jnp.maximum)
        q_max = [jnp.max(t, axis=0, keepdims=True) for t in mt]

        lt = [None] * n_maps
        acc = [None] * n_maps
        for j in range(nk):
            vt = vt_scr[:, j * bk:(j + 1) * bk]
            for m in range(n_maps):
                p = jnp.exp2(s_scr[m, j] - q_max[m])
                lt[m] = merge(lt[m], fold(p, jnp.sum), jnp.add)
                acc[m] = merge(acc[m], _dot(vt, p.astype(BF16)), jnp.add)

        lams = [cp_ref[CP_SMALL:CP_SMALL + 1, n * LANES:(n + 1) * LANES] for n in range(4)]
        lam = (jnp.exp(jnp.sum(lams[0] * lams[1], axis=-1, keepdims=True))
               - jnp.exp(jnp.sum(lams[2] * lams[3], axis=-1, keepdims=True)) + LAM_INIT)
        l1 = jnp.sum(lt[0], axis=0, keepdims=True)
        l2 = jnp.sum(lt[1], axis=0, keepdims=True)
        o = acc[0] / l1 - lam * (acc[1] / l2)
        o = o * lax.rsqrt(jnp.mean(o * o, axis=0, keepdims=True) + SUBLN_EPS) * sw_ref[...]
        o_ref[...] = (o * (1.0 - LAM_INIT)).T.astype(BF16)

    for nk in range(1, k_ref.shape[0] // bk + 1):
        pl.when(qi == nk - 1)(lambda nk=nk: attend(nk))


def _attn_ssd(q, k, v, lam_q1, lam_k1, lam_q2, lam_k2, subln_w, zxd, conv_w, conv_b, dt_bias, a_log, d_skip,
              norm_w, bsz, seq):
    nq = seq // ATT_BQ
    nc = seq // SSD_CHUNK
    assert nc == ATT_HEADS * nq
    qspec = pl.BlockSpec((ATT_BQ, LANES), lambda b, h, i: (b * nq + i, h))
    kvspec = pl.BlockSpec((seq, LANES), lambda b, h, i: (b, h))
    full = lambda a: pl.BlockSpec(a.shape, lambda b, h, i: (0, 0))
    chunk = lambda n: pl.BlockSpec((SSD_CHUNK, n), lambda b, h, i: (b * nc + h * nq + i, 0))
    sw = subln_w.reshape(-1, 1)
    pad_to = lambda v, n: jnp.pad(v.reshape(1, -1), ((0, 0), (0, n - v.size)))
    wide = jnp.concatenate([jnp.repeat(d_skip, SSD_HEAD_DIM), norm_w]).reshape(1, SSD_CONV_CH)
    small = jnp.concatenate([pad_to(a, LANES) for a in (lam_q1, lam_k1, lam_q2, lam_k2, dt_bias, a_log)], axis=1)
    consts = jnp.concatenate([conv_w, conv_b.reshape(1, -1), wide, pad_to(small, SSD_CONV_CH),
                              jnp.zeros((SUBLANES - CP_SMALL - 1, SSD_CONV_CH), F32)], axis=0)
    return pl.pallas_call(
        _attn_ssd_kernel,
        grid=(bsz, ATT_HEADS, nq),
        in_specs=[qspec, kvspec, kvspec, full(sw), chunk(ZXD_WIDTH), full(consts)],
        out_specs=[qspec, chunk(SSD_WIDTH)],
        out_shape=[jax.ShapeDtypeStruct((bsz * seq, ATT_WIDTH), BF16),
                   jax.ShapeDtypeStruct((bsz * seq, SSD_WIDTH), BF16)],
        scratch_shapes=[pltpu.VMEM((2, seq // ATT_BK, ATT_BK, ATT_BQ), F32),
                        pltpu.VMEM((LANES, seq), BF16),
                        pltpu.VMEM((SSD_CHUNK + 2 * SUBLANES, SSD_CONV_CH), F32),
                        pltpu.VMEM((SSD_HEADS // 2, SSD_STATE, LANES), F32)],
        compiler_params=_cparams(("arbitrary", "arbitrary", "arbitrary")),
        name="attn_ssd",
    )(q, k, v, sw, zxd, consts)


def _mixer(x2, g_mix, w_in, conv_w, conv_b, dt_bias, a_log, d_skip, ssd_norm_w,
           lam_q1, lam_k1, lam_q2, lam_k2, subln_w, bsz, seq):
    zxd, q, k, v = _in_proj(x2, g_mix.reshape(1, -1), w_in.T)
    y_att, y_ssd = _attn_ssd(q, k, v, lam_q1, lam_k1, lam_q2, lam_k2, subln_w, zxd, conv_w, conv_b, dt_bias,
                             a_log, d_skip, ssd_norm_w, bsz, seq)
    return y_ssd, y_att


def _out_proj_kernel(x_ref, ys_ref, ya_ref, wo_ref, g_ref, wr2_ref, br_ref,
                     x1_ref, xn_ref, route_ref, gate_ref, cnt_ref):
    tm = TM_ROUTE
    x1 = x_ref[...] + _dot(jnp.concatenate([ys_ref[...], ya_ref[...]], axis=-1), wo_ref[...])
    x1_ref[...] = x1
    xn_all = _rms(x1, g_ref[...], NORM_EPS)
    xn_b = xn_all.astype(BF16)
    xn_ref[...] = xn_b

    n_tok = OUT_SUB * tm
    lg2 = _dot(xn_b, wr2_ref[...])
    logits = (lg2[:, :LANES] + lg2[:, LANES:]).T[:N_EXPERTS, :] + br_ref[...]

    eidx = lax.broadcasted_iota(jnp.int32, (N_EXPERTS, n_tok), 0).astype(F32)
    work = logits
    vals, idxs, hots = [], [], []
    for _ in range(TOP_K):
        m = jnp.max(work, axis=0, keepdims=True)
        idx = jnp.min(jnp.where(work == m, eidx, float(N_EXPERTS)), axis=0, keepdims=True)
        hot = eidx == idx
        vals.append(m)
        idxs.append(idx.astype(jnp.int32))
        hots.append(hot)
        work = jnp.where(hot, -jnp.inf, work)
    exps = [jnp.exp(v - vals[0]) for v in vals]
    denom = exps[0] + exps[1] + exps[2] + exps[3]
    gates = [e / denom for e in exps]

    cnt = jnp.zeros((N_EXPERTS, n_tok), F32)
    for hot in hots:
        cnt = cnt + jnp.where(hot, 1.0, 0.0)
    cnt_b = cnt.astype(BF16)
    r = lax.broadcasted_iota(jnp.int32, (tm, tm), 0)
    c = lax.broadcasted_iota(jnp.int32, (tm, tm), 1)
    earlier_tok = jnp.where(r < c, 1.0, 0.0).astype(BF16)
    er = lax.broadcasted_iota(jnp.int32, (N_EXPERTS, N_EXPERTS), 0)
    ec = lax.broadcasted_iota(jnp.int32, (N_EXPERTS, N_EXPERTS), 1)
    lower_exp = jnp.where(ec < er, 1.0, 0.0).astype(BF16)
    cnt_pad = jnp.concatenate([cnt_b, jnp.zeros((LANES - N_EXPERTS, n_tok), BF16)], axis=0)
    pos = []
    for sub in range(OUT_SUB):
        cols = slice(sub * tm, (sub + 1) * tm)
        run = jnp.sum(cnt[:, cols], axis=1, keepdims=True)
        run_even = 2.0 * jnp.floor(0.5 * run + 0.5)
        run_start = _dot(lower_exp, jnp.broadcast_to(run_even, (N_EXPERTS, LANES)).astype(BF16))[:, 0:1]
        pos.append(_dot(cnt_b[:, cols], earlier_tok) + run_start)
        cnt_ref[sub] = _dot_nt(jnp.ones((SUBLANES, tm), BF16), cnt_pad[:, cols])
    pos = jnp.concatenate(pos, axis=1)
    lps = [jnp.sum(jnp.where(hot, pos, 0.0), axis=0, keepdims=True).astype(jnp.int32) for hot in hots]

    route_ref[...] = jnp.concatenate(idxs + lps, axis=0)
    gate_ref[...] = jnp.concatenate(gates + [jnp.zeros((SUBLANES - TOP_K, n_tok), F32)], axis=0)


def _out_proj(x2, y_ssd, y_att, w_out, g_ffn, w_router, b_router):
    t = x2.shape[0]
    tm = OUT_SUB * TM_ROUTE
    nt = t // TM_ROUTE
    wo = w_out.astype(BF16)
    wrh, wrm, _ = _split3(jnp.pad(w_router, ((0, 0), (0, LANES - N_EXPERTS))))
    wr2 = jnp.concatenate([wrh, wrm], axis=1)
    br = b_router.reshape(N_EXPERTS, 1)
    row = lambda n: pl.BlockSpec((tm, n), lambda i: (i, 0))
    col = pl.BlockSpec((SUBLANES, tm), lambda i: (0, i))
    full = lambda a: pl.BlockSpec(a.shape, lambda i: (0, 0))
    args = (x2, y_ssd, y_att, wo, g_ffn.reshape(1, -1), wr2, br)
    return pl.pallas_call(
        _out_proj_kernel,
        grid=(t // tm,),
        in_specs=[row(D_MODEL), row(SSD_WIDTH), row(ATT_WIDTH)] + [full(a) for a in args[3:]],
        out_specs=[row(D_MODEL), row(D_MODEL), col, col,
                   pl.BlockSpec((OUT_SUB, SUBLANES, LANES), lambda i: (i, 0, 0))],
        out_shape=[
            jax.ShapeDtypeStruct((t, D_MODEL), F32),
            jax.ShapeDtypeStruct((t, D_MODEL), BF16),
            jax.ShapeDtypeStruct((SUBLANES, t), jnp.int32),
            jax.ShapeDtypeStruct((SUBLANES, t), F32),
            jax.ShapeDtypeStruct((nt, SUBLANES, LANES), F32),
        ],
        compiler_params=_cparams(("arbitrary",)),
        name="out_proj",
    )(*args)


def _copy_run(n, start_copy):
    @pl.when(n > 0)
    def _():
        start_copy(n)


def _select_by_position(positions, values, n_pos):
    tm = positions[0].shape[1]
    r = lax.broadcasted_iota(jnp.int32, (n_pos, tm), 0)
    out = jnp.zeros((n_pos, tm), F32)
    for k in reversed(range(TOP_K)):
        out = jnp.where(r == positions[k], values[k], out)
    return out


def _dispatch_kernel(cnt_ref, ls_ref, base_ref, zoff_ref, rows_ref, xn_ref, route_ref, xs_hbm, xloc, zeros_vmem,
                     sem_z, sems):
    n_pos = N_POS
    i = pl.program_id(0)
    n_steps = pl.num_programs(0)
    slot = lax.rem(i, 2)

    def zero_fill(op):
        def pad_rows(e, _):
            n = zoff_ref[N_EXPERTS + 1 + e]

            @pl.when(n > 0)
            def _():
                op(pltpu.make_async_copy(zeros_vmem.at[_row_slice(0, n, PACK_SUB)],
                                         xs_hbm.at[_row_slice(zoff_ref[e], n, PACK_SUB)], sem_z))
            return 0

        def unused_block(b, _):
            op(pltpu.make_async_copy(zeros_vmem, xs_hbm.at[_row_slice(b * ROW_BLK, ROW_BLK, PACK_SUB)], sem_z))
            return 0

        lax.fori_loop(0, N_EXPERTS, pad_rows, 0)
        lax.fori_loop(zoff_ref[N_EXPERTS], xs_hbm.shape[0] // (ROW_BLK * PACK_SUB), unused_block, 0)

    @pl.when(i == 0)
    def _():
        zeros_vmem[...] = jnp.zeros_like(zeros_vmem)
        zero_fill(lambda cp: cp.start())

    def slot_wait(s, step):
        n = rows_ref[step]
        pltpu.make_async_copy(xloc.at[s, _row_slice(0, n, PACK_SUB)], xs_hbm.at[_row_slice(0, n, PACK_SUB)],
                              sems.at[s]).wait()

    @pl.when(i >= 2)
    def _():
        slot_wait(slot, i - 2)

    positions = [route_ref[TOP_K + k:TOP_K + k + 1, :] for k in range(TOP_K)]
    sel = _select_by_position(positions, [1.0] * TOP_K, n_pos).astype(BF16)
    rows = _dot(sel, xn_ref[...])
    _pack_rows(xloc.at[slot], rows)

    def per_expert(e, _):
        idx = i * N_EXPERTS + e
        src0 = ls_ref[idx]
        dst0 = base_ref[idx]

        def start_copy(n):
            pltpu.make_async_copy(xloc.at[slot, _row_slice(src0, n, PACK_SUB)],
                                  xs_hbm.at[_row_slice(dst0, n, PACK_SUB)], sems.at[slot]).start()

        _copy_run(cnt_ref[idx], start_copy)
        return 0

    lax.fori_loop(0, N_EXPERTS, per_expert, 0)

    @pl.when(i == n_steps - 1)
    def _():
        slot_wait(slot, i)

        @pl.when(n_steps > 1)
        def _():
            slot_wait(1 - slot, i - 1)

        zero_fill(lambda cp: cp.wait())


def _dispatch(xn, route, tables, n_rows):
    t = xn.shape[0]
    tm = TM_ROUTE
    cnt_tbl, ls_tbl, base_tbl, zoff, tile_rows = tables
    return pl.pallas_call(
        _dispatch_kernel,
        grid_spec=pltpu.PrefetchScalarGridSpec(
            num_scalar_prefetch=5,
            grid=(t // tm,),
            in_specs=[pl.BlockSpec((tm, D_MODEL), lambda i, *_: (i, 0)),
                      pl.BlockSpec((SUBLANES, tm), lambda i, *_: (0, i))],
            out_specs=pl.BlockSpec(memory_space=pl.ANY),
            scratch_shapes=[
                pltpu.VMEM((2, N_POS * PACK_SUB, LANES), jnp.uint32),
                pltpu.VMEM((ROW_BLK * PACK_SUB, LANES), jnp.uint32),
                pltpu.SemaphoreType.DMA,
                pltpu.SemaphoreType.DMA((2,)),
            ],
        ),
        out_shape=jax.ShapeDtypeStruct(((n_rows + ROW_BLK) * PACK_SUB, LANES), jnp.uint32),
        compiler_params=_cparams(("arbitrary",)),
        name="dispatch",
    )(cnt_tbl, ls_tbl, base_tbl, zoff, tile_rows, xn, route)


def _experts_kernel(blk_e_ref, nvalid_ref, first_ref, wslot_ref, enext_ref, pieces_ref, xs_ref, wup_hbm, wdn_hbm,
                    bgu_ref, bd_ref, ys_ref, wup_buf, wdn_buf, wg_s, wu_s, wd_s, sems):
    i = pl.program_id(0)
    slot = wslot_ref[i]

    def weight_copies(e, s):
        return (pltpu.make_async_copy(wup_hbm.at[e], wup_buf.at[s], sems.at[0, s]),
                pltpu.make_async_copy(wdn_hbm.at[e], wdn_buf.at[s], sems.at[1, s]))

    @pl.when(i == 0)
    def _():
        for cp in weight_copies(blk_e_ref[0], slot):
            cp.start()

    @pl.when(first_ref[i] != 0)
    def _():
        for cp in weight_copies(blk_e_ref[i], slot):
            cp.wait()

        @pl.when(enext_ref[i] >= 0)
        def _():
            for cp in weight_copies(enext_ref[i], 1 - slot):
                cp.start()

        src = lax.broadcasted_iota(jnp.int32, (DEINT, DEINT), 0)
        dst = lax.broadcasted_iota(jnp.int32, (DEINT, DEINT), 1)
        perm = jnp.where(src == jnp.where(dst < LANES, 2 * dst, 2 * (dst - LANES) + 1), 1.0, 0.0).astype(BF16)
        for g in range(2 * D_EXPERT // DEINT):
            sep = _dot(wup_buf[slot, :, g * DEINT:(g + 1) * DEINT].astype(BF16), perm)
            wg_s[:, g * LANES:(g + 1) * LANES] = sep[:, :LANES].astype(BF16)
            wu_s[:, g * LANES:(g + 1) * LANES] = sep[:, LANES:].astype(BF16)
        wd_s[...] = wdn_buf[slot].astype(BF16)

    def ffn(n_rows):
        used = n_rows * ROW_SUB
        if n_rows:
            xb = _unpack_rows(xs_ref.at[pl.ds(0, n_rows * PACK_SUB)])
            gate = jnp.minimum(_dot(xb, wg_s[...]) + bgu_ref[0, 0:1, :], SWIGLU_LIMIT)
            up = jnp.clip(_dot(xb, wu_s[...]) + bgu_ref[0, 1:2, :], -SWIGLU_LIMIT, SWIGLU_LIMIT)
            act = (up + 1.0) * gate * jax.nn.sigmoid(SWIGLU_ALPHA * gate)
            _store_rows(ys_ref.at[pl.ds(0, used)], _dot(act.astype(BF16), wd_s[...]) + bd_ref[0])
        if n_rows < ROW_BLK:
            ys_ref[used:, :] = jnp.zeros((ROW_BLK * ROW_SUB - used, LANES), F32)

    for pieces in range(ROW_BLK // ROW_PIECE + 1):
        pl.when(pieces_ref[i] == pieces)(lambda pieces=pieces: ffn(pieces * ROW_PIECE))


def _experts(xs, plan, w_up, b_up, w_down, b_down, n_rows):
    nb = n_rows // ROW_BLK
    b_gu = b_up.reshape(N_EXPERTS, D_EXPERT, 2).transpose(0, 2, 1)
    bd = b_down.reshape(N_EXPERTS, 1, D_MODEL)
    blk_e, nvalid, first, wslot, enext, pieces = plan
    src = lambda i, be, nv, *_: (jnp.minimum(i, nv[0] - 1), 0)
    bspec = lambda r, m: pl.BlockSpec((1, r, m), lambda i, be, *_: (be[i], 0, 0))
    anyspec = pl.BlockSpec(memory_space=pl.ANY)
    return pl.pallas_call(
        _experts_kernel,
        grid_spec=pltpu.PrefetchScalarGridSpec(
            num_scalar_prefetch=6,
            grid=(nb,),
            in_specs=[pl.BlockSpec((ROW_BLK * PACK_SUB, LANES), src), anyspec, anyspec,
                      bspec(2, D_EXPERT), bspec(1, D_MODEL)],
            out_specs=_rows_spec(ROW_BLK, lambda i, *_: (i, 0)),
            scratch_shapes=[
                pltpu.VMEM((2, D_MODEL, 2 * D_EXPERT), F32),
                pltpu.VMEM((2, D_EXPERT, D_MODEL), F32),
                pltpu.VMEM((D_MODEL, D_EXPERT), BF16),
                pltpu.VMEM((D_MODEL, D_EXPERT), BF16),
                pltpu.VMEM((D_EXPERT, D_MODEL), BF16),
                pltpu.SemaphoreType.DMA((2, 2)),
            ],
        ),
        out_shape=jax.ShapeDtypeStruct((n_rows * ROW_SUB, LANES), F32),
        compiler_params=_cparams(("arbitrary",)),
        name="experts",
    )(blk_e, nvalid, first, wslot, enext, pieces, xs, w_up, w_down, b_gu, bd)


def _combine_kernel(cnt_ref, ls_ref, base_ref, rows_ref, ys_hbm, route_ref, gate_ref, x1_ref, p_ref, gp_ref, wpg_ref,
                    wpp_ref, gf_ref, o_ref, yloc, sems):
    tm = TM_ROUTE
    n_pos = N_POS
    i = pl.program_id(0)
    n_steps = pl.num_programs(0)
    slot = lax.rem(i, 2)

    def gather_step(step, s):
        for u in range(COMB_SUB):
            def per_expert(e, _, u=u):
                idx = (step * COMB_SUB + u) * N_EXPERTS + e
                src0 = base_ref[idx]
                dst0 = ls_ref[idx]

                def start_copy(n):
                    pltpu.make_async_copy(ys_hbm.at[_row_slice(src0, n)],
                                          yloc.at[s, u, _row_slice(dst0, n)], sems.at[s]).start()

                _copy_run(cnt_ref[idx], start_copy)
                return 0

            lax.fori_loop(0, N_EXPERTS, per_expert, 0)

    @pl.when(i == 0)
    def _():
        yloc[...] = jnp.zeros_like(yloc)
        gather_step(0, 0)

    @pl.when(i + 1 < n_steps)
    def _():
        gather_step(i + 1, 1 - slot)

    for u in range(COMB_SUB):
        n = rows_ref[i * COMB_SUB + u]
        pltpu.make_async_copy(ys_hbm.at[_row_slice(0, n)], yloc.at[slot, u, _row_slice(0, n)], sems.at[slot]).wait()

    moe = []
    for u in range(COMB_SUB):
        cols = slice(u * tm, (u + 1) * tm)
        positions = [route_ref[TOP_K + k:TOP_K + k + 1, cols] for k in range(TOP_K)]
        gsel = _select_by_position(positions, [gate_ref[k:k + 1, cols] for k in range(TOP_K)], n_pos)
        g_pos = jnp.sum(gsel, axis=1, keepdims=True)
        sel = jnp.where(gsel != 0.0, 1.0, 0.0).astype(BF16)
        y_gated = (_load_rows(yloc.at[slot, u]) * g_pos).astype(BF16)
        moe.append(_dot_tn(sel, y_gated))
    x2 = x1_ref[...] + jnp.concatenate(moe, axis=0)
    xn = _rms(x2, gp_ref[...], NORM_EPS).astype(BF16)
    pp = _dot(p_ref[...].astype(BF16), wpp_ref[...])
    x3 = x2 + pp * jax.nn.sigmoid(_dot(xn, wpg_ref[...]))
    o_ref[...] = _rms(x3, gf_ref[...], NORM_EPS)


def _combine(ys, route, gate_t, x1, p2, tables, g_ple, w_ple_gate, w_ple_proj, g_final):
    t = x1.shape[0]
    tm = COMB_SUB * TM_ROUTE
    cnt_tbl, ls_tbl, base_tbl, _, tile_rows = tables
    row = lambda n: pl.BlockSpec((tm, n), lambda i, *_: (i, 0))
    col = pl.BlockSpec((SUBLANES, tm), lambda i, *_: (0, i))
    full = lambda a: pl.BlockSpec(a.shape, lambda i, *_: (0, 0))
    consts = (g_ple.reshape(1, -1), w_ple_gate.astype(BF16), w_ple_proj.astype(BF16), g_final.reshape(1, -1))
    return pl.pallas_call(
        _combine_kernel,
        grid_spec=pltpu.PrefetchScalarGridSpec(
            num_scalar_prefetch=4,
            grid=(t // tm,),
            in_specs=[pl.BlockSpec(memory_space=pl.ANY), col, col, row(D_MODEL), row(PLE_DIM)]
                     + [full(a) for a in consts],
            out_specs=row(D_MODEL),
            scratch_shapes=[
                pltpu.VMEM((2, COMB_SUB, N_POS * ROW_SUB, LANES), F32),
                pltpu.SemaphoreType.DMA((2,)),
            ],
        ),
        out_shape=jax.ShapeDtypeStruct((t, D_MODEL), F32),
        compiler_params=_cparams(("arbitrary",)),
        name="combine",
    )(cnt_tbl, ls_tbl, base_tbl, tile_rows, ys, route, gate_t, x1, p2, *consts)


def _routing_tables(cnt, n_blocks):
    tile_cnt = cnt[:, 0, :N_EXPERTS].astype(jnp.int32)
    tile_cnt = tile_cnt + tile_cnt % 2
    counts = jnp.sum(tile_cnt, axis=0)
    padded = (counts + ROW_BLK - 1) // ROW_BLK * ROW_BLK
    pend = jnp.cumsum(padded)
    pstart = pend - padded
    base = pstart[None, :] + jnp.cumsum(tile_cnt, axis=0) - tile_cnt
    lstart = jnp.cumsum(tile_cnt, axis=1) - tile_cnt
    nvalid = pend[-1:] // ROW_BLK
    zoff = jnp.concatenate([pstart + counts, nvalid, padded - counts])
    blk_start = jnp.minimum(jnp.arange(n_blocks, dtype=jnp.int32) * ROW_BLK, pend[-1] - 1)
    blk_e = jnp.minimum(jnp.sum((pend[None, :] <= blk_start[:, None]).astype(jnp.int32), axis=1), N_EXPERTS - 1)
    first = jnp.concatenate([jnp.ones((1,), bool), blk_e[1:] != blk_e[:-1]])
    wslot = (jnp.cumsum(first.astype(jnp.int32)) - 1) % 2
    eids = jnp.arange(N_EXPERTS, dtype=jnp.int32)
    later_nonempty = (eids[None, :] > eids[:, None]) & (padded[None, :] > 0)
    next_e = jnp.min(jnp.where(later_nonempty, eids[None, :], N_EXPERTS), axis=1)
    next_e = jnp.where(next_e == N_EXPERTS, -1, next_e)
    blk_hot = blk_e[:, None] == eids[None, :]
    per_block = lambda v: jnp.sum(jnp.where(blk_hot, v[None, :], 0), axis=1)
    blk_ids = jnp.arange(n_blocks, dtype=jnp.int32)
    real_rows = jnp.clip(per_block(pstart + counts) - blk_ids * ROW_BLK, 0, ROW_BLK) * (blk_ids < nvalid[0])
    pieces = (real_rows + ROW_PIECE - 1) // ROW_PIECE
    tile_rows = jnp.sum(tile_cnt, axis=1)
    i32 = lambda a: a.reshape(-1).astype(jnp.int32)
    plan = (i32(blk_e), i32(nvalid), i32(first), i32(wslot), i32(per_block(next_e)), i32(pieces))
    return (i32(tile_cnt), i32(lstart), i32(base), i32(zoff), i32(tile_rows)), plan


def kernel(x, p, g_mix, w_in, conv_w, conv_b, dt_bias, a_log, d_skip, ssd_norm_w, lam_q1, lam_k1, lam_q2, lam_k2, subln_w, w_out, g_ffn, w_router, b_router, w_up, b_up, w_down, b_down, g_ple, w_ple_gate, w_ple_proj, g_final):
    bsz, seq, d = x.shape
    t = bsz * seq
    x2 = x.reshape(t, d)
    y_ssd, y_att = _mixer(x2, g_mix[0], w_in[0], conv_w[0], conv_b[0], dt_bias[0], a_log[0], d_skip[0], ssd_norm_w[0],
                          lam_q1[0], lam_k1[0], lam_q2[0], lam_k2[0], subln_w[0], bsz, seq)
    x1, xn, route, gate_t, cnt = _out_proj(x2, y_ssd, y_att, w_out[0], g_ffn[0], w_router[0], b_router[0])

    n_rows = t * TOP_K + (t // TM_ROUTE) * N_EXPERTS + N_EXPERTS * ROW_BLK
    n_rows = -(-n_rows // ROW_BLK) * ROW_BLK
    tables, plan = _routing_tables(cnt, n_rows // ROW_BLK)
    xs = _dispatch(xn, route, tables, n_rows)
    ys = _experts(xs, plan, w_up[0], b_up[0], w_down[0], b_down[0], n_rows)
    out = _combine(ys, route, gate_t, x1, p[0].reshape(t, PLE_DIM), tables, g_ple[0], w_ple_gate[0], w_ple_proj[0],
                   g_final)
    return out.reshape(bsz, seq, d)
```

```python
import math

import jax
import jax.numpy as jnp
from jax import lax
from jax.experimental import pallas as pl
from jax.experimental.pallas import tpu as pltpu

F32 = jnp.float32
BF16 = jnp.bfloat16

D_MODEL = 1024
PLE_DIM = 256
SSD_WIDTH = 512
ATT_WIDTH = 512
SSD_HEAD_DIM = 64
SSD_HEADS = 8
SSD_GROUPS = 2
SSD_STATE = 128
SSD_CONV = 4
SSD_CHUNK = 128
SSD_CONV_CH = SSD_WIDTH + 2 * SSD_GROUPS * SSD_STATE
SSD_NORM_EPS = 1e-5
ATT_HEAD_DIM = 64
ATT_HEADS = 4
SUBLN_EPS = 1e-5
OFF_Z = 0
OFF_XBC = OFF_Z + SSD_WIDTH
OFF_DT = OFF_XBC + SSD_CONV_CH
OFF_Q = OFF_DT + SSD_HEADS
OFF_K = OFF_Q + ATT_WIDTH
OFF_V = OFF_K + ATT_WIDTH
IN_PROJ = OFF_V + ATT_WIDTH
N_EXPERTS = 32
TOP_K = 4
D_EXPERT = 1024
SWIGLU_LIMIT = 7.0
SWIGLU_ALPHA = 1.702
NORM_EPS = 1e-6
LAM_INIT = 0.8 - 0.6 * math.exp(-0.3 * 0)

LANES = 128
SUBLANES = 8
VMEM_LIMIT_BYTES = 56 * 1024 * 1024
ROW_SUB = D_MODEL // LANES
PACK_SUB = ROW_SUB // 2

TM_PROJ = 512
ATT_BQ = 512
ATT_BK = 512
ROW_BLK = 512
ROW_PIECE = 128
TM_ROUTE = 256
N_POS = TOP_K * TM_ROUTE + N_EXPERTS
OUT_SUB = 4
COMB_SUB = 2
RUN_UNROLL = 4
DEINT = 2 * LANES
ZXD_WIDTH = OFF_DT + LANES
CP_CONV_B = SSD_CONV
CP_WIDE = SSD_CONV + 1
CP_SMALL = SSD_CONV + 2
CP_DT_BIAS = 4 * LANES
CP_A_LOG = 5 * LANES


def _cparams(sem):
    return pltpu.CompilerParams(dimension_semantics=sem, vmem_limit_bytes=VMEM_LIMIT_BYTES)


def _rms(x, w, eps):
    return x * lax.rsqrt(jnp.mean(x * x, axis=-1, keepdims=True) + eps) * w


def _dot(a, b):
    return jnp.dot(a, b, preferred_element_type=F32)


def _dot_nt(a, b):
    return lax.dot_general(a, b, (((1,), (1,)), ((), ())), preferred_element_type=F32)


def _dot_tn(a, b):
    return lax.dot_general(a, b, (((0,), (0,)), ((), ())), preferred_element_type=F32)


def _store_rows(ref, val):
    n = val.shape[0]
    for s in range(ROW_SUB):
        ref[pl.ds(s, n, stride=ROW_SUB), :] = val[:, s * LANES:(s + 1) * LANES]


def _load_rows(ref):
    n = ref.shape[0] // ROW_SUB
    return jnp.concatenate([ref[pl.ds(s, n, stride=ROW_SUB), :] for s in range(ROW_SUB)], axis=-1)


def _rows_spec(n, index_map):
    return pl.BlockSpec((n * ROW_SUB, LANES), index_map)


def _row_slice(start, n, sub=ROW_SUB):
    return pl.ds(pl.multiple_of(start * sub, SUBLANES), n * sub)


def _pack_rows(ref, val):
    n = val.shape[0]
    bits = lambda v: lax.bitcast_convert_type(v, jnp.uint32)
    words = (bits(val[:, :D_MODEL // 2]) & jnp.uint32(0xFFFF0000)) | (bits(val[:, D_MODEL // 2:]) >> 16)
    for s in range(PACK_SUB):
        ref[pl.ds(s, n, stride=PACK_SUB), :] = words[:, s * LANES:(s + 1) * LANES]


def _unpack_rows(ref):
    n = ref.shape[0] // PACK_SUB
    words = [ref[pl.ds(s, n, stride=PACK_SUB), :] for s in range(PACK_SUB)]
    as_f32 = lambda w: lax.bitcast_convert_type(w, F32)
    hi = [as_f32(w & jnp.uint32(0xFFFF0000)) for w in words]
    lo = [as_f32(w << 16) for w in words]
    return jnp.concatenate(hi + lo, axis=-1).astype(BF16)


def _split3(x):
    hi = x.astype(BF16)
    r1 = x - hi.astype(F32)
    mid = r1.astype(BF16)
    lo = (r1 - mid.astype(F32)).astype(BF16)
    return hi, mid, lo


def _in_proj_kernel(x_ref, g_ref, win_ref, zxd_ref, q_ref, k_ref, v_ref, w_ref):
    @pl.when(pl.program_id(0) == 0)
    def _():
        w_ref[:OFF_DT, :] = win_ref[:OFF_DT, :].astype(BF16)
        dt_rows = jnp.concatenate([win_ref[OFF_DT:OFF_Q, :], jnp.zeros((LANES - SSD_HEADS, D_MODEL), F32)], axis=0)
        w_ref[OFF_DT:OFF_DT + LANES, :] = dt_rows.astype(BF16)
        w_ref[OFF_DT + LANES:, :] = win_ref[OFF_Q:, :].astype(BF16)

    h = _rms(x_ref[...], g_ref[...], NORM_EPS).astype(BF16)
    q0 = ZXD_WIDTH
    k0, v0 = q0 + ATT_WIDTH, q0 + 2 * ATT_WIDTH
    zxd_ref[...] = _dot_nt(h, w_ref[:q0, :])
    q_ref[...] = (_dot_nt(h, w_ref[q0:k0, :]) * (ATT_HEAD_DIM ** -0.5 * math.log2(math.e))).astype(BF16)
    k_ref[...] = _dot_nt(h, w_ref[k0:v0, :]).astype(BF16)
    v_ref[...] = _dot_nt(h, w_ref[v0:v0 + ATT_WIDTH, :]).astype(BF16)


def _in_proj(x2, g_mix, w_in_t):
    t = x2.shape[0]
    tm = TM_PROJ
    row = lambda n: pl.BlockSpec((tm, n), lambda i: (i, 0))
    full = lambda a: pl.BlockSpec(a.shape, lambda i: (0, 0))
    return pl.pallas_call(
        _in_proj_kernel,
        grid=(t // tm,),
        in_specs=[row(D_MODEL), full(g_mix), full(w_in_t)],
        out_specs=[row(ZXD_WIDTH), row(ATT_WIDTH), row(ATT_WIDTH), row(ATT_WIDTH)],
        out_shape=[
            jax.ShapeDtypeStruct((t, ZXD_WIDTH), F32),
            jax.ShapeDtypeStruct((t, ATT_WIDTH), BF16),
            jax.ShapeDtypeStruct((t, ATT_WIDTH), BF16),
            jax.ShapeDtypeStruct((t, ATT_WIDTH), BF16),
        ],
        scratch_shapes=[pltpu.VMEM((IN_PROJ + LANES - SSD_HEADS, D_MODEL), BF16)],
        compiler_params=_cparams(("arbitrary",)),
        name="in_proj",
    )(x2, g_mix, w_in_t)


def _ssd_chunk_init(c, xpad_ref, state_ref):
    L = SSD_CHUNK

    @pl.when(c == 0)
    def _():
        xpad_ref[0:SUBLANES, :] = jnp.zeros((SUBLANES, SSD_CONV_CH), F32)
        state_ref[...] = jnp.zeros_like(state_ref)

    @pl.when(c != 0)
    def _():
        xpad_ref[0:SUBLANES, :] = xpad_ref[L:L + SUBLANES, :]


def _ssd_chunk(zxd_ref, cp_ref, y_ref, xpad_ref, state_ref):
    L = SSD_CHUNK
    xpad_ref[SUBLANES:SUBLANES + L, :] = zxd_ref[:, OFF_XBC:OFF_DT]

    conv = cp_ref[CP_CONV_B:CP_CONV_B + 1, :]
    for j in range(SSD_CONV):
        off = SUBLANES - (SSD_CONV - 1) + j
        conv = conv + cp_ref[j:j + 1, :] * xpad_ref[off:off + L, :]
    act = conv * jax.nn.sigmoid(conv)
    xs = act[:, :SSD_WIDTH]
    bm = act[:, SSD_WIDTH:SSD_WIDTH + SSD_GROUPS * SSD_STATE].astype(BF16)
    cm = act[:, SSD_WIDTH + SSD_GROUPS * SSD_STATE:].astype(BF16)

    dt_in = zxd_ref[:, OFF_DT:] + cp_ref[CP_SMALL:CP_SMALL + 1, CP_DT_BIAS:CP_DT_BIAS + LANES]
    dt_all = jnp.maximum(dt_in, 0.0) + jnp.log1p(jnp.exp(-jnp.abs(dt_in)))
    adt = dt_all * (-jnp.exp(cp_ref[CP_SMALL:CP_SMALL + 1, CP_A_LOG:CP_A_LOG + LANES]))

    ri = lax.broadcasted_iota(jnp.int32, (L, L), 0)
    ci = lax.broadcasted_iota(jnp.int32, (L, L), 1)
    causal = ci <= ri
    tril = jnp.where(causal, 1.0, 0.0).astype(BF16)
    hi, mid, lo = _split3(adt)
    acum_all = _dot(tril, hi) + _dot(tril, mid) + _dot(tril, lo)
    acum_t = acum_all.T
    a_last = acum_all[L - 1:L, :]
    decay_in_all = jnp.exp(a_last - acum_all)
    decay_out_all = jnp.exp(acum_all)
    chunk_decay_all = jnp.exp(a_last)
    acum = lambda h: acum_all[:, h:h + 1]
    dt = lambda h: dt_all[:, h:h + 1]
    decay_out = lambda h: decay_out_all[:, h:h + 1]
    decay_in = lambda h: decay_in_all[:, h:h + 1]
    chunk_decay = lambda h: chunk_decay_all[:, h:h + 1]

    lane = lax.broadcasted_iota(jnp.int32, (L, LANES), 1)
    lo_half = lane < SSD_HEAD_DIM

    def per_pair(col_a, col_b):
        return jnp.where(lo_half, col_a, col_b)

    ys = []
    for pair in range(SSD_HEADS // 2):
        g = pair // 2
        h0, h1 = 2 * pair, 2 * pair + 1
        cg = cm[:, g * SSD_STATE:(g + 1) * SSD_STATE]
        bg = bm[:, g * SSD_STATE:(g + 1) * SSD_STATE]
        cb = _dot_nt(cg, bg)
        x_pair = xs[:, pair * LANES:(pair + 1) * LANES]
        xdt = x_pair * per_pair(dt(h0), dt(h1))
        y_pair = jnp.zeros((L, LANES), F32)
        for hh, keep in ((h0, lo_half), (h1, jnp.logical_not(lo_half))):
            seg = acum(hh) - acum_t[hh:hh + 1, :]
            lmat = jnp.where(causal, jnp.exp(jnp.where(causal, seg, 0.0)), 0.0)
            m = (cb * lmat).astype(BF16)
            y_pair = y_pair + _dot(m, jnp.where(keep, xdt, 0.0).astype(BF16))
        s_prev = state_ref[pair]
        y_off = _dot(cg, s_prev.astype(BF16)) * per_pair(decay_out(h0), decay_out(h1))
        w_in = (xdt * per_pair(decay_in(h0), decay_in(h1))).astype(BF16)
        cd = jnp.where(lane[0:1, :] < SSD_HEAD_DIM, chunk_decay(h0), chunk_decay(h1))
        state_ref[pair] = s_prev * cd + _dot_tn(bg, w_in)
        ys.append(y_pair + y_off + cp_ref[CP_WIDE:CP_WIDE + 1, pair * LANES:(pair + 1) * LANES] * x_pair)

    y = jnp.concatenate(ys, axis=-1)
    zz = zxd_ref[:, OFF_Z:OFF_XBC]
    y = y * (zz * jax.nn.sigmoid(zz))
    gw = SSD_WIDTH // SSD_GROUPS
    outs = []
    for g in range(SSD_GROUPS):
        yg = y[:, g * gw:(g + 1) * gw]
        outs.append(yg * lax.rsqrt(jnp.mean(yg * yg, axis=-1, keepdims=True) + SSD_NORM_EPS))
    y_ref[...] = (jnp.concatenate(outs, axis=-1) * cp_ref[CP_WIDE:CP_WIDE + 1, SSD_WIDTH:]).astype(BF16)


def _attn_ssd_kernel(q_ref, k_ref, v_ref, sw_ref, zxd_ref, cp_ref, o_ref, y_ref, s_scr, vt_scr, xpad_ref, state_ref):
    bq, bk = ATT_BQ, ATT_BK
    qi = pl.program_id(2)
    n_maps = 2
    _ssd_chunk_init(pl.program_id(1) * pl.num_programs(2) + qi, xpad_ref, state_ref)

    @pl.when(qi == 0)
    def _():
        vt_scr[...] = v_ref[...].astype(F32).T.astype(BF16)

    def fold(t, reduce):
        return reduce(t.reshape(bk // SUBLANES, SUBLANES, bq), axis=0)

    def merge(old, new, op):
        return new if old is None else op(old, new)

    def attend(nk):
        _ssd_chunk(zxd_ref, cp_ref, y_ref, xpad_ref, state_ref)
        q = q_ref[...]
        lane = lax.broadcasted_iota(jnp.int32, (bq, LANES), 1)
        zero = jnp.zeros_like(q)
        q_maps = (jnp.where(lane < ATT_HEAD_DIM, q, zero), jnp.where(lane >= ATT_HEAD_DIM, q, zero))
        key = lax.broadcasted_iota(jnp.int32, (bk, bq), 0)
        qry = lax.broadcasted_iota(jnp.int32, (bk, bq), 1)
        causal = key <= qry

        mt = [None] * n_maps
        for j in range(nk):
            kb = k_ref[j * bk:(j + 1) * bk, :]
            for m in range(n_maps):
                s = _dot_nt(kb, q_maps[m])
                if j == nk - 1:
                    s = jnp.where(causal, s, -jnp.inf)
                s_scr[m, j] = s
                mt[m] = merge(mt[m], fold(s, jnp.max), jnp.maximum)
        q_max = [jnp.max(t, axis=0, keepdims=True) for t in mt]

        lt = [None] * n_maps
        acc = [None] * n_maps
        for j in range(nk):
            vt = vt_scr[:, j * bk:(j + 1) * bk]
            for m in range(n_maps):
                p = jnp.exp2(s_scr[m, j] - q_max[m])
                lt[m] = merge(lt[m], fold(p, jnp.sum), jnp.add)
                acc[m] = merge(acc[m], _dot(vt, p.astype(BF16)), jnp.add)

        lams = [cp_ref[CP_SMALL:CP_SMALL + 1, n * LANES:(n + 1) * LANES] for n in range(4)]
        lam = (jnp.exp(jnp.sum(lams[0] * lams[1], axis=-1, keepdims=True))
               - jnp.exp(jnp.sum(lams[2] * lams[3], axis=-1, keepdims=True)) + LAM_INIT)
        l1 = jnp.sum(lt[0], axis=0, keepdims=True)
        l2 = jnp.sum(lt[1], axis=0, keepdims=True)
        o = acc[0] / l1 - lam * (acc[1] / l2)
        o = o * lax.rsqrt(jnp.mean(o * o, axis=0, keepdims=True) + SUBLN_EPS) * sw_ref[...]
        o_ref[...] = (o * (1.0 - LAM_INIT)).T.astype(BF16)

    for nk in range(1, k_ref.shape[0] // bk + 1):
        pl.when(qi == nk - 1)(lambda nk=nk: attend(nk))


def _attn_ssd(q, k, v, lam_q1, lam_k1, lam_q2, lam_k2, subln_w, zxd, conv_w, conv_b, dt_bias, a_log, d_skip,
              norm_w, bsz, seq):
    nq = seq // ATT_BQ
    nc = seq // SSD_CHUNK
    assert nc == ATT_HEADS * nq
    qspec = pl.BlockSpec((ATT_BQ, LANES), lambda b, h, i: (b * nq + i, h))
    kvspec = pl.BlockSpec((seq, LANES), lambda b, h, i: (b, h))
    full = lambda a: pl.BlockSpec(a.shape, lambda b, h, i: (0, 0))
    chunk = lambda n: pl.BlockSpec((SSD_CHUNK, n), lambda b, h, i: (b * nc + h * nq + i, 0))
    sw = subln_w.reshape(-1, 1)
    pad_to = lambda v, n: jnp.pad(v.reshape(1, -1), ((0, 0), (0, n - v.size)))
    wide = jnp.concatenate([jnp.repeat(d_skip, SSD_HEAD_DIM), norm_w]).reshape(1, SSD_CONV_CH)
    small = jnp.concatenate([pad_to(a, LANES) for a in (lam_q1, lam_k1, lam_q2, lam_k2, dt_bias, a_log)], axis=1)
    consts = jnp.concatenate([conv_w, conv_b.reshape(1, -1), wide, pad_to(small, SSD_CONV_CH),
                              jnp.zeros((SUBLANES - CP_SMALL - 1, SSD_CONV_CH), F32)], axis=0)
    return pl.pallas_call(
        _attn_ssd_kernel,
        grid=(bsz, ATT_HEADS, nq),
        in_specs=[qspec, kvspec, kvspec, full(sw), chunk(ZXD_WIDTH), full(consts)],
        out_specs=[qspec, chunk(SSD_WIDTH)],
        out_shape=[jax.ShapeDtypeStruct((bsz * seq, ATT_WIDTH), BF16),
                   jax.ShapeDtypeStruct((bsz * seq, SSD_WIDTH), BF16)],
        scratch_shapes=[pltpu.VMEM((2, seq // ATT_BK, ATT_BK, ATT_BQ), F32),
                        pltpu.VMEM((LANES, seq), BF16),
                        pltpu.VMEM((SSD_CHUNK + 2 * SUBLANES, SSD_CONV_CH), F32),
                        pltpu.VMEM((SSD_HEADS // 2, SSD_STATE, LANES), F32)],
        compiler_params=_cparams(("arbitrary", "arbitrary", "arbitrary")),
        name="attn_ssd",
    )(q, k, v, sw, zxd, consts)


def _mixer(x2, g_mix, w_in, conv_w, conv_b, dt_bias, a_log, d_skip, ssd_norm_w,
           lam_q1, lam_k1, lam_q2, lam_k2, subln_w, bsz, seq):
    zxd, q, k, v = _in_proj(x2, g_mix.reshape(1, -1), w_in.T)
    y_att, y_ssd = _attn_ssd(q, k, v, lam_q1, lam_k1, lam_q2, lam_k2, subln_w, zxd, conv_w, conv_b, dt_bias,
                             a_log, d_skip, ssd_norm_w, bsz, seq)
    return y_ssd, y_att


def _out_proj_kernel(x_ref, ys_ref, ya_ref, wo_ref, g_ref, wr2_ref, br_ref,
                     x1_ref, xn_ref, route_ref, gate_ref, cnt_ref):
    tm = TM_ROUTE
    x1 = x_ref[...] + _dot(jnp.concatenate([ys_ref[...], ya_ref[...]], axis=-1), wo_ref[...])
    x1_ref[...] = x1
    xn_all = _rms(x1, g_ref[...], NORM_EPS)
    xn_b = xn_all.astype(BF16)
    xn_ref[...] = xn_b

    n_tok = OUT_SUB * tm
    lg2 = _dot(xn_b, wr2_ref[...])
    logits = (lg2[:, :LANES] + lg2[:, LANES:]).T[:N_EXPERTS, :] + br_ref[...]

    eidx = lax.broadcasted_iota(jnp.int32, (N_EXPERTS, n_tok), 0).astype(F32)
    work = logits
    vals, idxs, hots = [], [], []
    for _ in range(TOP_K):
        m = jnp.max(work, axis=0, keepdims=True)
        idx = jnp.min(jnp.where(work == m, eidx, float(N_EXPERTS)), axis=0, keepdims=True)
        hot = eidx == idx
        vals.append(m)
        idxs.append(idx.astype(jnp.int32))
        hots.append(hot)
        work = jnp.where(hot, -jnp.inf, work)
    exps = [jnp.exp(v - vals[0]) for v in vals]
    denom = exps[0] + exps[1] + exps[2] + exps[3]
    gates = [e / denom for e in exps]

    cnt = jnp.zeros((N_EXPERTS, n_tok), F32)
    for hot in hots:
        cnt = cnt + jnp.where(hot, 1.0, 0.0)
    cnt_b = cnt.astype(BF16)
    r = lax.broadcasted_iota(jnp.int32, (tm, tm), 0)
    c = lax.broadcasted_iota(jnp.int32, (tm, tm), 1)
    earlier_tok = jnp.where(r < c, 1.0, 0.0).astype(BF16)
    er = lax.broadcasted_iota(jnp.int32, (N_EXPERTS, N_EXPERTS), 0)
    ec = lax.broadcasted_iota(jnp.int32, (N_EXPERTS, N_EXPERTS), 1)
    lower_exp = jnp.where(ec < er, 1.0, 0.0).astype(BF16)
    cnt_pad = jnp.concatenate([cnt_b, jnp.zeros((LANES - N_EXPERTS, n_tok), BF16)], axis=0)
    pos = []
    for sub in range(OUT_SUB):
        cols = slice(sub * tm, (sub + 1) * tm)
        run = jnp.sum(cnt[:, cols], axis=1, keepdims=True)
        run_even = 2.0 * jnp.floor(0.5 * run + 0.5)
        run_start = _dot(lower_exp, jnp.broadcast_to(run_even, (N_EXPERTS, LANES)).astype(BF16))[:, 0:1]
        pos.append(_dot(cnt_b[:, cols], earlier_tok) + run_start)
        cnt_ref[sub] = _dot_nt(jnp.ones((SUBLANES, tm), BF16), cnt_pad[:, cols])
    pos = jnp.concatenate(pos, axis=1)
    lps = [jnp.sum(jnp.where(hot, pos, 0.0), axis=0, keepdims=True).astype(jnp.int32) for hot in hots]

    route_ref[...] = jnp.concatenate(idxs + lps, axis=0)
    gate_ref[...] = jnp.concatenate(gates + [jnp.zeros((SUBLANES - TOP_K, n_tok), F32)], axis=0)


def _out_proj(x2, y_ssd, y_att, w_out, g_ffn, w_router, b_router):
    t = x2.shape[0]
    tm = OUT_SUB * TM_ROUTE
    nt = t // TM_ROUTE
    wo = w_out.astype(BF16)
    wrh, wrm, _ = _split3(jnp.pad(w_router, ((0, 0), (0, LANES - N_EXPERTS))))
    wr2 = jnp.concatenate([wrh, wrm], axis=1)
    br = b_router.reshape(N_EXPERTS, 1)
    row = lambda n: pl.BlockSpec((tm, n), lambda i: (i, 0))
    col = pl.BlockSpec((SUBLANES, tm), lambda i: (0, i))
    full = lambda a: pl.BlockSpec(a.shape, lambda i: (0, 0))
    args = (x2, y_ssd, y_att, wo, g_ffn.reshape(1, -1), wr2, br)
    return pl.pallas_call(
        _out_proj_kernel,
        grid=(t // tm,),
        in_specs=[row(D_MODEL), row(SSD_WIDTH), row(ATT_WIDTH)] + [full(a) for a in args[3:]],
        out_specs=[row(D_MODEL), row(D_MODEL), col, col,
                   pl.BlockSpec((OUT_SUB, SUBLANES, LANES), lambda i: (i, 0, 0))],
        out_shape=[
            jax.ShapeDtypeStruct((t, D_MODEL), F32),
            jax.ShapeDtypeStruct((t, D_MODEL), BF16),
            jax.ShapeDtypeStruct((SUBLANES, t), jnp.int32),
            jax.ShapeDtypeStruct((SUBLANES, t), F32),
            jax.ShapeDtypeStruct((nt, SUBLANES, LANES), F32),
        ],
        compiler_params=_cparams(("arbitrary",)),
        name="out_proj",
    )(*args)


def _copy_run(n, start_copy):
    @pl.when(n > 0)
    def _():
        start_copy(n)


def _select_by_position(positions, values, n_pos):
    tm = positions[0].shape[1]
    r = lax.broadcasted_iota(jnp.int32, (n_pos, tm), 0)
    out = jnp.zeros((n_pos, tm), F32)
    for k in reversed(range(TOP_K)):
        out = jnp.where(r == positions[k], values[k], out)
    return out


def _dispatch_kernel(cnt_ref, ls_ref, base_ref, zoff_ref, rows_ref, xn_ref, route_ref, xs_hbm, xloc, zeros_vmem,
                     sem_z, sems):
    n_pos = N_POS
    i = pl.program_id(0)
    n_steps = pl.num_programs(0)
    slot = lax.rem(i, 2)

    def zero_fill(op):
        def pad_rows(e, _):
            n = zoff_ref[N_EXPERTS + 1 + e]

            @pl.when(n > 0)
            def _():
                op(pltpu.make_async_copy(zeros_vmem.at[_row_slice(0, n, PACK_SUB)],
                                         xs_hbm.at[_row_slice(zoff_ref[e], n, PACK_SUB)], sem_z))
            return 0

        def unused_block(b, _):
            op(pltpu.make_async_copy(zeros_vmem, xs_hbm.at[_row_slice(b * ROW_BLK, ROW_BLK, PACK_SUB)], sem_z))
            return 0

        lax.fori_loop(0, N_EXPERTS, pad_rows, 0)
        lax.fori_loop(zoff_ref[N_EXPERTS], xs_hbm.shape[0] // (ROW_BLK * PACK_SUB), unused_block, 0)

    @pl.when(i == 0)
    def _():
        zeros_vmem[...] = jnp.zeros_like(zeros_vmem)
        zero_fill(lambda cp: cp.start())

    def slot_wait(s, step):
        n = rows_ref[step]
        pltpu.make_async_copy(xloc.at[s, _row_slice(0, n, PACK_SUB)], xs_hbm.at[_row_slice(0, n, PACK_SUB)],
                              sems.at[s]).wait()

    @pl.when(i >= 2)
    def _():
        slot_wait(slot, i - 2)

    positions = [route_ref[TOP_K + k:TOP_K + k + 1, :] for k in range(TOP_K)]
    sel = _select_by_position(positions, [1.0] * TOP_K, n_pos).astype(BF16)
    rows = _dot(sel, xn_ref[...])
    _pack_rows(xloc.at[slot], rows)

    def per_expert(e, _):
        idx = i * N_EXPERTS + e
        src0 = ls_ref[idx]
        dst0 = base_ref[idx]

        def start_copy(n):
            pltpu.make_async_copy(xloc.at[slot, _row_slice(src0, n, PACK_SUB)],
                                  xs_hbm.at[_row_slice(dst0, n, PACK_SUB)], sems.at[slot]).start()

        _copy_run(cnt_ref[idx], start_copy)
        return 0

    lax.fori_loop(0, N_EXPERTS, per_expert, 0, unroll=RUN_UNROLL)

    @pl.when(i == n_steps - 1)
    def _():
        slot_wait(slot, i)

        @pl.when(n_steps > 1)
        def _():
            slot_wait(1 - slot, i - 1)

        zero_fill(lambda cp: cp.wait())


def _dispatch(xn, route, tables, n_rows):
    t = xn.shape[0]
    tm = TM_ROUTE
    cnt_tbl, ls_tbl, base_tbl, zoff, tile_rows = tables
    return pl.pallas_call(
        _dispatch_kernel,
        grid_spec=pltpu.PrefetchScalarGridSpec(
            num_scalar_prefetch=5,
            grid=(t // tm,),
            in_specs=[pl.BlockSpec((tm, D_MODEL), lambda i, *_: (i, 0)),
                      pl.BlockSpec((SUBLANES, tm), lambda i, *_: (0, i))],
            out_specs=pl.BlockSpec(memory_space=pl.ANY),
            scratch_shapes=[
                pltpu.VMEM((2, N_POS * PACK_SUB, LANES), jnp.uint32),
                pltpu.VMEM((ROW_BLK * PACK_SUB, LANES), jnp.uint32),
                pltpu.SemaphoreType.DMA,
                pltpu.SemaphoreType.DMA((2,)),
            ],
        ),
        out_shape=jax.ShapeDtypeStruct(((n_rows + ROW_BLK) * PACK_SUB, LANES), jnp.uint32),
        compiler_params=_cparams(("arbitrary",)),
        name="dispatch",
    )(cnt_tbl, ls_tbl, base_tbl, zoff, tile_rows, xn, route)


def _experts_kernel(blk_e_ref, nvalid_ref, first_ref, wslot_ref, enext_ref, pieces_ref, xs_ref, wup_hbm, wdn_hbm,
                    bgu_ref, bd_ref, ys_ref, wup_buf, wdn_buf, wg_s, wu_s, wd_s, sems):
    i = pl.program_id(0)
    slot = wslot_ref[i]

    def weight_copies(e, s):
        return (pltpu.make_async_copy(wup_hbm.at[e], wup_buf.at[s], sems.at[0, s]),
                pltpu.make_async_copy(wdn_hbm.at[e], wdn_buf.at[s], sems.at[1, s]))

    @pl.when(i == 0)
    def _():
        for cp in weight_copies(blk_e_ref[0], slot):
            cp.start()

    @pl.when(first_ref[i] != 0)
    def _():
        for cp in weight_copies(blk_e_ref[i], slot):
            cp.wait()

        @pl.when(enext_ref[i] >= 0)
        def _():
            for cp in weight_copies(enext_ref[i], 1 - slot):
                cp.start()

        src = lax.broadcasted_iota(jnp.int32, (DEINT, DEINT), 0)
        dst = lax.broadcasted_iota(jnp.int32, (DEINT, DEINT), 1)
        perm = jnp.where(src == jnp.where(dst < LANES, 2 * dst, 2 * (dst - LANES) + 1), 1.0, 0.0).astype(BF16)
        for g in range(2 * D_EXPERT // DEINT):
            sep = _dot(wup_buf[slot, :, g * DEINT:(g + 1) * DEINT].astype(BF16), perm)
            wg_s[:, g * LANES:(g + 1) * LANES] = sep[:, :LANES].astype(BF16)
            wu_s[:, g * LANES:(g + 1) * LANES] = sep[:, LANES:].astype(BF16)
        wd_s[...] = wdn_buf[slot].astype(BF16)

    def ffn(n_rows):
        used = n_rows * ROW_SUB
        if n_rows:
            xb = _unpack_rows(xs_ref.at[pl.ds(0, n_rows * PACK_SUB)])
            gate = jnp.minimum(_dot(xb, wg_s[...]) + bgu_ref[0, 0:1, :], SWIGLU_LIMIT)
            up = jnp.clip(_dot(xb, wu_s[...]) + bgu_ref[0, 1:2, :], -SWIGLU_LIMIT, SWIGLU_LIMIT)
            act = (up + 1.0) * gate * jax.nn.sigmoid(SWIGLU_ALPHA * gate)
            _store_rows(ys_ref.at[pl.ds(0, used)], _dot(act.astype(BF16), wd_s[...]) + bd_ref[0])
        if n_rows < ROW_BLK:
            ys_ref[used:, :] = jnp.zeros((ROW_BLK * ROW_SUB - used, LANES), F32)

    for pieces in range(ROW_BLK // ROW_PIECE + 1):
        pl.when(pieces_ref[i] == pieces)(lambda pieces=pieces: ffn(pieces * ROW_PIECE))


def _experts(xs, plan, w_up, b_up, w_down, b_down, n_rows):
    nb = n_rows // ROW_BLK
    b_gu = b_up.reshape(N_EXPERTS, D_EXPERT, 2).transpose(0, 2, 1)
    bd = b_down.reshape(N_EXPERTS, 1, D_MODEL)
    blk_e, nvalid, first, wslot, enext, pieces = plan
    src = lambda i, be, nv, *_: (jnp.minimum(i, nv[0] - 1), 0)
    bspec = lambda r, m: pl.BlockSpec((1, r, m), lambda i, be, *_: (be[i], 0, 0))
    anyspec = pl.BlockSpec(memory_space=pl.ANY)
    return pl.pallas_call(
        _experts_kernel,
        grid_spec=pltpu.PrefetchScalarGridSpec(
            num_scalar_prefetch=6,
            grid=(nb,),
            in_specs=[pl.BlockSpec((ROW_BLK * PACK_SUB, LANES), src), anyspec, anyspec,
                      bspec(2, D_EXPERT), bspec(1, D_MODEL)],
            out_specs=_rows_spec(ROW_BLK, lambda i, *_: (i, 0)),
            scratch_shapes=[
                pltpu.VMEM((2, D_MODEL, 2 * D_EXPERT), F32),
                pltpu.VMEM((2, D_EXPERT, D_MODEL), F32),
                pltpu.VMEM((D_MODEL, D_EXPERT), BF16),
                pltpu.VMEM((D_MODEL, D_EXPERT), BF16),
                pltpu.VMEM((D_EXPERT, D_MODEL), BF16),
                pltpu.SemaphoreType.DMA((2, 2)),
            ],
        ),
        out_shape=jax.ShapeDtypeStruct((n_rows * ROW_SUB, LANES), F32),
        compiler_params=_cparams(("arbitrary",)),
        name="experts",
    )(blk_e, nvalid, first, wslot, enext, pieces, xs, w_up, w_down, b_gu, bd)


def _combine_kernel(cnt_ref, ls_ref, base_ref, rows_ref, ys_hbm, route_ref, gate_ref, x1_ref, p_ref, gp_ref, wpg_ref,
                    wpp_ref, gf_ref, o_ref, yloc, sems):
    tm = TM_ROUTE
    n_pos = N_POS
    i = pl.program_id(0)
    n_steps = pl.num_programs(0)
    slot = lax.rem(i, 2)

    def gather_step(step, s):
        for u in range(COMB_SUB):
            def per_expert(e, _, u=u):
                idx = (step * COMB_SUB + u) * N_EXPERTS + e
                src0 = base_ref[idx]
                dst0 = ls_ref[idx]

                def start_copy(n):
                    pltpu.make_async_copy(ys_hbm.at[_row_slice(src0, n)],
                                          yloc.at[s, u, _row_slice(dst0, n)], sems.at[s]).start()

                _copy_run(cnt_ref[idx], start_copy)
                return 0

            lax.fori_loop(0, N_EXPERTS, per_expert, 0, unroll=RUN_UNROLL)

    @pl.when(i == 0)
    def _():
        yloc[...] = jnp.zeros_like(yloc)
        gather_step(0, 0)

    @pl.when(i + 1 < n_steps)
    def _():
        gather_step(i + 1, 1 - slot)

    for u in range(COMB_SUB):
        n = rows_ref[i * COMB_SUB + u]
        pltpu.make_async_copy(ys_hbm.at[_row_slice(0, n)], yloc.at[slot, u, _row_slice(0, n)], sems.at[slot]).wait()

    moe = []
    for u in range(COMB_SUB):
        cols = slice(u * tm, (u + 1) * tm)
        positions = [route_ref[TOP_K + k:TOP_K + k + 1, cols] for k in range(TOP_K)]
        gsel = _select_by_position(positions, [gate_ref[k:k + 1, cols] for k in range(TOP_K)], n_pos)
        g_pos = jnp.sum(gsel, axis=1, keepdims=True)
        sel = jnp.where(gsel != 0.0, 1.0, 0.0).astype(BF16)
        y_gated = (_load_rows(yloc.at[slot, u]) * g_pos).astype(BF16)
        moe.append(_dot_tn(sel, y_gated))
    x2 = x1_ref[...] + jnp.concatenate(moe, axis=0)
    xn = _rms(x2, gp_ref[...], NORM_EPS).astype(BF16)
    pp = _dot(p_ref[...].astype(BF16), wpp_ref[...])
    x3 = x2 + pp * jax.nn.sigmoid(_dot(xn, wpg_ref[...]))
    o_ref[...] = _rms(x3, gf_ref[...], NORM_EPS)


def _combine(ys, route, gate_t, x1, p2, tables, g_ple, w_ple_gate, w_ple_proj, g_final):
    t = x1.shape[0]
    tm = COMB_SUB * TM_ROUTE
    cnt_tbl, ls_tbl, base_tbl, _, tile_rows = tables
    row = lambda n: pl.BlockSpec((tm, n), lambda i, *_: (i, 0))
    col = pl.BlockSpec((SUBLANES, tm), lambda i, *_: (0, i))
    full = lambda a: pl.BlockSpec(a.shape, lambda i, *_: (0, 0))
    consts = (g_ple.reshape(1, -1), w_ple_gate.astype(BF16), w_ple_proj.astype(BF16), g_final.reshape(1, -1))
    return pl.pallas_call(
        _combine_kernel,
        grid_spec=pltpu.PrefetchScalarGridSpec(
            num_scalar_prefetch=4,
            grid=(t // tm,),
            in_specs=[pl.BlockSpec(memory_space=pl.ANY), col, col, row(D_MODEL), row(PLE_DIM)]
                     + [full(a) for a in consts],
            out_specs=row(D_MODEL),
            scratch_shapes=[
                pltpu.VMEM((2, COMB_SUB, N_POS * ROW_SUB, LANES), F32),
                pltpu.SemaphoreType.DMA((2,)),
            ],
        ),
        out_shape=jax.ShapeDtypeStruct((t, D_MODEL), F32),
        compiler_params=_cparams(("arbitrary",)),
        name="combine",
    )(cnt_tbl, ls_tbl, base_tbl, tile_rows, ys, route, gate_t, x1, p2, *consts)


def _routing_tables(cnt, n_blocks):
    tile_cnt = cnt[:, 0, :N_EXPERTS].astype(jnp.int32)
    tile_cnt = tile_cnt + tile_cnt % 2
    counts = jnp.sum(tile_cnt, axis=0)
    padded = (counts + ROW_BLK - 1) // ROW_BLK * ROW_BLK
    pend = jnp.cumsum(padded)
    pstart = pend - padded
    base = pstart[None, :] + jnp.cumsum(tile_cnt, axis=0) - tile_cnt
    lstart = jnp.cumsum(tile_cnt, axis=1) - tile_cnt
    nvalid = pend[-1:] // ROW_BLK
    zoff = jnp.concatenate([pstart + counts, nvalid, padded - counts])
    blk_start = jnp.minimum(jnp.arange(n_blocks, dtype=jnp.int32) * ROW_BLK, pend[-1] - 1)
    blk_e = jnp.minimum(jnp.sum((pend[None, :] <= blk_start[:, None]).astype(jnp.int32), axis=1), N_EXPERTS - 1)
    first = jnp.concatenate([jnp.ones((1,), bool), blk_e[1:] != blk_e[:-1]])
    wslot = (jnp.cumsum(first.astype(jnp.int32)) - 1) % 2
    eids = jnp.arange(N_EXPERTS, dtype=jnp.int32)
    later_nonempty = (eids[None, :] > eids[:, None]) & (padded[None, :] > 0)
    next_e = jnp.min(jnp.where(later_nonempty, eids[None, :], N_EXPERTS), axis=1)
    next_e = jnp.where(next_e == N_EXPERTS, -1, next_e)
    blk_hot = blk_e[:, None] == eids[None, :]
    per_block = lambda v: jnp.sum(jnp.where(blk_hot, v[None, :], 0), axis=1)
    blk_ids = jnp.arange(n_blocks, dtype=jnp.int32)
    real_rows = jnp.clip(per_block(pstart + counts) - blk_ids * ROW_BLK, 0, ROW_BLK) * (blk_ids < nvalid[0])
    pieces = (real_rows + ROW_PIECE - 1) // ROW_PIECE
    tile_rows = jnp.sum(tile_cnt, axis=1)
    i32 = lambda a: a.reshape(-1).astype(jnp.int32)
    plan = (i32(blk_e), i32(nvalid), i32(first), i32(wslot), i32(per_block(next_e)), i32(pieces))
    return (i32(tile_cnt), i32(lstart), i32(base), i32(zoff), i32(tile_rows)), plan


def kernel(x, p, g_mix, w_in, conv_w, conv_b, dt_bias, a_log, d_skip, ssd_norm_w, lam_q1, lam_k1, lam_q2, lam_k2, subln_w, w_out, g_ffn, w_router, b_router, w_up, b_up, w_down, b_down, g_ple, w_ple_gate, w_ple_proj, g_final):
    bsz, seq, d = x.shape
    t = bsz * seq
    x2 = x.reshape(t, d)
    y_ssd, y_att = _mixer(x2, g_mix[0], w_in[0], conv_w[0], conv_b[0], dt_bias[0], a_log[0], d_skip[0], ssd_norm_w[0],
                          lam_q1[0], lam_k1[0], lam_q2[0], lam_k2[0], subln_w[0], bsz, seq)
    x1, xn, route, gate_t, cnt = _out_proj(x2, y_ssd, y_att, w_out[0], g_ffn[0], w_router[0], b_router[0])

    n_rows = t * TOP_K + (t // TM_ROUTE) * N_EXPERTS + N_EXPERTS * ROW_BLK
    n_rows = -(-n_rows // ROW_BLK) * ROW_BLK
    tables, plan = _routing_tables(cnt, n_rows // ROW_BLK)
    xs = _dispatch(xn, route, tables, n_rows)
    ys = _experts(xs, plan, w_up[0], b_up[0], w_down[0], b_down[0], n_rows)
    out = _combine(ys, route, gate_t, x1, p[0].reshape(t, PLE_DIM), tables, g_ple[0], w_ple_gate[0], w_ple_proj[0],
                   g_final)
    return out.reshape(bsz, seq, d)
```

```python
import math

import jax
import jax.numpy as jnp
from jax import lax
from jax.experimental import pallas as pl
from jax.experimental.pallas import tpu as pltpu

F32 = jnp.float32
BF16 = jnp.bfloat16

D_MODEL = 1024
PLE_DIM = 256
SSD_WIDTH = 512
ATT_WIDTH = 512
SSD_HEAD_DIM = 64
SSD_HEADS = 8
SSD_GROUPS = 2
SSD_STATE = 128
SSD_CONV = 4
SSD_CHUNK = 128
SSD_CONV_CH = SSD_WIDTH + 2 * SSD_GROUPS * SSD_STATE
SSD_NORM_EPS = 1e-5
ATT_HEAD_DIM = 64
ATT_HEADS = 4
SUBLN_EPS = 1e-5
OFF_Z = 0
OFF_XBC = OFF_Z + SSD_WIDTH
OFF_DT = OFF_XBC + SSD_CONV_CH
OFF_Q = OFF_DT + SSD_HEADS
OFF_K = OFF_Q + ATT_WIDTH
OFF_V = OFF_K + ATT_WIDTH
IN_PROJ = OFF_V + ATT_WIDTH
N_EXPERTS = 32
TOP_K = 4
D_EXPERT = 1024
SWIGLU_LIMIT = 7.0
SWIGLU_ALPHA = 1.702
NORM_EPS = 1e-6
LAM_INIT = 0.8 - 0.6 * math.exp(-0.3 * 0)

LANES = 128
SUBLANES = 8
VMEM_LIMIT_BYTES = 56 * 1024 * 1024
ROW_SUB = D_MODEL // LANES
PACK_SUB = ROW_SUB // 2

TM_PROJ = 512
ATT_BQ = 512
ATT_BK = 512
ROW_BLK = 512
ROW_PIECE = 128
TM_ROUTE = 256
N_POS = TOP_K * TM_ROUTE + N_EXPERTS
OUT_SUB = 4
COMB_SUB = 2
W_COPIES = 4
RUN_UNROLL = 4
DEINT = 2 * LANES
ZXD_WIDTH = OFF_DT + LANES
CP_CONV_B = SSD_CONV
CP_WIDE = SSD_CONV + 1
CP_SMALL = SSD_CONV + 2
CP_DT_BIAS = 4 * LANES
CP_A_LOG = 5 * LANES


def _cparams(sem):
    return pltpu.CompilerParams(dimension_semantics=sem, vmem_limit_bytes=VMEM_LIMIT_BYTES)


def _rms(x, w, eps):
    return x * lax.rsqrt(jnp.mean(x * x, axis=-1, keepdims=True) + eps) * w


def _dot(a, b):
    return jnp.dot(a, b, preferred_element_type=F32)


def _dot_nt(a, b):
    return lax.dot_general(a, b, (((1,), (1,)), ((), ())), preferred_element_type=F32)


def _dot_tn(a, b):
    return lax.dot_general(a, b, (((0,), (0,)), ((), ())), preferred_element_type=F32)


def _store_rows(ref, val):
    n = val.shape[0]
    for s in range(ROW_SUB):
        ref[pl.ds(s, n, stride=ROW_SUB), :] = val[:, s * LANES:(s + 1) * LANES]


def _load_rows(ref):
    n = ref.shape[0] // ROW_SUB
    return jnp.concatenate([ref[pl.ds(s, n, stride=ROW_SUB), :] for s in range(ROW_SUB)], axis=-1)


def _rows_spec(n, index_map):
    return pl.BlockSpec((n * ROW_SUB, LANES), index_map)


def _row_slice(start, n, sub=ROW_SUB):
    return pl.ds(pl.multiple_of(start * sub, SUBLANES), n * sub)


def _pack_rows(ref, val):
    n = val.shape[0]
    bits = lambda v: lax.bitcast_convert_type(v, jnp.uint32)
    words = (bits(val[:, :D_MODEL // 2]) & jnp.uint32(0xFFFF0000)) | (bits(val[:, D_MODEL // 2:]) >> 16)
    for s in range(PACK_SUB):
        ref[pl.ds(s, n, stride=PACK_SUB), :] = words[:, s * LANES:(s + 1) * LANES]


def _unpack_rows(ref):
    n = ref.shape[0] // PACK_SUB
    words = [ref[pl.ds(s, n, stride=PACK_SUB), :] for s in range(PACK_SUB)]
    as_f32 = lambda w: lax.bitcast_convert_type(w, F32)
    hi = [as_f32(w & jnp.uint32(0xFFFF0000)) for w in words]
    lo = [as_f32(w << 16) for w in words]
    return jnp.concatenate(hi + lo, axis=-1).astype(BF16)


def _split3(x):
    hi = x.astype(BF16)
    r1 = x - hi.astype(F32)
    mid = r1.astype(BF16)
    lo = (r1 - mid.astype(F32)).astype(BF16)
    return hi, mid, lo


def _in_proj_kernel(x_ref, g_ref, win_ref, zxd_ref, q_ref, k_ref, v_ref, w_ref):
    @pl.when(pl.program_id(0) == 0)
    def _():
        w_ref[:OFF_DT, :] = win_ref[:OFF_DT, :].astype(BF16)
        dt_rows = jnp.concatenate([win_ref[OFF_DT:OFF_Q, :], jnp.zeros((LANES - SSD_HEADS, D_MODEL), F32)], axis=0)
        w_ref[OFF_DT:OFF_DT + LANES, :] = dt_rows.astype(BF16)
        w_ref[OFF_DT + LANES:, :] = win_ref[OFF_Q:, :].astype(BF16)

    h = _rms(x_ref[...], g_ref[...], NORM_EPS).astype(BF16)
    q0 = ZXD_WIDTH
    k0, v0 = q0 + ATT_WIDTH, q0 + 2 * ATT_WIDTH
    zxd_ref[...] = _dot_nt(h, w_ref[:q0, :])
    q_ref[...] = (_dot_nt(h, w_ref[q0:k0, :]) * (ATT_HEAD_DIM ** -0.5 * math.log2(math.e))).astype(BF16)
    k_ref[...] = _dot_nt(h, w_ref[k0:v0, :]).astype(BF16)
    v_ref[...] = _dot_nt(h, w_ref[v0:v0 + ATT_WIDTH, :]).astype(BF16)


def _in_proj(x2, g_mix, w_in_t):
    t = x2.shape[0]
    tm = TM_PROJ
    row = lambda n: pl.BlockSpec((tm, n), lambda i: (i, 0))
    full = lambda a: pl.BlockSpec(a.shape, lambda i: (0, 0))
    return pl.pallas_call(
        _in_proj_kernel,
        grid=(t // tm,),
        in_specs=[row(D_MODEL), full(g_mix), full(w_in_t)],
        out_specs=[row(ZXD_WIDTH), row(ATT_WIDTH), row(ATT_WIDTH), row(ATT_WIDTH)],
        out_shape=[
            jax.ShapeDtypeStruct((t, ZXD_WIDTH), F32),
            jax.ShapeDtypeStruct((t, ATT_WIDTH), BF16),
            jax.ShapeDtypeStruct((t, ATT_WIDTH), BF16),
            jax.ShapeDtypeStruct((t, ATT_WIDTH), BF16),
        ],
        scratch_shapes=[pltpu.VMEM((IN_PROJ + LANES - SSD_HEADS, D_MODEL), BF16)],
        compiler_params=_cparams(("arbitrary",)),
        name="in_proj",
    )(x2, g_mix, w_in_t)


def _ssd_chunk_init(c, xpad_ref, state_ref):
    L = SSD_CHUNK

    @pl.when(c == 0)
    def _():
        xpad_ref[0:SUBLANES, :] = jnp.zeros((SUBLANES, SSD_CONV_CH), F32)
        state_ref[...] = jnp.zeros_like(state_ref)

    @pl.when(c != 0)
    def _():
        xpad_ref[0:SUBLANES, :] = xpad_ref[L:L + SUBLANES, :]


def _ssd_chunk(zxd_ref, cp_ref, y_ref, xpad_ref, state_ref):
    L = SSD_CHUNK
    xpad_ref[SUBLANES:SUBLANES + L, :] = zxd_ref[:, OFF_XBC:OFF_DT]

    conv = cp_ref[CP_CONV_B:CP_CONV_B + 1, :]
    for j in range(SSD_CONV):
        off = SUBLANES - (SSD_CONV - 1) + j
        conv = conv + cp_ref[j:j + 1, :] * xpad_ref[off:off + L, :]
    act = conv * jax.nn.sigmoid(conv)
    xs = act[:, :SSD_WIDTH]
    bm = act[:, SSD_WIDTH:SSD_WIDTH + SSD_GROUPS * SSD_STATE].astype(BF16)
    cm = act[:, SSD_WIDTH + SSD_GROUPS * SSD_STATE:].astype(BF16)

    dt_in = zxd_ref[:, OFF_DT:] + cp_ref[CP_SMALL:CP_SMALL + 1, CP_DT_BIAS:CP_DT_BIAS + LANES]
    dt_all = jnp.maximum(dt_in, 0.0) + jnp.log1p(jnp.exp(-jnp.abs(dt_in)))
    adt = dt_all * (-jnp.exp(cp_ref[CP_SMALL:CP_SMALL + 1, CP_A_LOG:CP_A_LOG + LANES]))

    ri = lax.broadcasted_iota(jnp.int32, (L, L), 0)
    ci = lax.broadcasted_iota(jnp.int32, (L, L), 1)
    causal = ci <= ri
    tril = jnp.where(causal, 1.0, 0.0).astype(BF16)
    hi, mid, lo = _split3(adt)
    acum_all = _dot(tril, hi) + _dot(tril, mid) + _dot(tril, lo)
    acum_t = acum_all.T
    a_last = acum_all[L - 1:L, :]
    decay_in_all = jnp.exp(a_last - acum_all)
    decay_out_all = jnp.exp(acum_all)
    chunk_decay_all = jnp.exp(a_last)
    acum = lambda h: acum_all[:, h:h + 1]
    dt = lambda h: dt_all[:, h:h + 1]
    decay_out = lambda h: decay_out_all[:, h:h + 1]
    decay_in = lambda h: decay_in_all[:, h:h + 1]
    chunk_decay = lambda h: chunk_decay_all[:, h:h + 1]

    lane = lax.broadcasted_iota(jnp.int32, (L, LANES), 1)
    lo_half = lane < SSD_HEAD_DIM

    def per_pair(col_a, col_b):
        return jnp.where(lo_half, col_a, col_b)

    ys = []
    for pair in range(SSD_HEADS // 2):
        g = pair // 2
        h0, h1 = 2 * pair, 2 * pair + 1
        cg = cm[:, g * SSD_STATE:(g + 1) * SSD_STATE]
        bg = bm[:, g * SSD_STATE:(g + 1) * SSD_STATE]
        cb = _dot_nt(cg, bg)
        x_pair = xs[:, pair * LANES:(pair + 1) * LANES]
        xdt = x_pair * per_pair(dt(h0), dt(h1))
        y_pair = jnp.zeros((L, LANES), F32)
        for hh, keep in ((h0, lo_half), (h1, jnp.logical_not(lo_half))):
            seg = acum(hh) - acum_t[hh:hh + 1, :]
            lmat = jnp.where(causal, jnp.exp(jnp.where(causal, seg, 0.0)), 0.0)
            m = (cb * lmat).astype(BF16)
            y_pair = y_pair + _dot(m, jnp.where(keep, xdt, 0.0).astype(BF16))
        s_prev = state_ref[pair]
        y_off = _dot(cg, s_prev.astype(BF16)) * per_pair(decay_out(h0), decay_out(h1))
        w_in = (xdt * per_pair(decay_in(h0), decay_in(h1))).astype(BF16)
        cd = jnp.where(lane[0:1, :] < SSD_HEAD_DIM, chunk_decay(h0), chunk_decay(h1))
        state_ref[pair] = s_prev * cd + _dot_tn(bg, w_in)
        ys.append(y_pair + y_off + cp_ref[CP_WIDE:CP_WIDE + 1, pair * LANES:(pair + 1) * LANES] * x_pair)

    y = jnp.concatenate(ys, axis=-1)
    zz = zxd_ref[:, OFF_Z:OFF_XBC]
    y = y * (zz * jax.nn.sigmoid(zz))
    gw = SSD_WIDTH // SSD_GROUPS
    outs = []
    for g in range(SSD_GROUPS):
        yg = y[:, g * gw:(g + 1) * gw]
        outs.append(yg * lax.rsqrt(jnp.mean(yg * yg, axis=-1, keepdims=True) + SSD_NORM_EPS))
    y_ref[...] = (jnp.concatenate(outs, axis=-1) * cp_ref[CP_WIDE:CP_WIDE + 1, SSD_WIDTH:]).astype(BF16)


def _attn_ssd_kernel(q_ref, k_ref, v_ref, sw_ref, zxd_ref, cp_ref, o_ref, y_ref, s_scr, vt_scr, xpad_ref, state_ref):
    bq, bk = ATT_BQ, ATT_BK
    qi = pl.program_id(2)
    n_maps = 2
    _ssd_chunk_init(pl.program_id(1) * pl.num_programs(2) + qi, xpad_ref, state_ref)

    @pl.when(qi == 0)
    def _():
        vt_scr[...] = v_ref[...].astype(F32).T.astype(BF16)

    def fold(t, reduce):
        return reduce(t.reshape(bk // SUBLANES, SUBLANES, bq), axis=0)

    def merge(old, new, op):
        return new if old is None else op(old, new)

    def attend(nk):
        _ssd_chunk(zxd_ref, cp_ref, y_ref, xpad_ref, state_ref)
        q = q_ref[...]
        lane = lax.broadcasted_iota(jnp.int32, (bq, LANES), 1)
        zero = jnp.zeros_like(q)
        q_maps = (jnp.where(lane < ATT_HEAD_DIM, q, zero), jnp.where(lane >= ATT_HEAD_DIM, q, zero))
        key = lax.broadcasted_iota(jnp.int32, (bk, bq), 0)
        qry = lax.broadcasted_iota(jnp.int32, (bk, bq), 1)
        causal = key <= qry

        mt = [None] * n_maps
        for j in range(nk):
            kb = k_ref[j * bk:(j + 1) * bk, :]
            for m in range(n_maps):
                s = _dot_nt(kb, q_maps[m])
                if j == nk - 1:
                    s = jnp.where(causal, s, -jnp.inf)
                s_scr[m, j] = s
                mt[m] = merge(mt[m], fold(s, jnp.max), jnp.maximum)
        q_max = [jnp.max(t, axis=0, keepdims=True) for t in mt]

        lt = [None] * n_maps
        acc = [None] * n_maps
        for j in range(nk):
            vt = vt_scr[:, j * bk:(j + 1) * bk]
            for m in range(n_maps):
                p = jnp.exp2(s_scr[m, j] - q_max[m])
                lt[m] = merge(lt[m], fold(p, jnp.sum), jnp.add)
                acc[m] = merge(acc[m], _dot(vt, p.astype(BF16)), jnp.add)

        lams = [cp_ref[CP_SMALL:CP_SMALL + 1, n * LANES:(n + 1) * LANES] for n in range(4)]
        lam = (jnp.exp(jnp.sum(lams[0] * lams[1], axis=-1, keepdims=True))
               - jnp.exp(jnp.sum(lams[2] * lams[3], axis=-1, keepdims=True)) + LAM_INIT)
        l1 = jnp.sum(lt[0], axis=0, keepdims=True)
        l2 = jnp.sum(lt[1], axis=0, keepdims=True)
        o = acc[0] / l1 - lam * (acc[1] / l2)
        o = o * lax.rsqrt(jnp.mean(o * o, axis=0, keepdims=True) + SUBLN_EPS) * sw_ref[...]
        o_ref[...] = (o * (1.0 - LAM_INIT)).T.astype(BF16)

    for nk in range(1, k_ref.shape[0] // bk + 1):
        pl.when(qi == nk - 1)(lambda nk=nk: attend(nk))


def _attn_ssd(q, k, v, lam_q1, lam_k1, lam_q2, lam_k2, subln_w, zxd, conv_w, conv_b, dt_bias, a_log, d_skip,
              norm_w, bsz, seq):
    nq = seq // ATT_BQ
    nc = seq // SSD_CHUNK
    assert nc == ATT_HEADS * nq
    qspec = pl.BlockSpec((ATT_BQ, LANES), lambda b, h, i: (b * nq + i, h))
    kvspec = pl.BlockSpec((seq, LANES), lambda b, h, i: (b, h))
    full = lambda a: pl.BlockSpec(a.shape, lambda b, h, i: (0, 0))
    chunk = lambda n: pl.BlockSpec((SSD_CHUNK, n), lambda b, h, i: (b * nc + h * nq + i, 0))
    sw = subln_w.reshape(-1, 1)
    pad_to = lambda v, n: jnp.pad(v.reshape(1, -1), ((0, 0), (0, n - v.size)))
    wide = jnp.concatenate([jnp.repeat(d_skip, SSD_HEAD_DIM), norm_w]).reshape(1, SSD_CONV_CH)
    small = jnp.concatenate([pad_to(a, LANES) for a in (lam_q1, lam_k1, lam_q2, lam_k2, dt_bias, a_log)], axis=1)
    consts = jnp.concatenate([conv_w, conv_b.reshape(1, -1), wide, pad_to(small, SSD_CONV_CH),
                              jnp.zeros((SUBLANES - CP_SMALL - 1, SSD_CONV_CH), F32)], axis=0)
    return pl.pallas_call(
        _attn_ssd_kernel,
        grid=(bsz, ATT_HEADS, nq),
        in_specs=[qspec, kvspec, kvspec, full(sw), chunk(ZXD_WIDTH), full(consts)],
        out_specs=[qspec, chunk(SSD_WIDTH)],
        out_shape=[jax.ShapeDtypeStruct((bsz * seq, ATT_WIDTH), BF16),
                   jax.ShapeDtypeStruct((bsz * seq, SSD_WIDTH), BF16)],
        scratch_shapes=[pltpu.VMEM((2, seq // ATT_BK, ATT_BK, ATT_BQ), F32),
                        pltpu.VMEM((LANES, seq), BF16),
                        pltpu.VMEM((SSD_CHUNK + 2 * SUBLANES, SSD_CONV_CH), F32),
                        pltpu.VMEM((SSD_HEADS // 2, SSD_STATE, LANES), F32)],
        compiler_params=_cparams(("arbitrary", "arbitrary", "arbitrary")),
        name="attn_ssd",
    )(q, k, v, sw, zxd, consts)


def _mixer(x2, g_mix, w_in, conv_w, conv_b, dt_bias, a_log, d_skip, ssd_norm_w,
           lam_q1, lam_k1, lam_q2, lam_k2, subln_w, bsz, seq):
    zxd, q, k, v = _in_proj(x2, g_mix.reshape(1, -1), w_in.T)
    y_att, y_ssd = _attn_ssd(q, k, v, lam_q1, lam_k1, lam_q2, lam_k2, subln_w, zxd, conv_w, conv_b, dt_bias,
                             a_log, d_skip, ssd_norm_w, bsz, seq)
    return y_ssd, y_att


def _out_proj_kernel(x_ref, ys_ref, ya_ref, wo_ref, g_ref, wr2_ref, br_ref,
                     x1_ref, xn_ref, route_ref, gate_ref, cnt_ref):
    tm = TM_ROUTE
    x1 = x_ref[...] + _dot(jnp.concatenate([ys_ref[...], ya_ref[...]], axis=-1), wo_ref[...])
    x1_ref[...] = x1
    xn_all = _rms(x1, g_ref[...], NORM_EPS)
    xn_b = xn_all.astype(BF16)
    xn_ref[...] = xn_b

    n_tok = OUT_SUB * tm
    lg2 = _dot(xn_b, wr2_ref[...])
    logits = (lg2[:, :LANES] + lg2[:, LANES:]).T[:N_EXPERTS, :] + br_ref[...]

    eidx = lax.broadcasted_iota(jnp.int32, (N_EXPERTS, n_tok), 0).astype(F32)
    work = logits
    vals, idxs, hots = [], [], []
    for _ in range(TOP_K):
        m = jnp.max(work, axis=0, keepdims=True)
        idx = jnp.min(jnp.where(work == m, eidx, float(N_EXPERTS)), axis=0, keepdims=True)
        hot = eidx == idx
        vals.append(m)
        idxs.append(idx.astype(jnp.int32))
        hots.append(hot)
        work = jnp.where(hot, -jnp.inf, work)
    exps = [jnp.exp(v - vals[0]) for v in vals]
    denom = exps[0] + exps[1] + exps[2] + exps[3]
    gates = [e / denom for e in exps]

    cnt = jnp.zeros((N_EXPERTS, n_tok), F32)
    for hot in hots:
        cnt = cnt + jnp.where(hot, 1.0, 0.0)
    cnt_b = cnt.astype(BF16)
    r = lax.broadcasted_iota(jnp.int32, (tm, tm), 0)
    c = lax.broadcasted_iota(jnp.int32, (tm, tm), 1)
    earlier_tok = jnp.where(r < c, 1.0, 0.0).astype(BF16)
    er = lax.broadcasted_iota(jnp.int32, (N_EXPERTS, N_EXPERTS), 0)
    ec = lax.broadcasted_iota(jnp.int32, (N_EXPERTS, N_EXPERTS), 1)
    lower_exp = jnp.where(ec < er, 1.0, 0.0).astype(BF16)
    cnt_pad = jnp.concatenate([cnt_b, jnp.zeros((LANES - N_EXPERTS, n_tok), BF16)], axis=0)
    pos = []
    for sub in range(OUT_SUB):
        cols = slice(sub * tm, (sub + 1) * tm)
        run = jnp.sum(cnt[:, cols], axis=1, keepdims=True)
        run_even = 2.0 * jnp.floor(0.5 * run + 0.5)
        run_start = _dot(lower_exp, jnp.broadcast_to(run_even, (N_EXPERTS, LANES)).astype(BF16))[:, 0:1]
        pos.append(_dot(cnt_b[:, cols], earlier_tok) + run_start)
        cnt_ref[sub] = _dot_nt(jnp.ones((SUBLANES, tm), BF16), cnt_pad[:, cols])
    pos = jnp.concatenate(pos, axis=1)
    lps = [jnp.sum(jnp.where(hot, pos, 0.0), axis=0, keepdims=True).astype(jnp.int32) for hot in hots]

    route_ref[...] = jnp.concatenate(idxs + lps, axis=0)
    gate_ref[...] = jnp.concatenate(gates + [jnp.zeros((SUBLANES - TOP_K, n_tok), F32)], axis=0)


def _out_proj(x2, y_ssd, y_att, w_out, g_ffn, w_router, b_router):
    t = x2.shape[0]
    tm = OUT_SUB * TM_ROUTE
    nt = t // TM_ROUTE
    wo = w_out.astype(BF16)
    wrh, wrm, _ = _split3(jnp.pad(w_router, ((0, 0), (0, LANES - N_EXPERTS))))
    wr2 = jnp.concatenate([wrh, wrm], axis=1)
    br = b_router.reshape(N_EXPERTS, 1)
    row = lambda n: pl.BlockSpec((tm, n), lambda i: (i, 0))
    col = pl.BlockSpec((SUBLANES, tm), lambda i: (0, i))
    full = lambda a: pl.BlockSpec(a.shape, lambda i: (0, 0))
    args = (x2, y_ssd, y_att, wo, g_ffn.reshape(1, -1), wr2, br)
    return pl.pallas_call(
        _out_proj_kernel,
        grid=(t // tm,),
        in_specs=[row(D_MODEL), row(SSD_WIDTH), row(ATT_WIDTH)] + [full(a) for a in args[3:]],
        out_specs=[row(D_MODEL), row(D_MODEL), col, col,
                   pl.BlockSpec((OUT_SUB, SUBLANES, LANES), lambda i: (i, 0, 0))],
        out_shape=[
            jax.ShapeDtypeStruct((t, D_MODEL), F32),
            jax.ShapeDtypeStruct((t, D_MODEL), BF16),
            jax.ShapeDtypeStruct((SUBLANES, t), jnp.int32),
            jax.ShapeDtypeStruct((SUBLANES, t), F32),
            jax.ShapeDtypeStruct((nt, SUBLANES, LANES), F32),
        ],
        compiler_params=_cparams(("arbitrary",)),
        name="out_proj",
    )(*args)


def _copy_run(n, start_copy):
    @pl.when(n > 0)
    def _():
        start_copy(n)


def _select_by_position(positions, values, n_pos):
    tm = positions[0].shape[1]
    r = lax.broadcasted_iota(jnp.int32, (n_pos, tm), 0)
    out = jnp.zeros((n_pos, tm), F32)
    for k in reversed(range(TOP_K)):
        out = jnp.where(r == positions[k], values[k], out)
    return out


def _dispatch_kernel(cnt_ref, ls_ref, base_ref, zoff_ref, rows_ref, xn_ref, route_ref, xs_hbm, xloc, zeros_vmem,
                     sem_z, sems):
    n_pos = N_POS
    i = pl.program_id(0)
    n_steps = pl.num_programs(0)
    slot = lax.rem(i, 2)

    def zero_fill(op):
        def pad_rows(e, _):
            n = zoff_ref[N_EXPERTS + 1 + e]

            @pl.when(n > 0)
            def _():
                op(pltpu.make_async_copy(zeros_vmem.at[_row_slice(0, n, PACK_SUB)],
                                         xs_hbm.at[_row_slice(zoff_ref[e], n, PACK_SUB)], sem_z))
            return 0

        def unused_block(b, _):
            op(pltpu.make_async_copy(zeros_vmem, xs_hbm.at[_row_slice(b * ROW_BLK, ROW_BLK, PACK_SUB)], sem_z))
            return 0

        lax.fori_loop(0, N_EXPERTS, pad_rows, 0)
        lax.fori_loop(zoff_ref[N_EXPERTS], xs_hbm.shape[0] // (ROW_BLK * PACK_SUB), unused_block, 0)

    @pl.when(i == 0)
    def _():
        zeros_vmem[...] = jnp.zeros_like(zeros_vmem)
        zero_fill(lambda cp: cp.start())

    def slot_wait(s, step):
        n = rows_ref[step]
        pltpu.make_async_copy(xloc.at[s, _row_slice(0, n, PACK_SUB)], xs_hbm.at[_row_slice(0, n, PACK_SUB)],
                              sems.at[s]).wait()

    @pl.when(i >= 2)
    def _():
        slot_wait(slot, i - 2)

    positions = [route_ref[TOP_K + k:TOP_K + k + 1, :] for k in range(TOP_K)]
    sel = _select_by_position(positions, [1.0] * TOP_K, n_pos).astype(BF16)
    rows = _dot(sel, xn_ref[...])
    _pack_rows(xloc.at[slot], rows)

    def per_expert(e, _):
        idx = i * N_EXPERTS + e
        src0 = ls_ref[idx]
        dst0 = base_ref[idx]

        def start_copy(n):
            pltpu.make_async_copy(xloc.at[slot, _row_slice(src0, n, PACK_SUB)],
                                  xs_hbm.at[_row_slice(dst0, n, PACK_SUB)], sems.at[slot]).start()

        _copy_run(cnt_ref[idx], start_copy)
        return 0

    lax.fori_loop(0, N_EXPERTS, per_expert, 0, unroll=RUN_UNROLL)

    @pl.when(i == n_steps - 1)
    def _():
        slot_wait(slot, i)

        @pl.when(n_steps > 1)
        def _():
            slot_wait(1 - slot, i - 1)

        zero_fill(lambda cp: cp.wait())


def _dispatch(xn, route, tables, n_rows):
    t = xn.shape[0]
    tm = TM_ROUTE
    cnt_tbl, ls_tbl, base_tbl, zoff, tile_rows = tables
    return pl.pallas_call(
        _dispatch_kernel,
        grid_spec=pltpu.PrefetchScalarGridSpec(
            num_scalar_prefetch=5,
            grid=(t // tm,),
            in_specs=[pl.BlockSpec((tm, D_MODEL), lambda i, *_: (i, 0)),
                      pl.BlockSpec((SUBLANES, tm), lambda i, *_: (0, i))],
            out_specs=pl.BlockSpec(memory_space=pl.ANY),
            scratch_shapes=[
                pltpu.VMEM((2, N_POS * PACK_SUB, LANES), jnp.uint32),
                pltpu.VMEM((ROW_BLK * PACK_SUB, LANES), jnp.uint32),
                pltpu.SemaphoreType.DMA,
                pltpu.SemaphoreType.DMA((2,)),
            ],
        ),
        out_shape=jax.ShapeDtypeStruct(((n_rows + ROW_BLK) * PACK_SUB, LANES), jnp.uint32),
        compiler_params=_cparams(("arbitrary",)),
        name="dispatch",
    )(cnt_tbl, ls_tbl, base_tbl, zoff, tile_rows, xn, route)


def _experts_kernel(blk_e_ref, nvalid_ref, first_ref, wslot_ref, enext_ref, pieces_ref, xs_ref, wup_hbm, wdn_hbm,
                    bgu_ref, bd_ref, ys_ref, wup_buf, wdn_buf, wg_s, wu_s, wd_s, sems):
    i = pl.program_id(0)
    slot = wslot_ref[i]

    def weight_copies(e, s):
        copies = []
        for c in range(W_COPIES):
            up_rows = pl.ds(c * (D_MODEL // W_COPIES), D_MODEL // W_COPIES)
            dn_rows = pl.ds(c * (D_EXPERT // W_COPIES), D_EXPERT // W_COPIES)
            copies.append(pltpu.make_async_copy(wup_hbm.at[e, up_rows], wup_buf.at[s, up_rows], sems.at[0, s]))
            copies.append(pltpu.make_async_copy(wdn_hbm.at[e, dn_rows], wdn_buf.at[s, dn_rows], sems.at[1, s]))
        return copies

    @pl.when(i == 0)
    def _():
        for cp in weight_copies(blk_e_ref[0], slot):
            cp.start()

    @pl.when(first_ref[i] != 0)
    def _():
        for cp in weight_copies(blk_e_ref[i], slot):
            cp.wait()

        @pl.when(enext_ref[i] >= 0)
        def _():
            for cp in weight_copies(enext_ref[i], 1 - slot):
                cp.start()

        src = lax.broadcasted_iota(jnp.int32, (DEINT, DEINT), 0)
        dst = lax.broadcasted_iota(jnp.int32, (DEINT, DEINT), 1)
        perm = jnp.where(src == jnp.where(dst < LANES, 2 * dst, 2 * (dst - LANES) + 1), 1.0, 0.0).astype(BF16)
        for g in range(2 * D_EXPERT // DEINT):
            sep = _dot(wup_buf[slot, :, g * DEINT:(g + 1) * DEINT].astype(BF16), perm)
            wg_s[:, g * LANES:(g + 1) * LANES] = sep[:, :LANES].astype(BF16)
            wu_s[:, g * LANES:(g + 1) * LANES] = sep[:, LANES:].astype(BF16)
        wd_s[...] = wdn_buf[slot].astype(BF16)

    def ffn(n_rows):
        used = n_rows * ROW_SUB
        if n_rows:
            xb = _unpack_rows(xs_ref.at[pl.ds(0, n_rows * PACK_SUB)])
            gate = jnp.minimum(_dot(xb, wg_s[...]) + bgu_ref[0, 0:1, :], SWIGLU_LIMIT)
            up = jnp.clip(_dot(xb, wu_s[...]) + bgu_ref[0, 1:2, :], -SWIGLU_LIMIT, SWIGLU_LIMIT)
            act = (up + 1.0) * gate * jax.nn.sigmoid(SWIGLU_ALPHA * gate)
            _store_rows(ys_ref.at[pl.ds(0, used)], _dot(act.astype(BF16), wd_s[...]) + bd_ref[0])
        if n_rows < ROW_BLK:
            ys_ref[used:, :] = jnp.zeros((ROW_BLK * ROW_SUB - used, LANES), F32)

    for pieces in range(ROW_BLK // ROW_PIECE + 1):
        pl.when(pieces_ref[i] == pieces)(lambda pieces=pieces: ffn(pieces * ROW_PIECE))


def _experts(xs, plan, w_up, b_up, w_down, b_down, n_rows):
    nb = n_rows // ROW_BLK
    b_gu = b_up.reshape(N_EXPERTS, D_EXPERT, 2).transpose(0, 2, 1)
    bd = b_down.reshape(N_EXPERTS, 1, D_MODEL)
    blk_e, nvalid, first, wslot, enext, pieces = plan
    src = lambda i, be, nv, *_: (jnp.minimum(i, nv[0] - 1), 0)
    bspec = lambda r, m: pl.BlockSpec((1, r, m), lambda i, be, *_: (be[i], 0, 0))
    anyspec = pl.BlockSpec(memory_space=pl.ANY)
    return pl.pallas_call(
        _experts_kernel,
        grid_spec=pltpu.PrefetchScalarGridSpec(
            num_scalar_prefetch=6,
            grid=(nb,),
            in_specs=[pl.BlockSpec((ROW_BLK * PACK_SUB, LANES), src), anyspec, anyspec,
                      bspec(2, D_EXPERT), bspec(1, D_MODEL)],
            out_specs=_rows_spec(ROW_BLK, lambda i, *_: (i, 0)),
            scratch_shapes=[
                pltpu.VMEM((2, D_MODEL, 2 * D_EXPERT), F32),
                pltpu.VMEM((2, D_EXPERT, D_MODEL), F32),
                pltpu.VMEM((D_MODEL, D_EXPERT), BF16),
                pltpu.VMEM((D_MODEL, D_EXPERT), BF16),
                pltpu.VMEM((D_EXPERT, D_MODEL), BF16),
                pltpu.SemaphoreType.DMA((2, 2)),
            ],
        ),
        out_shape=jax.ShapeDtypeStruct((n_rows * ROW_SUB, LANES), F32),
        compiler_params=_cparams(("arbitrary",)),
        name="experts",
    )(blk_e, nvalid, first, wslot, enext, pieces, xs, w_up, w_down, b_gu, bd)


def _combine_kernel(cnt_ref, ls_ref, base_ref, rows_ref, ys_hbm, route_ref, gate_ref, x1_ref, p_ref, gp_ref, wpg_ref,
                    wpp_ref, gf_ref, o_ref, yloc, sems):
    tm = TM_ROUTE
    n_pos = N_POS
    i = pl.program_id(0)
    n_steps = pl.num_programs(0)
    slot = lax.rem(i, 2)

    def gather_step(step, s):
        for u in range(COMB_SUB):
            def per_expert(e, _, u=u):
                idx = (step * COMB_SUB + u) * N_EXPERTS + e
                src0 = base_ref[idx]
                dst0 = ls_ref[idx]

                def start_copy(n):
                    pltpu.make_async_copy(ys_hbm.at[_row_slice(src0, n)],
                                          yloc.at[s, u, _row_slice(dst0, n)], sems.at[s]).start()

                _copy_run(cnt_ref[idx], start_copy)
                return 0

            lax.fori_loop(0, N_EXPERTS, per_expert, 0, unroll=RUN_UNROLL)

    @pl.when(i == 0)
    def _():
        yloc[...] = jnp.zeros_like(yloc)
        gather_step(0, 0)

    @pl.when(i + 1 < n_steps)
    def _():
        gather_step(i + 1, 1 - slot)

    for u in range(COMB_SUB):
        n = rows_ref[i * COMB_SUB + u]
        pltpu.make_async_copy(ys_hbm.at[_row_slice(0, n)], yloc.at[slot, u, _row_slice(0, n)], sems.at[slot]).wait()

    moe = []
    for u in range(COMB_SUB):
        cols = slice(u * tm, (u + 1) * tm)
        positions = [route_ref[TOP_K + k:TOP_K + k + 1, cols] for k in range(TOP_K)]
        gsel = _select_by_position(positions, [gate_ref[k:k + 1, cols] for k in range(TOP_K)], n_pos)
        g_pos = jnp.sum(gsel, axis=1, keepdims=True)
        sel = jnp.where(gsel != 0.0, 1.0, 0.0).astype(BF16)
        y_gated = (_load_rows(yloc.at[slot, u]) * g_pos).astype(BF16)
        moe.append(_dot_tn(sel, y_gated))
    x2 = x1_ref[...] + jnp.concatenate(moe, axis=0)
    xn = _rms(x2, gp_ref[...], NORM_EPS).astype(BF16)
    pp = _dot(p_ref[...].astype(BF16), wpp_ref[...])
    x3 = x2 + pp * jax.nn.sigmoid(_dot(xn, wpg_ref[...]))
    o_ref[...] = _rms(x3, gf_ref[...], NORM_EPS)


def _combine(ys, route, gate_t, x1, p2, tables, g_ple, w_ple_gate, w_ple_proj, g_final):
    t = x1.shape[0]
    tm = COMB_SUB * TM_ROUTE
    cnt_tbl, ls_tbl, base_tbl, _, tile_rows = tables
    row = lambda n: pl.BlockSpec((tm, n), lambda i, *_: (i, 0))
    col = pl.BlockSpec((SUBLANES, tm), lambda i, *_: (0, i))
    full = lambda a: pl.BlockSpec(a.shape, lambda i, *_: (0, 0))
    consts = (g_ple.reshape(1, -1), w_ple_gate.astype(BF16), w_ple_proj.astype(BF16), g_final.reshape(1, -1))
    return pl.pallas_call(
        _combine_kernel,
        grid_spec=pltpu.PrefetchScalarGridSpec(
            num_scalar_prefetch=4,
            grid=(t // tm,),
            in_specs=[pl.BlockSpec(memory_space=pl.ANY), col, col, row(D_MODEL), row(PLE_DIM)]
                     + [full(a) for a in consts],
            out_specs=row(D_MODEL),
            scratch_shapes=[
                pltpu.VMEM((2, COMB_SUB, N_POS * ROW_SUB, LANES), F32),
                pltpu.SemaphoreType.DMA((2,)),
            ],
        ),
        out_shape=jax.ShapeDtypeStruct((t, D_MODEL), F32),
        compiler_params=_cparams(("arbitrary",)),
        name="combine",
    )(cnt_tbl, ls_tbl, base_tbl, tile_rows, ys, route, gate_t, x1, p2, *consts)


def _routing_tables(cnt, n_blocks):
    tile_cnt = cnt[:, 0, :N_EXPERTS].astype(jnp.int32)
    tile_cnt = tile_cnt + tile_cnt % 2
    counts = jnp.sum(tile_cnt, axis=0)
    padded = (counts + ROW_BLK - 1) // ROW_BLK * ROW_BLK
    pend = jnp.cumsum(padded)
    pstart = pend - padded
    base = pstart[None, :] + jnp.cumsum(tile_cnt, axis=0) - tile_cnt
    lstart = jnp.cumsum(tile_cnt, axis=1) - tile_cnt
    nvalid = pend[-1:] // ROW_BLK
    zoff = jnp.concatenate([pstart + counts, nvalid, padded - counts])
    blk_start = jnp.minimum(jnp.arange(n_blocks, dtype=jnp.int32) * ROW_BLK, pend[-1] - 1)
    blk_e = jnp.minimum(jnp.sum((pend[None, :] <= blk_start[:, None]).astype(jnp.int32), axis=1), N_EXPERTS - 1)
    first = jnp.concatenate([jnp.ones((1,), bool), blk_e[1:] != blk_e[:-1]])
    wslot = (jnp.cumsum(first.astype(jnp.int32)) - 1) % 2
    eids = jnp.arange(N_EXPERTS, dtype=jnp.int32)
    later_nonempty = (eids[None, :] > eids[:, None]) & (padded[None, :] > 0)
    next_e = jnp.min(jnp.where(later_nonempty, eids[None, :], N_EXPERTS), axis=1)
    next_e = jnp.where(next_e == N_EXPERTS, -1, next_e)
    blk_hot = blk_e[:, None] == eids[None, :]
    per_block = lambda v: jnp.sum(jnp.where(blk_hot, v[None, :], 0), axis=1)
    blk_ids = jnp.arange(n_blocks, dtype=jnp.int32)
    real_rows = jnp.clip(per_block(pstart + counts) - blk_ids * ROW_BLK, 0, ROW_BLK) * (blk_ids < nvalid[0])
    pieces = (real_rows + ROW_PIECE - 1) // ROW_PIECE
    tile_rows = jnp.sum(tile_cnt, axis=1)
    i32 = lambda a: a.reshape(-1).astype(jnp.int32)
    plan = (i32(blk_e), i32(nvalid), i32(first), i32(wslot), i32(per_block(next_e)), i32(pieces))
    return (i32(tile_cnt), i32(lstart), i32(base), i32(zoff), i32(tile_rows)), plan


def kernel(x, p, g_mix, w_in, conv_w, conv_b, dt_bias, a_log, d_skip, ssd_norm_w, lam_q1, lam_k1, lam_q2, lam_k2, subln_w, w_out, g_ffn, w_router, b_router, w_up, b_up, w_down, b_down, g_ple, w_ple_gate, w_ple_proj, g_final):
    bsz, seq, d = x.shape
    t = bsz * seq
    x2 = x.reshape(t, d)
    y_ssd, y_att = _mixer(x2, g_mix[0], w_in[0], conv_w[0], conv_b[0], dt_bias[0], a_log[0], d_skip[0], ssd_norm_w[0],
                          lam_q1[0], lam_k1[0], lam_q2[0], lam_k2[0], subln_w[0], bsz, seq)
    x1, xn, route, gate_t, cnt = _out_proj(x2, y_ssd, y_att, w_out[0], g_ffn[0], w_router[0], b_router[0])

    n_rows = t * TOP_K + (t // TM_ROUTE) * N_EXPERTS + N_EXPERTS * ROW_BLK
    n_rows = -(-n_rows // ROW_BLK) * ROW_BLK
    tables, plan = _routing_tables(cnt, n_rows // ROW_BLK)
    xs = _dispatch(xn, route, tables, n_rows)
    ys = _experts(xs, plan, w_up[0], b_up[0], w_down[0], b_down[0], n_rows)
    out = _combine(ys, route, gate_t, x1, p[0].reshape(t, PLE_DIM), tables, g_ple[0], w_ple_gate[0], w_ple_proj[0],
                   g_final)
    return out.reshape(bsz, seq, d)
```

```python
import math

import jax
import jax.numpy as jnp
from jax import lax
from jax.experimental import pallas as pl
from jax.experimental.pallas import tpu as pltpu

F32 = jnp.float32
BF16 = jnp.bfloat16

D_MODEL = 1024
PLE_DIM = 256
SSD_WIDTH = 512
ATT_WIDTH = 512
SSD_HEAD_DIM = 64
SSD_HEADS = 8
SSD_GROUPS = 2
SSD_STATE = 128
SSD_CONV = 4
SSD_CHUNK = 128
SSD_CONV_CH = SSD_WIDTH + 2 * SSD_GROUPS * SSD_STATE
SSD_NORM_EPS = 1e-5
ATT_HEAD_DIM = 64
ATT_HEADS = 4
SUBLN_EPS = 1e-5
OFF_Z = 0
OFF_XBC = OFF_Z + SSD_WIDTH
OFF_DT = OFF_XBC + SSD_CONV_CH
OFF_Q = OFF_DT + SSD_HEADS
OFF_K = OFF_Q + ATT_WIDTH
OFF_V = OFF_K + ATT_WIDTH
IN_PROJ = OFF_V + ATT_WIDTH
N_EXPERTS = 32
TOP_K = 4
D_EXPERT = 1024
SWIGLU_LIMIT = 7.0
SWIGLU_ALPHA = 1.702
NORM_EPS = 1e-6
LAM_INIT = 0.8 - 0.6 * math.exp(-0.3 * 0)

LANES = 128
SUBLANES = 8
VMEM_LIMIT_BYTES = 56 * 1024 * 1024
ROW_SUB = D_MODEL // LANES
PACK_SUB = ROW_SUB // 2

TM_PROJ = 512
ATT_BQ = 512
ATT_BK = 512
ROW_BLK = 512
ROW_PIECE = 128
TM_ROUTE = 256
N_POS = TOP_K * TM_ROUTE + N_EXPERTS
OUT_SUB = 4
COMB_SUB = 2
RUN_UNROLL = 4
DEINT = 2 * LANES
ZXD_WIDTH = OFF_DT + LANES
CP_CONV_B = SSD_CONV
CP_WIDE = SSD_CONV + 1
CP_SMALL = SSD_CONV + 2
CP_DT_BIAS = 4 * LANES
CP_A_LOG = 5 * LANES


def _cparams(sem):
    return pltpu.CompilerParams(dimension_semantics=sem, vmem_limit_bytes=VMEM_LIMIT_BYTES)


def _rms(x, w, eps):
    return x * lax.rsqrt(jnp.mean(x * x, axis=-1, keepdims=True) + eps) * w


def _dot(a, b):
    return jnp.dot(a, b, preferred_element_type=F32)


def _dot_nt(a, b):
    return lax.dot_general(a, b, (((1,), (1,)), ((), ())), preferred_element_type=F32)


def _dot_tn(a, b):
    return lax.dot_general(a, b, (((0,), (0,)), ((), ())), preferred_element_type=F32)


def _store_rows(ref, val):
    n = val.shape[0]
    for s in range(ROW_SUB):
        ref[pl.ds(s, n, stride=ROW_SUB), :] = val[:, s * LANES:(s + 1) * LANES]


def _load_rows(ref):
    n = ref.shape[0] // ROW_SUB
    return jnp.concatenate([ref[pl.ds(s, n, stride=ROW_SUB), :] for s in range(ROW_SUB)], axis=-1)


def _rows_spec(n, index_map):
    return pl.BlockSpec((n * ROW_SUB, LANES), index_map)


def _row_slice(start, n, sub=ROW_SUB):
    return pl.ds(pl.multiple_of(start * sub, SUBLANES), n * sub)


def _pack_rows(ref, val):
    n = val.shape[0]
    bits = lambda v: lax.bitcast_convert_type(v, jnp.uint32)
    words = (bits(val[:, :D_MODEL // 2]) & jnp.uint32(0xFFFF0000)) | (bits(val[:, D_MODEL // 2:]) >> 16)
    for s in range(PACK_SUB):
        ref[pl.ds(s, n, stride=PACK_SUB), :] = words[:, s * LANES:(s + 1) * LANES]


def _unpack_rows(ref):
    n = ref.shape[0] // PACK_SUB
    words = [ref[pl.ds(s, n, stride=PACK_SUB), :] for s in range(PACK_SUB)]
    as_f32 = lambda w: lax.bitcast_convert_type(w, F32)
    hi = [as_f32(w & jnp.uint32(0xFFFF0000)) for w in words]
    lo = [as_f32(w << 16) for w in words]
    return jnp.concatenate(hi + lo, axis=-1).astype(BF16)


def _split3(x):
    hi = x.astype(BF16)
    r1 = x - hi.astype(F32)
    mid = r1.astype(BF16)
    lo = (r1 - mid.astype(F32)).astype(BF16)
    return hi, mid, lo


def _in_proj_kernel(x_ref, g_ref, win_ref, zxd_ref, q_ref, k_ref, v_ref, w_ref):
    @pl.when(pl.program_id(0) == 0)
    def _():
        w_ref[:OFF_DT, :] = win_ref[:OFF_DT, :].astype(BF16)
        dt_rows = jnp.concatenate([win_ref[OFF_DT:OFF_Q, :], jnp.zeros((LANES - SSD_HEADS, D_MODEL), F32)], axis=0)
        w_ref[OFF_DT:OFF_DT + LANES, :] = dt_rows.astype(BF16)
        w_ref[OFF_DT + LANES:, :] = win_ref[OFF_Q:, :].astype(BF16)

    h = _rms(x_ref[...], g_ref[...], NORM_EPS).astype(BF16)
    q0 = ZXD_WIDTH
    k0, v0 = q0 + ATT_WIDTH, q0 + 2 * ATT_WIDTH
    zxd_ref[...] = _dot_nt(h, w_ref[:q0, :])
    q_ref[...] = (_dot_nt(h, w_ref[q0:k0, :]) * (ATT_HEAD_DIM ** -0.5 * math.log2(math.e))).astype(BF16)
    k_ref[...] = _dot_nt(h, w_ref[k0:v0, :]).astype(BF16)
    v_ref[...] = _dot_nt(h, w_ref[v0:v0 + ATT_WIDTH, :]).astype(BF16)


def _in_proj(x2, g_mix, w_in_t):
    t = x2.shape[0]
    tm = TM_PROJ
    row = lambda n: pl.BlockSpec((tm, n), lambda i: (i, 0))
    full = lambda a: pl.BlockSpec(a.shape, lambda i: (0, 0))
    return pl.pallas_call(
        _in_proj_kernel,
        grid=(t // tm,),
        in_specs=[row(D_MODEL), full(g_mix), full(w_in_t)],
        out_specs=[row(ZXD_WIDTH), row(ATT_WIDTH), row(ATT_WIDTH), row(ATT_WIDTH)],
        out_shape=[
            jax.ShapeDtypeStruct((t, ZXD_WIDTH), F32),
            jax.ShapeDtypeStruct((t, ATT_WIDTH), BF16),
            jax.ShapeDtypeStruct((t, ATT_WIDTH), BF16),
            jax.ShapeDtypeStruct((t, ATT_WIDTH), BF16),
        ],
        scratch_shapes=[pltpu.VMEM((IN_PROJ + LANES - SSD_HEADS, D_MODEL), BF16)],
        compiler_params=_cparams(("arbitrary",)),
        name="in_proj",
    )(x2, g_mix, w_in_t)


def _ssd_chunk_init(c, xpad_ref, state_ref):
    L = SSD_CHUNK

    @pl.when(c == 0)
    def _():
        xpad_ref[0:SUBLANES, :] = jnp.zeros((SUBLANES, SSD_CONV_CH), F32)
        state_ref[...] = jnp.zeros_like(state_ref)

    @pl.when(c != 0)
    def _():
        xpad_ref[0:SUBLANES, :] = xpad_ref[L:L + SUBLANES, :]


def _ssd_chunk(zxd_ref, cp_ref, y_ref, xpad_ref, state_ref):
    L = SSD_CHUNK
    xpad_ref[SUBLANES:SUBLANES + L, :] = zxd_ref[:, OFF_XBC:OFF_DT]

    conv = cp_ref[CP_CONV_B:CP_CONV_B + 1, :]
    for j in range(SSD_CONV):
        off = SUBLANES - (SSD_CONV - 1) + j
        conv = conv + cp_ref[j:j + 1, :] * xpad_ref[off:off + L, :]
    act = conv * jax.nn.sigmoid(conv)
    xs = act[:, :SSD_WIDTH]
    bm = act[:, SSD_WIDTH:SSD_WIDTH + SSD_GROUPS * SSD_STATE].astype(BF16)
    cm = act[:, SSD_WIDTH + SSD_GROUPS * SSD_STATE:].astype(BF16)

    dt_in = zxd_ref[:, OFF_DT:] + cp_ref[CP_SMALL:CP_SMALL + 1, CP_DT_BIAS:CP_DT_BIAS + LANES]
    dt_all = jnp.maximum(dt_in, 0.0) + jnp.log1p(jnp.exp(-jnp.abs(dt_in)))
    adt = dt_all * (-jnp.exp(cp_ref[CP_SMALL:CP_SMALL + 1, CP_A_LOG:CP_A_LOG + LANES]))

    ri = lax.broadcasted_iota(jnp.int32, (L, L), 0)
    ci = lax.broadcasted_iota(jnp.int32, (L, L), 1)
    causal = ci <= ri
    tril = jnp.where(causal, 1.0, 0.0).astype(BF16)
    hi, mid, lo = _split3(adt)
    acum_all = _dot(tril, hi) + _dot(tril, mid) + _dot(tril, lo)
    acum_t = acum_all.T
    a_last = acum_all[L - 1:L, :]
    decay_in_all = jnp.exp(a_last - acum_all)
    decay_out_all = jnp.exp(acum_all)
    chunk_decay_all = jnp.exp(a_last)
    acum = lambda h: acum_all[:, h:h + 1]
    dt = lambda h: dt_all[:, h:h + 1]
    decay_out = lambda h: decay_out_all[:, h:h + 1]
    decay_in = lambda h: decay_in_all[:, h:h + 1]
    chunk_decay = lambda h: chunk_decay_all[:, h:h + 1]

    lane = lax.broadcasted_iota(jnp.int32, (L, LANES), 1)
    lo_half = lane < SSD_HEAD_DIM

    def per_pair(col_a, col_b):
        return jnp.where(lo_half, col_a, col_b)

    ys = []
    for pair in range(SSD_HEADS // 2):
        g = pair // 2
        h0, h1 = 2 * pair, 2 * pair + 1
        cg = cm[:, g * SSD_STATE:(g + 1) * SSD_STATE]
        bg = bm[:, g * SSD_STATE:(g + 1) * SSD_STATE]
        cb = _dot_nt(cg, bg)
        x_pair = xs[:, pair * LANES:(pair + 1) * LANES]
        xdt = x_pair * per_pair(dt(h0), dt(h1))
        y_pair = jnp.zeros((L, LANES), F32)
        for hh, keep in ((h0, lo_half), (h1, jnp.logical_not(lo_half))):
            seg = acum(hh) - acum_t[hh:hh + 1, :]
            lmat = jnp.where(causal, jnp.exp(jnp.where(causal, seg, 0.0)), 0.0)
            m = (cb * lmat).astype(BF16)
            y_pair = y_pair + _dot(m, jnp.where(keep, xdt, 0.0).astype(BF16))
        s_prev = state_ref[pair]
        y_off = _dot(cg, s_prev.astype(BF16)) * per_pair(decay_out(h0), decay_out(h1))
        w_in = (xdt * per_pair(decay_in(h0), decay_in(h1))).astype(BF16)
        cd = jnp.where(lane[0:1, :] < SSD_HEAD_DIM, chunk_decay(h0), chunk_decay(h1))
        state_ref[pair] = s_prev * cd + _dot_tn(bg, w_in)
        ys.append(y_pair + y_off + cp_ref[CP_WIDE:CP_WIDE + 1, pair * LANES:(pair + 1) * LANES] * x_pair)

    y = jnp.concatenate(ys, axis=-1)
    zz = zxd_ref[:, OFF_Z:OFF_XBC]
    y = y * (zz * jax.nn.sigmoid(zz))
    gw = SSD_WIDTH // SSD_GROUPS
    outs = []
    for g in range(SSD_GROUPS):
        yg = y[:, g * gw:(g + 1) * gw]
        outs.append(yg * lax.rsqrt(jnp.mean(yg * yg, axis=-1, keepdims=True) + SSD_NORM_EPS))
    y_ref[...] = (jnp.concatenate(outs, axis=-1) * cp_ref[CP_WIDE:CP_WIDE + 1, SSD_WIDTH:]).astype(BF16)


def _attn_ssd_kernel(q_ref, k_ref, v_ref, sw_ref, zxd_ref, cp_ref, o_ref, y_ref, s_scr, vt_scr, xpad_ref, state_ref):
    bq, bk = ATT_BQ, ATT_BK
    qi = pl.program_id(2)
    n_maps = 2
    _ssd_chunk_init(pl.program_id(1) * pl.num_programs(2) + qi, xpad_ref, state_ref)

    @pl.when(qi == 0)
    def _():
        vt_scr[...] = v_ref[...].astype(F32).T.astype(BF16)

    def fold(t, reduce):
        return reduce(t.reshape(bk // SUBLANES, SUBLANES, bq), axis=0)

    def merge(old, new, op):
        return new if old is None else op(old, new)

    def attend(nk):
        _ssd_chunk(zxd_ref, cp_ref, y_ref, xpad_ref, state_ref)
        q = q_ref[...]
        lane = lax.broadcasted_iota(jnp.int32, (bq, LANES), 1)
        zero = jnp.zeros_like(q)
        q_maps = (jnp.where(lane < ATT_HEAD_DIM, q, zero), jnp.where(lane >= ATT_HEAD_DIM, q, zero))
        key = lax.broadcasted_iota(jnp.int32, (bk, bq), 0)
        qry = lax.broadcasted_iota(jnp.int32, (bk, bq), 1)
        causal = key <= qry

        mt = [None] * n_maps
        for j in range(nk):
            kb = k_ref[j * bk:(j + 1) * bk, :]
            for m in range(n_maps):
                s = _dot_nt(kb, q_maps[m])
                if j == nk - 1:
                    s = jnp.where(causal, s, -jnp.inf)
                s_scr[m, j] = s
                mt[m] = merge(mt[m], fold(s, jnp.max), jnp.maximum)
        q_max = [jnp.max(t, axis=0, keepdims=True) for t in mt]

        lt = [None] * n_maps
        acc = [None] * n_maps
        for j in range(nk):
            vt = vt_scr[:, j * bk:(j + 1) * bk]
            for m in range(n_maps):
                p = jnp.exp2(s_scr[m, j] - q_max[m])
                lt[m] = merge(lt[m], fold(p, jnp.sum), jnp.add)
                acc[m] = merge(acc[m], _dot(vt, p.astype(BF16)), jnp.add)

        lams = [cp_ref[CP_SMALL:CP_SMALL + 1, n * LANES:(n + 1) * LANES] for n in range(4)]
        lam = (jnp.exp(jnp.sum(lams[0] * lams[1], axis=-1, keepdims=True))
               - jnp.exp(jnp.sum(lams[2] * lams[3], axis=-1, keepdims=True)) + LAM_INIT)
        l1 = jnp.sum(lt[0], axis=0, keepdims=True)
        l2 = jnp.sum(lt[1], axis=0, keepdims=True)
        o = acc[0] / l1 - lam * (acc[1] / l2)
        o = o * lax.rsqrt(jnp.mean(o * o, axis=0, keepdims=True) + SUBLN_EPS) * sw_ref[...]
        o_ref[...] = (o * (1.0 - LAM_INIT)).T.astype(BF16)

    for nk in range(1, k_ref.shape[0] // bk + 1):
        pl.when(qi == nk - 1)(lambda nk=nk: attend(nk))


def _attn_ssd(q, k, v, lam_q1, lam_k1, lam_q2, lam_k2, subln_w, zxd, conv_w, conv_b, dt_bias, a_log, d_skip,
              norm_w, bsz, seq):
    nq = seq // ATT_BQ
    nc = seq // SSD_CHUNK
    assert nc == ATT_HEADS * nq
    qspec = pl.BlockSpec((ATT_BQ, LANES), lambda b, h, i: (b * nq + i, h))
    kvspec = pl.BlockSpec((seq, LANES), lambda b, h, i: (b, h))
    full = lambda a: pl.BlockSpec(a.shape, lambda b, h, i: (0, 0))
    chunk = lambda n: pl.BlockSpec((SSD_CHUNK, n), lambda b, h, i: (b * nc + h * nq + i, 0))
    sw = subln_w.reshape(-1, 1)
    pad_to = lambda v, n: jnp.pad(v.reshape(1, -1), ((0, 0), (0, n - v.size)))
    wide = jnp.concatenate([jnp.repeat(d_skip, SSD_HEAD_DIM), norm_w]).reshape(1, SSD_CONV_CH)
    small = jnp.concatenate([pad_to(a, LANES) for a in (lam_q1, lam_k1, lam_q2, lam_k2, dt_bias, a_log)], axis=1)
    consts = jnp.concatenate([conv_w, conv_b.reshape(1, -1), wide, pad_to(small, SSD_CONV_CH),
                              jnp.zeros((SUBLANES - CP_SMALL - 1, SSD_CONV_CH), F32)], axis=0)
    return pl.pallas_call(
        _attn_ssd_kernel,
        grid=(bsz, ATT_HEADS, nq),
        in_specs=[qspec, kvspec, kvspec, full(sw), chunk(ZXD_WIDTH), full(consts)],
        out_specs=[qspec, chunk(SSD_WIDTH)],
        out_shape=[jax.ShapeDtypeStruct((bsz * seq, ATT_WIDTH), BF16),
                   jax.ShapeDtypeStruct((bsz * seq, SSD_WIDTH), BF16)],
        scratch_shapes=[pltpu.VMEM((2, seq // ATT_BK, ATT_BK, ATT_BQ), F32),
                        pltpu.VMEM((LANES, seq), BF16),
                        pltpu.VMEM((SSD_CHUNK + 2 * SUBLANES, SSD_CONV_CH), F32),
                        pltpu.VMEM((SSD_HEADS // 2, SSD_STATE, LANES), F32)],
        compiler_params=_cparams(("arbitrary", "arbitrary", "arbitrary")),
        name="attn_ssd",
    )(q, k, v, sw, zxd, consts)


def _mixer(x2, g_mix, w_in, conv_w, conv_b, dt_bias, a_log, d_skip, ssd_norm_w,
           lam_q1, lam_k1, lam_q2, lam_k2, subln_w, bsz, seq):
    zxd, q, k, v = _in_proj(x2, g_mix.reshape(1, -1), w_in.T)
    y_att, y_ssd = _attn_ssd(q, k, v, lam_q1, lam_k1, lam_q2, lam_k2, subln_w, zxd, conv_w, conv_b, dt_bias,
                             a_log, d_skip, ssd_norm_w, bsz, seq)
    return y_ssd, y_att


def _out_proj_kernel(x_ref, ys_ref, ya_ref, wo_ref, g_ref, wr2_ref, br_ref,
                     x1_ref, xn_ref, route_ref, gate_ref, cnt_ref):
    tm = TM_ROUTE
    x1 = x_ref[...] + _dot(jnp.concatenate([ys_ref[...], ya_ref[...]], axis=-1), wo_ref[...])
    x1_ref[...] = x1
    xn_all = _rms(x1, g_ref[...], NORM_EPS)
    xn_b = xn_all.astype(BF16)
    xn_ref[...] = xn_b

    n_tok = OUT_SUB * tm
    half = n_tok // 2
    lg2 = jnp.concatenate([_dot(xn_b[:half], wr2_ref[...]), _dot(xn_b[half:], wr2_ref[...])], axis=0)
    logits = (lg2[:, :LANES] + lg2[:, LANES:]).T[:N_EXPERTS, :] + br_ref[...]

    eidx = lax.broadcasted_iota(jnp.int32, (N_EXPERTS, n_tok), 0).astype(F32)
    work = logits
    vals, idxs, hots = [], [], []
    for _ in range(TOP_K):
        m = jnp.max(work, axis=0, keepdims=True)
        idx = jnp.min(jnp.where(work == m, eidx, float(N_EXPERTS)), axis=0, keepdims=True)
        hot = eidx == idx
        vals.append(m)
        idxs.append(idx.astype(jnp.int32))
        hots.append(hot)
        work = jnp.where(hot, -jnp.inf, work)
    exps = [jnp.exp(v - vals[0]) for v in vals]
    denom = exps[0] + exps[1] + exps[2] + exps[3]
    gates = [e / denom for e in exps]

    cnt = jnp.zeros((N_EXPERTS, n_tok), F32)
    for hot in hots:
        cnt = cnt + jnp.where(hot, 1.0, 0.0)
    cnt_b = cnt.astype(BF16)
    r = lax.broadcasted_iota(jnp.int32, (tm, tm), 0)
    c = lax.broadcasted_iota(jnp.int32, (tm, tm), 1)
    earlier_tok = jnp.where(r < c, 1.0, 0.0).astype(BF16)
    er = lax.broadcasted_iota(jnp.int32, (N_EXPERTS, N_EXPERTS), 0)
    ec = lax.broadcasted_iota(jnp.int32, (N_EXPERTS, N_EXPERTS), 1)
    lower_exp = jnp.where(ec < er, 1.0, 0.0).astype(BF16)
    cnt_pad = jnp.concatenate([cnt_b, jnp.zeros((LANES - N_EXPERTS, n_tok), BF16)], axis=0)
    pos = []
    for sub in range(OUT_SUB):
        cols = slice(sub * tm, (sub + 1) * tm)
        run = jnp.sum(cnt[:, cols], axis=1, keepdims=True)
        run_even = 2.0 * jnp.floor(0.5 * run + 0.5)
        run_start = _dot(lower_exp, jnp.broadcast_to(run_even, (N_EXPERTS, LANES)).astype(BF16))[:, 0:1]
        pos.append(_dot(cnt_b[:, cols], earlier_tok) + run_start)
        cnt_ref[sub] = _dot_nt(jnp.ones((SUBLANES, tm), BF16), cnt_pad[:, cols])
    pos = jnp.concatenate(pos, axis=1)
    lps = [jnp.sum(jnp.where(hot, pos, 0.0), axis=0, keepdims=True).astype(jnp.int32) for hot in hots]

    route_ref[...] = jnp.concatenate(idxs + lps, axis=0)
    gate_ref[...] = jnp.concatenate(gates + [jnp.zeros((SUBLANES - TOP_K, n_tok), F32)], axis=0)


def _out_proj(x2, y_ssd, y_att, w_out, g_ffn, w_router, b_router):
    t = x2.shape[0]
    tm = OUT_SUB * TM_ROUTE
    nt = t // TM_ROUTE
    wo = w_out.astype(BF16)
    wrh, wrm, _ = _split3(jnp.pad(w_router, ((0, 0), (0, LANES - N_EXPERTS))))
    wr2 = jnp.concatenate([wrh, wrm], axis=1)
    br = b_router.reshape(N_EXPERTS, 1)
    row = lambda n: pl.BlockSpec((tm, n), lambda i: (i, 0))
    col = pl.BlockSpec((SUBLANES, tm), lambda i: (0, i))
    full = lambda a: pl.BlockSpec(a.shape, lambda i: (0, 0))
    args = (x2, y_ssd, y_att, wo, g_ffn.reshape(1, -1), wr2, br)
    return pl.pallas_call(
        _out_proj_kernel,
        grid=(t // tm,),
        in_specs=[row(D_MODEL), row(SSD_WIDTH), row(ATT_WIDTH)] + [full(a) for a in args[3:]],
        out_specs=[row(D_MODEL), row(D_MODEL), col, col,
                   pl.BlockSpec((OUT_SUB, SUBLANES, LANES), lambda i: (i, 0, 0))],
        out_shape=[
            jax.ShapeDtypeStruct((t, D_MODEL), F32),
            jax.ShapeDtypeStruct((t, D_MODEL), BF16),
            jax.ShapeDtypeStruct((SUBLANES, t), jnp.int32),
            jax.ShapeDtypeStruct((SUBLANES, t), F32),
            jax.ShapeDtypeStruct((nt, SUBLANES, LANES), F32),
        ],
        compiler_params=_cparams(("arbitrary",)),
        name="out_proj",
    )(*args)


def _copy_run(n, start_copy):
    @pl.when(n > 0)
    def _():
        start_copy(n)


def _select_by_position(positions, values, n_pos):
    tm = positions[0].shape[1]
    r = lax.broadcasted_iota(jnp.int32, (n_pos, tm), 0)
    out = jnp.zeros((n_pos, tm), F32)
    for k in reversed(range(TOP_K)):
        out = jnp.where(r == positions[k], values[k], out)
    return out


def _dispatch_kernel(cnt_ref, ls_ref, base_ref, zoff_ref, rows_ref, xn_ref, route_ref, xs_hbm, xloc, zeros_vmem,
                     sem_z, sems):
    n_pos = N_POS
    i = pl.program_id(0)
    n_steps = pl.num_programs(0)
    slot = lax.rem(i, 2)

    def zero_fill(op):
        def pad_rows(e, _):
            n = zoff_ref[N_EXPERTS + 1 + e]

            @pl.when(n > 0)
            def _():
                op(pltpu.make_async_copy(zeros_vmem.at[_row_slice(0, n, PACK_SUB)],
                                         xs_hbm.at[_row_slice(zoff_ref[e], n, PACK_SUB)], sem_z))
            return 0

        def unused_block(b, _):
            op(pltpu.make_async_copy(zeros_vmem, xs_hbm.at[_row_slice(b * ROW_BLK, ROW_BLK, PACK_SUB)], sem_z))
            return 0

        lax.fori_loop(0, N_EXPERTS, pad_rows, 0)
        lax.fori_loop(zoff_ref[N_EXPERTS], xs_hbm.shape[0] // (ROW_BLK * PACK_SUB), unused_block, 0)

    @pl.when(i == 0)
    def _():
        zeros_vmem[...] = jnp.zeros_like(zeros_vmem)
        zero_fill(lambda cp: cp.start())

    def slot_wait(s, step):
        n = rows_ref[step]
        pltpu.make_async_copy(xloc.at[s, _row_slice(0, n, PACK_SUB)], xs_hbm.at[_row_slice(0, n, PACK_SUB)],
                              sems.at[s]).wait()

    @pl.when(i >= 2)
    def _():
        slot_wait(slot, i - 2)

    positions = [route_ref[TOP_K + k:TOP_K + k + 1, :] for k in range(TOP_K)]
    sel = _select_by_position(positions, [1.0] * TOP_K, n_pos).astype(BF16)
    rows = _dot(sel, xn_ref[...])
    _pack_rows(xloc.at[slot], rows)

    def per_expert(e, _):
        idx = i * N_EXPERTS + e
        src0 = ls_ref[idx]
        dst0 = base_ref[idx]

        def start_copy(n):
            pltpu.make_async_copy(xloc.at[slot, _row_slice(src0, n, PACK_SUB)],
                                  xs_hbm.at[_row_slice(dst0, n, PACK_SUB)], sems.at[slot]).start()

        _copy_run(cnt_ref[idx], start_copy)
        return 0

    lax.fori_loop(0, N_EXPERTS, per_expert, 0, unroll=RUN_UNROLL)

    @pl.when(i == n_steps - 1)
    def _():
        slot_wait(slot, i)

        @pl.when(n_steps > 1)
        def _():
            slot_wait(1 - slot, i - 1)

        zero_fill(lambda cp: cp.wait())


def _dispatch(xn, route, tables, n_rows):
    t = xn.shape[0]
    tm = TM_ROUTE
    cnt_tbl, ls_tbl, base_tbl, zoff, tile_rows = tables
    return pl.pallas_call(
        _dispatch_kernel,
        grid_spec=pltpu.PrefetchScalarGridSpec(
            num_scalar_prefetch=5,
            grid=(t // tm,),
            in_specs=[pl.BlockSpec((tm, D_MODEL), lambda i, *_: (i, 0)),
                      pl.BlockSpec((SUBLANES, tm), lambda i, *_: (0, i))],
            out_specs=pl.BlockSpec(memory_space=pl.ANY),
            scratch_shapes=[
                pltpu.VMEM((2, N_POS * PACK_SUB, LANES), jnp.uint32),
                pltpu.VMEM((ROW_BLK * PACK_SUB, LANES), jnp.uint32),
                pltpu.SemaphoreType.DMA,
                pltpu.SemaphoreType.DMA((2,)),
            ],
        ),
        out_shape=jax.ShapeDtypeStruct(((n_rows + ROW_BLK) * PACK_SUB, LANES), jnp.uint32),
        compiler_params=_cparams(("arbitrary",)),
        name="dispatch",
    )(cnt_tbl, ls_tbl, base_tbl, zoff, tile_rows, xn, route)


def _experts_kernel(blk_e_ref, nvalid_ref, first_ref, wslot_ref, enext_ref, pieces_ref, xs_ref, wup_hbm, wdn_hbm,
                    bgu_ref, bd_ref, ys_ref, wup_buf, wdn_buf, wg_s, wu_s, wd_s, sems):
    i = pl.program_id(0)
    slot = wslot_ref[i]

    def weight_copies(e, s):
        return (pltpu.make_async_copy(wup_hbm.at[e], wup_buf.at[s], sems.at[0, s]),
                pltpu.make_async_copy(wdn_hbm.at[e], wdn_buf.at[s], sems.at[1, s]))

    @pl.when(i == 0)
    def _():
        for cp in weight_copies(blk_e_ref[0], slot):
            cp.start()

    @pl.when(first_ref[i] != 0)
    def _():
        for cp in weight_copies(blk_e_ref[i], slot):
            cp.wait()

        @pl.when(enext_ref[i] >= 0)
        def _():
            for cp in weight_copies(enext_ref[i], 1 - slot):
                cp.start()

        src = lax.broadcasted_iota(jnp.int32, (DEINT, DEINT), 0)
        dst = lax.broadcasted_iota(jnp.int32, (DEINT, DEINT), 1)
        perm = jnp.where(src == jnp.where(dst < LANES, 2 * dst, 2 * (dst - LANES) + 1), 1.0, 0.0).astype(BF16)
        for g in range(2 * D_EXPERT // DEINT):
            sep = _dot(wup_buf[slot, :, g * DEINT:(g + 1) * DEINT].astype(BF16), perm)
            wg_s[:, g * LANES:(g + 1) * LANES] = sep[:, :LANES].astype(BF16)
            wu_s[:, g * LANES:(g + 1) * LANES] = sep[:, LANES:].astype(BF16)
        wd_s[...] = wdn_buf[slot].astype(BF16)

    def ffn(n_rows):
        used = n_rows * ROW_SUB
        if n_rows:
            xb = _unpack_rows(xs_ref.at[pl.ds(0, n_rows * PACK_SUB)])
            gate = jnp.minimum(_dot(xb, wg_s[...]) + bgu_ref[0, 0:1, :], SWIGLU_LIMIT)
            up = jnp.clip(_dot(xb, wu_s[...]) + bgu_ref[0, 1:2, :], -SWIGLU_LIMIT, SWIGLU_LIMIT)
            act = (up + 1.0) * gate * jax.nn.sigmoid(SWIGLU_ALPHA * gate)
            _store_rows(ys_ref.at[pl.ds(0, used)], _dot(act.astype(BF16), wd_s[...]) + bd_ref[0])
        if n_rows < ROW_BLK:
            ys_ref[used:, :] = jnp.zeros((ROW_BLK * ROW_SUB - used, LANES), F32)

    for pieces in range(ROW_BLK // ROW_PIECE + 1):
        pl.when(pieces_ref[i] == pieces)(lambda pieces=pieces: ffn(pieces * ROW_PIECE))


def _experts(xs, plan, w_up, b_up, w_down, b_down, n_rows):
    nb = n_rows // ROW_BLK
    b_gu = b_up.reshape(N_EXPERTS, D_EXPERT, 2).transpose(0, 2, 1)
    bd = b_down.reshape(N_EXPERTS, 1, D_MODEL)
    blk_e, nvalid, first, wslot, enext, pieces = plan
    src = lambda i, be, nv, *_: (jnp.minimum(i, nv[0] - 1), 0)
    bspec = lambda r, m: pl.BlockSpec((1, r, m), lambda i, be, *_: (be[i], 0, 0))
    anyspec = pl.BlockSpec(memory_space=pl.ANY)
    return pl.pallas_call(
        _experts_kernel,
        grid_spec=pltpu.PrefetchScalarGridSpec(
            num_scalar_prefetch=6,
            grid=(nb,),
            in_specs=[pl.BlockSpec((ROW_BLK * PACK_SUB, LANES), src), anyspec, anyspec,
                      bspec(2, D_EXPERT), bspec(1, D_MODEL)],
            out_specs=_rows_spec(ROW_BLK, lambda i, *_: (i, 0)),
            scratch_shapes=[
                pltpu.VMEM((2, D_MODEL, 2 * D_EXPERT), F32),
                pltpu.VMEM((2, D_EXPERT, D_MODEL), F32),
                pltpu.VMEM((D_MODEL, D_EXPERT), BF16),
                pltpu.VMEM((D_MODEL, D_EXPERT), BF16),
                pltpu.VMEM((D_EXPERT, D_MODEL), BF16),
                pltpu.SemaphoreType.DMA((2, 2)),
            ],
        ),
        out_shape=jax.ShapeDtypeStruct((n_rows * ROW_SUB, LANES), F32),
        compiler_params=_cparams(("arbitrary",)),
        name="experts",
    )(blk_e, nvalid, first, wslot, enext, pieces, xs, w_up, w_down, b_gu, bd)


def _combine_kernel(cnt_ref, ls_ref, base_ref, rows_ref, ys_hbm, route_ref, gate_ref, x1_ref, p_ref, gp_ref, wpg_ref,
                    wpp_ref, gf_ref, o_ref, yloc, sems):
    tm = TM_ROUTE
    n_pos = N_POS
    i = pl.program_id(0)
    n_steps = pl.num_programs(0)
    slot = lax.rem(i, 2)

    def gather_step(step, s):
        for u in range(COMB_SUB):
            def per_expert(e, _, u=u):
                idx = (step * COMB_SUB + u) * N_EXPERTS + e
                src0 = base_ref[idx]
                dst0 = ls_ref[idx]

                def start_copy(n):
                    pltpu.make_async_copy(ys_hbm.at[_row_slice(src0, n)],
                                          yloc.at[s, u, _row_slice(dst0, n)], sems.at[s]).start()

                _copy_run(cnt_ref[idx], start_copy)
                return 0

            lax.fori_loop(0, N_EXPERTS, per_expert, 0, unroll=RUN_UNROLL)

    @pl.when(i == 0)
    def _():
        yloc[...] = jnp.zeros_like(yloc)
        gather_step(0, 0)

    @pl.when(i + 1 < n_steps)
    def _():
        gather_step(i + 1, 1 - slot)

    for u in range(COMB_SUB):
        n = rows_ref[i * COMB_SUB + u]
        pltpu.make_async_copy(ys_hbm.at[_row_slice(0, n)], yloc.at[slot, u, _row_slice(0, n)], sems.at[slot]).wait()

    moe = []
    for u in range(COMB_SUB):
        cols = slice(u * tm, (u + 1) * tm)
        positions = [route_ref[TOP_K + k:TOP_K + k + 1, cols] for k in range(TOP_K)]
        gsel = _select_by_position(positions, [gate_ref[k:k + 1, cols] for k in range(TOP_K)], n_pos)
        g_pos = jnp.sum(gsel, axis=1, keepdims=True)
        sel = jnp.where(gsel != 0.0, 1.0, 0.0).astype(BF16)
        y_gated = (_load_rows(yloc.at[slot, u]) * g_pos).astype(BF16)
        moe.append(_dot_tn(sel, y_gated))
    x2 = x1_ref[...] + jnp.concatenate(moe, axis=0)
    xn = _rms(x2, gp_ref[...], NORM_EPS).astype(BF16)
    pp = _dot(p_ref[...].astype(BF16), wpp_ref[...])
    x3 = x2 + pp * jax.nn.sigmoid(_dot(xn, wpg_ref[...]))
    o_ref[...] = _rms(x3, gf_ref[...], NORM_EPS)


def _combine(ys, route, gate_t, x1, p2, tables, g_ple, w_ple_gate, w_ple_proj, g_final):
    t = x1.shape[0]
    tm = COMB_SUB * TM_ROUTE
    cnt_tbl, ls_tbl, base_tbl, _, tile_rows = tables
    row = lambda n: pl.BlockSpec((tm, n), lambda i, *_: (i, 0))
    col = pl.BlockSpec((SUBLANES, tm), lambda i, *_: (0, i))
    full = lambda a: pl.BlockSpec(a.shape, lambda i, *_: (0, 0))
    consts = (g_ple.reshape(1, -1), w_ple_gate.astype(BF16), w_ple_proj.astype(BF16), g_final.reshape(1, -1))
    return pl.pallas_call(
        _combine_kernel,
        grid_spec=pltpu.PrefetchScalarGridSpec(
            num_scalar_prefetch=4,
            grid=(t // tm,),
            in_specs=[pl.BlockSpec(memory_space=pl.ANY), col, col, row(D_MODEL), row(PLE_DIM)]
                     + [full(a) for a in consts],
            out_specs=row(D_MODEL),
            scratch_shapes=[
                pltpu.VMEM((2, COMB_SUB, N_POS * ROW_SUB, LANES), F32),
                pltpu.SemaphoreType.DMA((2,)),
            ],
        ),
        out_shape=jax.ShapeDtypeStruct((t, D_MODEL), F32),
        compiler_params=_cparams(("arbitrary",)),
        name="combine",
    )(cnt_tbl, ls_tbl, base_tbl, tile_rows, ys, route, gate_t, x1, p2, *consts)


def _routing_tables(cnt, n_blocks):
    tile_cnt = cnt[:, 0, :N_EXPERTS].astype(jnp.int32)
    tile_cnt = tile_cnt + tile_cnt % 2
    counts = jnp.sum(tile_cnt, axis=0)
    padded = (counts + ROW_BLK - 1) // ROW_BLK * ROW_BLK
    pend = jnp.cumsum(padded)
    pstart = pend - padded
    base = pstart[None, :] + jnp.cumsum(tile_cnt, axis=0) - tile_cnt
    lstart = jnp.cumsum(tile_cnt, axis=1) - tile_cnt
    nvalid = pend[-1:] // ROW_BLK
    zoff = jnp.concatenate([pstart + counts, nvalid, padded - counts])
    blk_start = jnp.minimum(jnp.arange(n_blocks, dtype=jnp.int32) * ROW_BLK, pend[-1] - 1)
    blk_e = jnp.minimum(jnp.sum((pend[None, :] <= blk_start[:, None]).astype(jnp.int32), axis=1), N_EXPERTS - 1)
    first = jnp.concatenate([jnp.ones((1,), bool), blk_e[1:] != blk_e[:-1]])
    wslot = (jnp.cumsum(first.astype(jnp.int32)) - 1) % 2
    eids = jnp.arange(N_EXPERTS, dtype=jnp.int32)
    later_nonempty = (eids[None, :] > eids[:, None]) & (padded[None, :] > 0)
    next_e = jnp.min(jnp.where(later_nonempty, eids[None, :], N_EXPERTS), axis=1)
    next_e = jnp.where(next_e == N_EXPERTS, -1, next_e)
    blk_hot = blk_e[:, None] == eids[None, :]
    per_block = lambda v: jnp.sum(jnp.where(blk_hot, v[None, :], 0), axis=1)
    blk_ids = jnp.arange(n_blocks, dtype=jnp.int32)
    real_rows = jnp.clip(per_block(pstart + counts) - blk_ids * ROW_BLK, 0, ROW_BLK) * (blk_ids < nvalid[0])
    pieces = (real_rows + ROW_PIECE - 1) // ROW_PIECE
    tile_rows = jnp.sum(tile_cnt, axis=1)
    i32 = lambda a: a.reshape(-1).astype(jnp.int32)
    plan = (i32(blk_e), i32(nvalid), i32(first), i32(wslot), i32(per_block(next_e)), i32(pieces))
    return (i32(tile_cnt), i32(lstart), i32(base), i32(zoff), i32(tile_rows)), plan


def kernel(x, p, g_mix, w_in, conv_w, conv_b, dt_bias, a_log, d_skip, ssd_norm_w, lam_q1, lam_k1, lam_q2, lam_k2, subln_w, w_out, g_ffn, w_router, b_router, w_up, b_up, w_down, b_down, g_ple, w_ple_gate, w_ple_proj, g_final):
    bsz, seq, d = x.shape
    t = bsz * seq
    x2 = x.reshape(t, d)
    y_ssd, y_att = _mixer(x2, g_mix[0], w_in[0], conv_w[0], conv_b[0], dt_bias[0], a_log[0], d_skip[0], ssd_norm_w[0],
                          lam_q1[0], lam_k1[0], lam_q2[0], lam_k2[0], subln_w[0], bsz, seq)
    x1, xn, route, gate_t, cnt = _out_proj(x2, y_ssd, y_att, w_out[0], g_ffn[0], w_router[0], b_router[0])

    n_rows = t * TOP_K + (t // TM_ROUTE) * N_EXPERTS + N_EXPERTS * ROW_BLK
    n_rows = -(-n_rows // ROW_BLK) * ROW_BLK
    tables, plan = _routing_tables(cnt, n_rows // ROW_BLK)
    xs = _dispatch(xn, route, tables, n_rows)
    ys = _experts(xs, plan, w_up[0], b_up[0], w_down[0], b_down[0], n_rows)
    out = _combine(ys, route, gate_t, x1, p[0].reshape(t, PLE_DIM), tables, g_ple[0], w_ple_gate[0], w_ple_proj[0],
                   g_final)
    return out.reshape(bsz, seq, d)
```

```python
import math

import jax
import jax.numpy as jnp
from jax import lax
from jax.experimental import pallas as pl
from jax.experimental.pallas import tpu as pltpu

F32 = jnp.float32
BF16 = jnp.bfloat16

D_MODEL = 1024
PLE_DIM = 256
SSD_WIDTH = 512
ATT_WIDTH = 512
SSD_HEAD_DIM = 64
SSD_HEADS = 8
SSD_GROUPS = 2
SSD_STATE = 128
SSD_CONV = 4
SSD_CHUNK = 128
SSD_CONV_CH = SSD_WIDTH + 2 * SSD_GROUPS * SSD_STATE
SSD_NORM_EPS = 1e-5
ATT_HEAD_DIM = 64
ATT_HEADS = 4
SUBLN_EPS = 1e-5
OFF_Z = 0
OFF_XBC = OFF_Z + SSD_WIDTH
OFF_DT = OFF_XBC + SSD_CONV_CH
OFF_Q = OFF_DT + SSD_HEADS
OFF_K = OFF_Q + ATT_WIDTH
OFF_V = OFF_K + ATT_WIDTH
IN_PROJ = OFF_V + ATT_WIDTH
N_EXPERTS = 32
TOP_K = 4
D_EXPERT = 1024
SWIGLU_LIMIT = 7.0
SWIGLU_ALPHA = 1.702
NORM_EPS = 1e-6
LAM_INIT = 0.8 - 0.6 * math.exp(-0.3 * 0)

LANES = 128
SUBLANES = 8
VMEM_LIMIT_BYTES = 56 * 1024 * 1024
ROW_SUB = D_MODEL // LANES
PACK_SUB = ROW_SUB // 2

TM_PROJ = 512
ATT_BQ = 512
ATT_BK = 512
ROW_BLK = 512
ROW_PIECE = 128
TM_ROUTE = 256
N_POS = TOP_K * TM_ROUTE + N_EXPERTS
OUT_SUB = 4
COMB_SUB = 2
RUN_UNROLL = 4
DEINT = 2 * LANES
ZXD_WIDTH = OFF_DT + LANES
CP_CONV_B = SSD_CONV
CP_WIDE = SSD_CONV + 1
CP_SMALL = SSD_CONV + 2
CP_DT_BIAS = 4 * LANES
CP_A_LOG = 5 * LANES


def _cparams(sem):
    return pltpu.CompilerParams(dimension_semantics=sem, vmem_limit_bytes=VMEM_LIMIT_BYTES)


def _rms(x, w, eps):
    return x * lax.rsqrt(jnp.mean(x * x, axis=-1, keepdims=True) + eps) * w


def _dot(a, b):
    return jnp.dot(a, b, preferred_element_type=F32)


def _dot_nt(a, b):
    return lax.dot_general(a, b, (((1,), (1,)), ((), ())), preferred_element_type=F32)


def _dot_tn(a, b):
    return lax.dot_general(a, b, (((0,), (0,)), ((), ())), preferred_element_type=F32)


def _store_rows(ref, val):
    n = val.shape[0]
    for s in range(ROW_SUB):
        ref[pl.ds(s, n, stride=ROW_SUB), :] = val[:, s * LANES:(s + 1) * LANES]


def _load_rows(ref):
    n = ref.shape[0] // ROW_SUB
    return jnp.concatenate([ref[pl.ds(s, n, stride=ROW_SUB), :] for s in range(ROW_SUB)], axis=-1)


def _rows_spec(n, index_map):
    return pl.BlockSpec((n * ROW_SUB, LANES), index_map)


def _row_slice(start, n, sub=ROW_SUB):
    return pl.ds(pl.multiple_of(start * sub, SUBLANES), n * sub)


def _pack_rows(ref, val):
    n = val.shape[0]
    bits = lambda v: lax.bitcast_convert_type(v, jnp.uint32)
    words = (bits(val[:, :D_MODEL // 2]) & jnp.uint32(0xFFFF0000)) | (bits(val[:, D_MODEL // 2:]) >> 16)
    for s in range(PACK_SUB):
        ref[pl.ds(s, n, stride=PACK_SUB), :] = words[:, s * LANES:(s + 1) * LANES]


def _unpack_rows(ref):
    n = ref.shape[0] // PACK_SUB
    words = [ref[pl.ds(s, n, stride=PACK_SUB), :] for s in range(PACK_SUB)]
    as_f32 = lambda w: lax.bitcast_convert_type(w, F32)
    hi = [as_f32(w & jnp.uint32(0xFFFF0000)) for w in words]
    lo = [as_f32(w << 16) for w in words]
    return jnp.concatenate(hi + lo, axis=-1).astype(BF16)


def _split3(x):
    hi = x.astype(BF16)
    r1 = x - hi.astype(F32)
    mid = r1.astype(BF16)
    lo = (r1 - mid.astype(F32)).astype(BF16)
    return hi, mid, lo


def _in_proj_kernel(x_ref, g_ref, win_ref, zxd_ref, q_ref, k_ref, v_ref, w_ref):
    @pl.when(pl.program_id(0) == 0)
    def _():
        w_ref[:OFF_DT, :] = win_ref[:OFF_DT, :].astype(BF16)
        dt_rows = jnp.concatenate([win_ref[OFF_DT:OFF_Q, :], jnp.zeros((LANES - SSD_HEADS, D_MODEL), F32)], axis=0)
        w_ref[OFF_DT:OFF_DT + LANES, :] = dt_rows.astype(BF16)
        w_ref[OFF_DT + LANES:, :] = win_ref[OFF_Q:, :].astype(BF16)

    h = _rms(x_ref[...], g_ref[...], NORM_EPS).astype(BF16)
    q0 = ZXD_WIDTH
    k0, v0 = q0 + ATT_WIDTH, q0 + 2 * ATT_WIDTH
    zxd_ref[...] = _dot_nt(h, w_ref[:q0, :])
    q_ref[...] = (_dot_nt(h, w_ref[q0:k0, :]) * (ATT_HEAD_DIM ** -0.5 * math.log2(math.e))).astype(BF16)
    k_ref[...] = _dot_nt(h, w_ref[k0:v0, :]).astype(BF16)
    v_ref[...] = _dot_nt(h, w_ref[v0:v0 + ATT_WIDTH, :]).astype(BF16)


def _in_proj(x2, g_mix, w_in_t):
    t = x2.shape[0]
    tm = TM_PROJ
    row = lambda n: pl.BlockSpec((tm, n), lambda i: (i, 0))
    full = lambda a: pl.BlockSpec(a.shape, lambda i: (0, 0))
    return pl.pallas_call(
        _in_proj_kernel,
        grid=(t // tm,),
        in_specs=[row(D_MODEL), full(g_mix), full(w_in_t)],
        out_specs=[row(ZXD_WIDTH), row(ATT_WIDTH), row(ATT_WIDTH), row(ATT_WIDTH)],
        out_shape=[
            jax.ShapeDtypeStruct((t, ZXD_WIDTH), F32),
            jax.ShapeDtypeStruct((t, ATT_WIDTH), BF16),
            jax.ShapeDtypeStruct((t, ATT_WIDTH), BF16),
            jax.ShapeDtypeStruct((t, ATT_WIDTH), BF16),
        ],
        scratch_shapes=[pltpu.VMEM((IN_PROJ + LANES - SSD_HEADS, D_MODEL), BF16)],
        compiler_params=_cparams(("arbitrary",)),
        name="in_proj",
    )(x2, g_mix, w_in_t)


def _ssd_chunk_init(c, xpad_ref, state_ref):
    L = SSD_CHUNK

    @pl.when(c == 0)
    def _():
        xpad_ref[0:SUBLANES, :] = jnp.zeros((SUBLANES, SSD_CONV_CH), F32)
        state_ref[...] = jnp.zeros_like(state_ref)

    @pl.when(c != 0)
    def _():
        xpad_ref[0:SUBLANES, :] = xpad_ref[L:L + SUBLANES, :]


def _ssd_chunk(zxd_ref, cp_ref, y_ref, xpad_ref, state_ref):
    L = SSD_CHUNK
    xpad_ref[SUBLANES:SUBLANES + L, :] = zxd_ref[:, OFF_XBC:OFF_DT]

    conv = cp_ref[CP_CONV_B:CP_CONV_B + 1, :]
    for j in range(SSD_CONV):
        off = SUBLANES - (SSD_CONV - 1) + j
        conv = conv + cp_ref[j:j + 1, :] * xpad_ref[off:off + L, :]
    act = conv * jax.nn.sigmoid(conv)
    xs = act[:, :SSD_WIDTH]
    bm = act[:, SSD_WIDTH:SSD_WIDTH + SSD_GROUPS * SSD_STATE].astype(BF16)
    cm = act[:, SSD_WIDTH + SSD_GROUPS * SSD_STATE:].astype(BF16)

    dt_in = zxd_ref[:, OFF_DT:] + cp_ref[CP_SMALL:CP_SMALL + 1, CP_DT_BIAS:CP_DT_BIAS + LANES]
    dt_all = jnp.maximum(dt_in, 0.0) + jnp.log1p(jnp.exp(-jnp.abs(dt_in)))
    adt = dt_all * (-jnp.exp(cp_ref[CP_SMALL:CP_SMALL + 1, CP_A_LOG:CP_A_LOG + LANES]))

    ri = lax.broadcasted_iota(jnp.int32, (L, L), 0)
    ci = lax.broadcasted_iota(jnp.int32, (L, L), 1)
    causal = ci <= ri
    tril = jnp.where(causal, 1.0, 0.0).astype(BF16)
    hi, mid, lo = _split3(adt)
    acum_all = _dot(tril, hi) + _dot(tril, mid) + _dot(tril, lo)
    acum_t = acum_all.T
    a_last = acum_all[L - 1:L, :]
    decay_in_all = jnp.exp(a_last - acum_all)
    decay_out_all = jnp.exp(acum_all)
    chunk_decay_all = jnp.exp(a_last)
    acum = lambda h: acum_all[:, h:h + 1]
    dt = lambda h: dt_all[:, h:h + 1]
    decay_out = lambda h: decay_out_all[:, h:h + 1]
    decay_in = lambda h: decay_in_all[:, h:h + 1]
    chunk_decay = lambda h: chunk_decay_all[:, h:h + 1]

    lane = lax.broadcasted_iota(jnp.int32, (L, LANES), 1)
    lo_half = lane < SSD_HEAD_DIM

    def per_pair(col_a, col_b):
        return jnp.where(lo_half, col_a, col_b)

    ys = []
    for pair in range(SSD_HEADS // 2):
        g = pair // 2
        h0, h1 = 2 * pair, 2 * pair + 1
        cg = cm[:, g * SSD_STATE:(g + 1) * SSD_STATE]
        bg = bm[:, g * SSD_STATE:(g + 1) * SSD_STATE]
        cb = _dot_nt(cg, bg)
        x_pair = xs[:, pair * LANES:(pair + 1) * LANES]
        xdt = x_pair * per_pair(dt(h0), dt(h1))
        y_pair = jnp.zeros((L, LANES), F32)
        for hh, keep in ((h0, lo_half), (h1, jnp.logical_not(lo_half))):
            seg = acum(hh) - acum_t[hh:hh + 1, :]
            lmat = jnp.where(causal, jnp.exp(jnp.where(causal, seg, 0.0)), 0.0)
            m = (cb * lmat).astype(BF16)
            y_pair = y_pair + _dot(m, jnp.where(keep, xdt, 0.0).astype(BF16))
        s_prev = state_ref[pair]
        y_off = _dot(cg, s_prev.astype(BF16)) * per_pair(decay_out(h0), decay_out(h1))
        w_in = (xdt * per_pair(decay_in(h0), decay_in(h1))).astype(BF16)
        cd = jnp.where(lane[0:1, :] < SSD_HEAD_DIM, chunk_decay(h0), chunk_decay(h1))
        state_ref[pair] = s_prev * cd + _dot_tn(bg, w_in)
        ys.append(y_pair + y_off + cp_ref[CP_WIDE:CP_WIDE + 1, pair * LANES:(pair + 1) * LANES] * x_pair)

    y = jnp.concatenate(ys, axis=-1)
    zz = zxd_ref[:, OFF_Z:OFF_XBC]
    y = y * (zz * jax.nn.sigmoid(zz))
    gw = SSD_WIDTH // SSD_GROUPS
    outs = []
    for g in range(SSD_GROUPS):
        yg = y[:, g * gw:(g + 1) * gw]
        outs.append(yg * lax.rsqrt(jnp.mean(yg * yg, axis=-1, keepdims=True) + SSD_NORM_EPS))
    y_ref[...] = (jnp.concatenate(outs, axis=-1) * cp_ref[CP_WIDE:CP_WIDE + 1, SSD_WIDTH:]).astype(BF16)


def _attn_ssd_kernel(q_ref, k_ref, v_ref, sw_ref, zxd_ref, cp_ref, o_ref, y_ref, s_scr, vt_scr, xpad_ref, state_ref):
    bq, bk = ATT_BQ, ATT_BK
    qi = pl.program_id(2)
    n_maps = 2
    _ssd_chunk_init(pl.program_id(1) * pl.num_programs(2) + qi, xpad_ref, state_ref)

    @pl.when(qi == 0)
    def _():
        vt_scr[...] = v_ref[...].astype(F32).T.astype(BF16)

    def fold(t, reduce):
        return reduce(t.reshape(bk // SUBLANES, SUBLANES, bq), axis=0)

    def merge(old, new, op):
        return new if old is None else op(old, new)

    def attend(nk):
        _ssd_chunk(zxd_ref, cp_ref, y_ref, xpad_ref, state_ref)
        q = q_ref[...]
        lane = lax.broadcasted_iota(jnp.int32, (bq, LANES), 1)
        zero = jnp.zeros_like(q)
        q_maps = (jnp.where(lane < ATT_HEAD_DIM, q, zero), jnp.where(lane >= ATT_HEAD_DIM, q, zero))
        key = lax.broadcasted_iota(jnp.int32, (bk, bq), 0)
        qry = lax.broadcasted_iota(jnp.int32, (bk, bq), 1)
        causal = key <= qry

        mt = [None] * n_maps
        for j in range(nk):
            kb = k_ref[j * bk:(j + 1) * bk, :]
            for m in range(n_maps):
                s = _dot_nt(kb, q_maps[m])
                if j == nk - 1:
                    s = jnp.where(causal, s, -jnp.inf)
                s_scr[m, j] = s
                mt[m] = merge(mt[m], fold(s, jnp.max), jnp.maximum)
        q_max = [jnp.max(t, axis=0, keepdims=True) for t in mt]

        lt = [None] * n_maps
        acc = [None] * n_maps
        for j in range(nk):
            vt = vt_scr[:, j * bk:(j + 1) * bk]
            for m in range(n_maps):
                p = jnp.exp2(s_scr[m, j] - q_max[m])
                lt[m] = merge(lt[m], fold(p, jnp.sum), jnp.add)
                acc[m] = merge(acc[m], _dot(vt, p.astype(BF16)), jnp.add)

        lams = [cp_ref[CP_SMALL:CP_SMALL + 1, n * LANES:(n + 1) * LANES] for n in range(4)]
        lam = (jnp.exp(jnp.sum(lams[0] * lams[1], axis=-1, keepdims=True))
               - jnp.exp(jnp.sum(lams[2] * lams[3], axis=-1, keepdims=True)) + LAM_INIT)
        l1 = jnp.sum(lt[0], axis=0, keepdims=True)
        l2 = jnp.sum(lt[1], axis=0, keepdims=True)
        o = acc[0] / l1 - lam * (acc[1] / l2)
        o = o * lax.rsqrt(jnp.mean(o * o, axis=0, keepdims=True) + SUBLN_EPS) * sw_ref[...]
        o_ref[...] = (o * (1.0 - LAM_INIT)).T.astype(BF16)

    for nk in range(1, k_ref.shape[0] // bk + 1):
        pl.when(qi == nk - 1)(lambda nk=nk: attend(nk))


def _attn_ssd(q, k, v, lam_q1, lam_k1, lam_q2, lam_k2, subln_w, zxd, conv_w, conv_b, dt_bias, a_log, d_skip,
              norm_w, bsz, seq):
    nq = seq // ATT_BQ
    nc = seq // SSD_CHUNK
    assert nc == ATT_HEADS * nq
    qspec = pl.BlockSpec((ATT_BQ, LANES), lambda b, h, i: (b * nq + i, h))
    kvspec = pl.BlockSpec((seq, LANES), lambda b, h, i: (b, h))
    full = lambda a: pl.BlockSpec(a.shape, lambda b, h, i: (0, 0))
    chunk = lambda n: pl.BlockSpec((SSD_CHUNK, n), lambda b, h, i: (b * nc + h * nq + i, 0))
    sw = subln_w.reshape(-1, 1)
    pad_to = lambda v, n: jnp.pad(v.reshape(1, -1), ((0, 0), (0, n - v.size)))
    wide = jnp.concatenate([jnp.repeat(d_skip, SSD_HEAD_DIM), norm_w]).reshape(1, SSD_CONV_CH)
    small = jnp.concatenate([pad_to(a, LANES) for a in (lam_q1, lam_k1, lam_q2, lam_k2, dt_bias, a_log)], axis=1)
    consts = jnp.concatenate([conv_w, conv_b.reshape(1, -1), wide, pad_to(small, SSD_CONV_CH),
                              jnp.zeros((SUBLANES - CP_SMALL - 1, SSD_CONV_CH), F32)], axis=0)
    return pl.pallas_call(
        _attn_ssd_kernel,
        grid=(bsz, ATT_HEADS, nq),
        in_specs=[qspec, kvspec, kvspec, full(sw), chunk(ZXD_WIDTH), full(consts)],
        out_specs=[qspec, chunk(SSD_WIDTH)],
        out_shape=[jax.ShapeDtypeStruct((bsz * seq, ATT_WIDTH), BF16),
                   jax.ShapeDtypeStruct((bsz * seq, SSD_WIDTH), BF16)],
        scratch_shapes=[pltpu.VMEM((2, seq // ATT_BK, ATT_BK, ATT_BQ), F32),
                        pltpu.VMEM((LANES, seq), BF16),
                        pltpu.VMEM((SSD_CHUNK + 2 * SUBLANES, SSD_CONV_CH), F32),
                        pltpu.VMEM((SSD_HEADS // 2, SSD_STATE, LANES), F32)],
        compiler_params=_cparams(("arbitrary", "arbitrary", "arbitrary")),
        name="attn_ssd",
    )(q, k, v, sw, zxd, consts)


def _mixer(x2, g_mix, w_in, conv_w, conv_b, dt_bias, a_log, d_skip, ssd_norm_w,
           lam_q1, lam_k1, lam_q2, lam_k2, subln_w, bsz, seq):
    zxd, q, k, v = _in_proj(x2, g_mix.reshape(1, -1), w_in.T)
    y_att, y_ssd = _attn_ssd(q, k, v, lam_q1, lam_k1, lam_q2, lam_k2, subln_w, zxd, conv_w, conv_b, dt_bias,
                             a_log, d_skip, ssd_norm_w, bsz, seq)
    return y_ssd, y_att


def _out_proj_kernel(x_ref, ys_ref, ya_ref, wo_ref, g_ref, wr2_ref, br_ref,
                     x1_ref, xn_ref, route_ref, gate_ref, cnt_ref):
    tm = TM_ROUTE
    x1 = x_ref[...] + _dot(jnp.concatenate([ys_ref[...], ya_ref[...]], axis=-1), wo_ref[...])
    x1_ref[...] = x1
    xn_all = _rms(x1, g_ref[...], NORM_EPS)
    xn_b = xn_all.astype(BF16)
    xn_ref[...] = xn_b

    n_tok = OUT_SUB * tm
    half = n_tok // 2
    lg2 = jnp.concatenate([_dot(xn_b[:half], wr2_ref[...]), _dot(xn_b[half:], wr2_ref[...])], axis=0)
    logits = (lg2[:, :LANES] + lg2[:, LANES:]).T[:N_EXPERTS, :] + br_ref[...]

    eidx = lax.broadcasted_iota(jnp.int32, (N_EXPERTS, n_tok), 0).astype(F32)
    work = logits
    vals, idxs, hots = [], [], []
    for _ in range(TOP_K):
        m = jnp.max(work, axis=0, keepdims=True)
        idx = jnp.min(jnp.where(work == m, eidx, float(N_EXPERTS)), axis=0, keepdims=True)
        hot = eidx == idx
        vals.append(m)
        idxs.append(idx.astype(jnp.int32))
        hots.append(hot)
        work = jnp.where(hot, -jnp.inf, work)
    exps = [jnp.exp(v - vals[0]) for v in vals]
    denom = exps[0] + exps[1] + exps[2] + exps[3]
    gates = [e / denom for e in exps]

    cnt = jnp.zeros((N_EXPERTS, n_tok), F32)
    for hot in hots:
        cnt = cnt + jnp.where(hot, 1.0, 0.0)
    cnt_b = cnt.astype(BF16)
    r = lax.broadcasted_iota(jnp.int32, (tm, tm), 0)
    c = lax.broadcasted_iota(jnp.int32, (tm, tm), 1)
    earlier_tok = jnp.where(r < c, 1.0, 0.0).astype(BF16)
    er = lax.broadcasted_iota(jnp.int32, (N_EXPERTS, N_EXPERTS), 0)
    ec = lax.broadcasted_iota(jnp.int32, (N_EXPERTS, N_EXPERTS), 1)
    lower_exp = jnp.where(ec < er, 1.0, 0.0).astype(BF16)
    cnt_pad = jnp.concatenate([cnt_b, jnp.zeros((LANES - N_EXPERTS, n_tok), BF16)], axis=0)
    pos = []
    for sub in range(OUT_SUB):
        cols = slice(sub * tm, (sub + 1) * tm)
        run = jnp.sum(cnt[:, cols], axis=1, keepdims=True)
        run_even = 2.0 * jnp.floor(0.5 * run + 0.5)
        run_start = _dot(lower_exp, jnp.broadcast_to(run_even, (N_EXPERTS, LANES)).astype(BF16))[:, 0:1]
        pos.append(_dot(cnt_b[:, cols], earlier_tok) + run_start)
        cnt_ref[sub] = _dot_nt(jnp.ones((SUBLANES, tm), BF16), cnt_pad[:, cols])
    pos = jnp.concatenate(pos, axis=1)
    lps = [jnp.sum(jnp.where(hot, pos, 0.0), axis=0, keepdims=True).astype(jnp.int32) for hot in hots]

    route_ref[...] = jnp.concatenate(idxs + lps, axis=0)
    gate_ref[...] = jnp.concatenate(gates + [jnp.zeros((SUBLANES - TOP_K, n_tok), F32)], axis=0)


def _out_proj(x2, y_ssd, y_att, w_out, g_ffn, w_router, b_router):
    t = x2.shape[0]
    tm = OUT_SUB * TM_ROUTE
    nt = t // TM_ROUTE
    wo = w_out.astype(BF16)
    wrh, wrm, _ = _split3(jnp.pad(w_router, ((0, 0), (0, LANES - N_EXPERTS))))
    wr2 = jnp.concatenate([wrh, wrm], axis=1)
    br = b_router.reshape(N_EXPERTS, 1)
    row = lambda n: pl.BlockSpec((tm, n), lambda i: (i, 0))
    col = pl.BlockSpec((SUBLANES, tm), lambda i: (0, i))
    full = lambda a: pl.BlockSpec(a.shape, lambda i: (0, 0))
    args = (x2, y_ssd, y_att, wo, g_ffn.reshape(1, -1), wr2, br)
    return pl.pallas_call(
        _out_proj_kernel,
        grid=(t // tm,),
        in_specs=[row(D_MODEL), row(SSD_WIDTH), row(ATT_WIDTH)] + [full(a) for a in args[3:]],
        out_specs=[row(D_MODEL), row(D_MODEL), col, col,
                   pl.BlockSpec((OUT_SUB, SUBLANES, LANES), lambda i: (i, 0, 0))],
        out_shape=[
            jax.ShapeDtypeStruct((t, D_MODEL), F32),
            jax.ShapeDtypeStruct((t, D_MODEL), BF16),
            jax.ShapeDtypeStruct((SUBLANES, t), jnp.int32),
            jax.ShapeDtypeStruct((SUBLANES, t), F32),
            jax.ShapeDtypeStruct((nt, SUBLANES, LANES), F32),
        ],
        compiler_params=_cparams(("arbitrary",)),
        name="out_proj",
    )(*args)


def _copy_run(n, start_copy):
    @pl.when(n > 0)
    def _():
        start_copy(n)


def _select_by_position(positions, values, n_pos):
    tm = positions[0].shape[1]
    r = lax.broadcasted_iota(jnp.int32, (n_pos, tm), 0)
    out = jnp.zeros((n_pos, tm), F32)
    for k in reversed(range(TOP_K)):
        out = jnp.where(r == positions[k], values[k], out)
    return out


def _dispatch_kernel(cnt_ref, ls_ref, base_ref, zoff_ref, rows_ref, xn_ref, route_ref, xs_hbm, xloc, zeros_vmem,
                     sem_z, sems):
    n_pos = N_POS
    i = pl.program_id(0)
    n_steps = pl.num_programs(0)
    slot = lax.rem(i, 2)

    def zero_fill(op):
        def pad_rows(e, _):
            n = zoff_ref[N_EXPERTS + 1 + e]

            @pl.when(n > 0)
            def _():
                op(pltpu.make_async_copy(zeros_vmem.at[_row_slice(0, n, PACK_SUB)],
                                         xs_hbm.at[_row_slice(zoff_ref[e], n, PACK_SUB)], sem_z))
            return 0

        def unused_block(b, _):
            op(pltpu.make_async_copy(zeros_vmem, xs_hbm.at[_row_slice(b * ROW_BLK, ROW_BLK, PACK_SUB)], sem_z))
            return 0

        lax.fori_loop(0, N_EXPERTS, pad_rows, 0)
        lax.fori_loop(zoff_ref[N_EXPERTS], xs_hbm.shape[0] // (ROW_BLK * PACK_SUB), unused_block, 0)

    @pl.when(i == 0)
    def _():
        zeros_vmem[...] = jnp.zeros_like(zeros_vmem)
        zero_fill(lambda cp: cp.start())

    def slot_wait(s, step):
        n = rows_ref[step]
        pltpu.make_async_copy(xloc.at[s, _row_slice(0, n, PACK_SUB)], xs_hbm.at[_row_slice(0, n, PACK_SUB)],
                              sems.at[s]).wait()

    @pl.when(i >= 2)
    def _():
        slot_wait(slot, i - 2)

    positions = [route_ref[TOP_K + k:TOP_K + k + 1, :] for k in range(TOP_K)]
    sel = _select_by_position(positions, [1.0] * TOP_K, n_pos).astype(BF16)
    rows = _dot(sel, xn_ref[...])
    _pack_rows(xloc.at[slot], rows)

    def per_expert(e, _):
        idx = i * N_EXPERTS + e
        src0 = ls_ref[idx]
        dst0 = base_ref[idx]

        def start_copy(n):
            pltpu.make_async_copy(xloc.at[slot, _row_slice(src0, n, PACK_SUB)],
                                  xs_hbm.at[_row_slice(dst0, n, PACK_SUB)], sems.at[slot]).start()

        _copy_run(cnt_ref[idx], start_copy)
        return 0

    lax.fori_loop(0, N_EXPERTS, per_expert, 0, unroll=RUN_UNROLL)

    @pl.when(i == n_steps - 1)
    def _():
        slot_wait(slot, i)

        @pl.when(n_steps > 1)
        def _():
            slot_wait(1 - slot, i - 1)

        zero_fill(lambda cp: cp.wait())


def _dispatch(xn, route, tables, n_rows):
    t = xn.shape[0]
    tm = TM_ROUTE
    cnt_tbl, ls_tbl, base_tbl, zoff, tile_rows = tables
    return pl.pallas_call(
        _dispatch_kernel,
        grid_spec=pltpu.PrefetchScalarGridSpec(
            num_scalar_prefetch=5,
            grid=(t // tm,),
            in_specs=[pl.BlockSpec((tm, D_MODEL), lambda i, *_: (i, 0)),
                      pl.BlockSpec((SUBLANES, tm), lambda i, *_: (0, i))],
            out_specs=pl.BlockSpec(memory_space=pl.ANY),
            scratch_shapes=[
                pltpu.VMEM((2, N_POS * PACK_SUB, LANES), jnp.uint32),
                pltpu.VMEM((ROW_BLK * PACK_SUB, LANES), jnp.uint32),
                pltpu.SemaphoreType.DMA,
                pltpu.SemaphoreType.DMA((2,)),
            ],
        ),
        out_shape=jax.ShapeDtypeStruct(((n_rows + ROW_BLK) * PACK_SUB, LANES), jnp.uint32),
        compiler_params=_cparams(("arbitrary",)),
        name="dispatch",
    )(cnt_tbl, ls_tbl, base_tbl, zoff, tile_rows, xn, route)


def _experts_kernel(blk_e_ref, nvalid_ref, first_ref, wslot_ref, enext_ref, pieces_ref, xs_ref, wup_hbm, wdn_hbm,
                    bgu_ref, bd_ref, ys_ref, wup_buf, wdn_buf, wg_s, wu_s, wd_s, sems):
    i = pl.program_id(0)
    slot = wslot_ref[i]

    def weight_copies(e, s):
        return (pltpu.make_async_copy(wup_hbm.at[e], wup_buf.at[s], sems.at[0, s]),
                pltpu.make_async_copy(wdn_hbm.at[e], wdn_buf.at[s], sems.at[1, s]))

    @pl.when(i == 0)
    def _():
        for cp in weight_copies(blk_e_ref[0], slot):
            cp.start()

    @pl.when(first_ref[i] != 0)
    def _():
        for cp in weight_copies(blk_e_ref[i], slot):
            cp.wait()

        @pl.when(enext_ref[i] >= 0)
        def _():
            for cp in weight_copies(enext_ref[i], 1 - slot):
                cp.start()

        src = lax.broadcasted_iota(jnp.int32, (DEINT, DEINT), 0)
        dst = lax.broadcasted_iota(jnp.int32, (DEINT, DEINT), 1)
        perm = jnp.where(src == jnp.where(dst < LANES, 2 * dst, 2 * (dst - LANES) + 1), 1.0, 0.0).astype(BF16)
        for g in range(2 * D_EXPERT // DEINT):
            sep = _dot(wup_buf[slot, :, g * DEINT:(g + 1) * DEINT].astype(BF16), perm)
            wg_s[:, g * LANES:(g + 1) * LANES] = sep[:, :LANES].astype(BF16)
            wu_s[:, g * LANES:(g + 1) * LANES] = sep[:, LANES:].astype(BF16)
        wd_s[...] = wdn_buf[slot].astype(BF16)

    def ffn(n_rows):
        used = n_rows * ROW_SUB
        if n_rows:
            xb = _unpack_rows(xs_ref.at[pl.ds(0, n_rows * PACK_SUB)])
            gate = jnp.minimum(_dot(xb, wg_s[...]) + bgu_ref[0, 0:1, :], SWIGLU_LIMIT)
            up = jnp.clip(_dot(xb, wu_s[...]) + bgu_ref[0, 1:2, :], -SWIGLU_LIMIT, SWIGLU_LIMIT)
            act = (up + 1.0) * gate * jax.nn.sigmoid(SWIGLU_ALPHA * gate)
            _store_rows(ys_ref.at[pl.ds(0, used)], _dot(act.astype(BF16), wd_s[...]) + bd_ref[0])
        if n_rows < ROW_BLK:
            ys_ref[used:, :] = jnp.zeros((ROW_BLK * ROW_SUB - used, LANES), F32)

    for pieces in range(ROW_BLK // ROW_PIECE + 1):
        pl.when(pieces_ref[i] == pieces)(lambda pieces=pieces: ffn(pieces * ROW_PIECE))


def _experts(xs, plan, w_up, b_up, w_down, b_down, n_rows):
    nb = n_rows // ROW_BLK
    b_gu = b_up.reshape(N_EXPERTS, D_EXPERT, 2).transpose(0, 2, 1)
    bd = b_down.reshape(N_EXPERTS, 1, D_MODEL)
    blk_e, nvalid, first, wslot, enext, pieces = plan
    src = lambda i, be, nv, *_: (jnp.minimum(i, nv[0] - 1), 0)
    bspec = lambda r, m: pl.BlockSpec((1, r, m), lambda i, be, *_: (be[i], 0, 0))
    anyspec = pl.BlockSpec(memory_space=pl.ANY)
    return pl.pallas_call(
        _experts_kernel,
        grid_spec=pltpu.PrefetchScalarGridSpec(
            num_scalar_prefetch=6,
            grid=(nb,),
            in_specs=[pl.BlockSpec((ROW_BLK * PACK_SUB, LANES), src), anyspec, anyspec,
                      bspec(2, D_EXPERT), bspec(1, D_MODEL)],
            out_specs=_rows_spec(ROW_BLK, lambda i, *_: (i, 0)),
            scratch_shapes=[
                pltpu.VMEM((2, D_MODEL, 2 * D_EXPERT), F32),
                pltpu.VMEM((2, D_EXPERT, D_MODEL), F32),
                pltpu.VMEM((D_MODEL, D_EXPERT), BF16),
                pltpu.VMEM((D_MODEL, D_EXPERT), BF16),
                pltpu.VMEM((D_EXPERT, D_MODEL), BF16),
                pltpu.SemaphoreType.DMA((2, 2)),
            ],
        ),
        out_shape=jax.ShapeDtypeStruct((n_rows * ROW_SUB, LANES), F32),
        compiler_params=_cparams(("arbitrary",)),
        name="experts",
    )(blk_e, nvalid, first, wslot, enext, pieces, xs, w_up, w_down, b_gu, bd)


def _combine_kernel(cnt_ref, ls_ref, base_ref, rows_ref, ys_hbm, route_ref, gate_ref, x1_ref, p_ref, gp_ref, wpg_ref,
                    wpp_ref, gf_ref, o_ref, yloc, sems):
    tm = TM_ROUTE
    n_pos = N_POS
    i = pl.program_id(0)
    n_steps = pl.num_programs(0)
    slot = lax.rem(i, 2)

    def gather_step(step, s):
        for u in range(COMB_SUB):
            def per_expert(e, _, u=u):
                idx = (step * COMB_SUB + u) * N_EXPERTS + e
                src0 = base_ref[idx]
                dst0 = ls_ref[idx]

                def start_copy(n):
                    pltpu.make_async_copy(ys_hbm.at[_row_slice(src0, n)],
                                          yloc.at[s, u, _row_slice(dst0, n)], sems.at[s]).start()

                _copy_run(cnt_ref[idx], start_copy)
                return 0

            lax.fori_loop(0, N_EXPERTS, per_expert, 0, unroll=RUN_UNROLL)

    @pl.when(i == 0)
    def _():
        yloc[...] = jnp.zeros_like(yloc)
        gather_step(0, 0)

    @pl.when(i + 1 < n_steps)
    def _():
        gather_step(i + 1, 1 - slot)

    for u in range(COMB_SUB):
        n = rows_ref[i * COMB_SUB + u]
        pltpu.make_async_copy(ys_hbm.at[_row_slice(0, n)], yloc.at[slot, u, _row_slice(0, n)], sems.at[slot]).wait()

    moe = []
    for u in range(COMB_SUB):
        cols = slice(u * tm, (u + 1) * tm)
        positions = [route_ref[TOP_K + k:TOP_K + k + 1, cols] for k in range(TOP_K)]
        gsel = _select_by_position(positions, [gate_ref[k:k + 1, cols] for k in range(TOP_K)], n_pos)
        g_pos = jnp.sum(gsel, axis=1, keepdims=True)
        sel = jnp.where(gsel != 0.0, 1.0, 0.0).astype(BF16)
        y_gated = (_load_rows(yloc.at[slot, u]) * g_pos).astype(BF16)
        moe.append(_dot_tn(sel, y_gated))
    x2 = x1_ref[...] + jnp.concatenate(moe, axis=0)
    xn = _rms(x2, gp_ref[...], NORM_EPS).astype(BF16)
    pp = _dot(p_ref[...].astype(BF16), wpp_ref[...])
    half = xn.shape[0] // 2
    ple_gate = jnp.concatenate([_dot(xn[:half], wpg_ref[...]), _dot(xn[half:], wpg_ref[...])], axis=0)
    x3 = x2 + pp * jax.nn.sigmoid(ple_gate)
    o_ref[...] = _rms(x3, gf_ref[...], NORM_EPS)


def _combine(ys, route, gate_t, x1, p2, tables, g_ple, w_ple_gate, w_ple_proj, g_final):
    t = x1.shape[0]
    tm = COMB_SUB * TM_ROUTE
    cnt_tbl, ls_tbl, base_tbl, _, tile_rows = tables
    row = lambda n: pl.BlockSpec((tm, n), lambda i, *_: (i, 0))
    col = pl.BlockSpec((SUBLANES, tm), lambda i, *_: (0, i))
    full = lambda a: pl.BlockSpec(a.shape, lambda i, *_: (0, 0))
    consts = (g_ple.reshape(1, -1), w_ple_gate.astype(BF16), w_ple_proj.astype(BF16), g_final.reshape(1, -1))
    return pl.pallas_call(
        _combine_kernel,
        grid_spec=pltpu.PrefetchScalarGridSpec(
            num_scalar_prefetch=4,
            grid=(t // tm,),
            in_specs=[pl.BlockSpec(memory_space=pl.ANY), col, col, row(D_MODEL), row(PLE_DIM)]
                     + [full(a) for a in consts],
            out_specs=row(D_MODEL),
            scratch_shapes=[
                pltpu.VMEM((2, COMB_SUB, N_POS * ROW_SUB, LANES), F32),
                pltpu.SemaphoreType.DMA((2,)),
            ],
        ),
        out_shape=jax.ShapeDtypeStruct((t, D_MODEL), F32),
        compiler_params=_cparams(("arbitrary",)),
        name="combine",
    )(cnt_tbl, ls_tbl, base_tbl, tile_rows, ys, route, gate_t, x1, p2, *consts)


def _routing_tables(cnt, n_blocks):
    tile_cnt = cnt[:, 0, :N_EXPERTS].astype(jnp.int32)
    tile_cnt = tile_cnt + tile_cnt % 2
    counts = jnp.sum(tile_cnt, axis=0)
    padded = (counts + ROW_BLK - 1) // ROW_BLK * ROW_BLK
    pend = jnp.cumsum(padded)
    pstart = pend - padded
    base = pstart[None, :] + jnp.cumsum(tile_cnt, axis=0) - tile_cnt
    lstart = jnp.cumsum(tile_cnt, axis=1) - tile_cnt
    nvalid = pend[-1:] // ROW_BLK
    zoff = jnp.concatenate([pstart + counts, nvalid, padded - counts])
    blk_start = jnp.minimum(jnp.arange(n_blocks, dtype=jnp.int32) * ROW_BLK, pend[-1] - 1)
    blk_e = jnp.minimum(jnp.sum((pend[None, :] <= blk_start[:, None]).astype(jnp.int32), axis=1), N_EXPERTS - 1)
    first = jnp.concatenate([jnp.ones((1,), bool), blk_e[1:] != blk_e[:-1]])
    wslot = (jnp.cumsum(first.astype(jnp.int32)) - 1) % 2
    eids = jnp.arange(N_EXPERTS, dtype=jnp.int32)
    later_nonempty = (eids[None, :] > eids[:, None]) & (padded[None, :] > 0)
    next_e = jnp.min(jnp.where(later_nonempty, eids[None, :], N_EXPERTS), axis=1)
    next_e = jnp.where(next_e == N_EXPERTS, -1, next_e)
    blk_hot = blk_e[:, None] == eids[None, :]
    per_block = lambda v: jnp.sum(jnp.where(blk_hot, v[None, :], 0), axis=1)
    blk_ids = jnp.arange(n_blocks, dtype=jnp.int32)
    real_rows = jnp.clip(per_block(pstart + counts) - blk_ids * ROW_BLK, 0, ROW_BLK) * (blk_ids < nvalid[0])
    pieces = (real_rows + ROW_PIECE - 1) // ROW_PIECE
    tile_rows = jnp.sum(tile_cnt, axis=1)
    i32 = lambda a: a.reshape(-1).astype(jnp.int32)
    plan = (i32(blk_e), i32(nvalid), i32(first), i32(wslot), i32(per_block(next_e)), i32(pieces))
    return (i32(tile_cnt), i32(lstart), i32(base), i32(zoff), i32(tile_rows)), plan


def kernel(x, p, g_mix, w_in, conv_w, conv_b, dt_bias, a_log, d_skip, ssd_norm_w, lam_q1, lam_k1, lam_q2, lam_k2, subln_w, w_out, g_ffn, w_router, b_router, w_up, b_up, w_down, b_down, g_ple, w_ple_gate, w_ple_proj, g_final):
    bsz, seq, d = x.shape
    t = bsz * seq
    x2 = x.reshape(t, d)
    y_ssd, y_att = _mixer(x2, g_mix[0], w_in[0], conv_w[0], conv_b[0], dt_bias[0], a_log[0], d_skip[0], ssd_norm_w[0],
                          lam_q1[0], lam_k1[0], lam_q2[0], lam_k2[0], subln_w[0], bsz, seq)
    x1, xn, route, gate_t, cnt = _out_proj(x2, y_ssd, y_att, w_out[0], g_ffn[0], w_router[0], b_router[0])

    n_rows = t * TOP_K + (t // TM_ROUTE) * N_EXPERTS + N_EXPERTS * ROW_BLK
    n_rows = -(-n_rows // ROW_BLK) * ROW_BLK
    tables, plan = _routing_tables(cnt, n_rows // ROW_BLK)
    xs = _dispatch(xn, route, tables, n_rows)
    ys = _experts(xs, plan, w_up[0], b_up[0], w_down[0], b_down[0], n_rows)
    out = _combine(ys, route, gate_t, x1, p[0].reshape(t, PLE_DIM), tables, g_ple[0], w_ple_gate[0], w_ple_proj[0],
                   g_final)
    return out.reshape(bsz, seq, d)
```

```python
import math

import jax
import jax.numpy as jnp
from jax import lax
from jax.experimental import pallas as pl
from jax.experimental.pallas import tpu as pltpu

F32 = jnp.float32
BF16 = jnp.bfloat16

D_MODEL = 1024
PLE_DIM = 256
SSD_WIDTH = 512
ATT_WIDTH = 512
SSD_HEAD_DIM = 64
SSD_HEADS = 8
SSD_GROUPS = 2
SSD_STATE = 128
SSD_CONV = 4
SSD_CHUNK = 128
SSD_CONV_CH = SSD_WIDTH + 2 * SSD_GROUPS * SSD_STATE
SSD_NORM_EPS = 1e-5
ATT_HEAD_DIM = 64
ATT_HEADS = 4
SUBLN_EPS = 1e-5
OFF_Z = 0
OFF_XBC = OFF_Z + SSD_WIDTH
OFF_DT = OFF_XBC + SSD_CONV_CH
OFF_Q = OFF_DT + SSD_HEADS
OFF_K = OFF_Q + ATT_WIDTH
OFF_V = OFF_K + ATT_WIDTH
IN_PROJ = OFF_V + ATT_WIDTH
N_EXPERTS = 32
TOP_K = 4
D_EXPERT = 1024
SWIGLU_LIMIT = 7.0
SWIGLU_ALPHA = 1.702
NORM_EPS = 1e-6
LAM_INIT = 0.8 - 0.6 * math.exp(-0.3 * 0)

LANES = 128
SUBLANES = 8
VMEM_LIMIT_BYTES = 56 * 1024 * 1024
ROW_SUB = D_MODEL // LANES
PACK_SUB = ROW_SUB // 2

TM_PROJ = 512
ATT_BQ = 512
ATT_BK = 512
ATT_SUBK = 256
ROW_BLK = 512
ROW_PIECE = 128
TM_ROUTE = 256
N_POS = TOP_K * TM_ROUTE + N_EXPERTS
OUT_SUB = 4
COMB_SUB = 2
RUN_UNROLL = 4
DEINT = 2 * LANES
ZXD_WIDTH = OFF_DT + LANES
CP_CONV_B = SSD_CONV
CP_WIDE = SSD_CONV + 1
CP_SMALL = SSD_CONV + 2
CP_DT_BIAS = 4 * LANES
CP_A_LOG = 5 * LANES


def _cparams(sem):
    return pltpu.CompilerParams(dimension_semantics=sem, vmem_limit_bytes=VMEM_LIMIT_BYTES)


def _rms(x, w, eps):
    return x * lax.rsqrt(jnp.mean(x * x, axis=-1, keepdims=True) + eps) * w


def _dot(a, b):
    return jnp.dot(a, b, preferred_element_type=F32)


def _dot_nt(a, b):
    return lax.dot_general(a, b, (((1,), (1,)), ((), ())), preferred_element_type=F32)


def _dot_tn(a, b):
    return lax.dot_general(a, b, (((0,), (0,)), ((), ())), preferred_element_type=F32)


def _store_rows(ref, val):
    n = val.shape[0]
    for s in range(ROW_SUB):
        ref[pl.ds(s, n, stride=ROW_SUB), :] = val[:, s * LANES:(s + 1) * LANES]


def _load_rows(ref):
    n = ref.shape[0] // ROW_SUB
    return jnp.concatenate([ref[pl.ds(s, n, stride=ROW_SUB), :] for s in range(ROW_SUB)], axis=-1)


def _rows_spec(n, index_map):
    return pl.BlockSpec((n * ROW_SUB, LANES), index_map)


def _row_slice(start, n, sub=ROW_SUB):
    return pl.ds(pl.multiple_of(start * sub, SUBLANES), n * sub)


def _pack_rows(ref, val):
    n = val.shape[0]
    bits = lambda v: lax.bitcast_convert_type(v, jnp.uint32)
    words = (bits(val[:, :D_MODEL // 2]) & jnp.uint32(0xFFFF0000)) | (bits(val[:, D_MODEL // 2:]) >> 16)
    for s in range(PACK_SUB):
        ref[pl.ds(s, n, stride=PACK_SUB), :] = words[:, s * LANES:(s + 1) * LANES]


def _unpack_rows(ref):
    n = ref.shape[0] // PACK_SUB
    words = [ref[pl.ds(s, n, stride=PACK_SUB), :] for s in range(PACK_SUB)]
    as_f32 = lambda w: lax.bitcast_convert_type(w, F32)
    hi = [as_f32(w & jnp.uint32(0xFFFF0000)) for w in words]
    lo = [as_f32(w << 16) for w in words]
    return jnp.concatenate(hi + lo, axis=-1).astype(BF16)


def _split3(x):
    hi = x.astype(BF16)
    r1 = x - hi.astype(F32)
    mid = r1.astype(BF16)
    lo = (r1 - mid.astype(F32)).astype(BF16)
    return hi, mid, lo


def _in_proj_kernel(x_ref, g_ref, win_ref, zxd_ref, q_ref, k_ref, v_ref, w_ref):
    @pl.when(pl.program_id(0) == 0)
    def _():
        w_ref[:OFF_DT, :] = win_ref[:OFF_DT, :].astype(BF16)
        dt_rows = jnp.concatenate([win_ref[OFF_DT:OFF_Q, :], jnp.zeros((LANES - SSD_HEADS, D_MODEL), F32)], axis=0)
        w_ref[OFF_DT:OFF_DT + LANES, :] = dt_rows.astype(BF16)
        w_ref[OFF_DT + LANES:, :] = win_ref[OFF_Q:, :].astype(BF16)

    h = _rms(x_ref[...], g_ref[...], NORM_EPS).astype(BF16)
    q0 = ZXD_WIDTH
    k0, v0 = q0 + ATT_WIDTH, q0 + 2 * ATT_WIDTH
    zxd_ref[...] = _dot_nt(h, w_ref[:q0, :])
    q_ref[...] = (_dot_nt(h, w_ref[q0:k0, :]) * (ATT_HEAD_DIM ** -0.5 * math.log2(math.e))).astype(BF16)
    k_ref[...] = _dot_nt(h, w_ref[k0:v0, :]).astype(BF16)
    v_ref[...] = _dot_nt(h, w_ref[v0:v0 + ATT_WIDTH, :]).astype(BF16)


def _in_proj(x2, g_mix, w_in_t):
    t = x2.shape[0]
    tm = TM_PROJ
    row = lambda n: pl.BlockSpec((tm, n), lambda i: (i, 0))
    full = lambda a: pl.BlockSpec(a.shape, lambda i: (0, 0))
    return pl.pallas_call(
        _in_proj_kernel,
        grid=(t // tm,),
        in_specs=[row(D_MODEL), full(g_mix), full(w_in_t)],
        out_specs=[row(ZXD_WIDTH), row(ATT_WIDTH), row(ATT_WIDTH), row(ATT_WIDTH)],
        out_shape=[
            jax.ShapeDtypeStruct((t, ZXD_WIDTH), F32),
            jax.ShapeDtypeStruct((t, ATT_WIDTH), BF16),
            jax.ShapeDtypeStruct((t, ATT_WIDTH), BF16),
            jax.ShapeDtypeStruct((t, ATT_WIDTH), BF16),
        ],
        scratch_shapes=[pltpu.VMEM((IN_PROJ + LANES - SSD_HEADS, D_MODEL), BF16)],
        compiler_params=_cparams(("arbitrary",)),
        name="in_proj",
    )(x2, g_mix, w_in_t)


def _ssd_chunk_init(c, xpad_ref, state_ref):
    L = SSD_CHUNK

    @pl.when(c == 0)
    def _():
        xpad_ref[0:SUBLANES, :] = jnp.zeros((SUBLANES, SSD_CONV_CH), F32)
        state_ref[...] = jnp.zeros_like(state_ref)

    @pl.when(c != 0)
    def _():
        xpad_ref[0:SUBLANES, :] = xpad_ref[L:L + SUBLANES, :]


def _ssd_chunk(zxd_ref, cp_ref, y_ref, xpad_ref, state_ref):
    L = SSD_CHUNK
    xpad_ref[SUBLANES:SUBLANES + L, :] = zxd_ref[:, OFF_XBC:OFF_DT]

    conv = cp_ref[CP_CONV_B:CP_CONV_B + 1, :]
    for j in range(SSD_CONV):
        off = SUBLANES - (SSD_CONV - 1) + j
        conv = conv + cp_ref[j:j + 1, :] * xpad_ref[off:off + L, :]
    act = conv * jax.nn.sigmoid(conv)
    xs = act[:, :SSD_WIDTH]
    bm = act[:, SSD_WIDTH:SSD_WIDTH + SSD_GROUPS * SSD_STATE].astype(BF16)
    cm = act[:, SSD_WIDTH + SSD_GROUPS * SSD_STATE:].astype(BF16)

    dt_in = zxd_ref[:, OFF_DT:] + cp_ref[CP_SMALL:CP_SMALL + 1, CP_DT_BIAS:CP_DT_BIAS + LANES]
    dt_all = jnp.maximum(dt_in, 0.0) + jnp.log1p(jnp.exp(-jnp.abs(dt_in)))
    adt = dt_all * (-jnp.exp(cp_ref[CP_SMALL:CP_SMALL + 1, CP_A_LOG:CP_A_LOG + LANES]))

    ri = lax.broadcasted_iota(jnp.int32, (L, L), 0)
    ci = lax.broadcasted_iota(jnp.int32, (L, L), 1)
    causal = ci <= ri
    tril = jnp.where(causal, 1.0, 0.0).astype(BF16)
    hi, mid, lo = _split3(adt)
    acum_all = _dot(tril, hi) + _dot(tril, mid) + _dot(tril, lo)
    acum_t = acum_all.T
    a_last = acum_all[L - 1:L, :]
    decay_in_all = jnp.exp(a_last - acum_all)
    decay_out_all = jnp.exp(acum_all)
    chunk_decay_all = jnp.exp(a_last)
    acum = lambda h: acum_all[:, h:h + 1]
    dt = lambda h: dt_all[:, h:h + 1]
    decay_out = lambda h: decay_out_all[:, h:h + 1]
    decay_in = lambda h: decay_in_all[:, h:h + 1]
    chunk_decay = lambda h: chunk_decay_all[:, h:h + 1]

    lane = lax.broadcasted_iota(jnp.int32, (L, LANES), 1)
    lo_half = lane < SSD_HEAD_DIM

    def per_pair(col_a, col_b):
        return jnp.where(lo_half, col_a, col_b)

    ys = []
    for pair in range(SSD_HEADS // 2):
        g = pair // 2
        h0, h1 = 2 * pair, 2 * pair + 1
        cg = cm[:, g * SSD_STATE:(g + 1) * SSD_STATE]
        bg = bm[:, g * SSD_STATE:(g + 1) * SSD_STATE]
        cb = _dot_nt(cg, bg)
        x_pair = xs[:, pair * LANES:(pair + 1) * LANES]
        xdt = x_pair * per_pair(dt(h0), dt(h1))
        y_pair = jnp.zeros((L, LANES), F32)
        for hh, keep in ((h0, lo_half), (h1, jnp.logical_not(lo_half))):
            seg = acum(hh) - acum_t[hh:hh + 1, :]
            lmat = jnp.where(causal, jnp.exp(jnp.where(causal, seg, 0.0)), 0.0)
            m = (cb * lmat).astype(BF16)
            y_pair = y_pair + _dot(m, jnp.where(keep, xdt, 0.0).astype(BF16))
        s_prev = state_ref[pair]
        y_off = _dot(cg, s_prev.astype(BF16)) * per_pair(decay_out(h0), decay_out(h1))
        w_in = (xdt * per_pair(decay_in(h0), decay_in(h1))).astype(BF16)
        cd = jnp.where(lane[0:1, :] < SSD_HEAD_DIM, chunk_decay(h0), chunk_decay(h1))
        state_ref[pair] = s_prev * cd + _dot_tn(bg, w_in)
        ys.append(y_pair + y_off + cp_ref[CP_WIDE:CP_WIDE + 1, pair * LANES:(pair + 1) * LANES] * x_pair)

    y = jnp.concatenate(ys, axis=-1)
    zz = zxd_ref[:, OFF_Z:OFF_XBC]
    y = y * (zz * jax.nn.sigmoid(zz))
    gw = SSD_WIDTH // SSD_GROUPS
    outs = []
    for g in range(SSD_GROUPS):
        yg = y[:, g * gw:(g + 1) * gw]
        outs.append(yg * lax.rsqrt(jnp.mean(yg * yg, axis=-1, keepdims=True) + SSD_NORM_EPS))
    y_ref[...] = (jnp.concatenate(outs, axis=-1) * cp_ref[CP_WIDE:CP_WIDE + 1, SSD_WIDTH:]).astype(BF16)


def _attn_ssd_kernel(q_ref, k_ref, v_ref, sw_ref, zxd_ref, cp_ref, o_ref, y_ref, s_scr, vt_scr, xpad_ref, state_ref):
    bq, bk = ATT_BQ, ATT_BK
    qi = pl.program_id(2)
    n_maps = 2
    _ssd_chunk_init(pl.program_id(1) * pl.num_programs(2) + qi, xpad_ref, state_ref)

    @pl.when(qi == 0)
    def _():
        vt_scr[...] = v_ref[...].astype(F32).T.astype(BF16)

    def fold(t, reduce):
        return reduce(t.reshape(bk // SUBLANES, SUBLANES, bq), axis=0)

    def merge(old, new, op):
        return new if old is None else op(old, new)

    def attend(nk):
        _ssd_chunk(zxd_ref, cp_ref, y_ref, xpad_ref, state_ref)
        q = q_ref[...]
        lane = lax.broadcasted_iota(jnp.int32, (bq, LANES), 1)
        zero = jnp.zeros_like(q)
        q_maps = (jnp.where(lane < ATT_HEAD_DIM, q, zero), jnp.where(lane >= ATT_HEAD_DIM, q, zero))
        key = lax.broadcasted_iota(jnp.int32, (bk, bq), 0)
        qry = lax.broadcasted_iota(jnp.int32, (bk, bq), 1)
        causal = key <= qry

        mt = [None] * n_maps
        for j in range(nk):
            kb = k_ref[j * bk:(j + 1) * bk, :]
            for m in range(n_maps):
                s = _dot_nt(kb, q_maps[m])
                if j == nk - 1:
                    s = jnp.where(causal, s, -jnp.inf)
                s_scr[m, j] = s
                mt[m] = merge(mt[m], fold(s, jnp.max), jnp.maximum)
        q_max = [jnp.max(t, axis=0, keepdims=True) for t in mt]

        lt = [None] * n_maps
        acc = [None] * n_maps
        for j in range(nk):
            for c in range(bk // ATT_SUBK):
                k0 = c * ATT_SUBK
                vt = vt_scr[:, j * bk + k0:j * bk + k0 + ATT_SUBK]
                for m in range(n_maps):
                    p = jnp.exp2(s_scr[m, j, k0:k0 + ATT_SUBK, :] - q_max[m])
                    lt[m] = merge(lt[m], jnp.sum(p.reshape(ATT_SUBK // SUBLANES, SUBLANES, bq), axis=0), jnp.add)
                    acc[m] = merge(acc[m], _dot(vt, p.astype(BF16)), jnp.add)

        lams = [cp_ref[CP_SMALL:CP_SMALL + 1, n * LANES:(n + 1) * LANES] for n in range(4)]
        lam = (jnp.exp(jnp.sum(lams[0] * lams[1], axis=-1, keepdims=True))
               - jnp.exp(jnp.sum(lams[2] * lams[3], axis=-1, keepdims=True)) + LAM_INIT)
        l1 = jnp.sum(lt[0], axis=0, keepdims=True)
        l2 = jnp.sum(lt[1], axis=0, keepdims=True)
        o = acc[0] / l1 - lam * (acc[1] / l2)
        o = o * lax.rsqrt(jnp.mean(o * o, axis=0, keepdims=True) + SUBLN_EPS) * sw_ref[...]
        o_ref[...] = (o * (1.0 - LAM_INIT)).T.astype(BF16)

    for nk in range(1, k_ref.shape[0] // bk + 1):
        pl.when(qi == nk - 1)(lambda nk=nk: attend(nk))


def _attn_ssd(q, k, v, lam_q1, lam_k1, lam_q2, lam_k2, subln_w, zxd, conv_w, conv_b, dt_bias, a_log, d_skip,
              norm_w, bsz, seq):
    nq = seq // ATT_BQ
    nc = seq // SSD_CHUNK
    assert nc == ATT_HEADS * nq
    qspec = pl.BlockSpec((ATT_BQ, LANES), lambda b, h, i: (b * nq + i, h))
    kvspec = pl.BlockSpec((seq, LANES), lambda b, h, i: (b, h))
    full = lambda a: pl.BlockSpec(a.shape, lambda b, h, i: (0, 0))
    chunk = lambda n: pl.BlockSpec((SSD_CHUNK, n), lambda b, h, i: (b * nc + h * nq + i, 0))
    sw = subln_w.reshape(-1, 1)
    pad_to = lambda v, n: jnp.pad(v.reshape(1, -1), ((0, 0), (0, n - v.size)))
    wide = jnp.concatenate([jnp.repeat(d_skip, SSD_HEAD_DIM), norm_w]).reshape(1, SSD_CONV_CH)
    small = jnp.concatenate([pad_to(a, LANES) for a in (lam_q1, lam_k1, lam_q2, lam_k2, dt_bias, a_log)], axis=1)
    consts = jnp.concatenate([conv_w, conv_b.reshape(1, -1), wide, pad_to(small, SSD_CONV_CH),
                              jnp.zeros((SUBLANES - CP_SMALL - 1, SSD_CONV_CH), F32)], axis=0)
    return pl.pallas_call(
        _attn_ssd_kernel,
        grid=(bsz, ATT_HEADS, nq),
        in_specs=[qspec, kvspec, kvspec, full(sw), chunk(ZXD_WIDTH), full(consts)],
        out_specs=[qspec, chunk(SSD_WIDTH)],
        out_shape=[jax.ShapeDtypeStruct((bsz * seq, ATT_WIDTH), BF16),
                   jax.ShapeDtypeStruct((bsz * seq, SSD_WIDTH), BF16)],
        scratch_shapes=[pltpu.VMEM((2, seq // ATT_BK, ATT_BK, ATT_BQ), F32),
                        pltpu.VMEM((LANES, seq), BF16),
                        pltpu.VMEM((SSD_CHUNK + 2 * SUBLANES, SSD_CONV_CH), F32),
                        pltpu.VMEM((SSD_HEADS // 2, SSD_STATE, LANES), F32)],
        compiler_params=_cparams(("arbitrary", "arbitrary", "arbitrary")),
        name="attn_ssd",
    )(q, k, v, sw, zxd, consts)


def _mixer(x2, g_mix, w_in, conv_w, conv_b, dt_bias, a_log, d_skip, ssd_norm_w,
           lam_q1, lam_k1, lam_q2, lam_k2, subln_w, bsz, seq):
    zxd, q, k, v = _in_proj(x2, g_mix.reshape(1, -1), w_in.T)
    y_att, y_ssd = _attn_ssd(q, k, v, lam_q1, lam_k1, lam_q2, lam_k2, subln_w, zxd, conv_w, conv_b, dt_bias,
                             a_log, d_skip, ssd_norm_w, bsz, seq)
    return y_ssd, y_att


def _out_proj_kernel(x_ref, ys_ref, ya_ref, wo_ref, g_ref, wr2_ref, br_ref,
                     x1_ref, xn_ref, route_ref, gate_ref, cnt_ref):
    tm = TM_ROUTE
    x1 = x_ref[...] + _dot(jnp.concatenate([ys_ref[...], ya_ref[...]], axis=-1), wo_ref[...])
    x1_ref[...] = x1
    xn_all = _rms(x1, g_ref[...], NORM_EPS)
    xn_b = xn_all.astype(BF16)
    xn_ref[...] = xn_b

    n_tok = OUT_SUB * tm
    half = n_tok // 2
    lg2 = jnp.concatenate([_dot(xn_b[:half], wr2_ref[...]), _dot(xn_b[half:], wr2_ref[...])], axis=0)
    logits = (lg2[:, :LANES] + lg2[:, LANES:]).T[:N_EXPERTS, :] + br_ref[...]

    eidx = lax.broadcasted_iota(jnp.int32, (N_EXPERTS, n_tok), 0).astype(F32)
    work = logits
    vals, idxs, hots = [], [], []
    for _ in range(TOP_K):
        m = jnp.max(work, axis=0, keepdims=True)
        idx = jnp.min(jnp.where(work == m, eidx, float(N_EXPERTS)), axis=0, keepdims=True)
        hot = eidx == idx
        vals.append(m)
        idxs.append(idx.astype(jnp.int32))
        hots.append(hot)
        work = jnp.where(hot, -jnp.inf, work)
    exps = [jnp.exp(v - vals[0]) for v in vals]
    denom = exps[0] + exps[1] + exps[2] + exps[3]
    gates = [e / denom for e in exps]

    cnt = jnp.zeros((N_EXPERTS, n_tok), F32)
    for hot in hots:
        cnt = cnt + jnp.where(hot, 1.0, 0.0)
    cnt_b = cnt.astype(BF16)
    r = lax.broadcasted_iota(jnp.int32, (tm, tm), 0)
    c = lax.broadcasted_iota(jnp.int32, (tm, tm), 1)
    earlier_tok = jnp.where(r < c, 1.0, 0.0).astype(BF16)
    er = lax.broadcasted_iota(jnp.int32, (N_EXPERTS, N_EXPERTS), 0)
    ec = lax.broadcasted_iota(jnp.int32, (N_EXPERTS, N_EXPERTS), 1)
    lower_exp = jnp.where(ec < er, 1.0, 0.0).astype(BF16)
    cnt_pad = jnp.concatenate([cnt_b, jnp.zeros((LANES - N_EXPERTS, n_tok), BF16)], axis=0)
    pos = []
    for sub in range(OUT_SUB):
        cols = slice(sub * tm, (sub + 1) * tm)
        run = jnp.sum(cnt[:, cols], axis=1, keepdims=True)
        run_even = 2.0 * jnp.floor(0.5 * run + 0.5)
        run_start = _dot(lower_exp, jnp.broadcast_to(run_even, (N_EXPERTS, LANES)).astype(BF16))[:, 0:1]
        pos.append(_dot(cnt_b[:, cols], earlier_tok) + run_start)
        cnt_ref[sub] = _dot_nt(jnp.ones((SUBLANES, tm), BF16), cnt_pad[:, cols])
    pos = jnp.concatenate(pos, axis=1)
    lps = [jnp.sum(jnp.where(hot, pos, 0.0), axis=0, keepdims=True).astype(jnp.int32) for hot in hots]

    route_ref[...] = jnp.concatenate(idxs + lps, axis=0)
    gate_ref[...] = jnp.concatenate(gates + [jnp.zeros((SUBLANES - TOP_K, n_tok), F32)], axis=0)


def _out_proj(x2, y_ssd, y_att, w_out, g_ffn, w_router, b_router):
    t = x2.shape[0]
    tm = OUT_SUB * TM_ROUTE
    nt = t // TM_ROUTE
    wo = w_out.astype(BF16)
    wrh, wrm, _ = _split3(jnp.pad(w_router, ((0, 0), (0, LANES - N_EXPERTS))))
    wr2 = jnp.concatenate([wrh, wrm], axis=1)
    br = b_router.reshape(N_EXPERTS, 1)
    row = lambda n: pl.BlockSpec((tm, n), lambda i: (i, 0))
    col = pl.BlockSpec((SUBLANES, tm), lambda i: (0, i))
    full = lambda a: pl.BlockSpec(a.shape, lambda i: (0, 0))
    args = (x2, y_ssd, y_att, wo, g_ffn.reshape(1, -1), wr2, br)
    return pl.pallas_call(
        _out_proj_kernel,
        grid=(t // tm,),
        in_specs=[row(D_MODEL), row(SSD_WIDTH), row(ATT_WIDTH)] + [full(a) for a in args[3:]],
        out_specs=[row(D_MODEL), row(D_MODEL), col, col,
                   pl.BlockSpec((OUT_SUB, SUBLANES, LANES), lambda i: (i, 0, 0))],
        out_shape=[
            jax.ShapeDtypeStruct((t, D_MODEL), F32),
            jax.ShapeDtypeStruct((t, D_MODEL), BF16),
            jax.ShapeDtypeStruct((SUBLANES, t), jnp.int32),
            jax.ShapeDtypeStruct((SUBLANES, t), F32),
            jax.ShapeDtypeStruct((nt, SUBLANES, LANES), F32),
        ],
        compiler_params=_cparams(("arbitrary",)),
        name="out_proj",
    )(*args)


def _copy_run(n, start_copy):
    @pl.when(n > 0)
    def _():
        start_copy(n)


def _select_by_position(positions, values, n_pos):
    tm = positions[0].shape[1]
    r = lax.broadcasted_iota(jnp.int32, (n_pos, tm), 0)
    out = jnp.zeros((n_pos, tm), F32)
    for k in reversed(range(TOP_K)):
        out = jnp.where(r == positions[k], values[k], out)
    return out


def _dispatch_kernel(cnt_ref, ls_ref, base_ref, zoff_ref, rows_ref, xn_ref, route_ref, xs_hbm, xloc, zeros_vmem,
                     sem_z, sems):
    n_pos = N_POS
    i = pl.program_id(0)
    n_steps = pl.num_programs(0)
    slot = lax.rem(i, 2)

    def zero_fill(op):
        def pad_rows(e, _):
            n = zoff_ref[N_EXPERTS + 1 + e]

            @pl.when(n > 0)
            def _():
                op(pltpu.make_async_copy(zeros_vmem.at[_row_slice(0, n, PACK_SUB)],
                                         xs_hbm.at[_row_slice(zoff_ref[e], n, PACK_SUB)], sem_z))
            return 0

        def unused_block(b, _):
            op(pltpu.make_async_copy(zeros_vmem, xs_hbm.at[_row_slice(b * ROW_BLK, ROW_BLK, PACK_SUB)], sem_z))
            return 0

        lax.fori_loop(0, N_EXPERTS, pad_rows, 0)
        lax.fori_loop(zoff_ref[N_EXPERTS], xs_hbm.shape[0] // (ROW_BLK * PACK_SUB), unused_block, 0)

    @pl.when(i == 0)
    def _():
        zeros_vmem[...] = jnp.zeros_like(zeros_vmem)
        zero_fill(lambda cp: cp.start())

    def slot_wait(s, step):
        n = rows_ref[step]
        pltpu.make_async_copy(xloc.at[s, _row_slice(0, n, PACK_SUB)], xs_hbm.at[_row_slice(0, n, PACK_SUB)],
                              sems.at[s]).wait()

    @pl.when(i >= 2)
    def _():
        slot_wait(slot, i - 2)

    positions = [route_ref[TOP_K + k:TOP_K + k + 1, :] for k in range(TOP_K)]
    sel = _select_by_position(positions, [1.0] * TOP_K, n_pos).astype(BF16)
    rows = _dot(sel, xn_ref[...])
    _pack_rows(xloc.at[slot], rows)

    def per_expert(e, _):
        idx = i * N_EXPERTS + e
        src0 = ls_ref[idx]
        dst0 = base_ref[idx]

        def start_copy(n):
            pltpu.make_async_copy(xloc.at[slot, _row_slice(src0, n, PACK_SUB)],
                                  xs_hbm.at[_row_slice(dst0, n, PACK_SUB)], sems.at[slot]).start()

        _copy_run(cnt_ref[idx], start_copy)
        return 0

    lax.fori_loop(0, N_EXPERTS, per_expert, 0, unroll=RUN_UNROLL)

    @pl.when(i == n_steps - 1)
    def _():
        slot_wait(slot, i)

        @pl.when(n_steps > 1)
        def _():
            slot_wait(1 - slot, i - 1)

        zero_fill(lambda cp: cp.wait())


def _dispatch(xn, route, tables, n_rows):
    t = xn.shape[0]
    tm = TM_ROUTE
    cnt_tbl, ls_tbl, base_tbl, zoff, tile_rows = tables
    return pl.pallas_call(
        _dispatch_kernel,
        grid_spec=pltpu.PrefetchScalarGridSpec(
            num_scalar_prefetch=5,
            grid=(t // tm,),
            in_specs=[pl.BlockSpec((tm, D_MODEL), lambda i, *_: (i, 0)),
                      pl.BlockSpec((SUBLANES, tm), lambda i, *_: (0, i))],
            out_specs=pl.BlockSpec(memory_space=pl.ANY),
            scratch_shapes=[
                pltpu.VMEM((2, N_POS * PACK_SUB, LANES), jnp.uint32),
                pltpu.VMEM((ROW_BLK * PACK_SUB, LANES), jnp.uint32),
                pltpu.SemaphoreType.DMA,
                pltpu.SemaphoreType.DMA((2,)),
            ],
        ),
        out_shape=jax.ShapeDtypeStruct(((n_rows + ROW_BLK) * PACK_SUB, LANES), jnp.uint32),
        compiler_params=_cparams(("arbitrary",)),
        name="dispatch",
    )(cnt_tbl, ls_tbl, base_tbl, zoff, tile_rows, xn, route)


def _experts_kernel(blk_e_ref, nvalid_ref, first_ref, wslot_ref, enext_ref, pieces_ref, xs_ref, wup_hbm, wdn_hbm,
                    bgu_ref, bd_ref, ys_ref, wup_buf, wdn_buf, wg_s, wu_s, wd_s, sems):
    i = pl.program_id(0)
    slot = wslot_ref[i]

    def weight_copies(e, s):
        return (pltpu.make_async_copy(wup_hbm.at[e], wup_buf.at[s], sems.at[0, s]),
                pltpu.make_async_copy(wdn_hbm.at[e], wdn_buf.at[s], sems.at[1, s]))

    @pl.when(i == 0)
    def _():
        for cp in weight_copies(blk_e_ref[0], slot):
            cp.start()

    @pl.when(first_ref[i] != 0)
    def _():
        for cp in weight_copies(blk_e_ref[i], slot):
            cp.wait()

        @pl.when(enext_ref[i] >= 0)
        def _():
            for cp in weight_copies(enext_ref[i], 1 - slot):
                cp.start()

        src = lax.broadcasted_iota(jnp.int32, (DEINT, DEINT), 0)
        dst = lax.broadcasted_iota(jnp.int32, (DEINT, DEINT), 1)
        perm = jnp.where(src == jnp.where(dst < LANES, 2 * dst, 2 * (dst - LANES) + 1), 1.0, 0.0).astype(BF16)
        for g in range(2 * D_EXPERT // DEINT):
            sep = _dot(wup_buf[slot, :, g * DEINT:(g + 1) * DEINT].astype(BF16), perm)
            wg_s[:, g * LANES:(g + 1) * LANES] = sep[:, :LANES].astype(BF16)
            wu_s[:, g * LANES:(g + 1) * LANES] = sep[:, LANES:].astype(BF16)
        wd_s[...] = wdn_buf[slot].astype(BF16)

    def ffn(n_rows):
        used = n_rows * ROW_SUB
        if n_rows:
            xb = _unpack_rows(xs_ref.at[pl.ds(0, n_rows * PACK_SUB)])
            gate = jnp.minimum(_dot(xb, wg_s[...]) + bgu_ref[0, 0:1, :], SWIGLU_LIMIT)
            up = jnp.clip(_dot(xb, wu_s[...]) + bgu_ref[0, 1:2, :], -SWIGLU_LIMIT, SWIGLU_LIMIT)
            act = (up + 1.0) * gate * jax.nn.sigmoid(SWIGLU_ALPHA * gate)
            _store_rows(ys_ref.at[pl.ds(0, used)], _dot(act.astype(BF16), wd_s[...]) + bd_ref[0])
        if n_rows < ROW_BLK:
            ys_ref[used:, :] = jnp.zeros((ROW_BLK * ROW_SUB - used, LANES), F32)

    for pieces in range(ROW_BLK // ROW_PIECE + 1):
        pl.when(pieces_ref[i] == pieces)(lambda pieces=pieces: ffn(pieces * ROW_PIECE))


def _experts(xs, plan, w_up, b_up, w_down, b_down, n_rows):
    nb = n_rows // ROW_BLK
    b_gu = b_up.reshape(N_EXPERTS, D_EXPERT, 2).transpose(0, 2, 1)
    bd = b_down.reshape(N_EXPERTS, 1, D_MODEL)
    blk_e, nvalid, first, wslot, enext, pieces = plan
    src = lambda i, be, nv, *_: (jnp.minimum(i, nv[0] - 1), 0)
    bspec = lambda r, m: pl.BlockSpec((1, r, m), lambda i, be, *_: (be[i], 0, 0))
    anyspec = pl.BlockSpec(memory_space=pl.ANY)
    return pl.pallas_call(
        _experts_kernel,
        grid_spec=pltpu.PrefetchScalarGridSpec(
            num_scalar_prefetch=6,
            grid=(nb,),
            in_specs=[pl.BlockSpec((ROW_BLK * PACK_SUB, LANES), src), anyspec, anyspec,
                      bspec(2, D_EXPERT), bspec(1, D_MODEL)],
            out_specs=_rows_spec(ROW_BLK, lambda i, *_: (i, 0)),
            scratch_shapes=[
                pltpu.VMEM((2, D_MODEL, 2 * D_EXPERT), F32),
                pltpu.VMEM((2, D_EXPERT, D_MODEL), F32),
                pltpu.VMEM((D_MODEL, D_EXPERT), BF16),
                pltpu.VMEM((D_MODEL, D_EXPERT), BF16),
                pltpu.VMEM((D_EXPERT, D_MODEL), BF16),
                pltpu.SemaphoreType.DMA((2, 2)),
            ],
        ),
        out_shape=jax.ShapeDtypeStruct((n_rows * ROW_SUB, LANES), F32),
        compiler_params=_cparams(("arbitrary",)),
        name="experts",
    )(blk_e, nvalid, first, wslot, enext, pieces, xs, w_up, w_down, b_gu, bd)


def _combine_kernel(cnt_ref, ls_ref, base_ref, rows_ref, ys_hbm, route_ref, gate_ref, x1_ref, p_ref, gp_ref, wpg_ref,
                    wpp_ref, gf_ref, o_ref, yloc, sems):
    tm = TM_ROUTE
    n_pos = N_POS
    i = pl.program_id(0)
    n_steps = pl.num_programs(0)
    slot = lax.rem(i, 2)

    def gather_step(step, s):
        for u in range(COMB_SUB):
            def per_expert(e, _, u=u):
                idx = (step * COMB_SUB + u) * N_EXPERTS + e
                src0 = base_ref[idx]
                dst0 = ls_ref[idx]

                def start_copy(n):
                    pltpu.make_async_copy(ys_hbm.at[_row_slice(src0, n)],
                                          yloc.at[s, u, _row_slice(dst0, n)], sems.at[s]).start()

                _copy_run(cnt_ref[idx], start_copy)
                return 0

            lax.fori_loop(0, N_EXPERTS, per_expert, 0, unroll=RUN_UNROLL)

    @pl.when(i == 0)
    def _():
        yloc[...] = jnp.zeros_like(yloc)
        gather_step(0, 0)

    @pl.when(i + 1 < n_steps)
    def _():
        gather_step(i + 1, 1 - slot)

    for u in range(COMB_SUB):
        n = rows_ref[i * COMB_SUB + u]
        pltpu.make_async_copy(ys_hbm.at[_row_slice(0, n)], yloc.at[slot, u, _row_slice(0, n)], sems.at[slot]).wait()

    moe = []
    for u in range(COMB_SUB):
        cols = slice(u * tm, (u + 1) * tm)
        positions = [route_ref[TOP_K + k:TOP_K + k + 1, cols] for k in range(TOP_K)]
        gsel = _select_by_position(positions, [gate_ref[k:k + 1, cols] for k in range(TOP_K)], n_pos)
        g_pos = jnp.sum(gsel, axis=1, keepdims=True)
        sel = jnp.where(gsel != 0.0, 1.0, 0.0).astype(BF16)
        y_gated = (_load_rows(yloc.at[slot, u]) * g_pos).astype(BF16)
        moe.append(_dot_tn(sel, y_gated))
    x2 = x1_ref[...] + jnp.concatenate(moe, axis=0)
    xn = _rms(x2, gp_ref[...], NORM_EPS).astype(BF16)
    pp = _dot(p_ref[...].astype(BF16), wpp_ref[...])
    half = xn.shape[0] // 2
    ple_gate = jnp.concatenate([_dot(xn[:half], wpg_ref[...]), _dot(xn[half:], wpg_ref[...])], axis=0)
    x3 = x2 + pp * jax.nn.sigmoid(ple_gate)
    o_ref[...] = _rms(x3, gf_ref[...], NORM_EPS)


def _combine(ys, route, gate_t, x1, p2, tables, g_ple, w_ple_gate, w_ple_proj, g_final):
    t = x1.shape[0]
    tm = COMB_SUB * TM_ROUTE
    cnt_tbl, ls_tbl, base_tbl, _, tile_rows = tables
    row = lambda n: pl.BlockSpec((tm, n), lambda i, *_: (i, 0))
    col = pl.BlockSpec((SUBLANES, tm), lambda i, *_: (0, i))
    full = lambda a: pl.BlockSpec(a.shape, lambda i, *_: (0, 0))
    consts = (g_ple.reshape(1, -1), w_ple_gate.astype(BF16), w_ple_proj.astype(BF16), g_final.reshape(1, -1))
    return pl.pallas_call(
        _combine_kernel,
        grid_spec=pltpu.PrefetchScalarGridSpec(
            num_scalar_prefetch=4,
            grid=(t // tm,),
            in_specs=[pl.BlockSpec(memory_space=pl.ANY), col, col, row(D_MODEL), row(PLE_DIM)]
                     + [full(a) for a in consts],
            out_specs=row(D_MODEL),
            scratch_shapes=[
                pltpu.VMEM((2, COMB_SUB, N_POS * ROW_SUB, LANES), F32),
                pltpu.SemaphoreType.DMA((2,)),
            ],
        ),
        out_shape=jax.ShapeDtypeStruct((t, D_MODEL), F32),
        compiler_params=_cparams(("arbitrary",)),
        name="combine",
    )(cnt_tbl, ls_tbl, base_tbl, tile_rows, ys, route, gate_t, x1, p2, *consts)


def _routing_tables(cnt, n_blocks):
    tile_cnt = cnt[:, 0, :N_EXPERTS].astype(jnp.int32)
    tile_cnt = tile_cnt + tile_cnt % 2
    counts = jnp.sum(tile_cnt, axis=0)
    padded = (counts + ROW_BLK - 1) // ROW_BLK * ROW_BLK
    pend = jnp.cumsum(padded)
    pstart = pend - padded
    base = pstart[None, :] + jnp.cumsum(tile_cnt, axis=0) - tile_cnt
    lstart = jnp.cumsum(tile_cnt, axis=1) - tile_cnt
    nvalid = pend[-1:] // ROW_BLK
    zoff = jnp.concatenate([pstart + counts, nvalid, padded - counts])
    blk_start = jnp.minimum(jnp.arange(n_blocks, dtype=jnp.int32) * ROW_BLK, pend[-1] - 1)
    blk_e = jnp.minimum(jnp.sum((pend[None, :] <= blk_start[:, None]).astype(jnp.int32), axis=1), N_EXPERTS - 1)
    first = jnp.concatenate([jnp.ones((1,), bool), blk_e[1:] != blk_e[:-1]])
    wslot = (jnp.cumsum(first.astype(jnp.int32)) - 1) % 2
    eids = jnp.arange(N_EXPERTS, dtype=jnp.int32)
    later_nonempty = (eids[None, :] > eids[:, None]) & (padded[None, :] > 0)
    next_e = jnp.min(jnp.where(later_nonempty, eids[None, :], N_EXPERTS), axis=1)
    next_e = jnp.where(next_e == N_EXPERTS, -1, next_e)
    blk_hot = blk_e[:, None] == eids[None, :]
    per_block = lambda v: jnp.sum(jnp.where(blk_hot, v[None, :], 0), axis=1)
    blk_ids = jnp.arange(n_blocks, dtype=jnp.int32)
    real_rows = jnp.clip(per_block(pstart + counts) - blk_ids * ROW_BLK, 0, ROW_BLK) * (blk_ids < nvalid[0])
    pieces = (real_rows + ROW_PIECE - 1) // ROW_PIECE
    tile_rows = jnp.sum(tile_cnt, axis=1)
    i32 = lambda a: a.reshape(-1).astype(jnp.int32)
    plan = (i32(blk_e), i32(nvalid), i32(first), i32(wslot), i32(per_block(next_e)), i32(pieces))
    return (i32(tile_cnt), i32(lstart), i32(base), i32(zoff), i32(tile_rows)), plan


def kernel(x, p, g_mix, w_in, conv_w, conv_b, dt_bias, a_log, d_skip, ssd_norm_w, lam_q1, lam_k1, lam_q2, lam_k2, subln_w, w_out, g_ffn, w_router, b_router, w_up, b_up, w_down, b_down, g_ple, w_ple_gate, w_ple_proj, g_final):
    bsz, seq, d = x.shape
    t = bsz * seq
    x2 = x.reshape(t, d)
    y_ssd, y_att = _mixer(x2, g_mix[0], w_in[0], conv_w[0], conv_b[0], dt_bias[0], a_log[0], d_skip[0], ssd_norm_w[0],
                          lam_q1[0], lam_k1[0], lam_q2[0], lam_k2[0], subln_w[0], bsz, seq)
    x1, xn, route, gate_t, cnt = _out_proj(x2, y_ssd, y_att, w_out[0], g_ffn[0], w_router[0], b_router[0])

    n_rows = t * TOP_K + (t // TM_ROUTE) * N_EXPERTS + N_EXPERTS * ROW_BLK
    n_rows = -(-n_rows // ROW_BLK) * ROW_BLK
    tables, plan = _routing_tables(cnt, n_rows // ROW_BLK)
    xs = _dispatch(xn, route, tables, n_rows)
    ys = _experts(xs, plan, w_up[0], b_up[0], w_down[0], b_down[0], n_rows)
    out = _combine(ys, route, gate_t, x1, p[0].reshape(t, PLE_DIM), tables, g_ple[0], w_ple_gate[0], w_ple_proj[0],
                   g_final)
    return out.reshape(bsz, seq, d)
```
